```python
import jax, jax.numpy as jnp
from jax import lax
import numpy as np

D_MODEL = 2048
BATCH = 8
SEQ = 2048
DEPTH = 1

D_MIX = D_MODEL
D_POOL = D_MIX // 2
POOL_WINDOWS = (2, 4, 8, 16)
N_POOL_GROUPS = len(POOL_WINDOWS)
POOL_GROUP_DIM = D_POOL // N_POOL_GROUPS
D_GLA = D_MIX - D_POOL
GLA_HEADS = 4
GLA_DV = D_GLA // GLA_HEADS
GLA_DK_TOTAL = D_GLA // 2
GLA_DK = GLA_DK_TOTAL // GLA_HEADS
GLA_GATE_RANK = 16
GATE_LOGIT_NORMALIZER = 16.0
CHUNK = 64
D_IN = D_POOL + 2 * GLA_DK_TOTAL + 2 * D_GLA + GLA_GATE_RANK
D_FF = 5632
EPS = 1e-6

kernel_name = "hymba_pool_gla_macaron_block"


def rmsnorm(x, g):
    xf = x.astype(jnp.float32)
    y = xf * lax.rsqrt(jnp.mean(xf * xf, axis=-1, keepdims=True) + EPS)
    return (y * g.astype(jnp.float32)).astype(x.dtype)


def swiglu(h, w_in, w_out):
    gu = h @ w_in
    gate, up = gu[..., :D_FF], gu[..., D_FF:]
    return (jax.nn.silu(gate) * up) @ w_out


def pool_mixer(u, w_pool, pool_scale):
    B, S, _ = u.shape
    uf = u.astype(jnp.float32).reshape(B, S, N_POOL_GROUPS, POOL_GROUP_DIM)
    cs = jnp.cumsum(uf, axis=1)
    pos1 = jnp.arange(1, S + 1, dtype=jnp.int32)
    means = []
    for gi, w in enumerate(POOL_WINDOWS):
        c = cs[:, :, gi]
        shifted = jnp.pad(c, ((0, 0), (w, 0), (0, 0)))[:, :S]
        cnt = jnp.minimum(pos1, w).astype(jnp.float32)[None, :, None]
        means.append((c - shifted) / cnt)
    pooled = jnp.stack(means, axis=2) - uf
    y = jnp.einsum('bsgc,gcd->bsgd', pooled.astype(u.dtype), w_pool)
    return y.reshape(B, S, D_POOL) * pool_scale


def gla_mixer(q, k, v, g_out, gate_lr, w_alpha, b_alpha, gla_norm):
    B, S, _ = q.shape
    N = S // CHUNK
    log_alpha = jax.nn.log_sigmoid((gate_lr @ w_alpha + b_alpha).astype(jnp.float32)) / GATE_LOGIT_NORMALIZER

    def heads(t, d):
        return t.astype(jnp.float32).reshape(B, N, CHUNK, GLA_HEADS, d).transpose(0, 3, 1, 2, 4)

    qh = heads(q, GLA_DK) * (GLA_DK ** -0.5)
    kh = heads(k, GLA_DK)
    vh = heads(v, GLA_DV)
    bcum = jnp.cumsum(heads(log_alpha, GLA_DK), axis=3)
    b_last = bcum[:, :, :, -1:]
    q_dec = qh * jnp.exp(bcum)
    k_inv = kh * jnp.exp(-bcum)
    k_tail = kh * jnp.exp(b_last - bcum)

    mask = jnp.tril(jnp.ones((CHUNK, CHUNK), dtype=bool))
    scores = jnp.where(mask, jnp.einsum('bhnid,bhnjd->bhnij', q_dec, k_inv), 0.0)
    o_intra = jnp.einsum('bhnij,bhnjv->bhniv', scores, vh)

    kv_chunk = jnp.einsum('bhncd,bhncv->bhndv', k_tail, vh)
    decay_chunk = jnp.exp(b_last[:, :, :, 0])

    def step(state, inp):
        dec, kv = inp
        return state * dec[..., None] + kv, state

    init = jnp.zeros((B, GLA_HEADS, GLA_DK, GLA_DV), jnp.float32)
    _, states = lax.scan(step, init, (decay_chunk.transpose(2, 0, 1, 3), kv_chunk.transpose(2, 0, 1, 3, 4)))
    states = states.transpose(1, 2, 0, 3, 4)
    o = o_intra + jnp.einsum('bhncd,bhndv->bhncv', q_dec, states)

    o = o * lax.rsqrt(jnp.mean(o * o, axis=-1, keepdims=True) + EPS) * gla_norm.astype(jnp.float32)
    o = o.transpose(0, 2, 3, 1, 4).reshape(B, S, D_GLA)
    return (o * jax.nn.silu(g_out.astype(jnp.float32))).astype(q.dtype)


def _fwd_setup_inputs(seed: int = 0) -> dict:
    key = jax.random.key(seed)
    ks = jax.random.split(key, 16)
    f32 = jnp.float32

    def nrm(k, shape, fan_in):
        return jax.random.normal(k, shape, f32) * (fan_in ** -0.5)

    def gain(k, shape):
        return 1.0 + 0.02 * jax.random.normal(k, shape, f32)

    L = DEPTH
    return {
        "x": jax.random.normal(ks[0], (BATCH, SEQ, D_MODEL), f32),
        "ffn1_norm": gain(ks[1], (L, D_MODEL)),
        "ffn1_w_in": nrm(ks[2], (L, D_MODEL, 2 * D_FF), D_MODEL),
        "ffn1_w_out": nrm(ks[3], (L, D_FF, D_MODEL), D_FF),
        "mix_norm": gain(ks[4], (L, D_MODEL)),
        "w_in_mix": nrm(ks[5], (L, D_MODEL, D_IN), D_MODEL),
        "w_pool": nrm(ks[6], (L, N_POOL_GROUPS, POOL_GROUP_DIM, POOL_GROUP_DIM), POOL_GROUP_DIM),
        "pool_scale": 1.0 + 0.1 * jax.random.normal(ks[7], (L, D_POOL), f32),
        "w_alpha": nrm(ks[8], (L, GLA_GATE_RANK, GLA_DK_TOTAL), GLA_GATE_RANK),
        "b_alpha": 0.01 * jax.random.normal(ks[9], (L, GLA_DK_TOTAL), f32),
        "gla_norm": gain(ks[10], (L, GLA_DV)),
        "w_out_mix": nrm(ks[11], (L, D_MIX, D_MODEL), D_MIX),
        "ffn2_norm": gain(ks[12], (L, D_MODEL)),
        "ffn2_w_in": nrm(ks[13], (L, D_MODEL, 2 * D_FF), D_MODEL),
        "ffn2_w_out": nrm(ks[14], (L, D_FF, D_MODEL), D_FF),
        "final_norm": gain(ks[15], (D_MODEL,)),
    }


def _fwd_reference(x, ffn1_norm, ffn1_w_in, ffn1_w_out, mix_norm, w_in_mix, w_pool, pool_scale,
              w_alpha, b_alpha, gla_norm, w_out_mix, ffn2_norm, ffn2_w_in, ffn2_w_out, final_norm):
    h = x
    o_q = D_POOL
    o_k = o_q + GLA_DK_TOTAL
    o_v = o_k + GLA_DK_TOTAL
    o_g = o_v + D_GLA
    o_r = o_g + D_GLA
    for l in range(DEPTH):
        h = h + 0.5 * swiglu(rmsnorm(h, ffn1_norm[l]), ffn1_w_in[l], ffn1_w_out[l])
        u = rmsnorm(h, mix_norm[l]) @ w_in_mix[l]
        y_pool = pool_mixer(u[..., :o_q], w_pool[l], pool_scale[l])
        y_gla = gla_mixer(u[..., o_q:o_k], u[..., o_k:o_v], u[..., o_v:o_g], u[..., o_g:o_r],
                          u[..., o_r:], w_alpha[l], b_alpha[l], gla_norm[l])
        h = h + jnp.concatenate([y_pool.astype(h.dtype), y_gla.astype(h.dtype)], axis=-1) @ w_out_mix[l]
        h = h + 0.5 * swiglu(rmsnorm(h, ffn2_norm[l]), ffn2_w_in[l], ffn2_w_out[l])
    return rmsnorm(h, final_norm)


import jax as _jax
import jax.numpy as _jnp

TWIN_FORMAT = 'train_step'
FWD_PARAMS = ['x', 'ffn1_norm', 'ffn1_w_in', 'ffn1_w_out', 'mix_norm', 'w_in_mix', 'w_pool', 'pool_scale', 'w_alpha', 'b_alpha', 'gla_norm', 'w_out_mix', 'ffn2_norm', 'ffn2_w_in', 'ffn2_w_out', 'final_norm']
TWIN_WEIGHTS = ['ffn1_norm', 'ffn1_w_in', 'ffn1_w_out', 'mix_norm', 'w_in_mix', 'w_pool', 'pool_scale', 'w_alpha', 'b_alpha', 'gla_norm', 'w_out_mix', 'ffn2_norm', 'ffn2_w_in', 'ffn2_w_out', 'final_norm']
TWIN_DIFF_INPUT = 'x'
TWIN_INPUTS = ['x', 'ffn1_norm', 'ffn1_w_in', 'ffn1_w_out', 'mix_norm', 'w_in_mix', 'w_pool', 'pool_scale', 'w_alpha', 'b_alpha', 'gla_norm', 'w_out_mix', 'ffn2_norm', 'ffn2_w_in', 'ffn2_w_out', 'final_norm', 'loss_target', 'm_ffn1_norm', 'm_ffn1_w_in', 'm_ffn1_w_out', 'm_mix_norm', 'm_w_in_mix', 'm_w_pool', 'm_pool_scale', 'm_w_alpha', 'm_b_alpha', 'm_gla_norm', 'm_w_out_mix', 'm_ffn2_norm', 'm_ffn2_w_in', 'm_ffn2_w_out', 'm_final_norm', 'v_ffn1_norm', 'v_ffn1_w_in', 'v_ffn1_w_out', 'v_mix_norm', 'v_w_in_mix', 'v_w_pool', 'v_pool_scale', 'v_w_alpha', 'v_b_alpha', 'v_gla_norm', 'v_w_out_mix', 'v_ffn2_norm', 'v_ffn2_w_in', 'v_ffn2_w_out', 'v_final_norm']
TWIN_OUTPUTS = ['loss', 'grad_x', 'grad_ffn1_norm', 'grad_ffn1_w_in', 'grad_ffn1_w_out', 'grad_mix_norm', 'grad_w_in_mix', 'grad_w_pool', 'grad_pool_scale', 'grad_w_alpha', 'grad_b_alpha', 'grad_gla_norm', 'grad_w_out_mix', 'grad_ffn2_norm', 'grad_ffn2_w_in', 'grad_ffn2_w_out', 'grad_final_norm', 'delta_ffn1_norm', 'delta_ffn1_w_in', 'delta_ffn1_w_out', 'delta_mix_norm', 'delta_w_in_mix', 'delta_w_pool', 'delta_pool_scale', 'delta_w_alpha', 'delta_b_alpha', 'delta_gla_norm', 'delta_w_out_mix', 'delta_ffn2_norm', 'delta_ffn2_w_in', 'delta_ffn2_w_out', 'delta_final_norm', 'new_m_ffn1_norm', 'new_m_ffn1_w_in', 'new_m_ffn1_w_out', 'new_m_mix_norm', 'new_m_w_in_mix', 'new_m_w_pool', 'new_m_pool_scale', 'new_m_w_alpha', 'new_m_b_alpha', 'new_m_gla_norm', 'new_m_w_out_mix', 'new_m_ffn2_norm', 'new_m_ffn2_w_in', 'new_m_ffn2_w_out', 'new_m_final_norm', 'new_v_ffn1_norm', 'new_v_ffn1_w_in', 'new_v_ffn1_w_out', 'new_v_mix_norm', 'new_v_w_in_mix', 'new_v_w_pool', 'new_v_pool_scale', 'new_v_w_alpha', 'new_v_b_alpha', 'new_v_gla_norm', 'new_v_w_out_mix', 'new_v_ffn2_norm', 'new_v_ffn2_w_in', 'new_v_ffn2_w_out', 'new_v_final_norm']
TWIN_LEAF_KINDS = {'loss': 'loss', 'grad_x': 'grad_x', 'grad_ffn1_norm': 'grad_w', 'grad_ffn1_w_in': 'grad_w', 'grad_ffn1_w_out': 'grad_w', 'grad_mix_norm': 'grad_w', 'grad_w_in_mix': 'grad_w', 'grad_w_pool': 'grad_w', 'grad_pool_scale': 'grad_w', 'grad_w_alpha': 'grad_w', 'grad_b_alpha': 'grad_w', 'grad_gla_norm': 'grad_w', 'grad_w_out_mix': 'grad_w', 'grad_ffn2_norm': 'grad_w', 'grad_ffn2_w_in': 'grad_w', 'grad_ffn2_w_out': 'grad_w', 'grad_final_norm': 'grad_w', 'delta_ffn1_norm': 'delta_w', 'delta_ffn1_w_in': 'delta_w', 'delta_ffn1_w_out': 'delta_w', 'delta_mix_norm': 'delta_w', 'delta_w_in_mix': 'delta_w', 'delta_w_pool': 'delta_w', 'delta_pool_scale': 'delta_w', 'delta_w_alpha': 'delta_w', 'delta_b_alpha': 'delta_w', 'delta_gla_norm': 'delta_w', 'delta_w_out_mix': 'delta_w', 'delta_ffn2_norm': 'delta_w', 'delta_ffn2_w_in': 'delta_w', 'delta_ffn2_w_out': 'delta_w', 'delta_final_norm': 'delta_w', 'new_m_ffn1_norm': 'new_m', 'new_m_ffn1_w_in': 'new_m', 'new_m_ffn1_w_out': 'new_m', 'new_m_mix_norm': 'new_m', 'new_m_w_in_mix': 'new_m', 'new_m_w_pool': 'new_m', 'new_m_pool_scale': 'new_m', 'new_m_w_alpha': 'new_m', 'new_m_b_alpha': 'new_m', 'new_m_gla_norm': 'new_m', 'new_m_w_out_mix': 'new_m', 'new_m_ffn2_norm': 'new_m', 'new_m_ffn2_w_in': 'new_m', 'new_m_ffn2_w_out': 'new_m', 'new_m_final_norm': 'new_m', 'new_v_ffn1_norm': 'new_v', 'new_v_ffn1_w_in': 'new_v', 'new_v_ffn1_w_out': 'new_v', 'new_v_mix_norm': 'new_v', 'new_v_w_in_mix': 'new_v', 'new_v_w_pool': 'new_v', 'new_v_pool_scale': 'new_v', 'new_v_w_alpha': 'new_v', 'new_v_b_alpha': 'new_v', 'new_v_gla_norm': 'new_v', 'new_v_w_out_mix': 'new_v', 'new_v_ffn2_norm': 'new_v', 'new_v_ffn2_w_in': 'new_v', 'new_v_ffn2_w_out': 'new_v', 'new_v_final_norm': 'new_v'}


def _forward(args):
    return _fwd_reference(*[args[k] for k in FWD_PARAMS])


def _output_shape():
    out = _jax.eval_shape(lambda: _forward(_fwd_setup_inputs(0)))
    return out.shape, out.dtype

N_MICROBATCH = 1
ADAM_LR = 0.001
ADAM_B1 = 0.9
ADAM_B2 = 0.999
ADAM_EPS = 1e-08
ADAM_WD = 0.01
ADAM_STEP = 10
PER_EXAMPLE_BATCH_AXIS = {'x': 0, 'loss_target': 0}
SHARED_INPUTS = []
_WEIGHT_DTYPES = {'ffn1_norm': _jnp.float32, 'ffn1_w_in': _jnp.float32, 'ffn1_w_out': _jnp.float32, 'mix_norm': _jnp.float32, 'w_in_mix': _jnp.float32, 'w_pool': _jnp.float32, 'pool_scale': _jnp.float32, 'w_alpha': _jnp.float32, 'b_alpha': _jnp.float32, 'gla_norm': _jnp.float32, 'w_out_mix': _jnp.float32, 'ffn2_norm': _jnp.float32, 'ffn2_w_in': _jnp.float32, 'ffn2_w_out': _jnp.float32, 'final_norm': _jnp.float32}
MOMENT_SCALE = {'ffn1_norm': 3.163016e-02, 'ffn1_w_in': 1.304951e-02, 'ffn1_w_out': 2.129510e-02, 'mix_norm': 5.545368e-02, 'w_in_mix': 3.755730e-02, 'w_pool': 4.407726e-02, 'pool_scale': 4.567408e-02, 'w_alpha': 4.744899e-03, 'b_alpha': 1.886061e-02, 'gla_norm': 6.540423e-02, 'w_out_mix': 3.767983e-02, 'ffn2_norm': 2.096802e-02, 'ffn2_w_in': 8.700742e-03, 'ffn2_w_out': 1.419933e-02, 'final_norm': 7.993658e+00}


def _to_microbatches(a, axis):
    t = _jnp.moveaxis(a, axis, 0)
    t = t.reshape((N_MICROBATCH, t.shape[0] // N_MICROBATCH) + t.shape[1:])
    return _jnp.moveaxis(t, 1, axis + 1)


def setup_inputs(seed: int = 0) -> dict:
    inp = _fwd_setup_inputs(seed)
    key = _jax.random.fold_in(_jax.random.key(seed), 7919)
    shape, _ = _output_shape()
    out = dict(inp)
    out["loss_target"] = _jax.random.normal(_jax.random.fold_in(key, 0), shape, _jnp.float32)
    for i, name in enumerate(TWIN_WEIGHTS):
        w = inp[name].astype(_jnp.float32)
        if MOMENT_SCALE is None:
            s = _jnp.sqrt(_jnp.mean(_jnp.square(w)) + 1e-30)
        else:
            s = MOMENT_SCALE[name]
        km, kv = _jax.random.split(_jax.random.fold_in(key, i + 1))
        out[name] = w
        out["m_" + name] = s * _jax.random.normal(km, w.shape, _jnp.float32)
        out["v_" + name] = (s * s) * _jax.random.uniform(kv, w.shape, _jnp.float32, 0.5, 1.5)
    if N_MICROBATCH > 1:
        for name, axis in PER_EXAMPLE_BATCH_AXIS.items():
            out[name] = _to_microbatches(out[name], axis)
    return {'x': out['x'], 'ffn1_norm': out['ffn1_norm'], 'ffn1_w_in': out['ffn1_w_in'], 'ffn1_w_out': out['ffn1_w_out'], 'mix_norm': out['mix_norm'], 'w_in_mix': out['w_in_mix'], 'w_pool': out['w_pool'], 'pool_scale': out['pool_scale'], 'w_alpha': out['w_alpha'], 'b_alpha': out['b_alpha'], 'gla_norm': out['gla_norm'], 'w_out_mix': out['w_out_mix'], 'ffn2_norm': out['ffn2_norm'], 'ffn2_w_in': out['ffn2_w_in'], 'ffn2_w_out': out['ffn2_w_out'], 'final_norm': out['final_norm'], 'loss_target': out['loss_target'], 'm_ffn1_norm': out['m_ffn1_norm'], 'm_ffn1_w_in': out['m_ffn1_w_in'], 'm_ffn1_w_out': out['m_ffn1_w_out'], 'm_mix_norm': out['m_mix_norm'], 'm_w_in_mix': out['m_w_in_mix'], 'm_w_pool': out['m_w_pool'], 'm_pool_scale': out['m_pool_scale'], 'm_w_alpha': out['m_w_alpha'], 'm_b_alpha': out['m_b_alpha'], 'm_gla_norm': out['m_gla_norm'], 'm_w_out_mix': out['m_w_out_mix'], 'm_ffn2_norm': out['m_ffn2_norm'], 'm_ffn2_w_in': out['m_ffn2_w_in'], 'm_ffn2_w_out': out['m_ffn2_w_out'], 'm_final_norm': out['m_final_norm'], 'v_ffn1_norm': out['v_ffn1_norm'], 'v_ffn1_w_in': out['v_ffn1_w_in'], 'v_ffn1_w_out': out['v_ffn1_w_out'], 'v_mix_norm': out['v_mix_norm'], 'v_w_in_mix': out['v_w_in_mix'], 'v_w_pool': out['v_w_pool'], 'v_pool_scale': out['v_pool_scale'], 'v_w_alpha': out['v_w_alpha'], 'v_b_alpha': out['v_b_alpha'], 'v_gla_norm': out['v_gla_norm'], 'v_w_out_mix': out['v_w_out_mix'], 'v_ffn2_norm': out['v_ffn2_norm'], 'v_ffn2_w_in': out['v_ffn2_w_in'], 'v_ffn2_w_out': out['v_ffn2_w_out'], 'v_final_norm': out['v_final_norm']}


def _loss(weights, diff, rest, loss_target):
    with _jax.named_scope("forward"):
        args = {**rest, TWIN_DIFF_INPUT: diff, **{k: w.astype(_WEIGHT_DTYPES[k]) for k, w in weights.items()}}
        y = _forward(args)
    with _jax.named_scope("loss_head"):
        err = _jnp.square(y.astype(_jnp.float32) - loss_target)
        return 0.5 * _jnp.sum(_jnp.mean(err, axis=-1)) if err.ndim else 0.5 * err


def _adamw(w, g, m, v):
    m = ADAM_B1 * m + (1.0 - ADAM_B1) * g
    v = ADAM_B2 * v + (1.0 - ADAM_B2) * _jnp.square(g)
    m_hat = m / (1.0 - ADAM_B1 ** ADAM_STEP)
    v_hat = v / (1.0 - ADAM_B2 ** ADAM_STEP)
    delta = -ADAM_LR * (m_hat / (_jnp.sqrt(v_hat) + ADAM_EPS) + ADAM_WD * w)
    return delta, m, v


def reference(x, ffn1_norm, ffn1_w_in, ffn1_w_out, mix_norm, w_in_mix, w_pool, pool_scale, w_alpha, b_alpha, gla_norm, w_out_mix, ffn2_norm, ffn2_w_in, ffn2_w_out, final_norm, loss_target, m_ffn1_norm, m_ffn1_w_in, m_ffn1_w_out, m_mix_norm, m_w_in_mix, m_w_pool, m_pool_scale, m_w_alpha, m_b_alpha, m_gla_norm, m_w_out_mix, m_ffn2_norm, m_ffn2_w_in, m_ffn2_w_out, m_final_norm, v_ffn1_norm, v_ffn1_w_in, v_ffn1_w_out, v_mix_norm, v_w_in_mix, v_w_pool, v_pool_scale, v_w_alpha, v_b_alpha, v_gla_norm, v_w_out_mix, v_ffn2_norm, v_ffn2_w_in, v_ffn2_w_out, v_final_norm):
    given = dict(x=x, ffn1_norm=ffn1_norm, ffn1_w_in=ffn1_w_in, ffn1_w_out=ffn1_w_out, mix_norm=mix_norm, w_in_mix=w_in_mix, w_pool=w_pool, pool_scale=pool_scale, w_alpha=w_alpha, b_alpha=b_alpha, gla_norm=gla_norm, w_out_mix=w_out_mix, ffn2_norm=ffn2_norm, ffn2_w_in=ffn2_w_in, ffn2_w_out=ffn2_w_out, final_norm=final_norm, loss_target=loss_target, m_ffn1_norm=m_ffn1_norm, m_ffn1_w_in=m_ffn1_w_in, m_ffn1_w_out=m_ffn1_w_out, m_mix_norm=m_mix_norm, m_w_in_mix=m_w_in_mix, m_w_pool=m_w_pool, m_pool_scale=m_pool_scale, m_w_alpha=m_w_alpha, m_b_alpha=m_b_alpha, m_gla_norm=m_gla_norm, m_w_out_mix=m_w_out_mix, m_ffn2_norm=m_ffn2_norm, m_ffn2_w_in=m_ffn2_w_in, m_ffn2_w_out=m_ffn2_w_out, m_final_norm=m_final_norm, v_ffn1_norm=v_ffn1_norm, v_ffn1_w_in=v_ffn1_w_in, v_ffn1_w_out=v_ffn1_w_out, v_mix_norm=v_mix_norm, v_w_in_mix=v_w_in_mix, v_w_pool=v_w_pool, v_pool_scale=v_pool_scale, v_w_alpha=v_w_alpha, v_b_alpha=v_b_alpha, v_gla_norm=v_gla_norm, v_w_out_mix=v_w_out_mix, v_ffn2_norm=v_ffn2_norm, v_ffn2_w_in=v_ffn2_w_in, v_ffn2_w_out=v_ffn2_w_out, v_final_norm=v_final_norm)
    weights = {n: given[n] for n in TWIN_WEIGHTS}
    shared = {n: given[n] for n in SHARED_INPUTS}
    per_example = {n: given[n] for n in ['x']}
    grad_fn = _jax.value_and_grad(_loss, argnums=(0, 1))

    def one_microbatch(ex, loss_target):
        ex = dict(ex)
        diff = ex.pop(TWIN_DIFF_INPUT)
        return grad_fn(weights, diff, {**shared, **ex}, loss_target)

    if N_MICROBATCH == 1:
        loss, (grad_w, grad_x) = one_microbatch(per_example, given["loss_target"])
    else:
        def body(carry, xs):
            loss_sum, grad_sum = carry
            l_k, (gw_k, gx_k) = one_microbatch(xs[0], xs[1])
            with _jax.named_scope("update"):
                return (loss_sum + l_k, _jax.tree.map(_jnp.add, grad_sum, gw_k)), gx_k

        init = (_jnp.zeros((), _jnp.float32), _jax.tree.map(_jnp.zeros_like, weights))
        (loss, grad_w), grad_x = _jax.lax.scan(body, init, (per_example, given["loss_target"]))
    with _jax.named_scope("update"):
        delta_w, new_m, new_v = {}, {}, {}
        for n in TWIN_WEIGHTS:
            delta_w[n], new_m[n], new_v[n] = _adamw(weights[n], grad_w[n], given["m_" + n], given["v_" + n])
    return (loss, grad_x, *[grad_w[n] for n in TWIN_WEIGHTS], *[delta_w[n] for n in TWIN_WEIGHTS],
            *[new_m[n] for n in TWIN_WEIGHTS], *[new_v[n] for n in TWIN_WEIGHTS])
```

```python
import functools

import jax
import jax.numpy as jnp
from jax import lax
from jax.experimental import pallas as pl
from jax.experimental.pallas import tpu as pltpu

F32 = jnp.float32
BF16 = jnp.bfloat16
MESH = pl.DeviceIdType.MESH

D_MODEL = 2048
D_FF = 5632
D_POOL = 1024
POOL_WINDOWS = (2, 4, 8, 16)
POOL_GROUP_DIM = 256
D_GLA = 1024
GLA_HEADS = 4
GLA_DV = 256
GLA_DK = 128
GLA_DK_TOTAL = 512
GLA_GATE_RANK = 16
GATE_LOGIT_NORMALIZER = 16.0
CHUNK = 64
D_IN = 4112
D_IN_PAD = 4224
EPS = 1e-6

ADAM_LR = 0.001
ADAM_B1 = 0.9
ADAM_B2 = 0.999
ADAM_EPS = 1e-08
ADAM_WD = 0.01
ADAM_STEP = 10

N_CHIPS = 4
N_DEV = 8
V7X_VMEM_BYTES = 64 * 1024 * 1024
LANES = 128
MXU_TILE = 256
COL_TILE = 1408


def _cparams(semantics, vmem_mb):
    assert vmem_mb * 1024 * 1024 < V7X_VMEM_BYTES
    return pltpu.CompilerParams(dimension_semantics=semantics, vmem_limit_bytes=vmem_mb * 1024 * 1024)


def _dot_nn(a, b):
    return jnp.dot(a, b, preferred_element_type=F32)


def _dot_nt(a, b):
    return lax.dot_general(a, b, (((1,), (1,)), ((), ())), preferred_element_type=F32)


def _dot_tn(a, b):
    return lax.dot_general(a, b, (((0,), (0,)), ((), ())), preferred_element_type=F32)


def _sigmoid(x):
    return 1.0 / (1.0 + jnp.exp(-x))


def _rms_fwd(x, g, *, name):
    S, D = x.shape
    tm = 256

    def body(x_ref, g_ref, o_ref):
        xv = x_ref[...]
        r = lax.rsqrt(jnp.mean(xv * xv, axis=-1, keepdims=True) + EPS)
        o_ref[...] = (xv * r * g_ref[...]).astype(BF16)

    return pl.pallas_call(
        body, name=name, grid=(S // tm,),
        in_specs=[pl.BlockSpec((tm, D), lambda i: (i, 0)), pl.BlockSpec((1, D), lambda i: (0, 0))],
        out_specs=pl.BlockSpec((tm, D), lambda i: (i, 0)),
        out_shape=jax.ShapeDtypeStruct((S, D), BF16),
        compiler_params=_cparams(("parallel",), 32),
    )(x, g)


def _ffn_up(n, w_in, *, name):
    S, D = n.shape
    ns, _, cs = w_in.shape
    half = ns // 2
    F = cs * half
    tm, tn = 256, COL_TILE
    nb = cs // tn

    def body(n_ref, wg_ref, wu_ref, gu_ref, a_ref):
        nv = n_ref[...]
        g = _dot_nn(nv, wg_ref[...])
        u = _dot_nn(nv, wu_ref[...])
        gu_ref[0] = g.astype(BF16)
        gu_ref[1] = u.astype(BF16)
        a_ref[...] = (g * _sigmoid(g) * u).astype(BF16)

    return pl.pallas_call(
        body, name=name, grid=(F // tn, S // tm),
        in_specs=[
            pl.BlockSpec((tm, D), lambda j, i: (i, 0)),
            pl.BlockSpec((None, D, tn), lambda j, i: (lax.div(j, nb), 0, lax.rem(j, nb))),
            pl.BlockSpec((None, D, tn), lambda j, i: (half + lax.div(j, nb), 0, lax.rem(j, nb))),
        ],
        out_specs=[
            pl.BlockSpec((2, tm, tn), lambda j, i: (0, i, j)),
            pl.BlockSpec((tm, tn), lambda j, i: (i, j)),
        ],
        out_shape=[jax.ShapeDtypeStruct((2, S, F), BF16), jax.ShapeDtypeStruct((S, F), BF16)],
        compiler_params=_cparams(("parallel", "parallel"), 48),
    )(n, w_in, w_in)


def _mm_nn(a, b, resid, scale, *, tm, tn, tk, name):
    S, K = a.shape
    N = b.shape[1]
    nk = K // tk

    def body(*refs):
        if resid is None:
            a_ref, b_ref, o_ref, acc_ref = refs
            r_ref = None
        else:
            a_ref, b_ref, r_ref, o_ref, acc_ref = refs
        k = pl.program_id(2)

        @pl.when(k == 0)
        def _():
            acc_ref[...] = jnp.zeros_like(acc_ref)

        acc_ref[...] += _dot_nn(a_ref[...], b_ref[...])

        @pl.when(k == nk - 1)
        def _():
            out = acc_ref[...] * scale
            if r_ref is not None:
                out = r_ref[...] + out
            o_ref[...] = out

    in_specs = [pl.BlockSpec((tm, tk), lambda i, j, k: (i, k)), pl.BlockSpec((tk, tn), lambda i, j, k: (k, j))]
    args = [a, b]
    if resid is not None:
        in_specs.append(pl.BlockSpec((tm, tn), lambda i, j, k: (i, j)))
        args.append(resid)
    return pl.pallas_call(
        body, name=name, grid=(S // tm, N // tn, nk),
        in_specs=in_specs,
        out_specs=pl.BlockSpec((tm, tn), lambda i, j, k: (i, j)),
        out_shape=jax.ShapeDtypeStruct((S, N), F32),
        scratch_shapes=[pltpu.VMEM((tm, tn), F32)],
        compiler_params=_cparams(("parallel", "parallel", "arbitrary"), 48),
    )(*args)


def _mm_nt(a, b, *, tm, tn, name):
    S, K = a.shape
    N = b.shape[0]

    def body(a_ref, b_ref, o_ref):
        o_ref[...] = _dot_nt(a_ref[...], b_ref[...])

    return pl.pallas_call(
        body, name=name, grid=(N // tn, S // tm),
        in_specs=[pl.BlockSpec((tm, K), lambda j, i: (i, 0)), pl.BlockSpec((tn, K), lambda j, i: (j, 0))],
        out_specs=pl.BlockSpec((tm, tn), lambda j, i: (i, j)),
        out_shape=jax.ShapeDtypeStruct((S, N), F32),
        compiler_params=_cparams(("parallel", "parallel"), 48),
    )(a, b)


def _mm_tn(a, b, *, grid, a_spec, b_spec, out_spec, out_shape, scale, name, a_is_transposed=False):
    dot = _dot_nn if a_is_transposed else _dot_tn

    def body(a_ref, b_ref, o_ref):
        o_ref[...] = (scale * dot(a_ref[...], b_ref[...])).astype(o_ref.dtype)

    return pl.pallas_call(
        body, name=name, grid=grid, in_specs=[a_spec, b_spec], out_specs=out_spec, out_shape=out_shape,
        compiler_params=_cparams(("parallel",) * len(grid), 56),
    )(a, b)


def _ffn_bwd_act(dhb, w_out, gu, *, name):
    S, D = dhb.shape
    F = w_out.shape[0]
    tm, tn = 256, COL_TILE

    def body(dh_ref, w_ref, gu_ref, dgu_ref):
        da = 0.5 * _dot_nt(dh_ref[...], w_ref[...])
        g = gu_ref[0].astype(F32)
        u = gu_ref[1].astype(F32)
        s = _sigmoid(g)
        dgu_ref[0] = (da * u * (s * (1.0 + g * (1.0 - s)))).astype(BF16)
        dgu_ref[1] = (da * (g * s)).astype(BF16)

    return pl.pallas_call(
        body, name=name, grid=(F // tn, S // tm),
        in_specs=[
            pl.BlockSpec((tm, D), lambda j, i: (i, 0)),
            pl.BlockSpec((tn, D), lambda j, i: (j, 0)),
            pl.BlockSpec((2, tm, tn), lambda j, i: (0, i, j)),
        ],
        out_specs=pl.BlockSpec((2, tm, tn), lambda j, i: (0, i, j)),
        out_shape=jax.ShapeDtypeStruct((2, S, F), BF16),
        compiler_params=_cparams(("parallel", "parallel"), 48),
    )(dhb, w_out, gu)


def _mm_nt_rmsbwd(dact, w, h_in, dh_out, g, *, tk, name):
    ng, S, fg = dact.shape
    ns, D, cs = w.shape
    assert ng * fg == ns * cs
    tm, rc = 512, 64
    kpg, kps = fg // tk, cs // tk
    nk = ng * kpg

    def body(a_ref, w_ref, h_ref, dho_ref, g_ref, dh_ref, dhb_ref, dg_ref, acc_ref):
        i = pl.program_id(0)
        k = pl.program_id(1)

        @pl.when(k == 0)
        def _():
            acc_ref[...] = jnp.zeros_like(acc_ref)

        acc_ref[...] += _dot_nt(a_ref[...], w_ref[...])

        @pl.when(jnp.logical_and(i == 0, k == 0))
        def _():
            dg_ref[...] = jnp.zeros_like(dg_ref)

        @pl.when(k == nk - 1)
        def _():
            gv = g_ref[...]

            def rows_step(c, dg):
                rows = pl.ds(pl.multiple_of(c * rc, rc), rc)
                dn = acc_ref[rows, :]
                xv = h_ref[rows, :]
                r = lax.rsqrt(jnp.mean(xv * xv, axis=-1, keepdims=True) + EPS)
                xh = xv * r
                dng = dn * gv
                dx = r * (dng - xh * jnp.mean(dng * xh, axis=-1, keepdims=True))
                out = dho_ref[rows, :] + dx
                dh_ref[rows, :] = out
                dhb_ref[rows, :] = out.astype(BF16)
                return dg + jnp.sum(dn * xh, axis=0, keepdims=True)

            dg_ref[...] += lax.fori_loop(0, tm // rc, rows_step, jnp.zeros((1, D), F32))

    return pl.pallas_call(
        body, name=name, grid=(S // tm, nk),
        in_specs=[
            pl.BlockSpec((None, tm, tk), lambda i, k: (lax.div(k, kpg), i, lax.rem(k, kpg))),
            pl.BlockSpec((None, D, tk), lambda i, k: (lax.div(k, kps), 0, lax.rem(k, kps))),
            pl.BlockSpec((tm, D), lambda i, k: (i, 0)),
            pl.BlockSpec((tm, D), lambda i, k: (i, 0)),
            pl.BlockSpec((1, D), lambda i, k: (0, 0)),
        ],
        out_specs=[
            pl.BlockSpec((tm, D), lambda i, k: (i, 0)),
            pl.BlockSpec((tm, D), lambda i, k: (i, 0)),
            pl.BlockSpec((1, D), lambda i, k: (0, 0)),
        ],
        out_shape=[jax.ShapeDtypeStruct((S, D), F32), jax.ShapeDtypeStruct((S, D), BF16),
                   jax.ShapeDtypeStruct((1, D), F32)],
        scratch_shapes=[pltpu.VMEM((tm, D), F32)],
        compiler_params=_cparams(("arbitrary", "arbitrary"), 56),
    )(dact, w, h_in, dh_out, g)


def _final_loss(h, g, target, *, name):
    S, D = h.shape
    tm = 256

    def body(h_ref, g_ref, t_ref, loss_ref, dh_ref, dhb_ref, dg_ref):
        i = pl.program_id(0)

        @pl.when(i == 0)
        def _():
            loss_ref[...] = jnp.zeros_like(loss_ref)
            dg_ref[...] = jnp.zeros_like(dg_ref)

        xv = h_ref[...]
        gv = g_ref[...]
        r = lax.rsqrt(jnp.mean(xv * xv, axis=-1, keepdims=True) + EPS)
        xh = xv * r
        e = xh * gv - t_ref[...]
        loss_ref[...] += 0.5 * jnp.sum(jnp.mean(e * e, axis=-1, keepdims=True))
        dy = e * (1.0 / D)
        dg_ref[...] += jnp.sum(dy * xh, axis=0, keepdims=True)
        dyg = dy * gv
        dx = r * (dyg - xh * jnp.mean(dyg * xh, axis=-1, keepdims=True))
        dh_ref[...] = dx
        dhb_ref[...] = dx.astype(BF16)

    return pl.pallas_call(
        body, name=name, grid=(S // tm,),
        in_specs=[pl.BlockSpec((tm, D), lambda i: (i, 0)), pl.BlockSpec((1, D), lambda i: (0, 0)),
                  pl.BlockSpec((tm, D), lambda i: (i, 0))],
        out_specs=[pl.BlockSpec((8, LANES), lambda i: (0, 0)), pl.BlockSpec((tm, D), lambda i: (i, 0)),
                   pl.BlockSpec((tm, D), lambda i: (i, 0)), pl.BlockSpec((1, D), lambda i: (0, 0))],
        out_shape=[jax.ShapeDtypeStruct((8, LANES), F32), jax.ShapeDtypeStruct((S, D), F32),
                   jax.ShapeDtypeStruct((S, D), BF16), jax.ShapeDtypeStruct((1, D), F32)],
        compiler_params=_cparams(("arbitrary",), 40),
    )(h, g, target)


POOL_HALO = 16
POOL_ROWS = 256


def _pool_window_mean_minus_token(ext, tok0, w):
    s = ext
    k = 1
    while k < w:
        s = s + pltpu.roll(s, k, 0)
        k *= 2
    win = s[POOL_HALO:, :]
    tok = tok0 + lax.broadcasted_iota(jnp.int32, (POOL_ROWS, 1), 0)
    cnt = jnp.minimum(tok + 1, w).astype(F32)
    return win / cnt - ext[POOL_HALO:, :], cnt


def _pool_fwd(u, w_pool, scale, *, name):
    S = u.shape[0]
    C = POOL_GROUP_DIM
    nsteps = S // POOL_ROWS

    def body(p_ref, w_ref, sc_ref, y_ref, xp_ref):
        xp_ref[0:POOL_HALO, :] = jnp.zeros((POOL_HALO, D_POOL), F32)
        xp_ref[POOL_HALO:, :] = p_ref[...]
        for gi, win in enumerate(POOL_WINDOWS):
            cols = slice(gi * C, (gi + 1) * C)

            def step(c, carry, cols=cols, win=win, gi=gi):
                r0 = pl.multiple_of(c * POOL_ROWS, POOL_ROWS)
                ext = xp_ref[pl.ds(r0, POOL_ROWS + POOL_HALO), cols]
                pooled, _ = _pool_window_mean_minus_token(ext, r0, win)
                y = _dot_nn(pooled.astype(BF16), w_ref[gi]) * sc_ref[:, cols]
                y_ref[pl.ds(r0, POOL_ROWS), cols] = y.astype(BF16)
                return carry

            lax.fori_loop(0, nsteps, step, 0)

    return pl.pallas_call(
        body, name=name, grid=(1,),
        in_specs=[pl.BlockSpec((S, D_POOL), lambda i: (0, 0)),
                  pl.BlockSpec((4, C, C), lambda i: (0, 0, 0)),
                  pl.BlockSpec((1, D_POOL), lambda i: (0, 0))],
        out_specs=pl.BlockSpec((S, D_POOL), lambda i: (0, 0)),
        out_shape=jax.ShapeDtypeStruct((S, D_POOL), BF16),
        scratch_shapes=[pltpu.VMEM((S + POOL_HALO, D_POOL), F32)],
        compiler_params=_cparams(("arbitrary",), 48),
    )(u, w_pool, scale)


def _pool_bwd(u, dcat, w_pool, scale, *, name):
    S = u.shape[0]
    C = POOL_GROUP_DIM
    nsteps = S // POOL_ROWS

    def body(p_ref, dy_ref, w_ref, sc_ref, dp_ref, dw_ref, dsc_ref, xp_ref, e_ref, neg_ref):
        xp_ref[0:POOL_HALO, :] = jnp.zeros((POOL_HALO, D_POOL), F32)
        xp_ref[POOL_HALO:, :] = p_ref[...]
        e_ref[S:, :] = jnp.zeros((POOL_HALO, C), F32)
        for gi, win in enumerate(POOL_WINDOWS):
            cols = slice(gi * C, (gi + 1) * C)

            def step_a(c, carry, cols=cols, win=win, gi=gi):
                dw, dsc = carry
                r0 = pl.multiple_of(c * POOL_ROWS, POOL_ROWS)
                ext = xp_ref[pl.ds(r0, POOL_ROWS + POOL_HALO), cols]
                pooled, cnt = _pool_window_mean_minus_token(ext, r0, win)
                pb = pooled.astype(BF16)
                wv = w_ref[gi]
                dy = dy_ref[pl.ds(r0, POOL_ROWS), cols]
                dsc = dsc + jnp.sum(dy * _dot_nn(pb, wv), axis=0, keepdims=True)
                dyp = (dy * sc_ref[:, cols]).astype(BF16)
                dw = dw + _dot_tn(pb, dyp)
                dpooled = _dot_nt(dyp, wv)
                e_ref[pl.ds(r0, POOL_ROWS), :] = dpooled / cnt
                neg_ref[pl.ds(r0, POOL_ROWS), :] = -dpooled
                return dw, dsc

            dw, dsc = lax.fori_loop(0, nsteps, step_a, (jnp.zeros((C, C), F32), jnp.zeros((1, C), F32)))
            dw_ref[gi] = dw
            dsc_ref[:, cols] = dsc

            def step_b(c, carry, cols=cols, win=win):
                r0 = pl.multiple_of(c * POOL_ROWS, POOL_ROWS)
                s = e_ref[pl.ds(r0, POOL_ROWS + POOL_HALO), :]
                n = POOL_ROWS + POOL_HALO
                k = 1
                while k < win:
                    s = s + pltpu.roll(s, n - k, 0)
                    k *= 2
                du = s[:POOL_ROWS, :] + neg_ref[pl.ds(r0, POOL_ROWS), :]
                dp_ref[pl.ds(r0, POOL_ROWS), cols] = du.astype(BF16)
                return carry

            lax.fori_loop(0, nsteps, step_b, 0)

    return pl.pallas_call(
        body, name=name, grid=(1,),
        in_specs=[pl.BlockSpec((S, D_POOL), lambda i: (0, 0)),
                  pl.BlockSpec((S, D_POOL), lambda i: (0, 0)),
                  pl.BlockSpec((4, C, C), lambda i: (0, 0, 0)),
                  pl.BlockSpec((1, D_POOL), lambda i: (0, 0))],
        out_specs=[pl.BlockSpec((S, D_POOL), lambda i: (0, 0)),
                   pl.BlockSpec((4, C, C), lambda i: (0, 0, 0)),
                   pl.BlockSpec((1, D_POOL), lambda i: (0, 0))],
        out_shape=[jax.ShapeDtypeStruct((S, D_POOL), BF16), jax.ShapeDtypeStruct((4, C, C), F32),
                   jax.ShapeDtypeStruct((1, D_POOL), F32)],
        scratch_shapes=[pltpu.VMEM((S + POOL_HALO, D_POOL), F32), pltpu.VMEM((S + POOL_HALO, C), F32),
                        pltpu.VMEM((S, C), F32)],
        compiler_params=_cparams(("arbitrary",), 56),
    )(u, dcat, w_pool, scale)


GLA_ROWS = 128
U_Q_BLK, U_K_BLK = 2, 3
U_V_BLK, U_G_BLK = 2, 3
U_R_BLK = 32


def _prefix_sum_rows(x):
    n = x.shape[0]
    row = lax.broadcasted_iota(jnp.int32, x.shape, 0)
    k = 1
    while k < n:
        x = x + jnp.where(row >= k, pltpu.roll(x, k, 0), 0.0)
        k *= 2
    return x


def _suffix_sum_rows(x):
    n = x.shape[0]
    row = lax.broadcasted_iota(jnp.int32, x.shape, 0)
    k = 1
    while k < n:
        x = x + jnp.where(row < n - k, pltpu.roll(x, n - k, 0), 0.0)
        k *= 2
    return x


def _log_sigmoid(z):
    return jnp.minimum(z, 0.0) - jnp.log(1.0 + jnp.exp(-jnp.abs(z)))


def _gla_chunk_terms(la_c, q_c, k_c):
    bc = _prefix_sum_rows(la_c)
    bl = jnp.sum(la_c, axis=0, keepdims=True)
    eb = jnp.exp(bc)
    enb = jnp.exp(-bc)
    etail = jnp.exp(bl - bc)
    qd = q_c * (GLA_DK ** -0.5) * eb
    ki = k_c * enb
    kt = k_c * etail
    d = jnp.exp(bl)
    return eb, enb, etail, qd, ki, kt, d


def _gla_fwd(u, w_alpha, b_alpha, gnorm, *, name):
    S = u.shape[0]
    RB = GLA_ROWS
    ncc = RB // CHUNK
    H, DK, DV = GLA_HEADS, GLA_DK, GLA_DV

    def body(q_ref, k_ref, v_ref, go_ref, r_ref, wa_ref, ba_ref, gn_ref, y_ref, o_ref, st_ref, state):
        i = pl.program_id(0)

        @pl.when(i == 0)
        def _():
            state[...] = jnp.zeros_like(state)

        z = _dot_nn(r_ref[...].astype(BF16), wa_ref[...]) + ba_ref[...]
        la = _log_sigmoid(z) / GATE_LOGIT_NORMALIZER
        ri = lax.broadcasted_iota(jnp.int32, (CHUNK, CHUNK), 0)
        ci = lax.broadcasted_iota(jnp.int32, (CHUNK, CHUNK), 1)
        tri = ri >= ci
        gn = gn_ref[...]
        for cc in range(ncc):
            rs = slice(cc * CHUNK, (cc + 1) * CHUNK)
            for h in range(H):
                ks = slice(h * DK, (h + 1) * DK)
                vs = slice(h * DV, (h + 1) * DV)
                _, _, _, qd, ki, kt, d = _gla_chunk_terms(la[rs, ks], q_ref[rs, ks], k_ref[rs, ks])
                qdb = qd.astype(BF16)
                vb = v_ref[rs, vs].astype(BF16)
                p = jnp.where(tri, _dot_nt(qdb, ki.astype(BF16)), 0.0)
                st = state[h]
                st_ref[cc, h] = st
                o = _dot_nn(p.astype(BF16), vb) + _dot_nt(qdb, st.astype(BF16))
                state[h] = st * d + _dot_tn(vb, kt.astype(BF16))
                o_ref[rs, vs] = o
                rinv = lax.rsqrt(jnp.mean(o * o, axis=-1, keepdims=True) + EPS)
                go = go_ref[rs, vs]
                y_ref[rs, vs] = (o * rinv * gn * (go * _sigmoid(go))).astype(BF16)

    nblk = S // RB
    return pl.pallas_call(
        body, name=name, grid=(nblk,),
        in_specs=[
            pl.BlockSpec((RB, GLA_DK_TOTAL), lambda i: (i, U_Q_BLK)),
            pl.BlockSpec((RB, GLA_DK_TOTAL), lambda i: (i, U_K_BLK)),
            pl.BlockSpec((RB, D_GLA), lambda i: (i, U_V_BLK)),
            pl.BlockSpec((RB, D_GLA), lambda i: (i, U_G_BLK)),
            pl.BlockSpec((RB, LANES), lambda i: (i, U_R_BLK)),
            pl.BlockSpec((LANES, GLA_DK_TOTAL), lambda i: (0, 0)),
            pl.BlockSpec((1, GLA_DK_TOTAL), lambda i: (0, 0)),
            pl.BlockSpec((1, DV), lambda i: (0, 0)),
        ],
        out_specs=[
            pl.BlockSpec((RB, D_GLA), lambda i: (i, 0)),
            pl.BlockSpec((RB, D_GLA), lambda i: (i, 0)),
            pl.BlockSpec((ncc, H, DV, DK), lambda i: (i, 0, 0, 0)),
        ],
        out_shape=[jax.ShapeDtypeStruct((S, D_GLA), BF16), jax.ShapeDtypeStruct((S, D_GLA), F32),
                   jax.ShapeDtypeStruct((S // CHUNK, H, DV, DK), F32)],
        scratch_shapes=[pltpu.VMEM((H, DV, DK), F32)],
        compiler_params=_cparams(("arbitrary",), 32),
    )(u, u, u, u, u, w_alpha, b_alpha, gnorm)


def _gla_bwd(u, o, states, dcat, w_alpha, b_alpha, gnorm, *, name):
    S = u.shape[0]
    RB = GLA_ROWS
    ncc = RB // CHUNK
    H, DK, DV = GLA_HEADS, GLA_DK, GLA_DV
    nblk = S // RB

    def body(q_ref, k_ref, v_ref, go_ref, r_ref, o_ref, st_ref, dy_ref, wa_ref, ba_ref, gn_ref,
             dq_ref, dk_ref, dv_ref, dgo_ref, dr_ref, dwa_ref, dba_ref, dgn_ref, dstate, dz_ref):
        i = pl.program_id(0)

        @pl.when(i == 0)
        def _():
            dstate[...] = jnp.zeros_like(dstate)
            dwa_ref[...] = jnp.zeros_like(dwa_ref)
            dba_ref[...] = jnp.zeros_like(dba_ref)
            dgn_ref[...] = jnp.zeros_like(dgn_ref)

        rb = r_ref[...].astype(BF16)
        wa = wa_ref[...]
        z = _dot_nn(rb, wa) + ba_ref[...]
        la = _log_sigmoid(z) / GATE_LOGIT_NORMALIZER
        ri = lax.broadcasted_iota(jnp.int32, (CHUNK, CHUNK), 0)
        ci = lax.broadcasted_iota(jnp.int32, (CHUNK, CHUNK), 1)
        tri = ri >= ci
        last_row = lax.broadcasted_iota(jnp.int32, (CHUNK, DK), 0) == CHUNK - 1
        gn = gn_ref[...]
        dgn = jnp.zeros((1, DV), F32)
        for cc in reversed(range(ncc)):
            rs = slice(cc * CHUNK, (cc + 1) * CHUNK)
            for h in range(H):
                ks = slice(h * DK, (h + 1) * DK)
                vs = slice(h * DV, (h + 1) * DV)
                eb, enb, etail, qd, ki, kt, d = _gla_chunk_terms(la[rs, ks], q_ref[rs, ks], k_ref[rs, ks])
                qdb, kib, ktb = qd.astype(BF16), ki.astype(BF16), kt.astype(BF16)
                vb = v_ref[rs, vs].astype(BF16)
                p = jnp.where(tri, _dot_nt(qdb, kib), 0.0)
                ov = o_ref[rs, vs]
                go = go_ref[rs, vs]
                dy = dy_ref[rs, vs]
                rinv = lax.rsqrt(jnp.mean(ov * ov, axis=-1, keepdims=True) + EPS)
                oh = ov * rinv
                sg = _sigmoid(go)
                dgo_ref[rs, vs] = (dy * (oh * gn) * (sg * (1.0 + go * (1.0 - sg)))).astype(BF16)
                don = dy * (go * sg)
                dgn = dgn + jnp.sum(don * oh, axis=0, keepdims=True)
                doh = don * gn
                do = rinv * (doh - oh * jnp.mean(doh * oh, axis=-1, keepdims=True))
                dob = do.astype(BF16)
                st = st_ref[cc, h]
                dst = dstate[h]
                stb, dstb = st.astype(BF16), dst.astype(BF16)
                dp = jnp.where(tri, _dot_nt(dob, vb), 0.0).astype(BF16)
                dv_ref[rs, vs] = (_dot_tn(p.astype(BF16), dob) + _dot_nt(ktb, dstb)).astype(BF16)
                dqd = _dot_nn(dp, kib) + _dot_nn(dob, stb)
                dki = _dot_tn(dp, qdb)
                dkt = _dot_nn(vb, dstb)
                dd = jnp.sum(dst * st, axis=0, keepdims=True)
                dstate[h] = dst * d + _dot_tn(dob, qdb)
                dq_ref[rs, ks] = (dqd * eb * (DK ** -0.5)).astype(BF16)
                dk_ref[rs, ks] = (dki * enb + dkt * etail).astype(BF16)
                dbl = jnp.sum(dkt * kt, axis=0, keepdims=True) + dd * d
                dbc = dqd * qd - dki * ki - dkt * kt
                dbc = dbc + jnp.where(last_row, dbl, 0.0)
                dla = _suffix_sum_rows(dbc)
                dz_ref[rs, ks] = dla * (1.0 / GATE_LOGIT_NORMALIZER) * (1.0 - _sigmoid(z[rs, ks]))
        dz = dz_ref[...]
        dzb = dz.astype(BF16)
        dr_ref[...] = _dot_nt(dzb, wa).astype(BF16)
        dwa_ref[...] += _dot_tn(rb, dzb)
        dba_ref[...] += jnp.sum(dz, axis=0, keepdims=True)
        dgn_ref[...] += dgn

    def rev(blk):
        return lambda i: (nblk - 1 - i, blk)

    return pl.pallas_call(
        body, name=name, grid=(nblk,),
        in_specs=[
            pl.BlockSpec((RB, GLA_DK_TOTAL), rev(U_Q_BLK)),
            pl.BlockSpec((RB, GLA_DK_TOTAL), rev(U_K_BLK)),
            pl.BlockSpec((RB, D_GLA), rev(U_V_BLK)),
            pl.BlockSpec((RB, D_GLA), rev(U_G_BLK)),
            pl.BlockSpec((RB, LANES), rev(U_R_BLK)),
            pl.BlockSpec((RB, D_GLA), rev(0)),
            pl.BlockSpec((ncc, H, DV, DK), lambda i: (nblk - 1 - i, 0, 0, 0)),
            pl.BlockSpec((RB, D_GLA), rev(1)),
            pl.BlockSpec((LANES, GLA_DK_TOTAL), lambda i: (0, 0)),
            pl.BlockSpec((1, GLA_DK_TOTAL), lambda i: (0, 0)),
            pl.BlockSpec((1, DV), lambda i: (0, 0)),
        ],
        out_specs=[
            pl.BlockSpec((RB, GLA_DK_TOTAL), rev(0)),
            pl.BlockSpec((RB, GLA_DK_TOTAL), rev(0)),
            pl.BlockSpec((RB, D_GLA), rev(0)),
            pl.BlockSpec((RB, D_GLA), rev(0)),
            pl.BlockSpec((RB, LANES), rev(0)),
            pl.BlockSpec((LANES, GLA_DK_TOTAL), lambda i: (0, 0)),
            pl.BlockSpec((1, GLA_DK_TOTAL), lambda i: (0, 0)),
            pl.BlockSpec((1, DV), lambda i: (0, 0)),
        ],
        out_shape=[
            jax.ShapeDtypeStruct((S, GLA_DK_TOTAL), BF16), jax.ShapeDtypeStruct((S, GLA_DK_TOTAL), BF16),
            jax.ShapeDtypeStruct((S, D_GLA), BF16), jax.ShapeDtypeStruct((S, D_GLA), BF16),
            jax.ShapeDtypeStruct((S, LANES), BF16),
            jax.ShapeDtypeStruct((LANES, GLA_DK_TOTAL), F32), jax.ShapeDtypeStruct((1, GLA_DK_TOTAL), F32),
            jax.ShapeDtypeStruct((1, DV), F32),
        ],
        scratch_shapes=[pltpu.VMEM((H, DV, DK), F32), pltpu.VMEM((RB, GLA_DK_TOTAL), F32)],
        compiler_params=_cparams(("arbitrary",), 32),
    )(u, u, u, u, u, o, states, dcat, w_alpha, b_alpha, gnorm)


def _row_tile(rows, cols, itemsize, budget=2 * 1024 * 1024):
    if rows * cols * itemsize <= budget or rows % 16:
        return rows
    best = 16
    for t in range(16, rows + 1, 16):
        if rows % t == 0 and t * cols * itemsize <= budget:
            best = t
    return best


def _adamw(w, g, m, v, *, name):
    R, C = w.shape
    tr = _row_tile(R, C, 4, budget=1024 * 1024)

    def body(w_ref, g_ref, m_ref, v_ref, d_ref, nm_ref, nv_ref):
        gv = g_ref[...]
        mn = ADAM_B1 * m_ref[...] + (1.0 - ADAM_B1) * gv
        vn = ADAM_B2 * v_ref[...] + (1.0 - ADAM_B2) * jnp.square(gv)
        m_hat = mn / (1.0 - ADAM_B1 ** ADAM_STEP)
        v_hat = vn / (1.0 - ADAM_B2 ** ADAM_STEP)
        d_ref[...] = -ADAM_LR * (m_hat / (jnp.sqrt(v_hat) + ADAM_EPS) + ADAM_WD * w_ref[...])
        nm_ref[...] = mn
        nv_ref[...] = vn

    spec = pl.BlockSpec((tr, C), lambda i: (i, 0))
    shp = jax.ShapeDtypeStruct((R, C), F32)
    return pl.pallas_call(
        body, name=name, grid=(R // tr,), in_specs=[spec] * 4, out_specs=[spec] * 3, out_shape=[shp] * 3,
        compiler_params=_cparams(("parallel",), 32),
    )(w, g, m, v)


def _pair_sum(g4, recv, c_idx, *, name):
    ns, _, R2, C = g4.shape
    tr = _row_tile(R2, C, 2)

    def body(c_ref, g_ref, r_ref, o_ref):
        o_ref[...] = (g_ref[...].astype(F32) + r_ref[...].astype(F32)).astype(BF16)

    return pl.pallas_call(
        body, name=name,
        grid_spec=pltpu.PrefetchScalarGridSpec(
            num_scalar_prefetch=1, grid=(ns, R2 // tr),
            in_specs=[pl.BlockSpec((None, None, tr, C), lambda s, i, c: (s, c[0], i, 0)),
                      pl.BlockSpec((None, tr, C), lambda s, i, c: (s, i, 0))],
            out_specs=pl.BlockSpec((None, tr, C), lambda s, i, c: (s, i, 0)),
        ),
        out_shape=jax.ShapeDtypeStruct((ns, R2, C), BF16),
        compiler_params=_cparams(("parallel", "parallel"), 32),
    )(c_idx, g4, recv)


def _chip_sum(part, recv, s_idx, *, name):
    _, R2, C = part.shape
    tr = _row_tile(R2, C, 4)

    def body(s_ref, p_ref, r_ref, o_ref):
        acc = p_ref[...].astype(F32)
        for j in range(N_CHIPS - 1):
            acc = acc + r_ref[j].astype(F32)
        o_ref[...] = acc

    return pl.pallas_call(
        body, name=name,
        grid_spec=pltpu.PrefetchScalarGridSpec(
            num_scalar_prefetch=1, grid=(R2 // tr,),
            in_specs=[pl.BlockSpec((None, tr, C), lambda i, s: (s[0], i, 0)),
                      pl.BlockSpec((N_CHIPS - 1, tr, C), lambda i, s: (0, i, 0))],
            out_specs=pl.BlockSpec((tr, C), lambda i, s: (i, 0)),
        ),
        out_shape=jax.ShapeDtypeStruct((R2, C), F32),
        compiler_params=_cparams(("parallel",), 32),
    )(s_idx, part, recv)


def _slab_sum(slabs, *, name):
    n, M, C = slabs.shape

    def body(x_ref, o_ref):
        acc = x_ref[0]
        for d in range(1, n):
            acc = acc + x_ref[d]
        o_ref[...] = acc

    return pl.pallas_call(
        body, name=name, out_shape=jax.ShapeDtypeStruct((M, C), F32),
    )(slabs)


def _mesh_position():
    x, y, c = lax.axis_index("x"), lax.axis_index("y"), lax.axis_index("c")
    other_chips = [(1 - x, y), (x, 1 - y), (1 - x, 1 - y)]
    return x, y, c, other_chips


ANY = pl.BlockSpec(memory_space=pl.ANY)


def _all_gather_weights(shards):
    K = len(shards)
    per = 2 * (N_CHIPS - 1)

    def body(*refs):
        ins, outs = refs[:K], refs[K:2 * K]
        send_sems, recv_sems, local_sems = refs[2 * K:]
        x, y, c, chips = _mesh_position()
        s = 2 * x + y
        sibling = (x, y, 1 - c)

        def region(k, slot, half):
            hr = shards[k].shape[0] // 2
            return outs[k].at[slot, pl.ds(half * hr, hr), :]

        def copy(k, n, src, dst, to):
            return pltpu.make_async_remote_copy(
                src_ref=src, dst_ref=dst, send_sem=send_sems.at[k * per + n], recv_sem=recv_sems.at[k * per + n],
                device_id=to, device_id_type=MESH)

        mine = [pltpu.make_async_copy(ins[k], outs[k].at[s], local_sems.at[k]) for k in range(K)]
        for cp in mine:
            cp.start()
        first = []
        for k in range(K):
            hr = shards[k].shape[0] // 2
            for j, chip in enumerate(chips):
                first.append(copy(k, j, ins[k].at[pl.ds(c * hr, hr), :], region(k, s, c), (*chip, c)))
        for cp in first:
            cp.start()
        passed = []
        for k in range(K):
            for j, chip in enumerate(chips):
                slot = 2 * chip[0] + chip[1]
                got = region(k, slot, c)
                copy(k, j, got, got, (*chip, c)).wait_recv()
                fwd = copy(k, (N_CHIPS - 1) + j, got, got, sibling)
                fwd.start()
                passed.append(fwd)
        for k in range(K):
            for j, chip in enumerate(chips):
                slot = 2 * chip[0] + chip[1]
                got = region(k, slot, 1 - c)
                copy(k, (N_CHIPS - 1) + j, got, got, sibling).wait_recv()
        for cp in first + passed:
            cp.wait_send()
        for cp in mine:
            cp.wait()

    return pl.pallas_call(
        body, name="gather_weights",
        in_specs=[ANY] * K, out_specs=[ANY] * K,
        out_shape=[jax.ShapeDtypeStruct((N_CHIPS,) + a.shape, a.dtype) for a in shards],
        scratch_shapes=[pltpu.SemaphoreType.DMA((K * per,)), pltpu.SemaphoreType.DMA((K * per,)),
                        pltpu.SemaphoreType.DMA((K,))],
    )(*shards)


def _swap_halves(grads4):
    K = len(grads4)

    def body(*refs):
        ins, outs = refs[:K], refs[K:2 * K]
        send_sems, recv_sems = refs[2 * K:]
        x, y, c, _ = _mesh_position()
        copies = []
        for k in range(K):
            ns = grads4[k].shape[0]
            cp = pltpu.make_async_remote_copy(
                src_ref=ins[k].at[pl.ds(0, ns), 1 - c], dst_ref=outs[k],
                send_sem=send_sems.at[k], recv_sem=recv_sems.at[k], device_id=(x, y, 1 - c), device_id_type=MESH)
            cp.start()
            copies.append(cp)
        for cp in copies:
            cp.wait()

    return pl.pallas_call(
        body, name="swap_grad_halves",
        in_specs=[ANY] * K, out_specs=[ANY] * K,
        out_shape=[jax.ShapeDtypeStruct((a.shape[0],) + a.shape[2:], a.dtype) for a in grads4],
        scratch_shapes=[pltpu.SemaphoreType.DMA((K,)), pltpu.SemaphoreType.DMA((K,))],
    )(*grads4)


def _exchange_chip_partials(parts):
    K = len(parts)
    per = N_CHIPS - 1

    def body(*refs):
        ins, outs = refs[:K], refs[K:2 * K]
        send_sems, recv_sems = refs[2 * K:]
        x, y, c, chips = _mesh_position()
        copies = []
        for k in range(K):
            for j, chip in enumerate(chips):
                cp = pltpu.make_async_remote_copy(
                    src_ref=ins[k].at[2 * chip[0] + chip[1]], dst_ref=outs[k].at[j],
                    send_sem=send_sems.at[k * per + j], recv_sem=recv_sems.at[k * per + j],
                    device_id=(*chip, c), device_id_type=MESH)
                cp.start()
                copies.append(cp)
        for cp in copies:
            cp.wait()

    return pl.pallas_call(
        body, name="exchange_chip_partials",
        in_specs=[ANY] * K, out_specs=[ANY] * K,
        out_shape=[jax.ShapeDtypeStruct((N_CHIPS - 1,) + a.shape[1:], a.dtype) for a in parts],
        scratch_shapes=[pltpu.SemaphoreType.DMA((K * per,)), pltpu.SemaphoreType.DMA((K * per,))],
    )(*parts)


def _join_halves(halves):
    K = len(halves)

    def body(*refs):
        ins, outs = refs[:K], refs[K:2 * K]
        send_sems, recv_sems, local_sems = refs[2 * K:]
        x, y, c, _ = _mesh_position()
        copies, mine = [], []
        for k in range(K):
            r2 = halves[k].shape[0]
            dst = outs[k].at[pl.ds(c * r2, r2), :]
            lc = pltpu.make_async_copy(ins[k], dst, local_sems.at[k])
            lc.start()
            mine.append(lc)
            cp = pltpu.make_async_remote_copy(
                src_ref=ins[k], dst_ref=dst, send_sem=send_sems.at[k], recv_sem=recv_sems.at[k],
                device_id=(x, y, 1 - c), device_id_type=MESH)
            cp.start()
            copies.append(cp)
        for cp in copies:
            cp.wait()
        for lc in mine:
            lc.wait()

    return pl.pallas_call(
        body, name="join_grad_halves",
        in_specs=[ANY] * K, out_specs=[ANY] * K,
        out_shape=[jax.ShapeDtypeStruct((2 * a.shape[0], a.shape[1]), a.dtype) for a in halves],
        scratch_shapes=[pltpu.SemaphoreType.DMA((K,)), pltpu.SemaphoreType.DMA((K,)), pltpu.SemaphoreType.DMA((K,))],
    )(*halves)


def _all_gather_slab(slab):
    m_per, n = slab.shape

    def body(x_ref, out_ref, send_sems, recv_sems, local_sem):
        x, y, c, chips = _mesh_position()
        me, sibling = (x, y, c), (x, y, 1 - c)

        def rows(px, py, pc):
            return out_ref.at[pl.ds((4 * px + 2 * py + pc) * m_per, m_per), :]

        def copy(k, block, to, src=None):
            return pltpu.make_async_remote_copy(
                src_ref=rows(*block) if src is None else src, dst_ref=rows(*block),
                send_sem=send_sems.at[k], recv_sem=recv_sems.at[k], device_id=to, device_id_type=MESH)

        mine = pltpu.make_async_copy(x_ref, rows(*me), local_sem)
        mine.start()
        first = [copy(0, me, sibling, src=x_ref)]
        first += [copy(1 + j, me, (*chip, c), src=x_ref) for j, chip in enumerate(chips)]
        for cp in first:
            cp.start()
        passed = [copy(4 + j, (*chip, c), sibling) for j, chip in enumerate(chips)]
        for j, chip in enumerate(chips):
            copy(1 + j, (*chip, c), me).wait_recv()
            passed[j].start()
        copy(0, sibling, me).wait_recv()
        for j, chip in enumerate(chips):
            copy(4 + j, (*chip, 1 - c), me).wait_recv()
        for cp in first + passed:
            cp.wait_send()
        mine.wait()

    return pl.pallas_call(
        body, name="gather_small_grads",
        out_shape=jax.ShapeDtypeStruct((N_DEV * m_per, n), slab.dtype),
        in_specs=[pl.BlockSpec(memory_space=pltpu.VMEM)],
        out_specs=pl.BlockSpec(memory_space=pltpu.VMEM),
        scratch_shapes=[pltpu.SemaphoreType.DMA((7,)), pltpu.SemaphoreType.DMA((7,)), pltpu.SemaphoreType.DMA],
    )(slab)


def _ffn_forward(h, g, w_in, w_out, tag):
    n = _rms_fwd(h, g, name=f"{tag}_norm")
    gu, a = _ffn_up(n, w_in, name=f"{tag}_up")
    h_out = _mm_nn(a, w_out, h, 0.5, tm=512, tn=D_MODEL, tk=COL_TILE, name=f"{tag}_down")
    return h_out, (n, gu, a)


def _ffn_backward(dh, dhb, saved, h_in, g, w_in, w_out, tag):
    n, gu, a = saved
    S = n.shape[0]
    ns, D, cs = w_in.shape
    F = w_out.shape[0]
    dgu = _ffn_bwd_act(dhb, w_out, gu, name=f"{tag}_bwd_act")
    to = 512
    dw_out = _mm_tn(
        a, dhb, grid=(F // to,),
        a_spec=pl.BlockSpec((S, to), lambda j: (0, j)), b_spec=pl.BlockSpec((S, D), lambda j: (0, 0)),
        out_spec=pl.BlockSpec((to, D), lambda j: (j, 0)), out_shape=jax.ShapeDtypeStruct((F, D), BF16),
        scale=0.5, name=f"{tag}_dw_out")
    ti = MXU_TILE
    per_g, per_s = F // ti, cs // ti
    dw_in = _mm_tn(
        n.T, dgu, grid=(2 * F // ti,), a_is_transposed=True,
        a_spec=pl.BlockSpec((D, S), lambda j: (0, 0)),
        b_spec=pl.BlockSpec((None, S, ti), lambda j: (lax.div(j, per_g), 0, lax.rem(j, per_g))),
        out_spec=pl.BlockSpec((None, D, ti), lambda j: (lax.div(j, per_s), 0, lax.rem(j, per_s))),
        out_shape=jax.ShapeDtypeStruct((ns, D, cs), BF16), scale=1.0, name=f"{tag}_dw_in")
    dh_in, dhb_in, dg = _mm_nt_rmsbwd(dgu, w_in, h_in, dh, g, tk=COL_TILE, name=f"{tag}_dx")
    return dh_in, dhb_in, dg, dw_in, dw_out


def kernel(x, ffn1_norm, ffn1_w_in, ffn1_w_out, mix_norm, w_in_mix, w_pool, pool_scale, w_alpha, b_alpha, gla_norm, w_out_mix, ffn2_norm, ffn2_w_in, ffn2_w_out, final_norm, loss_target, m_ffn1_norm, m_ffn1_w_in, m_ffn1_w_out, m_mix_norm, m_w_in_mix, m_w_pool, m_pool_scale, m_w_alpha, m_b_alpha, m_gla_norm, m_w_out_mix, m_ffn2_norm, m_ffn2_w_in, m_ffn2_w_out, m_final_norm, v_ffn1_norm, v_ffn1_w_in, v_ffn1_w_out, v_mix_norm, v_w_in_mix, v_w_pool, v_pool_scale, v_w_alpha, v_b_alpha, v_gla_norm, v_w_out_mix, v_ffn2_norm, v_ffn2_w_in, v_ffn2_w_out, v_final_norm):
    names = ["ffn1_norm", "ffn1_w_in", "ffn1_w_out", "mix_norm", "w_in_mix", "w_pool", "pool_scale", "w_alpha",
             "b_alpha", "gla_norm", "w_out_mix", "ffn2_norm", "ffn2_w_in", "ffn2_w_out", "final_norm"]
    weights = dict(zip(names, [ffn1_norm, ffn1_w_in, ffn1_w_out, mix_norm, w_in_mix, w_pool, pool_scale, w_alpha,
                               b_alpha, gla_norm, w_out_mix, ffn2_norm, ffn2_w_in, ffn2_w_out, final_norm]))
    moms = dict(zip(names, [m_ffn1_norm, m_ffn1_w_in, m_ffn1_w_out, m_mix_norm, m_w_in_mix, m_w_pool, m_pool_scale,
                            m_w_alpha, m_b_alpha, m_gla_norm, m_w_out_mix, m_ffn2_norm, m_ffn2_w_in, m_ffn2_w_out,
                            m_final_norm]))
    vels = dict(zip(names, [v_ffn1_norm, v_ffn1_w_in, v_ffn1_w_out, v_mix_norm, v_w_in_mix, v_w_pool, v_pool_scale,
                            v_w_alpha, v_b_alpha, v_gla_norm, v_w_out_mix, v_ffn2_norm, v_ffn2_w_in, v_ffn2_w_out,
                            v_final_norm]))
    xi, yi, ci = lax.axis_index("x"), lax.axis_index("y"), lax.axis_index("c")
    chip = 2 * xi + yi
    c_idx = jnp.reshape(ci, (1,)).astype(jnp.int32)
    s_idx = jnp.reshape(chip, (1,)).astype(jnp.int32)

    def flat2d(a):
        return a.reshape(-1, a.shape[-1])

    sharded = ["ffn1_w_in", "ffn1_w_out", "w_in_mix", "w_out_mix", "ffn2_w_in", "ffn2_w_out", "w_pool", "w_alpha"]
    send = [flat2d(weights[n]).astype(F32 if n == "w_alpha" else BF16) for n in sharded]
    gw = dict(zip(sharded, _all_gather_weights(send)))
    small_params = dict(g1=ffn1_norm, gm=mix_norm, g2=ffn2_norm, gf=final_norm.reshape(1, D_MODEL),
                        pool_scale=pool_scale, b_alpha=b_alpha, gla_norm=gla_norm)
    loss_blk, dx, full, small = _forward_backward(x[0], loss_target[0], gw, small_params)
    loss = lax.psum(loss_blk[0, 0], ("x", "y", "c"))
    big = list(full)

    g4 = [full[n].reshape(N_CHIPS, 2, full[n].shape[1] // 2, full[n].shape[2]) for n in big]
    from_sibling = _swap_halves(g4)
    pair = [_pair_sum(a, b, c_idx, name=f"pair_sum_{n}") for n, a, b in zip(big, g4, from_sibling)]
    from_chips = _exchange_chip_partials(pair)
    halves = [_chip_sum(a, b, s_idx, name=f"chip_sum_{n}") for n, a, b in zip(big, pair, from_chips)]
    grads = dict(zip(big, _join_halves(halves)))

    small_names = ["ffn1_norm", "mix_norm", "ffn2_norm", "final_norm", "pool_scale", "b_alpha", "gla_norm", "w_alpha"]
    rows = [a.size // LANES for a in small]
    slab = jnp.concatenate([a.reshape(-1, LANES) for a in small], axis=0)
    pad = -slab.shape[0] % 8
    slab = jnp.pad(slab, ((0, pad), (0, 0)))
    gathered = _all_gather_slab(slab).reshape(N_DEV, slab.shape[0], LANES)
    total = _slab_sum(gathered, name="sum_small_grads")
    off = 0
    for n, a, r in zip(small_names, small, rows):
        grads[n] = total[off:off + r].reshape(a.shape)
        off += r
    grads["w_alpha"] = lax.dynamic_slice_in_dim(grads["w_alpha"], chip * (GLA_DK_TOTAL // N_CHIPS),
                                                GLA_DK_TOTAL // N_CHIPS, axis=1)

    out_g, out_d, out_m, out_v = [], [], [], []
    for n in names:
        w = weights[n]
        g = grads[n].reshape(w.shape)
        w2 = flat2d(w) if w.ndim > 1 else w.reshape(1, -1)
        d, nm, nv = _adamw(w2, g.reshape(w2.shape), moms[n].reshape(w2.shape), vels[n].reshape(w2.shape),
                           name=f"adamw_{n}")
        out_g.append(g)
        out_d.append(d.reshape(w.shape))
        out_m.append(nm.reshape(w.shape))
        out_v.append(nv.reshape(w.shape))
    return (loss, dx[None], *out_g, *out_d, *out_m, *out_v)


def _forward_backward(h0, target, gw, sp):
    g1, gm, g2, gf = sp["g1"], sp["gm"], sp["g2"], sp["gf"]
    pool_scale, b_alpha, gla_norm = sp["pool_scale"], sp["b_alpha"], sp["gla_norm"]
    w1_in, w2_in = gw["ffn1_w_in"], gw["ffn2_w_in"]
    w1_out = gw["ffn1_w_out"].reshape(D_FF, D_MODEL)
    w2_out = gw["ffn2_w_out"].reshape(D_FF, D_MODEL)
    w_omix = gw["w_out_mix"].reshape(D_MODEL, D_MODEL)
    cs_mix = D_IN // N_CHIPS
    w_mix = jnp.concatenate([gw["w_in_mix"][t] for t in range(N_CHIPS)], axis=1)
    w_mix = jnp.pad(w_mix, ((0, 0), (0, D_IN_PAD - D_IN)))[None]
    wp = gw["w_pool"].reshape(N_CHIPS, 4, POOL_GROUP_DIM // N_CHIPS, POOL_GROUP_DIM)
    wp = wp.transpose(1, 0, 2, 3).reshape(4, POOL_GROUP_DIM, POOL_GROUP_DIM)
    wa = gw["w_alpha"].transpose(1, 0, 2).reshape(GLA_GATE_RANK, GLA_DK_TOTAL)
    wa = jnp.pad(wa, ((0, LANES - GLA_GATE_RANK), (0, 0))).astype(BF16)

    h1, saved1 = _ffn_forward(h0, g1, w1_in, w1_out, "ffn1")
    n_mix = _rms_fwd(h1, gm, name="mix_norm")
    u = _mm_nn(n_mix, w_mix[0], None, 1.0, tm=512, tn=1408, tk=D_MODEL, name="mix_in")
    y_pool = _pool_fwd(u, wp, pool_scale, name="pool_fwd")
    y_gla, o_gla, states = _gla_fwd(u, wa, b_alpha, gla_norm, name="gla_fwd")
    cat = jnp.concatenate([y_pool, y_gla], axis=1)
    h2 = _mm_nn(cat, w_omix, h1, 1.0, tm=512, tn=D_MODEL, tk=1024, name="mix_out")
    h3, saved2 = _ffn_forward(h2, g2, w2_in, w2_out, "ffn2")
    loss_blk, dh3, dh3b, d_gf = _final_loss(h3, gf, target, name="final_loss")

    dh2, dh2b, d_g2, dw2_in, dw2_out = _ffn_backward(dh3, dh3b, saved2, h2, g2, w2_in, w2_out, "ffn2")
    S = h0.shape[0]
    dcat = _mm_nt(dh2b, w_omix, tm=512, tn=1024, name="mix_out_dx")
    dw_omix = _mm_tn(
        cat, dh2b, grid=(4,),
        a_spec=pl.BlockSpec((S, 512), lambda j: (0, j)), b_spec=pl.BlockSpec((S, D_MODEL), lambda j: (0, 0)),
        out_spec=pl.BlockSpec((512, D_MODEL), lambda j: (j, 0)),
        out_shape=jax.ShapeDtypeStruct((D_MODEL, D_MODEL), BF16), scale=1.0, name="mix_out_dw")
    dp, dw_pool, d_pscale = _pool_bwd(u, dcat, wp, pool_scale, name="pool_bwd")
    dq, dk, dv, dgo, dr, d_wa, d_ba, d_gn = _gla_bwd(u, o_gla, states, dcat, wa, b_alpha, gla_norm, name="gla_bwd")
    du = jnp.concatenate([dp, dq, dk, dv, dgo, dr], axis=1)[None]
    tn_mix = COL_TILE
    dw_mix = _mm_tn(
        n_mix.T, du, grid=(2, D_IN_PAD // tn_mix), a_is_transposed=True,
        a_spec=pl.BlockSpec((D_MODEL // 2, S), lambda i, j: (i, 0)),
        b_spec=pl.BlockSpec((None, S, tn_mix), lambda i, j: (0, 0, j)),
        out_spec=pl.BlockSpec((D_MODEL // 2, tn_mix), lambda i, j: (i, j)),
        out_shape=jax.ShapeDtypeStruct((D_MODEL, D_IN_PAD), BF16), scale=1.0, name="mix_in_dw")
    dh1, dh1b, d_gm = _mm_nt_rmsbwd(du, w_mix, h1, dh2, gm, tk=tn_mix, name="mix_in_dx")
    dx, _, d_g1, dw1_in, dw1_out = _ffn_backward(dh1, dh1b, saved1, h0, g1, w1_in, w1_out, "ffn1")

    dw_mix_s = dw_mix[:, :D_IN].reshape(D_MODEL, N_CHIPS, cs_mix).transpose(1, 0, 2)
    dw_pool_s = dw_pool.reshape(4, N_CHIPS, POOL_GROUP_DIM // N_CHIPS, POOL_GROUP_DIM).transpose(1, 0, 2, 3)
    dw_pool_s = dw_pool_s.reshape(N_CHIPS, POOL_GROUP_DIM, POOL_GROUP_DIM).astype(BF16)
    full = {
        "ffn1_w_in": dw1_in,
        "ffn1_w_out": dw1_out.reshape(N_CHIPS, D_FF // N_CHIPS, D_MODEL),
        "w_in_mix": dw_mix_s,
        "w_out_mix": dw_omix.reshape(N_CHIPS, D_MODEL // N_CHIPS, D_MODEL),
        "ffn2_w_in": dw2_in,
        "ffn2_w_out": dw2_out.reshape(N_CHIPS, D_FF // N_CHIPS, D_MODEL),
        "w_pool": dw_pool_s,
    }
    small = [d_g1, d_gm, d_g2, d_gf, d_pscale, d_ba, d_gn, d_wa[:GLA_GATE_RANK]]
    return loss_blk, dx, full, small
```

```python
import functools

import jax
import jax.numpy as jnp
from jax import lax
from jax.experimental import pallas as pl
from jax.experimental.pallas import tpu as pltpu

F32 = jnp.float32
BF16 = jnp.bfloat16
MESH = pl.DeviceIdType.MESH

D_MODEL = 2048
D_FF = 5632
D_POOL = 1024
POOL_WINDOWS = (2, 4, 8, 16)
POOL_GROUP_DIM = 256
D_GLA = 1024
GLA_HEADS = 4
GLA_DV = 256
GLA_DK = 128
GLA_DK_TOTAL = 512
GLA_GATE_RANK = 16
GATE_LOGIT_NORMALIZER = 16.0
CHUNK = 64
D_IN = 4112
D_IN_PAD = 4224
EPS = 1e-6

ADAM_LR = 0.001
ADAM_B1 = 0.9
ADAM_B2 = 0.999
ADAM_EPS = 1e-08
ADAM_WD = 0.01
ADAM_STEP = 10

N_CHIPS = 4
N_DEV = 8
V7X_VMEM_BYTES = 64 * 1024 * 1024
LANES = 128
MXU_TILE = 256
COL_TILE = 1408


def _cparams(semantics, vmem_mb):
    assert vmem_mb * 1024 * 1024 < V7X_VMEM_BYTES
    return pltpu.CompilerParams(dimension_semantics=semantics, vmem_limit_bytes=vmem_mb * 1024 * 1024)


def _dot_nn(a, b):
    return jnp.dot(a, b, preferred_element_type=F32)


def _dot_nt(a, b):
    return lax.dot_general(a, b, (((1,), (1,)), ((), ())), preferred_element_type=F32)


def _dot_tn(a, b):
    return lax.dot_general(a, b, (((0,), (0,)), ((), ())), preferred_element_type=F32)


def _sigmoid(x):
    return 1.0 / (1.0 + jnp.exp(-x))


def _rms_fwd(x, g, *, name):
    S, D = x.shape
    tm = 256

    def body(x_ref, g_ref, o_ref):
        xv = x_ref[...]
        r = lax.rsqrt(jnp.mean(xv * xv, axis=-1, keepdims=True) + EPS)
        o_ref[...] = (xv * r * g_ref[...]).astype(BF16)

    return pl.pallas_call(
        body, name=name, grid=(S // tm,),
        in_specs=[pl.BlockSpec((tm, D), lambda i: (i, 0)), pl.BlockSpec((1, D), lambda i: (0, 0))],
        out_specs=pl.BlockSpec((tm, D), lambda i: (i, 0)),
        out_shape=jax.ShapeDtypeStruct((S, D), BF16),
        compiler_params=_cparams(("parallel",), 32),
    )(x, g)


def _ffn_up(n, w_in, *, name):
    S, D = n.shape
    ns, _, cs = w_in.shape
    half = ns // 2
    F = cs * half
    tm, tn = 256, COL_TILE
    nb = cs // tn

    def body(n_ref, wg_ref, wu_ref, gu_ref, a_ref):
        nv = n_ref[...]
        g = _dot_nn(nv, wg_ref[...])
        u = _dot_nn(nv, wu_ref[...])
        gu_ref[0] = g.astype(BF16)
        gu_ref[1] = u.astype(BF16)
        a_ref[...] = (g * _sigmoid(g) * u).astype(BF16)

    return pl.pallas_call(
        body, name=name, grid=(F // tn, S // tm),
        in_specs=[
            pl.BlockSpec((tm, D), lambda j, i: (i, 0)),
            pl.BlockSpec((None, D, tn), lambda j, i: (lax.div(j, nb), 0, lax.rem(j, nb))),
            pl.BlockSpec((None, D, tn), lambda j, i: (half + lax.div(j, nb), 0, lax.rem(j, nb))),
        ],
        out_specs=[
            pl.BlockSpec((2, tm, tn), lambda j, i: (0, i, j)),
            pl.BlockSpec((tm, tn), lambda j, i: (i, j)),
        ],
        out_shape=[jax.ShapeDtypeStruct((2, S, F), BF16), jax.ShapeDtypeStruct((S, F), BF16)],
        compiler_params=_cparams(("parallel", "parallel"), 48),
    )(n, w_in, w_in)


def _mm_nn(a, b, resid, scale, *, tm, tn, tk, name):
    S, K = a.shape
    N = b.shape[1]
    nk = K // tk

    def body(*refs):
        if resid is None:
            a_ref, b_ref, o_ref, acc_ref = refs
            r_ref = None
        else:
            a_ref, b_ref, r_ref, o_ref, acc_ref = refs
        k = pl.program_id(2)

        @pl.when(k == 0)
        def _():
            acc_ref[...] = jnp.zeros_like(acc_ref)

        acc_ref[...] += _dot_nn(a_ref[...], b_ref[...])

        @pl.when(k == nk - 1)
        def _():
            out = acc_ref[...] * scale
            if r_ref is not None:
                out = r_ref[...] + out
            o_ref[...] = out

    in_specs = [pl.BlockSpec((tm, tk), lambda i, j, k: (i, k)), pl.BlockSpec((tk, tn), lambda i, j, k: (k, j))]
    args = [a, b]
    if resid is not None:
        in_specs.append(pl.BlockSpec((tm, tn), lambda i, j, k: (i, j)))
        args.append(resid)
    return pl.pallas_call(
        body, name=name, grid=(S // tm, N // tn, nk),
        in_specs=in_specs,
        out_specs=pl.BlockSpec((tm, tn), lambda i, j, k: (i, j)),
        out_shape=jax.ShapeDtypeStruct((S, N), F32),
        scratch_shapes=[pltpu.VMEM((tm, tn), F32)],
        compiler_params=_cparams(("parallel", "parallel", "arbitrary"), 48),
    )(*args)


def _mm_nt(a, b, *, tm, tn, name):
    S, K = a.shape
    N = b.shape[0]

    def body(a_ref, b_ref, o_ref):
        o_ref[...] = _dot_nt(a_ref[...], b_ref[...])

    return pl.pallas_call(
        body, name=name, grid=(N // tn, S // tm),
        in_specs=[pl.BlockSpec((tm, K), lambda j, i: (i, 0)), pl.BlockSpec((tn, K), lambda j, i: (j, 0))],
        out_specs=pl.BlockSpec((tm, tn), lambda j, i: (i, j)),
        out_shape=jax.ShapeDtypeStruct((S, N), F32),
        compiler_params=_cparams(("parallel", "parallel"), 48),
    )(a, b)


def _mm_tn(a, b, *, grid, a_spec, b_spec, out_spec, out_shape, scale, name, a_is_transposed=False):
    dot = _dot_nn if a_is_transposed else _dot_tn

    def body(a_ref, b_ref, o_ref):
        o_ref[...] = (scale * dot(a_ref[...], b_ref[...])).astype(o_ref.dtype)

    return pl.pallas_call(
        body, name=name, grid=grid, in_specs=[a_spec, b_spec], out_specs=out_spec, out_shape=out_shape,
        compiler_params=_cparams(("parallel",) * len(grid), 56),
    )(a, b)


def _ffn_bwd_act(dhb, w_out, gu, *, name):
    S, D = dhb.shape
    F = w_out.shape[0]
    tm, tn = 256, COL_TILE

    def body(dh_ref, w_ref, gu_ref, dgu_ref):
        da = 0.5 * _dot_nt(dh_ref[...], w_ref[...])
        g = gu_ref[0].astype(F32)
        u = gu_ref[1].astype(F32)
        s = _sigmoid(g)
        dgu_ref[0] = (da * u * (s * (1.0 + g * (1.0 - s)))).astype(BF16)
        dgu_ref[1] = (da * (g * s)).astype(BF16)

    return pl.pallas_call(
        body, name=name, grid=(F // tn, S // tm),
        in_specs=[
            pl.BlockSpec((tm, D), lambda j, i: (i, 0)),
            pl.BlockSpec((tn, D), lambda j, i: (j, 0)),
            pl.BlockSpec((2, tm, tn), lambda j, i: (0, i, j)),
        ],
        out_specs=pl.BlockSpec((2, tm, tn), lambda j, i: (0, i, j)),
        out_shape=jax.ShapeDtypeStruct((2, S, F), BF16),
        compiler_params=_cparams(("parallel", "parallel"), 48),
    )(dhb, w_out, gu)


def _mm_nt_rmsbwd(dact, w, h_in, dh_out, g, *, tk, name):
    ng, S, fg = dact.shape
    ns, D, cs = w.shape
    assert ng * fg == ns * cs
    tm, rc = 512, 64
    kpg, kps = fg // tk, cs // tk
    nk = ng * kpg

    def body(a_ref, w_ref, h_ref, dho_ref, g_ref, dh_ref, dhb_ref, dg_ref, acc_ref):
        i = pl.program_id(0)
        k = pl.program_id(1)

        @pl.when(k == 0)
        def _():
            acc_ref[...] = jnp.zeros_like(acc_ref)

        acc_ref[...] += _dot_nt(a_ref[...], w_ref[...])

        @pl.when(jnp.logical_and(i == 0, k == 0))
        def _():
            dg_ref[...] = jnp.zeros_like(dg_ref)

        @pl.when(k == nk - 1)
        def _():
            gv = g_ref[...]

            def rows_step(c, dg):
                rows = pl.ds(pl.multiple_of(c * rc, rc), rc)
                dn = acc_ref[rows, :]
                xv = h_ref[rows, :]
                r = lax.rsqrt(jnp.mean(xv * xv, axis=-1, keepdims=True) + EPS)
                xh = xv * r
                dng = dn * gv
                dx = r * (dng - xh * jnp.mean(dng * xh, axis=-1, keepdims=True))
                out = dho_ref[rows, :] + dx
                dh_ref[rows, :] = out
                dhb_ref[rows, :] = out.astype(BF16)
                return dg + jnp.sum(dn * xh, axis=0, keepdims=True)

            dg_ref[...] += lax.fori_loop(0, tm // rc, rows_step, jnp.zeros((1, D), F32))

    return pl.pallas_call(
        body, name=name, grid=(S // tm, nk),
        in_specs=[
            pl.BlockSpec((None, tm, tk), lambda i, k: (lax.div(k, kpg), i, lax.rem(k, kpg))),
            pl.BlockSpec((None, D, tk), lambda i, k: (lax.div(k, kps), 0, lax.rem(k, kps))),
            pl.BlockSpec((tm, D), lambda i, k: (i, 0)),
            pl.BlockSpec((tm, D), lambda i, k: (i, 0)),
            pl.BlockSpec((1, D), lambda i, k: (0, 0)),
        ],
        out_specs=[
            pl.BlockSpec((tm, D), lambda i, k: (i, 0)),
            pl.BlockSpec((tm, D), lambda i, k: (i, 0)),
            pl.BlockSpec((1, D), lambda i, k: (0, 0)),
        ],
        out_shape=[jax.ShapeDtypeStruct((S, D), F32), jax.ShapeDtypeStruct((S, D), BF16),
                   jax.ShapeDtypeStruct((1, D), F32)],
        scratch_shapes=[pltpu.VMEM((tm, D), F32)],
        compiler_params=_cparams(("arbitrary", "arbitrary"), 56),
    )(dact, w, h_in, dh_out, g)


def _final_loss(h, g, target, *, name):
    S, D = h.shape
    tm = 256

    def body(h_ref, g_ref, t_ref, loss_ref, dh_ref, dhb_ref, dg_ref):
        i = pl.program_id(0)

        @pl.when(i == 0)
        def _():
            loss_ref[...] = jnp.zeros_like(loss_ref)
            dg_ref[...] = jnp.zeros_like(dg_ref)

        xv = h_ref[...]
        gv = g_ref[...]
        r = lax.rsqrt(jnp.mean(xv * xv, axis=-1, keepdims=True) + EPS)
        xh = xv * r
        e = xh * gv - t_ref[...]
        loss_ref[...] += 0.5 * jnp.sum(jnp.mean(e * e, axis=-1, keepdims=True))
        dy = e * (1.0 / D)
        dg_ref[...] += jnp.sum(dy * xh, axis=0, keepdims=True)
        dyg = dy * gv
        dx = r * (dyg - xh * jnp.mean(dyg * xh, axis=-1, keepdims=True))
        dh_ref[...] = dx
        dhb_ref[...] = dx.astype(BF16)

    return pl.pallas_call(
        body, name=name, grid=(S // tm,),
        in_specs=[pl.BlockSpec((tm, D), lambda i: (i, 0)), pl.BlockSpec((1, D), lambda i: (0, 0)),
                  pl.BlockSpec((tm, D), lambda i: (i, 0))],
        out_specs=[pl.BlockSpec((8, LANES), lambda i: (0, 0)), pl.BlockSpec((tm, D), lambda i: (i, 0)),
                   pl.BlockSpec((tm, D), lambda i: (i, 0)), pl.BlockSpec((1, D), lambda i: (0, 0))],
        out_shape=[jax.ShapeDtypeStruct((8, LANES), F32), jax.ShapeDtypeStruct((S, D), F32),
                   jax.ShapeDtypeStruct((S, D), BF16), jax.ShapeDtypeStruct((1, D), F32)],
        compiler_params=_cparams(("arbitrary",), 40),
    )(h, g, target)


POOL_HALO = 16
POOL_ROWS = 256


def _pool_window_mean_minus_token(ext, tok0, w):
    s = ext
    k = 1
    while k < w:
        s = s + pltpu.roll(s, k, 0)
        k *= 2
    win = s[POOL_HALO:, :]
    tok = tok0 + lax.broadcasted_iota(jnp.int32, (POOL_ROWS, 1), 0)
    cnt = jnp.minimum(tok + 1, w).astype(F32)
    return win / cnt - ext[POOL_HALO:, :], cnt


def _pool_fwd(u, w_pool, scale, *, name):
    S = u.shape[0]
    C = POOL_GROUP_DIM
    nsteps = S // POOL_ROWS

    def body(p_ref, w_ref, sc_ref, y_ref, xp_ref):
        xp_ref[0:POOL_HALO, :] = jnp.zeros((POOL_HALO, D_POOL), F32)
        xp_ref[POOL_HALO:, :] = p_ref[...]
        for gi, win in enumerate(POOL_WINDOWS):
            cols = slice(gi * C, (gi + 1) * C)

            def step(c, carry, cols=cols, win=win, gi=gi):
                r0 = pl.multiple_of(c * POOL_ROWS, POOL_ROWS)
                ext = xp_ref[pl.ds(r0, POOL_ROWS + POOL_HALO), cols]
                pooled, _ = _pool_window_mean_minus_token(ext, r0, win)
                y = _dot_nn(pooled.astype(BF16), w_ref[gi]) * sc_ref[:, cols]
                y_ref[pl.ds(r0, POOL_ROWS), cols] = y.astype(BF16)
                return carry

            lax.fori_loop(0, nsteps, step, 0)

    return pl.pallas_call(
        body, name=name, grid=(1,),
        in_specs=[pl.BlockSpec((S, D_POOL), lambda i: (0, 0)),
                  pl.BlockSpec((4, C, C), lambda i: (0, 0, 0)),
                  pl.BlockSpec((1, D_POOL), lambda i: (0, 0))],
        out_specs=pl.BlockSpec((S, D_POOL), lambda i: (0, 0)),
        out_shape=jax.ShapeDtypeStruct((S, D_POOL), BF16),
        scratch_shapes=[pltpu.VMEM((S + POOL_HALO, D_POOL), F32)],
        compiler_params=_cparams(("arbitrary",), 48),
    )(u, w_pool, scale)


def _pool_bwd(u, dcat, w_pool, scale, *, name):
    S = u.shape[0]
    C = POOL_GROUP_DIM
    nsteps = S // POOL_ROWS

    def body(p_ref, dy_ref, w_ref, sc_ref, dp_ref, dw_ref, dsc_ref, xp_ref, e_ref, neg_ref):
        xp_ref[0:POOL_HALO, :] = jnp.zeros((POOL_HALO, D_POOL), F32)
        xp_ref[POOL_HALO:, :] = p_ref[...]
        e_ref[S:, :] = jnp.zeros((POOL_HALO, C), F32)
        for gi, win in enumerate(POOL_WINDOWS):
            cols = slice(gi * C, (gi + 1) * C)

            def step_a(c, carry, cols=cols, win=win, gi=gi):
                dw, dsc = carry
                r0 = pl.multiple_of(c * POOL_ROWS, POOL_ROWS)
                ext = xp_ref[pl.ds(r0, POOL_ROWS + POOL_HALO), cols]
                pooled, cnt = _pool_window_mean_minus_token(ext, r0, win)
                pb = pooled.astype(BF16)
                wv = w_ref[gi]
                dy = dy_ref[pl.ds(r0, POOL_ROWS), cols]
                dsc = dsc + jnp.sum(dy * _dot_nn(pb, wv), axis=0, keepdims=True)
                dyp = (dy * sc_ref[:, cols]).astype(BF16)
                dw = dw + _dot_tn(pb, dyp)
                dpooled = _dot_nt(dyp, wv)
                e_ref[pl.ds(r0, POOL_ROWS), :] = dpooled / cnt
                neg_ref[pl.ds(r0, POOL_ROWS), :] = -dpooled
                return dw, dsc

            dw, dsc = lax.fori_loop(0, nsteps, step_a, (jnp.zeros((C, C), F32), jnp.zeros((1, C), F32)))
            dw_ref[gi] = dw
            dsc_ref[:, cols] = dsc

            def step_b(c, carry, cols=cols, win=win):
                r0 = pl.multiple_of(c * POOL_ROWS, POOL_ROWS)
                s = e_ref[pl.ds(r0, POOL_ROWS + POOL_HALO), :]
                n = POOL_ROWS + POOL_HALO
                k = 1
                while k < win:
                    s = s + pltpu.roll(s, n - k, 0)
                    k *= 2
                du = s[:POOL_ROWS, :] + neg_ref[pl.ds(r0, POOL_ROWS), :]
                dp_ref[pl.ds(r0, POOL_ROWS), cols] = du.astype(BF16)
                return carry

            lax.fori_loop(0, nsteps, step_b, 0)

    return pl.pallas_call(
        body, name=name, grid=(1,),
        in_specs=[pl.BlockSpec((S, D_POOL), lambda i: (0, 0)),
                  pl.BlockSpec((S, D_POOL), lambda i: (0, 0)),
                  pl.BlockSpec((4, C, C), lambda i: (0, 0, 0)),
                  pl.BlockSpec((1, D_POOL), lambda i: (0, 0))],
        out_specs=[pl.BlockSpec((S, D_POOL), lambda i: (0, 0)),
                   pl.BlockSpec((4, C, C), lambda i: (0, 0, 0)),
                   pl.BlockSpec((1, D_POOL), lambda i: (0, 0))],
        out_shape=[jax.ShapeDtypeStruct((S, D_POOL), BF16), jax.ShapeDtypeStruct((4, C, C), F32),
                   jax.ShapeDtypeStruct((1, D_POOL), F32)],
        scratch_shapes=[pltpu.VMEM((S + POOL_HALO, D_POOL), F32), pltpu.VMEM((S + POOL_HALO, C), F32),
                        pltpu.VMEM((S, C), F32)],
        compiler_params=_cparams(("arbitrary",), 56),
    )(u, dcat, w_pool, scale)


GLA_ROWS = 128
U_Q_BLK, U_K_BLK = 2, 3
U_V_BLK, U_G_BLK = 2, 3
U_R_BLK = 32


def _prefix_sum_rows(x):
    n = x.shape[0]
    row = lax.broadcasted_iota(jnp.int32, x.shape, 0)
    k = 1
    while k < n:
        x = x + jnp.where(row >= k, pltpu.roll(x, k, 0), 0.0)
        k *= 2
    return x


def _suffix_sum_rows(x):
    n = x.shape[0]
    row = lax.broadcasted_iota(jnp.int32, x.shape, 0)
    k = 1
    while k < n:
        x = x + jnp.where(row < n - k, pltpu.roll(x, n - k, 0), 0.0)
        k *= 2
    return x


def _log_sigmoid(z):
    return jnp.minimum(z, 0.0) - jnp.log(1.0 + jnp.exp(-jnp.abs(z)))


def _gla_chunk_terms(la_c, q_c, k_c):
    bc = _prefix_sum_rows(la_c)
    bl = jnp.sum(la_c, axis=0, keepdims=True)
    eb = jnp.exp(bc)
    enb = jnp.exp(-bc)
    etail = jnp.exp(bl - bc)
    qd = q_c * (GLA_DK ** -0.5) * eb
    ki = k_c * enb
    kt = k_c * etail
    d = jnp.exp(bl)
    return eb, enb, etail, qd, ki, kt, d


def _gla_fwd(u, w_alpha, b_alpha, gnorm, *, name):
    S = u.shape[0]
    RB = GLA_ROWS
    ncc = RB // CHUNK
    H, DK, DV = GLA_HEADS, GLA_DK, GLA_DV

    def body(q_ref, k_ref, v_ref, go_ref, r_ref, wa_ref, ba_ref, gn_ref, y_ref, o_ref, st_ref, state):
        i = pl.program_id(0)

        @pl.when(i == 0)
        def _():
            state[...] = jnp.zeros_like(state)

        z = _dot_nn(r_ref[...].astype(BF16), wa_ref[...]) + ba_ref[...]
        la = _log_sigmoid(z) / GATE_LOGIT_NORMALIZER
        ri = lax.broadcasted_iota(jnp.int32, (CHUNK, CHUNK), 0)
        ci = lax.broadcasted_iota(jnp.int32, (CHUNK, CHUNK), 1)
        tri = ri >= ci
        gn = gn_ref[...]
        for cc in range(ncc):
            rs = slice(cc * CHUNK, (cc + 1) * CHUNK)
            for h in range(H):
                ks = slice(h * DK, (h + 1) * DK)
                vs = slice(h * DV, (h + 1) * DV)
                _, _, _, qd, ki, kt, d = _gla_chunk_terms(la[rs, ks], q_ref[rs, ks], k_ref[rs, ks])
                qdb = qd.astype(BF16)
                vb = v_ref[rs, vs].astype(BF16)
                p = jnp.where(tri, _dot_nt(qdb, ki.astype(BF16)), 0.0)
                st = state[h]
                st_ref[cc, h] = st
                o = _dot_nn(p.astype(BF16), vb) + _dot_nt(qdb, st.astype(BF16))
                state[h] = st * d + _dot_tn(vb, kt.astype(BF16))
                o_ref[rs, vs] = o
                rinv = lax.rsqrt(jnp.mean(o * o, axis=-1, keepdims=True) + EPS)
                go = go_ref[rs, vs]
                y_ref[rs, vs] = (o * rinv * gn * (go * _sigmoid(go))).astype(BF16)

    nblk = S // RB
    return pl.pallas_call(
        body, name=name, grid=(nblk,),
        in_specs=[
            pl.BlockSpec((RB, GLA_DK_TOTAL), lambda i: (i, U_Q_BLK)),
            pl.BlockSpec((RB, GLA_DK_TOTAL), lambda i: (i, U_K_BLK)),
            pl.BlockSpec((RB, D_GLA), lambda i: (i, U_V_BLK)),
            pl.BlockSpec((RB, D_GLA), lambda i: (i, U_G_BLK)),
            pl.BlockSpec((RB, LANES), lambda i: (i, U_R_BLK)),
            pl.BlockSpec((LANES, GLA_DK_TOTAL), lambda i: (0, 0)),
            pl.BlockSpec((1, GLA_DK_TOTAL), lambda i: (0, 0)),
            pl.BlockSpec((1, DV), lambda i: (0, 0)),
        ],
        out_specs=[
            pl.BlockSpec((RB, D_GLA), lambda i: (i, 0)),
            pl.BlockSpec((RB, D_GLA), lambda i: (i, 0)),
            pl.BlockSpec((ncc, H, DV, DK), lambda i: (i, 0, 0, 0)),
        ],
        out_shape=[jax.ShapeDtypeStruct((S, D_GLA), BF16), jax.ShapeDtypeStruct((S, D_GLA), F32),
                   jax.ShapeDtypeStruct((S // CHUNK, H, DV, DK), F32)],
        scratch_shapes=[pltpu.VMEM((H, DV, DK), F32)],
        compiler_params=_cparams(("arbitrary",), 32),
    )(u, u, u, u, u, w_alpha, b_alpha, gnorm)


def _gla_bwd(u, o, states, dcat, w_alpha, b_alpha, gnorm, *, name):
    S = u.shape[0]
    RB = GLA_ROWS
    ncc = RB // CHUNK
    H, DK, DV = GLA_HEADS, GLA_DK, GLA_DV
    nblk = S // RB

    def body(q_ref, k_ref, v_ref, go_ref, r_ref, o_ref, st_ref, dy_ref, wa_ref, ba_ref, gn_ref,
             dq_ref, dk_ref, dv_ref, dgo_ref, dr_ref, dwa_ref, dba_ref, dgn_ref, dstate, dz_ref):
        i = pl.program_id(0)

        @pl.when(i == 0)
        def _():
            dstate[...] = jnp.zeros_like(dstate)
            dwa_ref[...] = jnp.zeros_like(dwa_ref)
            dba_ref[...] = jnp.zeros_like(dba_ref)
            dgn_ref[...] = jnp.zeros_like(dgn_ref)

        rb = r_ref[...].astype(BF16)
        wa = wa_ref[...]
        z = _dot_nn(rb, wa) + ba_ref[...]
        la = _log_sigmoid(z) / GATE_LOGIT_NORMALIZER
        ri = lax.broadcasted_iota(jnp.int32, (CHUNK, CHUNK), 0)
        ci = lax.broadcasted_iota(jnp.int32, (CHUNK, CHUNK), 1)
        tri = ri >= ci
        last_row = lax.broadcasted_iota(jnp.int32, (CHUNK, DK), 0) == CHUNK - 1
        gn = gn_ref[...]
        dgn = jnp.zeros((1, DV), F32)
        for cc in reversed(range(ncc)):
            rs = slice(cc * CHUNK, (cc + 1) * CHUNK)
            for h in range(H):
                ks = slice(h * DK, (h + 1) * DK)
                vs = slice(h * DV, (h + 1) * DV)
                eb, enb, etail, qd, ki, kt, d = _gla_chunk_terms(la[rs, ks], q_ref[rs, ks], k_ref[rs, ks])
                qdb, kib, ktb = qd.astype(BF16), ki.astype(BF16), kt.astype(BF16)
                vb = v_ref[rs, vs].astype(BF16)
                p = jnp.where(tri, _dot_nt(qdb, kib), 0.0)
                ov = o_ref[rs, vs]
                go = go_ref[rs, vs]
                dy = dy_ref[rs, vs]
                rinv = lax.rsqrt(jnp.mean(ov * ov, axis=-1, keepdims=True) + EPS)
                oh = ov * rinv
                sg = _sigmoid(go)
                dgo_ref[rs, vs] = (dy * (oh * gn) * (sg * (1.0 + go * (1.0 - sg)))).astype(BF16)
                don = dy * (go * sg)
                dgn = dgn + jnp.sum(don * oh, axis=0, keepdims=True)
                doh = don * gn
                do = rinv * (doh - oh * jnp.mean(doh * oh, axis=-1, keepdims=True))
                dob = do.astype(BF16)
                st = st_ref[cc, h]
                dst = dstate[h]
                stb, dstb = st.astype(BF16), dst.astype(BF16)
                dp = jnp.where(tri, _dot_nt(dob, vb), 0.0).astype(BF16)
                dv_ref[rs, vs] = (_dot_tn(p.astype(BF16), dob) + _dot_nt(ktb, dstb)).astype(BF16)
                dqd = _dot_nn(dp, kib) + _dot_nn(dob, stb)
                dki = _dot_tn(dp, qdb)
                dkt = _dot_nn(vb, dstb)
                dd = jnp.sum(dst * st, axis=0, keepdims=True)
                dstate[h] = dst * d + _dot_tn(dob, qdb)
                dq_ref[rs, ks] = (dqd * eb * (DK ** -0.5)).astype(BF16)
                dk_ref[rs, ks] = (dki * enb + dkt * etail).astype(BF16)
                dbl = jnp.sum(dkt * kt, axis=0, keepdims=True) + dd * d
                dbc = dqd * qd - dki * ki - dkt * kt
                dbc = dbc + jnp.where(last_row, dbl, 0.0)
                dla = _suffix_sum_rows(dbc)
                dz_ref[rs, ks] = dla * (1.0 / GATE_LOGIT_NORMALIZER) * (1.0 - _sigmoid(z[rs, ks]))
        dz = dz_ref[...]
        dzb = dz.astype(BF16)
        dr_ref[...] = _dot_nt(dzb, wa).astype(BF16)
        dwa_ref[...] += _dot_tn(rb, dzb)
        dba_ref[...] += jnp.sum(dz, axis=0, keepdims=True)
        dgn_ref[...] += dgn

    def rev(blk):
        return lambda i: (nblk - 1 - i, blk)

    return pl.pallas_call(
        body, name=name, grid=(nblk,),
        in_specs=[
            pl.BlockSpec((RB, GLA_DK_TOTAL), rev(U_Q_BLK)),
            pl.BlockSpec((RB, GLA_DK_TOTAL), rev(U_K_BLK)),
            pl.BlockSpec((RB, D_GLA), rev(U_V_BLK)),
            pl.BlockSpec((RB, D_GLA), rev(U_G_BLK)),
            pl.BlockSpec((RB, LANES), rev(U_R_BLK)),
            pl.BlockSpec((RB, D_GLA), rev(0)),
            pl.BlockSpec((ncc, H, DV, DK), lambda i: (nblk - 1 - i, 0, 0, 0)),
            pl.BlockSpec((RB, D_GLA), rev(1)),
            pl.BlockSpec((LANES, GLA_DK_TOTAL), lambda i: (0, 0)),
            pl.BlockSpec((1, GLA_DK_TOTAL), lambda i: (0, 0)),
            pl.BlockSpec((1, DV), lambda i: (0, 0)),
        ],
        out_specs=[
            pl.BlockSpec((RB, GLA_DK_TOTAL), rev(0)),
            pl.BlockSpec((RB, GLA_DK_TOTAL), rev(0)),
            pl.BlockSpec((RB, D_GLA), rev(0)),
            pl.BlockSpec((RB, D_GLA), rev(0)),
            pl.BlockSpec((RB, LANES), rev(0)),
            pl.BlockSpec((LANES, GLA_DK_TOTAL), lambda i: (0, 0)),
            pl.BlockSpec((1, GLA_DK_TOTAL), lambda i: (0, 0)),
            pl.BlockSpec((1, DV), lambda i: (0, 0)),
        ],
        out_shape=[
            jax.ShapeDtypeStruct((S, GLA_DK_TOTAL), BF16), jax.ShapeDtypeStruct((S, GLA_DK_TOTAL), BF16),
            jax.ShapeDtypeStruct((S, D_GLA), BF16), jax.ShapeDtypeStruct((S, D_GLA), BF16),
            jax.ShapeDtypeStruct((S, LANES), BF16),
            jax.ShapeDtypeStruct((LANES, GLA_DK_TOTAL), F32), jax.ShapeDtypeStruct((1, GLA_DK_TOTAL), F32),
            jax.ShapeDtypeStruct((1, DV), F32),
        ],
        scratch_shapes=[pltpu.VMEM((H, DV, DK), F32), pltpu.VMEM((RB, GLA_DK_TOTAL), F32)],
        compiler_params=_cparams(("arbitrary",), 32),
    )(u, u, u, u, u, o, states, dcat, w_alpha, b_alpha, gnorm)


def _row_tile(rows, cols, itemsize, budget=2 * 1024 * 1024):
    if rows * cols * itemsize <= budget or rows % 16:
        return rows
    best = 16
    for t in range(16, rows + 1, 16):
        if rows % t == 0 and t * cols * itemsize <= budget:
            best = t
    return best


def _adamw(w, g, m, v, *, name):
    R, C = w.shape
    tr = _row_tile(R, C, 4, budget=1024 * 1024)

    def body(w_ref, g_ref, m_ref, v_ref, d_ref, nm_ref, nv_ref):
        gv = g_ref[...]
        mn = ADAM_B1 * m_ref[...] + (1.0 - ADAM_B1) * gv
        vn = ADAM_B2 * v_ref[...] + (1.0 - ADAM_B2) * jnp.square(gv)
        m_hat = mn / (1.0 - ADAM_B1 ** ADAM_STEP)
        v_hat = vn / (1.0 - ADAM_B2 ** ADAM_STEP)
        d_ref[...] = -ADAM_LR * (m_hat / (jnp.sqrt(v_hat) + ADAM_EPS) + ADAM_WD * w_ref[...])
        nm_ref[...] = mn
        nv_ref[...] = vn

    spec = pl.BlockSpec((tr, C), lambda i: (i, 0))
    shp = jax.ShapeDtypeStruct((R, C), F32)
    return pl.pallas_call(
        body, name=name, grid=(R // tr,), in_specs=[spec] * 4, out_specs=[spec] * 3, out_shape=[shp] * 3,
        compiler_params=_cparams(("parallel",), 32),
    )(w, g, m, v)


def _pair_sum(g4, recv, c_idx, *, name):
    ns, _, R2, C = g4.shape
    tr = _row_tile(R2, C, 2)

    def body(c_ref, g_ref, r_ref, o_ref):
        o_ref[...] = (g_ref[...].astype(F32) + r_ref[...].astype(F32)).astype(BF16)

    return pl.pallas_call(
        body, name=name,
        grid_spec=pltpu.PrefetchScalarGridSpec(
            num_scalar_prefetch=1, grid=(ns, R2 // tr),
            in_specs=[pl.BlockSpec((None, None, tr, C), lambda s, i, c: (s, c[0], i, 0)),
                      pl.BlockSpec((None, tr, C), lambda s, i, c: (s, i, 0))],
            out_specs=pl.BlockSpec((None, tr, C), lambda s, i, c: (s, i, 0)),
        ),
        out_shape=jax.ShapeDtypeStruct((ns, R2, C), BF16),
        compiler_params=_cparams(("parallel", "parallel"), 32),
    )(c_idx, g4, recv)


def _chip_sum(part, recv, sc_idx, *, name):
    _, R2, C = part.shape
    tr = _row_tile(R2, C, 4)
    nblk = R2 // tr

    def body(s_ref, p_ref, r_ref, o_ref):
        acc = p_ref[...].astype(F32)
        for j in range(N_CHIPS - 1):
            acc = acc + r_ref[j].astype(F32)
        o_ref[...] = acc

    return pl.pallas_call(
        body, name=name,
        grid_spec=pltpu.PrefetchScalarGridSpec(
            num_scalar_prefetch=1, grid=(nblk,),
            in_specs=[pl.BlockSpec((None, tr, C), lambda i, s: (s[0], i, 0)),
                      pl.BlockSpec((N_CHIPS - 1, tr, C), lambda i, s: (0, i, 0))],
            out_specs=pl.BlockSpec((tr, C), lambda i, s: (s[1] * nblk + i, 0)),
        ),
        out_shape=jax.ShapeDtypeStruct((2 * R2, C), F32),
        compiler_params=_cparams(("parallel",), 32),
    )(sc_idx, part, recv)


def _cast_into_slot(w, sc_idx, dtype, *, name):
    R, C = w.shape
    tr = _row_tile(R, C, 4)

    def body(s_ref, w_ref, o_ref):
        o_ref[...] = w_ref[...].astype(dtype)

    return pl.pallas_call(
        body, name=name,
        grid_spec=pltpu.PrefetchScalarGridSpec(
            num_scalar_prefetch=1, grid=(R // tr,),
            in_specs=[pl.BlockSpec((tr, C), lambda i, s: (i, 0))],
            out_specs=pl.BlockSpec((None, tr, C), lambda i, s: (s[0], i, 0)),
        ),
        out_shape=jax.ShapeDtypeStruct((N_CHIPS, R, C), dtype),
        compiler_params=_cparams(("parallel",), 32),
    )(sc_idx, w)


def _slab_sum(slabs, *, name):
    n, M, C = slabs.shape

    def body(x_ref, o_ref):
        acc = x_ref[0]
        for d in range(1, n):
            acc = acc + x_ref[d]
        o_ref[...] = acc

    return pl.pallas_call(
        body, name=name, out_shape=jax.ShapeDtypeStruct((M, C), F32),
    )(slabs)


def _mesh_position():
    x, y, c = lax.axis_index("x"), lax.axis_index("y"), lax.axis_index("c")
    other_chips = [(1 - x, y), (x, 1 - y), (1 - x, 1 - y)]
    return x, y, c, other_chips


ANY = pl.BlockSpec(memory_space=pl.ANY)


def _all_gather_weights(bufs):
    K = len(bufs)
    per = 2 * (N_CHIPS - 1)

    def body(*refs):
        outs = refs[K:2 * K]
        send_sems, recv_sems = refs[2 * K:]
        x, y, c, chips = _mesh_position()
        s = 2 * x + y
        sibling = (x, y, 1 - c)

        def region(k, slot, half):
            hr = bufs[k].shape[1] // 2
            return outs[k].at[slot, pl.ds(half * hr, hr), :]

        def copy(k, n, src, dst, to):
            return pltpu.make_async_remote_copy(
                src_ref=src, dst_ref=dst, send_sem=send_sems.at[k * per + n], recv_sem=recv_sems.at[k * per + n],
                device_id=to, device_id_type=MESH)

        first = []
        for k in range(K):
            for j, chip in enumerate(chips):
                first.append(copy(k, j, region(k, s, c), region(k, s, c), (*chip, c)))
        for cp in first:
            cp.start()
        passed = []
        for k in range(K):
            for j, chip in enumerate(chips):
                slot = 2 * chip[0] + chip[1]
                got = region(k, slot, c)
                copy(k, j, got, got, (*chip, c)).wait_recv()
                fwd = copy(k, (N_CHIPS - 1) + j, got, got, sibling)
                fwd.start()
                passed.append(fwd)
        for k in range(K):
            for j, chip in enumerate(chips):
                slot = 2 * chip[0] + chip[1]
                got = region(k, slot, 1 - c)
                copy(k, (N_CHIPS - 1) + j, got, got, sibling).wait_recv()
        for cp in first + passed:
            cp.wait_send()

    return pl.pallas_call(
        body, name="gather_weights",
        in_specs=[ANY] * K, out_specs=[ANY] * K,
        out_shape=[jax.ShapeDtypeStruct(a.shape, a.dtype) for a in bufs],
        input_output_aliases={k: k for k in range(K)},
        scratch_shapes=[pltpu.SemaphoreType.DMA((K * per,)), pltpu.SemaphoreType.DMA((K * per,))],
    )(*bufs)


def _swap_halves(grads4):
    K = len(grads4)

    def body(*refs):
        ins, outs = refs[:K], refs[K:2 * K]
        send_sems, recv_sems = refs[2 * K:]
        x, y, c, _ = _mesh_position()
        copies = []
        for k in range(K):
            ns = grads4[k].shape[0]
            cp = pltpu.make_async_remote_copy(
                src_ref=ins[k].at[pl.ds(0, ns), 1 - c], dst_ref=outs[k],
                send_sem=send_sems.at[k], recv_sem=recv_sems.at[k], device_id=(x, y, 1 - c), device_id_type=MESH)
            cp.start()
            copies.append(cp)
        for cp in copies:
            cp.wait()

    return pl.pallas_call(
        body, name="swap_grad_halves",
        in_specs=[ANY] * K, out_specs=[ANY] * K,
        out_shape=[jax.ShapeDtypeStruct((a.shape[0],) + a.shape[2:], a.dtype) for a in grads4],
        scratch_shapes=[pltpu.SemaphoreType.DMA((K,)), pltpu.SemaphoreType.DMA((K,))],
    )(*grads4)


def _exchange_chip_partials(parts):
    K = len(parts)
    per = N_CHIPS - 1

    def body(*refs):
        ins, outs = refs[:K], refs[K:2 * K]
        send_sems, recv_sems = refs[2 * K:]
        x, y, c, chips = _mesh_position()
        copies = []
        for k in range(K):
            for j, chip in enumerate(chips):
                cp = pltpu.make_async_remote_copy(
                    src_ref=ins[k].at[2 * chip[0] + chip[1]], dst_ref=outs[k].at[j],
                    send_sem=send_sems.at[k * per + j], recv_sem=recv_sems.at[k * per + j],
                    device_id=(*chip, c), device_id_type=MESH)
                cp.start()
                copies.append(cp)
        for cp in copies:
            cp.wait()

    return pl.pallas_call(
        body, name="exchange_chip_partials",
        in_specs=[ANY] * K, out_specs=[ANY] * K,
        out_shape=[jax.ShapeDtypeStruct((N_CHIPS - 1,) + a.shape[1:], a.dtype) for a in parts],
        scratch_shapes=[pltpu.SemaphoreType.DMA((K * per,)), pltpu.SemaphoreType.DMA((K * per,))],
    )(*parts)


def _join_halves(bufs):
    K = len(bufs)

    def body(*refs):
        outs = refs[K:2 * K]
        send_sems, recv_sems = refs[2 * K:]
        x, y, c, _ = _mesh_position()
        copies = []
        for k in range(K):
            r2 = bufs[k].shape[0] // 2
            mine = outs[k].at[pl.ds(c * r2, r2), :]
            cp = pltpu.make_async_remote_copy(
                src_ref=mine, dst_ref=mine, send_sem=send_sems.at[k], recv_sem=recv_sems.at[k],
                device_id=(x, y, 1 - c), device_id_type=MESH)
            cp.start()
            copies.append(cp)
        for cp in copies:
            cp.wait()

    return pl.pallas_call(
        body, name="join_grad_halves",
        in_specs=[ANY] * K, out_specs=[ANY] * K,
        out_shape=[jax.ShapeDtypeStruct(a.shape, a.dtype) for a in bufs],
        input_output_aliases={k: k for k in range(K)},
        scratch_shapes=[pltpu.SemaphoreType.DMA((K,)), pltpu.SemaphoreType.DMA((K,))],
    )(*bufs)


def _all_gather_slab(slab):
    m_per, n = slab.shape

    def body(x_ref, out_ref, send_sems, recv_sems, local_sem):
        x, y, c, chips = _mesh_position()
        me, sibling = (x, y, c), (x, y, 1 - c)

        def rows(px, py, pc):
            return out_ref.at[pl.ds((4 * px + 2 * py + pc) * m_per, m_per), :]

        def copy(k, block, to, src=None):
            return pltpu.make_async_remote_copy(
                src_ref=rows(*block) if src is None else src, dst_ref=rows(*block),
                send_sem=send_sems.at[k], recv_sem=recv_sems.at[k], device_id=to, device_id_type=MESH)

        mine = pltpu.make_async_copy(x_ref, rows(*me), local_sem)
        mine.start()
        first = [copy(0, me, sibling, src=x_ref)]
        first += [copy(1 + j, me, (*chip, c), src=x_ref) for j, chip in enumerate(chips)]
        for cp in first:
            cp.start()
        passed = [copy(4 + j, (*chip, c), sibling) for j, chip in enumerate(chips)]
        for j, chip in enumerate(chips):
            copy(1 + j, (*chip, c), me).wait_recv()
            passed[j].start()
        copy(0, sibling, me).wait_recv()
        for j, chip in enumerate(chips):
            copy(4 + j, (*chip, 1 - c), me).wait_recv()
        for cp in first + passed:
            cp.wait_send()
        mine.wait()

    return pl.pallas_call(
        body, name="gather_small_grads",
        out_shape=jax.ShapeDtypeStruct((N_DEV * m_per, n), slab.dtype),
        in_specs=[pl.BlockSpec(memory_space=pltpu.VMEM)],
        out_specs=pl.BlockSpec(memory_space=pltpu.VMEM),
        scratch_shapes=[pltpu.SemaphoreType.DMA((7,)), pltpu.SemaphoreType.DMA((7,)), pltpu.SemaphoreType.DMA],
    )(slab)


def _ffn_forward(h, g, w_in, w_out, tag):
    n = _rms_fwd(h, g, name=f"{tag}_norm")
    gu, a = _ffn_up(n, w_in, name=f"{tag}_up")
    h_out = _mm_nn(a, w_out, h, 0.5, tm=512, tn=D_MODEL, tk=COL_TILE, name=f"{tag}_down")
    return h_out, (n, gu, a)


def _ffn_backward(dh, dhb, saved, h_in, g, w_in, w_out, tag):
    n, gu, a = saved
    S = n.shape[0]
    ns, D, cs = w_in.shape
    F = w_out.shape[0]
    dgu = _ffn_bwd_act(dhb, w_out, gu, name=f"{tag}_bwd_act")
    to = 512
    dw_out = _mm_tn(
        a, dhb, grid=(F // to,),
        a_spec=pl.BlockSpec((S, to), lambda j: (0, j)), b_spec=pl.BlockSpec((S, D), lambda j: (0, 0)),
        out_spec=pl.BlockSpec((to, D), lambda j: (j, 0)), out_shape=jax.ShapeDtypeStruct((F, D), BF16),
        scale=0.5, name=f"{tag}_dw_out")
    ti = MXU_TILE
    per_g, per_s = F // ti, cs // ti
    dw_in = _mm_tn(
        n.T, dgu, grid=(2 * F // ti,), a_is_transposed=True,
        a_spec=pl.BlockSpec((D, S), lambda j: (0, 0)),
        b_spec=pl.BlockSpec((None, S, ti), lambda j: (lax.div(j, per_g), 0, lax.rem(j, per_g))),
        out_spec=pl.BlockSpec((None, D, ti), lambda j: (lax.div(j, per_s), 0, lax.rem(j, per_s))),
        out_shape=jax.ShapeDtypeStruct((ns, D, cs), BF16), scale=1.0, name=f"{tag}_dw_in")
    dh_in, dhb_in, dg = _mm_nt_rmsbwd(dgu, w_in, h_in, dh, g, tk=COL_TILE, name=f"{tag}_dx")
    return dh_in, dhb_in, dg, dw_in, dw_out


def kernel(x, ffn1_norm, ffn1_w_in, ffn1_w_out, mix_norm, w_in_mix, w_pool, pool_scale, w_alpha, b_alpha, gla_norm, w_out_mix, ffn2_norm, ffn2_w_in, ffn2_w_out, final_norm, loss_target, m_ffn1_norm, m_ffn1_w_in, m_ffn1_w_out, m_mix_norm, m_w_in_mix, m_w_pool, m_pool_scale, m_w_alpha, m_b_alpha, m_gla_norm, m_w_out_mix, m_ffn2_norm, m_ffn2_w_in, m_ffn2_w_out, m_final_norm, v_ffn1_norm, v_ffn1_w_in, v_ffn1_w_out, v_mix_norm, v_w_in_mix, v_w_pool, v_pool_scale, v_w_alpha, v_b_alpha, v_gla_norm, v_w_out_mix, v_ffn2_norm, v_ffn2_w_in, v_ffn2_w_out, v_final_norm):
    names = ["ffn1_norm", "ffn1_w_in", "ffn1_w_out", "mix_norm", "w_in_mix", "w_pool", "pool_scale", "w_alpha",
             "b_alpha", "gla_norm", "w_out_mix", "ffn2_norm", "ffn2_w_in", "ffn2_w_out", "final_norm"]
    weights = dict(zip(names, [ffn1_norm, ffn1_w_in, ffn1_w_out, mix_norm, w_in_mix, w_pool, pool_scale, w_alpha,
                               b_alpha, gla_norm, w_out_mix, ffn2_norm, ffn2_w_in, ffn2_w_out, final_norm]))
    moms = dict(zip(names, [m_ffn1_norm, m_ffn1_w_in, m_ffn1_w_out, m_mix_norm, m_w_in_mix, m_w_pool, m_pool_scale,
                            m_w_alpha, m_b_alpha, m_gla_norm, m_w_out_mix, m_ffn2_norm, m_ffn2_w_in, m_ffn2_w_out,
                            m_final_norm]))
    vels = dict(zip(names, [v_ffn1_norm, v_ffn1_w_in, v_ffn1_w_out, v_mix_norm, v_w_in_mix, v_w_pool, v_pool_scale,
                            v_w_alpha, v_b_alpha, v_gla_norm, v_w_out_mix, v_ffn2_norm, v_ffn2_w_in, v_ffn2_w_out,
                            v_final_norm]))
    xi, yi, ci = lax.axis_index("x"), lax.axis_index("y"), lax.axis_index("c")
    chip = 2 * xi + yi
    c_idx = jnp.reshape(ci, (1,)).astype(jnp.int32)
    sc_idx = jnp.stack([chip, ci]).astype(jnp.int32)

    def flat2d(a):
        return a.reshape(-1, a.shape[-1])

    sharded = ["ffn1_w_in", "ffn1_w_out", "w_in_mix", "w_out_mix", "ffn2_w_in", "ffn2_w_out", "w_pool", "w_alpha"]
    bufs = [_cast_into_slot(flat2d(weights[n]), sc_idx, F32 if n == "w_alpha" else BF16, name=f"cast_{n}")
            for n in sharded]
    gw = dict(zip(sharded, _all_gather_weights(bufs)))
    small_params = dict(g1=ffn1_norm, gm=mix_norm, g2=ffn2_norm, gf=final_norm.reshape(1, D_MODEL),
                        pool_scale=pool_scale, b_alpha=b_alpha, gla_norm=gla_norm)
    loss_blk, dx, full, small = _forward_backward(x[0], loss_target[0], gw, small_params)
    loss = lax.psum(loss_blk[0, 0], ("x", "y", "c"))
    big = list(full)

    g4 = [full[n].reshape(N_CHIPS, 2, full[n].shape[1] // 2, full[n].shape[2]) for n in big]
    from_sibling = _swap_halves(g4)
    pair = [_pair_sum(a, b, c_idx, name=f"pair_sum_{n}") for n, a, b in zip(big, g4, from_sibling)]
    from_chips = _exchange_chip_partials(pair)
    halves = [_chip_sum(a, b, sc_idx, name=f"chip_sum_{n}") for n, a, b in zip(big, pair, from_chips)]
    grads = dict(zip(big, _join_halves(halves)))

    small_names = ["ffn1_norm", "mix_norm", "ffn2_norm", "final_norm", "pool_scale", "b_alpha", "gla_norm", "w_alpha"]
    rows = [a.size // LANES for a in small]
    slab = jnp.concatenate([a.reshape(-1, LANES) for a in small], axis=0)
    pad = -slab.shape[0] % 8
    slab = jnp.pad(slab, ((0, pad), (0, 0)))
    gathered = _all_gather_slab(slab).reshape(N_DEV, slab.shape[0], LANES)
    total = _slab_sum(gathered, name="sum_small_grads")
    off = 0
    for n, a, r in zip(small_names, small, rows):
        grads[n] = total[off:off + r].reshape(a.shape)
        off += r
    grads["w_alpha"] = lax.dynamic_slice_in_dim(grads["w_alpha"], chip * (GLA_DK_TOTAL // N_CHIPS),
                                                GLA_DK_TOTAL // N_CHIPS, axis=1)

    out_g, out_d, out_m, out_v = [], [], [], []
    for n in names:
        w = weights[n]
        g = grads[n].reshape(w.shape)
        w2 = flat2d(w) if w.ndim > 1 else w.reshape(1, -1)
        d, nm, nv = _adamw(w2, g.reshape(w2.shape), moms[n].reshape(w2.shape), vels[n].reshape(w2.shape),
                           name=f"adamw_{n}")
        out_g.append(g)
        out_d.append(d.reshape(w.shape))
        out_m.append(nm.reshape(w.shape))
        out_v.append(nv.reshape(w.shape))
    return (loss, dx[None], *out_g, *out_d, *out_m, *out_v)


def _forward_backward(h0, target, gw, sp):
    g1, gm, g2, gf = sp["g1"], sp["gm"], sp["g2"], sp["gf"]
    pool_scale, b_alpha, gla_norm = sp["pool_scale"], sp["b_alpha"], sp["gla_norm"]
    w1_in, w2_in = gw["ffn1_w_in"], gw["ffn2_w_in"]
    w1_out = gw["ffn1_w_out"].reshape(D_FF, D_MODEL)
    w2_out = gw["ffn2_w_out"].reshape(D_FF, D_MODEL)
    w_omix = gw["w_out_mix"].reshape(D_MODEL, D_MODEL)
    cs_mix = D_IN // N_CHIPS
    w_mix = jnp.concatenate([gw["w_in_mix"][t] for t in range(N_CHIPS)], axis=1)
    w_mix = jnp.pad(w_mix, ((0, 0), (0, D_IN_PAD - D_IN)))[None]
    wp = gw["w_pool"].reshape(N_CHIPS, 4, POOL_GROUP_DIM // N_CHIPS, POOL_GROUP_DIM)
    wp = wp.transpose(1, 0, 2, 3).reshape(4, POOL_GROUP_DIM, POOL_GROUP_DIM)
    wa = gw["w_alpha"].transpose(1, 0, 2).reshape(GLA_GATE_RANK, GLA_DK_TOTAL)
    wa = jnp.pad(wa, ((0, LANES - GLA_GATE_RANK), (0, 0))).astype(BF16)

    h1, saved1 = _ffn_forward(h0, g1, w1_in, w1_out, "ffn1")
    n_mix = _rms_fwd(h1, gm, name="mix_norm")
    u = _mm_nn(n_mix, w_mix[0], None, 1.0, tm=512, tn=1408, tk=D_MODEL, name="mix_in")
    y_pool = _pool_fwd(u, wp, pool_scale, name="pool_fwd")
    y_gla, o_gla, states = _gla_fwd(u, wa, b_alpha, gla_norm, name="gla_fwd")
    cat = jnp.concatenate([y_pool, y_gla], axis=1)
    h2 = _mm_nn(cat, w_omix, h1, 1.0, tm=512, tn=D_MODEL, tk=1024, name="mix_out")
    h3, saved2 = _ffn_forward(h2, g2, w2_in, w2_out, "ffn2")
    loss_blk, dh3, dh3b, d_gf = _final_loss(h3, gf, target, name="final_loss")

    dh2, dh2b, d_g2, dw2_in, dw2_out = _ffn_backward(dh3, dh3b, saved2, h2, g2, w2_in, w2_out, "ffn2")
    S = h0.shape[0]
    dcat = _mm_nt(dh2b, w_omix, tm=512, tn=1024, name="mix_out_dx")
    dw_omix = _mm_tn(
        cat, dh2b, grid=(4,),
        a_spec=pl.BlockSpec((S, 512), lambda j: (0, j)), b_spec=pl.BlockSpec((S, D_MODEL), lambda j: (0, 0)),
        out_spec=pl.BlockSpec((512, D_MODEL), lambda j: (j, 0)),
        out_shape=jax.ShapeDtypeStruct((D_MODEL, D_MODEL), BF16), scale=1.0, name="mix_out_dw")
    dp, dw_pool, d_pscale = _pool_bwd(u, dcat, wp, pool_scale, name="pool_bwd")
    dq, dk, dv, dgo, dr, d_wa, d_ba, d_gn = _gla_bwd(u, o_gla, states, dcat, wa, b_alpha, gla_norm, name="gla_bwd")
    du = jnp.concatenate([dp, dq, dk, dv, dgo, dr], axis=1)[None]
    tn_mix = COL_TILE
    dw_mix = _mm_tn(
        n_mix.T, du, grid=(2, D_IN_PAD // tn_mix), a_is_transposed=True,
        a_spec=pl.BlockSpec((D_MODEL // 2, S), lambda i, j: (i, 0)),
        b_spec=pl.BlockSpec((None, S, tn_mix), lambda i, j: (0, 0, j)),
        out_spec=pl.BlockSpec((D_MODEL // 2, tn_mix), lambda i, j: (i, j)),
        out_shape=jax.ShapeDtypeStruct((D_MODEL, D_IN_PAD), BF16), scale=1.0, name="mix_in_dw")
    dh1, dh1b, d_gm = _mm_nt_rmsbwd(du, w_mix, h1, dh2, gm, tk=tn_mix, name="mix_in_dx")
    dx, _, d_g1, dw1_in, dw1_out = _ffn_backward(dh1, dh1b, saved1, h0, g1, w1_in, w1_out, "ffn1")

    dw_mix_s = dw_mix[:, :D_IN].reshape(D_MODEL, N_CHIPS, cs_mix).transpose(1, 0, 2)
    dw_pool_s = dw_pool.reshape(4, N_CHIPS, POOL_GROUP_DIM // N_CHIPS, POOL_GROUP_DIM).transpose(1, 0, 2, 3)
    dw_pool_s = dw_pool_s.reshape(N_CHIPS, POOL_GROUP_DIM, POOL_GROUP_DIM).astype(BF16)
    full = {
        "ffn1_w_in": dw1_in,
        "ffn1_w_out": dw1_out.reshape(N_CHIPS, D_FF // N_CHIPS, D_MODEL),
        "w_in_mix": dw_mix_s,
        "w_out_mix": dw_omix.reshape(N_CHIPS, D_MODEL // N_CHIPS, D_MODEL),
        "ffn2_w_in": dw2_in,
        "ffn2_w_out": dw2_out.reshape(N_CHIPS, D_FF // N_CHIPS, D_MODEL),
        "w_pool": dw_pool_s,
    }
    small = [d_g1, d_gm, d_g2, d_gf, d_pscale, d_ba, d_gn, d_wa[:GLA_GATE_RANK]]
    return loss_blk, dx, full, small
```

```python
import functools

import jax
import jax.numpy as jnp
from jax import lax
from jax.experimental import pallas as pl
from jax.experimental.pallas import tpu as pltpu

F32 = jnp.float32
BF16 = jnp.bfloat16
MESH = pl.DeviceIdType.MESH

D_MODEL = 2048
D_FF = 5632
D_POOL = 1024
POOL_WINDOWS = (2, 4, 8, 16)
POOL_GROUP_DIM = 256
D_GLA = 1024
GLA_HEADS = 4
GLA_DV = 256
GLA_DK = 128
GLA_DK_TOTAL = 512
GLA_GATE_RANK = 16
GATE_LOGIT_NORMALIZER = 16.0
CHUNK = 64
D_IN = 4112
D_IN_PAD = 4224
EPS = 1e-6

ADAM_LR = 0.001
ADAM_B1 = 0.9
ADAM_B2 = 0.999
ADAM_EPS = 1e-08
ADAM_WD = 0.01
ADAM_STEP = 10

N_CHIPS = 4
N_DEV = 8
V7X_VMEM_BYTES = 64 * 1024 * 1024
LANES = 128
MXU_TILE = 256
COL_TILE = 1408


def _cparams(semantics, vmem_mb):
    assert vmem_mb * 1024 * 1024 < V7X_VMEM_BYTES
    return pltpu.CompilerParams(dimension_semantics=semantics, vmem_limit_bytes=vmem_mb * 1024 * 1024)


def _dot_nn(a, b):
    return jnp.dot(a, b, preferred_element_type=F32)


def _dot_nt(a, b):
    return lax.dot_general(a, b, (((1,), (1,)), ((), ())), preferred_element_type=F32)


def _dot_tn(a, b):
    return lax.dot_general(a, b, (((0,), (0,)), ((), ())), preferred_element_type=F32)


def _sigmoid(x):
    return 1.0 / (1.0 + jnp.exp(-x))


def _rms_fwd(x, g, *, name):
    S, D = x.shape
    tm = 256

    def body(x_ref, g_ref, o_ref):
        xv = x_ref[...]
        r = lax.rsqrt(jnp.mean(xv * xv, axis=-1, keepdims=True) + EPS)
        o_ref[...] = (xv * r * g_ref[...]).astype(BF16)

    return pl.pallas_call(
        body, name=name, grid=(S // tm,),
        in_specs=[pl.BlockSpec((tm, D), lambda i: (i, 0)), pl.BlockSpec((1, D), lambda i: (0, 0))],
        out_specs=pl.BlockSpec((tm, D), lambda i: (i, 0)),
        out_shape=jax.ShapeDtypeStruct((S, D), BF16),
        compiler_params=_cparams(("parallel",), 32),
    )(x, g)


def _ffn_up(n, w_in, *, name):
    S, D = n.shape
    ns, _, cs = w_in.shape
    half = ns // 2
    F = cs * half
    tm, tn = 256, COL_TILE
    nb = cs // tn

    def body(n_ref, wg_ref, wu_ref, gu_ref, a_ref):
        nv = n_ref[...]
        g = _dot_nn(nv, wg_ref[...])
        u = _dot_nn(nv, wu_ref[...])
        gu_ref[0] = g.astype(BF16)
        gu_ref[1] = u.astype(BF16)
        a_ref[...] = (g * _sigmoid(g) * u).astype(BF16)

    return pl.pallas_call(
        body, name=name, grid=(F // tn, S // tm),
        in_specs=[
            pl.BlockSpec((tm, D), lambda j, i: (i, 0)),
            pl.BlockSpec((None, D, tn), lambda j, i: (lax.div(j, nb), 0, lax.rem(j, nb))),
            pl.BlockSpec((None, D, tn), lambda j, i: (half + lax.div(j, nb), 0, lax.rem(j, nb))),
        ],
        out_specs=[
            pl.BlockSpec((2, tm, tn), lambda j, i: (0, i, j)),
            pl.BlockSpec((tm, tn), lambda j, i: (i, j)),
        ],
        out_shape=[jax.ShapeDtypeStruct((2, S, F), BF16), jax.ShapeDtypeStruct((S, F), BF16)],
        compiler_params=_cparams(("parallel", "parallel"), 48),
    )(n, w_in, w_in)


def _mm_nn(a, b, resid, scale, *, tm, tn, tk, name):
    S, K = a.shape
    N = b.shape[1]
    nk = K // tk

    def body(*refs):
        if resid is None:
            a_ref, b_ref, o_ref, acc_ref = refs
            r_ref = None
        else:
            a_ref, b_ref, r_ref, o_ref, acc_ref = refs
        k = pl.program_id(2)

        @pl.when(k == 0)
        def _():
            acc_ref[...] = jnp.zeros_like(acc_ref)

        acc_ref[...] += _dot_nn(a_ref[...], b_ref[...])

        @pl.when(k == nk - 1)
        def _():
            out = acc_ref[...] * scale
            if r_ref is not None:
                out = r_ref[...] + out
            o_ref[...] = out

    in_specs = [pl.BlockSpec((tm, tk), lambda i, j, k: (i, k)), pl.BlockSpec((tk, tn), lambda i, j, k: (k, j))]
    args = [a, b]
    if resid is not None:
        in_specs.append(pl.BlockSpec((tm, tn), lambda i, j, k: (i, j)))
        args.append(resid)
    return pl.pallas_call(
        body, name=name, grid=(S // tm, N // tn, nk),
        in_specs=in_specs,
        out_specs=pl.BlockSpec((tm, tn), lambda i, j, k: (i, j)),
        out_shape=jax.ShapeDtypeStruct((S, N), F32),
        scratch_shapes=[pltpu.VMEM((tm, tn), F32)],
        compiler_params=_cparams(("parallel", "parallel", "arbitrary"), 48),
    )(*args)


def _mm_nt(a, b, *, tm, tn, name):
    S, K = a.shape
    N = b.shape[0]

    def body(a_ref, b_ref, o_ref):
        o_ref[...] = _dot_nt(a_ref[...], b_ref[...])

    return pl.pallas_call(
        body, name=name, grid=(N // tn, S // tm),
        in_specs=[pl.BlockSpec((tm, K), lambda j, i: (i, 0)), pl.BlockSpec((tn, K), lambda j, i: (j, 0))],
        out_specs=pl.BlockSpec((tm, tn), lambda j, i: (i, j)),
        out_shape=jax.ShapeDtypeStruct((S, N), F32),
        compiler_params=_cparams(("parallel", "parallel"), 48),
    )(a, b)


def _mm_tn(a, b, *, grid, a_spec, b_spec, out_spec, out_shape, scale, name, a_is_transposed=False):
    dot = _dot_nn if a_is_transposed else _dot_tn

    def body(a_ref, b_ref, o_ref):
        o_ref[...] = (scale * dot(a_ref[...], b_ref[...])).astype(o_ref.dtype)

    return pl.pallas_call(
        body, name=name, grid=grid, in_specs=[a_spec, b_spec], out_specs=out_spec, out_shape=out_shape,
        compiler_params=_cparams(("parallel",) * len(grid), 56),
    )(a, b)


def _ffn_bwd_act(dhb, w_out, gu, *, name):
    S, D = dhb.shape
    F = w_out.shape[0]
    tm, tn = 256, COL_TILE

    def body(dh_ref, w_ref, gu_ref, dgu_ref):
        da = 0.5 * _dot_nt(dh_ref[...], w_ref[...])
        g = gu_ref[0].astype(F32)
        u = gu_ref[1].astype(F32)
        s = _sigmoid(g)
        dgu_ref[0] = (da * u * (s * (1.0 + g * (1.0 - s)))).astype(BF16)
        dgu_ref[1] = (da * (g * s)).astype(BF16)

    return pl.pallas_call(
        body, name=name, grid=(F // tn, S // tm),
        in_specs=[
            pl.BlockSpec((tm, D), lambda j, i: (i, 0)),
            pl.BlockSpec((tn, D), lambda j, i: (j, 0)),
            pl.BlockSpec((2, tm, tn), lambda j, i: (0, i, j)),
        ],
        out_specs=pl.BlockSpec((2, tm, tn), lambda j, i: (0, i, j)),
        out_shape=jax.ShapeDtypeStruct((2, S, F), BF16),
        compiler_params=_cparams(("parallel", "parallel"), 48),
    )(dhb, w_out, gu)


def _mm_nt_rmsbwd(dact, w, h_in, dh_out, g, *, tk, name):
    ng, S, fg = dact.shape
    ns, D, cs = w.shape
    assert ng * fg == ns * cs
    tm, rc = 512, 64
    kpg, kps = fg // tk, cs // tk
    nk = ng * kpg

    def body(a_ref, w_ref, h_ref, dho_ref, g_ref, dh_ref, dhb_ref, dg_ref, acc_ref):
        i = pl.program_id(0)
        k = pl.program_id(1)

        @pl.when(k == 0)
        def _():
            acc_ref[...] = jnp.zeros_like(acc_ref)

        acc_ref[...] += _dot_nt(a_ref[...], w_ref[...])

        @pl.when(jnp.logical_and(i == 0, k == 0))
        def _():
            dg_ref[...] = jnp.zeros_like(dg_ref)

        @pl.when(k == nk - 1)
        def _():
            gv = g_ref[...]

            def rows_step(c, dg):
                rows = pl.ds(pl.multiple_of(c * rc, rc), rc)
                dn = acc_ref[rows, :]
                xv = h_ref[rows, :]
                r = lax.rsqrt(jnp.mean(xv * xv, axis=-1, keepdims=True) + EPS)
                xh = xv * r
                dng = dn * gv
                dx = r * (dng - xh * jnp.mean(dng * xh, axis=-1, keepdims=True))
                out = dho_ref[rows, :] + dx
                dh_ref[rows, :] = out
                dhb_ref[rows, :] = out.astype(BF16)
                return dg + jnp.sum(dn * xh, axis=0, keepdims=True)

            dg_ref[...] += lax.fori_loop(0, tm // rc, rows_step, jnp.zeros((1, D), F32))

    return pl.pallas_call(
        body, name=name, grid=(S // tm, nk),
        in_specs=[
            pl.BlockSpec((None, tm, tk), lambda i, k: (lax.div(k, kpg), i, lax.rem(k, kpg))),
            pl.BlockSpec((None, D, tk), lambda i, k: (lax.div(k, kps), 0, lax.rem(k, kps))),
            pl.BlockSpec((tm, D), lambda i, k: (i, 0)),
            pl.BlockSpec((tm, D), lambda i, k: (i, 0)),
            pl.BlockSpec((1, D), lambda i, k: (0, 0)),
        ],
        out_specs=[
            pl.BlockSpec((tm, D), lambda i, k: (i, 0)),
            pl.BlockSpec((tm, D), lambda i, k: (i, 0)),
            pl.BlockSpec((1, D), lambda i, k: (0, 0)),
        ],
        out_shape=[jax.ShapeDtypeStruct((S, D), F32), jax.ShapeDtypeStruct((S, D), BF16),
                   jax.ShapeDtypeStruct((1, D), F32)],
        scratch_shapes=[pltpu.VMEM((tm, D), F32)],
        compiler_params=_cparams(("arbitrary", "arbitrary"), 56),
    )(dact, w, h_in, dh_out, g)


def _final_loss(h, g, target, *, name):
    S, D = h.shape
    tm = 256

    def body(h_ref, g_ref, t_ref, loss_ref, dh_ref, dhb_ref, dg_ref):
        i = pl.program_id(0)

        @pl.when(i == 0)
        def _():
            loss_ref[...] = jnp.zeros_like(loss_ref)
            dg_ref[...] = jnp.zeros_like(dg_ref)

        xv = h_ref[...]
        gv = g_ref[...]
        r = lax.rsqrt(jnp.mean(xv * xv, axis=-1, keepdims=True) + EPS)
        xh = xv * r
        e = xh * gv - t_ref[...]
        loss_ref[...] += 0.5 * jnp.sum(jnp.mean(e * e, axis=-1, keepdims=True))
        dy = e * (1.0 / D)
        dg_ref[...] += jnp.sum(dy * xh, axis=0, keepdims=True)
        dyg = dy * gv
        dx = r * (dyg - xh * jnp.mean(dyg * xh, axis=-1, keepdims=True))
        dh_ref[...] = dx
        dhb_ref[...] = dx.astype(BF16)

    return pl.pallas_call(
        body, name=name, grid=(S // tm,),
        in_specs=[pl.BlockSpec((tm, D), lambda i: (i, 0)), pl.BlockSpec((1, D), lambda i: (0, 0)),
                  pl.BlockSpec((tm, D), lambda i: (i, 0))],
        out_specs=[pl.BlockSpec((8, LANES), lambda i: (0, 0)), pl.BlockSpec((tm, D), lambda i: (i, 0)),
                   pl.BlockSpec((tm, D), lambda i: (i, 0)), pl.BlockSpec((1, D), lambda i: (0, 0))],
        out_shape=[jax.ShapeDtypeStruct((8, LANES), F32), jax.ShapeDtypeStruct((S, D), F32),
                   jax.ShapeDtypeStruct((S, D), BF16), jax.ShapeDtypeStruct((1, D), F32)],
        compiler_params=_cparams(("arbitrary",), 40),
    )(h, g, target)


POOL_HALO = 16
POOL_ROWS = 256


def _pool_window_mean_minus_token(ext, tok0, w):
    s = ext
    k = 1
    while k < w:
        s = s + pltpu.roll(s, k, 0)
        k *= 2
    win = s[POOL_HALO:, :]
    tok = tok0 + lax.broadcasted_iota(jnp.int32, (POOL_ROWS, 1), 0)
    cnt = jnp.minimum(tok + 1, w).astype(F32)
    return win / cnt - ext[POOL_HALO:, :], cnt


def _pool_fwd(u, w_pool, scale, *, name):
    S = u.shape[0]
    C = POOL_GROUP_DIM
    nsteps = S // POOL_ROWS

    def body(p_ref, w_ref, sc_ref, y_ref, xp_ref):
        xp_ref[0:POOL_HALO, :] = jnp.zeros((POOL_HALO, D_POOL), F32)
        xp_ref[POOL_HALO:, :] = p_ref[...]
        for gi, win in enumerate(POOL_WINDOWS):
            cols = slice(gi * C, (gi + 1) * C)

            def step(c, carry, cols=cols, win=win, gi=gi):
                r0 = pl.multiple_of(c * POOL_ROWS, POOL_ROWS)
                ext = xp_ref[pl.ds(r0, POOL_ROWS + POOL_HALO), cols]
                pooled, _ = _pool_window_mean_minus_token(ext, r0, win)
                y = _dot_nn(pooled.astype(BF16), w_ref[gi]) * sc_ref[:, cols]
                y_ref[pl.ds(r0, POOL_ROWS), cols] = y.astype(BF16)
                return carry

            lax.fori_loop(0, nsteps, step, 0)

    return pl.pallas_call(
        body, name=name, grid=(1,),
        in_specs=[pl.BlockSpec((S, D_POOL), lambda i: (0, 0)),
                  pl.BlockSpec((4, C, C), lambda i: (0, 0, 0)),
                  pl.BlockSpec((1, D_POOL), lambda i: (0, 0))],
        out_specs=pl.BlockSpec((S, D_POOL), lambda i: (0, 0)),
        out_shape=jax.ShapeDtypeStruct((S, D_POOL), BF16),
        scratch_shapes=[pltpu.VMEM((S + POOL_HALO, D_POOL), F32)],
        compiler_params=_cparams(("arbitrary",), 48),
    )(u, w_pool, scale)


def _pool_bwd(u, dcat, w_pool, scale, *, name):
    S = u.shape[0]
    C = POOL_GROUP_DIM
    nsteps = S // POOL_ROWS

    def body(p_ref, dy_ref, w_ref, sc_ref, dp_ref, dw_ref, dsc_ref, xp_ref, e_ref, neg_ref):
        xp_ref[0:POOL_HALO, :] = jnp.zeros((POOL_HALO, D_POOL), F32)
        xp_ref[POOL_HALO:, :] = p_ref[...]
        e_ref[S:, :] = jnp.zeros((POOL_HALO, C), F32)
        for gi, win in enumerate(POOL_WINDOWS):
            cols = slice(gi * C, (gi + 1) * C)

            def step_a(c, carry, cols=cols, win=win, gi=gi):
                dw, dsc = carry
                r0 = pl.multiple_of(c * POOL_ROWS, POOL_ROWS)
                ext = xp_ref[pl.ds(r0, POOL_ROWS + POOL_HALO), cols]
                pooled, cnt = _pool_window_mean_minus_token(ext, r0, win)
                pb = pooled.astype(BF16)
                wv = w_ref[gi]
                dy = dy_ref[pl.ds(r0, POOL_ROWS), cols]
                dsc = dsc + jnp.sum(dy * _dot_nn(pb, wv), axis=0, keepdims=True)
                dyp = (dy * sc_ref[:, cols]).astype(BF16)
                dw = dw + _dot_tn(pb, dyp)
                dpooled = _dot_nt(dyp, wv)
                e_ref[pl.ds(r0, POOL_ROWS), :] = dpooled / cnt
                neg_ref[pl.ds(r0, POOL_ROWS), :] = -dpooled
                return dw, dsc

            dw, dsc = lax.fori_loop(0, nsteps, step_a, (jnp.zeros((C, C), F32), jnp.zeros((1, C), F32)))
            dw_ref[gi] = dw
            dsc_ref[:, cols] = dsc

            def step_b(c, carry, cols=cols, win=win):
                r0 = pl.multiple_of(c * POOL_ROWS, POOL_ROWS)
                s = e_ref[pl.ds(r0, POOL_ROWS + POOL_HALO), :]
                n = POOL_ROWS + POOL_HALO
                k = 1
                while k < win:
                    s = s + pltpu.roll(s, n - k, 0)
                    k *= 2
                du = s[:POOL_ROWS, :] + neg_ref[pl.ds(r0, POOL_ROWS), :]
                dp_ref[pl.ds(r0, POOL_ROWS), cols] = du.astype(BF16)
                return carry

            lax.fori_loop(0, nsteps, step_b, 0)

    return pl.pallas_call(
        body, name=name, grid=(1,),
        in_specs=[pl.BlockSpec((S, D_POOL), lambda i: (0, 0)),
                  pl.BlockSpec((S, D_POOL), lambda i: (0, 0)),
                  pl.BlockSpec((4, C, C), lambda i: (0, 0, 0)),
                  pl.BlockSpec((1, D_POOL), lambda i: (0, 0))],
        out_specs=[pl.BlockSpec((S, D_POOL), lambda i: (0, 0)),
                   pl.BlockSpec((4, C, C), lambda i: (0, 0, 0)),
                   pl.BlockSpec((1, D_POOL), lambda i: (0, 0))],
        out_shape=[jax.ShapeDtypeStruct((S, D_POOL), BF16), jax.ShapeDtypeStruct((4, C, C), F32),
                   jax.ShapeDtypeStruct((1, D_POOL), F32)],
        scratch_shapes=[pltpu.VMEM((S + POOL_HALO, D_POOL), F32), pltpu.VMEM((S + POOL_HALO, C), F32),
                        pltpu.VMEM((S, C), F32)],
        compiler_params=_cparams(("arbitrary",), 56),
    )(u, dcat, w_pool, scale)


GLA_ROWS = 128
U_Q_BLK, U_K_BLK = 2, 3
U_V_BLK, U_G_BLK = 2, 3
U_R_BLK = 32


def _prefix_sum_rows(x):
    n = x.shape[0]
    row = lax.broadcasted_iota(jnp.int32, x.shape, 0)
    k = 1
    while k < n:
        x = x + jnp.where(row >= k, pltpu.roll(x, k, 0), 0.0)
        k *= 2
    return x


def _suffix_sum_rows(x):
    n = x.shape[0]
    row = lax.broadcasted_iota(jnp.int32, x.shape, 0)
    k = 1
    while k < n:
        x = x + jnp.where(row < n - k, pltpu.roll(x, n - k, 0), 0.0)
        k *= 2
    return x


def _log_sigmoid(z):
    return jnp.minimum(z, 0.0) - jnp.log(1.0 + jnp.exp(-jnp.abs(z)))


def _gla_chunk_terms(la_c, q_c, k_c):
    bc = _prefix_sum_rows(la_c)
    bl = jnp.sum(la_c, axis=0, keepdims=True)
    eb = jnp.exp(bc)
    enb = jnp.exp(-bc)
    etail = jnp.exp(bl - bc)
    qd = q_c * (GLA_DK ** -0.5) * eb
    ki = k_c * enb
    kt = k_c * etail
    d = jnp.exp(bl)
    return eb, enb, etail, qd, ki, kt, d


def _gla_fwd(u, w_alpha, b_alpha, gnorm, *, name):
    S = u.shape[0]
    RB = GLA_ROWS
    ncc = RB // CHUNK
    H, DK, DV = GLA_HEADS, GLA_DK, GLA_DV

    def body(q_ref, k_ref, v_ref, go_ref, r_ref, wa_ref, ba_ref, gn_ref, y_ref, o_ref, st_ref, state):
        i = pl.program_id(0)

        @pl.when(i == 0)
        def _():
            state[...] = jnp.zeros_like(state)

        z = _dot_nn(r_ref[...].astype(BF16), wa_ref[...]) + ba_ref[...]
        la = _log_sigmoid(z) / GATE_LOGIT_NORMALIZER
        ri = lax.broadcasted_iota(jnp.int32, (CHUNK, CHUNK), 0)
        ci = lax.broadcasted_iota(jnp.int32, (CHUNK, CHUNK), 1)
        tri = ri >= ci
        gn = gn_ref[...]
        for cc in range(ncc):
            rs = slice(cc * CHUNK, (cc + 1) * CHUNK)
            for h in range(H):
                ks = slice(h * DK, (h + 1) * DK)
                vs = slice(h * DV, (h + 1) * DV)
                _, _, _, qd, ki, kt, d = _gla_chunk_terms(la[rs, ks], q_ref[rs, ks], k_ref[rs, ks])
                qdb = qd.astype(BF16)
                vb = v_ref[rs, vs].astype(BF16)
                p = jnp.where(tri, _dot_nt(qdb, ki.astype(BF16)), 0.0)
                st = state[h]
                st_ref[cc, h] = st
                o = _dot_nn(p.astype(BF16), vb) + _dot_nt(qdb, st.astype(BF16))
                state[h] = st * d + _dot_tn(vb, kt.astype(BF16))
                o_ref[rs, vs] = o
                rinv = lax.rsqrt(jnp.mean(o * o, axis=-1, keepdims=True) + EPS)
                go = go_ref[rs, vs]
                y_ref[rs, vs] = (o * rinv * gn * (go * _sigmoid(go))).astype(BF16)

    nblk = S // RB
    return pl.pallas_call(
        body, name=name, grid=(nblk,),
        in_specs=[
            pl.BlockSpec((RB, GLA_DK_TOTAL), lambda i: (i, U_Q_BLK)),
            pl.BlockSpec((RB, GLA_DK_TOTAL), lambda i: (i, U_K_BLK)),
            pl.BlockSpec((RB, D_GLA), lambda i: (i, U_V_BLK)),
            pl.BlockSpec((RB, D_GLA), lambda i: (i, U_G_BLK)),
            pl.BlockSpec((RB, LANES), lambda i: (i, U_R_BLK)),
            pl.BlockSpec((LANES, GLA_DK_TOTAL), lambda i: (0, 0)),
            pl.BlockSpec((1, GLA_DK_TOTAL), lambda i: (0, 0)),
            pl.BlockSpec((1, DV), lambda i: (0, 0)),
        ],
        out_specs=[
            pl.BlockSpec((RB, D_GLA), lambda i: (i, 0)),
            pl.BlockSpec((RB, D_GLA), lambda i: (i, 0)),
            pl.BlockSpec((ncc, H, DV, DK), lambda i: (i, 0, 0, 0)),
        ],
        out_shape=[jax.ShapeDtypeStruct((S, D_GLA), BF16), jax.ShapeDtypeStruct((S, D_GLA), F32),
                   jax.ShapeDtypeStruct((S // CHUNK, H, DV, DK), F32)],
        scratch_shapes=[pltpu.VMEM((H, DV, DK), F32)],
        compiler_params=_cparams(("arbitrary",), 32),
    )(u, u, u, u, u, w_alpha, b_alpha, gnorm)


def _gla_bwd(u, o, states, dcat, w_alpha, b_alpha, gnorm, *, name):
    S = u.shape[0]
    RB = GLA_ROWS
    ncc = RB // CHUNK
    H, DK, DV = GLA_HEADS, GLA_DK, GLA_DV
    nblk = S // RB

    def body(q_ref, k_ref, v_ref, go_ref, r_ref, o_ref, st_ref, dy_ref, wa_ref, ba_ref, gn_ref,
             dq_ref, dk_ref, dv_ref, dgo_ref, dr_ref, dwa_ref, dba_ref, dgn_ref, dstate, dz_ref):
        i = pl.program_id(0)

        @pl.when(i == 0)
        def _():
            dstate[...] = jnp.zeros_like(dstate)
            dwa_ref[...] = jnp.zeros_like(dwa_ref)
            dba_ref[...] = jnp.zeros_like(dba_ref)
            dgn_ref[...] = jnp.zeros_like(dgn_ref)

        rb = r_ref[...].astype(BF16)
        wa = wa_ref[...]
        z = _dot_nn(rb, wa) + ba_ref[...]
        la = _log_sigmoid(z) / GATE_LOGIT_NORMALIZER
        ri = lax.broadcasted_iota(jnp.int32, (CHUNK, CHUNK), 0)
        ci = lax.broadcasted_iota(jnp.int32, (CHUNK, CHUNK), 1)
        tri = ri >= ci
        last_row = lax.broadcasted_iota(jnp.int32, (CHUNK, DK), 0) == CHUNK - 1
        gn = gn_ref[...]
        dgn = jnp.zeros((1, DV), F32)
        for cc in reversed(range(ncc)):
            rs = slice(cc * CHUNK, (cc + 1) * CHUNK)
            for h in range(H):
                ks = slice(h * DK, (h + 1) * DK)
                vs = slice(h * DV, (h + 1) * DV)
                eb, enb, etail, qd, ki, kt, d = _gla_chunk_terms(la[rs, ks], q_ref[rs, ks], k_ref[rs, ks])
                qdb, kib, ktb = qd.astype(BF16), ki.astype(BF16), kt.astype(BF16)
                vb = v_ref[rs, vs].astype(BF16)
                p = jnp.where(tri, _dot_nt(qdb, kib), 0.0)
                ov = o_ref[rs, vs]
                go = go_ref[rs, vs]
                dy = dy_ref[rs, vs]
                rinv = lax.rsqrt(jnp.mean(ov * ov, axis=-1, keepdims=True) + EPS)
                oh = ov * rinv
                sg = _sigmoid(go)
                dgo_ref[rs, vs] = (dy * (oh * gn) * (sg * (1.0 + go * (1.0 - sg)))).astype(BF16)
                don = dy * (go * sg)
                dgn = dgn + jnp.sum(don * oh, axis=0, keepdims=True)
                doh = don * gn
                do = rinv * (doh - oh * jnp.mean(doh * oh, axis=-1, keepdims=True))
                dob = do.astype(BF16)
                st = st_ref[cc, h]
                dst = dstate[h]
                stb, dstb = st.astype(BF16), dst.astype(BF16)
                dp = jnp.where(tri, _dot_nt(dob, vb), 0.0).astype(BF16)
                dv_ref[rs, vs] = (_dot_tn(p.astype(BF16), dob) + _dot_nt(ktb, dstb)).astype(BF16)
                dqd = _dot_nn(dp, kib) + _dot_nn(dob, stb)
                dki = _dot_tn(dp, qdb)
                dkt = _dot_nn(vb, dstb)
                dd = jnp.sum(dst * st, axis=0, keepdims=True)
                dstate[h] = dst * d + _dot_tn(dob, qdb)
                dq_ref[rs, ks] = (dqd * eb * (DK ** -0.5)).astype(BF16)
                dk_ref[rs, ks] = (dki * enb + dkt * etail).astype(BF16)
                dbl = jnp.sum(dkt * kt, axis=0, keepdims=True) + dd * d
                dbc = dqd * qd - dki * ki - dkt * kt
                dbc = dbc + jnp.where(last_row, dbl, 0.0)
                dla = _suffix_sum_rows(dbc)
                dz_ref[rs, ks] = dla * (1.0 / GATE_LOGIT_NORMALIZER) * (1.0 - _sigmoid(z[rs, ks]))
        dz = dz_ref[...]
        dzb = dz.astype(BF16)
        dr_ref[...] = _dot_nt(dzb, wa).astype(BF16)
        dwa_ref[...] += _dot_tn(rb, dzb)
        dba_ref[...] += jnp.sum(dz, axis=0, keepdims=True)
        dgn_ref[...] += dgn

    def rev(blk):
        return lambda i: (nblk - 1 - i, blk)

    return pl.pallas_call(
        body, name=name, grid=(nblk,),
        in_specs=[
            pl.BlockSpec((RB, GLA_DK_TOTAL), rev(U_Q_BLK)),
            pl.BlockSpec((RB, GLA_DK_TOTAL), rev(U_K_BLK)),
            pl.BlockSpec((RB, D_GLA), rev(U_V_BLK)),
            pl.BlockSpec((RB, D_GLA), rev(U_G_BLK)),
            pl.BlockSpec((RB, LANES), rev(U_R_BLK)),
            pl.BlockSpec((RB, D_GLA), rev(0)),
            pl.BlockSpec((ncc, H, DV, DK), lambda i: (nblk - 1 - i, 0, 0, 0)),
            pl.BlockSpec((RB, D_GLA), rev(1)),
            pl.BlockSpec((LANES, GLA_DK_TOTAL), lambda i: (0, 0)),
            pl.BlockSpec((1, GLA_DK_TOTAL), lambda i: (0, 0)),
            pl.BlockSpec((1, DV), lambda i: (0, 0)),
        ],
        out_specs=[
            pl.BlockSpec((RB, GLA_DK_TOTAL), rev(0)),
            pl.BlockSpec((RB, GLA_DK_TOTAL), rev(0)),
            pl.BlockSpec((RB, D_GLA), rev(0)),
            pl.BlockSpec((RB, D_GLA), rev(0)),
            pl.BlockSpec((RB, LANES), rev(0)),
            pl.BlockSpec((LANES, GLA_DK_TOTAL), lambda i: (0, 0)),
            pl.BlockSpec((1, GLA_DK_TOTAL), lambda i: (0, 0)),
            pl.BlockSpec((1, DV), lambda i: (0, 0)),
        ],
        out_shape=[
            jax.ShapeDtypeStruct((S, GLA_DK_TOTAL), BF16), jax.ShapeDtypeStruct((S, GLA_DK_TOTAL), BF16),
            jax.ShapeDtypeStruct((S, D_GLA), BF16), jax.ShapeDtypeStruct((S, D_GLA), BF16),
            jax.ShapeDtypeStruct((S, LANES), BF16),
            jax.ShapeDtypeStruct((LANES, GLA_DK_TOTAL), F32), jax.ShapeDtypeStruct((1, GLA_DK_TOTAL), F32),
            jax.ShapeDtypeStruct((1, DV), F32),
        ],
        scratch_shapes=[pltpu.VMEM((H, DV, DK), F32), pltpu.VMEM((RB, GLA_DK_TOTAL), F32)],
        compiler_params=_cparams(("arbitrary",), 32),
    )(u, u, u, u, u, o, states, dcat, w_alpha, b_alpha, gnorm)


def _row_tile(rows, cols, itemsize, budget=2 * 1024 * 1024):
    if rows * cols * itemsize <= budget or rows % 16:
        return rows
    best = 16
    for t in range(16, rows + 1, 16):
        if rows % t == 0 and t * cols * itemsize <= budget:
            best = t
    return best


def _adamw(w, g, m, v, *, name):
    R, C = w.shape
    tr = _row_tile(R, C, 4, budget=1024 * 1024)

    def body(w_ref, g_ref, m_ref, v_ref, d_ref, nm_ref, nv_ref):
        gv = g_ref[...]
        mn = ADAM_B1 * m_ref[...] + (1.0 - ADAM_B1) * gv
        vn = ADAM_B2 * v_ref[...] + (1.0 - ADAM_B2) * jnp.square(gv)
        m_hat = mn / (1.0 - ADAM_B1 ** ADAM_STEP)
        v_hat = vn / (1.0 - ADAM_B2 ** ADAM_STEP)
        d_ref[...] = -ADAM_LR * (m_hat / (jnp.sqrt(v_hat) + ADAM_EPS) + ADAM_WD * w_ref[...])
        nm_ref[...] = mn
        nv_ref[...] = vn

    spec = pl.BlockSpec((tr, C), lambda i: (i, 0))
    shp = jax.ShapeDtypeStruct((R, C), F32)
    return pl.pallas_call(
        body, name=name, grid=(R // tr,), in_specs=[spec] * 4, out_specs=[spec] * 3, out_shape=[shp] * 3,
        compiler_params=_cparams(("parallel",), 32),
    )(w, g, m, v)


def _pair_sum(g4, recv, c_idx, *, name):
    ns, _, R2, C = g4.shape
    tr = _row_tile(R2, C, 2)

    def body(c_ref, g_ref, r_ref, o_ref):
        o_ref[...] = (g_ref[...].astype(F32) + r_ref[...].astype(F32)).astype(BF16)

    return pl.pallas_call(
        body, name=name,
        grid_spec=pltpu.PrefetchScalarGridSpec(
            num_scalar_prefetch=1, grid=(ns, R2 // tr),
            in_specs=[pl.BlockSpec((None, None, tr, C), lambda s, i, c: (s, c[0], i, 0)),
                      pl.BlockSpec((None, tr, C), lambda s, i, c: (s, i, 0))],
            out_specs=pl.BlockSpec((None, tr, C), lambda s, i, c: (s, i, 0)),
        ),
        out_shape=jax.ShapeDtypeStruct((ns, R2, C), BF16),
        compiler_params=_cparams(("parallel", "parallel"), 32),
    )(c_idx, g4, recv)


def _chip_sum(part, recv, sc_idx, *, name):
    _, R2, C = part.shape
    tr = _row_tile(R2, C, 4)
    nblk = R2 // tr

    def body(s_ref, p_ref, r_ref, o_ref):
        acc = p_ref[...].astype(F32)
        for j in range(N_CHIPS - 1):
            acc = acc + r_ref[j].astype(F32)
        o_ref[...] = acc

    return pl.pallas_call(
        body, name=name,
        grid_spec=pltpu.PrefetchScalarGridSpec(
            num_scalar_prefetch=1, grid=(nblk,),
            in_specs=[pl.BlockSpec((None, tr, C), lambda i, s: (s[0], i, 0)),
                      pl.BlockSpec((N_CHIPS - 1, tr, C), lambda i, s: (0, i, 0))],
            out_specs=pl.BlockSpec((tr, C), lambda i, s: (s[1] * nblk + i, 0)),
        ),
        out_shape=jax.ShapeDtypeStruct((2 * R2, C), F32),
        compiler_params=_cparams(("parallel",), 32),
    )(sc_idx, part, recv)


def _cast_into_slot(w, sc_idx, dtype, *, name):
    R, C = w.shape
    tr = _row_tile(R, C, 4)

    def body(s_ref, w_ref, o_ref):
        o_ref[...] = w_ref[...].astype(dtype)

    return pl.pallas_call(
        body, name=name,
        grid_spec=pltpu.PrefetchScalarGridSpec(
            num_scalar_prefetch=1, grid=(R // tr,),
            in_specs=[pl.BlockSpec((tr, C), lambda i, s: (i, 0))],
            out_specs=pl.BlockSpec((None, tr, C), lambda i, s: (s[0], i, 0)),
        ),
        out_shape=jax.ShapeDtypeStruct((N_CHIPS, R, C), dtype),
        compiler_params=_cparams(("parallel",), 32),
    )(sc_idx, w)


def _slab_sum(slabs, *, name):
    n, M, C = slabs.shape

    def body(x_ref, o_ref):
        acc = x_ref[0]
        for d in range(1, n):
            acc = acc + x_ref[d]
        o_ref[...] = acc

    return pl.pallas_call(
        body, name=name, out_shape=jax.ShapeDtypeStruct((M, C), F32),
    )(slabs)


def _mesh_position():
    x, y, c = lax.axis_index("x"), lax.axis_index("y"), lax.axis_index("c")
    other_chips = [(1 - x, y), (x, 1 - y), (1 - x, 1 - y)]
    return x, y, c, other_chips


ANY = pl.BlockSpec(memory_space=pl.ANY)


HBM = pl.BlockSpec(memory_space=pltpu.HBM)
SEM = pl.BlockSpec(memory_space=pltpu.SEMAPHORE)
SPLIT_COPY = pltpu.CompilerParams(has_side_effects=pltpu.SideEffectType.DATAFLOW_SIDE_EFFECTING)
TOKEN = jax.ShapeDtypeStruct((8, LANES), F32)


def _in_hbm(a):
    return pltpu.with_memory_space_constraint(a, pltpu.HBM)


def _half_rows(ref, slot, half):
    hr = ref.shape[1] // 2
    return ref.at[slot, pl.ds(half * hr, hr), :]


def _gather_ici_start(groups):
    flat = [b for g in groups for b in g]
    K, G = len(flat), len(groups)

    def body(*refs):
        ins, sems, token = refs[:K], refs[K:K + 2 * G], refs[-1]
        x, y, c, chips = _mesh_position()
        s = 2 * x + y
        k = 0
        for gi, g in enumerate(groups):
            for n in range(len(g)):
                own = _half_rows(ins[k], s, c)
                for j, chip in enumerate(chips):
                    pltpu.make_async_remote_copy(
                        src_ref=own, dst_ref=own, send_sem=sems[2 * gi].at[n * (N_CHIPS - 1) + j],
                        recv_sem=sems[2 * gi + 1].at[n * (N_CHIPS - 1) + j],
                        device_id=(*chip, c), device_id_type=MESH).start()
                k += 1
        token[...] = jnp.zeros_like(token)

    sem_shapes = []
    for g in groups:
        sem_shapes += [pltpu.SemaphoreType.DMA((len(g) * (N_CHIPS - 1),))] * 2
    out = pl.pallas_call(
        body, name="gather_ici_start",
        in_specs=[HBM] * K,
        out_specs=[SEM] * (2 * G) + [HBM] * K + [pl.BlockSpec(memory_space=pltpu.VMEM)],
        out_shape=sem_shapes + [pltpu.HBM(b.shape, b.dtype) for b in flat] + [TOKEN],
        input_output_aliases={k: 2 * G + k for k in range(K)},
        compiler_params=SPLIT_COPY,
    )(*[_in_hbm(b) for b in flat])
    handles, k = [], 2 * G
    for gi, g in enumerate(groups):
        handles.append((out[2 * gi], out[2 * gi + 1], list(out[k:k + len(g)])))
        k += len(g)
    return handles, out[-1]


def _gather_ici_wait(handle, after, *, name):
    send, recv, bufs = handle
    n = len(bufs)

    def body(*refs):
        ins, send_ref, recv_ref = refs[:n], refs[n], refs[n + 1]
        x, y, c, chips = _mesh_position()
        s = 2 * x + y
        for k in range(n):
            own = _half_rows(ins[k], s, c)
            for j, chip in enumerate(chips):
                cp = pltpu.make_async_remote_copy(
                    src_ref=own, dst_ref=_half_rows(ins[k], 2 * chip[0] + chip[1], c),
                    send_sem=send_ref.at[k * (N_CHIPS - 1) + j], recv_sem=recv_ref.at[k * (N_CHIPS - 1) + j],
                    device_id=(*chip, c), device_id_type=MESH)
                cp.wait_send()
                cp.wait_recv()

    return pl.pallas_call(
        body, name=name,
        in_specs=[HBM] * n + [SEM, SEM, ANY], out_specs=[HBM] * n,
        out_shape=[pltpu.HBM(b.shape, b.dtype) for b in bufs],
        input_output_aliases={k: k for k in range(n)},
        compiler_params=SPLIT_COPY,
    )(*bufs, send, recv, after)


def _forward_halves(bufs, *, name):
    K = len(bufs)
    per = N_CHIPS - 1

    def body(*refs):
        outs = refs[K:2 * K]
        send_sems, recv_sems = refs[2 * K:]
        x, y, c, chips = _mesh_position()
        copies = []
        for k in range(K):
            for j, chip in enumerate(chips):
                got = _half_rows(outs[k], 2 * chip[0] + chip[1], c)
                cp = pltpu.make_async_remote_copy(
                    src_ref=got, dst_ref=got, send_sem=send_sems.at[k * per + j], recv_sem=recv_sems.at[k * per + j],
                    device_id=(x, y, 1 - c), device_id_type=MESH)
                cp.start()
                copies.append(cp)
        for cp in copies:
            cp.wait()

    return pl.pallas_call(
        body, name=name,
        in_specs=[ANY] * K, out_specs=[ANY] * K,
        out_shape=[jax.ShapeDtypeStruct(a.shape, a.dtype) for a in bufs],
        input_output_aliases={k: k for k in range(K)},
        scratch_shapes=[pltpu.SemaphoreType.DMA((K * per,)), pltpu.SemaphoreType.DMA((K * per,))],
    )(*bufs)


def _swap_halves(grads4, *, name):
    K = len(grads4)

    def body(*refs):
        ins, outs = refs[:K], refs[K:2 * K]
        send_sems, recv_sems = refs[2 * K:]
        x, y, c, _ = _mesh_position()
        copies = []
        for k in range(K):
            ns = grads4[k].shape[0]
            cp = pltpu.make_async_remote_copy(
                src_ref=ins[k].at[pl.ds(0, ns), 1 - c], dst_ref=outs[k],
                send_sem=send_sems.at[k], recv_sem=recv_sems.at[k], device_id=(x, y, 1 - c), device_id_type=MESH)
            cp.start()
            copies.append(cp)
        for cp in copies:
            cp.wait()

    return pl.pallas_call(
        body, name=name,
        in_specs=[ANY] * K, out_specs=[ANY] * K,
        out_shape=[jax.ShapeDtypeStruct((a.shape[0],) + a.shape[2:], a.dtype) for a in grads4],
        scratch_shapes=[pltpu.SemaphoreType.DMA((K,)), pltpu.SemaphoreType.DMA((K,))],
    )(*grads4)


def _exchange_start(parts, *, name):
    K = len(parts)
    per = N_CHIPS - 1

    def body(*refs):
        ins, lands, send_sems, recv_sems, token = refs[:K], refs[K:2 * K], refs[2 * K], refs[2 * K + 1], refs[-1]
        x, y, c, chips = _mesh_position()
        for k in range(K):
            for j, chip in enumerate(chips):
                pltpu.make_async_remote_copy(
                    src_ref=ins[k].at[2 * chip[0] + chip[1]], dst_ref=lands[k].at[j],
                    send_sem=send_sems.at[k * per + j], recv_sem=recv_sems.at[k * per + j],
                    device_id=(*chip, c), device_id_type=MESH).start()
        token[...] = jnp.zeros_like(token)

    land_shapes = [(N_CHIPS - 1,) + a.shape[1:] for a in parts]
    out = pl.pallas_call(
        body, name=name,
        in_specs=[HBM] * (2 * K),
        out_specs=[SEM, SEM] + [HBM] * (2 * K) + [pl.BlockSpec(memory_space=pltpu.VMEM)],
        out_shape=[pltpu.SemaphoreType.DMA((K * per,))] * 2
        + [pltpu.HBM(a.shape, a.dtype) for a in parts]
        + [pltpu.HBM(s, a.dtype) for s, a in zip(land_shapes, parts)] + [TOKEN],
        input_output_aliases={k: 2 + k for k in range(2 * K)},
        compiler_params=SPLIT_COPY,
    )(*[_in_hbm(a) for a in parts], *[_in_hbm(lax.empty(s, a.dtype)) for s, a in zip(land_shapes, parts)])
    return (out[0], out[1], list(out[2:2 + K]), list(out[2 + K:2 + 2 * K])), out[-1]


def _exchange_wait(handle, after, *, name):
    send, recv, parts, lands = handle
    K = len(parts)
    per = N_CHIPS - 1

    def body(*refs):
        ins, lnd, send_ref, recv_ref = refs[:K], refs[K:2 * K], refs[2 * K], refs[2 * K + 1]
        x, y, c, chips = _mesh_position()
        for k in range(K):
            for j, chip in enumerate(chips):
                cp = pltpu.make_async_remote_copy(
                    src_ref=ins[k].at[2 * chip[0] + chip[1]], dst_ref=lnd[k].at[j],
                    send_sem=send_ref.at[k * per + j], recv_sem=recv_ref.at[k * per + j],
                    device_id=(*chip, c), device_id_type=MESH)
                cp.wait_send()
                cp.wait_recv()

    out = pl.pallas_call(
        body, name=name,
        in_specs=[HBM] * (2 * K) + [SEM, SEM, ANY], out_specs=[HBM] * (2 * K),
        out_shape=[pltpu.HBM(a.shape, a.dtype) for a in parts] + [pltpu.HBM(a.shape, a.dtype) for a in lands],
        input_output_aliases={k: k for k in range(2 * K)},
        compiler_params=SPLIT_COPY,
    )(*parts, *lands, send, recv, after)
    return list(out[:K]), list(out[K:])


def _join_halves(bufs, *, name):
    K = len(bufs)

    def body(*refs):
        outs = refs[K:2 * K]
        send_sems, recv_sems = refs[2 * K:]
        x, y, c, _ = _mesh_position()
        copies = []
        for k in range(K):
            r2 = bufs[k].shape[0] // 2
            mine = outs[k].at[pl.ds(c * r2, r2), :]
            cp = pltpu.make_async_remote_copy(
                src_ref=mine, dst_ref=mine, send_sem=send_sems.at[k], recv_sem=recv_sems.at[k],
                device_id=(x, y, 1 - c), device_id_type=MESH)
            cp.start()
            copies.append(cp)
        for cp in copies:
            cp.wait()

    return pl.pallas_call(
        body, name=name,
        in_specs=[ANY] * K, out_specs=[ANY] * K,
        out_shape=[jax.ShapeDtypeStruct(a.shape, a.dtype) for a in bufs],
        input_output_aliases={k: k for k in range(K)},
        scratch_shapes=[pltpu.SemaphoreType.DMA((K,)), pltpu.SemaphoreType.DMA((K,))],
    )(*bufs)


def _all_gather_slab(slab):
    m_per, n = slab.shape

    def body(x_ref, out_ref, send_sems, recv_sems, local_sem):
        x, y, c, chips = _mesh_position()
        me, sibling = (x, y, c), (x, y, 1 - c)

        def rows(px, py, pc):
            return out_ref.at[pl.ds((4 * px + 2 * py + pc) * m_per, m_per), :]

        def copy(k, block, to, src=None):
            return pltpu.make_async_remote_copy(
                src_ref=rows(*block) if src is None else src, dst_ref=rows(*block),
                send_sem=send_sems.at[k], recv_sem=recv_sems.at[k], device_id=to, device_id_type=MESH)

        mine = pltpu.make_async_copy(x_ref, rows(*me), local_sem)
        mine.start()
        first = [copy(0, me, sibling, src=x_ref)]
        first += [copy(1 + j, me, (*chip, c), src=x_ref) for j, chip in enumerate(chips)]
        for cp in first:
            cp.start()
        passed = [copy(4 + j, (*chip, c), sibling) for j, chip in enumerate(chips)]
        for j, chip in enumerate(chips):
            copy(1 + j, (*chip, c), me).wait_recv()
            passed[j].start()
        copy(0, sibling, me).wait_recv()
        for j, chip in enumerate(chips):
            copy(4 + j, (*chip, 1 - c), me).wait_recv()
        for cp in first + passed:
            cp.wait_send()
        mine.wait()

    return pl.pallas_call(
        body, name="gather_small_grads",
        out_shape=jax.ShapeDtypeStruct((N_DEV * m_per, n), slab.dtype),
        in_specs=[pl.BlockSpec(memory_space=pltpu.VMEM)],
        out_specs=pl.BlockSpec(memory_space=pltpu.VMEM),
        scratch_shapes=[pltpu.SemaphoreType.DMA((7,)), pltpu.SemaphoreType.DMA((7,)), pltpu.SemaphoreType.DMA],
    )(slab)


def _ffn_backward_weights(dhb, saved, w_in, w_out, tag):
    n, gu, a = saved
    S = n.shape[0]
    ns, D, cs = w_in.shape
    F = w_out.shape[0]
    dgu = _ffn_bwd_act(dhb, w_out, gu, name=f"{tag}_bwd_act")
    to = 512
    dw_out = _mm_tn(
        a, dhb, grid=(F // to,),
        a_spec=pl.BlockSpec((S, to), lambda j: (0, j)), b_spec=pl.BlockSpec((S, D), lambda j: (0, 0)),
        out_spec=pl.BlockSpec((to, D), lambda j: (j, 0)), out_shape=jax.ShapeDtypeStruct((F, D), BF16),
        scale=0.5, name=f"{tag}_dw_out")
    ti = MXU_TILE
    per_g, per_s = F // ti, cs // ti
    dw_in = _mm_tn(
        n.T, dgu, grid=(2 * F // ti,), a_is_transposed=True,
        a_spec=pl.BlockSpec((D, S), lambda j: (0, 0)),
        b_spec=pl.BlockSpec((None, S, ti), lambda j: (lax.div(j, per_g), 0, lax.rem(j, per_g))),
        out_spec=pl.BlockSpec((None, D, ti), lambda j: (lax.div(j, per_s), 0, lax.rem(j, per_s))),
        out_shape=jax.ShapeDtypeStruct((ns, D, cs), BF16), scale=1.0, name=f"{tag}_dw_in")
    return dgu, dw_in, dw_out.reshape(N_CHIPS, F // N_CHIPS, D)


def kernel(x, ffn1_norm, ffn1_w_in, ffn1_w_out, mix_norm, w_in_mix, w_pool, pool_scale, w_alpha, b_alpha, gla_norm, w_out_mix, ffn2_norm, ffn2_w_in, ffn2_w_out, final_norm, loss_target, m_ffn1_norm, m_ffn1_w_in, m_ffn1_w_out, m_mix_norm, m_w_in_mix, m_w_pool, m_pool_scale, m_w_alpha, m_b_alpha, m_gla_norm, m_w_out_mix, m_ffn2_norm, m_ffn2_w_in, m_ffn2_w_out, m_final_norm, v_ffn1_norm, v_ffn1_w_in, v_ffn1_w_out, v_mix_norm, v_w_in_mix, v_w_pool, v_pool_scale, v_w_alpha, v_b_alpha, v_gla_norm, v_w_out_mix, v_ffn2_norm, v_ffn2_w_in, v_ffn2_w_out, v_final_norm):
    names = ["ffn1_norm", "ffn1_w_in", "ffn1_w_out", "mix_norm", "w_in_mix", "w_pool", "pool_scale", "w_alpha",
             "b_alpha", "gla_norm", "w_out_mix", "ffn2_norm", "ffn2_w_in", "ffn2_w_out", "final_norm"]
    weights = dict(zip(names, [ffn1_norm, ffn1_w_in, ffn1_w_out, mix_norm, w_in_mix, w_pool, pool_scale, w_alpha,
                               b_alpha, gla_norm, w_out_mix, ffn2_norm, ffn2_w_in, ffn2_w_out, final_norm]))
    moms = dict(zip(names, [m_ffn1_norm, m_ffn1_w_in, m_ffn1_w_out, m_mix_norm, m_w_in_mix, m_w_pool, m_pool_scale,
                            m_w_alpha, m_b_alpha, m_gla_norm, m_w_out_mix, m_ffn2_norm, m_ffn2_w_in, m_ffn2_w_out,
                            m_final_norm]))
    vels = dict(zip(names, [v_ffn1_norm, v_ffn1_w_in, v_ffn1_w_out, v_mix_norm, v_w_in_mix, v_w_pool, v_pool_scale,
                            v_w_alpha, v_b_alpha, v_gla_norm, v_w_out_mix, v_ffn2_norm, v_ffn2_w_in, v_ffn2_w_out,
                            v_final_norm]))
    xi, yi, ci = lax.axis_index("x"), lax.axis_index("y"), lax.axis_index("c")
    chip = 2 * xi + yi
    c_idx = jnp.reshape(ci, (1,)).astype(jnp.int32)
    sc_idx = jnp.stack([chip, ci]).astype(jnp.int32)

    def flat2d(a):
        return a.reshape(-1, a.shape[-1])

    sharded = [n for g in _Exchanges.GATHER_GROUPS for n in g]
    bufs = {n: _cast_into_slot(flat2d(weights[n]), sc_idx, F32 if n == "w_alpha" else BF16, name=f"cast_{n}")
            for n in sharded}
    ex = _Exchanges(sc_idx, c_idx)
    small_params = dict(g1=ffn1_norm, gm=mix_norm, g2=ffn2_norm, gf=final_norm.reshape(1, D_MODEL),
                        pool_scale=pool_scale, b_alpha=b_alpha, gla_norm=gla_norm)
    loss_blk, dx, small = _forward_backward(x[0], loss_target[0], ex, bufs, small_params)
    loss = lax.psum(loss_blk[0, 0], ("x", "y", "c"))

    outs = {}

    def update(n, g):
        w = weights[n]
        w2 = flat2d(w) if w.ndim > 1 else w.reshape(1, -1)
        d, nm, nv = _adamw(w2, g.reshape(w2.shape), moms[n].reshape(w2.shape), vels[n].reshape(w2.shape),
                           name=f"adamw_{n}")
        outs[n] = (g.reshape(w.shape), d.reshape(w.shape), nm.reshape(w.shape), nv.reshape(w.shape))
        return nv

    last = dx
    for tag in _Exchanges.REDUCE_ORDER:
        for n, g in ex.reduced(tag, after=last).items():
            last = update(n, g)

    grads = {}
    small_names = ["ffn1_norm", "mix_norm", "ffn2_norm", "final_norm", "pool_scale", "b_alpha", "gla_norm", "w_alpha"]
    rows = [a.size // LANES for a in small]
    slab = jnp.concatenate([a.reshape(-1, LANES) for a in small], axis=0)
    pad = -slab.shape[0] % 8
    slab = jnp.pad(slab, ((0, pad), (0, 0)))
    gathered = _all_gather_slab(slab).reshape(N_DEV, slab.shape[0], LANES)
    total = _slab_sum(gathered, name="sum_small_grads")
    off = 0
    for n, a, r in zip(small_names, small, rows):
        grads[n] = total[off:off + r].reshape(a.shape)
        off += r
    grads["w_alpha"] = lax.dynamic_slice_in_dim(grads["w_alpha"], chip * (GLA_DK_TOTAL // N_CHIPS),
                                                GLA_DK_TOTAL // N_CHIPS, axis=1)

    for n in small_names:
        update(n, grads[n])
    return (loss, dx[None], *[outs[n][0] for n in names], *[outs[n][1] for n in names],
            *[outs[n][2] for n in names], *[outs[n][3] for n in names])


class _Exchanges:
    GATHER_GROUPS = (("ffn1_w_in",), ("ffn1_w_out",), ("w_in_mix", "w_out_mix", "w_pool", "w_alpha"),
                     ("ffn2_w_in", "ffn2_w_out"))
    REDUCE_ORDER = ("ffn2", "mix", "ffn1")

    def __init__(self, sc_idx, c_idx):
        self.sc_idx, self.c_idx = sc_idx, c_idx
        self._gathers, self._reduces = {}, {}

    def start_gather(self, bufs):
        handles, token = _gather_ici_start([[bufs[n] for n in g] for g in self.GATHER_GROUPS])
        for g, h in zip(self.GATHER_GROUPS, handles):
            self._gathers[g[0]] = (g, h)
        return token

    def gathered(self, first, after):
        names, handle = self._gathers.pop(first)
        got = _gather_ici_wait(handle, after, name=f"gather_ici_wait_{first}")
        return dict(zip(names, _forward_halves(got, name=f"gather_forward_{first}")))

    def start_reduce(self, tag, full):
        g4 = [a.reshape(N_CHIPS, 2, a.shape[1] // 2, a.shape[2]) for a in full.values()]
        from_sibling = _swap_halves(g4, name=f"swap_halves_{tag}")
        pair = [_pair_sum(a, b, self.c_idx, name=f"pair_sum_{n}") for n, a, b in zip(full, g4, from_sibling)]
        handle, token = _exchange_start(pair, name=f"exchange_start_{tag}")
        self._reduces[tag] = (list(full), handle)
        return token

    def reduced(self, tag, after):
        names, handle = self._reduces.pop(tag)
        pair, lands = _exchange_wait(handle, after, name=f"exchange_wait_{tag}")
        halves = [_chip_sum(a, b, self.sc_idx, name=f"chip_sum_{n}") for n, a, b in zip(names, pair, lands)]
        return dict(zip(names, _join_halves(halves, name=f"join_halves_{tag}")))


def _forward_backward(h0, target, ex, bufs, sp):
    g1, gm, g2, gf = sp["g1"], sp["gm"], sp["g2"], sp["gf"]
    pool_scale, b_alpha, gla_norm = sp["pool_scale"], sp["b_alpha"], sp["gla_norm"]
    cs_mix = D_IN // N_CHIPS

    tok = ex.start_gather(bufs)
    n1 = _rms_fwd(h0, g1 + tok[0, 0], name="ffn1_norm")
    w1_in = ex.gathered("ffn1_w_in", after=n1)["ffn1_w_in"]
    gu1, a1 = _ffn_up(n1, w1_in, name="ffn1_up")
    w1_out = ex.gathered("ffn1_w_out", after=a1)["ffn1_w_out"].reshape(D_FF, D_MODEL)
    h1 = _mm_nn(a1, w1_out, h0, 0.5, tm=512, tn=D_MODEL, tk=COL_TILE, name="ffn1_down")
    saved1 = (n1, gu1, a1)
    n_mix = _rms_fwd(h1, gm, name="mix_norm")
    gw = ex.gathered("w_in_mix", after=n_mix)
    w_omix = gw["w_out_mix"].reshape(D_MODEL, D_MODEL)
    w_mix = jnp.concatenate([gw["w_in_mix"][t] for t in range(N_CHIPS)], axis=1)
    w_mix = jnp.pad(w_mix, ((0, 0), (0, D_IN_PAD - D_IN)))[None]
    wp = gw["w_pool"].reshape(N_CHIPS, 4, POOL_GROUP_DIM // N_CHIPS, POOL_GROUP_DIM)
    wp = wp.transpose(1, 0, 2, 3).reshape(4, POOL_GROUP_DIM, POOL_GROUP_DIM)
    wa = gw["w_alpha"].transpose(1, 0, 2).reshape(GLA_GATE_RANK, GLA_DK_TOTAL)
    wa = jnp.pad(wa, ((0, LANES - GLA_GATE_RANK), (0, 0))).astype(BF16)
    u = _mm_nn(n_mix, w_mix[0], None, 1.0, tm=512, tn=1408, tk=D_MODEL, name="mix_in")
    y_pool = _pool_fwd(u, wp, pool_scale, name="pool_fwd")
    y_gla, o_gla, states = _gla_fwd(u, wa, b_alpha, gla_norm, name="gla_fwd")
    cat = jnp.concatenate([y_pool, y_gla], axis=1)
    h2 = _mm_nn(cat, w_omix, h1, 1.0, tm=512, tn=D_MODEL, tk=1024, name="mix_out")
    n3 = _rms_fwd(h2, g2, name="ffn2_norm")
    gw2 = ex.gathered("ffn2_w_in", after=n3)
    w2_in, w2_out = gw2["ffn2_w_in"], gw2["ffn2_w_out"].reshape(D_FF, D_MODEL)
    gu2, a2 = _ffn_up(n3, w2_in, name="ffn2_up")
    h3 = _mm_nn(a2, w2_out, h2, 0.5, tm=512, tn=D_MODEL, tk=COL_TILE, name="ffn2_down")
    saved2 = (n3, gu2, a2)
    loss_blk, dh3, dh3b, d_gf = _final_loss(h3, gf, target, name="final_loss")

    dgu2, dw2_in, dw2_out = _ffn_backward_weights(dh3b, saved2, w2_in, w2_out, "ffn2")
    tok = ex.start_reduce("ffn2", {"ffn2_w_in": dw2_in, "ffn2_w_out": dw2_out})
    dh2, dh2b, d_g2 = _mm_nt_rmsbwd(dgu2, w2_in, h2, dh3, g2 + tok[0, 0], tk=COL_TILE, name="ffn2_dx")
    S = h0.shape[0]
    dcat = _mm_nt(dh2b, w_omix, tm=512, tn=1024, name="mix_out_dx")
    dw_omix = _mm_tn(
        cat, dh2b, grid=(4,),
        a_spec=pl.BlockSpec((S, 512), lambda j: (0, j)), b_spec=pl.BlockSpec((S, D_MODEL), lambda j: (0, 0)),
        out_spec=pl.BlockSpec((512, D_MODEL), lambda j: (j, 0)),
        out_shape=jax.ShapeDtypeStruct((D_MODEL, D_MODEL), BF16), scale=1.0, name="mix_out_dw")
    dp, dw_pool, d_pscale = _pool_bwd(u, dcat, wp, pool_scale, name="pool_bwd")
    dq, dk, dv, dgo, dr, d_wa, d_ba, d_gn = _gla_bwd(u, o_gla, states, dcat, wa, b_alpha, gla_norm, name="gla_bwd")
    du = jnp.concatenate([dp, dq, dk, dv, dgo, dr], axis=1)[None]
    tn_mix = COL_TILE
    dw_mix = _mm_tn(
        n_mix.T, du, grid=(2, D_IN_PAD // tn_mix), a_is_transposed=True,
        a_spec=pl.BlockSpec((D_MODEL // 2, S), lambda i, j: (i, 0)),
        b_spec=pl.BlockSpec((None, S, tn_mix), lambda i, j: (0, 0, j)),
        out_spec=pl.BlockSpec((D_MODEL // 2, tn_mix), lambda i, j: (i, j)),
        out_shape=jax.ShapeDtypeStruct((D_MODEL, D_IN_PAD), BF16), scale=1.0, name="mix_in_dw")
    dw_mix_s = dw_mix[:, :D_IN].reshape(D_MODEL, N_CHIPS, cs_mix).transpose(1, 0, 2)
    dw_pool_s = dw_pool.reshape(4, N_CHIPS, POOL_GROUP_DIM // N_CHIPS, POOL_GROUP_DIM).transpose(1, 0, 2, 3)
    dw_pool_s = dw_pool_s.reshape(N_CHIPS, POOL_GROUP_DIM, POOL_GROUP_DIM).astype(BF16)
    tok = ex.start_reduce("mix", {"w_in_mix": dw_mix_s,
                                  "w_out_mix": dw_omix.reshape(N_CHIPS, D_MODEL // N_CHIPS, D_MODEL),
                                  "w_pool": dw_pool_s})
    dh1, dh1b, d_gm = _mm_nt_rmsbwd(du, w_mix, h1, dh2, gm + tok[0, 0], tk=tn_mix, name="mix_in_dx")
    dgu1, dw1_in, dw1_out = _ffn_backward_weights(dh1b, saved1, w1_in, w1_out, "ffn1")
    tok = ex.start_reduce("ffn1", {"ffn1_w_in": dw1_in, "ffn1_w_out": dw1_out})
    dx, _, d_g1 = _mm_nt_rmsbwd(dgu1, w1_in, h0, dh1, g1 + tok[0, 0], tk=COL_TILE, name="ffn1_dx")
    small = [d_g1, d_gm, d_g2, d_gf, d_pscale, d_ba, d_gn, d_wa[:GLA_GATE_RANK]]
    return loss_blk, dx, small
```

```python
import functools

import jax
import jax.numpy as jnp
from jax import lax
from jax.experimental import pallas as pl
from jax.experimental.pallas import tpu as pltpu

F32 = jnp.float32
BF16 = jnp.bfloat16
MESH = pl.DeviceIdType.MESH

D_MODEL = 2048
D_FF = 5632
D_POOL = 1024
POOL_WINDOWS = (2, 4, 8, 16)
POOL_GROUP_DIM = 256
D_GLA = 1024
GLA_HEADS = 4
GLA_DV = 256
GLA_DK = 128
GLA_DK_TOTAL = 512
GLA_GATE_RANK = 16
GATE_LOGIT_NORMALIZER = 16.0
CHUNK = 64
D_IN = 4112
D_IN_PAD = 4224
EPS = 1e-6

ADAM_LR = 0.001
ADAM_B1 = 0.9
ADAM_B2 = 0.999
ADAM_EPS = 1e-08
ADAM_WD = 0.01
ADAM_STEP = 10

N_CHIPS = 4
N_DEV = 8
V7X_VMEM_BYTES = 64 * 1024 * 1024
LANES = 128
MXU_TILE = 256
COL_TILE = 1408


def _cparams(semantics, vmem_mb):
    assert vmem_mb * 1024 * 1024 < V7X_VMEM_BYTES
    return pltpu.CompilerParams(dimension_semantics=semantics, vmem_limit_bytes=vmem_mb * 1024 * 1024)


def _dot_nn(a, b):
    return jnp.dot(a, b, preferred_element_type=F32)


def _dot_nt(a, b):
    return lax.dot_general(a, b, (((1,), (1,)), ((), ())), preferred_element_type=F32)


def _dot_tn(a, b):
    return lax.dot_general(a, b, (((0,), (0,)), ((), ())), preferred_element_type=F32)


def _sigmoid(x):
    return 1.0 / (1.0 + jnp.exp(-x))


def _rms_fwd(x, g, *, name):
    S, D = x.shape
    tm = 256

    def body(x_ref, g_ref, o_ref):
        xv = x_ref[...]
        r = lax.rsqrt(jnp.mean(xv * xv, axis=-1, keepdims=True) + EPS)
        o_ref[...] = (xv * r * g_ref[...]).astype(BF16)

    return pl.pallas_call(
        body, name=name, grid=(S // tm,),
        in_specs=[pl.BlockSpec((tm, D), lambda i: (i, 0)), pl.BlockSpec((1, D), lambda i: (0, 0))],
        out_specs=pl.BlockSpec((tm, D), lambda i: (i, 0)),
        out_shape=jax.ShapeDtypeStruct((S, D), BF16),
        compiler_params=_cparams(("parallel",), 32),
    )(x, g)


def _ffn_up(n, w_in, *, name):
    S, D = n.shape
    ns, _, cs = w_in.shape
    half = ns // 2
    F = cs * half
    tm, tn = 256, COL_TILE
    nb = cs // tn

    def body(n_ref, wg_ref, wu_ref, gu_ref, a_ref):
        nv = n_ref[...]
        g = _dot_nn(nv, wg_ref[...])
        u = _dot_nn(nv, wu_ref[...])
        gu_ref[0] = g.astype(BF16)
        gu_ref[1] = u.astype(BF16)
        a_ref[...] = (g * _sigmoid(g) * u).astype(BF16)

    return pl.pallas_call(
        body, name=name, grid=(F // tn, S // tm),
        in_specs=[
            pl.BlockSpec((tm, D), lambda j, i: (i, 0)),
            pl.BlockSpec((None, D, tn), lambda j, i: (lax.div(j, nb), 0, lax.rem(j, nb))),
            pl.BlockSpec((None, D, tn), lambda j, i: (half + lax.div(j, nb), 0, lax.rem(j, nb))),
        ],
        out_specs=[
            pl.BlockSpec((2, tm, tn), lambda j, i: (0, i, j)),
            pl.BlockSpec((tm, tn), lambda j, i: (i, j)),
        ],
        out_shape=[jax.ShapeDtypeStruct((2, S, F), BF16), jax.ShapeDtypeStruct((S, F), BF16)],
        compiler_params=_cparams(("parallel", "parallel"), 48),
    )(n, w_in, w_in)


def _mm_nn(a, b, resid, scale, *, tm, tn, tk, name):
    S, K = a.shape
    N = b.shape[1]
    nk = K // tk

    def body(*refs):
        if resid is None:
            a_ref, b_ref, o_ref, acc_ref = refs
            r_ref = None
        else:
            a_ref, b_ref, r_ref, o_ref, acc_ref = refs
        k = pl.program_id(2)

        @pl.when(k == 0)
        def _():
            acc_ref[...] = jnp.zeros_like(acc_ref)

        acc_ref[...] += _dot_nn(a_ref[...], b_ref[...])

        @pl.when(k == nk - 1)
        def _():
            out = acc_ref[...] * scale
            if r_ref is not None:
                out = r_ref[...] + out
            o_ref[...] = out

    in_specs = [pl.BlockSpec((tm, tk), lambda i, j, k: (i, k)), pl.BlockSpec((tk, tn), lambda i, j, k: (k, j))]
    args = [a, b]
    if resid is not None:
        in_specs.append(pl.BlockSpec((tm, tn), lambda i, j, k: (i, j)))
        args.append(resid)
    return pl.pallas_call(
        body, name=name, grid=(S // tm, N // tn, nk),
        in_specs=in_specs,
        out_specs=pl.BlockSpec((tm, tn), lambda i, j, k: (i, j)),
        out_shape=jax.ShapeDtypeStruct((S, N), F32),
        scratch_shapes=[pltpu.VMEM((tm, tn), F32)],
        compiler_params=_cparams(("parallel", "parallel", "arbitrary"), 48),
    )(*args)


def _mm_nt(a, b, after, *, tm, tn, name):
    S, K = a.shape
    N = b.shape[0]

    def body(a_ref, b_ref, after_ref, o_ref):
        o_ref[...] = _dot_nt(a_ref[...], b_ref[...])

    return pl.pallas_call(
        body, name=name, grid=(N // tn, S // tm),
        in_specs=[pl.BlockSpec((tm, K), lambda j, i: (i, 0)), pl.BlockSpec((tn, K), lambda j, i: (j, 0)),
                  pl.BlockSpec(memory_space=pl.ANY)],
        out_specs=pl.BlockSpec((tm, tn), lambda j, i: (i, j)),
        out_shape=jax.ShapeDtypeStruct((S, N), F32),
        compiler_params=_cparams(("parallel", "parallel"), 48),
    )(a, b, after)


def _mm_tn(a, b, *, grid, a_spec, b_spec, out_spec, out_shape, scale, name, a_is_transposed=False):
    dot = _dot_nn if a_is_transposed else _dot_tn

    def body(a_ref, b_ref, o_ref):
        o_ref[...] = (scale * dot(a_ref[...], b_ref[...])).astype(o_ref.dtype)

    return pl.pallas_call(
        body, name=name, grid=grid, in_specs=[a_spec, b_spec], out_specs=out_spec, out_shape=out_shape,
        compiler_params=_cparams(("parallel",) * len(grid), 56),
    )(a, b)


def _ffn_bwd_act(dhb, w_out, gu, after, *, name):
    S, D = dhb.shape
    F = w_out.shape[0]
    tm, tn = 256, COL_TILE

    def body(dh_ref, w_ref, gu_ref, after_ref, dgu_ref):
        da = 0.5 * _dot_nt(dh_ref[...], w_ref[...])
        g = gu_ref[0].astype(F32)
        u = gu_ref[1].astype(F32)
        s = _sigmoid(g)
        dgu_ref[0] = (da * u * (s * (1.0 + g * (1.0 - s)))).astype(BF16)
        dgu_ref[1] = (da * (g * s)).astype(BF16)

    return pl.pallas_call(
        body, name=name, grid=(F // tn, S // tm),
        in_specs=[
            pl.BlockSpec((tm, D), lambda j, i: (i, 0)),
            pl.BlockSpec((tn, D), lambda j, i: (j, 0)),
            pl.BlockSpec((2, tm, tn), lambda j, i: (0, i, j)),
            pl.BlockSpec(memory_space=pl.ANY),
        ],
        out_specs=pl.BlockSpec((2, tm, tn), lambda j, i: (0, i, j)),
        out_shape=jax.ShapeDtypeStruct((2, S, F), BF16),
        compiler_params=_cparams(("parallel", "parallel"), 48),
    )(dhb, w_out, gu, after)


def _mm_nt_rmsbwd(dact, w, h_in, dh_out, g, *, tk, name):
    ng, S, fg = dact.shape
    ns, D, cs = w.shape
    assert ng * fg == ns * cs
    tm, rc = 512, 64
    kpg, kps = fg // tk, cs // tk
    nk = ng * kpg

    def body(a_ref, w_ref, h_ref, dho_ref, g_ref, dh_ref, dhb_ref, dg_ref, acc_ref):
        i = pl.program_id(0)
        k = pl.program_id(1)

        @pl.when(k == 0)
        def _():
            acc_ref[...] = jnp.zeros_like(acc_ref)

        acc_ref[...] += _dot_nt(a_ref[...], w_ref[...])

        @pl.when(jnp.logical_and(i == 0, k == 0))
        def _():
            dg_ref[...] = jnp.zeros_like(dg_ref)

        @pl.when(k == nk - 1)
        def _():
            gv = g_ref[...]

            def rows_step(c, dg):
                rows = pl.ds(pl.multiple_of(c * rc, rc), rc)
                dn = acc_ref[rows, :]
                xv = h_ref[rows, :]
                r = lax.rsqrt(jnp.mean(xv * xv, axis=-1, keepdims=True) + EPS)
                xh = xv * r
                dng = dn * gv
                dx = r * (dng - xh * jnp.mean(dng * xh, axis=-1, keepdims=True))
                out = dho_ref[rows, :] + dx
                dh_ref[rows, :] = out
                dhb_ref[rows, :] = out.astype(BF16)
                return dg + jnp.sum(dn * xh, axis=0, keepdims=True)

            dg_ref[...] += lax.fori_loop(0, tm // rc, rows_step, jnp.zeros((1, D), F32))

    return pl.pallas_call(
        body, name=name, grid=(S // tm, nk),
        in_specs=[
            pl.BlockSpec((None, tm, tk), lambda i, k: (lax.div(k, kpg), i, lax.rem(k, kpg))),
            pl.BlockSpec((None, D, tk), lambda i, k: (lax.div(k, kps), 0, lax.rem(k, kps))),
            pl.BlockSpec((tm, D), lambda i, k: (i, 0)),
            pl.BlockSpec((tm, D), lambda i, k: (i, 0)),
            pl.BlockSpec((1, D), lambda i, k: (0, 0)),
        ],
        out_specs=[
            pl.BlockSpec((tm, D), lambda i, k: (i, 0)),
            pl.BlockSpec((tm, D), lambda i, k: (i, 0)),
            pl.BlockSpec((1, D), lambda i, k: (0, 0)),
        ],
        out_shape=[jax.ShapeDtypeStruct((S, D), F32), jax.ShapeDtypeStruct((S, D), BF16),
                   jax.ShapeDtypeStruct((1, D), F32)],
        scratch_shapes=[pltpu.VMEM((tm, D), F32)],
        compiler_params=_cparams(("arbitrary", "arbitrary"), 56),
    )(dact, w, h_in, dh_out, g)


def _final_loss(h, g, target, *, name):
    S, D = h.shape
    tm = 256

    def body(h_ref, g_ref, t_ref, loss_ref, dh_ref, dhb_ref, dg_ref):
        i = pl.program_id(0)

        @pl.when(i == 0)
        def _():
            loss_ref[...] = jnp.zeros_like(loss_ref)
            dg_ref[...] = jnp.zeros_like(dg_ref)

        xv = h_ref[...]
        gv = g_ref[...]
        r = lax.rsqrt(jnp.mean(xv * xv, axis=-1, keepdims=True) + EPS)
        xh = xv * r
        e = xh * gv - t_ref[...]
        loss_ref[...] += 0.5 * jnp.sum(jnp.mean(e * e, axis=-1, keepdims=True))
        dy = e * (1.0 / D)
        dg_ref[...] += jnp.sum(dy * xh, axis=0, keepdims=True)
        dyg = dy * gv
        dx = r * (dyg - xh * jnp.mean(dyg * xh, axis=-1, keepdims=True))
        dh_ref[...] = dx
        dhb_ref[...] = dx.astype(BF16)

    return pl.pallas_call(
        body, name=name, grid=(S // tm,),
        in_specs=[pl.BlockSpec((tm, D), lambda i: (i, 0)), pl.BlockSpec((1, D), lambda i: (0, 0)),
                  pl.BlockSpec((tm, D), lambda i: (i, 0))],
        out_specs=[pl.BlockSpec((8, LANES), lambda i: (0, 0)), pl.BlockSpec((tm, D), lambda i: (i, 0)),
                   pl.BlockSpec((tm, D), lambda i: (i, 0)), pl.BlockSpec((1, D), lambda i: (0, 0))],
        out_shape=[jax.ShapeDtypeStruct((8, LANES), F32), jax.ShapeDtypeStruct((S, D), F32),
                   jax.ShapeDtypeStruct((S, D), BF16), jax.ShapeDtypeStruct((1, D), F32)],
        compiler_params=_cparams(("arbitrary",), 40),
    )(h, g, target)


POOL_HALO = 16
POOL_ROWS = 256


def _pool_window_mean_minus_token(ext, tok0, w):
    s = ext
    k = 1
    while k < w:
        s = s + pltpu.roll(s, k, 0)
        k *= 2
    win = s[POOL_HALO:, :]
    tok = tok0 + lax.broadcasted_iota(jnp.int32, (POOL_ROWS, 1), 0)
    cnt = jnp.minimum(tok + 1, w).astype(F32)
    return win / cnt - ext[POOL_HALO:, :], cnt


def _pool_fwd(u, w_pool, scale, *, name):
    S = u.shape[0]
    C = POOL_GROUP_DIM
    nsteps = S // POOL_ROWS

    def body(p_ref, w_ref, sc_ref, y_ref, xp_ref):
        xp_ref[0:POOL_HALO, :] = jnp.zeros((POOL_HALO, D_POOL), F32)
        xp_ref[POOL_HALO:, :] = p_ref[...]
        for gi, win in enumerate(POOL_WINDOWS):
            cols = slice(gi * C, (gi + 1) * C)

            def step(c, carry, cols=cols, win=win, gi=gi):
                r0 = pl.multiple_of(c * POOL_ROWS, POOL_ROWS)
                ext = xp_ref[pl.ds(r0, POOL_ROWS + POOL_HALO), cols]
                pooled, _ = _pool_window_mean_minus_token(ext, r0, win)
                y = _dot_nn(pooled.astype(BF16), w_ref[gi]) * sc_ref[:, cols]
                y_ref[pl.ds(r0, POOL_ROWS), cols] = y.astype(BF16)
                return carry

            lax.fori_loop(0, nsteps, step, 0)

    return pl.pallas_call(
        body, name=name, grid=(1,),
        in_specs=[pl.BlockSpec((S, D_POOL), lambda i: (0, 0)),
                  pl.BlockSpec((4, C, C), lambda i: (0, 0, 0)),
                  pl.BlockSpec((1, D_POOL), lambda i: (0, 0))],
        out_specs=pl.BlockSpec((S, D_POOL), lambda i: (0, 0)),
        out_shape=jax.ShapeDtypeStruct((S, D_POOL), BF16),
        scratch_shapes=[pltpu.VMEM((S + POOL_HALO, D_POOL), F32)],
        compiler_params=_cparams(("arbitrary",), 48),
    )(u, w_pool, scale)


def _pool_bwd(u, dcat, w_pool, scale, *, name):
    S = u.shape[0]
    C = POOL_GROUP_DIM
    nsteps = S // POOL_ROWS

    def body(p_ref, dy_ref, w_ref, sc_ref, dp_ref, dw_ref, dsc_ref, xp_ref, e_ref, neg_ref):
        xp_ref[0:POOL_HALO, :] = jnp.zeros((POOL_HALO, D_POOL), F32)
        xp_ref[POOL_HALO:, :] = p_ref[...]
        e_ref[S:, :] = jnp.zeros((POOL_HALO, C), F32)
        for gi, win in enumerate(POOL_WINDOWS):
            cols = slice(gi * C, (gi + 1) * C)

            def step_a(c, carry, cols=cols, win=win, gi=gi):
                dw, dsc = carry
                r0 = pl.multiple_of(c * POOL_ROWS, POOL_ROWS)
                ext = xp_ref[pl.ds(r0, POOL_ROWS + POOL_HALO), cols]
                pooled, cnt = _pool_window_mean_minus_token(ext, r0, win)
                pb = pooled.astype(BF16)
                wv = w_ref[gi]
                dy = dy_ref[pl.ds(r0, POOL_ROWS), cols]
                dsc = dsc + jnp.sum(dy * _dot_nn(pb, wv), axis=0, keepdims=True)
                dyp = (dy * sc_ref[:, cols]).astype(BF16)
                dw = dw + _dot_tn(pb, dyp)
                dpooled = _dot_nt(dyp, wv)
                e_ref[pl.ds(r0, POOL_ROWS), :] = dpooled / cnt
                neg_ref[pl.ds(r0, POOL_ROWS), :] = -dpooled
                return dw, dsc

            dw, dsc = lax.fori_loop(0, nsteps, step_a, (jnp.zeros((C, C), F32), jnp.zeros((1, C), F32)))
            dw_ref[gi] = dw
            dsc_ref[:, cols] = dsc

            def step_b(c, carry, cols=cols, win=win):
                r0 = pl.multiple_of(c * POOL_ROWS, POOL_ROWS)
                s = e_ref[pl.ds(r0, POOL_ROWS + POOL_HALO), :]
                n = POOL_ROWS + POOL_HALO
                k = 1
                while k < win:
                    s = s + pltpu.roll(s, n - k, 0)
                    k *= 2
                du = s[:POOL_ROWS, :] + neg_ref[pl.ds(r0, POOL_ROWS), :]
                dp_ref[pl.ds(r0, POOL_ROWS), cols] = du.astype(BF16)
                return carry

            lax.fori_loop(0, nsteps, step_b, 0)

    return pl.pallas_call(
        body, name=name, grid=(1,),
        in_specs=[pl.BlockSpec((S, D_POOL), lambda i: (0, 0)),
                  pl.BlockSpec((S, D_POOL), lambda i: (0, 0)),
                  pl.BlockSpec((4, C, C), lambda i: (0, 0, 0)),
                  pl.BlockSpec((1, D_POOL), lambda i: (0, 0))],
        out_specs=[pl.BlockSpec((S, D_POOL), lambda i: (0, 0)),
                   pl.BlockSpec((4, C, C), lambda i: (0, 0, 0)),
                   pl.BlockSpec((1, D_POOL), lambda i: (0, 0))],
        out_shape=[jax.ShapeDtypeStruct((S, D_POOL), BF16), jax.ShapeDtypeStruct((4, C, C), F32),
                   jax.ShapeDtypeStruct((1, D_POOL), F32)],
        scratch_shapes=[pltpu.VMEM((S + POOL_HALO, D_POOL), F32), pltpu.VMEM((S + POOL_HALO, C), F32),
                        pltpu.VMEM((S, C), F32)],
        compiler_params=_cparams(("arbitrary",), 56),
    )(u, dcat, w_pool, scale)


GLA_ROWS = 128
U_Q_BLK, U_K_BLK = 2, 3
U_V_BLK, U_G_BLK = 2, 3
U_R_BLK = 32


def _prefix_sum_rows(x):
    n = x.shape[0]
    row = lax.broadcasted_iota(jnp.int32, x.shape, 0)
    k = 1
    while k < n:
        x = x + jnp.where(row >= k, pltpu.roll(x, k, 0), 0.0)
        k *= 2
    return x


def _suffix_sum_rows(x):
    n = x.shape[0]
    row = lax.broadcasted_iota(jnp.int32, x.shape, 0)
    k = 1
    while k < n:
        x = x + jnp.where(row < n - k, pltpu.roll(x, n - k, 0), 0.0)
        k *= 2
    return x


def _log_sigmoid(z):
    return jnp.minimum(z, 0.0) - jnp.log(1.0 + jnp.exp(-jnp.abs(z)))


def _gla_chunk_terms(la_c, q_c, k_c):
    bc = _prefix_sum_rows(la_c)
    bl = jnp.sum(la_c, axis=0, keepdims=True)
    eb = jnp.exp(bc)
    enb = jnp.exp(-bc)
    etail = jnp.exp(bl - bc)
    qd = q_c * (GLA_DK ** -0.5) * eb
    ki = k_c * enb
    kt = k_c * etail
    d = jnp.exp(bl)
    return eb, enb, etail, qd, ki, kt, d


def _gla_fwd(u, y_pool, w_alpha, b_alpha, gnorm, *, name):
    S = u.shape[0]
    RB = GLA_ROWS
    ncc = RB // CHUNK
    H, DK, DV = GLA_HEADS, GLA_DK, GLA_DV

    def body(q_ref, k_ref, v_ref, go_ref, r_ref, yp_ref, wa_ref, ba_ref, gn_ref, cat_ref, o_ref, st_ref, state):
        i = pl.program_id(0)

        @pl.when(i == 0)
        def _():
            state[...] = jnp.zeros_like(state)

        cat_ref[:, :D_POOL] = yp_ref[...]
        y_ref = cat_ref.at[:, D_POOL:]

        z = _dot_nn(r_ref[...].astype(BF16), wa_ref[...]) + ba_ref[...]
        la = _log_sigmoid(z) / GATE_LOGIT_NORMALIZER
        ri = lax.broadcasted_iota(jnp.int32, (CHUNK, CHUNK), 0)
        ci = lax.broadcasted_iota(jnp.int32, (CHUNK, CHUNK), 1)
        tri = ri >= ci
        gn = gn_ref[...]
        for cc in range(ncc):
            rs = slice(cc * CHUNK, (cc + 1) * CHUNK)
            for h in range(H):
                ks = slice(h * DK, (h + 1) * DK)
                vs = slice(h * DV, (h + 1) * DV)
                _, _, _, qd, ki, kt, d = _gla_chunk_terms(la[rs, ks], q_ref[rs, ks], k_ref[rs, ks])
                qdb = qd.astype(BF16)
                vb = v_ref[rs, vs].astype(BF16)
                p = jnp.where(tri, _dot_nt(qdb, ki.astype(BF16)), 0.0)
                st = state[h]
                st_ref[cc, h] = st
                o = _dot_nn(p.astype(BF16), vb) + _dot_nt(qdb, st.astype(BF16))
                state[h] = st * d + _dot_tn(vb, kt.astype(BF16))
                o_ref[rs, vs] = o
                rinv = lax.rsqrt(jnp.mean(o * o, axis=-1, keepdims=True) + EPS)
                go = go_ref[rs, vs]
                y_ref[rs, vs] = (o * rinv * gn * (go * _sigmoid(go))).astype(BF16)

    nblk = S // RB
    return pl.pallas_call(
        body, name=name, grid=(nblk,),
        in_specs=[
            pl.BlockSpec((RB, GLA_DK_TOTAL), lambda i: (i, U_Q_BLK)),
            pl.BlockSpec((RB, GLA_DK_TOTAL), lambda i: (i, U_K_BLK)),
            pl.BlockSpec((RB, D_GLA), lambda i: (i, U_V_BLK)),
            pl.BlockSpec((RB, D_GLA), lambda i: (i, U_G_BLK)),
            pl.BlockSpec((RB, LANES), lambda i: (i, U_R_BLK)),
            pl.BlockSpec((RB, D_POOL), lambda i: (i, 0)),
            pl.BlockSpec((LANES, GLA_DK_TOTAL), lambda i: (0, 0)),
            pl.BlockSpec((1, GLA_DK_TOTAL), lambda i: (0, 0)),
            pl.BlockSpec((1, DV), lambda i: (0, 0)),
        ],
        out_specs=[
            pl.BlockSpec((RB, D_POOL + D_GLA), lambda i: (i, 0)),
            pl.BlockSpec((RB, D_GLA), lambda i: (i, 0)),
            pl.BlockSpec((ncc, H, DV, DK), lambda i: (i, 0, 0, 0)),
        ],
        out_shape=[jax.ShapeDtypeStruct((S, D_POOL + D_GLA), BF16), jax.ShapeDtypeStruct((S, D_GLA), F32),
                   jax.ShapeDtypeStruct((S // CHUNK, H, DV, DK), F32)],
        scratch_shapes=[pltpu.VMEM((H, DV, DK), F32)],
        compiler_params=_cparams(("arbitrary",), 32),
    )(u, u, u, u, u, y_pool, w_alpha, b_alpha, gnorm)


def _gla_bwd(u, o, states, dcat, dpool, w_alpha, b_alpha, gnorm, *, name):
    S = u.shape[0]
    RB = GLA_ROWS
    ncc = RB // CHUNK
    H, DK, DV = GLA_HEADS, GLA_DK, GLA_DV
    nblk = S // RB
    o_q, o_k = D_POOL, D_POOL + GLA_DK_TOTAL
    o_v, o_g, o_r = o_k + GLA_DK_TOTAL, o_k + GLA_DK_TOTAL + D_GLA, o_k + GLA_DK_TOTAL + 2 * D_GLA

    def body(q_ref, k_ref, v_ref, go_ref, r_ref, o_ref, st_ref, dy_ref, dpool_ref, wa_ref, ba_ref, gn_ref,
             du_ref, dwa_ref, dba_ref, dgn_ref, dstate, dz_ref):
        i = pl.program_id(0)

        @pl.when(i == 0)
        def _():
            dstate[...] = jnp.zeros_like(dstate)
            dwa_ref[...] = jnp.zeros_like(dwa_ref)
            dba_ref[...] = jnp.zeros_like(dba_ref)
            dgn_ref[...] = jnp.zeros_like(dgn_ref)

        du_ref[:, :o_q] = dpool_ref[...]
        dq_ref, dk_ref = du_ref.at[:, o_q:o_k], du_ref.at[:, o_k:o_v]
        dv_ref, dgo_ref, dr_ref = du_ref.at[:, o_v:o_g], du_ref.at[:, o_g:o_r], du_ref.at[:, o_r:]

        rb = r_ref[...].astype(BF16)
        wa = wa_ref[...]
        z = _dot_nn(rb, wa) + ba_ref[...]
        la = _log_sigmoid(z) / GATE_LOGIT_NORMALIZER
        ri = lax.broadcasted_iota(jnp.int32, (CHUNK, CHUNK), 0)
        ci = lax.broadcasted_iota(jnp.int32, (CHUNK, CHUNK), 1)
        tri = ri >= ci
        last_row = lax.broadcasted_iota(jnp.int32, (CHUNK, DK), 0) == CHUNK - 1
        gn = gn_ref[...]
        dgn = jnp.zeros((1, DV), F32)
        for cc in reversed(range(ncc)):
            rs = slice(cc * CHUNK, (cc + 1) * CHUNK)
            for h in range(H):
                ks = slice(h * DK, (h + 1) * DK)
                vs = slice(h * DV, (h + 1) * DV)
                eb, enb, etail, qd, ki, kt, d = _gla_chunk_terms(la[rs, ks], q_ref[rs, ks], k_ref[rs, ks])
                qdb, kib, ktb = qd.astype(BF16), ki.astype(BF16), kt.astype(BF16)
                vb = v_ref[rs, vs].astype(BF16)
                p = jnp.where(tri, _dot_nt(qdb, kib), 0.0)
                ov = o_ref[rs, vs]
                go = go_ref[rs, vs]
                dy = dy_ref[rs, vs]
                rinv = lax.rsqrt(jnp.mean(ov * ov, axis=-1, keepdims=True) + EPS)
                oh = ov * rinv
                sg = _sigmoid(go)
                dgo_ref[rs, vs] = (dy * (oh * gn) * (sg * (1.0 + go * (1.0 - sg)))).astype(BF16)
                don = dy * (go * sg)
                dgn = dgn + jnp.sum(don * oh, axis=0, keepdims=True)
                doh = don * gn
                do = rinv * (doh - oh * jnp.mean(doh * oh, axis=-1, keepdims=True))
                dob = do.astype(BF16)
                st = st_ref[cc, h]
                dst = dstate[h]
                stb, dstb = st.astype(BF16), dst.astype(BF16)
                dp = jnp.where(tri, _dot_nt(dob, vb), 0.0).astype(BF16)
                dv_ref[rs, vs] = (_dot_tn(p.astype(BF16), dob) + _dot_nt(ktb, dstb)).astype(BF16)
                dqd = _dot_nn(dp, kib) + _dot_nn(dob, stb)
                dki = _dot_tn(dp, qdb)
                dkt = _dot_nn(vb, dstb)
                dd = jnp.sum(dst * st, axis=0, keepdims=True)
                dstate[h] = dst * d + _dot_tn(dob, qdb)
                dq_ref[rs, ks] = (dqd * eb * (DK ** -0.5)).astype(BF16)
                dk_ref[rs, ks] = (dki * enb + dkt * etail).astype(BF16)
                dbl = jnp.sum(dkt * kt, axis=0, keepdims=True) + dd * d
                dbc = dqd * qd - dki * ki - dkt * kt
                dbc = dbc + jnp.where(last_row, dbl, 0.0)
                dla = _suffix_sum_rows(dbc)
                dz_ref[rs, ks] = dla * (1.0 / GATE_LOGIT_NORMALIZER) * (1.0 - _sigmoid(z[rs, ks]))
        dz = dz_ref[...]
        dzb = dz.astype(BF16)
        dr_ref[...] = _dot_nt(dzb, wa).astype(BF16)
        dwa_ref[...] += _dot_tn(rb, dzb)
        dba_ref[...] += jnp.sum(dz, axis=0, keepdims=True)
        dgn_ref[...] += dgn

    def rev(blk):
        return lambda i: (nblk - 1 - i, blk)

    return pl.pallas_call(
        body, name=name, grid=(nblk,),
        in_specs=[
            pl.BlockSpec((RB, GLA_DK_TOTAL), rev(U_Q_BLK)),
            pl.BlockSpec((RB, GLA_DK_TOTAL), rev(U_K_BLK)),
            pl.BlockSpec((RB, D_GLA), rev(U_V_BLK)),
            pl.BlockSpec((RB, D_GLA), rev(U_G_BLK)),
            pl.BlockSpec((RB, LANES), rev(U_R_BLK)),
            pl.BlockSpec((RB, D_GLA), rev(0)),
            pl.BlockSpec((ncc, H, DV, DK), lambda i: (nblk - 1 - i, 0, 0, 0)),
            pl.BlockSpec((RB, D_GLA), rev(1)),
            pl.BlockSpec((RB, D_POOL), rev(0)),
            pl.BlockSpec((LANES, GLA_DK_TOTAL), lambda i: (0, 0)),
            pl.BlockSpec((1, GLA_DK_TOTAL), lambda i: (0, 0)),
            pl.BlockSpec((1, DV), lambda i: (0, 0)),
        ],
        out_specs=[
            pl.BlockSpec((RB, D_IN_PAD), rev(0)),
            pl.BlockSpec((LANES, GLA_DK_TOTAL), lambda i: (0, 0)),
            pl.BlockSpec((1, GLA_DK_TOTAL), lambda i: (0, 0)),
            pl.BlockSpec((1, DV), lambda i: (0, 0)),
        ],
        out_shape=[
            jax.ShapeDtypeStruct((S, D_IN_PAD), BF16),
            jax.ShapeDtypeStruct((LANES, GLA_DK_TOTAL), F32), jax.ShapeDtypeStruct((1, GLA_DK_TOTAL), F32),
            jax.ShapeDtypeStruct((1, DV), F32),
        ],
        scratch_shapes=[pltpu.VMEM((H, DV, DK), F32), pltpu.VMEM((RB, GLA_DK_TOTAL), F32)],
        compiler_params=_cparams(("arbitrary",), 32),
    )(u, u, u, u, u, o, states, dcat, dpool, w_alpha, b_alpha, gnorm)


def _row_tile(rows, cols, itemsize, budget=2 * 1024 * 1024):
    if rows * cols * itemsize <= budget or rows % 16:
        return rows
    best = 16
    for t in range(16, rows + 1, 16):
        if rows % t == 0 and t * cols * itemsize <= budget:
            best = t
    return best


def _adamw(w, g, m, v, *, name):
    R, C = w.shape
    tr = _row_tile(R, C, 4, budget=1024 * 1024)

    def body(w_ref, g_ref, m_ref, v_ref, go_ref, d_ref, nm_ref, nv_ref):
        gv = g_ref[...]
        go_ref[...] = gv
        mn = ADAM_B1 * m_ref[...] + (1.0 - ADAM_B1) * gv
        vn = ADAM_B2 * v_ref[...] + (1.0 - ADAM_B2) * jnp.square(gv)
        m_hat = mn / (1.0 - ADAM_B1 ** ADAM_STEP)
        v_hat = vn / (1.0 - ADAM_B2 ** ADAM_STEP)
        d_ref[...] = -ADAM_LR * (m_hat / (jnp.sqrt(v_hat) + ADAM_EPS) + ADAM_WD * w_ref[...])
        nm_ref[...] = mn
        nv_ref[...] = vn

    spec = pl.BlockSpec((tr, C), lambda i: (i, 0))
    shp = jax.ShapeDtypeStruct((R, C), F32)
    return pl.pallas_call(
        body, name=name, grid=(R // tr,), in_specs=[spec] * 4, out_specs=[spec] * 4, out_shape=[shp] * 4,
        compiler_params=_cparams(("parallel",), 32),
    )(w, g, m, v)


def _pair_sum(g4, recv, c_idx, *, name):
    ns, _, R2, C = g4.shape
    tr = _row_tile(R2, C, 2)

    def body(c_ref, g_ref, r_ref, o_ref):
        o_ref[...] = (g_ref[...].astype(F32) + r_ref[...].astype(F32)).astype(BF16)

    return pl.pallas_call(
        body, name=name,
        grid_spec=pltpu.PrefetchScalarGridSpec(
            num_scalar_prefetch=1, grid=(ns, R2 // tr),
            in_specs=[pl.BlockSpec((None, None, tr, C), lambda s, i, c: (s, c[0], i, 0)),
                      pl.BlockSpec((None, tr, C), lambda s, i, c: (s, i, 0))],
            out_specs=pl.BlockSpec((None, tr, C), lambda s, i, c: (s, i, 0)),
        ),
        out_shape=jax.ShapeDtypeStruct((ns, R2, C), BF16),
        compiler_params=_cparams(("parallel", "parallel"), 32),
    )(c_idx, g4, recv)


def _chip_sum(part, recv, sc_idx, *, name):
    _, R2, C = part.shape
    tr = _row_tile(R2, C, 4)
    nblk = R2 // tr

    def body(s_ref, p_ref, r_ref, o_ref):
        acc = p_ref[...].astype(F32)
        for j in range(N_CHIPS - 1):
            acc = acc + r_ref[j].astype(F32)
        o_ref[...] = acc

    return pl.pallas_call(
        body, name=name,
        grid_spec=pltpu.PrefetchScalarGridSpec(
            num_scalar_prefetch=1, grid=(nblk,),
            in_specs=[pl.BlockSpec((None, tr, C), lambda i, s: (s[0], i, 0)),
                      pl.BlockSpec((N_CHIPS - 1, tr, C), lambda i, s: (0, i, 0))],
            out_specs=pl.BlockSpec((tr, C), lambda i, s: (s[1] * nblk + i, 0)),
        ),
        out_shape=jax.ShapeDtypeStruct((2 * R2, C), F32),
        compiler_params=_cparams(("parallel",), 32),
    )(sc_idx, part, recv)


def _cast_into_slot(w, sc_idx, dtype, after, *, name):
    R, C = w.shape
    tr = _row_tile(R, C, 4)

    def body(s_ref, w_ref, after_ref, o_ref):
        o_ref[...] = w_ref[...].astype(dtype)

    return pl.pallas_call(
        body, name=name,
        grid_spec=pltpu.PrefetchScalarGridSpec(
            num_scalar_prefetch=1, grid=(R // tr,),
            in_specs=[pl.BlockSpec((tr, C), lambda i, s: (i, 0)), pl.BlockSpec(memory_space=pl.ANY)],
            out_specs=pl.BlockSpec((None, tr, C), lambda i, s: (s[0], i, 0)),
        ),
        out_shape=jax.ShapeDtypeStruct((N_CHIPS, R, C), dtype),
        compiler_params=_cparams(("parallel",), 32),
    )(sc_idx, w, after)


def _slab_sum(slabs, *, name):
    n, M, C = slabs.shape

    def body(x_ref, o_ref):
        acc = x_ref[0]
        for d in range(1, n):
            acc = acc + x_ref[d]
        o_ref[...] = acc

    return pl.pallas_call(
        body, name=name, out_shape=jax.ShapeDtypeStruct((M, C), F32),
    )(slabs)


def _mesh_position():
    x, y, c = lax.axis_index("x"), lax.axis_index("y"), lax.axis_index("c")
    other_chips = [(1 - x, y), (x, 1 - y), (1 - x, 1 - y)]
    return x, y, c, other_chips


ANY = pl.BlockSpec(memory_space=pl.ANY)


HBM = pl.BlockSpec(memory_space=pltpu.HBM)
SEM = pl.BlockSpec(memory_space=pltpu.SEMAPHORE)
SPLIT_COPY = pltpu.CompilerParams(has_side_effects=pltpu.SideEffectType.DATAFLOW_SIDE_EFFECTING)
TOKEN = jax.ShapeDtypeStruct((8, LANES), F32)


def _in_hbm(a):
    return pltpu.with_memory_space_constraint(a, pltpu.HBM)


def _half_rows(ref, slot, half):
    hr = ref.shape[1] // 2
    return ref.at[slot, pl.ds(half * hr, hr), :]


def _gather_ici_start(groups, *, name):
    flat = [b for g in groups for b in g]
    K, G = len(flat), len(groups)

    def body(*refs):
        ins, sems, token = refs[:K], refs[K:K + 2 * G], refs[-1]
        x, y, c, chips = _mesh_position()
        s = 2 * x + y
        k = 0
        for gi, g in enumerate(groups):
            for n in range(len(g)):
                own = _half_rows(ins[k], s, c)
                for j, chip in enumerate(chips):
                    pltpu.make_async_remote_copy(
                        src_ref=own, dst_ref=own, send_sem=sems[2 * gi].at[n * (N_CHIPS - 1) + j],
                        recv_sem=sems[2 * gi + 1].at[n * (N_CHIPS - 1) + j],
                        device_id=(*chip, c), device_id_type=MESH).start()
                k += 1
        token[...] = jnp.zeros_like(token)

    sem_shapes = []
    for g in groups:
        sem_shapes += [pltpu.SemaphoreType.DMA((len(g) * (N_CHIPS - 1),))] * 2
    out = pl.pallas_call(
        body, name=name,
        in_specs=[HBM] * K,
        out_specs=[SEM] * (2 * G) + [HBM] * K + [pl.BlockSpec(memory_space=pltpu.VMEM)],
        out_shape=sem_shapes + [pltpu.HBM(b.shape, b.dtype) for b in flat] + [TOKEN],
        input_output_aliases={k: 2 * G + k for k in range(K)},
        compiler_params=SPLIT_COPY,
    )(*[_in_hbm(b) for b in flat])
    handles, k = [], 2 * G
    for gi, g in enumerate(groups):
        handles.append((out[2 * gi], out[2 * gi + 1], list(out[k:k + len(g)])))
        k += len(g)
    return handles, out[-1]


def _gather_ici_wait(handle, after, *, name):
    send, recv, bufs = handle
    n = len(bufs)

    def body(*refs):
        ins, send_ref, recv_ref = refs[:n], refs[n], refs[n + 1]
        x, y, c, chips = _mesh_position()
        s = 2 * x + y
        for k in range(n):
            own = _half_rows(ins[k], s, c)
            for j, chip in enumerate(chips):
                cp = pltpu.make_async_remote_copy(
                    src_ref=own, dst_ref=_half_rows(ins[k], 2 * chip[0] + chip[1], c),
                    send_sem=send_ref.at[k * (N_CHIPS - 1) + j], recv_sem=recv_ref.at[k * (N_CHIPS - 1) + j],
                    device_id=(*chip, c), device_id_type=MESH)
                cp.wait_send()
                cp.wait_recv()

    return pl.pallas_call(
        body, name=name,
        in_specs=[HBM] * n + [SEM, SEM, ANY], out_specs=[HBM] * n,
        out_shape=[pltpu.HBM(b.shape, b.dtype) for b in bufs],
        input_output_aliases={k: k for k in range(n)},
        compiler_params=SPLIT_COPY,
    )(*bufs, send, recv, after)


def _forward_halves(bufs, *, name):
    K = len(bufs)
    per = N_CHIPS - 1

    def body(*refs):
        outs = refs[K:2 * K]
        send_sems, recv_sems = refs[2 * K:]
        x, y, c, chips = _mesh_position()
        copies = []
        for k in range(K):
            for j, chip in enumerate(chips):
                got = _half_rows(outs[k], 2 * chip[0] + chip[1], c)
                cp = pltpu.make_async_remote_copy(
                    src_ref=got, dst_ref=got, send_sem=send_sems.at[k * per + j], recv_sem=recv_sems.at[k * per + j],
                    device_id=(x, y, 1 - c), device_id_type=MESH)
                cp.start()
                copies.append(cp)
        for cp in copies:
            cp.wait()

    return pl.pallas_call(
        body, name=name,
        in_specs=[ANY] * K, out_specs=[ANY] * K,
        out_shape=[jax.ShapeDtypeStruct(a.shape, a.dtype) for a in bufs],
        input_output_aliases={k: k for k in range(K)},
        scratch_shapes=[pltpu.SemaphoreType.DMA((K * per,)), pltpu.SemaphoreType.DMA((K * per,))],
    )(*bufs)


def _chip_exchange_copies(srcs, lands, send_sems, recv_sems):
    x, y, c, chips = _mesh_position()
    per = N_CHIPS - 1
    return [pltpu.make_async_remote_copy(
        src_ref=srcs[k].at[2 * chip[0] + chip[1]], dst_ref=lands[k].at[j],
        send_sem=send_sems.at[k * per + j], recv_sem=recv_sems.at[k * per + j],
        device_id=(*chip, c), device_id_type=MESH) for k in range(len(srcs)) for j, chip in enumerate(chips)]


def _sibling_swap_copies(srcs, lands, send_sems, recv_sems):
    x, y, c, _ = _mesh_position()
    return [pltpu.make_async_remote_copy(
        src_ref=srcs[k].at[pl.ds(0, srcs[k].shape[0]), 1 - c], dst_ref=lands[k],
        send_sem=send_sems.at[k], recv_sem=recv_sems.at[k],
        device_id=(x, y, 1 - c), device_id_type=MESH) for k in range(len(srcs))]


def _split_copy_start(srcs, land_shapes, n_sems, copies, *, name):
    K = len(srcs)

    def body(*refs):
        for cp in copies(refs[:K], refs[K:2 * K], refs[2 * K], refs[2 * K + 1]):
            cp.start()
        refs[-1][...] = jnp.zeros_like(refs[-1])

    out = pl.pallas_call(
        body, name=name,
        in_specs=[HBM] * (2 * K),
        out_specs=[SEM, SEM] + [HBM] * (2 * K) + [pl.BlockSpec(memory_space=pltpu.VMEM)],
        out_shape=[pltpu.SemaphoreType.DMA((n_sems,))] * 2
        + [pltpu.HBM(a.shape, a.dtype) for a in srcs]
        + [pltpu.HBM(s, a.dtype) for s, a in zip(land_shapes, srcs)] + [TOKEN],
        input_output_aliases={k: 2 + k for k in range(2 * K)},
        compiler_params=SPLIT_COPY,
    )(*[_in_hbm(a) for a in srcs], *[_in_hbm(lax.empty(s, a.dtype)) for s, a in zip(land_shapes, srcs)])
    return (out[0], out[1], list(out[2:2 + K]), list(out[2 + K:2 + 2 * K])), out[-1]


def _split_copy_wait(handle, copies, after, *, name):
    send, recv, srcs, lands = handle
    K = len(srcs)

    def body(*refs):
        for cp in copies(refs[:K], refs[K:2 * K], refs[2 * K], refs[2 * K + 1]):
            cp.wait_send()
            cp.wait_recv()

    out = pl.pallas_call(
        body, name=name,
        in_specs=[HBM] * (2 * K) + [SEM, SEM, ANY], out_specs=[HBM] * (2 * K),
        out_shape=[pltpu.HBM(a.shape, a.dtype) for a in srcs] + [pltpu.HBM(a.shape, a.dtype) for a in lands],
        input_output_aliases={k: k for k in range(2 * K)},
        compiler_params=SPLIT_COPY,
    )(*srcs, *lands, send, recv, after)
    return list(out[:K]), list(out[K:])


def _join_halves(bufs, *, name):
    K = len(bufs)

    def body(*refs):
        outs = refs[K:2 * K]
        send_sems, recv_sems = refs[2 * K:]
        x, y, c, _ = _mesh_position()
        copies = []
        for k in range(K):
            r2 = bufs[k].shape[0] // 2
            mine = outs[k].at[pl.ds(c * r2, r2), :]
            cp = pltpu.make_async_remote_copy(
                src_ref=mine, dst_ref=mine, send_sem=send_sems.at[k], recv_sem=recv_sems.at[k],
                device_id=(x, y, 1 - c), device_id_type=MESH)
            cp.start()
            copies.append(cp)
        for cp in copies:
            cp.wait()

    return pl.pallas_call(
        body, name=name,
        in_specs=[ANY] * K, out_specs=[ANY] * K,
        out_shape=[jax.ShapeDtypeStruct(a.shape, a.dtype) for a in bufs],
        input_output_aliases={k: k for k in range(K)},
        scratch_shapes=[pltpu.SemaphoreType.DMA((K,)), pltpu.SemaphoreType.DMA((K,))],
    )(*bufs)


def _all_gather_slab(slab):
    m_per, n = slab.shape

    def body(x_ref, out_ref, send_sems, recv_sems, local_sem):
        x, y, c, chips = _mesh_position()
        me, sibling = (x, y, c), (x, y, 1 - c)

        def rows(px, py, pc):
            return out_ref.at[pl.ds((4 * px + 2 * py + pc) * m_per, m_per), :]

        def copy(k, block, to, src=None):
            return pltpu.make_async_remote_copy(
                src_ref=rows(*block) if src is None else src, dst_ref=rows(*block),
                send_sem=send_sems.at[k], recv_sem=recv_sems.at[k], device_id=to, device_id_type=MESH)

        mine = pltpu.make_async_copy(x_ref, rows(*me), local_sem)
        mine.start()
        first = [copy(0, me, sibling, src=x_ref)]
        first += [copy(1 + j, me, (*chip, c), src=x_ref) for j, chip in enumerate(chips)]
        for cp in first:
            cp.start()
        passed = [copy(4 + j, (*chip, c), sibling) for j, chip in enumerate(chips)]
        for j, chip in enumerate(chips):
            copy(1 + j, (*chip, c), me).wait_recv()
            passed[j].start()
        copy(0, sibling, me).wait_recv()
        for j, chip in enumerate(chips):
            copy(4 + j, (*chip, 1 - c), me).wait_recv()
        for cp in first + passed:
            cp.wait_send()
        mine.wait()

    return pl.pallas_call(
        body, name="gather_small_grads",
        out_shape=jax.ShapeDtypeStruct((N_DEV * m_per, n), slab.dtype),
        in_specs=[pl.BlockSpec(memory_space=pltpu.VMEM)],
        out_specs=pl.BlockSpec(memory_space=pltpu.VMEM),
        scratch_shapes=[pltpu.SemaphoreType.DMA((7,)), pltpu.SemaphoreType.DMA((7,)), pltpu.SemaphoreType.DMA],
    )(slab)


def _ffn_backward_weights(dhb, saved, w_in, w_out, after, tag):
    n, gu, a = saved
    S = n.shape[0]
    ns, D, cs = w_in.shape
    F = w_out.shape[0]
    dgu = _ffn_bwd_act(dhb, w_out, gu, after, name=f"{tag}_bwd_act")
    to = 512
    dw_out = _mm_tn(
        a, dhb, grid=(F // to,),
        a_spec=pl.BlockSpec((S, to), lambda j: (0, j)), b_spec=pl.BlockSpec((S, D), lambda j: (0, 0)),
        out_spec=pl.BlockSpec((to, D), lambda j: (j, 0)), out_shape=jax.ShapeDtypeStruct((F, D), BF16),
        scale=0.5, name=f"{tag}_dw_out")
    ti = MXU_TILE
    per_g, per_s = F // ti, cs // ti
    dw_in = _mm_tn(
        n.T, dgu, grid=(2 * F // ti,), a_is_transposed=True,
        a_spec=pl.BlockSpec((D, S), lambda j: (0, 0)),
        b_spec=pl.BlockSpec((None, S, ti), lambda j: (lax.div(j, per_g), 0, lax.rem(j, per_g))),
        out_spec=pl.BlockSpec((None, D, ti), lambda j: (lax.div(j, per_s), 0, lax.rem(j, per_s))),
        out_shape=jax.ShapeDtypeStruct((ns, D, cs), BF16), scale=1.0, name=f"{tag}_dw_in")
    return dgu, dw_in, dw_out.reshape(N_CHIPS, F // N_CHIPS, D)


def kernel(x, ffn1_norm, ffn1_w_in, ffn1_w_out, mix_norm, w_in_mix, w_pool, pool_scale, w_alpha, b_alpha, gla_norm, w_out_mix, ffn2_norm, ffn2_w_in, ffn2_w_out, final_norm, loss_target, m_ffn1_norm, m_ffn1_w_in, m_ffn1_w_out, m_mix_norm, m_w_in_mix, m_w_pool, m_pool_scale, m_w_alpha, m_b_alpha, m_gla_norm, m_w_out_mix, m_ffn2_norm, m_ffn2_w_in, m_ffn2_w_out, m_final_norm, v_ffn1_norm, v_ffn1_w_in, v_ffn1_w_out, v_mix_norm, v_w_in_mix, v_w_pool, v_pool_scale, v_w_alpha, v_b_alpha, v_gla_norm, v_w_out_mix, v_ffn2_norm, v_ffn2_w_in, v_ffn2_w_out, v_final_norm):
    names = ["ffn1_norm", "ffn1_w_in", "ffn1_w_out", "mix_norm", "w_in_mix", "w_pool", "pool_scale", "w_alpha",
             "b_alpha", "gla_norm", "w_out_mix", "ffn2_norm", "ffn2_w_in", "ffn2_w_out", "final_norm"]
    weights = dict(zip(names, [ffn1_norm, ffn1_w_in, ffn1_w_out, mix_norm, w_in_mix, w_pool, pool_scale, w_alpha,
                               b_alpha, gla_norm, w_out_mix, ffn2_norm, ffn2_w_in, ffn2_w_out, final_norm]))
    moms = dict(zip(names, [m_ffn1_norm, m_ffn1_w_in, m_ffn1_w_out, m_mix_norm, m_w_in_mix, m_w_pool, m_pool_scale,
                            m_w_alpha, m_b_alpha, m_gla_norm, m_w_out_mix, m_ffn2_norm, m_ffn2_w_in, m_ffn2_w_out,
                            m_final_norm]))
    vels = dict(zip(names, [v_ffn1_norm, v_ffn1_w_in, v_ffn1_w_out, v_mix_norm, v_w_in_mix, v_w_pool, v_pool_scale,
                            v_w_alpha, v_b_alpha, v_gla_norm, v_w_out_mix, v_ffn2_norm, v_ffn2_w_in, v_ffn2_w_out,
                            v_final_norm]))
    xi, yi, ci = lax.axis_index("x"), lax.axis_index("y"), lax.axis_index("c")
    chip = 2 * xi + yi
    c_idx = jnp.reshape(ci, (1,)).astype(jnp.int32)
    sc_idx = jnp.stack([chip, ci]).astype(jnp.int32)

    def flat2d(a):
        return a.reshape(-1, a.shape[-1])

    ex = _Exchanges(sc_idx, c_idx)

    def cast(n, after):
        return _cast_into_slot(flat2d(weights[n]), sc_idx, F32 if n == "w_alpha" else BF16, after, name=f"cast_{n}")

    first, rest = _Exchanges.GATHER_GROUPS[:2], _Exchanges.GATHER_GROUPS[2:]
    tok = ex.start_gather({n: cast(n, sc_idx) for g in first for n in g}, first, name="gather_ici_start_ffn1")
    tok2 = ex.start_gather({n: cast(n, tok) for g in rest for n in g}, rest, name="gather_ici_start_rest")
    small_params = dict(g1=ffn1_norm, gm=mix_norm, g2=ffn2_norm, gf=final_norm.reshape(1, D_MODEL),
                        pool_scale=pool_scale, b_alpha=b_alpha, gla_norm=gla_norm)
    loss_blk, dx, small = _forward_backward(x[0], loss_target[0], ex, tok[0, 0] + tok2[0, 0], small_params)

    outs = {}

    def update(n, g):
        w = weights[n]
        w2 = flat2d(w) if w.ndim > 1 else w.reshape(1, -1)
        go, d, nm, nv = _adamw(w2, g.reshape(w2.shape), moms[n].reshape(w2.shape), vels[n].reshape(w2.shape),
                               name=f"adamw_{n}")
        outs[n] = (go.reshape(w.shape), d.reshape(w.shape), nm.reshape(w.shape), nv.reshape(w.shape))
        return nv

    last = dx
    for tag in _Exchanges.REDUCE_ORDER:
        for n, g in ex.reduced(tag, after=last).items():
            last = update(n, g)

    grads = {}
    small_names = ["ffn1_norm", "mix_norm", "ffn2_norm", "final_norm", "pool_scale", "b_alpha", "gla_norm", "w_alpha",
                   "loss"]
    small = small + [loss_blk[0:1]]
    rows = [a.size // LANES for a in small]
    slab = jnp.concatenate([a.reshape(-1, LANES) for a in small], axis=0)
    pad = -slab.shape[0] % 8
    slab = jnp.pad(slab, ((0, pad), (0, 0)))
    gathered = _all_gather_slab(slab).reshape(N_DEV, slab.shape[0], LANES)
    total = _slab_sum(gathered, name="sum_small_grads")
    off = 0
    for n, a, r in zip(small_names, small, rows):
        grads[n] = total[off:off + r].reshape(a.shape)
        off += r
    grads["w_alpha"] = lax.dynamic_slice_in_dim(grads["w_alpha"], chip * (GLA_DK_TOTAL // N_CHIPS),
                                                GLA_DK_TOTAL // N_CHIPS, axis=1)

    loss = grads.pop("loss")[0, 0]
    for n in small_names[:-1]:
        update(n, grads[n])
    return (loss, dx[None], *[outs[n][0] for n in names], *[outs[n][1] for n in names],
            *[outs[n][2] for n in names], *[outs[n][3] for n in names])


class _Exchanges:
    GATHER_GROUPS = (("ffn1_w_in",), ("ffn1_w_out",), ("w_in_mix", "w_out_mix", "w_pool", "w_alpha"),
                     ("ffn2_w_in", "ffn2_w_out"))
    REDUCE_ORDER = ("ffn2", "mix", "ffn1")

    def __init__(self, sc_idx, c_idx):
        self.sc_idx, self.c_idx = sc_idx, c_idx
        self._gathers, self._swaps, self._reduces = {}, {}, {}

    def start_gather(self, bufs, groups, *, name):
        handles, token = _gather_ici_start([[bufs[n] for n in g] for g in groups], name=name)
        for g, h in zip(groups, handles):
            self._gathers[g[0]] = (g, h)
        return token

    def gathered(self, first, after):
        names, handle = self._gathers.pop(first)
        got = _gather_ici_wait(handle, after, name=f"gather_ici_wait_{first}")
        return dict(zip(names, _forward_halves(got, name=f"gather_forward_{first}")))

    def begin_reduce(self, tag, full):
        g4 = [a.reshape(N_CHIPS, 2, a.shape[1] // 2, a.shape[2]) for a in full.values()]
        lands = [(a.shape[0],) + a.shape[2:] for a in g4]
        handle, token = _split_copy_start(g4, lands, len(g4), _sibling_swap_copies, name=f"swap_start_{tag}")
        self._swaps[tag] = (list(full), handle)
        return token

    def start_reduce(self, tag, after):
        names, handle = self._swaps.pop(tag)
        g4, from_sibling = _split_copy_wait(handle, _sibling_swap_copies, after, name=f"swap_wait_{tag}")
        pair = [_pair_sum(a, b, self.c_idx, name=f"pair_sum_{n}") for n, a, b in zip(names, g4, from_sibling)]
        lands = [(N_CHIPS - 1,) + a.shape[1:] for a in pair]
        handle, token = _split_copy_start(pair, lands, len(pair) * (N_CHIPS - 1), _chip_exchange_copies,
                                          name=f"exchange_start_{tag}")
        self._reduces[tag] = (names, handle)
        return token

    def reduced(self, tag, after):
        names, handle = self._reduces.pop(tag)
        pair, lands = _split_copy_wait(handle, _chip_exchange_copies, after, name=f"exchange_wait_{tag}")
        halves = [_chip_sum(a, b, self.sc_idx, name=f"chip_sum_{n}") for n, a, b in zip(names, pair, lands)]
        return dict(zip(names, _join_halves(halves, name=f"join_halves_{tag}")))


def _forward_backward(h0, target, ex, started, sp):
    g1, gm, g2, gf = sp["g1"], sp["gm"], sp["g2"], sp["gf"]
    pool_scale, b_alpha, gla_norm = sp["pool_scale"], sp["b_alpha"], sp["gla_norm"]
    cs_mix = D_IN // N_CHIPS

    n1 = _rms_fwd(h0, g1 + started, name="ffn1_norm")
    w1_in = ex.gathered("ffn1_w_in", after=n1)["ffn1_w_in"]
    gu1, a1 = _ffn_up(n1, w1_in, name="ffn1_up")
    w1_out = ex.gathered("ffn1_w_out", after=a1)["ffn1_w_out"].reshape(D_FF, D_MODEL)
    h1 = _mm_nn(a1, w1_out, h0, 0.5, tm=512, tn=D_MODEL, tk=COL_TILE, name="ffn1_down")
    saved1 = (n1, gu1, a1)
    n_mix = _rms_fwd(h1, gm, name="mix_norm")
    gw = ex.gathered("w_in_mix", after=n_mix)
    w_omix = gw["w_out_mix"].reshape(D_MODEL, D_MODEL)
    w_mix = jnp.concatenate([gw["w_in_mix"][t] for t in range(N_CHIPS)], axis=1)
    w_mix = jnp.pad(w_mix, ((0, 0), (0, D_IN_PAD - D_IN)))[None]
    wp = gw["w_pool"].reshape(N_CHIPS, 4, POOL_GROUP_DIM // N_CHIPS, POOL_GROUP_DIM)
    wp = wp.transpose(1, 0, 2, 3).reshape(4, POOL_GROUP_DIM, POOL_GROUP_DIM)
    wa = gw["w_alpha"].transpose(1, 0, 2).reshape(GLA_GATE_RANK, GLA_DK_TOTAL)
    wa = jnp.pad(wa, ((0, LANES - GLA_GATE_RANK), (0, 0))).astype(BF16)
    u = _mm_nn(n_mix, w_mix[0], None, 1.0, tm=512, tn=1408, tk=D_MODEL, name="mix_in")
    y_pool = _pool_fwd(u, wp, pool_scale, name="pool_fwd")
    cat, o_gla, states = _gla_fwd(u, y_pool, wa, b_alpha, gla_norm, name="gla_fwd")
    h2 = _mm_nn(cat, w_omix, h1, 1.0, tm=512, tn=D_MODEL, tk=1024, name="mix_out")
    n3 = _rms_fwd(h2, g2, name="ffn2_norm")
    gw2 = ex.gathered("ffn2_w_in", after=n3)
    w2_in, w2_out = gw2["ffn2_w_in"], gw2["ffn2_w_out"].reshape(D_FF, D_MODEL)
    gu2, a2 = _ffn_up(n3, w2_in, name="ffn2_up")
    h3 = _mm_nn(a2, w2_out, h2, 0.5, tm=512, tn=D_MODEL, tk=COL_TILE, name="ffn2_down")
    saved2 = (n3, gu2, a2)
    loss_blk, dh3, dh3b, d_gf = _final_loss(h3, gf, target, name="final_loss")

    dgu2, dw2_in, dw2_out = _ffn_backward_weights(dh3b, saved2, w2_in, w2_out, dh3b, "ffn2")
    tok = ex.begin_reduce("ffn2", {"ffn2_w_in": dw2_in, "ffn2_w_out": dw2_out})
    dh2, dh2b, d_g2 = _mm_nt_rmsbwd(dgu2, w2_in, h2, dh3, g2 + tok[0, 0], tk=COL_TILE, name="ffn2_dx")
    tok = ex.start_reduce("ffn2", after=dh2b)
    S = h0.shape[0]
    dcat = _mm_nt(dh2b, w_omix, tok, tm=512, tn=1024, name="mix_out_dx")
    dw_omix = _mm_tn(
        cat, dh2b, grid=(4,),
        a_spec=pl.BlockSpec((S, 512), lambda j: (0, j)), b_spec=pl.BlockSpec((S, D_MODEL), lambda j: (0, 0)),
        out_spec=pl.BlockSpec((512, D_MODEL), lambda j: (j, 0)),
        out_shape=jax.ShapeDtypeStruct((D_MODEL, D_MODEL), BF16), scale=1.0, name="mix_out_dw")
    dp, dw_pool, d_pscale = _pool_bwd(u, dcat, wp, pool_scale, name="pool_bwd")
    du, d_wa, d_ba, d_gn = _gla_bwd(u, o_gla, states, dcat, dp, wa, b_alpha, gla_norm, name="gla_bwd")
    du = du[None]
    tn_mix = COL_TILE
    dw_mix = _mm_tn(
        n_mix.T, du, grid=(2, D_IN_PAD // tn_mix), a_is_transposed=True,
        a_spec=pl.BlockSpec((D_MODEL // 2, S), lambda i, j: (i, 0)),
        b_spec=pl.BlockSpec((None, S, tn_mix), lambda i, j: (0, 0, j)),
        out_spec=pl.BlockSpec((D_MODEL // 2, tn_mix), lambda i, j: (i, j)),
        out_shape=jax.ShapeDtypeStruct((D_MODEL, D_IN_PAD), BF16), scale=1.0, name="mix_in_dw")
    dw_mix_s = dw_mix[:, :D_IN].reshape(D_MODEL, N_CHIPS, cs_mix).transpose(1, 0, 2)
    dw_pool_s = dw_pool.reshape(4, N_CHIPS, POOL_GROUP_DIM // N_CHIPS, POOL_GROUP_DIM).transpose(1, 0, 2, 3)
    dw_pool_s = dw_pool_s.reshape(N_CHIPS, POOL_GROUP_DIM, POOL_GROUP_DIM).astype(BF16)
    tok = ex.begin_reduce("mix", {"w_in_mix": dw_mix_s,
                                  "w_out_mix": dw_omix.reshape(N_CHIPS, D_MODEL // N_CHIPS, D_MODEL),
                                  "w_pool": dw_pool_s})
    dh1, dh1b, d_gm = _mm_nt_rmsbwd(du, w_mix, h1, dh2, gm + tok[0, 0], tk=tn_mix, name="mix_in_dx")
    tok = ex.start_reduce("mix", after=dh1b)
    dgu1, dw1_in, dw1_out = _ffn_backward_weights(dh1b, saved1, w1_in, w1_out, tok, "ffn1")
    tok = ex.begin_reduce("ffn1", {"ffn1_w_in": dw1_in, "ffn1_w_out": dw1_out})
    tok = ex.start_reduce("ffn1", after=tok)
    dx, _, d_g1 = _mm_nt_rmsbwd(dgu1, w1_in, h0, dh1, g1 + tok[0, 0], tk=COL_TILE, name="ffn1_dx")
    small = [d_g1, d_gm, d_g2, d_gf, d_pscale, d_ba, d_gn, d_wa[:GLA_GATE_RANK]]
    return loss_blk, dx, small
```

```python
import functools

import jax
import jax.numpy as jnp
from jax import lax
from jax.experimental import pallas as pl
from jax.experimental.pallas import tpu as pltpu

F32 = jnp.float32
BF16 = jnp.bfloat16
MESH = pl.DeviceIdType.MESH

D_MODEL = 2048
D_FF = 5632
D_POOL = 1024
POOL_WINDOWS = (2, 4, 8, 16)
POOL_GROUP_DIM = 256
D_GLA = 1024
GLA_HEADS = 4
GLA_DV = 256
GLA_DK = 128
GLA_DK_TOTAL = 512
GLA_GATE_RANK = 16
GATE_LOGIT_NORMALIZER = 16.0
CHUNK = 64
D_IN = 4112
D_IN_PAD = 4224
EPS = 1e-6

ADAM_LR = 0.001
ADAM_B1 = 0.9
ADAM_B2 = 0.999
ADAM_EPS = 1e-08
ADAM_WD = 0.01
ADAM_STEP = 10

N_CHIPS = 4
N_DEV = 8
V7X_VMEM_BYTES = 64 * 1024 * 1024
LANES = 128
MXU_TILE = 256
COL_TILE = 1408


def _cparams(semantics, vmem_mb):
    assert vmem_mb * 1024 * 1024 < V7X_VMEM_BYTES
    return pltpu.CompilerParams(dimension_semantics=semantics, vmem_limit_bytes=vmem_mb * 1024 * 1024)


def _dot_nn(a, b):
    return jnp.dot(a, b, preferred_element_type=F32)


def _dot_nt(a, b):
    return lax.dot_general(a, b, (((1,), (1,)), ((), ())), preferred_element_type=F32)


def _dot_tn(a, b):
    return lax.dot_general(a, b, (((0,), (0,)), ((), ())), preferred_element_type=F32)


def _sigmoid(x):
    return 1.0 / (1.0 + jnp.exp(-x))


def _rms_fwd(x, g, *, name):
    S, D = x.shape
    tm = 256

    def body(x_ref, g_ref, o_ref):
        xv = x_ref[...]
        r = lax.rsqrt(jnp.mean(xv * xv, axis=-1, keepdims=True) + EPS)
        o_ref[...] = (xv * r * g_ref[...]).astype(BF16)

    return pl.pallas_call(
        body, name=name, grid=(S // tm,),
        in_specs=[pl.BlockSpec((tm, D), lambda i: (i, 0)), pl.BlockSpec((1, D), lambda i: (0, 0))],
        out_specs=pl.BlockSpec((tm, D), lambda i: (i, 0)),
        out_shape=jax.ShapeDtypeStruct((S, D), BF16),
        compiler_params=_cparams(("parallel",), 32),
    )(x, g)


def _ffn_up(n, w_in, *, name):
    S, D = n.shape
    ns, _, cs = w_in.shape
    half = ns // 2
    F = cs * half
    tm, tn = 256, COL_TILE
    nb = cs // tn

    def body(n_ref, wg_ref, wu_ref, gu_ref, a_ref):
        nv = n_ref[...]
        g = _dot_nn(nv, wg_ref[...])
        u = _dot_nn(nv, wu_ref[...])
        gu_ref[0] = g.astype(BF16)
        gu_ref[1] = u.astype(BF16)
        a_ref[...] = (g * _sigmoid(g) * u).astype(BF16)

    return pl.pallas_call(
        body, name=name, grid=(F // tn, S // tm),
        in_specs=[
            pl.BlockSpec((tm, D), lambda j, i: (i, 0)),
            pl.BlockSpec((None, D, tn), lambda j, i: (lax.div(j, nb), 0, lax.rem(j, nb))),
            pl.BlockSpec((None, D, tn), lambda j, i: (half + lax.div(j, nb), 0, lax.rem(j, nb))),
        ],
        out_specs=[
            pl.BlockSpec((2, tm, tn), lambda j, i: (0, i, j)),
            pl.BlockSpec((tm, tn), lambda j, i: (i, j)),
        ],
        out_shape=[jax.ShapeDtypeStruct((2, S, F), BF16), jax.ShapeDtypeStruct((S, F), BF16)],
        compiler_params=_cparams(("parallel", "parallel"), 48),
    )(n, w_in, w_in)


def _mm_nn(a, b, resid, scale, *, tm, tn, tk, name):
    S, K = a.shape
    N = b.shape[1]
    nk = K // tk

    def body(*refs):
        if resid is None:
            a_ref, b_ref, o_ref, acc_ref = refs
            r_ref = None
        else:
            a_ref, b_ref, r_ref, o_ref, acc_ref = refs
        k = pl.program_id(2)

        @pl.when(k == 0)
        def _():
            acc_ref[...] = jnp.zeros_like(acc_ref)

        acc_ref[...] += _dot_nn(a_ref[...], b_ref[...])

        @pl.when(k == nk - 1)
        def _():
            out = acc_ref[...] * scale
            if r_ref is not None:
                out = r_ref[...] + out
            o_ref[...] = out

    in_specs = [pl.BlockSpec((tm, tk), lambda i, j, k: (i, k)), pl.BlockSpec((tk, tn), lambda i, j, k: (k, j))]
    args = [a, b]
    if resid is not None:
        in_specs.append(pl.BlockSpec((tm, tn), lambda i, j, k: (i, j)))
        args.append(resid)
    return pl.pallas_call(
        body, name=name, grid=(S // tm, N // tn, nk),
        in_specs=in_specs,
        out_specs=pl.BlockSpec((tm, tn), lambda i, j, k: (i, j)),
        out_shape=jax.ShapeDtypeStruct((S, N), F32),
        scratch_shapes=[pltpu.VMEM((tm, tn), F32)],
        compiler_params=_cparams(("parallel", "parallel", "arbitrary"), 48),
    )(*args)


def _mm_nt(a, b, after, *, tm, tn, name):
    S, K = a.shape
    N = b.shape[0]

    def body(a_ref, b_ref, after_ref, o_ref):
        o_ref[...] = _dot_nt(a_ref[...], b_ref[...])

    return pl.pallas_call(
        body, name=name, grid=(N // tn, S // tm),
        in_specs=[pl.BlockSpec((tm, K), lambda j, i: (i, 0)), pl.BlockSpec((tn, K), lambda j, i: (j, 0)),
                  pl.BlockSpec(memory_space=pl.ANY)],
        out_specs=pl.BlockSpec((tm, tn), lambda j, i: (i, j)),
        out_shape=jax.ShapeDtypeStruct((S, N), F32),
        compiler_params=_cparams(("parallel", "parallel"), 48),
    )(a, b, after)


def _mm_tn(a, b, *, grid, a_spec, b_spec, out_spec, out_shape, scale, name, a_is_transposed=False):
    dot = _dot_nn if a_is_transposed else _dot_tn

    def body(a_ref, b_ref, o_ref):
        o_ref[...] = (scale * dot(a_ref[...], b_ref[...])).astype(o_ref.dtype)

    return pl.pallas_call(
        body, name=name, grid=grid, in_specs=[a_spec, b_spec], out_specs=out_spec, out_shape=out_shape,
        compiler_params=_cparams(("parallel",) * len(grid), 56),
    )(a, b)


def _ffn_bwd_act(dhb, w_out, gu, after, *, name):
    S, D = dhb.shape
    F = w_out.shape[0]
    tm, tn = 256, COL_TILE

    def body(dh_ref, w_ref, gu_ref, after_ref, dgu_ref):
        da = 0.5 * _dot_nt(dh_ref[...], w_ref[...])
        g = gu_ref[0].astype(F32)
        u = gu_ref[1].astype(F32)
        s = _sigmoid(g)
        dgu_ref[0] = (da * u * (s * (1.0 + g * (1.0 - s)))).astype(BF16)
        dgu_ref[1] = (da * (g * s)).astype(BF16)

    return pl.pallas_call(
        body, name=name, grid=(F // tn, S // tm),
        in_specs=[
            pl.BlockSpec((tm, D), lambda j, i: (i, 0)),
            pl.BlockSpec((tn, D), lambda j, i: (j, 0)),
            pl.BlockSpec((2, tm, tn), lambda j, i: (0, i, j)),
            pl.BlockSpec(memory_space=pl.ANY),
        ],
        out_specs=pl.BlockSpec((2, tm, tn), lambda j, i: (0, i, j)),
        out_shape=jax.ShapeDtypeStruct((2, S, F), BF16),
        compiler_params=_cparams(("parallel", "parallel"), 48),
    )(dhb, w_out, gu, after)


def _mm_nt_rmsbwd(dact, w, h_in, dh_out, g, *, tk, name):
    ng, S, fg = dact.shape
    ns, D, cs = w.shape
    assert ng * fg == ns * cs
    tm, rc = 512, 64
    kpg, kps = fg // tk, cs // tk
    nk = ng * kpg

    def body(a_ref, w_ref, h_ref, dho_ref, g_ref, dh_ref, dhb_ref, dg_ref, acc_ref):
        i = pl.program_id(0)
        k = pl.program_id(1)

        @pl.when(k == 0)
        def _():
            acc_ref[...] = jnp.zeros_like(acc_ref)

        acc_ref[...] += _dot_nt(a_ref[...], w_ref[...])

        @pl.when(jnp.logical_and(i == 0, k == 0))
        def _():
            dg_ref[...] = jnp.zeros_like(dg_ref)

        @pl.when(k == nk - 1)
        def _():
            gv = g_ref[...]

            def rows_step(c, dg):
                rows = pl.ds(pl.multiple_of(c * rc, rc), rc)
                dn = acc_ref[rows, :]
                xv = h_ref[rows, :]
                r = lax.rsqrt(jnp.mean(xv * xv, axis=-1, keepdims=True) + EPS)
                xh = xv * r
                dng = dn * gv
                dx = r * (dng - xh * jnp.mean(dng * xh, axis=-1, keepdims=True))
                out = dho_ref[rows, :] + dx
                dh_ref[rows, :] = out
                dhb_ref[rows, :] = out.astype(BF16)
                return dg + jnp.sum(dn * xh, axis=0, keepdims=True)

            dg_ref[...] += lax.fori_loop(0, tm // rc, rows_step, jnp.zeros((1, D), F32))

    return pl.pallas_call(
        body, name=name, grid=(S // tm, nk),
        in_specs=[
            pl.BlockSpec((None, tm, tk), lambda i, k: (lax.div(k, kpg), i, lax.rem(k, kpg))),
            pl.BlockSpec((None, D, tk), lambda i, k: (lax.div(k, kps), 0, lax.rem(k, kps))),
            pl.BlockSpec((tm, D), lambda i, k: (i, 0)),
            pl.BlockSpec((tm, D), lambda i, k: (i, 0)),
            pl.BlockSpec((1, D), lambda i, k: (0, 0)),
        ],
        out_specs=[
            pl.BlockSpec((tm, D), lambda i, k: (i, 0)),
            pl.BlockSpec((tm, D), lambda i, k: (i, 0)),
            pl.BlockSpec((1, D), lambda i, k: (0, 0)),
        ],
        out_shape=[jax.ShapeDtypeStruct((S, D), F32), jax.ShapeDtypeStruct((S, D), BF16),
                   jax.ShapeDtypeStruct((1, D), F32)],
        scratch_shapes=[pltpu.VMEM((tm, D), F32)],
        compiler_params=_cparams(("arbitrary", "arbitrary"), 56),
    )(dact, w, h_in, dh_out, g)


def _final_loss(h, g, target, *, name):
    S, D = h.shape
    tm = 256

    def body(h_ref, g_ref, t_ref, loss_ref, dh_ref, dhb_ref, dg_ref):
        i = pl.program_id(0)

        @pl.when(i == 0)
        def _():
            loss_ref[...] = jnp.zeros_like(loss_ref)
            dg_ref[...] = jnp.zeros_like(dg_ref)

        xv = h_ref[...]
        gv = g_ref[...]
        r = lax.rsqrt(jnp.mean(xv * xv, axis=-1, keepdims=True) + EPS)
        xh = xv * r
        e = xh * gv - t_ref[...]
        loss_ref[...] += 0.5 * jnp.sum(jnp.mean(e * e, axis=-1, keepdims=True))
        dy = e * (1.0 / D)
        dg_ref[...] += jnp.sum(dy * xh, axis=0, keepdims=True)
        dyg = dy * gv
        dx = r * (dyg - xh * jnp.mean(dyg * xh, axis=-1, keepdims=True))
        dh_ref[...] = dx
        dhb_ref[...] = dx.astype(BF16)

    return pl.pallas_call(
        body, name=name, grid=(S // tm,),
        in_specs=[pl.BlockSpec((tm, D), lambda i: (i, 0)), pl.BlockSpec((1, D), lambda i: (0, 0)),
                  pl.BlockSpec((tm, D), lambda i: (i, 0))],
        out_specs=[pl.BlockSpec((8, LANES), lambda i: (0, 0)), pl.BlockSpec((tm, D), lambda i: (i, 0)),
                   pl.BlockSpec((tm, D), lambda i: (i, 0)), pl.BlockSpec((1, D), lambda i: (0, 0))],
        out_shape=[jax.ShapeDtypeStruct((8, LANES), F32), jax.ShapeDtypeStruct((S, D), F32),
                   jax.ShapeDtypeStruct((S, D), BF16), jax.ShapeDtypeStruct((1, D), F32)],
        compiler_params=_cparams(("arbitrary",), 40),
    )(h, g, target)


POOL_HALO = 16
POOL_ROWS = 256


def _pool_window_mean_minus_token(ext, tok0, w):
    s = ext
    k = 1
    while k < w:
        s = s + pltpu.roll(s, k, 0)
        k *= 2
    win = s[POOL_HALO:, :]
    tok = tok0 + lax.broadcasted_iota(jnp.int32, (POOL_ROWS, 1), 0)
    cnt = jnp.minimum(tok + 1, w).astype(F32)
    return win / cnt - ext[POOL_HALO:, :], cnt


def _pool_fwd(u, w_pool, scale, *, name):
    S = u.shape[0]
    C = POOL_GROUP_DIM
    nsteps = S // POOL_ROWS

    def body(p_ref, w_ref, sc_ref, y_ref, xp_ref):
        xp_ref[0:POOL_HALO, :] = jnp.zeros((POOL_HALO, D_POOL), F32)
        xp_ref[POOL_HALO:, :] = p_ref[...]
        for gi, win in enumerate(POOL_WINDOWS):
            cols = slice(gi * C, (gi + 1) * C)

            def step(c, carry, cols=cols, win=win, gi=gi):
                r0 = pl.multiple_of(c * POOL_ROWS, POOL_ROWS)
                ext = xp_ref[pl.ds(r0, POOL_ROWS + POOL_HALO), cols]
                pooled, _ = _pool_window_mean_minus_token(ext, r0, win)
                y = _dot_nn(pooled.astype(BF16), w_ref[gi]) * sc_ref[:, cols]
                y_ref[pl.ds(r0, POOL_ROWS), cols] = y.astype(BF16)
                return carry

            lax.fori_loop(0, nsteps, step, 0)

    return pl.pallas_call(
        body, name=name, grid=(1,),
        in_specs=[pl.BlockSpec((S, D_POOL), lambda i: (0, 0)),
                  pl.BlockSpec((4, C, C), lambda i: (0, 0, 0)),
                  pl.BlockSpec((1, D_POOL), lambda i: (0, 0))],
        out_specs=pl.BlockSpec((S, D_POOL), lambda i: (0, 0)),
        out_shape=jax.ShapeDtypeStruct((S, D_POOL), BF16),
        scratch_shapes=[pltpu.VMEM((S + POOL_HALO, D_POOL), F32)],
        compiler_params=_cparams(("arbitrary",), 48),
    )(u, w_pool, scale)


def _pool_bwd(u, dcat, w_pool, scale, *, name):
    S = u.shape[0]
    C = POOL_GROUP_DIM
    nsteps = S // POOL_ROWS

    def body(p_ref, dy_ref, w_ref, sc_ref, dp_ref, dw_ref, dsc_ref, xp_ref, e_ref, neg_ref):
        xp_ref[0:POOL_HALO, :] = jnp.zeros((POOL_HALO, D_POOL), F32)
        xp_ref[POOL_HALO:, :] = p_ref[...]
        e_ref[S:, :] = jnp.zeros((POOL_HALO, C), F32)
        for gi, win in enumerate(POOL_WINDOWS):
            cols = slice(gi * C, (gi + 1) * C)

            def step_a(c, carry, cols=cols, win=win, gi=gi):
                dw, dsc = carry
                r0 = pl.multiple_of(c * POOL_ROWS, POOL_ROWS)
                ext = xp_ref[pl.ds(r0, POOL_ROWS + POOL_HALO), cols]
                pooled, cnt = _pool_window_mean_minus_token(ext, r0, win)
                pb = pooled.astype(BF16)
                wv = w_ref[gi]
                dy = dy_ref[pl.ds(r0, POOL_ROWS), cols]
                dsc = dsc + jnp.sum(dy * _dot_nn(pb, wv), axis=0, keepdims=True)
                dyp = (dy * sc_ref[:, cols]).astype(BF16)
                dw = dw + _dot_tn(pb, dyp)
                dpooled = _dot_nt(dyp, wv)
                e_ref[pl.ds(r0, POOL_ROWS), :] = dpooled / cnt
                neg_ref[pl.ds(r0, POOL_ROWS), :] = -dpooled
                return dw, dsc

            dw, dsc = lax.fori_loop(0, nsteps, step_a, (jnp.zeros((C, C), F32), jnp.zeros((1, C), F32)))
            dw_ref[gi] = dw
            dsc_ref[:, cols] = dsc

            def step_b(c, carry, cols=cols, win=win):
                r0 = pl.multiple_of(c * POOL_ROWS, POOL_ROWS)
                s = e_ref[pl.ds(r0, POOL_ROWS + POOL_HALO), :]
                n = POOL_ROWS + POOL_HALO
                k = 1
                while k < win:
                    s = s + pltpu.roll(s, n - k, 0)
                    k *= 2
                du = s[:POOL_ROWS, :] + neg_ref[pl.ds(r0, POOL_ROWS), :]
                dp_ref[pl.ds(r0, POOL_ROWS), cols] = du.astype(BF16)
                return carry

            lax.fori_loop(0, nsteps, step_b, 0)

    return pl.pallas_call(
        body, name=name, grid=(1,),
        in_specs=[pl.BlockSpec((S, D_POOL), lambda i: (0, 0)),
                  pl.BlockSpec((S, D_POOL), lambda i: (0, 0)),
                  pl.BlockSpec((4, C, C), lambda i: (0, 0, 0)),
                  pl.BlockSpec((1, D_POOL), lambda i: (0, 0))],
        out_specs=[pl.BlockSpec((S, D_POOL), lambda i: (0, 0)),
                   pl.BlockSpec((4, C, C), lambda i: (0, 0, 0)),
                   pl.BlockSpec((1, D_POOL), lambda i: (0, 0))],
        out_shape=[jax.ShapeDtypeStruct((S, D_POOL), BF16), jax.ShapeDtypeStruct((4, C, C), F32),
                   jax.ShapeDtypeStruct((1, D_POOL), F32)],
        scratch_shapes=[pltpu.VMEM((S + POOL_HALO, D_POOL), F32), pltpu.VMEM((S + POOL_HALO, C), F32),
                        pltpu.VMEM((S, C), F32)],
        compiler_params=_cparams(("arbitrary",), 56),
    )(u, dcat, w_pool, scale)


GLA_ROWS = 128
U_Q_BLK, U_K_BLK = 2, 3
U_V_BLK, U_G_BLK = 2, 3
U_R_BLK = 32


def _prefix_sum_rows(x):
    n = x.shape[0]
    row = lax.broadcasted_iota(jnp.int32, x.shape, 0)
    k = 1
    while k < n:
        x = x + jnp.where(row >= k, pltpu.roll(x, k, 0), 0.0)
        k *= 2
    return x


def _suffix_sum_rows(x):
    n = x.shape[0]
    row = lax.broadcasted_iota(jnp.int32, x.shape, 0)
    k = 1
    while k < n:
        x = x + jnp.where(row < n - k, pltpu.roll(x, n - k, 0), 0.0)
        k *= 2
    return x


def _log_sigmoid(z):
    return jnp.minimum(z, 0.0) - jnp.log(1.0 + jnp.exp(-jnp.abs(z)))


def _gla_chunk_terms(la_c, q_c, k_c):
    bc = _prefix_sum_rows(la_c)
    bl = jnp.sum(la_c, axis=0, keepdims=True)
    eb = jnp.exp(bc)
    enb = jnp.exp(-bc)
    etail = jnp.exp(bl - bc)
    qd = q_c * (GLA_DK ** -0.5) * eb
    ki = k_c * enb
    kt = k_c * etail
    d = jnp.exp(bl)
    return eb, enb, etail, qd, ki, kt, d


def _gla_fwd(u, y_pool, w_alpha, b_alpha, gnorm, *, name):
    S = u.shape[0]
    RB = GLA_ROWS
    ncc = RB // CHUNK
    H, DK, DV = GLA_HEADS, GLA_DK, GLA_DV

    def body(q_ref, k_ref, v_ref, go_ref, r_ref, yp_ref, wa_ref, ba_ref, gn_ref, cat_ref, o_ref, st_ref, state):
        i = pl.program_id(0)

        @pl.when(i == 0)
        def _():
            state[...] = jnp.zeros_like(state)

        cat_ref[:, :D_POOL] = yp_ref[...]
        y_ref = cat_ref.at[:, D_POOL:]

        z = _dot_nn(r_ref[...].astype(BF16), wa_ref[...]) + ba_ref[...]
        la = _log_sigmoid(z) / GATE_LOGIT_NORMALIZER
        ri = lax.broadcasted_iota(jnp.int32, (CHUNK, CHUNK), 0)
        ci = lax.broadcasted_iota(jnp.int32, (CHUNK, CHUNK), 1)
        tri = ri >= ci
        gn = gn_ref[...]
        for cc in range(ncc):
            rs = slice(cc * CHUNK, (cc + 1) * CHUNK)
            for h in range(H):
                ks = slice(h * DK, (h + 1) * DK)
                vs = slice(h * DV, (h + 1) * DV)
                _, _, _, qd, ki, kt, d = _gla_chunk_terms(la[rs, ks], q_ref[rs, ks], k_ref[rs, ks])
                qdb = qd.astype(BF16)
                vb = v_ref[rs, vs].astype(BF16)
                p = jnp.where(tri, _dot_nt(qdb, ki.astype(BF16)), 0.0)
                st = state[h]
                st_ref[cc, h] = st
                o = _dot_nn(p.astype(BF16), vb) + _dot_nt(qdb, st.astype(BF16))
                state[h] = st * d + _dot_tn(vb, kt.astype(BF16))
                o_ref[rs, vs] = o
                rinv = lax.rsqrt(jnp.mean(o * o, axis=-1, keepdims=True) + EPS)
                go = go_ref[rs, vs]
                y_ref[rs, vs] = (o * rinv * gn * (go * _sigmoid(go))).astype(BF16)

    nblk = S // RB
    return pl.pallas_call(
        body, name=name, grid=(nblk,),
        in_specs=[
            pl.BlockSpec((RB, GLA_DK_TOTAL), lambda i: (i, U_Q_BLK)),
            pl.BlockSpec((RB, GLA_DK_TOTAL), lambda i: (i, U_K_BLK)),
            pl.BlockSpec((RB, D_GLA), lambda i: (i, U_V_BLK)),
            pl.BlockSpec((RB, D_GLA), lambda i: (i, U_G_BLK)),
            pl.BlockSpec((RB, LANES), lambda i: (i, U_R_BLK)),
            pl.BlockSpec((RB, D_POOL), lambda i: (i, 0)),
            pl.BlockSpec((LANES, GLA_DK_TOTAL), lambda i: (0, 0)),
            pl.BlockSpec((1, GLA_DK_TOTAL), lambda i: (0, 0)),
            pl.BlockSpec((1, DV), lambda i: (0, 0)),
        ],
        out_specs=[
            pl.BlockSpec((RB, D_POOL + D_GLA), lambda i: (i, 0)),
            pl.BlockSpec((RB, D_GLA), lambda i: (i, 0)),
            pl.BlockSpec((ncc, H, DV, DK), lambda i: (i, 0, 0, 0)),
        ],
        out_shape=[jax.ShapeDtypeStruct((S, D_POOL + D_GLA), BF16), jax.ShapeDtypeStruct((S, D_GLA), F32),
                   jax.ShapeDtypeStruct((S // CHUNK, H, DV, DK), F32)],
        scratch_shapes=[pltpu.VMEM((H, DV, DK), F32)],
        compiler_params=_cparams(("arbitrary",), 32),
    )(u, u, u, u, u, y_pool, w_alpha, b_alpha, gnorm)


def _gla_bwd(u, o, states, dcat, dpool, w_alpha, b_alpha, gnorm, *, name):
    S = u.shape[0]
    RB = GLA_ROWS
    ncc = RB // CHUNK
    H, DK, DV = GLA_HEADS, GLA_DK, GLA_DV
    nblk = S // RB
    o_q, o_k = D_POOL, D_POOL + GLA_DK_TOTAL
    o_v, o_g, o_r = o_k + GLA_DK_TOTAL, o_k + GLA_DK_TOTAL + D_GLA, o_k + GLA_DK_TOTAL + 2 * D_GLA

    def body(q_ref, k_ref, v_ref, go_ref, r_ref, o_ref, st_ref, dy_ref, dpool_ref, wa_ref, ba_ref, gn_ref,
             du_ref, dwa_ref, dba_ref, dgn_ref, dstate, dz_ref):
        i = pl.program_id(0)

        @pl.when(i == 0)
        def _():
            dstate[...] = jnp.zeros_like(dstate)
            dwa_ref[...] = jnp.zeros_like(dwa_ref)
            dba_ref[...] = jnp.zeros_like(dba_ref)
            dgn_ref[...] = jnp.zeros_like(dgn_ref)

        du_ref[:, :o_q] = dpool_ref[...]
        dq_ref, dk_ref = du_ref.at[:, o_q:o_k], du_ref.at[:, o_k:o_v]
        dv_ref, dgo_ref, dr_ref = du_ref.at[:, o_v:o_g], du_ref.at[:, o_g:o_r], du_ref.at[:, o_r:]

        rb = r_ref[...].astype(BF16)
        wa = wa_ref[...]
        z = _dot_nn(rb, wa) + ba_ref[...]
        la = _log_sigmoid(z) / GATE_LOGIT_NORMALIZER
        ri = lax.broadcasted_iota(jnp.int32, (CHUNK, CHUNK), 0)
        ci = lax.broadcasted_iota(jnp.int32, (CHUNK, CHUNK), 1)
        tri = ri >= ci
        last_row = lax.broadcasted_iota(jnp.int32, (CHUNK, DK), 0) == CHUNK - 1
        gn = gn_ref[...]
        dgn = jnp.zeros((1, DV), F32)
        for cc in reversed(range(ncc)):
            rs = slice(cc * CHUNK, (cc + 1) * CHUNK)
            for h in range(H):
                ks = slice(h * DK, (h + 1) * DK)
                vs = slice(h * DV, (h + 1) * DV)
                eb, enb, etail, qd, ki, kt, d = _gla_chunk_terms(la[rs, ks], q_ref[rs, ks], k_ref[rs, ks])
                qdb, kib, ktb = qd.astype(BF16), ki.astype(BF16), kt.astype(BF16)
                vb = v_ref[rs, vs].astype(BF16)
                p = jnp.where(tri, _dot_nt(qdb, kib), 0.0)
                ov = o_ref[rs, vs]
                go = go_ref[rs, vs]
                dy = dy_ref[rs, vs]
                rinv = lax.rsqrt(jnp.mean(ov * ov, axis=-1, keepdims=True) + EPS)
                oh = ov * rinv
                sg = _sigmoid(go)
                dgo_ref[rs, vs] = (dy * (oh * gn) * (sg * (1.0 + go * (1.0 - sg)))).astype(BF16)
                don = dy * (go * sg)
                dgn = dgn + jnp.sum(don * oh, axis=0, keepdims=True)
                doh = don * gn
                do = rinv * (doh - oh * jnp.mean(doh * oh, axis=-1, keepdims=True))
                dob = do.astype(BF16)
                st = st_ref[cc, h]
                dst = dstate[h]
                stb, dstb = st.astype(BF16), dst.astype(BF16)
                dp = jnp.where(tri, _dot_nt(dob, vb), 0.0).astype(BF16)
                dv_ref[rs, vs] = (_dot_tn(p.astype(BF16), dob) + _dot_nt(ktb, dstb)).astype(BF16)
                dqd = _dot_nn(dp, kib) + _dot_nn(dob, stb)
                dki = _dot_tn(dp, qdb)
                dkt = _dot_nn(vb, dstb)
                dd = jnp.sum(dst * st, axis=0, keepdims=True)
                dstate[h] = dst * d + _dot_tn(dob, qdb)
                dq_ref[rs, ks] = (dqd * eb * (DK ** -0.5)).astype(BF16)
                dk_ref[rs, ks] = (dki * enb + dkt * etail).astype(BF16)
                dbl = jnp.sum(dkt * kt, axis=0, keepdims=True) + dd * d
                dbc = dqd * qd - dki * ki - dkt * kt
                dbc = dbc + jnp.where(last_row, dbl, 0.0)
                dla = _suffix_sum_rows(dbc)
                dz_ref[rs, ks] = dla * (1.0 / GATE_LOGIT_NORMALIZER) * (1.0 - _sigmoid(z[rs, ks]))
        dz = dz_ref[...]
        dzb = dz.astype(BF16)
        dr_ref[...] = _dot_nt(dzb, wa).astype(BF16)
        dwa_ref[...] += _dot_tn(rb, dzb)
        dba_ref[...] += jnp.sum(dz, axis=0, keepdims=True)
        dgn_ref[...] += dgn

    def rev(blk):
        return lambda i: (nblk - 1 - i, blk)

    return pl.pallas_call(
        body, name=name, grid=(nblk,),
        in_specs=[
            pl.BlockSpec((RB, GLA_DK_TOTAL), rev(U_Q_BLK)),
            pl.BlockSpec((RB, GLA_DK_TOTAL), rev(U_K_BLK)),
            pl.BlockSpec((RB, D_GLA), rev(U_V_BLK)),
            pl.BlockSpec((RB, D_GLA), rev(U_G_BLK)),
            pl.BlockSpec((RB, LANES), rev(U_R_BLK)),
            pl.BlockSpec((RB, D_GLA), rev(0)),
            pl.BlockSpec((ncc, H, DV, DK), lambda i: (nblk - 1 - i, 0, 0, 0)),
            pl.BlockSpec((RB, D_GLA), rev(1)),
            pl.BlockSpec((RB, D_POOL), rev(0)),
            pl.BlockSpec((LANES, GLA_DK_TOTAL), lambda i: (0, 0)),
            pl.BlockSpec((1, GLA_DK_TOTAL), lambda i: (0, 0)),
            pl.BlockSpec((1, DV), lambda i: (0, 0)),
        ],
        out_specs=[
            pl.BlockSpec((RB, D_IN_PAD), rev(0)),
            pl.BlockSpec((LANES, GLA_DK_TOTAL), lambda i: (0, 0)),
            pl.BlockSpec((1, GLA_DK_TOTAL), lambda i: (0, 0)),
            pl.BlockSpec((1, DV), lambda i: (0, 0)),
        ],
        out_shape=[
            jax.ShapeDtypeStruct((S, D_IN_PAD), BF16),
            jax.ShapeDtypeStruct((LANES, GLA_DK_TOTAL), F32), jax.ShapeDtypeStruct((1, GLA_DK_TOTAL), F32),
            jax.ShapeDtypeStruct((1, DV), F32),
        ],
        scratch_shapes=[pltpu.VMEM((H, DV, DK), F32), pltpu.VMEM((RB, GLA_DK_TOTAL), F32)],
        compiler_params=_cparams(("arbitrary",), 32),
    )(u, u, u, u, u, o, states, dcat, dpool, w_alpha, b_alpha, gnorm)


def _row_tile(rows, cols, itemsize, budget=2 * 1024 * 1024):
    if rows * cols * itemsize <= budget or rows % 16:
        return rows
    best = 16
    for t in range(16, rows + 1, 16):
        if rows % t == 0 and t * cols * itemsize <= budget:
            best = t
    return best


def _adamw(w, g, m, v, *, name):
    R, C = w.shape
    tr = _row_tile(R, C, 4, budget=1024 * 1024)

    def body(w_ref, g_ref, m_ref, v_ref, go_ref, d_ref, nm_ref, nv_ref):
        gv = g_ref[...]
        go_ref[...] = gv
        mn = ADAM_B1 * m_ref[...] + (1.0 - ADAM_B1) * gv
        vn = ADAM_B2 * v_ref[...] + (1.0 - ADAM_B2) * jnp.square(gv)
        m_hat = mn / (1.0 - ADAM_B1 ** ADAM_STEP)
        v_hat = vn / (1.0 - ADAM_B2 ** ADAM_STEP)
        d_ref[...] = -ADAM_LR * (m_hat / (jnp.sqrt(v_hat) + ADAM_EPS) + ADAM_WD * w_ref[...])
        nm_ref[...] = mn
        nv_ref[...] = vn

    spec = pl.BlockSpec((tr, C), lambda i: (i, 0))
    shp = jax.ShapeDtypeStruct((R, C), F32)
    return pl.pallas_call(
        body, name=name, grid=(R // tr,), in_specs=[spec] * 4, out_specs=[spec] * 4, out_shape=[shp] * 4,
        compiler_params=_cparams(("parallel",), 32),
    )(w, g, m, v)


def _pair_sum(g4, recv, c_idx, *, name):
    ns, _, R2, C = g4.shape
    tr = _row_tile(R2, C, 2)

    def body(c_ref, g_ref, r_ref, o_ref):
        o_ref[...] = (g_ref[...].astype(F32) + r_ref[...].astype(F32)).astype(BF16)

    return pl.pallas_call(
        body, name=name,
        grid_spec=pltpu.PrefetchScalarGridSpec(
            num_scalar_prefetch=1, grid=(ns, R2 // tr),
            in_specs=[pl.BlockSpec((None, None, tr, C), lambda s, i, c: (s, c[0], i, 0)),
                      pl.BlockSpec((None, tr, C), lambda s, i, c: (s, i, 0))],
            out_specs=pl.BlockSpec((None, tr, C), lambda s, i, c: (s, i, 0)),
        ),
        out_shape=jax.ShapeDtypeStruct((ns, R2, C), BF16),
        compiler_params=_cparams(("parallel", "parallel"), 32),
    )(c_idx, g4, recv)


def _chip_sum(part, recv, sc_idx, *, name):
    _, R2, C = part.shape
    tr = _row_tile(R2, C, 4)
    nblk = R2 // tr

    def body(s_ref, p_ref, r_ref, o_ref):
        acc = p_ref[...].astype(F32)
        for j in range(N_CHIPS - 1):
            acc = acc + r_ref[j].astype(F32)
        o_ref[...] = acc

    return pl.pallas_call(
        body, name=name,
        grid_spec=pltpu.PrefetchScalarGridSpec(
            num_scalar_prefetch=1, grid=(nblk,),
            in_specs=[pl.BlockSpec((None, tr, C), lambda i, s: (s[0], i, 0)),
                      pl.BlockSpec((N_CHIPS - 1, tr, C), lambda i, s: (0, i, 0))],
            out_specs=pl.BlockSpec((tr, C), lambda i, s: (s[1] * nblk + i, 0)),
        ),
        out_shape=jax.ShapeDtypeStruct((2 * R2, C), F32),
        compiler_params=_cparams(("parallel",), 32),
    )(sc_idx, part, recv)


def _cast_into_slot(w, sc_idx, dtype, after, *, name):
    R, C = w.shape
    tr = _row_tile(R, C, 4)

    def body(s_ref, w_ref, after_ref, o_ref):
        o_ref[...] = w_ref[...].astype(dtype)

    return pl.pallas_call(
        body, name=name,
        grid_spec=pltpu.PrefetchScalarGridSpec(
            num_scalar_prefetch=1, grid=(R // tr,),
            in_specs=[pl.BlockSpec((tr, C), lambda i, s: (i, 0)), pl.BlockSpec(memory_space=pl.ANY)],
            out_specs=pl.BlockSpec((None, tr, C), lambda i, s: (s[0], i, 0)),
        ),
        out_shape=jax.ShapeDtypeStruct((N_CHIPS, R, C), dtype),
        compiler_params=_cparams(("parallel",), 32),
    )(sc_idx, w, after)


def _slab_sum(slabs, *, name):
    n, M, C = slabs.shape

    def body(x_ref, o_ref):
        acc = x_ref[0]
        for d in range(1, n):
            acc = acc + x_ref[d]
        o_ref[...] = acc

    return pl.pallas_call(
        body, name=name, out_shape=jax.ShapeDtypeStruct((M, C), F32),
    )(slabs)


def _mesh_position():
    x, y, c = lax.axis_index("x"), lax.axis_index("y"), lax.axis_index("c")
    other_chips = [(1 - x, y), (x, 1 - y), (1 - x, 1 - y)]
    return x, y, c, other_chips


ANY = pl.BlockSpec(memory_space=pl.ANY)


HBM = pl.BlockSpec(memory_space=pltpu.HBM)
SEM = pl.BlockSpec(memory_space=pltpu.SEMAPHORE)
SPLIT_COPY = pltpu.CompilerParams(has_side_effects=pltpu.SideEffectType.DATAFLOW_SIDE_EFFECTING)
TOKEN = jax.ShapeDtypeStruct((8, LANES), F32)


def _in_hbm(a):
    return pltpu.with_memory_space_constraint(a, pltpu.HBM)


def _half_rows(ref, slot, half):
    hr = ref.shape[1] // 2
    return ref.at[slot, pl.ds(half * hr, hr), :]


def _gather_ici_start(groups, *, name):
    flat = [b for g in groups for b in g]
    K, G = len(flat), len(groups)

    def body(*refs):
        ins, sems, token = refs[:K], refs[K:K + 2 * G], refs[-1]
        x, y, c, chips = _mesh_position()
        s = 2 * x + y
        k = 0
        for gi, g in enumerate(groups):
            for n in range(len(g)):
                own = _half_rows(ins[k], s, c)
                for j, chip in enumerate(chips):
                    pltpu.make_async_remote_copy(
                        src_ref=own, dst_ref=own, send_sem=sems[2 * gi].at[n * (N_CHIPS - 1) + j],
                        recv_sem=sems[2 * gi + 1].at[n * (N_CHIPS - 1) + j],
                        device_id=(*chip, c), device_id_type=MESH).start()
                k += 1
        token[...] = jnp.zeros_like(token)

    sem_shapes = []
    for g in groups:
        sem_shapes += [pltpu.SemaphoreType.DMA((len(g) * (N_CHIPS - 1),))] * 2
    out = pl.pallas_call(
        body, name=name,
        in_specs=[HBM] * K,
        out_specs=[SEM] * (2 * G) + [HBM] * K + [pl.BlockSpec(memory_space=pltpu.VMEM)],
        out_shape=sem_shapes + [pltpu.HBM(b.shape, b.dtype) for b in flat] + [TOKEN],
        input_output_aliases={k: 2 * G + k for k in range(K)},
        compiler_params=SPLIT_COPY,
    )(*[_in_hbm(b) for b in flat])
    handles, k = [], 2 * G
    for gi, g in enumerate(groups):
        handles.append((out[2 * gi], out[2 * gi + 1], list(out[k:k + len(g)])))
        k += len(g)
    return handles, out[-1]


def _gather_ici_wait(handle, after, *, name):
    send, recv, bufs = handle
    n = len(bufs)

    def body(*refs):
        ins, send_ref, recv_ref = refs[:n], refs[n], refs[n + 1]
        x, y, c, chips = _mesh_position()
        s = 2 * x + y
        for k in range(n):
            own = _half_rows(ins[k], s, c)
            for j, chip in enumerate(chips):
                cp = pltpu.make_async_remote_copy(
                    src_ref=own, dst_ref=_half_rows(ins[k], 2 * chip[0] + chip[1], c),
                    send_sem=send_ref.at[k * (N_CHIPS - 1) + j], recv_sem=recv_ref.at[k * (N_CHIPS - 1) + j],
                    device_id=(*chip, c), device_id_type=MESH)
                cp.wait_send()
                cp.wait_recv()

    return pl.pallas_call(
        body, name=name,
        in_specs=[HBM] * n + [SEM, SEM, ANY], out_specs=[HBM] * n,
        out_shape=[pltpu.HBM(b.shape, b.dtype) for b in bufs],
        input_output_aliases={k: k for k in range(n)},
        compiler_params=SPLIT_COPY,
    )(*bufs, send, recv, after)


def _forward_halves(bufs, *, name):
    K = len(bufs)
    per = N_CHIPS - 1

    def body(*refs):
        outs = refs[K:2 * K]
        send_sems, recv_sems = refs[2 * K:]
        x, y, c, chips = _mesh_position()
        copies = []
        for k in range(K):
            for j, chip in enumerate(chips):
                got = _half_rows(outs[k], 2 * chip[0] + chip[1], c)
                cp = pltpu.make_async_remote_copy(
                    src_ref=got, dst_ref=got, send_sem=send_sems.at[k * per + j], recv_sem=recv_sems.at[k * per + j],
                    device_id=(x, y, 1 - c), device_id_type=MESH)
                cp.start()
                copies.append(cp)
        for cp in copies:
            cp.wait()

    return pl.pallas_call(
        body, name=name,
        in_specs=[ANY] * K, out_specs=[ANY] * K,
        out_shape=[jax.ShapeDtypeStruct(a.shape, a.dtype) for a in bufs],
        input_output_aliases={k: k for k in range(K)},
        scratch_shapes=[pltpu.SemaphoreType.DMA((K * per,)), pltpu.SemaphoreType.DMA((K * per,))],
    )(*bufs)


def _chip_exchange_copies(srcs, lands, send_sems, recv_sems):
    x, y, c, chips = _mesh_position()
    per = N_CHIPS - 1
    return [pltpu.make_async_remote_copy(
        src_ref=srcs[k].at[2 * chip[0] + chip[1]], dst_ref=lands[k].at[j],
        send_sem=send_sems.at[k * per + j], recv_sem=recv_sems.at[k * per + j],
        device_id=(*chip, c), device_id_type=MESH) for k in range(len(srcs)) for j, chip in enumerate(chips)]


def _sibling_swap_copies(srcs, lands, send_sems, recv_sems):
    x, y, c, _ = _mesh_position()
    return [pltpu.make_async_remote_copy(
        src_ref=srcs[k].at[pl.ds(0, srcs[k].shape[0]), 1 - c], dst_ref=lands[k],
        send_sem=send_sems.at[k], recv_sem=recv_sems.at[k],
        device_id=(x, y, 1 - c), device_id_type=MESH) for k in range(len(srcs))]


def _split_copy_start(srcs, land_shapes, n_sems, copies, *, name):
    K = len(srcs)

    def body(*refs):
        for cp in copies(refs[:K], refs[K:2 * K], refs[2 * K], refs[2 * K + 1]):
            cp.start()
        refs[-1][...] = jnp.zeros_like(refs[-1])

    out = pl.pallas_call(
        body, name=name,
        in_specs=[HBM] * (2 * K),
        out_specs=[SEM, SEM] + [HBM] * (2 * K) + [pl.BlockSpec(memory_space=pltpu.VMEM)],
        out_shape=[pltpu.SemaphoreType.DMA((n_sems,))] * 2
        + [pltpu.HBM(a.shape, a.dtype) for a in srcs]
        + [pltpu.HBM(s, a.dtype) for s, a in zip(land_shapes, srcs)] + [TOKEN],
        input_output_aliases={k: 2 + k for k in range(2 * K)},
        compiler_params=SPLIT_COPY,
    )(*[_in_hbm(a) for a in srcs], *[_in_hbm(lax.empty(s, a.dtype)) for s, a in zip(land_shapes, srcs)])
    return (out[0], out[1], list(out[2:2 + K]), list(out[2 + K:2 + 2 * K])), out[-1]


def _split_copy_wait(handle, copies, after, *, name):
    send, recv, srcs, lands = handle
    K = len(srcs)

    def body(*refs):
        for cp in copies(refs[:K], refs[K:2 * K], refs[2 * K], refs[2 * K + 1]):
            cp.wait_send()
            cp.wait_recv()

    out = pl.pallas_call(
        body, name=name,
        in_specs=[HBM] * (2 * K) + [SEM, SEM, ANY], out_specs=[HBM] * (2 * K),
        out_shape=[pltpu.HBM(a.shape, a.dtype) for a in srcs] + [pltpu.HBM(a.shape, a.dtype) for a in lands],
        input_output_aliases={k: k for k in range(2 * K)},
        compiler_params=SPLIT_COPY,
    )(*srcs, *lands, send, recv, after)
    return list(out[:K]), list(out[K:])


def _join_halves(bufs, *, name):
    K = len(bufs)

    def body(*refs):
        outs = refs[K:2 * K]
        send_sems, recv_sems = refs[2 * K:]
        x, y, c, _ = _mesh_position()
        copies = []
        for k in range(K):
            r2 = bufs[k].shape[0] // 2
            mine = outs[k].at[pl.ds(c * r2, r2), :]
            cp = pltpu.make_async_remote_copy(
                src_ref=mine, dst_ref=mine, send_sem=send_sems.at[k], recv_sem=recv_sems.at[k],
                device_id=(x, y, 1 - c), device_id_type=MESH)
            cp.start()
            copies.append(cp)
        for cp in copies:
            cp.wait()

    return pl.pallas_call(
        body, name=name,
        in_specs=[ANY] * K, out_specs=[ANY] * K,
        out_shape=[jax.ShapeDtypeStruct(a.shape, a.dtype) for a in bufs],
        input_output_aliases={k: k for k in range(K)},
        scratch_shapes=[pltpu.SemaphoreType.DMA((K,)), pltpu.SemaphoreType.DMA((K,))],
    )(*bufs)


def _all_gather_slab(slab):
    m_per, n = slab.shape

    def body(x_ref, out_ref, send_sems, recv_sems, local_sem):
        x, y, c, chips = _mesh_position()
        me, sibling = (x, y, c), (x, y, 1 - c)

        def rows(px, py, pc):
            return out_ref.at[pl.ds((4 * px + 2 * py + pc) * m_per, m_per), :]

        def copy(k, block, to, src=None):
            return pltpu.make_async_remote_copy(
                src_ref=rows(*block) if src is None else src, dst_ref=rows(*block),
                send_sem=send_sems.at[k], recv_sem=recv_sems.at[k], device_id=to, device_id_type=MESH)

        mine = pltpu.make_async_copy(x_ref, rows(*me), local_sem)
        mine.start()
        first = [copy(0, me, sibling, src=x_ref)]
        first += [copy(1 + j, me, (*chip, c), src=x_ref) for j, chip in enumerate(chips)]
        for cp in first:
            cp.start()
        passed = [copy(4 + j, (*chip, c), sibling) for j, chip in enumerate(chips)]
        for j, chip in enumerate(chips):
            copy(1 + j, (*chip, c), me).wait_recv()
            passed[j].start()
        copy(0, sibling, me).wait_recv()
        for j, chip in enumerate(chips):
            copy(4 + j, (*chip, 1 - c), me).wait_recv()
        for cp in first + passed:
            cp.wait_send()
        mine.wait()

    return pl.pallas_call(
        body, name="gather_small_grads",
        out_shape=jax.ShapeDtypeStruct((N_DEV * m_per, n), slab.dtype),
        in_specs=[pl.BlockSpec(memory_space=pltpu.VMEM)],
        out_specs=pl.BlockSpec(memory_space=pltpu.VMEM),
        scratch_shapes=[pltpu.SemaphoreType.DMA((7,)), pltpu.SemaphoreType.DMA((7,)), pltpu.SemaphoreType.DMA],
    )(slab)


def _ffn_backward_weights(dhb, saved, w_in, w_out, after, tag, on_dw_out=None):
    n, gu, a = saved
    S = n.shape[0]
    ns, D, cs = w_in.shape
    F = w_out.shape[0]
    to = 512
    dw_out = _mm_tn(
        a, dhb, grid=(F // to,),
        a_spec=pl.BlockSpec((S, to), lambda j: (0, j)), b_spec=pl.BlockSpec((S, D), lambda j: (0, 0)),
        out_spec=pl.BlockSpec((to, D), lambda j: (j, 0)), out_shape=jax.ShapeDtypeStruct((F, D), BF16),
        scale=0.5, name=f"{tag}_dw_out").reshape(N_CHIPS, F // N_CHIPS, D)
    if on_dw_out is not None:
        after = on_dw_out(dw_out)
    dgu = _ffn_bwd_act(dhb, w_out, gu, after, name=f"{tag}_bwd_act")
    ti = MXU_TILE
    per_g, per_s = F // ti, cs // ti
    dw_in = _mm_tn(
        n.T, dgu, grid=(2 * F // ti,), a_is_transposed=True,
        a_spec=pl.BlockSpec((D, S), lambda j: (0, 0)),
        b_spec=pl.BlockSpec((None, S, ti), lambda j: (lax.div(j, per_g), 0, lax.rem(j, per_g))),
        out_spec=pl.BlockSpec((None, D, ti), lambda j: (lax.div(j, per_s), 0, lax.rem(j, per_s))),
        out_shape=jax.ShapeDtypeStruct((ns, D, cs), BF16), scale=1.0, name=f"{tag}_dw_in")
    return dgu, dw_in, dw_out


def kernel(x, ffn1_norm, ffn1_w_in, ffn1_w_out, mix_norm, w_in_mix, w_pool, pool_scale, w_alpha, b_alpha, gla_norm, w_out_mix, ffn2_norm, ffn2_w_in, ffn2_w_out, final_norm, loss_target, m_ffn1_norm, m_ffn1_w_in, m_ffn1_w_out, m_mix_norm, m_w_in_mix, m_w_pool, m_pool_scale, m_w_alpha, m_b_alpha, m_gla_norm, m_w_out_mix, m_ffn2_norm, m_ffn2_w_in, m_ffn2_w_out, m_final_norm, v_ffn1_norm, v_ffn1_w_in, v_ffn1_w_out, v_mix_norm, v_w_in_mix, v_w_pool, v_pool_scale, v_w_alpha, v_b_alpha, v_gla_norm, v_w_out_mix, v_ffn2_norm, v_ffn2_w_in, v_ffn2_w_out, v_final_norm):
    names = ["ffn1_norm", "ffn1_w_in", "ffn1_w_out", "mix_norm", "w_in_mix", "w_pool", "pool_scale", "w_alpha",
             "b_alpha", "gla_norm", "w_out_mix", "ffn2_norm", "ffn2_w_in", "ffn2_w_out", "final_norm"]
    weights = dict(zip(names, [ffn1_norm, ffn1_w_in, ffn1_w_out, mix_norm, w_in_mix, w_pool, pool_scale, w_alpha,
                               b_alpha, gla_norm, w_out_mix, ffn2_norm, ffn2_w_in, ffn2_w_out, final_norm]))
    moms = dict(zip(names, [m_ffn1_norm, m_ffn1_w_in, m_ffn1_w_out, m_mix_norm, m_w_in_mix, m_w_pool, m_pool_scale,
                            m_w_alpha, m_b_alpha, m_gla_norm, m_w_out_mix, m_ffn2_norm, m_ffn2_w_in, m_ffn2_w_out,
                            m_final_norm]))
    vels = dict(zip(names, [v_ffn1_norm, v_ffn1_w_in, v_ffn1_w_out, v_mix_norm, v_w_in_mix, v_w_pool, v_pool_scale,
                            v_w_alpha, v_b_alpha, v_gla_norm, v_w_out_mix, v_ffn2_norm, v_ffn2_w_in, v_ffn2_w_out,
                            v_final_norm]))
    xi, yi, ci = lax.axis_index("x"), lax.axis_index("y"), lax.axis_index("c")
    chip = 2 * xi + yi
    c_idx = jnp.reshape(ci, (1,)).astype(jnp.int32)
    sc_idx = jnp.stack([chip, ci]).astype(jnp.int32)

    def flat2d(a):
        return a.reshape(-1, a.shape[-1])

    ex = _Exchanges(sc_idx, c_idx)

    def cast(n, after):
        return _cast_into_slot(flat2d(weights[n]), sc_idx, F32 if n == "w_alpha" else BF16, after, name=f"cast_{n}")

    first, rest = _Exchanges.GATHER_GROUPS[:2], _Exchanges.GATHER_GROUPS[2:]
    tok = ex.start_gather({n: cast(n, sc_idx) for g in first for n in g}, first, name="gather_ici_start_ffn1")
    tok2 = ex.start_gather({n: cast(n, tok) for g in rest for n in g}, rest, name="gather_ici_start_rest")
    small_params = dict(g1=ffn1_norm, gm=mix_norm, g2=ffn2_norm, gf=final_norm.reshape(1, D_MODEL),
                        pool_scale=pool_scale, b_alpha=b_alpha, gla_norm=gla_norm)
    loss_blk, dx, small = _forward_backward(x[0], loss_target[0], ex, tok[0, 0] + tok2[0, 0], small_params)

    outs = {}

    def update(n, g):
        w = weights[n]
        w2 = flat2d(w) if w.ndim > 1 else w.reshape(1, -1)
        go, d, nm, nv = _adamw(w2, g.reshape(w2.shape), moms[n].reshape(w2.shape), vels[n].reshape(w2.shape),
                               name=f"adamw_{n}")
        outs[n] = (go.reshape(w.shape), d.reshape(w.shape), nm.reshape(w.shape), nv.reshape(w.shape))
        return nv

    last = dx
    for tag in _Exchanges.REDUCE_ORDER:
        for n, g in ex.reduced(tag, after=last).items():
            last = update(n, g)

    grads = {}
    small_names = ["ffn1_norm", "mix_norm", "ffn2_norm", "final_norm", "pool_scale", "b_alpha", "gla_norm", "w_alpha",
                   "loss"]
    small = small + [loss_blk[0:1]]
    rows = [a.size // LANES for a in small]
    slab = jnp.concatenate([a.reshape(-1, LANES) for a in small], axis=0)
    pad = -slab.shape[0] % 8
    slab = jnp.pad(slab, ((0, pad), (0, 0)))
    gathered = _all_gather_slab(slab).reshape(N_DEV, slab.shape[0], LANES)
    total = _slab_sum(gathered, name="sum_small_grads")
    off = 0
    for n, a, r in zip(small_names, small, rows):
        grads[n] = total[off:off + r].reshape(a.shape)
        off += r
    grads["w_alpha"] = lax.dynamic_slice_in_dim(grads["w_alpha"], chip * (GLA_DK_TOTAL // N_CHIPS),
                                                GLA_DK_TOTAL // N_CHIPS, axis=1)

    loss = grads.pop("loss")[0, 0]
    for n in small_names[:-1]:
        update(n, grads[n])
    return (loss, dx[None], *[outs[n][0] for n in names], *[outs[n][1] for n in names],
            *[outs[n][2] for n in names], *[outs[n][3] for n in names])


class _Exchanges:
    GATHER_GROUPS = (("ffn1_w_in",), ("ffn1_w_out",), ("w_in_mix", "w_pool", "w_alpha"), ("w_out_mix",),
                     ("ffn2_w_in",), ("ffn2_w_out",))
    REDUCE_ORDER = ("ffn2", "mix", "ffn1_out", "ffn1_in")

    def __init__(self, sc_idx, c_idx):
        self.sc_idx, self.c_idx = sc_idx, c_idx
        self._gathers, self._swaps, self._reduces = {}, {}, {}

    def start_gather(self, bufs, groups, *, name):
        handles, token = _gather_ici_start([[bufs[n] for n in g] for g in groups], name=name)
        for g, h in zip(groups, handles):
            self._gathers[g[0]] = (g, h)
        return token

    def gathered(self, first, after):
        names, handle = self._gathers.pop(first)
        got = _gather_ici_wait(handle, after, name=f"gather_ici_wait_{first}")
        return dict(zip(names, _forward_halves(got, name=f"gather_forward_{first}")))

    def begin_reduce(self, tag, full):
        g4 = [a.reshape(N_CHIPS, 2, a.shape[1] // 2, a.shape[2]) for a in full.values()]
        lands = [(a.shape[0],) + a.shape[2:] for a in g4]
        handle, token = _split_copy_start(g4, lands, len(g4), _sibling_swap_copies, name=f"swap_start_{tag}")
        self._swaps[tag] = (list(full), handle)
        return token

    def start_reduce(self, tag, after):
        names, handle = self._swaps.pop(tag)
        g4, from_sibling = _split_copy_wait(handle, _sibling_swap_copies, after, name=f"swap_wait_{tag}")
        pair = [_pair_sum(a, b, self.c_idx, name=f"pair_sum_{n}") for n, a, b in zip(names, g4, from_sibling)]
        lands = [(N_CHIPS - 1,) + a.shape[1:] for a in pair]
        handle, token = _split_copy_start(pair, lands, len(pair) * (N_CHIPS - 1), _chip_exchange_copies,
                                          name=f"exchange_start_{tag}")
        self._reduces[tag] = (names, handle)
        return token

    def reduced(self, tag, after):
        names, handle = self._reduces.pop(tag)
        pair, lands = _split_copy_wait(handle, _chip_exchange_copies, after, name=f"exchange_wait_{tag}")
        halves = [_chip_sum(a, b, self.sc_idx, name=f"chip_sum_{n}") for n, a, b in zip(names, pair, lands)]
        return dict(zip(names, _join_halves(halves, name=f"join_halves_{tag}")))


def _forward_backward(h0, target, ex, started, sp):
    g1, gm, g2, gf = sp["g1"], sp["gm"], sp["g2"], sp["gf"]
    pool_scale, b_alpha, gla_norm = sp["pool_scale"], sp["b_alpha"], sp["gla_norm"]
    cs_mix = D_IN // N_CHIPS

    n1 = _rms_fwd(h0, g1 + started, name="ffn1_norm")
    w1_in = ex.gathered("ffn1_w_in", after=n1)["ffn1_w_in"]
    gu1, a1 = _ffn_up(n1, w1_in, name="ffn1_up")
    w1_out = ex.gathered("ffn1_w_out", after=a1)["ffn1_w_out"].reshape(D_FF, D_MODEL)
    h1 = _mm_nn(a1, w1_out, h0, 0.5, tm=512, tn=D_MODEL, tk=COL_TILE, name="ffn1_down")
    saved1 = (n1, gu1, a1)
    n_mix = _rms_fwd(h1, gm, name="mix_norm")
    gw = ex.gathered("w_in_mix", after=n_mix)
    w_mix = jnp.concatenate([gw["w_in_mix"][t] for t in range(N_CHIPS)], axis=1)
    w_mix = jnp.pad(w_mix, ((0, 0), (0, D_IN_PAD - D_IN)))[None]
    wp = gw["w_pool"].reshape(N_CHIPS, 4, POOL_GROUP_DIM // N_CHIPS, POOL_GROUP_DIM)
    wp = wp.transpose(1, 0, 2, 3).reshape(4, POOL_GROUP_DIM, POOL_GROUP_DIM)
    wa = gw["w_alpha"].transpose(1, 0, 2).reshape(GLA_GATE_RANK, GLA_DK_TOTAL)
    wa = jnp.pad(wa, ((0, LANES - GLA_GATE_RANK), (0, 0))).astype(BF16)
    u = _mm_nn(n_mix, w_mix[0], None, 1.0, tm=512, tn=1408, tk=D_MODEL, name="mix_in")
    y_pool = _pool_fwd(u, wp, pool_scale, name="pool_fwd")
    cat, o_gla, states = _gla_fwd(u, y_pool, wa, b_alpha, gla_norm, name="gla_fwd")
    w_omix = ex.gathered("w_out_mix", after=cat)["w_out_mix"].reshape(D_MODEL, D_MODEL)
    h2 = _mm_nn(cat, w_omix, h1, 1.0, tm=512, tn=D_MODEL, tk=1024, name="mix_out")
    n3 = _rms_fwd(h2, g2, name="ffn2_norm")
    w2_in = ex.gathered("ffn2_w_in", after=n3)["ffn2_w_in"]
    gu2, a2 = _ffn_up(n3, w2_in, name="ffn2_up")
    w2_out = ex.gathered("ffn2_w_out", after=a2)["ffn2_w_out"].reshape(D_FF, D_MODEL)
    h3 = _mm_nn(a2, w2_out, h2, 0.5, tm=512, tn=D_MODEL, tk=COL_TILE, name="ffn2_down")
    saved2 = (n3, gu2, a2)
    loss_blk, dh3, dh3b, d_gf = _final_loss(h3, gf, target, name="final_loss")

    dgu2, dw2_in, dw2_out = _ffn_backward_weights(dh3b, saved2, w2_in, w2_out, dh3b, "ffn2")
    tok = ex.begin_reduce("ffn2", {"ffn2_w_in": dw2_in, "ffn2_w_out": dw2_out})
    dh2, dh2b, d_g2 = _mm_nt_rmsbwd(dgu2, w2_in, h2, dh3, g2 + tok[0, 0], tk=COL_TILE, name="ffn2_dx")
    tok = ex.start_reduce("ffn2", after=dh2b)
    S = h0.shape[0]
    dcat = _mm_nt(dh2b, w_omix, tok, tm=512, tn=1024, name="mix_out_dx")
    dw_omix = _mm_tn(
        cat, dh2b, grid=(4,),
        a_spec=pl.BlockSpec((S, 512), lambda j: (0, j)), b_spec=pl.BlockSpec((S, D_MODEL), lambda j: (0, 0)),
        out_spec=pl.BlockSpec((512, D_MODEL), lambda j: (j, 0)),
        out_shape=jax.ShapeDtypeStruct((D_MODEL, D_MODEL), BF16), scale=1.0, name="mix_out_dw")
    dp, dw_pool, d_pscale = _pool_bwd(u, dcat, wp, pool_scale, name="pool_bwd")
    du, d_wa, d_ba, d_gn = _gla_bwd(u, o_gla, states, dcat, dp, wa, b_alpha, gla_norm, name="gla_bwd")
    du = du[None]
    tn_mix = COL_TILE
    dw_mix = _mm_tn(
        n_mix.T, du, grid=(2, D_IN_PAD // tn_mix), a_is_transposed=True,
        a_spec=pl.BlockSpec((D_MODEL // 2, S), lambda i, j: (i, 0)),
        b_spec=pl.BlockSpec((None, S, tn_mix), lambda i, j: (0, 0, j)),
        out_spec=pl.BlockSpec((D_MODEL // 2, tn_mix), lambda i, j: (i, j)),
        out_shape=jax.ShapeDtypeStruct((D_MODEL, D_IN_PAD), BF16), scale=1.0, name="mix_in_dw")
    dw_mix_s = dw_mix[:, :D_IN].reshape(D_MODEL, N_CHIPS, cs_mix).transpose(1, 0, 2)
    dw_pool_s = dw_pool.reshape(4, N_CHIPS, POOL_GROUP_DIM // N_CHIPS, POOL_GROUP_DIM).transpose(1, 0, 2, 3)
    dw_pool_s = dw_pool_s.reshape(N_CHIPS, POOL_GROUP_DIM, POOL_GROUP_DIM).astype(BF16)
    tok = ex.begin_reduce("mix", {"w_in_mix": dw_mix_s,
                                  "w_out_mix": dw_omix.reshape(N_CHIPS, D_MODEL // N_CHIPS, D_MODEL),
                                  "w_pool": dw_pool_s})
    dh1, dh1b, d_gm = _mm_nt_rmsbwd(du, w_mix, h1, dh2, gm + tok[0, 0], tk=tn_mix, name="mix_in_dx")
    tok = ex.start_reduce("mix", after=dh1b)
    def reduce_now(tag, full):
        return ex.start_reduce(tag, after=ex.begin_reduce(tag, full))

    mix_started = tok
    dgu1, dw1_in, _ = _ffn_backward_weights(
        dh1b, saved1, w1_in, w1_out, None, "ffn1",
        on_dw_out=lambda dw: mix_started + reduce_now("ffn1_out", {"ffn1_w_out": dw}))
    tok = reduce_now("ffn1_in", {"ffn1_w_in": dw1_in})
    dx, _, d_g1 = _mm_nt_rmsbwd(dgu1, w1_in, h0, dh1, g1 + tok[0, 0], tk=COL_TILE, name="ffn1_dx")
    small = [d_g1, d_gm, d_g2, d_gf, d_pscale, d_ba, d_gn, d_wa[:GLA_GATE_RANK]]
    return loss_blk, dx, small
```

```python
import functools

import jax
import jax.numpy as jnp
from jax import lax
from jax.experimental import pallas as pl
from jax.experimental.pallas import tpu as pltpu

F32 = jnp.float32
BF16 = jnp.bfloat16
MESH = pl.DeviceIdType.MESH

D_MODEL = 2048
D_FF = 5632
D_POOL = 1024
POOL_WINDOWS = (2, 4, 8, 16)
POOL_GROUP_DIM = 256
D_GLA = 1024
GLA_HEADS = 4
GLA_DV = 256
GLA_DK = 128
GLA_DK_TOTAL = 512
GLA_GATE_RANK = 16
GATE_LOGIT_NORMALIZER = 16.0
CHUNK = 64
D_IN = 4112
D_IN_PAD = 4224
EPS = 1e-6

ADAM_LR = 0.001
ADAM_B1 = 0.9
ADAM_B2 = 0.999
ADAM_EPS = 1e-08
ADAM_WD = 0.01
ADAM_STEP = 10

N_CHIPS = 4
N_DEV = 8
V7X_VMEM_BYTES = 64 * 1024 * 1024
LANES = 128
MXU_TILE = 256
COL_TILE = 1408
SHARD_TILE = 2816


def _cparams(semantics, vmem_mb):
    assert vmem_mb * 1024 * 1024 < V7X_VMEM_BYTES
    return pltpu.CompilerParams(dimension_semantics=semantics, vmem_limit_bytes=vmem_mb * 1024 * 1024)


def _dot_nn(a, b):
    return jnp.dot(a, b, preferred_element_type=F32)


def _dot_nt(a, b):
    return lax.dot_general(a, b, (((1,), (1,)), ((), ())), preferred_element_type=F32)


def _dot_tn(a, b):
    return lax.dot_general(a, b, (((0,), (0,)), ((), ())), preferred_element_type=F32)


def _sigmoid(x):
    return 1.0 / (1.0 + jnp.exp(-x))


def _rms_fwd(x, g, *, name):
    S, D = x.shape
    tm = 256

    def body(x_ref, g_ref, o_ref):
        xv = x_ref[...]
        r = lax.rsqrt(jnp.mean(xv * xv, axis=-1, keepdims=True) + EPS)
        o_ref[...] = (xv * r * g_ref[...]).astype(BF16)

    return pl.pallas_call(
        body, name=name, grid=(S // tm,),
        in_specs=[pl.BlockSpec((tm, D), lambda i: (i, 0)), pl.BlockSpec((1, D), lambda i: (0, 0))],
        out_specs=pl.BlockSpec((tm, D), lambda i: (i, 0)),
        out_shape=jax.ShapeDtypeStruct((S, D), BF16),
        compiler_params=_cparams(("parallel",), 32),
    )(x, g)


def _ffn_up(n, w_in, *, name):
    S, D = n.shape
    ns, _, cs = w_in.shape
    half = ns // 2
    F = cs * half
    tm, tn = 128, SHARD_TILE
    nb = cs // tn

    def body(n_ref, wg_ref, wu_ref, gu_ref, a_ref):
        nv = n_ref[...]
        g = _dot_nn(nv, wg_ref[...])
        u = _dot_nn(nv, wu_ref[...])
        gu_ref[0] = g.astype(BF16)
        gu_ref[1] = u.astype(BF16)
        a_ref[...] = (g * _sigmoid(g) * u).astype(BF16)

    return pl.pallas_call(
        body, name=name, grid=(F // tn, S // tm),
        in_specs=[
            pl.BlockSpec((tm, D), lambda j, i: (i, 0)),
            pl.BlockSpec((None, D, tn), lambda j, i: (lax.div(j, nb), 0, lax.rem(j, nb))),
            pl.BlockSpec((None, D, tn), lambda j, i: (half + lax.div(j, nb), 0, lax.rem(j, nb))),
        ],
        out_specs=[
            pl.BlockSpec((2, tm, tn), lambda j, i: (0, i, j)),
            pl.BlockSpec((tm, tn), lambda j, i: (i, j)),
        ],
        out_shape=[jax.ShapeDtypeStruct((2, S, F), BF16), jax.ShapeDtypeStruct((S, F), BF16)],
        compiler_params=_cparams(("parallel", "parallel"), 56),
    )(n, w_in, w_in)


def _mm_nn(a, b, resid, scale, *, tm, tn, tk, name):
    S, K = a.shape
    N = b.shape[1]
    nk = K // tk

    def body(*refs):
        if resid is None:
            a_ref, b_ref, o_ref, acc_ref = refs
            r_ref = None
        else:
            a_ref, b_ref, r_ref, o_ref, acc_ref = refs
        k = pl.program_id(2)

        @pl.when(k == 0)
        def _():
            acc_ref[...] = jnp.zeros_like(acc_ref)

        acc_ref[...] += _dot_nn(a_ref[...], b_ref[...])

        @pl.when(k == nk - 1)
        def _():
            out = acc_ref[...] * scale
            if r_ref is not None:
                out = r_ref[...] + out
            o_ref[...] = out

    in_specs = [pl.BlockSpec((tm, tk), lambda i, j, k: (i, k)), pl.BlockSpec((tk, tn), lambda i, j, k: (k, j))]
    args = [a, b]
    if resid is not None:
        in_specs.append(pl.BlockSpec((tm, tn), lambda i, j, k: (i, j)))
        args.append(resid)
    return pl.pallas_call(
        body, name=name, grid=(S // tm, N // tn, nk),
        in_specs=in_specs,
        out_specs=pl.BlockSpec((tm, tn), lambda i, j, k: (i, j)),
        out_shape=jax.ShapeDtypeStruct((S, N), F32),
        scratch_shapes=[pltpu.VMEM((tm, tn), F32)],
        compiler_params=_cparams(("parallel", "parallel", "arbitrary"), 48),
    )(*args)


def _mm_nt(a, b, after, *, tm, tn, name):
    S, K = a.shape
    N = b.shape[0]

    def body(a_ref, b_ref, after_ref, o_ref):
        o_ref[...] = _dot_nt(a_ref[...], b_ref[...])

    return pl.pallas_call(
        body, name=name, grid=(N // tn, S // tm),
        in_specs=[pl.BlockSpec((tm, K), lambda j, i: (i, 0)), pl.BlockSpec((tn, K), lambda j, i: (j, 0)),
                  pl.BlockSpec(memory_space=pl.ANY)],
        out_specs=pl.BlockSpec((tm, tn), lambda j, i: (i, j)),
        out_shape=jax.ShapeDtypeStruct((S, N), F32),
        compiler_params=_cparams(("parallel", "parallel"), 48),
    )(a, b, after)


def _mm_tn(a, b, *, grid, a_spec, b_spec, out_spec, out_shape, scale, name, a_is_transposed=False):
    dot = _dot_nn if a_is_transposed else _dot_tn

    def body(a_ref, b_ref, o_ref):
        o_ref[...] = (scale * dot(a_ref[...], b_ref[...])).astype(o_ref.dtype)

    return pl.pallas_call(
        body, name=name, grid=grid, in_specs=[a_spec, b_spec], out_specs=out_spec, out_shape=out_shape,
        compiler_params=_cparams(("parallel",) * len(grid), 56),
    )(a, b)


def _ffn_bwd_act(dhb, w_out, gu, after, *, name):
    S, D = dhb.shape
    F = w_out.shape[0]
    tm, tn = 256, SHARD_TILE

    def body(dh_ref, w_ref, gu_ref, after_ref, dgu_ref):
        da = 0.5 * _dot_nt(dh_ref[...], w_ref[...])
        g = gu_ref[0].astype(F32)
        u = gu_ref[1].astype(F32)
        s = _sigmoid(g)
        dgu_ref[0] = (da * u * (s * (1.0 + g * (1.0 - s)))).astype(BF16)
        dgu_ref[1] = (da * (g * s)).astype(BF16)

    return pl.pallas_call(
        body, name=name, grid=(F // tn, S // tm),
        in_specs=[
            pl.BlockSpec((tm, D), lambda j, i: (i, 0)),
            pl.BlockSpec((tn, D), lambda j, i: (j, 0)),
            pl.BlockSpec((2, tm, tn), lambda j, i: (0, i, j)),
            pl.BlockSpec(memory_space=pl.ANY),
        ],
        out_specs=pl.BlockSpec((2, tm, tn), lambda j, i: (0, i, j)),
        out_shape=jax.ShapeDtypeStruct((2, S, F), BF16),
        compiler_params=_cparams(("parallel", "parallel"), 56),
    )(dhb, w_out, gu, after)


def _mm_nt_rmsbwd(dact, w, h_in, dh_out, g, *, tk, name):
    ng, S, fg = dact.shape
    ns, D, cs = w.shape
    assert ng * fg == ns * cs
    tm, rc = 512, 64
    kpg, kps = fg // tk, cs // tk
    nk = ng * kpg

    def body(a_ref, w_ref, h_ref, dho_ref, g_ref, dh_ref, dhb_ref, dg_ref, acc_ref):
        i = pl.program_id(0)
        k = pl.program_id(1)

        @pl.when(k == 0)
        def _():
            acc_ref[...] = jnp.zeros_like(acc_ref)

        acc_ref[...] += _dot_nt(a_ref[...], w_ref[...])

        @pl.when(jnp.logical_and(i == 0, k == 0))
        def _():
            dg_ref[...] = jnp.zeros_like(dg_ref)

        @pl.when(k == nk - 1)
        def _():
            gv = g_ref[...]

            def rows_step(c, dg):
                rows = pl.ds(pl.multiple_of(c * rc, rc), rc)
                dn = acc_ref[rows, :]
                xv = h_ref[rows, :]
                r = lax.rsqrt(jnp.mean(xv * xv, axis=-1, keepdims=True) + EPS)
                xh = xv * r
                dng = dn * gv
                dx = r * (dng - xh * jnp.mean(dng * xh, axis=-1, keepdims=True))
                out = dho_ref[rows, :] + dx
                dh_ref[rows, :] = out
                dhb_ref[rows, :] = out.astype(BF16)
                return dg + jnp.sum(dn * xh, axis=0, keepdims=True)

            dg_ref[...] += lax.fori_loop(0, tm // rc, rows_step, jnp.zeros((1, D), F32))

    return pl.pallas_call(
        body, name=name, grid=(S // tm, nk),
        in_specs=[
            pl.BlockSpec((None, tm, tk), lambda i, k: (lax.div(k, kpg), i, lax.rem(k, kpg))),
            pl.BlockSpec((None, D, tk), lambda i, k: (lax.div(k, kps), 0, lax.rem(k, kps))),
            pl.BlockSpec((tm, D), lambda i, k: (i, 0)),
            pl.BlockSpec((tm, D), lambda i, k: (i, 0)),
            pl.BlockSpec((1, D), lambda i, k: (0, 0)),
        ],
        out_specs=[
            pl.BlockSpec((tm, D), lambda i, k: (i, 0)),
            pl.BlockSpec((tm, D), lambda i, k: (i, 0)),
            pl.BlockSpec((1, D), lambda i, k: (0, 0)),
        ],
        out_shape=[jax.ShapeDtypeStruct((S, D), F32), jax.ShapeDtypeStruct((S, D), BF16),
                   jax.ShapeDtypeStruct((1, D), F32)],
        scratch_shapes=[pltpu.VMEM((tm, D), F32)],
        compiler_params=_cparams(("arbitrary", "arbitrary"), 56),
    )(dact, w, h_in, dh_out, g)


def _final_loss(h, g, target, *, name):
    S, D = h.shape
    tm = 256

    def body(h_ref, g_ref, t_ref, loss_ref, dh_ref, dhb_ref, dg_ref):
        i = pl.program_id(0)

        @pl.when(i == 0)
        def _():
            loss_ref[...] = jnp.zeros_like(loss_ref)
            dg_ref[...] = jnp.zeros_like(dg_ref)

        xv = h_ref[...]
        gv = g_ref[...]
        r = lax.rsqrt(jnp.mean(xv * xv, axis=-1, keepdims=True) + EPS)
        xh = xv * r
        e = xh * gv - t_ref[...]
        loss_ref[...] += 0.5 * jnp.sum(jnp.mean(e * e, axis=-1, keepdims=True))
        dy = e * (1.0 / D)
        dg_ref[...] += jnp.sum(dy * xh, axis=0, keepdims=True)
        dyg = dy * gv
        dx = r * (dyg - xh * jnp.mean(dyg * xh, axis=-1, keepdims=True))
        dh_ref[...] = dx
        dhb_ref[...] = dx.astype(BF16)

    return pl.pallas_call(
        body, name=name, grid=(S // tm,),
        in_specs=[pl.BlockSpec((tm, D), lambda i: (i, 0)), pl.BlockSpec((1, D), lambda i: (0, 0)),
                  pl.BlockSpec((tm, D), lambda i: (i, 0))],
        out_specs=[pl.BlockSpec((8, LANES), lambda i: (0, 0)), pl.BlockSpec((tm, D), lambda i: (i, 0)),
                   pl.BlockSpec((tm, D), lambda i: (i, 0)), pl.BlockSpec((1, D), lambda i: (0, 0))],
        out_shape=[jax.ShapeDtypeStruct((8, LANES), F32), jax.ShapeDtypeStruct((S, D), F32),
                   jax.ShapeDtypeStruct((S, D), BF16), jax.ShapeDtypeStruct((1, D), F32)],
        compiler_params=_cparams(("arbitrary",), 40),
    )(h, g, target)


POOL_HALO = 16
POOL_ROWS = 256


def _pool_window_mean_minus_token(ext, tok0, w):
    s = ext
    k = 1
    while k < w:
        s = s + pltpu.roll(s, k, 0)
        k *= 2
    win = s[POOL_HALO:, :]
    tok = tok0 + lax.broadcasted_iota(jnp.int32, (POOL_ROWS, 1), 0)
    cnt = jnp.minimum(tok + 1, w).astype(F32)
    return win / cnt - ext[POOL_HALO:, :], cnt


def _pool_fwd(u, w_pool, scale, *, name):
    S = u.shape[0]
    C = POOL_GROUP_DIM
    nsteps = S // POOL_ROWS

    def body(p_ref, w_ref, sc_ref, y_ref, xp_ref):
        xp_ref[0:POOL_HALO, :] = jnp.zeros((POOL_HALO, D_POOL), F32)
        xp_ref[POOL_HALO:, :] = p_ref[...]
        for gi, win in enumerate(POOL_WINDOWS):
            cols = slice(gi * C, (gi + 1) * C)

            def step(c, carry, cols=cols, win=win, gi=gi):
                r0 = pl.multiple_of(c * POOL_ROWS, POOL_ROWS)
                ext = xp_ref[pl.ds(r0, POOL_ROWS + POOL_HALO), cols]
                pooled, _ = _pool_window_mean_minus_token(ext, r0, win)
                y = _dot_nn(pooled.astype(BF16), w_ref[gi]) * sc_ref[:, cols]
                y_ref[pl.ds(r0, POOL_ROWS), cols] = y.astype(BF16)
                return carry

            lax.fori_loop(0, nsteps, step, 0)

    return pl.pallas_call(
        body, name=name, grid=(1,),
        in_specs=[pl.BlockSpec((S, D_POOL), lambda i: (0, 0)),
                  pl.BlockSpec((4, C, C), lambda i: (0, 0, 0)),
                  pl.BlockSpec((1, D_POOL), lambda i: (0, 0))],
        out_specs=pl.BlockSpec((S, D_POOL), lambda i: (0, 0)),
        out_shape=jax.ShapeDtypeStruct((S, D_POOL), BF16),
        scratch_shapes=[pltpu.VMEM((S + POOL_HALO, D_POOL), F32)],
        compiler_params=_cparams(("arbitrary",), 48),
    )(u, w_pool, scale)


def _pool_bwd(u, dcat, w_pool, scale, *, name):
    S = u.shape[0]
    C = POOL_GROUP_DIM
    nsteps = S // POOL_ROWS

    def body(p_ref, dy_ref, w_ref, sc_ref, dp_ref, dw_ref, dsc_ref, xp_ref, e_ref, neg_ref):
        xp_ref[0:POOL_HALO, :] = jnp.zeros((POOL_HALO, D_POOL), F32)
        xp_ref[POOL_HALO:, :] = p_ref[...]
        e_ref[S:, :] = jnp.zeros((POOL_HALO, C), F32)
        for gi, win in enumerate(POOL_WINDOWS):
            cols = slice(gi * C, (gi + 1) * C)

            def step_a(c, carry, cols=cols, win=win, gi=gi):
                dw, dsc = carry
                r0 = pl.multiple_of(c * POOL_ROWS, POOL_ROWS)
                ext = xp_ref[pl.ds(r0, POOL_ROWS + POOL_HALO), cols]
                pooled, cnt = _pool_window_mean_minus_token(ext, r0, win)
                pb = pooled.astype(BF16)
                wv = w_ref[gi]
                dy = dy_ref[pl.ds(r0, POOL_ROWS), cols]
                dsc = dsc + jnp.sum(dy * _dot_nn(pb, wv), axis=0, keepdims=True)
                dyp = (dy * sc_ref[:, cols]).astype(BF16)
                dw = dw + _dot_tn(pb, dyp)
                dpooled = _dot_nt(dyp, wv)
                e_ref[pl.ds(r0, POOL_ROWS), :] = dpooled / cnt
                neg_ref[pl.ds(r0, POOL_ROWS), :] = -dpooled
                return dw, dsc

            dw, dsc = lax.fori_loop(0, nsteps, step_a, (jnp.zeros((C, C), F32), jnp.zeros((1, C), F32)))
            dw_ref[gi] = dw
            dsc_ref[:, cols] = dsc

            def step_b(c, carry, cols=cols, win=win):
                r0 = pl.multiple_of(c * POOL_ROWS, POOL_ROWS)
                s = e_ref[pl.ds(r0, POOL_ROWS + POOL_HALO), :]
                n = POOL_ROWS + POOL_HALO
                k = 1
                while k < win:
                    s = s + pltpu.roll(s, n - k, 0)
                    k *= 2
                du = s[:POOL_ROWS, :] + neg_ref[pl.ds(r0, POOL_ROWS), :]
                dp_ref[pl.ds(r0, POOL_ROWS), cols] = du.astype(BF16)
                return carry

            lax.fori_loop(0, nsteps, step_b, 0)

    return pl.pallas_call(
        body, name=name, grid=(1,),
        in_specs=[pl.BlockSpec((S, D_POOL), lambda i: (0, 0)),
                  pl.BlockSpec((S, D_POOL), lambda i: (0, 0)),
                  pl.BlockSpec((4, C, C), lambda i: (0, 0, 0)),
                  pl.BlockSpec((1, D_POOL), lambda i: (0, 0))],
        out_specs=[pl.BlockSpec((S, D_POOL), lambda i: (0, 0)),
                   pl.BlockSpec((4, C, C), lambda i: (0, 0, 0)),
                   pl.BlockSpec((1, D_POOL), lambda i: (0, 0))],
        out_shape=[jax.ShapeDtypeStruct((S, D_POOL), BF16), jax.ShapeDtypeStruct((4, C, C), F32),
                   jax.ShapeDtypeStruct((1, D_POOL), F32)],
        scratch_shapes=[pltpu.VMEM((S + POOL_HALO, D_POOL), F32), pltpu.VMEM((S + POOL_HALO, C), F32),
                        pltpu.VMEM((S, C), F32)],
        compiler_params=_cparams(("arbitrary",), 56),
    )(u, dcat, w_pool, scale)


GLA_ROWS = 128
U_Q_BLK, U_K_BLK = 2, 3
U_V_BLK, U_G_BLK = 2, 3
U_R_BLK = 32


def _prefix_sum_rows(x):
    n = x.shape[0]
    row = lax.broadcasted_iota(jnp.int32, x.shape, 0)
    k = 1
    while k < n:
        x = x + jnp.where(row >= k, pltpu.roll(x, k, 0), 0.0)
        k *= 2
    return x


def _suffix_sum_rows(x):
    n = x.shape[0]
    row = lax.broadcasted_iota(jnp.int32, x.shape, 0)
    k = 1
    while k < n:
        x = x + jnp.where(row < n - k, pltpu.roll(x, n - k, 0), 0.0)
        k *= 2
    return x


def _log_sigmoid(z):
    return jnp.minimum(z, 0.0) - jnp.log(1.0 + jnp.exp(-jnp.abs(z)))


def _gla_chunk_terms(la_c, q_c, k_c):
    bc = _prefix_sum_rows(la_c)
    bl = jnp.sum(la_c, axis=0, keepdims=True)
    eb = jnp.exp(bc)
    enb = jnp.exp(-bc)
    etail = jnp.exp(bl - bc)
    qd = q_c * (GLA_DK ** -0.5) * eb
    ki = k_c * enb
    kt = k_c * etail
    d = jnp.exp(bl)
    return eb, enb, etail, qd, ki, kt, d


def _gla_fwd(u, y_pool, w_alpha, b_alpha, gnorm, *, name):
    S = u.shape[0]
    RB = GLA_ROWS
    ncc = RB // CHUNK
    H, DK, DV = GLA_HEADS, GLA_DK, GLA_DV

    def body(q_ref, k_ref, v_ref, go_ref, r_ref, yp_ref, wa_ref, ba_ref, gn_ref, cat_ref, o_ref, st_ref, state):
        i = pl.program_id(0)

        @pl.when(i == 0)
        def _():
            state[...] = jnp.zeros_like(state)

        cat_ref[:, :D_POOL] = yp_ref[...]
        y_ref = cat_ref.at[:, D_POOL:]

        z = _dot_nn(r_ref[...].astype(BF16), wa_ref[...]) + ba_ref[...]
        la = _log_sigmoid(z) / GATE_LOGIT_NORMALIZER
        ri = lax.broadcasted_iota(jnp.int32, (CHUNK, CHUNK), 0)
        ci = lax.broadcasted_iota(jnp.int32, (CHUNK, CHUNK), 1)
        tri = ri >= ci
        gn = gn_ref[...]
        for cc in range(ncc):
            rs = slice(cc * CHUNK, (cc + 1) * CHUNK)
            for h in range(H):
                ks = slice(h * DK, (h + 1) * DK)
                vs = slice(h * DV, (h + 1) * DV)
                _, _, _, qd, ki, kt, d = _gla_chunk_terms(la[rs, ks], q_ref[rs, ks], k_ref[rs, ks])
                qdb = qd.astype(BF16)
                vb = v_ref[rs, vs].astype(BF16)
                p = jnp.where(tri, _dot_nt(qdb, ki.astype(BF16)), 0.0)
                st = state[h]
                st_ref[cc, h] = st
                o = _dot_nn(p.astype(BF16), vb) + _dot_nt(qdb, st.astype(BF16))
                state[h] = st * d + _dot_tn(vb, kt.astype(BF16))
                o_ref[rs, vs] = o
                rinv = lax.rsqrt(jnp.mean(o * o, axis=-1, keepdims=True) + EPS)
                go = go_ref[rs, vs]
                y_ref[rs, vs] = (o * rinv * gn * (go * _sigmoid(go))).astype(BF16)

    nblk = S // RB
    return pl.pallas_call(
        body, name=name, grid=(nblk,),
        in_specs=[
            pl.BlockSpec((RB, GLA_DK_TOTAL), lambda i: (i, U_Q_BLK)),
            pl.BlockSpec((RB, GLA_DK_TOTAL), lambda i: (i, U_K_BLK)),
            pl.BlockSpec((RB, D_GLA), lambda i: (i, U_V_BLK)),
            pl.BlockSpec((RB, D_GLA), lambda i: (i, U_G_BLK)),
            pl.BlockSpec((RB, LANES), lambda i: (i, U_R_BLK)),
            pl.BlockSpec((RB, D_POOL), lambda i: (i, 0)),
            pl.BlockSpec((LANES, GLA_DK_TOTAL), lambda i: (0, 0)),
            pl.BlockSpec((1, GLA_DK_TOTAL), lambda i: (0, 0)),
            pl.BlockSpec((1, DV), lambda i: (0, 0)),
        ],
        out_specs=[
            pl.BlockSpec((RB, D_POOL + D_GLA), lambda i: (i, 0)),
            pl.BlockSpec((RB, D_GLA), lambda i: (i, 0)),
            pl.BlockSpec((ncc, H, DV, DK), lambda i: (i, 0, 0, 0)),
        ],
        out_shape=[jax.ShapeDtypeStruct((S, D_POOL + D_GLA), BF16), jax.ShapeDtypeStruct((S, D_GLA), F32),
                   jax.ShapeDtypeStruct((S // CHUNK, H, DV, DK), F32)],
        scratch_shapes=[pltpu.VMEM((H, DV, DK), F32)],
        compiler_params=_cparams(("arbitrary",), 32),
    )(u, u, u, u, u, y_pool, w_alpha, b_alpha, gnorm)


def _gla_bwd(u, o, states, dcat, dpool, w_alpha, b_alpha, gnorm, *, name):
    S = u.shape[0]
    RB = GLA_ROWS
    ncc = RB // CHUNK
    H, DK, DV = GLA_HEADS, GLA_DK, GLA_DV
    nblk = S // RB
    o_q, o_k = D_POOL, D_POOL + GLA_DK_TOTAL
    o_v, o_g, o_r = o_k + GLA_DK_TOTAL, o_k + GLA_DK_TOTAL + D_GLA, o_k + GLA_DK_TOTAL + 2 * D_GLA

    def body(q_ref, k_ref, v_ref, go_ref, r_ref, o_ref, st_ref, dy_ref, dpool_ref, wa_ref, ba_ref, gn_ref,
             du_ref, dwa_ref, dba_ref, dgn_ref, dstate, dz_ref):
        i = pl.program_id(0)

        @pl.when(i == 0)
        def _():
            dstate[...] = jnp.zeros_like(dstate)
            dwa_ref[...] = jnp.zeros_like(dwa_ref)
            dba_ref[...] = jnp.zeros_like(dba_ref)
            dgn_ref[...] = jnp.zeros_like(dgn_ref)

        du_ref[:, :o_q] = dpool_ref[...]
        dq_ref, dk_ref = du_ref.at[:, o_q:o_k], du_ref.at[:, o_k:o_v]
        dv_ref, dgo_ref, dr_ref = du_ref.at[:, o_v:o_g], du_ref.at[:, o_g:o_r], du_ref.at[:, o_r:]

        rb = r_ref[...].astype(BF16)
        wa = wa_ref[...]
        z = _dot_nn(rb, wa) + ba_ref[...]
        la = _log_sigmoid(z) / GATE_LOGIT_NORMALIZER
        ri = lax.broadcasted_iota(jnp.int32, (CHUNK, CHUNK), 0)
        ci = lax.broadcasted_iota(jnp.int32, (CHUNK, CHUNK), 1)
        tri = ri >= ci
        last_row = lax.broadcasted_iota(jnp.int32, (CHUNK, DK), 0) == CHUNK - 1
        gn = gn_ref[...]
        dgn = jnp.zeros((1, DV), F32)
        for cc in reversed(range(ncc)):
            rs = slice(cc * CHUNK, (cc + 1) * CHUNK)
            for h in range(H):
                ks = slice(h * DK, (h + 1) * DK)
                vs = slice(h * DV, (h + 1) * DV)
                eb, enb, etail, qd, ki, kt, d = _gla_chunk_terms(la[rs, ks], q_ref[rs, ks], k_ref[rs, ks])
                qdb, kib, ktb = qd.astype(BF16), ki.astype(BF16), kt.astype(BF16)
                vb = v_ref[rs, vs].astype(BF16)
                p = jnp.where(tri, _dot_nt(qdb, kib), 0.0)
                ov = o_ref[rs, vs]
                go = go_ref[rs, vs]
                dy = dy_ref[rs, vs]
                rinv = lax.rsqrt(jnp.mean(ov * ov, axis=-1, keepdims=True) + EPS)
                oh = ov * rinv
                sg = _sigmoid(go)
                dgo_ref[rs, vs] = (dy * (oh * gn) * (sg * (1.0 + go * (1.0 - sg)))).astype(BF16)
                don = dy * (go * sg)
                dgn = dgn + jnp.sum(don * oh, axis=0, keepdims=True)
                doh = don * gn
                do = rinv * (doh - oh * jnp.mean(doh * oh, axis=-1, keepdims=True))
                dob = do.astype(BF16)
                st = st_ref[cc, h]
                dst = dstate[h]
                stb, dstb = st.astype(BF16), dst.astype(BF16)
                dp = jnp.where(tri, _dot_nt(dob, vb), 0.0).astype(BF16)
                dv_ref[rs, vs] = (_dot_tn(p.astype(BF16), dob) + _dot_nt(ktb, dstb)).astype(BF16)
                dqd = _dot_nn(dp, kib) + _dot_nn(dob, stb)
                dki = _dot_tn(dp, qdb)
                dkt = _dot_nn(vb, dstb)
                dd = jnp.sum(dst * st, axis=0, keepdims=True)
                dstate[h] = dst * d + _dot_tn(dob, qdb)
                dq_ref[rs, ks] = (dqd * eb * (DK ** -0.5)).astype(BF16)
                dk_ref[rs, ks] = (dki * enb + dkt * etail).astype(BF16)
                dbl = jnp.sum(dkt * kt, axis=0, keepdims=True) + dd * d
                dbc = dqd * qd - dki * ki - dkt * kt
                dbc = dbc + jnp.where(last_row, dbl, 0.0)
                dla = _suffix_sum_rows(dbc)
                dz_ref[rs, ks] = dla * (1.0 / GATE_LOGIT_NORMALIZER) * (1.0 - _sigmoid(z[rs, ks]))
        dz = dz_ref[...]
        dzb = dz.astype(BF16)
        dr_ref[...] = _dot_nt(dzb, wa).astype(BF16)
        dwa_ref[...] += _dot_tn(rb, dzb)
        dba_ref[...] += jnp.sum(dz, axis=0, keepdims=True)
        dgn_ref[...] += dgn

    def rev(blk):
        return lambda i: (nblk - 1 - i, blk)

    return pl.pallas_call(
        body, name=name, grid=(nblk,),
        in_specs=[
            pl.BlockSpec((RB, GLA_DK_TOTAL), rev(U_Q_BLK)),
            pl.BlockSpec((RB, GLA_DK_TOTAL), rev(U_K_BLK)),
            pl.BlockSpec((RB, D_GLA), rev(U_V_BLK)),
            pl.BlockSpec((RB, D_GLA), rev(U_G_BLK)),
            pl.BlockSpec((RB, LANES), rev(U_R_BLK)),
            pl.BlockSpec((RB, D_GLA), rev(0)),
            pl.BlockSpec((ncc, H, DV, DK), lambda i: (nblk - 1 - i, 0, 0, 0)),
            pl.BlockSpec((RB, D_GLA), rev(1)),
            pl.BlockSpec((RB, D_POOL), rev(0)),
            pl.BlockSpec((LANES, GLA_DK_TOTAL), lambda i: (0, 0)),
            pl.BlockSpec((1, GLA_DK_TOTAL), lambda i: (0, 0)),
            pl.BlockSpec((1, DV), lambda i: (0, 0)),
        ],
        out_specs=[
            pl.BlockSpec((RB, D_IN_PAD), rev(0)),
            pl.BlockSpec((LANES, GLA_DK_TOTAL), lambda i: (0, 0)),
            pl.BlockSpec((1, GLA_DK_TOTAL), lambda i: (0, 0)),
            pl.BlockSpec((1, DV), lambda i: (0, 0)),
        ],
        out_shape=[
            jax.ShapeDtypeStruct((S, D_IN_PAD), BF16),
            jax.ShapeDtypeStruct((LANES, GLA_DK_TOTAL), F32), jax.ShapeDtypeStruct((1, GLA_DK_TOTAL), F32),
            jax.ShapeDtypeStruct((1, DV), F32),
        ],
        scratch_shapes=[pltpu.VMEM((H, DV, DK), F32), pltpu.VMEM((RB, GLA_DK_TOTAL), F32)],
        compiler_params=_cparams(("arbitrary",), 32),
    )(u, u, u, u, u, o, states, dcat, dpool, w_alpha, b_alpha, gnorm)


def _row_tile(rows, cols, itemsize, budget=2 * 1024 * 1024):
    if rows * cols * itemsize <= budget or rows % 16:
        return rows
    best = 16
    for t in range(16, rows + 1, 16):
        if rows % t == 0 and t * cols * itemsize <= budget:
            best = t
    return best


def _adamw(w, g, m, v, *, name):
    R, C = w.shape
    tr = _row_tile(R, C, 4, budget=1024 * 1024)

    def body(w_ref, g_ref, m_ref, v_ref, go_ref, d_ref, nm_ref, nv_ref):
        gv = g_ref[...]
        go_ref[...] = gv
        mn = ADAM_B1 * m_ref[...] + (1.0 - ADAM_B1) * gv
        vn = ADAM_B2 * v_ref[...] + (1.0 - ADAM_B2) * jnp.square(gv)
        m_hat = mn / (1.0 - ADAM_B1 ** ADAM_STEP)
        v_hat = vn / (1.0 - ADAM_B2 ** ADAM_STEP)
        d_ref[...] = -ADAM_LR * (m_hat / (jnp.sqrt(v_hat) + ADAM_EPS) + ADAM_WD * w_ref[...])
        nm_ref[...] = mn
        nv_ref[...] = vn

    spec = pl.BlockSpec((tr, C), lambda i: (i, 0))
    shp = jax.ShapeDtypeStruct((R, C), F32)
    return pl.pallas_call(
        body, name=name, grid=(R // tr,), in_specs=[spec] * 4, out_specs=[spec] * 4, out_shape=[shp] * 4,
        compiler_params=_cparams(("parallel",), 32),
    )(w, g, m, v)


def _pair_sum(g4, recv, c_idx, *, name):
    ns, _, R2, C = g4.shape
    tr = _row_tile(R2, C, 2)

    def body(c_ref, g_ref, r_ref, o_ref):
        o_ref[...] = (g_ref[...].astype(F32) + r_ref[...].astype(F32)).astype(BF16)

    return pl.pallas_call(
        body, name=name,
        grid_spec=pltpu.PrefetchScalarGridSpec(
            num_scalar_prefetch=1, grid=(ns, R2 // tr),
            in_specs=[pl.BlockSpec((None, None, tr, C), lambda s, i, c: (s, c[0], i, 0)),
                      pl.BlockSpec((None, tr, C), lambda s, i, c: (s, i, 0))],
            out_specs=pl.BlockSpec((None, tr, C), lambda s, i, c: (s, i, 0)),
        ),
        out_shape=jax.ShapeDtypeStruct((ns, R2, C), BF16),
        compiler_params=_cparams(("parallel", "parallel"), 32),
    )(c_idx, g4, recv)


def _chip_sum(part, recv, sc_idx, *, name):
    _, R2, C = part.shape
    tr = _row_tile(R2, C, 4)
    nblk = R2 // tr

    def body(s_ref, p_ref, r_ref, o_ref):
        acc = p_ref[...].astype(F32)
        for j in range(N_CHIPS - 1):
            acc = acc + r_ref[j].astype(F32)
        o_ref[...] = acc

    return pl.pallas_call(
        body, name=name,
        grid_spec=pltpu.PrefetchScalarGridSpec(
            num_scalar_prefetch=1, grid=(nblk,),
            in_specs=[pl.BlockSpec((None, tr, C), lambda i, s: (s[0], i, 0)),
                      pl.BlockSpec((N_CHIPS - 1, tr, C), lambda i, s: (0, i, 0))],
            out_specs=pl.BlockSpec((tr, C), lambda i, s: (s[1] * nblk + i, 0)),
        ),
        out_shape=jax.ShapeDtypeStruct((2 * R2, C), F32),
        compiler_params=_cparams(("parallel",), 32),
    )(sc_idx, part, recv)


def _cast_into_slot(w, sc_idx, dtype, after, *, name):
    R, C = w.shape
    tr = _row_tile(R, C, 4)

    def body(s_ref, w_ref, after_ref, o_ref):
        o_ref[...] = w_ref[...].astype(dtype)

    return pl.pallas_call(
        body, name=name,
        grid_spec=pltpu.PrefetchScalarGridSpec(
            num_scalar_prefetch=1, grid=(R // tr,),
            in_specs=[pl.BlockSpec((tr, C), lambda i, s: (i, 0)), pl.BlockSpec(memory_space=pl.ANY)],
            out_specs=pl.BlockSpec((None, tr, C), lambda i, s: (s[0], i, 0)),
        ),
        out_shape=jax.ShapeDtypeStruct((N_CHIPS, R, C), dtype),
        compiler_params=_cparams(("parallel",), 32),
    )(sc_idx, w, after)


def _slab_sum(slabs, *, name):
    n, M, C = slabs.shape

    def body(x_ref, o_ref):
        acc = x_ref[0]
        for d in range(1, n):
            acc = acc + x_ref[d]
        o_ref[...] = acc

    return pl.pallas_call(
        body, name=name, out_shape=jax.ShapeDtypeStruct((M, C), F32),
    )(slabs)


def _mesh_position():
    x, y, c = lax.axis_index("x"), lax.axis_index("y"), lax.axis_index("c")
    other_chips = [(1 - x, y), (x, 1 - y), (1 - x, 1 - y)]
    return x, y, c, other_chips


ANY = pl.BlockSpec(memory_space=pl.ANY)


HBM = pl.BlockSpec(memory_space=pltpu.HBM)
SEM = pl.BlockSpec(memory_space=pltpu.SEMAPHORE)
SPLIT_COPY = pltpu.CompilerParams(has_side_effects=pltpu.SideEffectType.DATAFLOW_SIDE_EFFECTING)
TOKEN = jax.ShapeDtypeStruct((8, LANES), F32)


def _in_hbm(a):
    return pltpu.with_memory_space_constraint(a, pltpu.HBM)


def _half_rows(ref, slot, half):
    hr = ref.shape[1] // 2
    return ref.at[slot, pl.ds(half * hr, hr), :]


def _gather_ici_start(groups, *, name):
    flat = [b for g in groups for b in g]
    K, G = len(flat), len(groups)

    def body(*refs):
        ins, sems, token = refs[:K], refs[K:K + 2 * G], refs[-1]
        x, y, c, chips = _mesh_position()
        s = 2 * x + y
        k = 0
        for gi, g in enumerate(groups):
            for n in range(len(g)):
                own = _half_rows(ins[k], s, c)
                for j, chip in enumerate(chips):
                    pltpu.make_async_remote_copy(
                        src_ref=own, dst_ref=own, send_sem=sems[2 * gi].at[n * (N_CHIPS - 1) + j],
                        recv_sem=sems[2 * gi + 1].at[n * (N_CHIPS - 1) + j],
                        device_id=(*chip, c), device_id_type=MESH).start()
                k += 1
        token[...] = jnp.zeros_like(token)

    sem_shapes = []
    for g in groups:
        sem_shapes += [pltpu.SemaphoreType.DMA((len(g) * (N_CHIPS - 1),))] * 2
    out = pl.pallas_call(
        body, name=name,
        in_specs=[HBM] * K,
        out_specs=[SEM] * (2 * G) + [HBM] * K + [pl.BlockSpec(memory_space=pltpu.VMEM)],
        out_shape=sem_shapes + [pltpu.HBM(b.shape, b.dtype) for b in flat] + [TOKEN],
        input_output_aliases={k: 2 * G + k for k in range(K)},
        compiler_params=SPLIT_COPY,
    )(*[_in_hbm(b) for b in flat])
    handles, k = [], 2 * G
    for gi, g in enumerate(groups):
        handles.append((out[2 * gi], out[2 * gi + 1], list(out[k:k + len(g)])))
        k += len(g)
    return handles, out[-1]


def _gather_ici_wait(handle, after, *, name):
    send, recv, bufs = handle
    n = len(bufs)

    def body(*refs):
        ins, send_ref, recv_ref = refs[:n], refs[n], refs[n + 1]
        x, y, c, chips = _mesh_position()
        s = 2 * x + y
        for k in range(n):
            own = _half_rows(ins[k], s, c)
            for j, chip in enumerate(chips):
                cp = pltpu.make_async_remote_copy(
                    src_ref=own, dst_ref=_half_rows(ins[k], 2 * chip[0] + chip[1], c),
                    send_sem=send_ref.at[k * (N_CHIPS - 1) + j], recv_sem=recv_ref.at[k * (N_CHIPS - 1) + j],
                    device_id=(*chip, c), device_id_type=MESH)
                cp.wait_send()
                cp.wait_recv()

    return pl.pallas_call(
        body, name=name,
        in_specs=[HBM] * n + [SEM, SEM, ANY], out_specs=[HBM] * n,
        out_shape=[pltpu.HBM(b.shape, b.dtype) for b in bufs],
        input_output_aliases={k: k for k in range(n)},
        compiler_params=SPLIT_COPY,
    )(*bufs, send, recv, after)


def _forward_halves(bufs, *, name):
    K = len(bufs)
    per = N_CHIPS - 1

    def body(*refs):
        outs = refs[K:2 * K]
        send_sems, recv_sems = refs[2 * K:]
        x, y, c, chips = _mesh_position()
        copies = []
        for k in range(K):
            for j, chip in enumerate(chips):
                got = _half_rows(outs[k], 2 * chip[0] + chip[1], c)
                cp = pltpu.make_async_remote_copy(
                    src_ref=got, dst_ref=got, send_sem=send_sems.at[k * per + j], recv_sem=recv_sems.at[k * per + j],
                    device_id=(x, y, 1 - c), device_id_type=MESH)
                cp.start()
                copies.append(cp)
        for cp in copies:
            cp.wait()

    return pl.pallas_call(
        body, name=name,
        in_specs=[ANY] * K, out_specs=[ANY] * K,
        out_shape=[jax.ShapeDtypeStruct(a.shape, a.dtype) for a in bufs],
        input_output_aliases={k: k for k in range(K)},
        scratch_shapes=[pltpu.SemaphoreType.DMA((K * per,)), pltpu.SemaphoreType.DMA((K * per,))],
    )(*bufs)


def _chip_exchange_copies(srcs, lands, send_sems, recv_sems):
    x, y, c, chips = _mesh_position()
    per = N_CHIPS - 1
    return [pltpu.make_async_remote_copy(
        src_ref=srcs[k].at[2 * chip[0] + chip[1]], dst_ref=lands[k].at[j],
        send_sem=send_sems.at[k * per + j], recv_sem=recv_sems.at[k * per + j],
        device_id=(*chip, c), device_id_type=MESH) for k in range(len(srcs)) for j, chip in enumerate(chips)]


def _sibling_swap_copies(srcs, lands, send_sems, recv_sems):
    x, y, c, _ = _mesh_position()
    return [pltpu.make_async_remote_copy(
        src_ref=srcs[k].at[pl.ds(0, srcs[k].shape[0]), 1 - c], dst_ref=lands[k],
        send_sem=send_sems.at[k], recv_sem=recv_sems.at[k],
        device_id=(x, y, 1 - c), device_id_type=MESH) for k in range(len(srcs))]


def _split_copy_start(srcs, land_shapes, n_sems, copies, *, name):
    K = len(srcs)

    def body(*refs):
        for cp in copies(refs[:K], refs[K:2 * K], refs[2 * K], refs[2 * K + 1]):
            cp.start()
        refs[-1][...] = jnp.zeros_like(refs[-1])

    out = pl.pallas_call(
        body, name=name,
        in_specs=[HBM] * (2 * K),
        out_specs=[SEM, SEM] + [HBM] * (2 * K) + [pl.BlockSpec(memory_space=pltpu.VMEM)],
        out_shape=[pltpu.SemaphoreType.DMA((n_sems,))] * 2
        + [pltpu.HBM(a.shape, a.dtype) for a in srcs]
        + [pltpu.HBM(s, a.dtype) for s, a in zip(land_shapes, srcs)] + [TOKEN],
        input_output_aliases={k: 2 + k for k in range(2 * K)},
        compiler_params=SPLIT_COPY,
    )(*[_in_hbm(a) for a in srcs], *[_in_hbm(lax.empty(s, a.dtype)) for s, a in zip(land_shapes, srcs)])
    return (out[0], out[1], list(out[2:2 + K]), list(out[2 + K:2 + 2 * K])), out[-1]


def _split_copy_wait(handle, copies, after, *, name):
    send, recv, srcs, lands = handle
    K = len(srcs)

    def body(*refs):
        for cp in copies(refs[:K], refs[K:2 * K], refs[2 * K], refs[2 * K + 1]):
            cp.wait_send()
            cp.wait_recv()

    out = pl.pallas_call(
        body, name=name,
        in_specs=[HBM] * (2 * K) + [SEM, SEM, ANY], out_specs=[HBM] * (2 * K),
        out_shape=[pltpu.HBM(a.shape, a.dtype) for a in srcs] + [pltpu.HBM(a.shape, a.dtype) for a in lands],
        input_output_aliases={k: k for k in range(2 * K)},
        compiler_params=SPLIT_COPY,
    )(*srcs, *lands, send, recv, after)
    return list(out[:K]), list(out[K:])


def _join_halves(bufs, *, name):
    K = len(bufs)

    def body(*refs):
        outs = refs[K:2 * K]
        send_sems, recv_sems = refs[2 * K:]
        x, y, c, _ = _mesh_position()
        copies = []
        for k in range(K):
            r2 = bufs[k].shape[0] // 2
            mine = outs[k].at[pl.ds(c * r2, r2), :]
            cp = pltpu.make_async_remote_copy(
                src_ref=mine, dst_ref=mine, send_sem=send_sems.at[k], recv_sem=recv_sems.at[k],
                device_id=(x, y, 1 - c), device_id_type=MESH)
            cp.start()
            copies.append(cp)
        for cp in copies:
            cp.wait()

    return pl.pallas_call(
        body, name=name,
        in_specs=[ANY] * K, out_specs=[ANY] * K,
        out_shape=[jax.ShapeDtypeStruct(a.shape, a.dtype) for a in bufs],
        input_output_aliases={k: k for k in range(K)},
        scratch_shapes=[pltpu.SemaphoreType.DMA((K,)), pltpu.SemaphoreType.DMA((K,))],
    )(*bufs)


def _all_gather_slab(slab):
    m_per, n = slab.shape

    def body(x_ref, out_ref, send_sems, recv_sems, local_sem):
        x, y, c, chips = _mesh_position()
        me, sibling = (x, y, c), (x, y, 1 - c)

        def rows(px, py, pc):
            return out_ref.at[pl.ds((4 * px + 2 * py + pc) * m_per, m_per), :]

        def copy(k, block, to, src=None):
            return pltpu.make_async_remote_copy(
                src_ref=rows(*block) if src is None else src, dst_ref=rows(*block),
                send_sem=send_sems.at[k], recv_sem=recv_sems.at[k], device_id=to, device_id_type=MESH)

        mine = pltpu.make_async_copy(x_ref, rows(*me), local_sem)
        mine.start()
        first = [copy(0, me, sibling, src=x_ref)]
        first += [copy(1 + j, me, (*chip, c), src=x_ref) for j, chip in enumerate(chips)]
        for cp in first:
            cp.start()
        passed = [copy(4 + j, (*chip, c), sibling) for j, chip in enumerate(chips)]
        for j, chip in enumerate(chips):
            copy(1 + j, (*chip, c), me).wait_recv()
            passed[j].start()
        copy(0, sibling, me).wait_recv()
        for j, chip in enumerate(chips):
            copy(4 + j, (*chip, 1 - c), me).wait_recv()
        for cp in first + passed:
            cp.wait_send()
        mine.wait()

    return pl.pallas_call(
        body, name="gather_small_grads",
        out_shape=jax.ShapeDtypeStruct((N_DEV * m_per, n), slab.dtype),
        in_specs=[pl.BlockSpec(memory_space=pltpu.VMEM)],
        out_specs=pl.BlockSpec(memory_space=pltpu.VMEM),
        scratch_shapes=[pltpu.SemaphoreType.DMA((7,)), pltpu.SemaphoreType.DMA((7,)), pltpu.SemaphoreType.DMA],
    )(slab)


def _ffn_backward_weights(dhb, saved, w_in, w_out, after, tag, on_dw_out=None):
    n, gu, a = saved
    S = n.shape[0]
    ns, D, cs = w_in.shape
    F = w_out.shape[0]
    to = 512
    dw_out = _mm_tn(
        a, dhb, grid=(F // to,),
        a_spec=pl.BlockSpec((S, to), lambda j: (0, j)), b_spec=pl.BlockSpec((S, D), lambda j: (0, 0)),
        out_spec=pl.BlockSpec((to, D), lambda j: (j, 0)), out_shape=jax.ShapeDtypeStruct((F, D), BF16),
        scale=0.5, name=f"{tag}_dw_out").reshape(N_CHIPS, F // N_CHIPS, D)
    if on_dw_out is not None:
        after = on_dw_out(dw_out)
    dgu = _ffn_bwd_act(dhb, w_out, gu, after, name=f"{tag}_bwd_act")
    ti = MXU_TILE
    per_g, per_s = F // ti, cs // ti
    dw_in = _mm_tn(
        n.T, dgu, grid=(2 * F // ti,), a_is_transposed=True,
        a_spec=pl.BlockSpec((D, S), lambda j: (0, 0)),
        b_spec=pl.BlockSpec((None, S, ti), lambda j: (lax.div(j, per_g), 0, lax.rem(j, per_g))),
        out_spec=pl.BlockSpec((None, D, ti), lambda j: (lax.div(j, per_s), 0, lax.rem(j, per_s))),
        out_shape=jax.ShapeDtypeStruct((ns, D, cs), BF16), scale=1.0, name=f"{tag}_dw_in")
    return dgu, dw_in, dw_out


def kernel(x, ffn1_norm, ffn1_w_in, ffn1_w_out, mix_norm, w_in_mix, w_pool, pool_scale, w_alpha, b_alpha, gla_norm, w_out_mix, ffn2_norm, ffn2_w_in, ffn2_w_out, final_norm, loss_target, m_ffn1_norm, m_ffn1_w_in, m_ffn1_w_out, m_mix_norm, m_w_in_mix, m_w_pool, m_pool_scale, m_w_alpha, m_b_alpha, m_gla_norm, m_w_out_mix, m_ffn2_norm, m_ffn2_w_in, m_ffn2_w_out, m_final_norm, v_ffn1_norm, v_ffn1_w_in, v_ffn1_w_out, v_mix_norm, v_w_in_mix, v_w_pool, v_pool_scale, v_w_alpha, v_b_alpha, v_gla_norm, v_w_out_mix, v_ffn2_norm, v_ffn2_w_in, v_ffn2_w_out, v_final_norm):
    names = ["ffn1_norm", "ffn1_w_in", "ffn1_w_out", "mix_norm", "w_in_mix", "w_pool", "pool_scale", "w_alpha",
             "b_alpha", "gla_norm", "w_out_mix", "ffn2_norm", "ffn2_w_in", "ffn2_w_out", "final_norm"]
    weights = dict(zip(names, [ffn1_norm, ffn1_w_in, ffn1_w_out, mix_norm, w_in_mix, w_pool, pool_scale, w_alpha,
                               b_alpha, gla_norm, w_out_mix, ffn2_norm, ffn2_w_in, ffn2_w_out, final_norm]))
    moms = dict(zip(names, [m_ffn1_norm, m_ffn1_w_in, m_ffn1_w_out, m_mix_norm, m_w_in_mix, m_w_pool, m_pool_scale,
                            m_w_alpha, m_b_alpha, m_gla_norm, m_w_out_mix, m_ffn2_norm, m_ffn2_w_in, m_ffn2_w_out,
                            m_final_norm]))
    vels = dict(zip(names, [v_ffn1_norm, v_ffn1_w_in, v_ffn1_w_out, v_mix_norm, v_w_in_mix, v_w_pool, v_pool_scale,
                            v_w_alpha, v_b_alpha, v_gla_norm, v_w_out_mix, v_ffn2_norm, v_ffn2_w_in, v_ffn2_w_out,
                            v_final_norm]))
    xi, yi, ci = lax.axis_index("x"), lax.axis_index("y"), lax.axis_index("c")
    chip = 2 * xi + yi
    c_idx = jnp.reshape(ci, (1,)).astype(jnp.int32)
    sc_idx = jnp.stack([chip, ci]).astype(jnp.int32)

    def flat2d(a):
        return a.reshape(-1, a.shape[-1])

    ex = _Exchanges(sc_idx, c_idx)

    def cast(n, after):
        return _cast_into_slot(flat2d(weights[n]), sc_idx, F32 if n == "w_alpha" else BF16, after, name=f"cast_{n}")

    first, rest = _Exchanges.GATHER_GROUPS[:2], _Exchanges.GATHER_GROUPS[2:]
    tok = ex.start_gather({n: cast(n, sc_idx) for g in first for n in g}, first, name="gather_ici_start_ffn1")
    tok2 = ex.start_gather({n: cast(n, tok) for g in rest for n in g}, rest, name="gather_ici_start_rest")
    small_params = dict(g1=ffn1_norm, gm=mix_norm, g2=ffn2_norm, gf=final_norm.reshape(1, D_MODEL),
                        pool_scale=pool_scale, b_alpha=b_alpha, gla_norm=gla_norm)
    loss_blk, dx, small = _forward_backward(x[0], loss_target[0], ex, tok[0, 0] + tok2[0, 0], small_params)

    outs = {}

    def update(n, g):
        w = weights[n]
        w2 = flat2d(w) if w.ndim > 1 else w.reshape(1, -1)
        go, d, nm, nv = _adamw(w2, g.reshape(w2.shape), moms[n].reshape(w2.shape), vels[n].reshape(w2.shape),
                               name=f"adamw_{n}")
        outs[n] = (go.reshape(w.shape), d.reshape(w.shape), nm.reshape(w.shape), nv.reshape(w.shape))
        return nv

    last = dx
    for tag in _Exchanges.REDUCE_ORDER:
        for n, g in ex.reduced(tag, after=last).items():
            last = update(n, g)

    grads = {}
    small_names = ["ffn1_norm", "mix_norm", "ffn2_norm", "final_norm", "pool_scale", "b_alpha", "gla_norm", "w_alpha",
                   "loss"]
    small = small + [loss_blk[0:1]]
    rows = [a.size // LANES for a in small]
    slab = jnp.concatenate([a.reshape(-1, LANES) for a in small], axis=0)
    pad = -slab.shape[0] % 8
    slab = jnp.pad(slab, ((0, pad), (0, 0)))
    gathered = _all_gather_slab(slab).reshape(N_DEV, slab.shape[0], LANES)
    total = _slab_sum(gathered, name="sum_small_grads")
    off = 0
    for n, a, r in zip(small_names, small, rows):
        grads[n] = total[off:off + r].reshape(a.shape)
        off += r
    grads["w_alpha"] = lax.dynamic_slice_in_dim(grads["w_alpha"], chip * (GLA_DK_TOTAL // N_CHIPS),
                                                GLA_DK_TOTAL // N_CHIPS, axis=1)

    loss = grads.pop("loss")[0, 0]
    for n in small_names[:-1]:
        update(n, grads[n])
    return (loss, dx[None], *[outs[n][0] for n in names], *[outs[n][1] for n in names],
            *[outs[n][2] for n in names], *[outs[n][3] for n in names])


class _Exchanges:
    GATHER_GROUPS = (("ffn1_w_in",), ("ffn1_w_out",), ("w_in_mix", "w_pool", "w_alpha"), ("w_out_mix",),
                     ("ffn2_w_in",), ("ffn2_w_out",))
    REDUCE_ORDER = ("ffn2", "mix", "ffn1_out", "ffn1_in")

    def __init__(self, sc_idx, c_idx):
        self.sc_idx, self.c_idx = sc_idx, c_idx
        self._gathers, self._swaps, self._reduces = {}, {}, {}

    def start_gather(self, bufs, groups, *, name):
        handles, token = _gather_ici_start([[bufs[n] for n in g] for g in groups], name=name)
        for g, h in zip(groups, handles):
            self._gathers[g[0]] = (g, h)
        return token

    def gathered(self, first, after):
        names, handle = self._gathers.pop(first)
        got = _gather_ici_wait(handle, after, name=f"gather_ici_wait_{first}")
        return dict(zip(names, _forward_halves(got, name=f"gather_forward_{first}")))

    def begin_reduce(self, tag, full):
        g4 = [a.reshape(N_CHIPS, 2, a.shape[1] // 2, a.shape[2]) for a in full.values()]
        lands = [(a.shape[0],) + a.shape[2:] for a in g4]
        handle, token = _split_copy_start(g4, lands, len(g4), _sibling_swap_copies, name=f"swap_start_{tag}")
        self._swaps[tag] = (list(full), handle)
        return token

    def start_reduce(self, tag, after):
        names, handle = self._swaps.pop(tag)
        g4, from_sibling = _split_copy_wait(handle, _sibling_swap_copies, after, name=f"swap_wait_{tag}")
        pair = [_pair_sum(a, b, self.c_idx, name=f"pair_sum_{n}") for n, a, b in zip(names, g4, from_sibling)]
        lands = [(N_CHIPS - 1,) + a.shape[1:] for a in pair]
        handle, token = _split_copy_start(pair, lands, len(pair) * (N_CHIPS - 1), _chip_exchange_copies,
                                          name=f"exchange_start_{tag}")
        self._reduces[tag] = (names, handle)
        return token

    def reduced(self, tag, after):
        names, handle = self._reduces.pop(tag)
        pair, lands = _split_copy_wait(handle, _chip_exchange_copies, after, name=f"exchange_wait_{tag}")
        halves = [_chip_sum(a, b, self.sc_idx, name=f"chip_sum_{n}") for n, a, b in zip(names, pair, lands)]
        return dict(zip(names, _join_halves(halves, name=f"join_halves_{tag}")))


def _forward_backward(h0, target, ex, started, sp):
    g1, gm, g2, gf = sp["g1"], sp["gm"], sp["g2"], sp["gf"]
    pool_scale, b_alpha, gla_norm = sp["pool_scale"], sp["b_alpha"], sp["gla_norm"]
    cs_mix = D_IN // N_CHIPS

    n1 = _rms_fwd(h0, g1 + started, name="ffn1_norm")
    w1_in = ex.gathered("ffn1_w_in", after=n1)["ffn1_w_in"]
    gu1, a1 = _ffn_up(n1, w1_in, name="ffn1_up")
    w1_out = ex.gathered("ffn1_w_out", after=a1)["ffn1_w_out"].reshape(D_FF, D_MODEL)
    h1 = _mm_nn(a1, w1_out, h0, 0.5, tm=512, tn=D_MODEL, tk=SHARD_TILE, name="ffn1_down")
    saved1 = (n1, gu1, a1)
    n_mix = _rms_fwd(h1, gm, name="mix_norm")
    gw = ex.gathered("w_in_mix", after=n_mix)
    w_mix = jnp.concatenate([gw["w_in_mix"][t] for t in range(N_CHIPS)], axis=1)
    w_mix = jnp.pad(w_mix, ((0, 0), (0, D_IN_PAD - D_IN)))[None]
    wp = gw["w_pool"].reshape(N_CHIPS, 4, POOL_GROUP_DIM // N_CHIPS, POOL_GROUP_DIM)
    wp = wp.transpose(1, 0, 2, 3).reshape(4, POOL_GROUP_DIM, POOL_GROUP_DIM)
    wa = gw["w_alpha"].transpose(1, 0, 2).reshape(GLA_GATE_RANK, GLA_DK_TOTAL)
    wa = jnp.pad(wa, ((0, LANES - GLA_GATE_RANK), (0, 0))).astype(BF16)
    u = _mm_nn(n_mix, w_mix[0], None, 1.0, tm=512, tn=D_IN_PAD, tk=D_MODEL, name="mix_in")
    y_pool = _pool_fwd(u, wp, pool_scale, name="pool_fwd")
    cat, o_gla, states = _gla_fwd(u, y_pool, wa, b_alpha, gla_norm, name="gla_fwd")
    w_omix = ex.gathered("w_out_mix", after=cat)["w_out_mix"].reshape(D_MODEL, D_MODEL)
    h2 = _mm_nn(cat, w_omix, h1, 1.0, tm=512, tn=D_MODEL, tk=1024, name="mix_out")
    n3 = _rms_fwd(h2, g2, name="ffn2_norm")
    w2_in = ex.gathered("ffn2_w_in", after=n3)["ffn2_w_in"]
    gu2, a2 = _ffn_up(n3, w2_in, name="ffn2_up")
    w2_out = ex.gathered("ffn2_w_out", after=a2)["ffn2_w_out"].reshape(D_FF, D_MODEL)
    h3 = _mm_nn(a2, w2_out, h2, 0.5, tm=512, tn=D_MODEL, tk=SHARD_TILE, name="ffn2_down")
    saved2 = (n3, gu2, a2)
    loss_blk, dh3, dh3b, d_gf = _final_loss(h3, gf, target, name="final_loss")

    dgu2, dw2_in, dw2_out = _ffn_backward_weights(dh3b, saved2, w2_in, w2_out, dh3b, "ffn2")
    tok = ex.begin_reduce("ffn2", {"ffn2_w_in": dw2_in, "ffn2_w_out": dw2_out})
    dh2, dh2b, d_g2 = _mm_nt_rmsbwd(dgu2, w2_in, h2, dh3, g2 + tok[0, 0], tk=COL_TILE, name="ffn2_dx")
    tok = ex.start_reduce("ffn2", after=dh2b)
    S = h0.shape[0]
    dcat = _mm_nt(dh2b, w_omix, tok, tm=512, tn=1024, name="mix_out_dx")
    dw_omix = _mm_tn(
        cat, dh2b, grid=(4,),
        a_spec=pl.BlockSpec((S, 512), lambda j: (0, j)), b_spec=pl.BlockSpec((S, D_MODEL), lambda j: (0, 0)),
        out_spec=pl.BlockSpec((512, D_MODEL), lambda j: (j, 0)),
        out_shape=jax.ShapeDtypeStruct((D_MODEL, D_MODEL), BF16), scale=1.0, name="mix_out_dw")
    dp, dw_pool, d_pscale = _pool_bwd(u, dcat, wp, pool_scale, name="pool_bwd")
    du, d_wa, d_ba, d_gn = _gla_bwd(u, o_gla, states, dcat, dp, wa, b_alpha, gla_norm, name="gla_bwd")
    du = du[None]
    tn_mix = COL_TILE
    dw_mix = _mm_tn(
        n_mix.T, du, grid=(2, D_IN_PAD // tn_mix), a_is_transposed=True,
        a_spec=pl.BlockSpec((D_MODEL // 2, S), lambda i, j: (i, 0)),
        b_spec=pl.BlockSpec((None, S, tn_mix), lambda i, j: (0, 0, j)),
        out_spec=pl.BlockSpec((D_MODEL // 2, tn_mix), lambda i, j: (i, j)),
        out_shape=jax.ShapeDtypeStruct((D_MODEL, D_IN_PAD), BF16), scale=1.0, name="mix_in_dw")
    dw_mix_s = dw_mix[:, :D_IN].reshape(D_MODEL, N_CHIPS, cs_mix).transpose(1, 0, 2)
    dw_pool_s = dw_pool.reshape(4, N_CHIPS, POOL_GROUP_DIM // N_CHIPS, POOL_GROUP_DIM).transpose(1, 0, 2, 3)
    dw_pool_s = dw_pool_s.reshape(N_CHIPS, POOL_GROUP_DIM, POOL_GROUP_DIM).astype(BF16)
    tok = ex.begin_reduce("mix", {"w_in_mix": dw_mix_s,
                                  "w_out_mix": dw_omix.reshape(N_CHIPS, D_MODEL // N_CHIPS, D_MODEL),
                                  "w_pool": dw_pool_s})
    dh1, dh1b, d_gm = _mm_nt_rmsbwd(du, w_mix, h1, dh2, gm + tok[0, 0], tk=tn_mix, name="mix_in_dx")
    tok = ex.start_reduce("mix", after=dh1b)
    def reduce_now(tag, full):
        return ex.start_reduce(tag, after=ex.begin_reduce(tag, full))

    mix_started = tok
    dgu1, dw1_in, _ = _ffn_backward_weights(
        dh1b, saved1, w1_in, w1_out, None, "ffn1",
        on_dw_out=lambda dw: mix_started + reduce_now("ffn1_out", {"ffn1_w_out": dw}))
    tok = reduce_now("ffn1_in", {"ffn1_w_in": dw1_in})
    dx, _, d_g1 = _mm_nt_rmsbwd(dgu1, w1_in, h0, dh1, g1 + tok[0, 0], tk=COL_TILE, name="ffn1_dx")
    small = [d_g1, d_gm, d_g2, d_gf, d_pscale, d_ba, d_gn, d_wa[:GLA_GATE_RANK]]
    return loss_blk, dx, small
```

```python
import functools

import jax
import jax.numpy as jnp
from jax import lax
from jax.experimental import pallas as pl
from jax.experimental.pallas import tpu as pltpu

F32 = jnp.float32
BF16 = jnp.bfloat16
MESH = pl.DeviceIdType.MESH

D_MODEL = 2048
D_FF = 5632
D_POOL = 1024
POOL_WINDOWS = (2, 4, 8, 16)
POOL_GROUP_DIM = 256
D_GLA = 1024
GLA_HEADS = 4
GLA_DV = 256
GLA_DK = 128
GLA_DK_TOTAL = 512
GLA_GATE_RANK = 16
GATE_LOGIT_NORMALIZER = 16.0
CHUNK = 64
D_IN = 4112
D_IN_PAD = 4224
EPS = 1e-6

ADAM_LR = 0.001
ADAM_B1 = 0.9
ADAM_B2 = 0.999
ADAM_EPS = 1e-08
ADAM_WD = 0.01
ADAM_STEP = 10

N_CHIPS = 4
N_DEV = 8
V7X_VMEM_BYTES = 64 * 1024 * 1024
LANES = 128
MXU_TILE = 256
COL_TILE = 1408
SHARD_TILE = 2816


def _cparams(semantics, vmem_mb):
    assert vmem_mb * 1024 * 1024 < V7X_VMEM_BYTES
    return pltpu.CompilerParams(dimension_semantics=semantics, vmem_limit_bytes=vmem_mb * 1024 * 1024)


def _dot_nn(a, b):
    return jnp.dot(a, b, preferred_element_type=F32)


def _dot_nt(a, b):
    return lax.dot_general(a, b, (((1,), (1,)), ((), ())), preferred_element_type=F32)


def _dot_tn(a, b):
    return lax.dot_general(a, b, (((0,), (0,)), ((), ())), preferred_element_type=F32)


def _sigmoid(x):
    return 1.0 / (1.0 + jnp.exp(-x))


def _rms_fwd(x, g, *, name):
    S, D = x.shape
    tm = 256

    def body(x_ref, g_ref, o_ref):
        xv = x_ref[...]
        r = lax.rsqrt(jnp.mean(xv * xv, axis=-1, keepdims=True) + EPS)
        o_ref[...] = (xv * r * g_ref[...]).astype(BF16)

    return pl.pallas_call(
        body, name=name, grid=(S // tm,),
        in_specs=[pl.BlockSpec((tm, D), lambda i: (i, 0)), pl.BlockSpec((1, D), lambda i: (0, 0))],
        out_specs=pl.BlockSpec((tm, D), lambda i: (i, 0)),
        out_shape=jax.ShapeDtypeStruct((S, D), BF16),
        compiler_params=_cparams(("parallel",), 32),
    )(x, g)


def _ffn_up(n, w_in, *, name):
    S, D = n.shape
    ns, _, cs = w_in.shape
    half = ns // 2
    F = cs * half
    tm, tn = 128, SHARD_TILE
    nb = cs // tn

    def body(n_ref, wg_ref, wu_ref, gu_ref, a_ref):
        nv = n_ref[...]
        g = _dot_nn(nv, wg_ref[...])
        u = _dot_nn(nv, wu_ref[...])
        gu_ref[0] = g.astype(BF16)
        gu_ref[1] = u.astype(BF16)
        a_ref[...] = (g * _sigmoid(g) * u).astype(BF16)

    return pl.pallas_call(
        body, name=name, grid=(F // tn, S // tm),
        in_specs=[
            pl.BlockSpec((tm, D), lambda j, i: (i, 0)),
            pl.BlockSpec((None, D, tn), lambda j, i: (lax.div(j, nb), 0, lax.rem(j, nb))),
            pl.BlockSpec((None, D, tn), lambda j, i: (half + lax.div(j, nb), 0, lax.rem(j, nb))),
        ],
        out_specs=[
            pl.BlockSpec((2, tm, tn), lambda j, i: (0, i, j)),
            pl.BlockSpec((tm, tn), lambda j, i: (i, j)),
        ],
        out_shape=[jax.ShapeDtypeStruct((2, S, F), BF16), jax.ShapeDtypeStruct((S, F), BF16)],
        compiler_params=_cparams(("parallel", "parallel"), 56),
    )(n, w_in, w_in)


def _mm_nn(a, b, resid, scale, *, tm, tn, tk, name):
    S, K = a.shape
    N = b.shape[1]
    nk = K // tk

    def body(*refs):
        if resid is None:
            a_ref, b_ref, o_ref, acc_ref = refs
            r_ref = None
        else:
            a_ref, b_ref, r_ref, o_ref, acc_ref = refs
        k = pl.program_id(2)

        @pl.when(k == 0)
        def _():
            acc_ref[...] = jnp.zeros_like(acc_ref)

        acc_ref[...] += _dot_nn(a_ref[...], b_ref[...])

        @pl.when(k == nk - 1)
        def _():
            out = acc_ref[...] * scale
            if r_ref is not None:
                out = r_ref[...] + out
            o_ref[...] = out

    in_specs = [pl.BlockSpec((tm, tk), lambda i, j, k: (i, k)), pl.BlockSpec((tk, tn), lambda i, j, k: (k, j))]
    args = [a, b]
    if resid is not None:
        in_specs.append(pl.BlockSpec((tm, tn), lambda i, j, k: (i, j)))
        args.append(resid)
    return pl.pallas_call(
        body, name=name, grid=(S // tm, N // tn, nk),
        in_specs=in_specs,
        out_specs=pl.BlockSpec((tm, tn), lambda i, j, k: (i, j)),
        out_shape=jax.ShapeDtypeStruct((S, N), F32),
        scratch_shapes=[pltpu.VMEM((tm, tn), F32)],
        compiler_params=_cparams(("parallel", "parallel", "arbitrary"), 48),
    )(*args)


def _mm_nt(a, b, after, *, tm, tn, name):
    S, K = a.shape
    N = b.shape[0]

    def body(a_ref, b_ref, after_ref, o_ref):
        o_ref[...] = _dot_nt(a_ref[...], b_ref[...])

    return pl.pallas_call(
        body, name=name, grid=(N // tn, S // tm),
        in_specs=[pl.BlockSpec((tm, K), lambda j, i: (i, 0)), pl.BlockSpec((tn, K), lambda j, i: (j, 0)),
                  pl.BlockSpec(memory_space=pl.ANY)],
        out_specs=pl.BlockSpec((tm, tn), lambda j, i: (i, j)),
        out_shape=jax.ShapeDtypeStruct((S, N), F32),
        compiler_params=_cparams(("parallel", "parallel"), 48),
    )(a, b, after)


def _mm_tn(a, b, *, grid, a_spec, b_spec, out_spec, out_shape, scale, name, a_is_transposed=False):
    dot = _dot_nn if a_is_transposed else _dot_tn

    def body(a_ref, b_ref, o_ref):
        o_ref[...] = (scale * dot(a_ref[...], b_ref[...])).astype(o_ref.dtype)

    return pl.pallas_call(
        body, name=name, grid=grid, in_specs=[a_spec, b_spec], out_specs=out_spec, out_shape=out_shape,
        compiler_params=_cparams(("parallel",) * len(grid), 56),
    )(a, b)


def _ffn_bwd_act(dhb, w_out, gu, after, *, name):
    S, D = dhb.shape
    F = w_out.shape[0]
    tm, tn = 256, SHARD_TILE

    def body(dh_ref, w_ref, gu_ref, after_ref, dgu_ref):
        da = 0.5 * _dot_nt(dh_ref[...], w_ref[...])
        g = gu_ref[0].astype(F32)
        u = gu_ref[1].astype(F32)
        s = _sigmoid(g)
        dgu_ref[0] = (da * u * (s * (1.0 + g * (1.0 - s)))).astype(BF16)
        dgu_ref[1] = (da * (g * s)).astype(BF16)

    return pl.pallas_call(
        body, name=name, grid=(F // tn, S // tm),
        in_specs=[
            pl.BlockSpec((tm, D), lambda j, i: (i, 0)),
            pl.BlockSpec((tn, D), lambda j, i: (j, 0)),
            pl.BlockSpec((2, tm, tn), lambda j, i: (0, i, j)),
            pl.BlockSpec(memory_space=pl.ANY),
        ],
        out_specs=pl.BlockSpec((2, tm, tn), lambda j, i: (0, i, j)),
        out_shape=jax.ShapeDtypeStruct((2, S, F), BF16),
        compiler_params=_cparams(("parallel", "parallel"), 56),
    )(dhb, w_out, gu, after)


def _mm_nt_rmsbwd(dact, w, h_in, dh_out, g, *, tk, name):
    ng, S, fg = dact.shape
    ns, D, cs = w.shape
    assert ng * fg == ns * cs
    tm, rc = 512, 64
    kpg, kps = fg // tk, cs // tk
    nk = ng * kpg

    def body(a_ref, w_ref, h_ref, dho_ref, g_ref, dh_ref, dhb_ref, dg_ref, acc_ref):
        i = pl.program_id(0)
        k = pl.program_id(1)

        @pl.when(k == 0)
        def _():
            acc_ref[...] = jnp.zeros_like(acc_ref)

        acc_ref[...] += _dot_nt(a_ref[...], w_ref[...])

        @pl.when(jnp.logical_and(i == 0, k == 0))
        def _():
            dg_ref[...] = jnp.zeros_like(dg_ref)

        @pl.when(k == nk - 1)
        def _():
            gv = g_ref[...]

            def rows_step(c, dg):
                rows = pl.ds(pl.multiple_of(c * rc, rc), rc)
                dn = acc_ref[rows, :]
                xv = h_ref[rows, :]
                r = lax.rsqrt(jnp.mean(xv * xv, axis=-1, keepdims=True) + EPS)
                xh = xv * r
                dng = dn * gv
                dx = r * (dng - xh * jnp.mean(dng * xh, axis=-1, keepdims=True))
                out = dho_ref[rows, :] + dx
                dh_ref[rows, :] = out
                dhb_ref[rows, :] = out.astype(BF16)
                return dg + jnp.sum(dn * xh, axis=0, keepdims=True)

            dg_ref[...] += lax.fori_loop(0, tm // rc, rows_step, jnp.zeros((1, D), F32))

    return pl.pallas_call(
        body, name=name, grid=(S // tm, nk),
        in_specs=[
            pl.BlockSpec((None, tm, tk), lambda i, k: (lax.div(k, kpg), i, lax.rem(k, kpg))),
            pl.BlockSpec((None, D, tk), lambda i, k: (lax.div(k, kps), 0, lax.rem(k, kps))),
            pl.BlockSpec((tm, D), lambda i, k: (i, 0)),
            pl.BlockSpec((tm, D), lambda i, k: (i, 0)),
            pl.BlockSpec((1, D), lambda i, k: (0, 0)),
        ],
        out_specs=[
            pl.BlockSpec((tm, D), lambda i, k: (i, 0)),
            pl.BlockSpec((tm, D), lambda i, k: (i, 0)),
            pl.BlockSpec((1, D), lambda i, k: (0, 0)),
        ],
        out_shape=[jax.ShapeDtypeStruct((S, D), F32), jax.ShapeDtypeStruct((S, D), BF16),
                   jax.ShapeDtypeStruct((1, D), F32)],
        scratch_shapes=[pltpu.VMEM((tm, D), F32)],
        compiler_params=_cparams(("arbitrary", "arbitrary"), 56),
    )(dact, w, h_in, dh_out, g)


def _final_loss(h, g, target, *, name):
    S, D = h.shape
    tm = 256

    def body(h_ref, g_ref, t_ref, loss_ref, dh_ref, dhb_ref, dg_ref):
        i = pl.program_id(0)

        @pl.when(i == 0)
        def _():
            loss_ref[...] = jnp.zeros_like(loss_ref)
            dg_ref[...] = jnp.zeros_like(dg_ref)

        xv = h_ref[...]
        gv = g_ref[...]
        r = lax.rsqrt(jnp.mean(xv * xv, axis=-1, keepdims=True) + EPS)
        xh = xv * r
        e = xh * gv - t_ref[...]
        loss_ref[...] += 0.5 * jnp.sum(jnp.mean(e * e, axis=-1, keepdims=True))
        dy = e * (1.0 / D)
        dg_ref[...] += jnp.sum(dy * xh, axis=0, keepdims=True)
        dyg = dy * gv
        dx = r * (dyg - xh * jnp.mean(dyg * xh, axis=-1, keepdims=True))
        dh_ref[...] = dx
        dhb_ref[...] = dx.astype(BF16)

    return pl.pallas_call(
        body, name=name, grid=(S // tm,),
        in_specs=[pl.BlockSpec((tm, D), lambda i: (i, 0)), pl.BlockSpec((1, D), lambda i: (0, 0)),
                  pl.BlockSpec((tm, D), lambda i: (i, 0))],
        out_specs=[pl.BlockSpec((8, LANES), lambda i: (0, 0)), pl.BlockSpec((tm, D), lambda i: (i, 0)),
                   pl.BlockSpec((tm, D), lambda i: (i, 0)), pl.BlockSpec((1, D), lambda i: (0, 0))],
        out_shape=[jax.ShapeDtypeStruct((8, LANES), F32), jax.ShapeDtypeStruct((S, D), F32),
                   jax.ShapeDtypeStruct((S, D), BF16), jax.ShapeDtypeStruct((1, D), F32)],
        compiler_params=_cparams(("arbitrary",), 40),
    )(h, g, target)


POOL_HALO = 16
POOL_ROWS = 256


def _pool_window_mean_minus_token(ext, tok0, w):
    s = ext
    k = 1
    while k < w:
        s = s + pltpu.roll(s, k, 0)
        k *= 2
    win = s[POOL_HALO:, :]
    tok = tok0 + lax.broadcasted_iota(jnp.int32, (POOL_ROWS, 1), 0)
    cnt = jnp.minimum(tok + 1, w).astype(F32)
    return win / cnt - ext[POOL_HALO:, :], cnt


def _pool_fwd(u, w_pool, scale, *, name):
    S = u.shape[0]
    C = POOL_GROUP_DIM
    nsteps = S // POOL_ROWS

    def body(p_ref, w_ref, sc_ref, y_ref, xp_ref):
        xp_ref[0:POOL_HALO, :] = jnp.zeros((POOL_HALO, D_POOL), F32)
        xp_ref[POOL_HALO:, :] = p_ref[...]
        for gi, win in enumerate(POOL_WINDOWS):
            cols = slice(gi * C, (gi + 1) * C)

            def step(c, carry, cols=cols, win=win, gi=gi):
                r0 = pl.multiple_of(c * POOL_ROWS, POOL_ROWS)
                ext = xp_ref[pl.ds(r0, POOL_ROWS + POOL_HALO), cols]
                pooled, _ = _pool_window_mean_minus_token(ext, r0, win)
                y = _dot_nn(pooled.astype(BF16), w_ref[gi]) * sc_ref[:, cols]
                y_ref[pl.ds(r0, POOL_ROWS), cols] = y.astype(BF16)
                return carry

            lax.fori_loop(0, nsteps, step, 0)

    return pl.pallas_call(
        body, name=name, grid=(1,),
        in_specs=[pl.BlockSpec((S, D_POOL), lambda i: (0, 0)),
                  pl.BlockSpec((4, C, C), lambda i: (0, 0, 0)),
                  pl.BlockSpec((1, D_POOL), lambda i: (0, 0))],
        out_specs=pl.BlockSpec((S, D_POOL), lambda i: (0, 0)),
        out_shape=jax.ShapeDtypeStruct((S, D_POOL), BF16),
        scratch_shapes=[pltpu.VMEM((S + POOL_HALO, D_POOL), F32)],
        compiler_params=_cparams(("arbitrary",), 48),
    )(u, w_pool, scale)


def _pool_bwd(u, dcat, w_pool, scale, *, name):
    S = u.shape[0]
    C = POOL_GROUP_DIM
    nsteps = S // POOL_ROWS

    def body(p_ref, dy_ref, w_ref, sc_ref, dp_ref, dw_ref, dsc_ref, xp_ref, e_ref, neg_ref):
        xp_ref[0:POOL_HALO, :] = jnp.zeros((POOL_HALO, D_POOL), F32)
        xp_ref[POOL_HALO:, :] = p_ref[...]
        e_ref[S:, :] = jnp.zeros((POOL_HALO, C), F32)
        for gi, win in enumerate(POOL_WINDOWS):
            cols = slice(gi * C, (gi + 1) * C)

            def step_a(c, carry, cols=cols, win=win, gi=gi):
                dw, dsc = carry
                r0 = pl.multiple_of(c * POOL_ROWS, POOL_ROWS)
                ext = xp_ref[pl.ds(r0, POOL_ROWS + POOL_HALO), cols]
                pooled, cnt = _pool_window_mean_minus_token(ext, r0, win)
                pb = pooled.astype(BF16)
                wv = w_ref[gi]
                dy = dy_ref[pl.ds(r0, POOL_ROWS), cols]
                dsc = dsc + jnp.sum(dy * _dot_nn(pb, wv), axis=0, keepdims=True)
                dyp = (dy * sc_ref[:, cols]).astype(BF16)
                dw = dw + _dot_tn(pb, dyp)
                dpooled = _dot_nt(dyp, wv)
                e_ref[pl.ds(r0, POOL_ROWS), :] = dpooled / cnt
                neg_ref[pl.ds(r0, POOL_ROWS), :] = -dpooled
                return dw, dsc

            dw, dsc = lax.fori_loop(0, nsteps, step_a, (jnp.zeros((C, C), F32), jnp.zeros((1, C), F32)))
            dw_ref[gi] = dw
            dsc_ref[:, cols] = dsc

            def step_b(c, carry, cols=cols, win=win):
                r0 = pl.multiple_of(c * POOL_ROWS, POOL_ROWS)
                s = e_ref[pl.ds(r0, POOL_ROWS + POOL_HALO), :]
                n = POOL_ROWS + POOL_HALO
                k = 1
                while k < win:
                    s = s + pltpu.roll(s, n - k, 0)
                    k *= 2
                du = s[:POOL_ROWS, :] + neg_ref[pl.ds(r0, POOL_ROWS), :]
                dp_ref[pl.ds(r0, POOL_ROWS), cols] = du.astype(BF16)
                return carry

            lax.fori_loop(0, nsteps, step_b, 0)

    return pl.pallas_call(
        body, name=name, grid=(1,),
        in_specs=[pl.BlockSpec((S, D_POOL), lambda i: (0, 0)),
                  pl.BlockSpec((S, D_POOL), lambda i: (0, 0)),
                  pl.BlockSpec((4, C, C), lambda i: (0, 0, 0)),
                  pl.BlockSpec((1, D_POOL), lambda i: (0, 0))],
        out_specs=[pl.BlockSpec((S, D_POOL), lambda i: (0, 0)),
                   pl.BlockSpec((4, C, C), lambda i: (0, 0, 0)),
                   pl.BlockSpec((1, D_POOL), lambda i: (0, 0))],
        out_shape=[jax.ShapeDtypeStruct((S, D_POOL), BF16), jax.ShapeDtypeStruct((4, C, C), F32),
                   jax.ShapeDtypeStruct((1, D_POOL), F32)],
        scratch_shapes=[pltpu.VMEM((S + POOL_HALO, D_POOL), F32), pltpu.VMEM((S + POOL_HALO, C), F32),
                        pltpu.VMEM((S, C), F32)],
        compiler_params=_cparams(("arbitrary",), 56),
    )(u, dcat, w_pool, scale)


GLA_ROWS = 128
U_Q_BLK, U_K_BLK = 2, 3
U_V_BLK, U_G_BLK = 2, 3
U_R_BLK = 32


def _prefix_sum_rows(x):
    n = x.shape[0]
    row = lax.broadcasted_iota(jnp.int32, x.shape, 0)
    k = 1
    while k < n:
        x = x + jnp.where(row >= k, pltpu.roll(x, k, 0), 0.0)
        k *= 2
    return x


def _suffix_sum_rows(x):
    n = x.shape[0]
    row = lax.broadcasted_iota(jnp.int32, x.shape, 0)
    k = 1
    while k < n:
        x = x + jnp.where(row < n - k, pltpu.roll(x, n - k, 0), 0.0)
        k *= 2
    return x


def _log_sigmoid(z):
    return jnp.minimum(z, 0.0) - jnp.log(1.0 + jnp.exp(-jnp.abs(z)))


def _gla_chunk_terms(la_c, q_c, k_c):
    bc = _prefix_sum_rows(la_c)
    bl = jnp.sum(la_c, axis=0, keepdims=True)
    eb = jnp.exp(bc)
    enb = jnp.exp(-bc)
    etail = jnp.exp(bl - bc)
    qd = q_c * (GLA_DK ** -0.5) * eb
    ki = k_c * enb
    kt = k_c * etail
    d = jnp.exp(bl)
    return eb, enb, etail, qd, ki, kt, d


def _gla_fwd(u, y_pool, w_alpha, b_alpha, gnorm, *, name):
    S = u.shape[0]
    RB = GLA_ROWS
    ncc = RB // CHUNK
    H, DK, DV = GLA_HEADS, GLA_DK, GLA_DV

    def body(q_ref, k_ref, v_ref, go_ref, r_ref, yp_ref, wa_ref, ba_ref, gn_ref, cat_ref, o_ref, st_ref, state):
        i = pl.program_id(0)

        @pl.when(i == 0)
        def _():
            state[...] = jnp.zeros_like(state)

        cat_ref[:, :D_POOL] = yp_ref[...]
        y_ref = cat_ref.at[:, D_POOL:]

        z = _dot_nn(r_ref[...].astype(BF16), wa_ref[...]) + ba_ref[...]
        la = _log_sigmoid(z) / GATE_LOGIT_NORMALIZER
        ri = lax.broadcasted_iota(jnp.int32, (CHUNK, CHUNK), 0)
        ci = lax.broadcasted_iota(jnp.int32, (CHUNK, CHUNK), 1)
        tri = ri >= ci
        gn = gn_ref[...]
        for cc in range(ncc):
            rs = slice(cc * CHUNK, (cc + 1) * CHUNK)
            for h in range(H):
                ks = slice(h * DK, (h + 1) * DK)
                vs = slice(h * DV, (h + 1) * DV)
                _, _, _, qd, ki, kt, d = _gla_chunk_terms(la[rs, ks], q_ref[rs, ks], k_ref[rs, ks])
                qdb = qd.astype(BF16)
                vb = v_ref[rs, vs].astype(BF16)
                p = jnp.where(tri, _dot_nt(qdb, ki.astype(BF16)), 0.0)
                st = state[h]
                st_ref[cc, h] = st
                o = _dot_nn(p.astype(BF16), vb) + _dot_nt(qdb, st.astype(BF16))
                state[h] = st * d + _dot_tn(vb, kt.astype(BF16))
                o_ref[rs, vs] = o
                rinv = lax.rsqrt(jnp.mean(o * o, axis=-1, keepdims=True) + EPS)
                go = go_ref[rs, vs]
                y_ref[rs, vs] = (o * rinv * gn * (go * _sigmoid(go))).astype(BF16)

    nblk = S // RB
    return pl.pallas_call(
        body, name=name, grid=(nblk,),
        in_specs=[
            pl.BlockSpec((RB, GLA_DK_TOTAL), lambda i: (i, U_Q_BLK)),
            pl.BlockSpec((RB, GLA_DK_TOTAL), lambda i: (i, U_K_BLK)),
            pl.BlockSpec((RB, D_GLA), lambda i: (i, U_V_BLK)),
            pl.BlockSpec((RB, D_GLA), lambda i: (i, U_G_BLK)),
            pl.BlockSpec((RB, LANES), lambda i: (i, U_R_BLK)),
            pl.BlockSpec((RB, D_POOL), lambda i: (i, 0)),
            pl.BlockSpec((LANES, GLA_DK_TOTAL), lambda i: (0, 0)),
            pl.BlockSpec((1, GLA_DK_TOTAL), lambda i: (0, 0)),
            pl.BlockSpec((1, DV), lambda i: (0, 0)),
        ],
        out_specs=[
            pl.BlockSpec((RB, D_POOL + D_GLA), lambda i: (i, 0)),
            pl.BlockSpec((RB, D_GLA), lambda i: (i, 0)),
            pl.BlockSpec((ncc, H, DV, DK), lambda i: (i, 0, 0, 0)),
        ],
        out_shape=[jax.ShapeDtypeStruct((S, D_POOL + D_GLA), BF16), jax.ShapeDtypeStruct((S, D_GLA), F32),
                   jax.ShapeDtypeStruct((S // CHUNK, H, DV, DK), F32)],
        scratch_shapes=[pltpu.VMEM((H, DV, DK), F32)],
        compiler_params=_cparams(("arbitrary",), 32),
    )(u, u, u, u, u, y_pool, w_alpha, b_alpha, gnorm)


def _gla_bwd(u, o, states, dcat, dpool, w_alpha, b_alpha, gnorm, *, name):
    S = u.shape[0]
    RB = GLA_ROWS
    ncc = RB // CHUNK
    H, DK, DV = GLA_HEADS, GLA_DK, GLA_DV
    nblk = S // RB
    o_q, o_k = D_POOL, D_POOL + GLA_DK_TOTAL
    o_v, o_g, o_r = o_k + GLA_DK_TOTAL, o_k + GLA_DK_TOTAL + D_GLA, o_k + GLA_DK_TOTAL + 2 * D_GLA

    def body(q_ref, k_ref, v_ref, go_ref, r_ref, o_ref, st_ref, dy_ref, dpool_ref, wa_ref, ba_ref, gn_ref,
             du_ref, dwa_ref, dba_ref, dgn_ref, dstate, dz_ref):
        i = pl.program_id(0)

        @pl.when(i == 0)
        def _():
            dstate[...] = jnp.zeros_like(dstate)
            dwa_ref[...] = jnp.zeros_like(dwa_ref)
            dba_ref[...] = jnp.zeros_like(dba_ref)
            dgn_ref[...] = jnp.zeros_like(dgn_ref)

        du_ref[:, :o_q] = dpool_ref[...]
        dq_ref, dk_ref = du_ref.at[:, o_q:o_k], du_ref.at[:, o_k:o_v]
        dv_ref, dgo_ref, dr_ref = du_ref.at[:, o_v:o_g], du_ref.at[:, o_g:o_r], du_ref.at[:, o_r:]

        rb = r_ref[...].astype(BF16)
        wa = wa_ref[...]
        z = _dot_nn(rb, wa) + ba_ref[...]
        la = _log_sigmoid(z) / GATE_LOGIT_NORMALIZER
        ri = lax.broadcasted_iota(jnp.int32, (CHUNK, CHUNK), 0)
        ci = lax.broadcasted_iota(jnp.int32, (CHUNK, CHUNK), 1)
        tri = ri >= ci
        last_row = lax.broadcasted_iota(jnp.int32, (CHUNK, DK), 0) == CHUNK - 1
        gn = gn_ref[...]
        dgn = jnp.zeros((1, DV), F32)
        for cc in reversed(range(ncc)):
            rs = slice(cc * CHUNK, (cc + 1) * CHUNK)
            for h in range(H):
                ks = slice(h * DK, (h + 1) * DK)
                vs = slice(h * DV, (h + 1) * DV)
                eb, enb, etail, qd, ki, kt, d = _gla_chunk_terms(la[rs, ks], q_ref[rs, ks], k_ref[rs, ks])
                qdb, kib, ktb = qd.astype(BF16), ki.astype(BF16), kt.astype(BF16)
                vb = v_ref[rs, vs].astype(BF16)
                p = jnp.where(tri, _dot_nt(qdb, kib), 0.0)
                ov = o_ref[rs, vs]
                go = go_ref[rs, vs]
                dy = dy_ref[rs, vs]
                rinv = lax.rsqrt(jnp.mean(ov * ov, axis=-1, keepdims=True) + EPS)
                oh = ov * rinv
                sg = _sigmoid(go)
                dgo_ref[rs, vs] = (dy * (oh * gn) * (sg * (1.0 + go * (1.0 - sg)))).astype(BF16)
                don = dy * (go * sg)
                dgn = dgn + jnp.sum(don * oh, axis=0, keepdims=True)
                doh = don * gn
                do = rinv * (doh - oh * jnp.mean(doh * oh, axis=-1, keepdims=True))
                dob = do.astype(BF16)
                st = st_ref[cc, h]
                dst = dstate[h]
                stb, dstb = st.astype(BF16), dst.astype(BF16)
                dp = jnp.where(tri, _dot_nt(dob, vb), 0.0).astype(BF16)
                dv_ref[rs, vs] = (_dot_tn(p.astype(BF16), dob) + _dot_nt(ktb, dstb)).astype(BF16)
                dqd = _dot_nn(dp, kib) + _dot_nn(dob, stb)
                dki = _dot_tn(dp, qdb)
                dkt = _dot_nn(vb, dstb)
                dd = jnp.sum(dst * st, axis=0, keepdims=True)
                dstate[h] = dst * d + _dot_tn(dob, qdb)
                dq_ref[rs, ks] = (dqd * eb * (DK ** -0.5)).astype(BF16)
                dk_ref[rs, ks] = (dki * enb + dkt * etail).astype(BF16)
                dbl = jnp.sum(dkt * kt, axis=0, keepdims=True) + dd * d
                dbc = dqd * qd - dki * ki - dkt * kt
                dbc = dbc + jnp.where(last_row, dbl, 0.0)
                dla = _suffix_sum_rows(dbc)
                dz_ref[rs, ks] = dla * (1.0 / GATE_LOGIT_NORMALIZER) * (1.0 - _sigmoid(z[rs, ks]))
        dz = dz_ref[...]
        dzb = dz.astype(BF16)
        dr_ref[...] = _dot_nt(dzb, wa).astype(BF16)
        dwa_ref[...] += _dot_tn(rb, dzb)
        dba_ref[...] += jnp.sum(dz, axis=0, keepdims=True)
        dgn_ref[...] += dgn

    def rev(blk):
        return lambda i: (nblk - 1 - i, blk)

    return pl.pallas_call(
        body, name=name, grid=(nblk,),
        in_specs=[
            pl.BlockSpec((RB, GLA_DK_TOTAL), rev(U_Q_BLK)),
            pl.BlockSpec((RB, GLA_DK_TOTAL), rev(U_K_BLK)),
            pl.BlockSpec((RB, D_GLA), rev(U_V_BLK)),
            pl.BlockSpec((RB, D_GLA), rev(U_G_BLK)),
            pl.BlockSpec((RB, LANES), rev(U_R_BLK)),
            pl.BlockSpec((RB, D_GLA), rev(0)),
            pl.BlockSpec((ncc, H, DV, DK), lambda i: (nblk - 1 - i, 0, 0, 0)),
            pl.BlockSpec((RB, D_GLA), rev(1)),
            pl.BlockSpec((RB, D_POOL), rev(0)),
            pl.BlockSpec((LANES, GLA_DK_TOTAL), lambda i: (0, 0)),
            pl.BlockSpec((1, GLA_DK_TOTAL), lambda i: (0, 0)),
            pl.BlockSpec((1, DV), lambda i: (0, 0)),
        ],
        out_specs=[
            pl.BlockSpec((RB, D_IN_PAD), rev(0)),
            pl.BlockSpec((LANES, GLA_DK_TOTAL), lambda i: (0, 0)),
            pl.BlockSpec((1, GLA_DK_TOTAL), lambda i: (0, 0)),
            pl.BlockSpec((1, DV), lambda i: (0, 0)),
        ],
        out_shape=[
            jax.ShapeDtypeStruct((S, D_IN_PAD), BF16),
            jax.ShapeDtypeStruct((LANES, GLA_DK_TOTAL), F32), jax.ShapeDtypeStruct((1, GLA_DK_TOTAL), F32),
            jax.ShapeDtypeStruct((1, DV), F32),
        ],
        scratch_shapes=[pltpu.VMEM((H, DV, DK), F32), pltpu.VMEM((RB, GLA_DK_TOTAL), F32)],
        compiler_params=_cparams(("arbitrary",), 32),
    )(u, u, u, u, u, o, states, dcat, dpool, w_alpha, b_alpha, gnorm)


def _row_tile(rows, cols, itemsize, budget=2 * 1024 * 1024):
    if rows * cols * itemsize <= budget or rows % 16:
        return rows
    best = 16
    for t in range(16, rows + 1, 16):
        if rows % t == 0 and t * cols * itemsize <= budget:
            best = t
    return best


def _adamw(w, g, m, v, *, name):
    R, C = w.shape
    tr = _row_tile(R, C, 4, budget=1024 * 1024)

    def body(w_ref, g_ref, m_ref, v_ref, go_ref, d_ref, nm_ref, nv_ref):
        gv = g_ref[...]
        go_ref[...] = gv
        mn = ADAM_B1 * m_ref[...] + (1.0 - ADAM_B1) * gv
        vn = ADAM_B2 * v_ref[...] + (1.0 - ADAM_B2) * jnp.square(gv)
        m_hat = mn / (1.0 - ADAM_B1 ** ADAM_STEP)
        v_hat = vn / (1.0 - ADAM_B2 ** ADAM_STEP)
        d_ref[...] = -ADAM_LR * (m_hat / (jnp.sqrt(v_hat) + ADAM_EPS) + ADAM_WD * w_ref[...])
        nm_ref[...] = mn
        nv_ref[...] = vn

    spec = pl.BlockSpec((tr, C), lambda i: (i, 0))
    shp = jax.ShapeDtypeStruct((R, C), F32)
    return pl.pallas_call(
        body, name=name, grid=(R // tr,), in_specs=[spec] * 4, out_specs=[spec] * 4, out_shape=[shp] * 4,
        compiler_params=_cparams(("parallel",), 32),
    )(w, g, m, v)


def _pair_sum(g4, recv, c_idx, *, name):
    ns, _, R2, C = g4.shape
    tr = _row_tile(R2, C, 2)

    def body(c_ref, g_ref, r_ref, o_ref):
        o_ref[...] = (g_ref[...].astype(F32) + r_ref[...].astype(F32)).astype(BF16)

    return pl.pallas_call(
        body, name=name,
        grid_spec=pltpu.PrefetchScalarGridSpec(
            num_scalar_prefetch=1, grid=(ns, R2 // tr),
            in_specs=[pl.BlockSpec((None, None, tr, C), lambda s, i, c: (s, c[0], i, 0)),
                      pl.BlockSpec((None, tr, C), lambda s, i, c: (s, i, 0))],
            out_specs=pl.BlockSpec((None, tr, C), lambda s, i, c: (s, i, 0)),
        ),
        out_shape=jax.ShapeDtypeStruct((ns, R2, C), BF16),
        compiler_params=_cparams(("parallel", "parallel"), 32),
    )(c_idx, g4, recv)


def _chip_sum(part, recv, sc_idx, *, name):
    _, R2, C = part.shape
    tr = _row_tile(R2, C, 4)
    nblk = R2 // tr

    def body(s_ref, p_ref, r_ref, o_ref):
        acc = p_ref[...].astype(F32)
        for j in range(N_CHIPS - 1):
            acc = acc + r_ref[j].astype(F32)
        o_ref[...] = acc

    return pl.pallas_call(
        body, name=name,
        grid_spec=pltpu.PrefetchScalarGridSpec(
            num_scalar_prefetch=1, grid=(nblk,),
            in_specs=[pl.BlockSpec((None, tr, C), lambda i, s: (s[0], i, 0)),
                      pl.BlockSpec((N_CHIPS - 1, tr, C), lambda i, s: (0, i, 0))],
            out_specs=pl.BlockSpec((tr, C), lambda i, s: (s[1] * nblk + i, 0)),
        ),
        out_shape=jax.ShapeDtypeStruct((2 * R2, C), F32),
        compiler_params=_cparams(("parallel",), 32),
    )(sc_idx, part, recv)


def _cast_into_slot(w, sc_idx, dtype, after, *, name):
    R, C = w.shape
    tr = _row_tile(R, C, 4)

    def body(s_ref, w_ref, after_ref, o_ref):
        o_ref[...] = w_ref[...].astype(dtype)

    return pl.pallas_call(
        body, name=name,
        grid_spec=pltpu.PrefetchScalarGridSpec(
            num_scalar_prefetch=1, grid=(R // tr,),
            in_specs=[pl.BlockSpec((tr, C), lambda i, s: (i, 0)), pl.BlockSpec(memory_space=pl.ANY)],
            out_specs=pl.BlockSpec((None, tr, C), lambda i, s: (s[0], i, 0)),
        ),
        out_shape=jax.ShapeDtypeStruct((N_CHIPS, R, C), dtype),
        compiler_params=_cparams(("parallel",), 32),
    )(sc_idx, w, after)


def _slab_sum(slabs, *, name):
    n, M, C = slabs.shape

    def body(x_ref, o_ref):
        acc = x_ref[0]
        for d in range(1, n):
            acc = acc + x_ref[d]
        o_ref[...] = acc

    return pl.pallas_call(
        body, name=name, out_shape=jax.ShapeDtypeStruct((M, C), F32),
    )(slabs)


def _mesh_position():
    x, y, c = lax.axis_index("x"), lax.axis_index("y"), lax.axis_index("c")
    other_chips = [(1 - x, y), (x, 1 - y), (1 - x, 1 - y)]
    return x, y, c, other_chips


ANY = pl.BlockSpec(memory_space=pl.ANY)


HBM = pl.BlockSpec(memory_space=pltpu.HBM)
SEM = pl.BlockSpec(memory_space=pltpu.SEMAPHORE)
SPLIT_COPY = pltpu.CompilerParams(has_side_effects=pltpu.SideEffectType.DATAFLOW_SIDE_EFFECTING)
TOKEN = jax.ShapeDtypeStruct((8, LANES), F32)


def _in_hbm(a):
    return pltpu.with_memory_space_constraint(a, pltpu.HBM)


def _half_rows(ref, slot, half):
    hr = ref.shape[1] // 2
    return ref.at[slot, pl.ds(half * hr, hr), :]


def _gather_ici_start(groups, *, name):
    flat = [b for g in groups for b in g]
    K, G = len(flat), len(groups)

    def body(*refs):
        ins, sems, token = refs[:K], refs[K:K + 2 * G], refs[-1]
        x, y, c, chips = _mesh_position()
        s = 2 * x + y
        k = 0
        for gi, g in enumerate(groups):
            for n in range(len(g)):
                own = _half_rows(ins[k], s, c)
                for j, chip in enumerate(chips):
                    pltpu.make_async_remote_copy(
                        src_ref=own, dst_ref=own, send_sem=sems[2 * gi].at[n * (N_CHIPS - 1) + j],
                        recv_sem=sems[2 * gi + 1].at[n * (N_CHIPS - 1) + j],
                        device_id=(*chip, c), device_id_type=MESH).start()
                k += 1
        token[...] = jnp.zeros_like(token)

    sem_shapes = []
    for g in groups:
        sem_shapes += [pltpu.SemaphoreType.DMA((len(g) * (N_CHIPS - 1),))] * 2
    out = pl.pallas_call(
        body, name=name,
        in_specs=[HBM] * K,
        out_specs=[SEM] * (2 * G) + [HBM] * K + [pl.BlockSpec(memory_space=pltpu.VMEM)],
        out_shape=sem_shapes + [pltpu.HBM(b.shape, b.dtype) for b in flat] + [TOKEN],
        input_output_aliases={k: 2 * G + k for k in range(K)},
        compiler_params=SPLIT_COPY,
    )(*[_in_hbm(b) for b in flat])
    handles, k = [], 2 * G
    for gi, g in enumerate(groups):
        handles.append((out[2 * gi], out[2 * gi + 1], list(out[k:k + len(g)])))
        k += len(g)
    return handles, out[-1]


def _gather_ici_wait(handle, after, *, name):
    send, recv, bufs = handle
    n = len(bufs)

    def body(*refs):
        ins, send_ref, recv_ref = refs[:n], refs[n], refs[n + 1]
        x, y, c, chips = _mesh_position()
        s = 2 * x + y
        for k in range(n):
            own = _half_rows(ins[k], s, c)
            for j, chip in enumerate(chips):
                cp = pltpu.make_async_remote_copy(
                    src_ref=own, dst_ref=_half_rows(ins[k], 2 * chip[0] + chip[1], c),
                    send_sem=send_ref.at[k * (N_CHIPS - 1) + j], recv_sem=recv_ref.at[k * (N_CHIPS - 1) + j],
                    device_id=(*chip, c), device_id_type=MESH)
                cp.wait_send()
                cp.wait_recv()

    return pl.pallas_call(
        body, name=name,
        in_specs=[HBM] * n + [SEM, SEM, ANY], out_specs=[HBM] * n,
        out_shape=[pltpu.HBM(b.shape, b.dtype) for b in bufs],
        input_output_aliases={k: k for k in range(n)},
        compiler_params=SPLIT_COPY,
    )(*bufs, send, recv, after)


def _forward_halves(bufs, *, name):
    K = len(bufs)
    per = N_CHIPS - 1

    def body(*refs):
        outs = refs[K:2 * K]
        send_sems, recv_sems = refs[2 * K:]
        x, y, c, chips = _mesh_position()
        copies = []
        for k in range(K):
            for j, chip in enumerate(chips):
                got = _half_rows(outs[k], 2 * chip[0] + chip[1], c)
                cp = pltpu.make_async_remote_copy(
                    src_ref=got, dst_ref=got, send_sem=send_sems.at[k * per + j], recv_sem=recv_sems.at[k * per + j],
                    device_id=(x, y, 1 - c), device_id_type=MESH)
                cp.start()
                copies.append(cp)
        for cp in copies:
            cp.wait()

    return pl.pallas_call(
        body, name=name,
        in_specs=[ANY] * K, out_specs=[ANY] * K,
        out_shape=[jax.ShapeDtypeStruct(a.shape, a.dtype) for a in bufs],
        input_output_aliases={k: k for k in range(K)},
        scratch_shapes=[pltpu.SemaphoreType.DMA((K * per,)), pltpu.SemaphoreType.DMA((K * per,))],
    )(*bufs)


def _chip_exchange_copies(srcs, lands, send_sems, recv_sems):
    x, y, c, chips = _mesh_position()
    per = N_CHIPS - 1
    return [pltpu.make_async_remote_copy(
        src_ref=srcs[k].at[2 * chip[0] + chip[1]], dst_ref=lands[k].at[j],
        send_sem=send_sems.at[k * per + j], recv_sem=recv_sems.at[k * per + j],
        device_id=(*chip, c), device_id_type=MESH) for k in range(len(srcs)) for j, chip in enumerate(chips)]


def _sibling_swap_copies(srcs, lands, send_sems, recv_sems):
    x, y, c, _ = _mesh_position()
    return [pltpu.make_async_remote_copy(
        src_ref=srcs[k].at[pl.ds(0, srcs[k].shape[0]), 1 - c], dst_ref=lands[k],
        send_sem=send_sems.at[k], recv_sem=recv_sems.at[k],
        device_id=(x, y, 1 - c), device_id_type=MESH) for k in range(len(srcs))]


def _split_copy_start(srcs, land_shapes, n_sems, copies, *, name):
    K = len(srcs)

    def body(*refs):
        for cp in copies(refs[:K], refs[K:2 * K], refs[2 * K], refs[2 * K + 1]):
            cp.start()
        refs[-1][...] = jnp.zeros_like(refs[-1])

    out = pl.pallas_call(
        body, name=name,
        in_specs=[HBM] * (2 * K),
        out_specs=[SEM, SEM] + [HBM] * (2 * K) + [pl.BlockSpec(memory_space=pltpu.VMEM)],
        out_shape=[pltpu.SemaphoreType.DMA((n_sems,))] * 2
        + [pltpu.HBM(a.shape, a.dtype) for a in srcs]
        + [pltpu.HBM(s, a.dtype) for s, a in zip(land_shapes, srcs)] + [TOKEN],
        input_output_aliases={k: 2 + k for k in range(2 * K)},
        compiler_params=SPLIT_COPY,
    )(*[_in_hbm(a) for a in srcs], *[_in_hbm(lax.empty(s, a.dtype)) for s, a in zip(land_shapes, srcs)])
    return (out[0], out[1], list(out[2:2 + K]), list(out[2 + K:2 + 2 * K])), out[-1]


def _split_copy_wait(handle, copies, after, *, name):
    send, recv, srcs, lands = handle
    K = len(srcs)

    def body(*refs):
        for cp in copies(refs[:K], refs[K:2 * K], refs[2 * K], refs[2 * K + 1]):
            cp.wait_send()
            cp.wait_recv()

    out = pl.pallas_call(
        body, name=name,
        in_specs=[HBM] * (2 * K) + [SEM, SEM, ANY], out_specs=[HBM] * (2 * K),
        out_shape=[pltpu.HBM(a.shape, a.dtype) for a in srcs] + [pltpu.HBM(a.shape, a.dtype) for a in lands],
        input_output_aliases={k: k for k in range(2 * K)},
        compiler_params=SPLIT_COPY,
    )(*srcs, *lands, send, recv, after)
    return list(out[:K]), list(out[K:])


def _join_copies(bufs, send_sems, recv_sems):
    x, y, c, _ = _mesh_position()
    copies = []
    for k, buf in enumerate(bufs):
        r2 = buf.shape[0] // 2
        mine = buf.at[pl.ds(c * r2, r2), :]
        copies.append(pltpu.make_async_remote_copy(
            src_ref=mine, dst_ref=mine, send_sem=send_sems.at[k], recv_sem=recv_sems.at[k],
            device_id=(x, y, 1 - c), device_id_type=MESH))
    return copies


def _join_start(bufs, *, name):
    K = len(bufs)

    def body(*refs):
        for cp in _join_copies(refs[:K], refs[K], refs[K + 1]):
            cp.start()
        refs[-1][...] = jnp.zeros_like(refs[-1])

    out = pl.pallas_call(
        body, name=name,
        in_specs=[HBM] * K,
        out_specs=[SEM, SEM] + [HBM] * K + [pl.BlockSpec(memory_space=pltpu.VMEM)],
        out_shape=[pltpu.SemaphoreType.DMA((K,))] * 2 + [pltpu.HBM(a.shape, a.dtype) for a in bufs] + [TOKEN],
        input_output_aliases={k: 2 + k for k in range(K)},
        compiler_params=SPLIT_COPY,
    )(*[_in_hbm(a) for a in bufs])
    return (out[0], out[1], list(out[2:2 + K])), out[-1]


def _join_wait(handle, after, *, name):
    send, recv, bufs = handle
    K = len(bufs)

    def body(*refs):
        for cp in _join_copies(refs[:K], refs[K], refs[K + 1]):
            cp.wait_send()
            cp.wait_recv()

    return pl.pallas_call(
        body, name=name,
        in_specs=[HBM] * K + [SEM, SEM, ANY], out_specs=[HBM] * K,
        out_shape=[pltpu.HBM(a.shape, a.dtype) for a in bufs],
        input_output_aliases={k: k for k in range(K)},
        compiler_params=SPLIT_COPY,
    )(*bufs, send, recv, after)


def _all_gather_slab(slab):
    m_per, n = slab.shape

    def body(x_ref, out_ref, send_sems, recv_sems, local_sem):
        x, y, c, chips = _mesh_position()
        me, sibling = (x, y, c), (x, y, 1 - c)

        def rows(px, py, pc):
            return out_ref.at[pl.ds((4 * px + 2 * py + pc) * m_per, m_per), :]

        def copy(k, block, to, src=None):
            return pltpu.make_async_remote_copy(
                src_ref=rows(*block) if src is None else src, dst_ref=rows(*block),
                send_sem=send_sems.at[k], recv_sem=recv_sems.at[k], device_id=to, device_id_type=MESH)

        mine = pltpu.make_async_copy(x_ref, rows(*me), local_sem)
        mine.start()
        first = [copy(0, me, sibling, src=x_ref)]
        first += [copy(1 + j, me, (*chip, c), src=x_ref) for j, chip in enumerate(chips)]
        for cp in first:
            cp.start()
        passed = [copy(4 + j, (*chip, c), sibling) for j, chip in enumerate(chips)]
        for j, chip in enumerate(chips):
            copy(1 + j, (*chip, c), me).wait_recv()
            passed[j].start()
        copy(0, sibling, me).wait_recv()
        for j, chip in enumerate(chips):
            copy(4 + j, (*chip, 1 - c), me).wait_recv()
        for cp in first + passed:
            cp.wait_send()
        mine.wait()

    return pl.pallas_call(
        body, name="gather_small_grads",
        out_shape=jax.ShapeDtypeStruct((N_DEV * m_per, n), slab.dtype),
        in_specs=[pl.BlockSpec(memory_space=pltpu.VMEM)],
        out_specs=pl.BlockSpec(memory_space=pltpu.VMEM),
        scratch_shapes=[pltpu.SemaphoreType.DMA((7,)), pltpu.SemaphoreType.DMA((7,)), pltpu.SemaphoreType.DMA],
    )(slab)


def _ffn_backward_weights(dhb, saved, w_in, w_out, after, tag, on_dw_out=None):
    n, gu, a = saved
    S = n.shape[0]
    ns, D, cs = w_in.shape
    F = w_out.shape[0]
    to = 512
    dw_out = _mm_tn(
        a, dhb, grid=(F // to,),
        a_spec=pl.BlockSpec((S, to), lambda j: (0, j)), b_spec=pl.BlockSpec((S, D), lambda j: (0, 0)),
        out_spec=pl.BlockSpec((to, D), lambda j: (j, 0)), out_shape=jax.ShapeDtypeStruct((F, D), BF16),
        scale=0.5, name=f"{tag}_dw_out").reshape(N_CHIPS, F // N_CHIPS, D)
    if on_dw_out is not None:
        after = on_dw_out(dw_out)
    dgu = _ffn_bwd_act(dhb, w_out, gu, after, name=f"{tag}_bwd_act")
    ti = MXU_TILE
    per_g, per_s = F // ti, cs // ti
    dw_in = _mm_tn(
        n.T, dgu, grid=(2 * F // ti,), a_is_transposed=True,
        a_spec=pl.BlockSpec((D, S), lambda j: (0, 0)),
        b_spec=pl.BlockSpec((None, S, ti), lambda j: (lax.div(j, per_g), 0, lax.rem(j, per_g))),
        out_spec=pl.BlockSpec((None, D, ti), lambda j: (lax.div(j, per_s), 0, lax.rem(j, per_s))),
        out_shape=jax.ShapeDtypeStruct((ns, D, cs), BF16), scale=1.0, name=f"{tag}_dw_in")
    return dgu, dw_in, dw_out


def kernel(x, ffn1_norm, ffn1_w_in, ffn1_w_out, mix_norm, w_in_mix, w_pool, pool_scale, w_alpha, b_alpha, gla_norm, w_out_mix, ffn2_norm, ffn2_w_in, ffn2_w_out, final_norm, loss_target, m_ffn1_norm, m_ffn1_w_in, m_ffn1_w_out, m_mix_norm, m_w_in_mix, m_w_pool, m_pool_scale, m_w_alpha, m_b_alpha, m_gla_norm, m_w_out_mix, m_ffn2_norm, m_ffn2_w_in, m_ffn2_w_out, m_final_norm, v_ffn1_norm, v_ffn1_w_in, v_ffn1_w_out, v_mix_norm, v_w_in_mix, v_w_pool, v_pool_scale, v_w_alpha, v_b_alpha, v_gla_norm, v_w_out_mix, v_ffn2_norm, v_ffn2_w_in, v_ffn2_w_out, v_final_norm):
    names = ["ffn1_norm", "ffn1_w_in", "ffn1_w_out", "mix_norm", "w_in_mix", "w_pool", "pool_scale", "w_alpha",
             "b_alpha", "gla_norm", "w_out_mix", "ffn2_norm", "ffn2_w_in", "ffn2_w_out", "final_norm"]
    weights = dict(zip(names, [ffn1_norm, ffn1_w_in, ffn1_w_out, mix_norm, w_in_mix, w_pool, pool_scale, w_alpha,
                               b_alpha, gla_norm, w_out_mix, ffn2_norm, ffn2_w_in, ffn2_w_out, final_norm]))
    moms = dict(zip(names, [m_ffn1_norm, m_ffn1_w_in, m_ffn1_w_out, m_mix_norm, m_w_in_mix, m_w_pool, m_pool_scale,
                            m_w_alpha, m_b_alpha, m_gla_norm, m_w_out_mix, m_ffn2_norm, m_ffn2_w_in, m_ffn2_w_out,
                            m_final_norm]))
    vels = dict(zip(names, [v_ffn1_norm, v_ffn1_w_in, v_ffn1_w_out, v_mix_norm, v_w_in_mix, v_w_pool, v_pool_scale,
                            v_w_alpha, v_b_alpha, v_gla_norm, v_w_out_mix, v_ffn2_norm, v_ffn2_w_in, v_ffn2_w_out,
                            v_final_norm]))
    xi, yi, ci = lax.axis_index("x"), lax.axis_index("y"), lax.axis_index("c")
    chip = 2 * xi + yi
    c_idx = jnp.reshape(ci, (1,)).astype(jnp.int32)
    sc_idx = jnp.stack([chip, ci]).astype(jnp.int32)

    def flat2d(a):
        return a.reshape(-1, a.shape[-1])

    ex = _Exchanges(sc_idx, c_idx)

    def cast(n, after):
        return _cast_into_slot(flat2d(weights[n]), sc_idx, F32 if n == "w_alpha" else BF16, after, name=f"cast_{n}")

    first, rest = _Exchanges.GATHER_GROUPS[:2], _Exchanges.GATHER_GROUPS[2:]
    tok = ex.start_gather({n: cast(n, sc_idx) for g in first for n in g}, first, name="gather_ici_start_ffn1")
    tok2 = ex.start_gather({n: cast(n, tok) for g in rest for n in g}, rest, name="gather_ici_start_rest")
    small_params = dict(g1=ffn1_norm, gm=mix_norm, g2=ffn2_norm, gf=final_norm.reshape(1, D_MODEL),
                        pool_scale=pool_scale, b_alpha=b_alpha, gla_norm=gla_norm)
    loss_blk, dx, small = _forward_backward(x[0], loss_target[0], ex, tok[0, 0] + tok2[0, 0], small_params)

    outs = {}

    def update(n, g):
        w = weights[n]
        w2 = flat2d(w) if w.ndim > 1 else w.reshape(1, -1)
        go, d, nm, nv = _adamw(w2, g.reshape(w2.shape), moms[n].reshape(w2.shape), vels[n].reshape(w2.shape),
                               name=f"adamw_{n}")
        outs[n] = (go.reshape(w.shape), d.reshape(w.shape), nm.reshape(w.shape), nv.reshape(w.shape))
        return nv

    tags = _Exchanges.REDUCE_ORDER
    last = ex.finish_exchange(tags[0], after=dx)
    for prev, tag in zip(tags, tags[1:]):
        last = ex.finish_exchange(tag, after=last)
        for n, g in ex.reduced(prev, after=last).items():
            last = update(n, g)

    grads = {}
    small_names = ["ffn1_norm", "mix_norm", "ffn2_norm", "final_norm", "pool_scale", "b_alpha", "gla_norm", "w_alpha",
                   "loss"]
    small = small + [loss_blk[0:1]]
    rows = [a.size // LANES for a in small]
    slab = jnp.concatenate([a.reshape(-1, LANES) for a in small], axis=0)
    pad = -slab.shape[0] % 8
    slab = jnp.pad(slab, ((0, pad), (0, 0)))
    gathered = _all_gather_slab(slab).reshape(N_DEV, slab.shape[0], LANES)
    total = _slab_sum(gathered, name="sum_small_grads")
    off = 0
    for n, a, r in zip(small_names, small, rows):
        grads[n] = total[off:off + r].reshape(a.shape)
        off += r
    grads["w_alpha"] = lax.dynamic_slice_in_dim(grads["w_alpha"], chip * (GLA_DK_TOTAL // N_CHIPS),
                                                GLA_DK_TOTAL // N_CHIPS, axis=1)

    loss = grads.pop("loss")[0, 0]
    for n in small_names[:-1]:
        last = update(n, grads[n])
    for n, g in ex.reduced(tags[-1], after=last).items():
        update(n, g)
    return (loss, dx[None], *[outs[n][0] for n in names], *[outs[n][1] for n in names],
            *[outs[n][2] for n in names], *[outs[n][3] for n in names])


class _Exchanges:
    GATHER_GROUPS = (("ffn1_w_in",), ("ffn1_w_out",), ("w_in_mix", "w_pool", "w_alpha"), ("w_out_mix",),
                     ("ffn2_w_in",), ("ffn2_w_out",))
    REDUCE_ORDER = ("ffn2", "mix", "ffn1_out", "ffn1_in")

    def __init__(self, sc_idx, c_idx):
        self.sc_idx, self.c_idx = sc_idx, c_idx
        self._gathers, self._swaps, self._reduces, self._joins = {}, {}, {}, {}

    def start_gather(self, bufs, groups, *, name):
        handles, token = _gather_ici_start([[bufs[n] for n in g] for g in groups], name=name)
        for g, h in zip(groups, handles):
            self._gathers[g[0]] = (g, h)
        return token

    def gathered(self, first, after):
        names, handle = self._gathers.pop(first)
        got = _gather_ici_wait(handle, after, name=f"gather_ici_wait_{first}")
        return dict(zip(names, _forward_halves(got, name=f"gather_forward_{first}")))

    def begin_reduce(self, tag, full):
        g4 = [a.reshape(N_CHIPS, 2, a.shape[1] // 2, a.shape[2]) for a in full.values()]
        lands = [(a.shape[0],) + a.shape[2:] for a in g4]
        handle, token = _split_copy_start(g4, lands, len(g4), _sibling_swap_copies, name=f"swap_start_{tag}")
        self._swaps[tag] = (list(full), handle)
        return token

    def start_reduce(self, tag, after):
        names, handle = self._swaps.pop(tag)
        g4, from_sibling = _split_copy_wait(handle, _sibling_swap_copies, after, name=f"swap_wait_{tag}")
        pair = [_pair_sum(a, b, self.c_idx, name=f"pair_sum_{n}") for n, a, b in zip(names, g4, from_sibling)]
        lands = [(N_CHIPS - 1,) + a.shape[1:] for a in pair]
        handle, token = _split_copy_start(pair, lands, len(pair) * (N_CHIPS - 1), _chip_exchange_copies,
                                          name=f"exchange_start_{tag}")
        self._reduces[tag] = (names, handle)
        return token

    def finish_exchange(self, tag, after):
        names, handle = self._reduces.pop(tag)
        pair, lands = _split_copy_wait(handle, _chip_exchange_copies, after, name=f"exchange_wait_{tag}")
        halves = [_chip_sum(a, b, self.sc_idx, name=f"chip_sum_{n}") for n, a, b in zip(names, pair, lands)]
        handle, token = _join_start(halves, name=f"join_start_{tag}")
        self._joins[tag] = (names, handle)
        return token

    def reduced(self, tag, after):
        names, handle = self._joins.pop(tag)
        return dict(zip(names, _join_wait(handle, after, name=f"join_wait_{tag}")))


def _forward_backward(h0, target, ex, started, sp):
    g1, gm, g2, gf = sp["g1"], sp["gm"], sp["g2"], sp["gf"]
    pool_scale, b_alpha, gla_norm = sp["pool_scale"], sp["b_alpha"], sp["gla_norm"]
    cs_mix = D_IN // N_CHIPS

    n1 = _rms_fwd(h0, g1 + started, name="ffn1_norm")
    w1_in = ex.gathered("ffn1_w_in", after=n1)["ffn1_w_in"]
    gu1, a1 = _ffn_up(n1, w1_in, name="ffn1_up")
    w1_out = ex.gathered("ffn1_w_out", after=a1)["ffn1_w_out"].reshape(D_FF, D_MODEL)
    h1 = _mm_nn(a1, w1_out, h0, 0.5, tm=512, tn=D_MODEL, tk=SHARD_TILE, name="ffn1_down")
    saved1 = (n1, gu1, a1)
    n_mix = _rms_fwd(h1, gm, name="mix_norm")
    gw = ex.gathered("w_in_mix", after=n_mix)
    w_mix = jnp.concatenate([gw["w_in_mix"][t] for t in range(N_CHIPS)], axis=1)
    w_mix = jnp.pad(w_mix, ((0, 0), (0, D_IN_PAD - D_IN)))[None]
    wp = gw["w_pool"].reshape(N_CHIPS, 4, POOL_GROUP_DIM // N_CHIPS, POOL_GROUP_DIM)
    wp = wp.transpose(1, 0, 2, 3).reshape(4, POOL_GROUP_DIM, POOL_GROUP_DIM)
    wa = gw["w_alpha"].transpose(1, 0, 2).reshape(GLA_GATE_RANK, GLA_DK_TOTAL)
    wa = jnp.pad(wa, ((0, LANES - GLA_GATE_RANK), (0, 0))).astype(BF16)
    u = _mm_nn(n_mix, w_mix[0], None, 1.0, tm=512, tn=D_IN_PAD, tk=D_MODEL, name="mix_in")
    y_pool = _pool_fwd(u, wp, pool_scale, name="pool_fwd")
    cat, o_gla, states = _gla_fwd(u, y_pool, wa, b_alpha, gla_norm, name="gla_fwd")
    w_omix = ex.gathered("w_out_mix", after=cat)["w_out_mix"].reshape(D_MODEL, D_MODEL)
    h2 = _mm_nn(cat, w_omix, h1, 1.0, tm=512, tn=D_MODEL, tk=1024, name="mix_out")
    n3 = _rms_fwd(h2, g2, name="ffn2_norm")
    w2_in = ex.gathered("ffn2_w_in", after=n3)["ffn2_w_in"]
    gu2, a2 = _ffn_up(n3, w2_in, name="ffn2_up")
    w2_out = ex.gathered("ffn2_w_out", after=a2)["ffn2_w_out"].reshape(D_FF, D_MODEL)
    h3 = _mm_nn(a2, w2_out, h2, 0.5, tm=512, tn=D_MODEL, tk=SHARD_TILE, name="ffn2_down")
    saved2 = (n3, gu2, a2)
    loss_blk, dh3, dh3b, d_gf = _final_loss(h3, gf, target, name="final_loss")

    dgu2, dw2_in, dw2_out = _ffn_backward_weights(dh3b, saved2, w2_in, w2_out, dh3b, "ffn2")
    tok = ex.begin_reduce("ffn2", {"ffn2_w_in": dw2_in, "ffn2_w_out": dw2_out})
    dh2, dh2b, d_g2 = _mm_nt_rmsbwd(dgu2, w2_in, h2, dh3, g2 + tok[0, 0], tk=COL_TILE, name="ffn2_dx")
    tok = ex.start_reduce("ffn2", after=dh2b)
    S = h0.shape[0]
    dcat = _mm_nt(dh2b, w_omix, tok, tm=512, tn=1024, name="mix_out_dx")
    dw_omix = _mm_tn(
        cat, dh2b, grid=(4,),
        a_spec=pl.BlockSpec((S, 512), lambda j: (0, j)), b_spec=pl.BlockSpec((S, D_MODEL), lambda j: (0, 0)),
        out_spec=pl.BlockSpec((512, D_MODEL), lambda j: (j, 0)),
        out_shape=jax.ShapeDtypeStruct((D_MODEL, D_MODEL), BF16), scale=1.0, name="mix_out_dw")
    dp, dw_pool, d_pscale = _pool_bwd(u, dcat, wp, pool_scale, name="pool_bwd")
    du, d_wa, d_ba, d_gn = _gla_bwd(u, o_gla, states, dcat, dp, wa, b_alpha, gla_norm, name="gla_bwd")
    du = du[None]
    tn_mix = COL_TILE
    dw_mix = _mm_tn(
        n_mix.T, du, grid=(2, D_IN_PAD // tn_mix), a_is_transposed=True,
        a_spec=pl.BlockSpec((D_MODEL // 2, S), lambda i, j: (i, 0)),
        b_spec=pl.BlockSpec((None, S, tn_mix), lambda i, j: (0, 0, j)),
        out_spec=pl.BlockSpec((D_MODEL // 2, tn_mix), lambda i, j: (i, j)),
        out_shape=jax.ShapeDtypeStruct((D_MODEL, D_IN_PAD), BF16), scale=1.0, name="mix_in_dw")
    dw_mix_s = dw_mix[:, :D_IN].reshape(D_MODEL, N_CHIPS, cs_mix).transpose(1, 0, 2)
    dw_pool_s = dw_pool.reshape(4, N_CHIPS, POOL_GROUP_DIM // N_CHIPS, POOL_GROUP_DIM).transpose(1, 0, 2, 3)
    dw_pool_s = dw_pool_s.reshape(N_CHIPS, POOL_GROUP_DIM, POOL_GROUP_DIM).astype(BF16)
    tok = ex.begin_reduce("mix", {"w_in_mix": dw_mix_s,
                                  "w_out_mix": dw_omix.reshape(N_CHIPS, D_MODEL // N_CHIPS, D_MODEL),
                                  "w_pool": dw_pool_s})
    dh1, dh1b, d_gm = _mm_nt_rmsbwd(du, w_mix, h1, dh2, gm + tok[0, 0], tk=tn_mix, name="mix_in_dx")
    tok = ex.start_reduce("mix", after=dh1b)
    def reduce_now(tag, full):
        return ex.start_reduce(tag, after=ex.begin_reduce(tag, full))

    mix_started = tok
    dgu1, dw1_in, _ = _ffn_backward_weights(
        dh1b, saved1, w1_in, w1_out, None, "ffn1",
        on_dw_out=lambda dw: mix_started + reduce_now("ffn1_out", {"ffn1_w_out": dw}))
    tok = reduce_now("ffn1_in", {"ffn1_w_in": dw1_in})
    dx, _, d_g1 = _mm_nt_rmsbwd(dgu1, w1_in, h0, dh1, g1 + tok[0, 0], tk=COL_TILE, name="ffn1_dx")
    small = [d_g1, d_gm, d_g2, d_gf, d_pscale, d_ba, d_gn, d_wa[:GLA_GATE_RANK]]
    return loss_blk, dx, small
```

```python
import functools

import jax
import jax.numpy as jnp
from jax import lax
from jax.experimental import pallas as pl
from jax.experimental.pallas import tpu as pltpu

F32 = jnp.float32
BF16 = jnp.bfloat16
MESH = pl.DeviceIdType.MESH

D_MODEL = 2048
D_FF = 5632
D_POOL = 1024
POOL_WINDOWS = (2, 4, 8, 16)
POOL_GROUP_DIM = 256
D_GLA = 1024
GLA_HEADS = 4
GLA_DV = 256
GLA_DK = 128
GLA_DK_TOTAL = 512
GLA_GATE_RANK = 16
GATE_LOGIT_NORMALIZER = 16.0
CHUNK = 64
D_IN = 4112
D_IN_PAD = 4224
EPS = 1e-6

ADAM_LR = 0.001
ADAM_B1 = 0.9
ADAM_B2 = 0.999
ADAM_EPS = 1e-08
ADAM_WD = 0.01
ADAM_STEP = 10

N_CHIPS = 4
N_DEV = 8
V7X_VMEM_BYTES = 64 * 1024 * 1024
LANES = 128
MXU_TILE = 256
COL_TILE = 1408
SHARD_TILE = 2816


def _cparams(semantics, vmem_mb):
    assert vmem_mb * 1024 * 1024 < V7X_VMEM_BYTES
    return pltpu.CompilerParams(dimension_semantics=semantics, vmem_limit_bytes=vmem_mb * 1024 * 1024)


def _dot_nn(a, b):
    return jnp.dot(a, b, preferred_element_type=F32)


def _dot_nt(a, b):
    return lax.dot_general(a, b, (((1,), (1,)), ((), ())), preferred_element_type=F32)


def _dot_tn(a, b):
    return lax.dot_general(a, b, (((0,), (0,)), ((), ())), preferred_element_type=F32)


def _sigmoid(x):
    return 1.0 / (1.0 + jnp.exp(-x))


def _rms_fwd(x, g, *, name):
    S, D = x.shape
    tm = 256

    def body(x_ref, g_ref, o_ref):
        xv = x_ref[...]
        r = lax.rsqrt(jnp.mean(xv * xv, axis=-1, keepdims=True) + EPS)
        o_ref[...] = (xv * r * g_ref[...]).astype(BF16)

    return pl.pallas_call(
        body, name=name, grid=(S // tm,),
        in_specs=[pl.BlockSpec((tm, D), lambda i: (i, 0)), pl.BlockSpec((1, D), lambda i: (0, 0))],
        out_specs=pl.BlockSpec((tm, D), lambda i: (i, 0)),
        out_shape=jax.ShapeDtypeStruct((S, D), BF16),
        compiler_params=_cparams(("parallel",), 32),
    )(x, g)


def _ffn_up(n, w_in, early, *, name):
    S, D = n.shape
    ns, _, cs = w_in.shape
    half = ns // 2
    F = cs * half
    tm, tn = 128, SHARD_TILE
    nb = cs // tn

    def body(n_ref, wg_ref, wu_ref, *rest):
        gu_ref, a_ref = rest[len(early):]
        nv = n_ref[...]
        g = _dot_nn(nv, wg_ref[...])
        u = _dot_nn(nv, wu_ref[...])
        gu_ref[0] = g.astype(BF16)
        gu_ref[1] = u.astype(BF16)
        a_ref[...] = (g * _sigmoid(g) * u).astype(BF16)

    return pl.pallas_call(
        body, name=name, grid=(F // tn, S // tm),
        in_specs=[
            pl.BlockSpec((tm, D), lambda j, i: (i, 0)),
            pl.BlockSpec((None, D, tn), lambda j, i: (lax.div(j, nb), 0, lax.rem(j, nb))),
            pl.BlockSpec((None, D, tn), lambda j, i: (half + lax.div(j, nb), 0, lax.rem(j, nb))),
        ] + [pl.BlockSpec(memory_space=pl.ANY)] * len(early),
        out_specs=[
            pl.BlockSpec((2, tm, tn), lambda j, i: (0, i, j)),
            pl.BlockSpec((tm, tn), lambda j, i: (i, j)),
        ],
        out_shape=[jax.ShapeDtypeStruct((2, S, F), BF16), jax.ShapeDtypeStruct((S, F), BF16)],
        compiler_params=_cparams(("parallel", "parallel"), 56),
    )(n, w_in, w_in, *early)


def _mm_nn(a, b, resid, scale, *, tm, tn, tk, name):
    S, K = a.shape
    N = b.shape[1]
    nk = K // tk

    def body(*refs):
        if resid is None:
            a_ref, b_ref, o_ref, acc_ref = refs
            r_ref = None
        else:
            a_ref, b_ref, r_ref, o_ref, acc_ref = refs
        k = pl.program_id(2)

        @pl.when(k == 0)
        def _():
            acc_ref[...] = jnp.zeros_like(acc_ref)

        acc_ref[...] += _dot_nn(a_ref[...], b_ref[...])

        @pl.when(k == nk - 1)
        def _():
            out = acc_ref[...] * scale
            if r_ref is not None:
                out = r_ref[...] + out
            o_ref[...] = out

    in_specs = [pl.BlockSpec((tm, tk), lambda i, j, k: (i, k)), pl.BlockSpec((tk, tn), lambda i, j, k: (k, j))]
    args = [a, b]
    if resid is not None:
        in_specs.append(pl.BlockSpec((tm, tn), lambda i, j, k: (i, j)))
        args.append(resid)
    return pl.pallas_call(
        body, name=name, grid=(S // tm, N // tn, nk),
        in_specs=in_specs,
        out_specs=pl.BlockSpec((tm, tn), lambda i, j, k: (i, j)),
        out_shape=jax.ShapeDtypeStruct((S, N), F32),
        scratch_shapes=[pltpu.VMEM((tm, tn), F32)],
        compiler_params=_cparams(("parallel", "parallel", "arbitrary"), 48),
    )(*args)


def _mm_nt(a, b, after, *, tm, tn, name):
    S, K = a.shape
    N = b.shape[0]

    def body(a_ref, b_ref, after_ref, o_ref):
        o_ref[...] = _dot_nt(a_ref[...], b_ref[...])

    return pl.pallas_call(
        body, name=name, grid=(N // tn, S // tm),
        in_specs=[pl.BlockSpec((tm, K), lambda j, i: (i, 0)), pl.BlockSpec((tn, K), lambda j, i: (j, 0)),
                  pl.BlockSpec(memory_space=pl.ANY)],
        out_specs=pl.BlockSpec((tm, tn), lambda j, i: (i, j)),
        out_shape=jax.ShapeDtypeStruct((S, N), F32),
        compiler_params=_cparams(("parallel", "parallel"), 48),
    )(a, b, after)


def _mm_tn(a, b, *, grid, a_spec, b_spec, out_spec, out_shape, scale, name, a_is_transposed=False):
    dot = _dot_nn if a_is_transposed else _dot_tn

    def body(a_ref, b_ref, o_ref):
        o_ref[...] = (scale * dot(a_ref[...], b_ref[...])).astype(o_ref.dtype)

    return pl.pallas_call(
        body, name=name, grid=grid, in_specs=[a_spec, b_spec], out_specs=out_spec, out_shape=out_shape,
        compiler_params=_cparams(("parallel",) * len(grid), 56),
    )(a, b)


def _ffn_bwd_act(dhb, w_out, gu, after, *, name):
    S, D = dhb.shape
    F = w_out.shape[0]
    tm, tn = 256, SHARD_TILE

    def body(dh_ref, w_ref, gu_ref, after_ref, dgu_ref):
        da = 0.5 * _dot_nt(dh_ref[...], w_ref[...])
        g = gu_ref[0].astype(F32)
        u = gu_ref[1].astype(F32)
        s = _sigmoid(g)
        dgu_ref[0] = (da * u * (s * (1.0 + g * (1.0 - s)))).astype(BF16)
        dgu_ref[1] = (da * (g * s)).astype(BF16)

    return pl.pallas_call(
        body, name=name, grid=(F // tn, S // tm),
        in_specs=[
            pl.BlockSpec((tm, D), lambda j, i: (i, 0)),
            pl.BlockSpec((tn, D), lambda j, i: (j, 0)),
            pl.BlockSpec((2, tm, tn), lambda j, i: (0, i, j)),
            pl.BlockSpec(memory_space=pl.ANY),
        ],
        out_specs=pl.BlockSpec((2, tm, tn), lambda j, i: (0, i, j)),
        out_shape=jax.ShapeDtypeStruct((2, S, F), BF16),
        compiler_params=_cparams(("parallel", "parallel"), 56),
    )(dhb, w_out, gu, after)


def _mm_nt_rmsbwd(dact, w, h_in, dh_out, g, *, tk, name):
    ng, S, fg = dact.shape
    ns, D, cs = w.shape
    assert ng * fg == ns * cs
    tm, rc = 512, 64
    kpg, kps = fg // tk, cs // tk
    nk = ng * kpg

    def body(a_ref, w_ref, h_ref, dho_ref, g_ref, dh_ref, dhb_ref, dg_ref, acc_ref):
        i = pl.program_id(0)
        k = pl.program_id(1)

        @pl.when(k == 0)
        def _():
            acc_ref[...] = jnp.zeros_like(acc_ref)

        acc_ref[...] += _dot_nt(a_ref[...], w_ref[...])

        @pl.when(jnp.logical_and(i == 0, k == 0))
        def _():
            dg_ref[...] = jnp.zeros_like(dg_ref)

        @pl.when(k == nk - 1)
        def _():
            gv = g_ref[...]

            def rows_step(c, dg):
                rows = pl.ds(pl.multiple_of(c * rc, rc), rc)
                dn = acc_ref[rows, :]
                xv = h_ref[rows, :]
                r = lax.rsqrt(jnp.mean(xv * xv, axis=-1, keepdims=True) + EPS)
                xh = xv * r
                dng = dn * gv
                dx = r * (dng - xh * jnp.mean(dng * xh, axis=-1, keepdims=True))
                out = dho_ref[rows, :] + dx
                dh_ref[rows, :] = out
                dhb_ref[rows, :] = out.astype(BF16)
                return dg + jnp.sum(dn * xh, axis=0, keepdims=True)

            dg_ref[...] += lax.fori_loop(0, tm // rc, rows_step, jnp.zeros((1, D), F32))

    return pl.pallas_call(
        body, name=name, grid=(S // tm, nk),
        in_specs=[
            pl.BlockSpec((None, tm, tk), lambda i, k: (lax.div(k, kpg), i, lax.rem(k, kpg))),
            pl.BlockSpec((None, D, tk), lambda i, k: (lax.div(k, kps), 0, lax.rem(k, kps))),
            pl.BlockSpec((tm, D), lambda i, k: (i, 0)),
            pl.BlockSpec((tm, D), lambda i, k: (i, 0)),
            pl.BlockSpec((1, D), lambda i, k: (0, 0)),
        ],
        out_specs=[
            pl.BlockSpec((tm, D), lambda i, k: (i, 0)),
            pl.BlockSpec((tm, D), lambda i, k: (i, 0)),
            pl.BlockSpec((1, D), lambda i, k: (0, 0)),
        ],
        out_shape=[jax.ShapeDtypeStruct((S, D), F32), jax.ShapeDtypeStruct((S, D), BF16),
                   jax.ShapeDtypeStruct((1, D), F32)],
        scratch_shapes=[pltpu.VMEM((tm, D), F32)],
        compiler_params=_cparams(("arbitrary", "arbitrary"), 56),
    )(dact, w, h_in, dh_out, g)


def _final_loss(h, g, target, *, name):
    S, D = h.shape
    tm = 256

    def body(h_ref, g_ref, t_ref, loss_ref, dh_ref, dhb_ref, dg_ref):
        i = pl.program_id(0)

        @pl.when(i == 0)
        def _():
            loss_ref[...] = jnp.zeros_like(loss_ref)
            dg_ref[...] = jnp.zeros_like(dg_ref)

        xv = h_ref[...]
        gv = g_ref[...]
        r = lax.rsqrt(jnp.mean(xv * xv, axis=-1, keepdims=True) + EPS)
        xh = xv * r
        e = xh * gv - t_ref[...]
        loss_ref[...] += 0.5 * jnp.sum(jnp.mean(e * e, axis=-1, keepdims=True))
        dy = e * (1.0 / D)
        dg_ref[...] += jnp.sum(dy * xh, axis=0, keepdims=True)
        dyg = dy * gv
        dx = r * (dyg - xh * jnp.mean(dyg * xh, axis=-1, keepdims=True))
        dh_ref[...] = dx
        dhb_ref[...] = dx.astype(BF16)

    return pl.pallas_call(
        body, name=name, grid=(S // tm,),
        in_specs=[pl.BlockSpec((tm, D), lambda i: (i, 0)), pl.BlockSpec((1, D), lambda i: (0, 0)),
                  pl.BlockSpec((tm, D), lambda i: (i, 0))],
        out_specs=[pl.BlockSpec((8, LANES), lambda i: (0, 0)), pl.BlockSpec((tm, D), lambda i: (i, 0)),
                   pl.BlockSpec((tm, D), lambda i: (i, 0)), pl.BlockSpec((1, D), lambda i: (0, 0))],
        out_shape=[jax.ShapeDtypeStruct((8, LANES), F32), jax.ShapeDtypeStruct((S, D), F32),
                   jax.ShapeDtypeStruct((S, D), BF16), jax.ShapeDtypeStruct((1, D), F32)],
        compiler_params=_cparams(("arbitrary",), 40),
    )(h, g, target)


POOL_HALO = 16
POOL_ROWS = 256


def _pool_window_mean_minus_token(ext, tok0, w):
    s = ext
    k = 1
    while k < w:
        s = s + pltpu.roll(s, k, 0)
        k *= 2
    win = s[POOL_HALO:, :]
    tok = tok0 + lax.broadcasted_iota(jnp.int32, (POOL_ROWS, 1), 0)
    cnt = jnp.minimum(tok + 1, w).astype(F32)
    return win / cnt - ext[POOL_HALO:, :], cnt


def _pool_fwd(u, w_pool, scale, *, name):
    S = u.shape[0]
    C = POOL_GROUP_DIM
    nsteps = S // POOL_ROWS

    def body(p_ref, w_ref, sc_ref, y_ref, xp_ref):
        xp_ref[0:POOL_HALO, :] = jnp.zeros((POOL_HALO, D_POOL), F32)
        xp_ref[POOL_HALO:, :] = p_ref[...]
        for gi, win in enumerate(POOL_WINDOWS):
            cols = slice(gi * C, (gi + 1) * C)

            def step(c, carry, cols=cols, win=win, gi=gi):
                r0 = pl.multiple_of(c * POOL_ROWS, POOL_ROWS)
                ext = xp_ref[pl.ds(r0, POOL_ROWS + POOL_HALO), cols]
                pooled, _ = _pool_window_mean_minus_token(ext, r0, win)
                y = _dot_nn(pooled.astype(BF16), w_ref[gi]) * sc_ref[:, cols]
                y_ref[pl.ds(r0, POOL_ROWS), cols] = y.astype(BF16)
                return carry

            lax.fori_loop(0, nsteps, step, 0)

    return pl.pallas_call(
        body, name=name, grid=(1,),
        in_specs=[pl.BlockSpec((S, D_POOL), lambda i: (0, 0)),
                  pl.BlockSpec((4, C, C), lambda i: (0, 0, 0)),
                  pl.BlockSpec((1, D_POOL), lambda i: (0, 0))],
        out_specs=pl.BlockSpec((S, D_POOL), lambda i: (0, 0)),
        out_shape=jax.ShapeDtypeStruct((S, D_POOL), BF16),
        scratch_shapes=[pltpu.VMEM((S + POOL_HALO, D_POOL), F32)],
        compiler_params=_cparams(("arbitrary",), 48),
    )(u, w_pool, scale)


def _pool_bwd(u, dcat, w_pool, scale, *, name):
    S = u.shape[0]
    C = POOL_GROUP_DIM
    nsteps = S // POOL_ROWS

    def body(p_ref, dy_ref, w_ref, sc_ref, dp_ref, dw_ref, dsc_ref, xp_ref, e_ref, neg_ref):
        xp_ref[0:POOL_HALO, :] = jnp.zeros((POOL_HALO, D_POOL), F32)
        xp_ref[POOL_HALO:, :] = p_ref[...]
        e_ref[S:, :] = jnp.zeros((POOL_HALO, C), F32)
        for gi, win in enumerate(POOL_WINDOWS):
            cols = slice(gi * C, (gi + 1) * C)

            def step_a(c, carry, cols=cols, win=win, gi=gi):
                dw, dsc = carry
                r0 = pl.multiple_of(c * POOL_ROWS, POOL_ROWS)
                ext = xp_ref[pl.ds(r0, POOL_ROWS + POOL_HALO), cols]
                pooled, cnt = _pool_window_mean_minus_token(ext, r0, win)
                pb = pooled.astype(BF16)
                wv = w_ref[gi]
                dy = dy_ref[pl.ds(r0, POOL_ROWS), cols]
                dsc = dsc + jnp.sum(dy * _dot_nn(pb, wv), axis=0, keepdims=True)
                dyp = (dy * sc_ref[:, cols]).astype(BF16)
                dw = dw + _dot_tn(pb, dyp)
                dpooled = _dot_nt(dyp, wv)
                e_ref[pl.ds(r0, POOL_ROWS), :] = dpooled / cnt
                neg_ref[pl.ds(r0, POOL_ROWS), :] = -dpooled
                return dw, dsc

            dw, dsc = lax.fori_loop(0, nsteps, step_a, (jnp.zeros((C, C), F32), jnp.zeros((1, C), F32)))
            dw_ref[gi] = dw
            dsc_ref[:, cols] = dsc

            def step_b(c, carry, cols=cols, win=win):
                r0 = pl.multiple_of(c * POOL_ROWS, POOL_ROWS)
                s = e_ref[pl.ds(r0, POOL_ROWS + POOL_HALO), :]
                n = POOL_ROWS + POOL_HALO
                k = 1
                while k < win:
                    s = s + pltpu.roll(s, n - k, 0)
                    k *= 2
                du = s[:POOL_ROWS, :] + neg_ref[pl.ds(r0, POOL_ROWS), :]
                dp_ref[pl.ds(r0, POOL_ROWS), cols] = du.astype(BF16)
                return carry

            lax.fori_loop(0, nsteps, step_b, 0)

    return pl.pallas_call(
        body, name=name, grid=(1,),
        in_specs=[pl.BlockSpec((S, D_POOL), lambda i: (0, 0)),
                  pl.BlockSpec((S, D_POOL), lambda i: (0, 0)),
                  pl.BlockSpec((4, C, C), lambda i: (0, 0, 0)),
                  pl.BlockSpec((1, D_POOL), lambda i: (0, 0))],
        out_specs=[pl.BlockSpec((S, D_POOL), lambda i: (0, 0)),
                   pl.BlockSpec((4, C, C), lambda i: (0, 0, 0)),
                   pl.BlockSpec((1, D_POOL), lambda i: (0, 0))],
        out_shape=[jax.ShapeDtypeStruct((S, D_POOL), BF16), jax.ShapeDtypeStruct((4, C, C), F32),
                   jax.ShapeDtypeStruct((1, D_POOL), F32)],
        scratch_shapes=[pltpu.VMEM((S + POOL_HALO, D_POOL), F32), pltpu.VMEM((S + POOL_HALO, C), F32),
                        pltpu.VMEM((S, C), F32)],
        compiler_params=_cparams(("arbitrary",), 56),
    )(u, dcat, w_pool, scale)


GLA_ROWS = 128
U_Q_BLK, U_K_BLK = 2, 3
U_V_BLK, U_G_BLK = 2, 3
U_R_BLK = 32


def _prefix_sum_rows(x):
    n = x.shape[0]
    row = lax.broadcasted_iota(jnp.int32, x.shape, 0)
    k = 1
    while k < n:
        x = x + jnp.where(row >= k, pltpu.roll(x, k, 0), 0.0)
        k *= 2
    return x


def _suffix_sum_rows(x):
    n = x.shape[0]
    row = lax.broadcasted_iota(jnp.int32, x.shape, 0)
    k = 1
    while k < n:
        x = x + jnp.where(row < n - k, pltpu.roll(x, n - k, 0), 0.0)
        k *= 2
    return x


def _log_sigmoid(z):
    return jnp.minimum(z, 0.0) - jnp.log(1.0 + jnp.exp(-jnp.abs(z)))


def _gla_chunk_terms(la_c, q_c, k_c):
    bc = _prefix_sum_rows(la_c)
    bl = jnp.sum(la_c, axis=0, keepdims=True)
    eb = jnp.exp(bc)
    enb = jnp.exp(-bc)
    etail = jnp.exp(bl - bc)
    qd = q_c * (GLA_DK ** -0.5) * eb
    ki = k_c * enb
    kt = k_c * etail
    d = jnp.exp(bl)
    return eb, enb, etail, qd, ki, kt, d


def _gla_fwd(u, y_pool, w_alpha, b_alpha, gnorm, *, name):
    S = u.shape[0]
    RB = GLA_ROWS
    ncc = RB // CHUNK
    H, DK, DV = GLA_HEADS, GLA_DK, GLA_DV

    def body(q_ref, k_ref, v_ref, go_ref, r_ref, yp_ref, wa_ref, ba_ref, gn_ref, cat_ref, o_ref, st_ref, state):
        i = pl.program_id(0)

        @pl.when(i == 0)
        def _():
            state[...] = jnp.zeros_like(state)

        cat_ref[:, :D_POOL] = yp_ref[...]
        y_ref = cat_ref.at[:, D_POOL:]

        z = _dot_nn(r_ref[...].astype(BF16), wa_ref[...]) + ba_ref[...]
        la = _log_sigmoid(z) / GATE_LOGIT_NORMALIZER
        ri = lax.broadcasted_iota(jnp.int32, (CHUNK, CHUNK), 0)
        ci = lax.broadcasted_iota(jnp.int32, (CHUNK, CHUNK), 1)
        tri = ri >= ci
        gn = gn_ref[...]
        for cc in range(ncc):
            rs = slice(cc * CHUNK, (cc + 1) * CHUNK)
            for h in range(H):
                ks = slice(h * DK, (h + 1) * DK)
                vs = slice(h * DV, (h + 1) * DV)
                _, _, _, qd, ki, kt, d = _gla_chunk_terms(la[rs, ks], q_ref[rs, ks], k_ref[rs, ks])
                qdb = qd.astype(BF16)
                vb = v_ref[rs, vs].astype(BF16)
                p = jnp.where(tri, _dot_nt(qdb, ki.astype(BF16)), 0.0)
                st = state[h]
                st_ref[cc, h] = st
                o = _dot_nn(p.astype(BF16), vb) + _dot_nt(qdb, st.astype(BF16))
                state[h] = st * d + _dot_tn(vb, kt.astype(BF16))
                o_ref[rs, vs] = o
                rinv = lax.rsqrt(jnp.mean(o * o, axis=-1, keepdims=True) + EPS)
                go = go_ref[rs, vs]
                y_ref[rs, vs] = (o * rinv * gn * (go * _sigmoid(go))).astype(BF16)

    nblk = S // RB
    return pl.pallas_call(
        body, name=name, grid=(nblk,),
        in_specs=[
            pl.BlockSpec((RB, GLA_DK_TOTAL), lambda i: (i, U_Q_BLK)),
            pl.BlockSpec((RB, GLA_DK_TOTAL), lambda i: (i, U_K_BLK)),
            pl.BlockSpec((RB, D_GLA), lambda i: (i, U_V_BLK)),
            pl.BlockSpec((RB, D_GLA), lambda i: (i, U_G_BLK)),
            pl.BlockSpec((RB, LANES), lambda i: (i, U_R_BLK)),
            pl.BlockSpec((RB, D_POOL), lambda i: (i, 0)),
            pl.BlockSpec((LANES, GLA_DK_TOTAL), lambda i: (0, 0)),
            pl.BlockSpec((1, GLA_DK_TOTAL), lambda i: (0, 0)),
            pl.BlockSpec((1, DV), lambda i: (0, 0)),
        ],
        out_specs=[
            pl.BlockSpec((RB, D_POOL + D_GLA), lambda i: (i, 0)),
            pl.BlockSpec((RB, D_GLA), lambda i: (i, 0)),
            pl.BlockSpec((ncc, H, DV, DK), lambda i: (i, 0, 0, 0)),
        ],
        out_shape=[jax.ShapeDtypeStruct((S, D_POOL + D_GLA), BF16), jax.ShapeDtypeStruct((S, D_GLA), F32),
                   jax.ShapeDtypeStruct((S // CHUNK, H, DV, DK), F32)],
        scratch_shapes=[pltpu.VMEM((H, DV, DK), F32)],
        compiler_params=_cparams(("arbitrary",), 32),
    )(u, u, u, u, u, y_pool, w_alpha, b_alpha, gnorm)


def _gla_bwd(u, o, states, dcat, dpool, w_alpha, b_alpha, gnorm, *, name):
    S = u.shape[0]
    RB = GLA_ROWS
    ncc = RB // CHUNK
    H, DK, DV = GLA_HEADS, GLA_DK, GLA_DV
    nblk = S // RB
    o_q, o_k = D_POOL, D_POOL + GLA_DK_TOTAL
    o_v, o_g, o_r = o_k + GLA_DK_TOTAL, o_k + GLA_DK_TOTAL + D_GLA, o_k + GLA_DK_TOTAL + 2 * D_GLA

    def body(q_ref, k_ref, v_ref, go_ref, r_ref, o_ref, st_ref, dy_ref, dpool_ref, wa_ref, ba_ref, gn_ref,
             du_ref, dwa_ref, dba_ref, dgn_ref, dstate, dz_ref):
        i = pl.program_id(0)

        @pl.when(i == 0)
        def _():
            dstate[...] = jnp.zeros_like(dstate)
            dwa_ref[...] = jnp.zeros_like(dwa_ref)
            dba_ref[...] = jnp.zeros_like(dba_ref)
            dgn_ref[...] = jnp.zeros_like(dgn_ref)

        du_ref[:, :o_q] = dpool_ref[...]
        dq_ref, dk_ref = du_ref.at[:, o_q:o_k], du_ref.at[:, o_k:o_v]
        dv_ref, dgo_ref, dr_ref = du_ref.at[:, o_v:o_g], du_ref.at[:, o_g:o_r], du_ref.at[:, o_r:]

        rb = r_ref[...].astype(BF16)
        wa = wa_ref[...]
        z = _dot_nn(rb, wa) + ba_ref[...]
        la = _log_sigmoid(z) / GATE_LOGIT_NORMALIZER
        ri = lax.broadcasted_iota(jnp.int32, (CHUNK, CHUNK), 0)
        ci = lax.broadcasted_iota(jnp.int32, (CHUNK, CHUNK), 1)
        tri = ri >= ci
        last_row = lax.broadcasted_iota(jnp.int32, (CHUNK, DK), 0) == CHUNK - 1
        gn = gn_ref[...]
        dgn = jnp.zeros((1, DV), F32)
        for cc in reversed(range(ncc)):
            rs = slice(cc * CHUNK, (cc + 1) * CHUNK)
            for h in range(H):
                ks = slice(h * DK, (h + 1) * DK)
                vs = slice(h * DV, (h + 1) * DV)
                eb, enb, etail, qd, ki, kt, d = _gla_chunk_terms(la[rs, ks], q_ref[rs, ks], k_ref[rs, ks])
                qdb, kib, ktb = qd.astype(BF16), ki.astype(BF16), kt.astype(BF16)
                vb = v_ref[rs, vs].astype(BF16)
                p = jnp.where(tri, _dot_nt(qdb, kib), 0.0)
                ov = o_ref[rs, vs]
                go = go_ref[rs, vs]
                dy = dy_ref[rs, vs]
                rinv = lax.rsqrt(jnp.mean(ov * ov, axis=-1, keepdims=True) + EPS)
                oh = ov * rinv
                sg = _sigmoid(go)
                dgo_ref[rs, vs] = (dy * (oh * gn) * (sg * (1.0 + go * (1.0 - sg)))).astype(BF16)
                don = dy * (go * sg)
                dgn = dgn + jnp.sum(don * oh, axis=0, keepdims=True)
                doh = don * gn
                do = rinv * (doh - oh * jnp.mean(doh * oh, axis=-1, keepdims=True))
                dob = do.astype(BF16)
                st = st_ref[cc, h]
                dst = dstate[h]
                stb, dstb = st.astype(BF16), dst.astype(BF16)
                dp = jnp.where(tri, _dot_nt(dob, vb), 0.0).astype(BF16)
                dv_ref[rs, vs] = (_dot_tn(p.astype(BF16), dob) + _dot_nt(ktb, dstb)).astype(BF16)
                dqd = _dot_nn(dp, kib) + _dot_nn(dob, stb)
                dki = _dot_tn(dp, qdb)
                dkt = _dot_nn(vb, dstb)
                dd = jnp.sum(dst * st, axis=0, keepdims=True)
                dstate[h] = dst * d + _dot_tn(dob, qdb)
                dq_ref[rs, ks] = (dqd * eb * (DK ** -0.5)).astype(BF16)
                dk_ref[rs, ks] = (dki * enb + dkt * etail).astype(BF16)
                dbl = jnp.sum(dkt * kt, axis=0, keepdims=True) + dd * d
                dbc = dqd * qd - dki * ki - dkt * kt
                dbc = dbc + jnp.where(last_row, dbl, 0.0)
                dla = _suffix_sum_rows(dbc)
                dz_ref[rs, ks] = dla * (1.0 / GATE_LOGIT_NORMALIZER) * (1.0 - _sigmoid(z[rs, ks]))
        dz = dz_ref[...]
        dzb = dz.astype(BF16)
        dr_ref[...] = _dot_nt(dzb, wa).astype(BF16)
        dwa_ref[...] += _dot_tn(rb, dzb)
        dba_ref[...] += jnp.sum(dz, axis=0, keepdims=True)
        dgn_ref[...] += dgn

    def rev(blk):
        return lambda i: (nblk - 1 - i, blk)

    return pl.pallas_call(
        body, name=name, grid=(nblk,),
        in_specs=[
            pl.BlockSpec((RB, GLA_DK_TOTAL), rev(U_Q_BLK)),
            pl.BlockSpec((RB, GLA_DK_TOTAL), rev(U_K_BLK)),
            pl.BlockSpec((RB, D_GLA), rev(U_V_BLK)),
            pl.BlockSpec((RB, D_GLA), rev(U_G_BLK)),
            pl.BlockSpec((RB, LANES), rev(U_R_BLK)),
            pl.BlockSpec((RB, D_GLA), rev(0)),
            pl.BlockSpec((ncc, H, DV, DK), lambda i: (nblk - 1 - i, 0, 0, 0)),
            pl.BlockSpec((RB, D_GLA), rev(1)),
            pl.BlockSpec((RB, D_POOL), rev(0)),
            pl.BlockSpec((LANES, GLA_DK_TOTAL), lambda i: (0, 0)),
            pl.BlockSpec((1, GLA_DK_TOTAL), lambda i: (0, 0)),
            pl.BlockSpec((1, DV), lambda i: (0, 0)),
        ],
        out_specs=[
            pl.BlockSpec((RB, D_IN_PAD), rev(0)),
            pl.BlockSpec((LANES, GLA_DK_TOTAL), lambda i: (0, 0)),
            pl.BlockSpec((1, GLA_DK_TOTAL), lambda i: (0, 0)),
            pl.BlockSpec((1, DV), lambda i: (0, 0)),
        ],
        out_shape=[
            jax.ShapeDtypeStruct((S, D_IN_PAD), BF16),
            jax.ShapeDtypeStruct((LANES, GLA_DK_TOTAL), F32), jax.ShapeDtypeStruct((1, GLA_DK_TOTAL), F32),
            jax.ShapeDtypeStruct((1, DV), F32),
        ],
        scratch_shapes=[pltpu.VMEM((H, DV, DK), F32), pltpu.VMEM((RB, GLA_DK_TOTAL), F32)],
        compiler_params=_cparams(("arbitrary",), 32),
    )(u, u, u, u, u, o, states, dcat, dpool, w_alpha, b_alpha, gnorm)


def _row_tile(rows, cols, itemsize, budget=2 * 1024 * 1024):
    if rows * cols * itemsize <= budget or rows % 16:
        return rows
    best = 16
    for t in range(16, rows + 1, 16):
        if rows % t == 0 and t * cols * itemsize <= budget:
            best = t
    return best


def _adamw(w, g, m, v, *, name):
    R, C = w.shape
    tr = _row_tile(R, C, 4, budget=1024 * 1024)

    def body(w_ref, g_ref, m_ref, v_ref, go_ref, d_ref, nm_ref, nv_ref):
        gv = g_ref[...]
        go_ref[...] = gv
        mn = ADAM_B1 * m_ref[...] + (1.0 - ADAM_B1) * gv
        vn = ADAM_B2 * v_ref[...] + (1.0 - ADAM_B2) * jnp.square(gv)
        m_hat = mn / (1.0 - ADAM_B1 ** ADAM_STEP)
        v_hat = vn / (1.0 - ADAM_B2 ** ADAM_STEP)
        d_ref[...] = -ADAM_LR * (m_hat / (jnp.sqrt(v_hat) + ADAM_EPS) + ADAM_WD * w_ref[...])
        nm_ref[...] = mn
        nv_ref[...] = vn

    spec = pl.BlockSpec((tr, C), lambda i: (i, 0))
    shp = jax.ShapeDtypeStruct((R, C), F32)
    return pl.pallas_call(
        body, name=name, grid=(R // tr,), in_specs=[spec] * 4, out_specs=[spec] * 4, out_shape=[shp] * 4,
        compiler_params=_cparams(("parallel",), 32),
    )(w, g, m, v)


def _pair_sum(g4, recv, c_idx, *, name):
    ns, _, R2, C = g4.shape
    tr = _row_tile(R2, C, 2)

    def body(c_ref, g_ref, r_ref, o_ref):
        o_ref[...] = (g_ref[...].astype(F32) + r_ref[...].astype(F32)).astype(BF16)

    return pl.pallas_call(
        body, name=name,
        grid_spec=pltpu.PrefetchScalarGridSpec(
            num_scalar_prefetch=1, grid=(ns, R2 // tr),
            in_specs=[pl.BlockSpec((None, None, tr, C), lambda s, i, c: (s, c[0], i, 0)),
                      pl.BlockSpec((None, tr, C), lambda s, i, c: (s, i, 0))],
            out_specs=pl.BlockSpec((None, tr, C), lambda s, i, c: (s, i, 0)),
        ),
        out_shape=jax.ShapeDtypeStruct((ns, R2, C), BF16),
        compiler_params=_cparams(("parallel", "parallel"), 32),
    )(c_idx, g4, recv)


def _chip_sum(part, recv, sc_idx, *, name):
    _, R2, C = part.shape
    tr = _row_tile(R2, C, 4)
    nblk = R2 // tr

    def body(s_ref, p_ref, r_ref, o_ref):
        acc = p_ref[...].astype(F32)
        for j in range(N_CHIPS - 1):
            acc = acc + r_ref[j].astype(F32)
        o_ref[...] = acc

    return pl.pallas_call(
        body, name=name,
        grid_spec=pltpu.PrefetchScalarGridSpec(
            num_scalar_prefetch=1, grid=(nblk,),
            in_specs=[pl.BlockSpec((None, tr, C), lambda i, s: (s[0], i, 0)),
                      pl.BlockSpec((N_CHIPS - 1, tr, C), lambda i, s: (0, i, 0))],
            out_specs=pl.BlockSpec((tr, C), lambda i, s: (s[1] * nblk + i, 0)),
        ),
        out_shape=jax.ShapeDtypeStruct((2 * R2, C), F32),
        compiler_params=_cparams(("parallel",), 32),
    )(sc_idx, part, recv)


def _cast_into_slot(w, sc_idx, dtype, after, *, name):
    R, C = w.shape
    tr = _row_tile(R, C, 4)

    def body(s_ref, w_ref, after_ref, o_ref):
        o_ref[...] = w_ref[...].astype(dtype)

    return pl.pallas_call(
        body, name=name,
        grid_spec=pltpu.PrefetchScalarGridSpec(
            num_scalar_prefetch=1, grid=(R // tr,),
            in_specs=[pl.BlockSpec((tr, C), lambda i, s: (i, 0)), pl.BlockSpec(memory_space=pl.ANY)],
            out_specs=pl.BlockSpec((None, tr, C), lambda i, s: (s[0], i, 0)),
        ),
        out_shape=jax.ShapeDtypeStruct((N_CHIPS, R, C), dtype),
        compiler_params=_cparams(("parallel",), 32),
    )(sc_idx, w, after)


def _slab_sum(slabs, *, name):
    n, M, C = slabs.shape

    def body(x_ref, o_ref):
        acc = x_ref[0]
        for d in range(1, n):
            acc = acc + x_ref[d]
        o_ref[...] = acc

    return pl.pallas_call(
        body, name=name, out_shape=jax.ShapeDtypeStruct((M, C), F32),
    )(slabs)


def _mesh_position():
    x, y, c = lax.axis_index("x"), lax.axis_index("y"), lax.axis_index("c")
    other_chips = [(1 - x, y), (x, 1 - y), (1 - x, 1 - y)]
    return x, y, c, other_chips


ANY = pl.BlockSpec(memory_space=pl.ANY)


HBM = pl.BlockSpec(memory_space=pltpu.HBM)
SEM = pl.BlockSpec(memory_space=pltpu.SEMAPHORE)
SPLIT_COPY = pltpu.CompilerParams(has_side_effects=pltpu.SideEffectType.DATAFLOW_SIDE_EFFECTING)
TOKEN = jax.ShapeDtypeStruct((8, LANES), F32)


def _in_hbm(a):
    return pltpu.with_memory_space_constraint(a, pltpu.HBM)


def _half_rows(ref, slot, half):
    hr = ref.shape[1] // 2
    return ref.at[slot, pl.ds(half * hr, hr), :]


def _gather_ici_start(groups, *, name):
    flat = [b for g in groups for b in g]
    K, G = len(flat), len(groups)

    def body(*refs):
        ins, sems, token = refs[:K], refs[K:K + 2 * G], refs[-1]
        x, y, c, chips = _mesh_position()
        s = 2 * x + y
        k = 0
        for gi, g in enumerate(groups):
            for n in range(len(g)):
                own = _half_rows(ins[k], s, c)
                for j, chip in enumerate(chips):
                    pltpu.make_async_remote_copy(
                        src_ref=own, dst_ref=own, send_sem=sems[2 * gi].at[n * (N_CHIPS - 1) + j],
                        recv_sem=sems[2 * gi + 1].at[n * (N_CHIPS - 1) + j],
                        device_id=(*chip, c), device_id_type=MESH).start()
                k += 1
        token[...] = jnp.zeros_like(token)

    sem_shapes = []
    for g in groups:
        sem_shapes += [pltpu.SemaphoreType.DMA((len(g) * (N_CHIPS - 1),))] * 2
    out = pl.pallas_call(
        body, name=name,
        in_specs=[HBM] * K,
        out_specs=[SEM] * (2 * G) + [HBM] * K + [pl.BlockSpec(memory_space=pltpu.VMEM)],
        out_shape=sem_shapes + [pltpu.HBM(b.shape, b.dtype) for b in flat] + [TOKEN],
        input_output_aliases={k: 2 * G + k for k in range(K)},
        compiler_params=SPLIT_COPY,
    )(*[_in_hbm(b) for b in flat])
    handles, k = [], 2 * G
    for gi, g in enumerate(groups):
        handles.append((out[2 * gi], out[2 * gi + 1], list(out[k:k + len(g)])))
        k += len(g)
    return handles, out[-1]


def _gather_ici_wait(handle, after, *, name):
    send, recv, bufs = handle
    n = len(bufs)

    def body(*refs):
        ins, send_ref, recv_ref = refs[:n], refs[n], refs[n + 1]
        x, y, c, chips = _mesh_position()
        s = 2 * x + y
        for k in range(n):
            own = _half_rows(ins[k], s, c)
            for j, chip in enumerate(chips):
                cp = pltpu.make_async_remote_copy(
                    src_ref=own, dst_ref=_half_rows(ins[k], 2 * chip[0] + chip[1], c),
                    send_sem=send_ref.at[k * (N_CHIPS - 1) + j], recv_sem=recv_ref.at[k * (N_CHIPS - 1) + j],
                    device_id=(*chip, c), device_id_type=MESH)
                cp.wait_send()
                cp.wait_recv()

    return pl.pallas_call(
        body, name=name,
        in_specs=[HBM] * n + [SEM, SEM, ANY], out_specs=[HBM] * n,
        out_shape=[pltpu.HBM(b.shape, b.dtype) for b in bufs],
        input_output_aliases={k: k for k in range(n)},
        compiler_params=SPLIT_COPY,
    )(*bufs, send, recv, after)


def _forward_halves(bufs, *, name):
    K = len(bufs)
    per = N_CHIPS - 1

    def body(*refs):
        outs = refs[K:2 * K]
        send_sems, recv_sems = refs[2 * K:]
        x, y, c, chips = _mesh_position()
        copies = []
        for k in range(K):
            for j, chip in enumerate(chips):
                got = _half_rows(outs[k], 2 * chip[0] + chip[1], c)
                cp = pltpu.make_async_remote_copy(
                    src_ref=got, dst_ref=got, send_sem=send_sems.at[k * per + j], recv_sem=recv_sems.at[k * per + j],
                    device_id=(x, y, 1 - c), device_id_type=MESH)
                cp.start()
                copies.append(cp)
        for cp in copies:
            cp.wait()

    return pl.pallas_call(
        body, name=name,
        in_specs=[ANY] * K, out_specs=[ANY] * K,
        out_shape=[jax.ShapeDtypeStruct(a.shape, a.dtype) for a in bufs],
        input_output_aliases={k: k for k in range(K)},
        scratch_shapes=[pltpu.SemaphoreType.DMA((K * per,)), pltpu.SemaphoreType.DMA((K * per,))],
    )(*bufs)


def _chip_exchange_copies(srcs, lands, send_sems, recv_sems):
    x, y, c, chips = _mesh_position()
    per = N_CHIPS - 1
    return [pltpu.make_async_remote_copy(
        src_ref=srcs[k].at[2 * chip[0] + chip[1]], dst_ref=lands[k].at[j],
        send_sem=send_sems.at[k * per + j], recv_sem=recv_sems.at[k * per + j],
        device_id=(*chip, c), device_id_type=MESH) for k in range(len(srcs)) for j, chip in enumerate(chips)]


def _sibling_swap_copies(srcs, lands, send_sems, recv_sems):
    x, y, c, _ = _mesh_position()
    return [pltpu.make_async_remote_copy(
        src_ref=srcs[k].at[pl.ds(0, srcs[k].shape[0]), 1 - c], dst_ref=lands[k],
        send_sem=send_sems.at[k], recv_sem=recv_sems.at[k],
        device_id=(x, y, 1 - c), device_id_type=MESH) for k in range(len(srcs))]


def _split_copy_start(srcs, land_shapes, n_sems, copies, *, name):
    K = len(srcs)

    def body(*refs):
        for cp in copies(refs[:K], refs[K:2 * K], refs[2 * K], refs[2 * K + 1]):
            cp.start()
        refs[-1][...] = jnp.zeros_like(refs[-1])

    out = pl.pallas_call(
        body, name=name,
        in_specs=[HBM] * (2 * K),
        out_specs=[SEM, SEM] + [HBM] * (2 * K) + [pl.BlockSpec(memory_space=pltpu.VMEM)],
        out_shape=[pltpu.SemaphoreType.DMA((n_sems,))] * 2
        + [pltpu.HBM(a.shape, a.dtype) for a in srcs]
        + [pltpu.HBM(s, a.dtype) for s, a in zip(land_shapes, srcs)] + [TOKEN],
        input_output_aliases={k: 2 + k for k in range(2 * K)},
        compiler_params=SPLIT_COPY,
    )(*[_in_hbm(a) for a in srcs], *[_in_hbm(lax.empty(s, a.dtype)) for s, a in zip(land_shapes, srcs)])
    return (out[0], out[1], list(out[2:2 + K]), list(out[2 + K:2 + 2 * K])), out[-1]


def _split_copy_wait(handle, copies, after, *, name):
    send, recv, srcs, lands = handle
    K = len(srcs)

    def body(*refs):
        for cp in copies(refs[:K], refs[K:2 * K], refs[2 * K], refs[2 * K + 1]):
            cp.wait_send()
            cp.wait_recv()

    out = pl.pallas_call(
        body, name=name,
        in_specs=[HBM] * (2 * K) + [SEM, SEM, ANY], out_specs=[HBM] * (2 * K),
        out_shape=[pltpu.HBM(a.shape, a.dtype) for a in srcs] + [pltpu.HBM(a.shape, a.dtype) for a in lands],
        input_output_aliases={k: k for k in range(2 * K)},
        compiler_params=SPLIT_COPY,
    )(*srcs, *lands, send, recv, after)
    return list(out[:K]), list(out[K:])


def _join_copies(bufs, send_sems, recv_sems):
    x, y, c, _ = _mesh_position()
    copies = []
    for k, buf in enumerate(bufs):
        r2 = buf.shape[0] // 2
        mine = buf.at[pl.ds(c * r2, r2), :]
        copies.append(pltpu.make_async_remote_copy(
            src_ref=mine, dst_ref=mine, send_sem=send_sems.at[k], recv_sem=recv_sems.at[k],
            device_id=(x, y, 1 - c), device_id_type=MESH))
    return copies


def _join_start(bufs, *, name):
    K = len(bufs)

    def body(*refs):
        for cp in _join_copies(refs[:K], refs[K], refs[K + 1]):
            cp.start()
        refs[-1][...] = jnp.zeros_like(refs[-1])

    out = pl.pallas_call(
        body, name=name,
        in_specs=[HBM] * K,
        out_specs=[SEM, SEM] + [HBM] * K + [pl.BlockSpec(memory_space=pltpu.VMEM)],
        out_shape=[pltpu.SemaphoreType.DMA((K,))] * 2 + [pltpu.HBM(a.shape, a.dtype) for a in bufs] + [TOKEN],
        input_output_aliases={k: 2 + k for k in range(K)},
        compiler_params=SPLIT_COPY,
    )(*[_in_hbm(a) for a in bufs])
    return (out[0], out[1], list(out[2:2 + K])), out[-1]


def _join_wait(handle, after, *, name):
    send, recv, bufs = handle
    K = len(bufs)

    def body(*refs):
        for cp in _join_copies(refs[:K], refs[K], refs[K + 1]):
            cp.wait_send()
            cp.wait_recv()

    return pl.pallas_call(
        body, name=name,
        in_specs=[HBM] * K + [SEM, SEM, ANY], out_specs=[HBM] * K,
        out_shape=[pltpu.HBM(a.shape, a.dtype) for a in bufs],
        input_output_aliases={k: k for k in range(K)},
        compiler_params=SPLIT_COPY,
    )(*bufs, send, recv, after)


def _all_gather_slab(slab):
    m_per, n = slab.shape

    def body(x_ref, out_ref, send_sems, recv_sems, local_sem):
        x, y, c, chips = _mesh_position()
        me, sibling = (x, y, c), (x, y, 1 - c)

        def rows(px, py, pc):
            return out_ref.at[pl.ds((4 * px + 2 * py + pc) * m_per, m_per), :]

        def copy(k, block, to, src=None):
            return pltpu.make_async_remote_copy(
                src_ref=rows(*block) if src is None else src, dst_ref=rows(*block),
                send_sem=send_sems.at[k], recv_sem=recv_sems.at[k], device_id=to, device_id_type=MESH)

        mine = pltpu.make_async_copy(x_ref, rows(*me), local_sem)
        mine.start()
        first = [copy(0, me, sibling, src=x_ref)]
        first += [copy(1 + j, me, (*chip, c), src=x_ref) for j, chip in enumerate(chips)]
        for cp in first:
            cp.start()
        passed = [copy(4 + j, (*chip, c), sibling) for j, chip in enumerate(chips)]
        for j, chip in enumerate(chips):
            copy(1 + j, (*chip, c), me).wait_recv()
            passed[j].start()
        copy(0, sibling, me).wait_recv()
        for j, chip in enumerate(chips):
            copy(4 + j, (*chip, 1 - c), me).wait_recv()
        for cp in first + passed:
            cp.wait_send()
        mine.wait()

    return pl.pallas_call(
        body, name="gather_small_grads",
        out_shape=jax.ShapeDtypeStruct((N_DEV * m_per, n), slab.dtype),
        in_specs=[pl.BlockSpec(memory_space=pltpu.VMEM)],
        out_specs=pl.BlockSpec(memory_space=pltpu.VMEM),
        scratch_shapes=[pltpu.SemaphoreType.DMA((7,)), pltpu.SemaphoreType.DMA((7,)), pltpu.SemaphoreType.DMA],
    )(slab)


def _ffn_backward_weights(dhb, saved, w_in, w_out, after, tag, on_dw_out=None):
    n, gu, a = saved
    S = n.shape[0]
    ns, D, cs = w_in.shape
    F = w_out.shape[0]
    to = 512
    dw_out = _mm_tn(
        a, dhb, grid=(F // to,),
        a_spec=pl.BlockSpec((S, to), lambda j: (0, j)), b_spec=pl.BlockSpec((S, D), lambda j: (0, 0)),
        out_spec=pl.BlockSpec((to, D), lambda j: (j, 0)), out_shape=jax.ShapeDtypeStruct((F, D), BF16),
        scale=0.5, name=f"{tag}_dw_out").reshape(N_CHIPS, F // N_CHIPS, D)
    if on_dw_out is not None:
        after = on_dw_out(dw_out)
    dgu = _ffn_bwd_act(dhb, w_out, gu, after, name=f"{tag}_bwd_act")
    ti = MXU_TILE
    per_g, per_s = F // ti, cs // ti
    dw_in = _mm_tn(
        n.T, dgu, grid=(2 * F // ti,), a_is_transposed=True,
        a_spec=pl.BlockSpec((D, S), lambda j: (0, 0)),
        b_spec=pl.BlockSpec((None, S, ti), lambda j: (lax.div(j, per_g), 0, lax.rem(j, per_g))),
        out_spec=pl.BlockSpec((None, D, ti), lambda j: (lax.div(j, per_s), 0, lax.rem(j, per_s))),
        out_shape=jax.ShapeDtypeStruct((ns, D, cs), BF16), scale=1.0, name=f"{tag}_dw_in")
    return dgu, dw_in, dw_out


def kernel(x, ffn1_norm, ffn1_w_in, ffn1_w_out, mix_norm, w_in_mix, w_pool, pool_scale, w_alpha, b_alpha, gla_norm, w_out_mix, ffn2_norm, ffn2_w_in, ffn2_w_out, final_norm, loss_target, m_ffn1_norm, m_ffn1_w_in, m_ffn1_w_out, m_mix_norm, m_w_in_mix, m_w_pool, m_pool_scale, m_w_alpha, m_b_alpha, m_gla_norm, m_w_out_mix, m_ffn2_norm, m_ffn2_w_in, m_ffn2_w_out, m_final_norm, v_ffn1_norm, v_ffn1_w_in, v_ffn1_w_out, v_mix_norm, v_w_in_mix, v_w_pool, v_pool_scale, v_w_alpha, v_b_alpha, v_gla_norm, v_w_out_mix, v_ffn2_norm, v_ffn2_w_in, v_ffn2_w_out, v_final_norm):
    names = ["ffn1_norm", "ffn1_w_in", "ffn1_w_out", "mix_norm", "w_in_mix", "w_pool", "pool_scale", "w_alpha",
             "b_alpha", "gla_norm", "w_out_mix", "ffn2_norm", "ffn2_w_in", "ffn2_w_out", "final_norm"]
    weights = dict(zip(names, [ffn1_norm, ffn1_w_in, ffn1_w_out, mix_norm, w_in_mix, w_pool, pool_scale, w_alpha,
                               b_alpha, gla_norm, w_out_mix, ffn2_norm, ffn2_w_in, ffn2_w_out, final_norm]))
    moms = dict(zip(names, [m_ffn1_norm, m_ffn1_w_in, m_ffn1_w_out, m_mix_norm, m_w_in_mix, m_w_pool, m_pool_scale,
                            m_w_alpha, m_b_alpha, m_gla_norm, m_w_out_mix, m_ffn2_norm, m_ffn2_w_in, m_ffn2_w_out,
                            m_final_norm]))
    vels = dict(zip(names, [v_ffn1_norm, v_ffn1_w_in, v_ffn1_w_out, v_mix_norm, v_w_in_mix, v_w_pool, v_pool_scale,
                            v_w_alpha, v_b_alpha, v_gla_norm, v_w_out_mix, v_ffn2_norm, v_ffn2_w_in, v_ffn2_w_out,
                            v_final_norm]))
    xi, yi, ci = lax.axis_index("x"), lax.axis_index("y"), lax.axis_index("c")
    chip = 2 * xi + yi
    c_idx = jnp.reshape(ci, (1,)).astype(jnp.int32)
    sc_idx = jnp.stack([chip, ci]).astype(jnp.int32)

    def flat2d(a):
        return a.reshape(-1, a.shape[-1])

    ex = _Exchanges(sc_idx, c_idx)

    def cast(n, after):
        return _cast_into_slot(flat2d(weights[n]), sc_idx, F32 if n == "w_alpha" else BF16, after, name=f"cast_{n}")

    first, rest = _Exchanges.GATHER_GROUPS[:2], _Exchanges.GATHER_GROUPS[2:]
    tok = ex.start_gather({n: cast(n, sc_idx) for g in first for n in g}, first, name="gather_ici_start_ffn1")
    tok2 = ex.start_gather({n: cast(n, tok) for g in rest for n in g}, rest, name="gather_ici_start_rest")
    early = [flat2d(moms["w_in_mix"]), flat2d(vels["w_in_mix"])]
    small_params = dict(g1=ffn1_norm, gm=mix_norm, g2=ffn2_norm, gf=final_norm.reshape(1, D_MODEL),
                        pool_scale=pool_scale, b_alpha=b_alpha, gla_norm=gla_norm, early=early)
    loss_blk, dx, small = _forward_backward(x[0], loss_target[0], ex, tok[0, 0] + tok2[0, 0], small_params)

    outs = {}

    def update(n, g):
        w = weights[n]
        w2 = flat2d(w) if w.ndim > 1 else w.reshape(1, -1)
        go, d, nm, nv = _adamw(w2, g.reshape(w2.shape), moms[n].reshape(w2.shape), vels[n].reshape(w2.shape),
                               name=f"adamw_{n}")
        outs[n] = (go.reshape(w.shape), d.reshape(w.shape), nm.reshape(w.shape), nv.reshape(w.shape))
        return nv

    tags = _Exchanges.REDUCE_ORDER
    last = ex.finish_exchange(tags[0], after=dx)
    for prev, tag in zip(tags, tags[1:]):
        last = ex.finish_exchange(tag, after=last)
        for n, g in ex.reduced(prev, after=last).items():
            last = update(n, g)

    grads = {}
    small_names = ["ffn1_norm", "mix_norm", "ffn2_norm", "final_norm", "pool_scale", "b_alpha", "gla_norm", "w_alpha",
                   "loss"]
    small = small + [loss_blk[0:1]]
    rows = [a.size // LANES for a in small]
    slab = jnp.concatenate([a.reshape(-1, LANES) for a in small], axis=0)
    pad = -slab.shape[0] % 8
    slab = jnp.pad(slab, ((0, pad), (0, 0)))
    gathered = _all_gather_slab(slab).reshape(N_DEV, slab.shape[0], LANES)
    total = _slab_sum(gathered, name="sum_small_grads")
    off = 0
    for n, a, r in zip(small_names, small, rows):
        grads[n] = total[off:off + r].reshape(a.shape)
        off += r
    grads["w_alpha"] = lax.dynamic_slice_in_dim(grads["w_alpha"], chip * (GLA_DK_TOTAL // N_CHIPS),
                                                GLA_DK_TOTAL // N_CHIPS, axis=1)

    loss = grads.pop("loss")[0, 0]
    for n in small_names[:-1]:
        last = update(n, grads[n])
    for n, g in ex.reduced(tags[-1], after=last).items():
        update(n, g)
    return (loss, dx[None], *[outs[n][0] for n in names], *[outs[n][1] for n in names],
            *[outs[n][2] for n in names], *[outs[n][3] for n in names])


class _Exchanges:
    GATHER_GROUPS = (("ffn1_w_in",), ("ffn1_w_out",), ("w_in_mix", "w_pool", "w_alpha"), ("w_out_mix",),
                     ("ffn2_w_in",), ("ffn2_w_out",))
    REDUCE_ORDER = ("ffn2", "mix", "ffn1_out", "ffn1_in")

    def __init__(self, sc_idx, c_idx):
        self.sc_idx, self.c_idx = sc_idx, c_idx
        self._gathers, self._swaps, self._reduces, self._joins = {}, {}, {}, {}

    def start_gather(self, bufs, groups, *, name):
        handles, token = _gather_ici_start([[bufs[n] for n in g] for g in groups], name=name)
        for g, h in zip(groups, handles):
            self._gathers[g[0]] = (g, h)
        return token

    def gathered(self, first, after):
        names, handle = self._gathers.pop(first)
        got = _gather_ici_wait(handle, after, name=f"gather_ici_wait_{first}")
        return dict(zip(names, _forward_halves(got, name=f"gather_forward_{first}")))

    def begin_reduce(self, tag, full):
        g4 = [a.reshape(N_CHIPS, 2, a.shape[1] // 2, a.shape[2]) for a in full.values()]
        lands = [(a.shape[0],) + a.shape[2:] for a in g4]
        handle, token = _split_copy_start(g4, lands, len(g4), _sibling_swap_copies, name=f"swap_start_{tag}")
        self._swaps[tag] = (list(full), handle)
        return token

    def start_reduce(self, tag, after):
        names, handle = self._swaps.pop(tag)
        g4, from_sibling = _split_copy_wait(handle, _sibling_swap_copies, after, name=f"swap_wait_{tag}")
        pair = [_pair_sum(a, b, self.c_idx, name=f"pair_sum_{n}") for n, a, b in zip(names, g4, from_sibling)]
        lands = [(N_CHIPS - 1,) + a.shape[1:] for a in pair]
        handle, token = _split_copy_start(pair, lands, len(pair) * (N_CHIPS - 1), _chip_exchange_copies,
                                          name=f"exchange_start_{tag}")
        self._reduces[tag] = (names, handle)
        return token

    def finish_exchange(self, tag, after):
        names, handle = self._reduces.pop(tag)
        pair, lands = _split_copy_wait(handle, _chip_exchange_copies, after, name=f"exchange_wait_{tag}")
        halves = [_chip_sum(a, b, self.sc_idx, name=f"chip_sum_{n}") for n, a, b in zip(names, pair, lands)]
        handle, token = _join_start(halves, name=f"join_start_{tag}")
        self._joins[tag] = (names, handle)
        return token

    def reduced(self, tag, after):
        names, handle = self._joins.pop(tag)
        return dict(zip(names, _join_wait(handle, after, name=f"join_wait_{tag}")))


def _forward_backward(h0, target, ex, started, sp):
    g1, gm, g2, gf = sp["g1"], sp["gm"], sp["g2"], sp["gf"]
    pool_scale, b_alpha, gla_norm = sp["pool_scale"], sp["b_alpha"], sp["gla_norm"]
    cs_mix = D_IN // N_CHIPS

    n1 = _rms_fwd(h0, g1 + started, name="ffn1_norm")
    w1_in = ex.gathered("ffn1_w_in", after=n1)["ffn1_w_in"]
    gu1, a1 = _ffn_up(n1, w1_in, [n1.T] + sp["early"], name="ffn1_up")
    w1_out = ex.gathered("ffn1_w_out", after=a1)["ffn1_w_out"].reshape(D_FF, D_MODEL)
    h1 = _mm_nn(a1, w1_out, h0, 0.5, tm=512, tn=D_MODEL, tk=SHARD_TILE, name="ffn1_down")
    saved1 = (n1, gu1, a1)
    n_mix = _rms_fwd(h1, gm, name="mix_norm")
    gw = ex.gathered("w_in_mix", after=n_mix)
    w_mix = jnp.concatenate([gw["w_in_mix"][t] for t in range(N_CHIPS)], axis=1)
    w_mix = jnp.pad(w_mix, ((0, 0), (0, D_IN_PAD - D_IN)))[None]
    wp = gw["w_pool"].reshape(N_CHIPS, 4, POOL_GROUP_DIM // N_CHIPS, POOL_GROUP_DIM)
    wp = wp.transpose(1, 0, 2, 3).reshape(4, POOL_GROUP_DIM, POOL_GROUP_DIM)
    wa = gw["w_alpha"].transpose(1, 0, 2).reshape(GLA_GATE_RANK, GLA_DK_TOTAL)
    wa = jnp.pad(wa, ((0, LANES - GLA_GATE_RANK), (0, 0))).astype(BF16)
    u = _mm_nn(n_mix, w_mix[0], None, 1.0, tm=512, tn=D_IN_PAD, tk=D_MODEL, name="mix_in")
    y_pool = _pool_fwd(u, wp, pool_scale, name="pool_fwd")
    cat, o_gla, states = _gla_fwd(u, y_pool, wa, b_alpha, gla_norm, name="gla_fwd")
    w_omix = ex.gathered("w_out_mix", after=cat)["w_out_mix"].reshape(D_MODEL, D_MODEL)
    h2 = _mm_nn(cat, w_omix, h1, 1.0, tm=512, tn=D_MODEL, tk=1024, name="mix_out")
    n3 = _rms_fwd(h2, g2, name="ffn2_norm")
    w2_in = ex.gathered("ffn2_w_in", after=n3)["ffn2_w_in"]
    gu2, a2 = _ffn_up(n3, w2_in, [], name="ffn2_up")
    w2_out = ex.gathered("ffn2_w_out", after=a2)["ffn2_w_out"].reshape(D_FF, D_MODEL)
    h3 = _mm_nn(a2, w2_out, h2, 0.5, tm=512, tn=D_MODEL, tk=SHARD_TILE, name="ffn2_down")
    saved2 = (n3, gu2, a2)
    loss_blk, dh3, dh3b, d_gf = _final_loss(h3, gf, target, name="final_loss")

    dgu2, dw2_in, dw2_out = _ffn_backward_weights(dh3b, saved2, w2_in, w2_out, dh3b, "ffn2")
    tok = ex.begin_reduce("ffn2", {"ffn2_w_in": dw2_in, "ffn2_w_out": dw2_out})
    dh2, dh2b, d_g2 = _mm_nt_rmsbwd(dgu2, w2_in, h2, dh3, g2 + tok[0, 0], tk=COL_TILE, name="ffn2_dx")
    tok = ex.start_reduce("ffn2", after=dh2b)
    S = h0.shape[0]
    dcat = _mm_nt(dh2b, w_omix, tok, tm=512, tn=1024, name="mix_out_dx")
    dw_omix = _mm_tn(
        cat, dh2b, grid=(4,),
        a_spec=pl.BlockSpec((S, 512), lambda j: (0, j)), b_spec=pl.BlockSpec((S, D_MODEL), lambda j: (0, 0)),
        out_spec=pl.BlockSpec((512, D_MODEL), lambda j: (j, 0)),
        out_shape=jax.ShapeDtypeStruct((D_MODEL, D_MODEL), BF16), scale=1.0, name="mix_out_dw")
    dp, dw_pool, d_pscale = _pool_bwd(u, dcat, wp, pool_scale, name="pool_bwd")
    du, d_wa, d_ba, d_gn = _gla_bwd(u, o_gla, states, dcat, dp, wa, b_alpha, gla_norm, name="gla_bwd")
    du = du[None]
    tn_mix = COL_TILE
    dw_mix = _mm_tn(
        n_mix.T, du, grid=(2, D_IN_PAD // tn_mix), a_is_transposed=True,
        a_spec=pl.BlockSpec((D_MODEL // 2, S), lambda i, j: (i, 0)),
        b_spec=pl.BlockSpec((None, S, tn_mix), lambda i, j: (0, 0, j)),
        out_spec=pl.BlockSpec((D_MODEL // 2, tn_mix), lambda i, j: (i, j)),
        out_shape=jax.ShapeDtypeStruct((D_MODEL, D_IN_PAD), BF16), scale=1.0, name="mix_in_dw")
    dw_mix_s = dw_mix[:, :D_IN].reshape(D_MODEL, N_CHIPS, cs_mix).transpose(1, 0, 2)
    dw_pool_s = dw_pool.reshape(4, N_CHIPS, POOL_GROUP_DIM // N_CHIPS, POOL_GROUP_DIM).transpose(1, 0, 2, 3)
    dw_pool_s = dw_pool_s.reshape(N_CHIPS, POOL_GROUP_DIM, POOL_GROUP_DIM).astype(BF16)
    tok = ex.begin_reduce("mix", {"w_in_mix": dw_mix_s,
                                  "w_out_mix": dw_omix.reshape(N_CHIPS, D_MODEL // N_CHIPS, D_MODEL),
                                  "w_pool": dw_pool_s})
    dh1, dh1b, d_gm = _mm_nt_rmsbwd(du, w_mix, h1, dh2, gm + tok[0, 0], tk=tn_mix, name="mix_in_dx")
    tok = ex.start_reduce("mix", after=dh1b)
    mix_started = tok
    dgu1, dw1_in, _ = _ffn_backward_weights(
        dh1b, saved1, w1_in, w1_out, None, "ffn1",
        on_dw_out=lambda dw: mix_started + ex.begin_reduce("ffn1_out", {"ffn1_w_out": dw}))
    out_started = ex.start_reduce("ffn1_out", after=dgu1)
    tok = ex.start_reduce("ffn1_in", after=out_started + ex.begin_reduce("ffn1_in", {"ffn1_w_in": dw1_in}))
    dx, _, d_g1 = _mm_nt_rmsbwd(dgu1, w1_in, h0, dh1, g1 + tok[0, 0], tk=COL_TILE, name="ffn1_dx")
    small = [d_g1, d_gm, d_g2, d_gf, d_pscale, d_ba, d_gn, d_wa[:GLA_GATE_RANK]]
    return loss_blk, dx, small
```

```python
import functools

import jax
import jax.numpy as jnp
from jax import lax
from jax.experimental import pallas as pl
from jax.experimental.pallas import tpu as pltpu

F32 = jnp.float32
BF16 = jnp.bfloat16
MESH = pl.DeviceIdType.MESH

D_MODEL = 2048
D_FF = 5632
D_POOL = 1024
POOL_WINDOWS = (2, 4, 8, 16)
POOL_GROUP_DIM = 256
D_GLA = 1024
GLA_HEADS = 4
GLA_DV = 256
GLA_DK = 128
GLA_DK_TOTAL = 512
GLA_GATE_RANK = 16
GATE_LOGIT_NORMALIZER = 16.0
CHUNK = 64
D_IN = 4112
D_IN_PAD = 4224
EPS = 1e-6

ADAM_LR = 0.001
ADAM_B1 = 0.9
ADAM_B2 = 0.999
ADAM_EPS = 1e-08
ADAM_WD = 0.01
ADAM_STEP = 10

N_CHIPS = 4
N_DEV = 8
V7X_VMEM_BYTES = 64 * 1024 * 1024
LANES = 128
MXU_TILE = 256
COL_TILE = 1408
SHARD_TILE = 2816


def _cparams(semantics, vmem_mb):
    assert vmem_mb * 1024 * 1024 < V7X_VMEM_BYTES
    return pltpu.CompilerParams(dimension_semantics=semantics, vmem_limit_bytes=vmem_mb * 1024 * 1024)


def _dot_nn(a, b):
    return jnp.dot(a, b, preferred_element_type=F32)


def _dot_nt(a, b):
    return lax.dot_general(a, b, (((1,), (1,)), ((), ())), preferred_element_type=F32)


def _dot_tn(a, b):
    return lax.dot_general(a, b, (((0,), (0,)), ((), ())), preferred_element_type=F32)


def _sigmoid(x):
    return 1.0 / (1.0 + jnp.exp(-x))


def _rms_fwd(x, g, *, name):
    S, D = x.shape
    tm = 256

    def body(x_ref, g_ref, o_ref):
        xv = x_ref[...]
        r = lax.rsqrt(jnp.mean(xv * xv, axis=-1, keepdims=True) + EPS)
        o_ref[...] = (xv * r * g_ref[...]).astype(BF16)

    return pl.pallas_call(
        body, name=name, grid=(S // tm,),
        in_specs=[pl.BlockSpec((tm, D), lambda i: (i, 0)), pl.BlockSpec((1, D), lambda i: (0, 0))],
        out_specs=pl.BlockSpec((tm, D), lambda i: (i, 0)),
        out_shape=jax.ShapeDtypeStruct((S, D), BF16),
        compiler_params=_cparams(("parallel",), 32),
    )(x, g)


def _ffn_up(n, w_in, early, *, name):
    S, D = n.shape
    ns, _, cs = w_in.shape
    half = ns // 2
    F = cs * half
    tm, tn = 128, SHARD_TILE
    nb = cs // tn

    def body(n_ref, wg_ref, wu_ref, *rest):
        gu_ref, a_ref = rest[len(early):]
        nv = n_ref[...]
        g = _dot_nn(nv, wg_ref[...])
        u = _dot_nn(nv, wu_ref[...])
        gu_ref[0] = g.astype(BF16)
        gu_ref[1] = u.astype(BF16)
        a_ref[...] = (g * _sigmoid(g) * u).astype(BF16)

    return pl.pallas_call(
        body, name=name, grid=(F // tn, S // tm),
        in_specs=[
            pl.BlockSpec((tm, D), lambda j, i: (i, 0)),
            pl.BlockSpec((None, D, tn), lambda j, i: (lax.div(j, nb), 0, lax.rem(j, nb))),
            pl.BlockSpec((None, D, tn), lambda j, i: (half + lax.div(j, nb), 0, lax.rem(j, nb))),
        ] + [pl.BlockSpec(memory_space=pl.ANY)] * len(early),
        out_specs=[
            pl.BlockSpec((2, tm, tn), lambda j, i: (0, i, j)),
            pl.BlockSpec((tm, tn), lambda j, i: (i, j)),
        ],
        out_shape=[jax.ShapeDtypeStruct((2, S, F), BF16), jax.ShapeDtypeStruct((S, F), BF16)],
        compiler_params=_cparams(("parallel", "parallel"), 56),
    )(n, w_in, w_in, *early)


def _mm_nn(a, b, resid, scale, *, tm, tn, tk, name):
    S, K = a.shape
    N = b.shape[1]
    nk = K // tk

    def body(*refs):
        if resid is None:
            a_ref, b_ref, o_ref, acc_ref = refs
            r_ref = None
        else:
            a_ref, b_ref, r_ref, o_ref, acc_ref = refs
        k = pl.program_id(2)

        @pl.when(k == 0)
        def _():
            acc_ref[...] = jnp.zeros_like(acc_ref)

        acc_ref[...] += _dot_nn(a_ref[...], b_ref[...])

        @pl.when(k == nk - 1)
        def _():
            out = acc_ref[...] * scale
            if r_ref is not None:
                out = r_ref[...] + out
            o_ref[...] = out

    in_specs = [pl.BlockSpec((tm, tk), lambda i, j, k: (i, k)), pl.BlockSpec((tk, tn), lambda i, j, k: (k, j))]
    args = [a, b]
    if resid is not None:
        in_specs.append(pl.BlockSpec((tm, tn), lambda i, j, k: (i, j)))
        args.append(resid)
    return pl.pallas_call(
        body, name=name, grid=(S // tm, N // tn, nk),
        in_specs=in_specs,
        out_specs=pl.BlockSpec((tm, tn), lambda i, j, k: (i, j)),
        out_shape=jax.ShapeDtypeStruct((S, N), F32),
        scratch_shapes=[pltpu.VMEM((tm, tn), F32)],
        compiler_params=_cparams(("parallel", "parallel", "arbitrary"), 48),
    )(*args)


def _mm_nt(a, b, after, *, tm, tn, name):
    S, K = a.shape
    N = b.shape[0]

    def body(a_ref, b_ref, after_ref, o_ref):
        o_ref[...] = _dot_nt(a_ref[...], b_ref[...])

    return pl.pallas_call(
        body, name=name, grid=(N // tn, S // tm),
        in_specs=[pl.BlockSpec((tm, K), lambda j, i: (i, 0)), pl.BlockSpec((tn, K), lambda j, i: (j, 0)),
                  pl.BlockSpec(memory_space=pl.ANY)],
        out_specs=pl.BlockSpec((tm, tn), lambda j, i: (i, j)),
        out_shape=jax.ShapeDtypeStruct((S, N), F32),
        compiler_params=_cparams(("parallel", "parallel"), 48),
    )(a, b, after)


def _mm_tn(a, b, *, grid, a_spec, b_spec, out_spec, out_shape, scale, name, a_is_transposed=False):
    dot = _dot_nn if a_is_transposed else _dot_tn

    def body(a_ref, b_ref, o_ref):
        o_ref[...] = (scale * dot(a_ref[...], b_ref[...])).astype(o_ref.dtype)

    return pl.pallas_call(
        body, name=name, grid=grid, in_specs=[a_spec, b_spec], out_specs=out_spec, out_shape=out_shape,
        compiler_params=_cparams(("parallel",) * len(grid), 56),
    )(a, b)


def _ffn_bwd_act(dhb, w_out, gu, after, *, name):
    S, D = dhb.shape
    F = w_out.shape[0]
    tm, tn = 256, SHARD_TILE

    def body(dh_ref, w_ref, gu_ref, after_ref, dgu_ref):
        da = 0.5 * _dot_nt(dh_ref[...], w_ref[...])
        g = gu_ref[0].astype(F32)
        u = gu_ref[1].astype(F32)
        s = _sigmoid(g)
        dgu_ref[0] = (da * u * (s * (1.0 + g * (1.0 - s)))).astype(BF16)
        dgu_ref[1] = (da * (g * s)).astype(BF16)

    return pl.pallas_call(
        body, name=name, grid=(F // tn, S // tm),
        in_specs=[
            pl.BlockSpec((tm, D), lambda j, i: (i, 0)),
            pl.BlockSpec((tn, D), lambda j, i: (j, 0)),
            pl.BlockSpec((2, tm, tn), lambda j, i: (0, i, j)),
            pl.BlockSpec(memory_space=pl.ANY),
        ],
        out_specs=pl.BlockSpec((2, tm, tn), lambda j, i: (0, i, j)),
        out_shape=jax.ShapeDtypeStruct((2, S, F), BF16),
        compiler_params=_cparams(("parallel", "parallel"), 56),
    )(dhb, w_out, gu, after)


def _mm_nt_rmsbwd(dact, w, h_in, dh_out, g, *, tk, name):
    ng, S, fg = dact.shape
    ns, D, cs = w.shape
    assert ng * fg == ns * cs
    tm, rc = 512, 64
    kpg, kps = fg // tk, cs // tk
    nk = ng * kpg

    def body(a_ref, w_ref, h_ref, dho_ref, g_ref, dh_ref, dhb_ref, dg_ref, acc_ref):
        i = pl.program_id(0)
        k = pl.program_id(1)

        @pl.when(k == 0)
        def _():
            acc_ref[...] = jnp.zeros_like(acc_ref)

        acc_ref[...] += _dot_nt(a_ref[...], w_ref[...])

        @pl.when(jnp.logical_and(i == 0, k == 0))
        def _():
            dg_ref[...] = jnp.zeros_like(dg_ref)

        @pl.when(k == nk - 1)
        def _():
            gv = g_ref[...]

            def rows_step(c, dg):
                rows = pl.ds(pl.multiple_of(c * rc, rc), rc)
                dn = acc_ref[rows, :]
                xv = h_ref[rows, :]
                r = lax.rsqrt(jnp.mean(xv * xv, axis=-1, keepdims=True) + EPS)
                xh = xv * r
                dng = dn * gv
                dx = r * (dng - xh * jnp.mean(dng * xh, axis=-1, keepdims=True))
                out = dho_ref[rows, :] + dx
                dh_ref[rows, :] = out
                dhb_ref[rows, :] = out.astype(BF16)
                return dg + jnp.sum(dn * xh, axis=0, keepdims=True)

            dg_ref[...] += lax.fori_loop(0, tm // rc, rows_step, jnp.zeros((1, D), F32))

    return pl.pallas_call(
        body, name=name, grid=(S // tm, nk),
        in_specs=[
            pl.BlockSpec((None, tm, tk), lambda i, k: (lax.div(k, kpg), i, lax.rem(k, kpg))),
            pl.BlockSpec((None, D, tk), lambda i, k: (lax.div(k, kps), 0, lax.rem(k, kps))),
            pl.BlockSpec((tm, D), lambda i, k: (i, 0)),
            pl.BlockSpec((tm, D), lambda i, k: (i, 0)),
            pl.BlockSpec((1, D), lambda i, k: (0, 0)),
        ],
        out_specs=[
            pl.BlockSpec((tm, D), lambda i, k: (i, 0), pipeline_mode=pl.Buffered(1)),
            pl.BlockSpec((tm, D), lambda i, k: (i, 0), pipeline_mode=pl.Buffered(1)),
            pl.BlockSpec((1, D), lambda i, k: (0, 0)),
        ],
        out_shape=[jax.ShapeDtypeStruct((S, D), F32), jax.ShapeDtypeStruct((S, D), BF16),
                   jax.ShapeDtypeStruct((1, D), F32)],
        scratch_shapes=[pltpu.VMEM((tm, D), F32)],
        compiler_params=_cparams(("arbitrary", "arbitrary"), 56),
    )(dact, w, h_in, dh_out, g)


def _final_loss(h, g, target, *, name):
    S, D = h.shape
    tm = 256

    def body(h_ref, g_ref, t_ref, loss_ref, dh_ref, dhb_ref, dg_ref):
        i = pl.program_id(0)

        @pl.when(i == 0)
        def _():
            loss_ref[...] = jnp.zeros_like(loss_ref)
            dg_ref[...] = jnp.zeros_like(dg_ref)

        xv = h_ref[...]
        gv = g_ref[...]
        r = lax.rsqrt(jnp.mean(xv * xv, axis=-1, keepdims=True) + EPS)
        xh = xv * r
        e = xh * gv - t_ref[...]
        loss_ref[...] += 0.5 * jnp.sum(jnp.mean(e * e, axis=-1, keepdims=True))
        dy = e * (1.0 / D)
        dg_ref[...] += jnp.sum(dy * xh, axis=0, keepdims=True)
        dyg = dy * gv
        dx = r * (dyg - xh * jnp.mean(dyg * xh, axis=-1, keepdims=True))
        dh_ref[...] = dx
        dhb_ref[...] = dx.astype(BF16)

    return pl.pallas_call(
        body, name=name, grid=(S // tm,),
        in_specs=[pl.BlockSpec((tm, D), lambda i: (i, 0)), pl.BlockSpec((1, D), lambda i: (0, 0)),
                  pl.BlockSpec((tm, D), lambda i: (i, 0))],
        out_specs=[pl.BlockSpec((8, LANES), lambda i: (0, 0)), pl.BlockSpec((tm, D), lambda i: (i, 0)),
                   pl.BlockSpec((tm, D), lambda i: (i, 0)), pl.BlockSpec((1, D), lambda i: (0, 0))],
        out_shape=[jax.ShapeDtypeStruct((8, LANES), F32), jax.ShapeDtypeStruct((S, D), F32),
                   jax.ShapeDtypeStruct((S, D), BF16), jax.ShapeDtypeStruct((1, D), F32)],
        compiler_params=_cparams(("arbitrary",), 40),
    )(h, g, target)


POOL_HALO = 16
POOL_ROWS = 256


def _pool_window_mean_minus_token(ext, tok0, w):
    s = ext
    k = 1
    while k < w:
        s = s + pltpu.roll(s, k, 0)
        k *= 2
    win = s[POOL_HALO:, :]
    tok = tok0 + lax.broadcasted_iota(jnp.int32, (POOL_ROWS, 1), 0)
    cnt = jnp.minimum(tok + 1, w).astype(F32)
    return win / cnt - ext[POOL_HALO:, :], cnt


def _pool_fwd(u, w_pool, scale, *, name):
    S = u.shape[0]
    C = POOL_GROUP_DIM
    nsteps = S // POOL_ROWS

    def body(p_ref, w_ref, sc_ref, y_ref, xp_ref):
        xp_ref[0:POOL_HALO, :] = jnp.zeros((POOL_HALO, D_POOL), F32)
        xp_ref[POOL_HALO:, :] = p_ref[...]
        for gi, win in enumerate(POOL_WINDOWS):
            cols = slice(gi * C, (gi + 1) * C)

            def step(c, carry, cols=cols, win=win, gi=gi):
                r0 = pl.multiple_of(c * POOL_ROWS, POOL_ROWS)
                ext = xp_ref[pl.ds(r0, POOL_ROWS + POOL_HALO), cols]
                pooled, _ = _pool_window_mean_minus_token(ext, r0, win)
                y = _dot_nn(pooled.astype(BF16), w_ref[gi]) * sc_ref[:, cols]
                y_ref[pl.ds(r0, POOL_ROWS), cols] = y.astype(BF16)
                return carry

            lax.fori_loop(0, nsteps, step, 0)

    return pl.pallas_call(
        body, name=name, grid=(1,),
        in_specs=[pl.BlockSpec((S, D_POOL), lambda i: (0, 0)),
                  pl.BlockSpec((4, C, C), lambda i: (0, 0, 0)),
                  pl.BlockSpec((1, D_POOL), lambda i: (0, 0))],
        out_specs=pl.BlockSpec((S, D_POOL), lambda i: (0, 0)),
        out_shape=jax.ShapeDtypeStruct((S, D_POOL), BF16),
        scratch_shapes=[pltpu.VMEM((S + POOL_HALO, D_POOL), F32)],
        compiler_params=_cparams(("arbitrary",), 48),
    )(u, w_pool, scale)


def _pool_bwd(u, dcat, w_pool, scale, *, name):
    S = u.shape[0]
    C = POOL_GROUP_DIM
    nsteps = S // POOL_ROWS

    def body(p_ref, dy_ref, w_ref, sc_ref, dp_ref, dw_ref, dsc_ref, xp_ref, e_ref, neg_ref):
        xp_ref[0:POOL_HALO, :] = jnp.zeros((POOL_HALO, D_POOL), F32)
        xp_ref[POOL_HALO:, :] = p_ref[...]
        e_ref[S:, :] = jnp.zeros((POOL_HALO, C), F32)
        for gi, win in enumerate(POOL_WINDOWS):
            cols = slice(gi * C, (gi + 1) * C)

            def step_a(c, carry, cols=cols, win=win, gi=gi):
                dw, dsc = carry
                r0 = pl.multiple_of(c * POOL_ROWS, POOL_ROWS)
                ext = xp_ref[pl.ds(r0, POOL_ROWS + POOL_HALO), cols]
                pooled, cnt = _pool_window_mean_minus_token(ext, r0, win)
                pb = pooled.astype(BF16)
                wv = w_ref[gi]
                dy = dy_ref[pl.ds(r0, POOL_ROWS), cols]
                dsc = dsc + jnp.sum(dy * _dot_nn(pb, wv), axis=0, keepdims=True)
                dyp = (dy * sc_ref[:, cols]).astype(BF16)
                dw = dw + _dot_tn(pb, dyp)
                dpooled = _dot_nt(dyp, wv)
                e_ref[pl.ds(r0, POOL_ROWS), :] = dpooled / cnt
                neg_ref[pl.ds(r0, POOL_ROWS), :] = -dpooled
                return dw, dsc

            dw, dsc = lax.fori_loop(0, nsteps, step_a, (jnp.zeros((C, C), F32), jnp.zeros((1, C), F32)))
            dw_ref[gi] = dw
            dsc_ref[:, cols] = dsc

            def step_b(c, carry, cols=cols, win=win):
                r0 = pl.multiple_of(c * POOL_ROWS, POOL_ROWS)
                s = e_ref[pl.ds(r0, POOL_ROWS + POOL_HALO), :]
                n = POOL_ROWS + POOL_HALO
                k = 1
                while k < win:
                    s = s + pltpu.roll(s, n - k, 0)
                    k *= 2
                du = s[:POOL_ROWS, :] + neg_ref[pl.ds(r0, POOL_ROWS), :]
                dp_ref[pl.ds(r0, POOL_ROWS), cols] = du.astype(BF16)
                return carry

            lax.fori_loop(0, nsteps, step_b, 0)

    return pl.pallas_call(
        body, name=name, grid=(1,),
        in_specs=[pl.BlockSpec((S, D_POOL), lambda i: (0, 0)),
                  pl.BlockSpec((S, D_POOL), lambda i: (0, 0)),
                  pl.BlockSpec((4, C, C), lambda i: (0, 0, 0)),
                  pl.BlockSpec((1, D_POOL), lambda i: (0, 0))],
        out_specs=[pl.BlockSpec((S, D_POOL), lambda i: (0, 0)),
                   pl.BlockSpec((4, C, C), lambda i: (0, 0, 0)),
                   pl.BlockSpec((1, D_POOL), lambda i: (0, 0))],
        out_shape=[jax.ShapeDtypeStruct((S, D_POOL), BF16), jax.ShapeDtypeStruct((4, C, C), F32),
                   jax.ShapeDtypeStruct((1, D_POOL), F32)],
        scratch_shapes=[pltpu.VMEM((S + POOL_HALO, D_POOL), F32), pltpu.VMEM((S + POOL_HALO, C), F32),
                        pltpu.VMEM((S, C), F32)],
        compiler_params=_cparams(("arbitrary",), 56),
    )(u, dcat, w_pool, scale)


GLA_ROWS = 128
U_Q_BLK, U_K_BLK = 2, 3
U_V_BLK, U_G_BLK = 2, 3
U_R_BLK = 32


def _prefix_sum_rows(x):
    n = x.shape[0]
    row = lax.broadcasted_iota(jnp.int32, x.shape, 0)
    k = 1
    while k < n:
        x = x + jnp.where(row >= k, pltpu.roll(x, k, 0), 0.0)
        k *= 2
    return x


def _suffix_sum_rows(x):
    n = x.shape[0]
    row = lax.broadcasted_iota(jnp.int32, x.shape, 0)
    k = 1
    while k < n:
        x = x + jnp.where(row < n - k, pltpu.roll(x, n - k, 0), 0.0)
        k *= 2
    return x


def _log_sigmoid(z):
    return jnp.minimum(z, 0.0) - jnp.log(1.0 + jnp.exp(-jnp.abs(z)))


def _gla_chunk_terms(la_c, q_c, k_c):
    bc = _prefix_sum_rows(la_c)
    bl = jnp.sum(la_c, axis=0, keepdims=True)
    eb = jnp.exp(bc)
    enb = jnp.exp(-bc)
    etail = jnp.exp(bl - bc)
    qd = q_c * (GLA_DK ** -0.5) * eb
    ki = k_c * enb
    kt = k_c * etail
    d = jnp.exp(bl)
    return eb, enb, etail, qd, ki, kt, d


def _gla_fwd(u, y_pool, w_alpha, b_alpha, gnorm, *, name):
    S = u.shape[0]
    RB = GLA_ROWS
    ncc = RB // CHUNK
    H, DK, DV = GLA_HEADS, GLA_DK, GLA_DV

    def body(q_ref, k_ref, v_ref, go_ref, r_ref, yp_ref, wa_ref, ba_ref, gn_ref, cat_ref, o_ref, st_ref, state):
        i = pl.program_id(0)

        @pl.when(i == 0)
        def _():
            state[...] = jnp.zeros_like(state)

        cat_ref[:, :D_POOL] = yp_ref[...]
        y_ref = cat_ref.at[:, D_POOL:]

        z = _dot_nn(r_ref[...].astype(BF16), wa_ref[...]) + ba_ref[...]
        la = _log_sigmoid(z) / GATE_LOGIT_NORMALIZER
        ri = lax.broadcasted_iota(jnp.int32, (CHUNK, CHUNK), 0)
        ci = lax.broadcasted_iota(jnp.int32, (CHUNK, CHUNK), 1)
        tri = ri >= ci
        gn = gn_ref[...]
        for cc in range(ncc):
            rs = slice(cc * CHUNK, (cc + 1) * CHUNK)
            for h in range(H):
                ks = slice(h * DK, (h + 1) * DK)
                vs = slice(h * DV, (h + 1) * DV)
                _, _, _, qd, ki, kt, d = _gla_chunk_terms(la[rs, ks], q_ref[rs, ks], k_ref[rs, ks])
                qdb = qd.astype(BF16)
                vb = v_ref[rs, vs].astype(BF16)
                p = jnp.where(tri, _dot_nt(qdb, ki.astype(BF16)), 0.0)
                st = state[h]
                st_ref[cc, h] = st
                o = _dot_nn(p.astype(BF16), vb) + _dot_nt(qdb, st.astype(BF16))
                state[h] = st * d + _dot_tn(vb, kt.astype(BF16))
                o_ref[rs, vs] = o
                rinv = lax.rsqrt(jnp.mean(o * o, axis=-1, keepdims=True) + EPS)
                go = go_ref[rs, vs]
                y_ref[rs, vs] = (o * rinv * gn * (go * _sigmoid(go))).astype(BF16)

    nblk = S // RB
    return pl.pallas_call(
        body, name=name, grid=(nblk,),
        in_specs=[
            pl.BlockSpec((RB, GLA_DK_TOTAL), lambda i: (i, U_Q_BLK)),
            pl.BlockSpec((RB, GLA_DK_TOTAL), lambda i: (i, U_K_BLK)),
            pl.BlockSpec((RB, D_GLA), lambda i: (i, U_V_BLK)),
            pl.BlockSpec((RB, D_GLA), lambda i: (i, U_G_BLK)),
            pl.BlockSpec((RB, LANES), lambda i: (i, U_R_BLK)),
            pl.BlockSpec((RB, D_POOL), lambda i: (i, 0)),
            pl.BlockSpec((LANES, GLA_DK_TOTAL), lambda i: (0, 0)),
            pl.BlockSpec((1, GLA_DK_TOTAL), lambda i: (0, 0)),
            pl.BlockSpec((1, DV), lambda i: (0, 0)),
        ],
        out_specs=[
            pl.BlockSpec((RB, D_POOL + D_GLA), lambda i: (i, 0)),
            pl.BlockSpec((RB, D_GLA), lambda i: (i, 0)),
            pl.BlockSpec((ncc, H, DV, DK), lambda i: (i, 0, 0, 0)),
        ],
        out_shape=[jax.ShapeDtypeStruct((S, D_POOL + D_GLA), BF16), jax.ShapeDtypeStruct((S, D_GLA), F32),
                   jax.ShapeDtypeStruct((S // CHUNK, H, DV, DK), F32)],
        scratch_shapes=[pltpu.VMEM((H, DV, DK), F32)],
        compiler_params=_cparams(("arbitrary",), 32),
    )(u, u, u, u, u, y_pool, w_alpha, b_alpha, gnorm)


def _gla_bwd(u, o, states, dcat, dpool, w_alpha, b_alpha, gnorm, *, name):
    S = u.shape[0]
    RB = GLA_ROWS
    ncc = RB // CHUNK
    H, DK, DV = GLA_HEADS, GLA_DK, GLA_DV
    nblk = S // RB
    o_q, o_k = D_POOL, D_POOL + GLA_DK_TOTAL
    o_v, o_g, o_r = o_k + GLA_DK_TOTAL, o_k + GLA_DK_TOTAL + D_GLA, o_k + GLA_DK_TOTAL + 2 * D_GLA

    def body(q_ref, k_ref, v_ref, go_ref, r_ref, o_ref, st_ref, dy_ref, dpool_ref, wa_ref, ba_ref, gn_ref,
             du_ref, dwa_ref, dba_ref, dgn_ref, dstate, dz_ref):
        i = pl.program_id(0)

        @pl.when(i == 0)
        def _():
            dstate[...] = jnp.zeros_like(dstate)
            dwa_ref[...] = jnp.zeros_like(dwa_ref)
            dba_ref[...] = jnp.zeros_like(dba_ref)
            dgn_ref[...] = jnp.zeros_like(dgn_ref)

        du_ref[:, :o_q] = dpool_ref[...]
        dq_ref, dk_ref = du_ref.at[:, o_q:o_k], du_ref.at[:, o_k:o_v]
        dv_ref, dgo_ref, dr_ref = du_ref.at[:, o_v:o_g], du_ref.at[:, o_g:o_r], du_ref.at[:, o_r:]

        rb = r_ref[...].astype(BF16)
        wa = wa_ref[...]
        z = _dot_nn(rb, wa) + ba_ref[...]
        la = _log_sigmoid(z) / GATE_LOGIT_NORMALIZER
        ri = lax.broadcasted_iota(jnp.int32, (CHUNK, CHUNK), 0)
        ci = lax.broadcasted_iota(jnp.int32, (CHUNK, CHUNK), 1)
        tri = ri >= ci
        last_row = lax.broadcasted_iota(jnp.int32, (CHUNK, DK), 0) == CHUNK - 1
        gn = gn_ref[...]
        dgn = jnp.zeros((1, DV), F32)
        for cc in reversed(range(ncc)):
            rs = slice(cc * CHUNK, (cc + 1) * CHUNK)
            for h in range(H):
                ks = slice(h * DK, (h + 1) * DK)
                vs = slice(h * DV, (h + 1) * DV)
                eb, enb, etail, qd, ki, kt, d = _gla_chunk_terms(la[rs, ks], q_ref[rs, ks], k_ref[rs, ks])
                qdb, kib, ktb = qd.astype(BF16), ki.astype(BF16), kt.astype(BF16)
                vb = v_ref[rs, vs].astype(BF16)
                p = jnp.where(tri, _dot_nt(qdb, kib), 0.0)
                ov = o_ref[rs, vs]
                go = go_ref[rs, vs]
                dy = dy_ref[rs, vs]
                rinv = lax.rsqrt(jnp.mean(ov * ov, axis=-1, keepdims=True) + EPS)
                oh = ov * rinv
                sg = _sigmoid(go)
                dgo_ref[rs, vs] = (dy * (oh * gn) * (sg * (1.0 + go * (1.0 - sg)))).astype(BF16)
                don = dy * (go * sg)
                dgn = dgn + jnp.sum(don * oh, axis=0, keepdims=True)
                doh = don * gn
                do = rinv * (doh - oh * jnp.mean(doh * oh, axis=-1, keepdims=True))
                dob = do.astype(BF16)
                st = st_ref[cc, h]
                dst = dstate[h]
                stb, dstb = st.astype(BF16), dst.astype(BF16)
                dp = jnp.where(tri, _dot_nt(dob, vb), 0.0).astype(BF16)
                dv_ref[rs, vs] = (_dot_tn(p.astype(BF16), dob) + _dot_nt(ktb, dstb)).astype(BF16)
                dqd = _dot_nn(dp, kib) + _dot_nn(dob, stb)
                dki = _dot_tn(dp, qdb)
                dkt = _dot_nn(vb, dstb)
                dd = jnp.sum(dst * st, axis=0, keepdims=True)
                dstate[h] = dst * d + _dot_tn(dob, qdb)
                dq_ref[rs, ks] = (dqd * eb * (DK ** -0.5)).astype(BF16)
                dk_ref[rs, ks] = (dki * enb + dkt * etail).astype(BF16)
                dbl = jnp.sum(dkt * kt, axis=0, keepdims=True) + dd * d
                dbc = dqd * qd - dki * ki - dkt * kt
                dbc = dbc + jnp.where(last_row, dbl, 0.0)
                dla = _suffix_sum_rows(dbc)
                dz_ref[rs, ks] = dla * (1.0 / GATE_LOGIT_NORMALIZER) * (1.0 - _sigmoid(z[rs, ks]))
        dz = dz_ref[...]
        dzb = dz.astype(BF16)
        dr_ref[...] = _dot_nt(dzb, wa).astype(BF16)
        dwa_ref[...] += _dot_tn(rb, dzb)
        dba_ref[...] += jnp.sum(dz, axis=0, keepdims=True)
        dgn_ref[...] += dgn

    def rev(blk):
        return lambda i: (nblk - 1 - i, blk)

    return pl.pallas_call(
        body, name=name, grid=(nblk,),
        in_specs=[
            pl.BlockSpec((RB, GLA_DK_TOTAL), rev(U_Q_BLK)),
            pl.BlockSpec((RB, GLA_DK_TOTAL), rev(U_K_BLK)),
            pl.BlockSpec((RB, D_GLA), rev(U_V_BLK)),
            pl.BlockSpec((RB, D_GLA), rev(U_G_BLK)),
            pl.BlockSpec((RB, LANES), rev(U_R_BLK)),
            pl.BlockSpec((RB, D_GLA), rev(0)),
            pl.BlockSpec((ncc, H, DV, DK), lambda i: (nblk - 1 - i, 0, 0, 0)),
            pl.BlockSpec((RB, D_GLA), rev(1)),
            pl.BlockSpec((RB, D_POOL), rev(0)),
            pl.BlockSpec((LANES, GLA_DK_TOTAL), lambda i: (0, 0)),
            pl.BlockSpec((1, GLA_DK_TOTAL), lambda i: (0, 0)),
            pl.BlockSpec((1, DV), lambda i: (0, 0)),
        ],
        out_specs=[
            pl.BlockSpec((RB, D_IN_PAD), rev(0)),
            pl.BlockSpec((LANES, GLA_DK_TOTAL), lambda i: (0, 0)),
            pl.BlockSpec((1, GLA_DK_TOTAL), lambda i: (0, 0)),
            pl.BlockSpec((1, DV), lambda i: (0, 0)),
        ],
        out_shape=[
            jax.ShapeDtypeStruct((S, D_IN_PAD), BF16),
            jax.ShapeDtypeStruct((LANES, GLA_DK_TOTAL), F32), jax.ShapeDtypeStruct((1, GLA_DK_TOTAL), F32),
            jax.ShapeDtypeStruct((1, DV), F32),
        ],
        scratch_shapes=[pltpu.VMEM((H, DV, DK), F32), pltpu.VMEM((RB, GLA_DK_TOTAL), F32)],
        compiler_params=_cparams(("arbitrary",), 32),
    )(u, u, u, u, u, o, states, dcat, dpool, w_alpha, b_alpha, gnorm)


def _row_tile(rows, cols, itemsize, budget=2 * 1024 * 1024):
    if rows * cols * itemsize <= budget or rows % 16:
        return rows
    best = 16
    for t in range(16, rows + 1, 16):
        if rows % t == 0 and t * cols * itemsize <= budget:
            best = t
    return best


def _adamw(w, g, m, v, after, *, name):
    R, C = w.shape
    tr = _row_tile(R, C, 4)

    def body(w_ref, g_ref, m_ref, v_ref, after_ref, go_ref, d_ref, nm_ref, nv_ref):
        gv = g_ref[...]
        go_ref[...] = gv
        mn = ADAM_B1 * m_ref[...] + (1.0 - ADAM_B1) * gv
        vn = ADAM_B2 * v_ref[...] + (1.0 - ADAM_B2) * jnp.square(gv)
        m_hat = mn / (1.0 - ADAM_B1 ** ADAM_STEP)
        v_hat = vn / (1.0 - ADAM_B2 ** ADAM_STEP)
        d_ref[...] = -ADAM_LR * (m_hat / (jnp.sqrt(v_hat) + ADAM_EPS) + ADAM_WD * w_ref[...])
        nm_ref[...] = mn
        nv_ref[...] = vn

    spec = pl.BlockSpec((tr, C), lambda i: (i, 0))
    shp = jax.ShapeDtypeStruct((R, C), F32)
    return pl.pallas_call(
        body, name=name, grid=(R // tr,), in_specs=[spec] * 4 + [pl.BlockSpec(memory_space=pl.ANY)],
        out_specs=[spec] * 4, out_shape=[shp] * 4,
        compiler_params=_cparams(("parallel",), 48),
    )(w, g, m, v, after)


def _pair_sum(g4, recv, c_idx, *, name):
    ns, _, R2, C = g4.shape
    tr = _row_tile(R2, C, 2)

    def body(c_ref, g_ref, r_ref, o_ref):
        o_ref[...] = (g_ref[...].astype(F32) + r_ref[...].astype(F32)).astype(BF16)

    return pl.pallas_call(
        body, name=name,
        grid_spec=pltpu.PrefetchScalarGridSpec(
            num_scalar_prefetch=1, grid=(ns, R2 // tr),
            in_specs=[pl.BlockSpec((None, None, tr, C), lambda s, i, c: (s, c[0], i, 0)),
                      pl.BlockSpec((None, tr, C), lambda s, i, c: (s, i, 0))],
            out_specs=pl.BlockSpec((None, tr, C), lambda s, i, c: (s, i, 0)),
        ),
        out_shape=jax.ShapeDtypeStruct((ns, R2, C), BF16),
        compiler_params=_cparams(("parallel", "parallel"), 32),
    )(c_idx, g4, recv)


def _chip_sum(part, recv, sc_idx, *, name):
    _, R2, C = part.shape
    tr = _row_tile(R2, C, 4)
    nblk = R2 // tr

    def body(s_ref, p_ref, r_ref, o_ref):
        acc = p_ref[...].astype(F32)
        for j in range(N_CHIPS - 1):
            acc = acc + r_ref[j].astype(F32)
        o_ref[...] = acc

    return pl.pallas_call(
        body, name=name,
        grid_spec=pltpu.PrefetchScalarGridSpec(
            num_scalar_prefetch=1, grid=(nblk,),
            in_specs=[pl.BlockSpec((None, tr, C), lambda i, s: (s[0], i, 0)),
                      pl.BlockSpec((N_CHIPS - 1, tr, C), lambda i, s: (0, i, 0))],
            out_specs=pl.BlockSpec((tr, C), lambda i, s: (s[1] * nblk + i, 0)),
        ),
        out_shape=jax.ShapeDtypeStruct((2 * R2, C), F32),
        compiler_params=_cparams(("parallel",), 32),
    )(sc_idx, part, recv)


def _cast_into_slot(w, sc_idx, dtype, after, *, name):
    R, C = w.shape
    tr = _row_tile(R, C, 4)

    def body(s_ref, w_ref, after_ref, o_ref):
        o_ref[...] = w_ref[...].astype(dtype)

    return pl.pallas_call(
        body, name=name,
        grid_spec=pltpu.PrefetchScalarGridSpec(
            num_scalar_prefetch=1, grid=(R // tr,),
            in_specs=[pl.BlockSpec((tr, C), lambda i, s: (i, 0)), pl.BlockSpec(memory_space=pl.ANY)],
            out_specs=pl.BlockSpec((None, tr, C), lambda i, s: (s[0], i, 0)),
        ),
        out_shape=jax.ShapeDtypeStruct((N_CHIPS, R, C), dtype),
        compiler_params=_cparams(("parallel",), 32),
    )(sc_idx, w, after)


def _slab_sum(slabs, *, name):
    n, M, C = slabs.shape

    def body(x_ref, o_ref):
        acc = x_ref[0]
        for d in range(1, n):
            acc = acc + x_ref[d]
        o_ref[...] = acc

    return pl.pallas_call(
        body, name=name, out_shape=jax.ShapeDtypeStruct((M, C), F32),
    )(slabs)


def _mesh_position():
    x, y, c = lax.axis_index("x"), lax.axis_index("y"), lax.axis_index("c")
    other_chips = [(1 - x, y), (x, 1 - y), (1 - x, 1 - y)]
    return x, y, c, other_chips


ANY = pl.BlockSpec(memory_space=pl.ANY)


HBM = pl.BlockSpec(memory_space=pltpu.HBM)
SEM = pl.BlockSpec(memory_space=pltpu.SEMAPHORE)
SPLIT_COPY = pltpu.CompilerParams(has_side_effects=pltpu.SideEffectType.DATAFLOW_SIDE_EFFECTING)
TOKEN = jax.ShapeDtypeStruct((8, LANES), F32)


def _in_hbm(a):
    return pltpu.with_memory_space_constraint(a, pltpu.HBM)


def _half_rows(ref, slot, half):
    hr = ref.shape[1] // 2
    return ref.at[slot, pl.ds(half * hr, hr), :]


def _gather_ici_start(groups, *, name):
    flat = [b for g in groups for b in g]
    K, G = len(flat), len(groups)

    def body(*refs):
        ins, sems, token = refs[:K], refs[K:K + 2 * G], refs[-1]
        x, y, c, chips = _mesh_position()
        s = 2 * x + y
        k = 0
        for gi, g in enumerate(groups):
            for n in range(len(g)):
                own = _half_rows(ins[k], s, c)
                for j, chip in enumerate(chips):
                    pltpu.make_async_remote_copy(
                        src_ref=own, dst_ref=own, send_sem=sems[2 * gi].at[n * (N_CHIPS - 1) + j],
                        recv_sem=sems[2 * gi + 1].at[n * (N_CHIPS - 1) + j],
                        device_id=(*chip, c), device_id_type=MESH).start()
                k += 1
        token[...] = jnp.zeros_like(token)

    sem_shapes = []
    for g in groups:
        sem_shapes += [pltpu.SemaphoreType.DMA((len(g) * (N_CHIPS - 1),))] * 2
    out = pl.pallas_call(
        body, name=name,
        in_specs=[HBM] * K,
        out_specs=[SEM] * (2 * G) + [HBM] * K + [pl.BlockSpec(memory_space=pltpu.VMEM)],
        out_shape=sem_shapes + [pltpu.HBM(b.shape, b.dtype) for b in flat] + [TOKEN],
        input_output_aliases={k: 2 * G + k for k in range(K)},
        compiler_params=SPLIT_COPY,
    )(*[_in_hbm(b) for b in flat])
    handles, k = [], 2 * G
    for gi, g in enumerate(groups):
        handles.append((out[2 * gi], out[2 * gi + 1], list(out[k:k + len(g)])))
        k += len(g)
    return handles, out[-1]


def _gather_ici_wait(handle, after, *, name):
    send, recv, bufs = handle
    n = len(bufs)

    def body(*refs):
        ins, send_ref, recv_ref = refs[:n], refs[n], refs[n + 1]
        x, y, c, chips = _mesh_position()
        s = 2 * x + y
        for k in range(n):
            own = _half_rows(ins[k], s, c)
            for j, chip in enumerate(chips):
                cp = pltpu.make_async_remote_copy(
                    src_ref=own, dst_ref=_half_rows(ins[k], 2 * chip[0] + chip[1], c),
                    send_sem=send_ref.at[k * (N_CHIPS - 1) + j], recv_sem=recv_ref.at[k * (N_CHIPS - 1) + j],
                    device_id=(*chip, c), device_id_type=MESH)
                cp.wait_send()
                cp.wait_recv()

    return pl.pallas_call(
        body, name=name,
        in_specs=[HBM] * n + [SEM, SEM, ANY], out_specs=[HBM] * n,
        out_shape=[pltpu.HBM(b.shape, b.dtype) for b in bufs],
        input_output_aliases={k: k for k in range(n)},
        compiler_params=SPLIT_COPY,
    )(*bufs, send, recv, after)


def _forward_halves(bufs, *, name):
    K = len(bufs)
    per = N_CHIPS - 1

    def body(*refs):
        outs = refs[K:2 * K]
        send_sems, recv_sems = refs[2 * K:]
        x, y, c, chips = _mesh_position()
        copies = []
        for k in range(K):
            for j, chip in enumerate(chips):
                got = _half_rows(outs[k], 2 * chip[0] + chip[1], c)
                cp = pltpu.make_async_remote_copy(
                    src_ref=got, dst_ref=got, send_sem=send_sems.at[k * per + j], recv_sem=recv_sems.at[k * per + j],
                    device_id=(x, y, 1 - c), device_id_type=MESH)
                cp.start()
                copies.append(cp)
        for cp in copies:
            cp.wait()

    return pl.pallas_call(
        body, name=name,
        in_specs=[ANY] * K, out_specs=[ANY] * K,
        out_shape=[jax.ShapeDtypeStruct(a.shape, a.dtype) for a in bufs],
        input_output_aliases={k: k for k in range(K)},
        scratch_shapes=[pltpu.SemaphoreType.DMA((K * per,)), pltpu.SemaphoreType.DMA((K * per,))],
    )(*bufs)


def _chip_exchange_copies(srcs, lands, send_sems, recv_sems):
    x, y, c, chips = _mesh_position()
    per = N_CHIPS - 1
    return [pltpu.make_async_remote_copy(
        src_ref=srcs[k].at[2 * chip[0] + chip[1]], dst_ref=lands[k].at[j],
        send_sem=send_sems.at[k * per + j], recv_sem=recv_sems.at[k * per + j],
        device_id=(*chip, c), device_id_type=MESH) for k in range(len(srcs)) for j, chip in enumerate(chips)]


def _sibling_swap_copies(srcs, lands, send_sems, recv_sems):
    x, y, c, _ = _mesh_position()
    return [pltpu.make_async_remote_copy(
        src_ref=srcs[k].at[pl.ds(0, srcs[k].shape[0]), 1 - c], dst_ref=lands[k],
        send_sem=send_sems.at[k], recv_sem=recv_sems.at[k],
        device_id=(x, y, 1 - c), device_id_type=MESH) for k in range(len(srcs))]


def _split_copy_start(srcs, land_shapes, n_sems, copies, *, name):
    K = len(srcs)

    def body(*refs):
        for cp in copies(refs[:K], refs[K:2 * K], refs[2 * K], refs[2 * K + 1]):
            cp.start()
        refs[-1][...] = jnp.zeros_like(refs[-1])

    out = pl.pallas_call(
        body, name=name,
        in_specs=[HBM] * (2 * K),
        out_specs=[SEM, SEM] + [HBM] * (2 * K) + [pl.BlockSpec(memory_space=pltpu.VMEM)],
        out_shape=[pltpu.SemaphoreType.DMA((n_sems,))] * 2
        + [pltpu.HBM(a.shape, a.dtype) for a in srcs]
        + [pltpu.HBM(s, a.dtype) for s, a in zip(land_shapes, srcs)] + [TOKEN],
        input_output_aliases={k: 2 + k for k in range(2 * K)},
        compiler_params=SPLIT_COPY,
    )(*[_in_hbm(a) for a in srcs], *[_in_hbm(lax.empty(s, a.dtype)) for s, a in zip(land_shapes, srcs)])
    return (out[0], out[1], list(out[2:2 + K]), list(out[2 + K:2 + 2 * K])), out[-1]


def _split_copy_wait(handle, copies, after, *, name):
    send, recv, srcs, lands = handle
    K = len(srcs)

    def body(*refs):
        for cp in copies(refs[:K], refs[K:2 * K], refs[2 * K], refs[2 * K + 1]):
            cp.wait_send()
            cp.wait_recv()

    out = pl.pallas_call(
        body, name=name,
        in_specs=[HBM] * (2 * K) + [SEM, SEM, ANY], out_specs=[HBM] * (2 * K),
        out_shape=[pltpu.HBM(a.shape, a.dtype) for a in srcs] + [pltpu.HBM(a.shape, a.dtype) for a in lands],
        input_output_aliases={k: k for k in range(2 * K)},
        compiler_params=SPLIT_COPY,
    )(*srcs, *lands, send, recv, after)
    return list(out[:K]), list(out[K:])


def _join_copies(bufs, send_sems, recv_sems):
    x, y, c, _ = _mesh_position()
    copies = []
    for k, buf in enumerate(bufs):
        r2 = buf.shape[0] // 2
        mine = buf.at[pl.ds(c * r2, r2), :]
        copies.append(pltpu.make_async_remote_copy(
            src_ref=mine, dst_ref=mine, send_sem=send_sems.at[k], recv_sem=recv_sems.at[k],
            device_id=(x, y, 1 - c), device_id_type=MESH))
    return copies


def _join_start(bufs, *, name):
    K = len(bufs)

    def body(*refs):
        for cp in _join_copies(refs[:K], refs[K], refs[K + 1]):
            cp.start()
        refs[-1][...] = jnp.zeros_like(refs[-1])

    out = pl.pallas_call(
        body, name=name,
        in_specs=[HBM] * K,
        out_specs=[SEM, SEM] + [HBM] * K + [pl.BlockSpec(memory_space=pltpu.VMEM)],
        out_shape=[pltpu.SemaphoreType.DMA((K,))] * 2 + [pltpu.HBM(a.shape, a.dtype) for a in bufs] + [TOKEN],
        input_output_aliases={k: 2 + k for k in range(K)},
        compiler_params=SPLIT_COPY,
    )(*[_in_hbm(a) for a in bufs])
    return (out[0], out[1], list(out[2:2 + K])), out[-1]


def _join_wait(handle, after, *, name):
    send, recv, bufs = handle
    K = len(bufs)

    def body(*refs):
        for cp in _join_copies(refs[:K], refs[K], refs[K + 1]):
            cp.wait_send()
            cp.wait_recv()

    return pl.pallas_call(
        body, name=name,
        in_specs=[HBM] * K + [SEM, SEM, ANY], out_specs=[HBM] * K,
        out_shape=[pltpu.HBM(a.shape, a.dtype) for a in bufs],
        input_output_aliases={k: k for k in range(K)},
        compiler_params=SPLIT_COPY,
    )(*bufs, send, recv, after)


def _all_gather_slab(slab):
    m_per, n = slab.shape

    def body(x_ref, out_ref, send_sems, recv_sems, local_sem):
        x, y, c, chips = _mesh_position()
        me, sibling = (x, y, c), (x, y, 1 - c)

        def rows(px, py, pc):
            return out_ref.at[pl.ds((4 * px + 2 * py + pc) * m_per, m_per), :]

        def copy(k, block, to, src=None):
            return pltpu.make_async_remote_copy(
                src_ref=rows(*block) if src is None else src, dst_ref=rows(*block),
                send_sem=send_sems.at[k], recv_sem=recv_sems.at[k], device_id=to, device_id_type=MESH)

        mine = pltpu.make_async_copy(x_ref, rows(*me), local_sem)
        mine.start()
        first = [copy(0, me, sibling, src=x_ref)]
        first += [copy(1 + j, me, (*chip, c), src=x_ref) for j, chip in enumerate(chips)]
        for cp in first:
            cp.start()
        passed = [copy(4 + j, (*chip, c), sibling) for j, chip in enumerate(chips)]
        for j, chip in enumerate(chips):
            copy(1 + j, (*chip, c), me).wait_recv()
            passed[j].start()
        copy(0, sibling, me).wait_recv()
        for j, chip in enumerate(chips):
            copy(4 + j, (*chip, 1 - c), me).wait_recv()
        for cp in first + passed:
            cp.wait_send()
        mine.wait()

    return pl.pallas_call(
        body, name="gather_small_grads",
        out_shape=jax.ShapeDtypeStruct((N_DEV * m_per, n), slab.dtype),
        in_specs=[pl.BlockSpec(memory_space=pltpu.VMEM)],
        out_specs=pl.BlockSpec(memory_space=pltpu.VMEM),
        scratch_shapes=[pltpu.SemaphoreType.DMA((7,)), pltpu.SemaphoreType.DMA((7,)), pltpu.SemaphoreType.DMA],
    )(slab)


def _ffn_backward_weights(dhb, saved, w_in, w_out, after, tag, on_dw_out=None):
    n, gu, a = saved
    S = n.shape[0]
    ns, D, cs = w_in.shape
    F = w_out.shape[0]
    to = 512
    dw_out = _mm_tn(
        a, dhb, grid=(F // to,),
        a_spec=pl.BlockSpec((S, to), lambda j: (0, j)), b_spec=pl.BlockSpec((S, D), lambda j: (0, 0)),
        out_spec=pl.BlockSpec((to, D), lambda j: (j, 0)), out_shape=jax.ShapeDtypeStruct((F, D), BF16),
        scale=0.5, name=f"{tag}_dw_out").reshape(N_CHIPS, F // N_CHIPS, D)
    if on_dw_out is not None:
        after = on_dw_out(dw_out)
    dgu = _ffn_bwd_act(dhb, w_out, gu, after, name=f"{tag}_bwd_act")
    ti = MXU_TILE
    per_g, per_s = F // ti, cs // ti
    dw_in = _mm_tn(
        n.T, dgu, grid=(2 * F // ti,), a_is_transposed=True,
        a_spec=pl.BlockSpec((D, S), lambda j: (0, 0)),
        b_spec=pl.BlockSpec((None, S, ti), lambda j: (lax.div(j, per_g), 0, lax.rem(j, per_g))),
        out_spec=pl.BlockSpec((None, D, ti), lambda j: (lax.div(j, per_s), 0, lax.rem(j, per_s))),
        out_shape=jax.ShapeDtypeStruct((ns, D, cs), BF16), scale=1.0, name=f"{tag}_dw_in")
    return dgu, dw_in, dw_out


def kernel(x, ffn1_norm, ffn1_w_in, ffn1_w_out, mix_norm, w_in_mix, w_pool, pool_scale, w_alpha, b_alpha, gla_norm, w_out_mix, ffn2_norm, ffn2_w_in, ffn2_w_out, final_norm, loss_target, m_ffn1_norm, m_ffn1_w_in, m_ffn1_w_out, m_mix_norm, m_w_in_mix, m_w_pool, m_pool_scale, m_w_alpha, m_b_alpha, m_gla_norm, m_w_out_mix, m_ffn2_norm, m_ffn2_w_in, m_ffn2_w_out, m_final_norm, v_ffn1_norm, v_ffn1_w_in, v_ffn1_w_out, v_mix_norm, v_w_in_mix, v_w_pool, v_pool_scale, v_w_alpha, v_b_alpha, v_gla_norm, v_w_out_mix, v_ffn2_norm, v_ffn2_w_in, v_ffn2_w_out, v_final_norm):
    names = ["ffn1_norm", "ffn1_w_in", "ffn1_w_out", "mix_norm", "w_in_mix", "w_pool", "pool_scale", "w_alpha",
             "b_alpha", "gla_norm", "w_out_mix", "ffn2_norm", "ffn2_w_in", "ffn2_w_out", "final_norm"]
    weights = dict(zip(names, [ffn1_norm, ffn1_w_in, ffn1_w_out, mix_norm, w_in_mix, w_pool, pool_scale, w_alpha,
                               b_alpha, gla_norm, w_out_mix, ffn2_norm, ffn2_w_in, ffn2_w_out, final_norm]))
    moms = dict(zip(names, [m_ffn1_norm, m_ffn1_w_in, m_ffn1_w_out, m_mix_norm, m_w_in_mix, m_w_pool, m_pool_scale,
                            m_w_alpha, m_b_alpha, m_gla_norm, m_w_out_mix, m_ffn2_norm, m_ffn2_w_in, m_ffn2_w_out,
                            m_final_norm]))
    vels = dict(zip(names, [v_ffn1_norm, v_ffn1_w_in, v_ffn1_w_out, v_mix_norm, v_w_in_mix, v_w_pool, v_pool_scale,
                            v_w_alpha, v_b_alpha, v_gla_norm, v_w_out_mix, v_ffn2_norm, v_ffn2_w_in, v_ffn2_w_out,
                            v_final_norm]))
    xi, yi, ci = lax.axis_index("x"), lax.axis_index("y"), lax.axis_index("c")
    chip = 2 * xi + yi
    c_idx = jnp.reshape(ci, (1,)).astype(jnp.int32)
    sc_idx = jnp.stack([chip, ci]).astype(jnp.int32)

    def flat2d(a):
        return a.reshape(-1, a.shape[-1])

    ex = _Exchanges(sc_idx, c_idx)

    def cast(n, after):
        return _cast_into_slot(flat2d(weights[n]), sc_idx, F32 if n == "w_alpha" else BF16, after, name=f"cast_{n}")

    first, rest = _Exchanges.GATHER_GROUPS[:2], _Exchanges.GATHER_GROUPS[2:]
    tok = ex.start_gather({n: cast(n, sc_idx) for g in first for n in g}, first, name="gather_ici_start_ffn1")
    tok2 = ex.start_gather({n: cast(n, tok) for g in rest for n in g}, rest, name="gather_ici_start_rest")
    early = [flat2d(moms["w_in_mix"]), flat2d(vels["w_in_mix"])]
    small_params = dict(g1=ffn1_norm, gm=mix_norm, g2=ffn2_norm, gf=final_norm.reshape(1, D_MODEL),
                        pool_scale=pool_scale, b_alpha=b_alpha, gla_norm=gla_norm, early=early)
    loss_blk, dx, small = _forward_backward(x[0], loss_target[0], ex, tok[0, 0] + tok2[0, 0], small_params)

    outs = {}

    def update(n, g, after):
        w = weights[n]
        w2 = flat2d(w) if w.ndim > 1 else w.reshape(1, -1)
        go, d, nm, nv = _adamw(w2, g.reshape(w2.shape), moms[n].reshape(w2.shape), vels[n].reshape(w2.shape), after,
                               name=f"adamw_{n}")
        outs[n] = (go.reshape(w.shape), d.reshape(w.shape), nm.reshape(w.shape), nv.reshape(w.shape))
        return nv

    tags = _Exchanges.REDUCE_ORDER
    last = ex.finish_exchange(tags[0], after=dx)
    for prev, tag in zip(tags, tags[1:]):
        last = ex.finish_exchange(tag, after=last)
        for n, g in ex.reduced(prev, after=last).items():
            last = update(n, g, last)

    grads = {}
    small_names = ["ffn1_norm", "mix_norm", "ffn2_norm", "final_norm", "pool_scale", "b_alpha", "gla_norm", "w_alpha",
                   "loss"]
    small = small + [loss_blk[0:1]]
    rows = [a.size // LANES for a in small]
    slab = jnp.concatenate([a.reshape(-1, LANES) for a in small], axis=0)
    pad = -slab.shape[0] % 8
    slab = jnp.pad(slab, ((0, pad), (0, 0)))
    gathered = _all_gather_slab(slab).reshape(N_DEV, slab.shape[0], LANES)
    total = _slab_sum(gathered, name="sum_small_grads")
    off = 0
    for n, a, r in zip(small_names, small, rows):
        grads[n] = total[off:off + r].reshape(a.shape)
        off += r
    grads["w_alpha"] = lax.dynamic_slice_in_dim(grads["w_alpha"], chip * (GLA_DK_TOTAL // N_CHIPS),
                                                GLA_DK_TOTAL // N_CHIPS, axis=1)

    loss = grads.pop("loss")[0, 0]
    for n in small_names[:-1]:
        last = update(n, grads[n], last)
    for n, g in ex.reduced(tags[-1], after=last).items():
        last = update(n, g, last)
    return (loss, dx[None], *[outs[n][0] for n in names], *[outs[n][1] for n in names],
            *[outs[n][2] for n in names], *[outs[n][3] for n in names])


class _Exchanges:
    GATHER_GROUPS = (("ffn1_w_in",), ("ffn1_w_out",), ("w_in_mix", "w_pool", "w_alpha"), ("w_out_mix",),
                     ("ffn2_w_in",), ("ffn2_w_out",))
    REDUCE_ORDER = ("ffn2", "mix", "ffn1_out", "ffn1_in")

    def __init__(self, sc_idx, c_idx):
        self.sc_idx, self.c_idx = sc_idx, c_idx
        self._gathers, self._swaps, self._reduces, self._joins = {}, {}, {}, {}

    def start_gather(self, bufs, groups, *, name):
        handles, token = _gather_ici_start([[bufs[n] for n in g] for g in groups], name=name)
        for g, h in zip(groups, handles):
            self._gathers[g[0]] = (g, h)
        return token

    def gathered(self, first, after):
        names, handle = self._gathers.pop(first)
        got = _gather_ici_wait(handle, after, name=f"gather_ici_wait_{first}")
        return dict(zip(names, _forward_halves(got, name=f"gather_forward_{first}")))

    def begin_reduce(self, tag, full):
        g4 = [a.reshape(N_CHIPS, 2, a.shape[1] // 2, a.shape[2]) for a in full.values()]
        lands = [(a.shape[0],) + a.shape[2:] for a in g4]
        handle, token = _split_copy_start(g4, lands, len(g4), _sibling_swap_copies, name=f"swap_start_{tag}")
        self._swaps[tag] = (list(full), handle)
        return token

    def start_reduce(self, tag, after):
        names, handle = self._swaps.pop(tag)
        g4, from_sibling = _split_copy_wait(handle, _sibling_swap_copies, after, name=f"swap_wait_{tag}")
        pair = [_pair_sum(a, b, self.c_idx, name=f"pair_sum_{n}") for n, a, b in zip(names, g4, from_sibling)]
        lands = [(N_CHIPS - 1,) + a.shape[1:] for a in pair]
        handle, token = _split_copy_start(pair, lands, len(pair) * (N_CHIPS - 1), _chip_exchange_copies,
                                          name=f"exchange_start_{tag}")
        self._reduces[tag] = (names, handle)
        return token

    def finish_exchange(self, tag, after):
        names, handle = self._reduces.pop(tag)
        pair, lands = _split_copy_wait(handle, _chip_exchange_copies, after, name=f"exchange_wait_{tag}")
        halves = [_chip_sum(a, b, self.sc_idx, name=f"chip_sum_{n}") for n, a, b in zip(names, pair, lands)]
        handle, token = _join_start(halves, name=f"join_start_{tag}")
        self._joins[tag] = (names, handle)
        return token

    def reduced(self, tag, after):
        names, handle = self._joins.pop(tag)
        return dict(zip(names, _join_wait(handle, after, name=f"join_wait_{tag}")))


def _forward_backward(h0, target, ex, started, sp):
    g1, gm, g2, gf = sp["g1"], sp["gm"], sp["g2"], sp["gf"]
    pool_scale, b_alpha, gla_norm = sp["pool_scale"], sp["b_alpha"], sp["gla_norm"]
    cs_mix = D_IN // N_CHIPS

    n1 = _rms_fwd(h0, g1 + started, name="ffn1_norm")
    w1_in = ex.gathered("ffn1_w_in", after=n1)["ffn1_w_in"]
    gu1, a1 = _ffn_up(n1, w1_in, [n1.T] + sp["early"], name="ffn1_up")
    w1_out = ex.gathered("ffn1_w_out", after=a1)["ffn1_w_out"].reshape(D_FF, D_MODEL)
    h1 = _mm_nn(a1, w1_out, h0, 0.5, tm=512, tn=D_MODEL, tk=SHARD_TILE, name="ffn1_down")
    saved1 = (n1, gu1, a1)
    n_mix = _rms_fwd(h1, gm, name="mix_norm")
    gw = ex.gathered("w_in_mix", after=n_mix)
    w_mix = jnp.concatenate([gw["w_in_mix"][t] for t in range(N_CHIPS)], axis=1)
    w_mix = jnp.pad(w_mix, ((0, 0), (0, D_IN_PAD - D_IN)))[None]
    wp = gw["w_pool"].reshape(N_CHIPS, 4, POOL_GROUP_DIM // N_CHIPS, POOL_GROUP_DIM)
    wp = wp.transpose(1, 0, 2, 3).reshape(4, POOL_GROUP_DIM, POOL_GROUP_DIM)
    wa = gw["w_alpha"].transpose(1, 0, 2).reshape(GLA_GATE_RANK, GLA_DK_TOTAL)
    wa = jnp.pad(wa, ((0, LANES - GLA_GATE_RANK), (0, 0))).astype(BF16)
    u = _mm_nn(n_mix, w_mix[0], None, 1.0, tm=512, tn=D_IN_PAD, tk=D_MODEL, name="mix_in")
    y_pool = _pool_fwd(u, wp, pool_scale, name="pool_fwd")
    cat, o_gla, states = _gla_fwd(u, y_pool, wa, b_alpha, gla_norm, name="gla_fwd")
    w_omix = ex.gathered("w_out_mix", after=cat)["w_out_mix"].reshape(D_MODEL, D_MODEL)
    h2 = _mm_nn(cat, w_omix, h1, 1.0, tm=512, tn=D_MODEL, tk=1024, name="mix_out")
    n3 = _rms_fwd(h2, g2, name="ffn2_norm")
    w2_in = ex.gathered("ffn2_w_in", after=n3)["ffn2_w_in"]
    gu2, a2 = _ffn_up(n3, w2_in, [], name="ffn2_up")
    w2_out = ex.gathered("ffn2_w_out", after=a2)["ffn2_w_out"].reshape(D_FF, D_MODEL)
    h3 = _mm_nn(a2, w2_out, h2, 0.5, tm=512, tn=D_MODEL, tk=SHARD_TILE, name="ffn2_down")
    saved2 = (n3, gu2, a2)
    loss_blk, dh3, dh3b, d_gf = _final_loss(h3, gf, target, name="final_loss")

    dgu2, dw2_in, dw2_out = _ffn_backward_weights(dh3b, saved2, w2_in, w2_out, dh3b, "ffn2")
    tok = ex.begin_reduce("ffn2", {"ffn2_w_in": dw2_in, "ffn2_w_out": dw2_out})
    dh2, dh2b, d_g2 = _mm_nt_rmsbwd(dgu2, w2_in, h2, dh3, g2 + tok[0, 0], tk=SHARD_TILE, name="ffn2_dx")
    tok = ex.start_reduce("ffn2", after=dh2b)
    S = h0.shape[0]
    dcat = _mm_nt(dh2b, w_omix, tok, tm=512, tn=1024, name="mix_out_dx")
    dw_omix = _mm_tn(
        cat, dh2b, grid=(4,),
        a_spec=pl.BlockSpec((S, 512), lambda j: (0, j)), b_spec=pl.BlockSpec((S, D_MODEL), lambda j: (0, 0)),
        out_spec=pl.BlockSpec((512, D_MODEL), lambda j: (j, 0)),
        out_shape=jax.ShapeDtypeStruct((D_MODEL, D_MODEL), BF16), scale=1.0, name="mix_out_dw")
    dp, dw_pool, d_pscale = _pool_bwd(u, dcat, wp, pool_scale, name="pool_bwd")
    du, d_wa, d_ba, d_gn = _gla_bwd(u, o_gla, states, dcat, dp, wa, b_alpha, gla_norm, name="gla_bwd")
    du = du[None]
    tn_mix = COL_TILE
    dw_mix = _mm_tn(
        n_mix.T, du, grid=(2, D_IN_PAD // tn_mix), a_is_transposed=True,
        a_spec=pl.BlockSpec((D_MODEL // 2, S), lambda i, j: (i, 0)),
        b_spec=pl.BlockSpec((None, S, tn_mix), lambda i, j: (0, 0, j)),
        out_spec=pl.BlockSpec((D_MODEL // 2, tn_mix), lambda i, j: (i, j)),
        out_shape=jax.ShapeDtypeStruct((D_MODEL, D_IN_PAD), BF16), scale=1.0, name="mix_in_dw")
    dw_mix_s = dw_mix[:, :D_IN].reshape(D_MODEL, N_CHIPS, cs_mix).transpose(1, 0, 2)
    dw_pool_s = dw_pool.reshape(4, N_CHIPS, POOL_GROUP_DIM // N_CHIPS, POOL_GROUP_DIM).transpose(1, 0, 2, 3)
    dw_pool_s = dw_pool_s.reshape(N_CHIPS, POOL_GROUP_DIM, POOL_GROUP_DIM).astype(BF16)
    tok = ex.begin_reduce("mix", {"w_in_mix": dw_mix_s,
                                  "w_out_mix": dw_omix.reshape(N_CHIPS, D_MODEL // N_CHIPS, D_MODEL),
                                  "w_pool": dw_pool_s})
    dh1, dh1b, d_gm = _mm_nt_rmsbwd(du, w_mix, h1, dh2, gm + tok[0, 0], tk=tn_mix, name="mix_in_dx")
    tok = ex.start_reduce("mix", after=dh1b)
    mix_started = tok
    dgu1, dw1_in, _ = _ffn_backward_weights(
        dh1b, saved1, w1_in, w1_out, None, "ffn1",
        on_dw_out=lambda dw: mix_started + ex.begin_reduce("ffn1_out", {"ffn1_w_out": dw}))
    out_started = ex.start_reduce("ffn1_out", after=dgu1)
    tok = ex.start_reduce("ffn1_in", after=out_started + ex.begin_reduce("ffn1_in", {"ffn1_w_in": dw1_in}))
    dx, _, d_g1 = _mm_nt_rmsbwd(dgu1, w1_in, h0, dh1, g1 + tok[0, 0], tk=SHARD_TILE, name="ffn1_dx")
    small = [d_g1, d_gm, d_g2, d_gf, d_pscale, d_ba, d_gn, d_wa[:GLA_GATE_RANK]]
    return loss_blk, dx, small
```

```python
import functools

import jax
import jax.numpy as jnp
from jax import lax
from jax.experimental import pallas as pl
from jax.experimental.pallas import tpu as pltpu

F32 = jnp.float32
BF16 = jnp.bfloat16
MESH = pl.DeviceIdType.MESH

D_MODEL = 2048
D_FF = 5632
D_POOL = 1024
POOL_WINDOWS = (2, 4, 8, 16)
POOL_GROUP_DIM = 256
D_GLA = 1024
GLA_HEADS = 4
GLA_DV = 256
GLA_DK = 128
GLA_DK_TOTAL = 512
GLA_GATE_RANK = 16
GATE_LOGIT_NORMALIZER = 16.0
CHUNK = 64
D_IN = 4112
D_IN_PAD = 4224
EPS = 1e-6

ADAM_LR = 0.001
ADAM_B1 = 0.9
ADAM_B2 = 0.999
ADAM_EPS = 1e-08
ADAM_WD = 0.01
ADAM_STEP = 10

N_CHIPS = 4
N_DEV = 8
V7X_VMEM_BYTES = 64 * 1024 * 1024
LANES = 128
MXU_TILE = 256
COL_TILE = 1408
SHARD_TILE = 2816


def _cparams(semantics, vmem_mb):
    assert vmem_mb * 1024 * 1024 < V7X_VMEM_BYTES
    return pltpu.CompilerParams(dimension_semantics=semantics, vmem_limit_bytes=vmem_mb * 1024 * 1024)


def _dot_nn(a, b):
    return jnp.dot(a, b, preferred_element_type=F32)


def _dot_nt(a, b):
    return lax.dot_general(a, b, (((1,), (1,)), ((), ())), preferred_element_type=F32)


def _dot_tn(a, b):
    return lax.dot_general(a, b, (((0,), (0,)), ((), ())), preferred_element_type=F32)


def _sigmoid(x):
    return 1.0 / (1.0 + jnp.exp(-x))


def _rms_fwd(x, g, early, *, name):
    S, D = x.shape
    tm = 256

    def body(x_ref, g_ref, *rest):
        o_ref = rest[len(early)]
        xv = x_ref[...]
        r = lax.rsqrt(jnp.mean(xv * xv, axis=-1, keepdims=True) + EPS)
        o_ref[...] = (xv * r * g_ref[...]).astype(BF16)

    return pl.pallas_call(
        body, name=name, grid=(S // tm,),
        in_specs=[pl.BlockSpec((tm, D), lambda i: (i, 0)), pl.BlockSpec((1, D), lambda i: (0, 0))]
        + [pl.BlockSpec(memory_space=pl.ANY)] * len(early),
        out_specs=pl.BlockSpec((tm, D), lambda i: (i, 0)),
        out_shape=jax.ShapeDtypeStruct((S, D), BF16),
        compiler_params=_cparams(("parallel",), 32),
    )(x, g, *early)


def _ffn_up(n, w_in, early, *, name):
    S, D = n.shape
    ns, _, cs = w_in.shape
    half = ns // 2
    F = cs * half
    tm, tn = 128, SHARD_TILE
    nb = cs // tn

    def body(n_ref, wg_ref, wu_ref, *rest):
        fa_ref, a_ref = rest[len(early):]
        nv = n_ref[...]
        g = _dot_nn(nv, wg_ref[...])
        u = _dot_nn(nv, wu_ref[...])
        s = _sigmoid(g)
        silu = g * s
        fa_ref[0] = (u * (s * (1.0 + g * (1.0 - s)))).astype(BF16)
        fa_ref[1] = silu.astype(BF16)
        a_ref[...] = (silu * u).astype(BF16)

    return pl.pallas_call(
        body, name=name, grid=(F // tn, S // tm),
        in_specs=[
            pl.BlockSpec((tm, D), lambda j, i: (i, 0)),
            pl.BlockSpec((None, D, tn), lambda j, i: (lax.div(j, nb), 0, lax.rem(j, nb))),
            pl.BlockSpec((None, D, tn), lambda j, i: (half + lax.div(j, nb), 0, lax.rem(j, nb))),
        ] + [pl.BlockSpec(memory_space=pl.ANY)] * len(early),
        out_specs=[
            pl.BlockSpec((2, tm, tn), lambda j, i: (0, i, j)),
            pl.BlockSpec((tm, tn), lambda j, i: (i, j)),
        ],
        out_shape=[jax.ShapeDtypeStruct((2, S, F), BF16), jax.ShapeDtypeStruct((S, F), BF16)],
        compiler_params=_cparams(("parallel", "parallel"), 56),
    )(n, w_in, w_in, *early)


def _mm_nn(a, b, resid, scale, *, tm, tn, tk, name):
    S, K = a.shape
    N = b.shape[1]
    nk = K // tk

    def body(*refs):
        if resid is None:
            a_ref, b_ref, o_ref, acc_ref = refs
            r_ref = None
        else:
            a_ref, b_ref, r_ref, o_ref, acc_ref = refs
        k = pl.program_id(2)

        @pl.when(k == 0)
        def _():
            acc_ref[...] = jnp.zeros_like(acc_ref)

        acc_ref[...] += _dot_nn(a_ref[...], b_ref[...])

        @pl.when(k == nk - 1)
        def _():
            out = acc_ref[...] * scale
            if r_ref is not None:
                out = r_ref[...] + out
            o_ref[...] = out

    in_specs = [pl.BlockSpec((tm, tk), lambda i, j, k: (i, k)), pl.BlockSpec((tk, tn), lambda i, j, k: (k, j))]
    args = [a, b]
    if resid is not None:
        in_specs.append(pl.BlockSpec((tm, tn), lambda i, j, k: (i, j)))
        args.append(resid)
    return pl.pallas_call(
        body, name=name, grid=(S // tm, N // tn, nk),
        in_specs=in_specs,
        out_specs=pl.BlockSpec((tm, tn), lambda i, j, k: (i, j)),
        out_shape=jax.ShapeDtypeStruct((S, N), F32),
        scratch_shapes=[pltpu.VMEM((tm, tn), F32)],
        compiler_params=_cparams(("parallel", "parallel", "arbitrary"), 48),
    )(*args)


def _mm_nt(a, b, after, *, tm, tn, name):
    S, K = a.shape
    N = b.shape[0]

    def body(a_ref, b_ref, after_ref, o_ref):
        o_ref[...] = _dot_nt(a_ref[...], b_ref[...])

    return pl.pallas_call(
        body, name=name, grid=(N // tn, S // tm),
        in_specs=[pl.BlockSpec((tm, K), lambda j, i: (i, 0)), pl.BlockSpec((tn, K), lambda j, i: (j, 0)),
                  pl.BlockSpec(memory_space=pl.ANY)],
        out_specs=pl.BlockSpec((tm, tn), lambda j, i: (i, j)),
        out_shape=jax.ShapeDtypeStruct((S, N), F32),
        compiler_params=_cparams(("parallel", "parallel"), 48),
    )(a, b, after)


def _mm_tn(a, b, *, grid, a_spec, b_spec, out_spec, out_shape, scale, name, a_is_transposed=False):
    dot = _dot_nn if a_is_transposed else _dot_tn

    def body(a_ref, b_ref, o_ref):
        o_ref[...] = (scale * dot(a_ref[...], b_ref[...])).astype(o_ref.dtype)

    return pl.pallas_call(
        body, name=name, grid=grid, in_specs=[a_spec, b_spec], out_specs=out_spec, out_shape=out_shape,
        compiler_params=_cparams(("parallel",) * len(grid), 56),
    )(a, b)


def _ffn_bwd_act(dhb, w_out, fa, after, *, name):
    S, D = dhb.shape
    F = w_out.shape[0]
    tm, tn = 256, SHARD_TILE

    def body(dh_ref, w_ref, fa_ref, after_ref, dgu_ref):
        da = 0.5 * _dot_nt(dh_ref[...], w_ref[...])
        dgu_ref[0] = (da * fa_ref[0].astype(F32)).astype(BF16)
        dgu_ref[1] = (da * fa_ref[1].astype(F32)).astype(BF16)

    return pl.pallas_call(
        body, name=name, grid=(F // tn, S // tm),
        in_specs=[
            pl.BlockSpec((tm, D), lambda j, i: (i, 0)),
            pl.BlockSpec((tn, D), lambda j, i: (j, 0)),
            pl.BlockSpec((2, tm, tn), lambda j, i: (0, i, j)),
            pl.BlockSpec(memory_space=pl.ANY),
        ],
        out_specs=pl.BlockSpec((2, tm, tn), lambda j, i: (0, i, j)),
        out_shape=jax.ShapeDtypeStruct((2, S, F), BF16),
        compiler_params=_cparams(("parallel", "parallel"), 56),
    )(dhb, w_out, fa, after)


def _mm_nt_rmsbwd(dact, w, h_in, dh_out, g, *, tk, name):
    ng, S, fg = dact.shape
    ns, D, cs = w.shape
    assert ng * fg == ns * cs
    tm, rc = 512, 64
    kpg, kps = fg // tk, cs // tk
    nk = ng * kpg

    def body(a_ref, w_ref, h_ref, dho_ref, g_ref, dh_ref, dhb_ref, dg_ref, acc_ref):
        i = pl.program_id(0)
        k = pl.program_id(1)

        @pl.when(k == 0)
        def _():
            acc_ref[...] = jnp.zeros_like(acc_ref)

        acc_ref[...] += _dot_nt(a_ref[...], w_ref[...])

        @pl.when(jnp.logical_and(i == 0, k == 0))
        def _():
            dg_ref[...] = jnp.zeros_like(dg_ref)

        @pl.when(k == nk - 1)
        def _():
            gv = g_ref[...]

            def rows_step(c, dg):
                rows = pl.ds(pl.multiple_of(c * rc, rc), rc)
                dn = acc_ref[rows, :]
                xv = h_ref[rows, :]
                r = lax.rsqrt(jnp.mean(xv * xv, axis=-1, keepdims=True) + EPS)
                xh = xv * r
                dng = dn * gv
                dx = r * (dng - xh * jnp.mean(dng * xh, axis=-1, keepdims=True))
                out = dho_ref[rows, :] + dx
                dh_ref[rows, :] = out
                dhb_ref[rows, :] = out.astype(BF16)
                return dg + jnp.sum(dn * xh, axis=0, keepdims=True)

            dg_ref[...] += lax.fori_loop(0, tm // rc, rows_step, jnp.zeros((1, D), F32))

    return pl.pallas_call(
        body, name=name, grid=(S // tm, nk),
        in_specs=[
            pl.BlockSpec((None, tm, tk), lambda i, k: (lax.div(k, kpg), i, lax.rem(k, kpg))),
            pl.BlockSpec((None, D, tk), lambda i, k: (lax.div(k, kps), 0, lax.rem(k, kps))),
            pl.BlockSpec((tm, D), lambda i, k: (i, 0)),
            pl.BlockSpec((tm, D), lambda i, k: (i, 0)),
            pl.BlockSpec((1, D), lambda i, k: (0, 0)),
        ],
        out_specs=[
            pl.BlockSpec((tm, D), lambda i, k: (i, 0), pipeline_mode=pl.Buffered(1)),
            pl.BlockSpec((tm, D), lambda i, k: (i, 0), pipeline_mode=pl.Buffered(1)),
            pl.BlockSpec((1, D), lambda i, k: (0, 0)),
        ],
        out_shape=[jax.ShapeDtypeStruct((S, D), F32), jax.ShapeDtypeStruct((S, D), BF16),
                   jax.ShapeDtypeStruct((1, D), F32)],
        scratch_shapes=[pltpu.VMEM((tm, D), F32)],
        compiler_params=_cparams(("arbitrary", "arbitrary"), 56),
    )(dact, w, h_in, dh_out, g)


def _final_loss(h, g, target, *, name):
    S, D = h.shape
    tm = 256

    def body(h_ref, g_ref, t_ref, loss_ref, dh_ref, dhb_ref, dg_ref):
        i = pl.program_id(0)

        @pl.when(i == 0)
        def _():
            loss_ref[...] = jnp.zeros_like(loss_ref)
            dg_ref[...] = jnp.zeros_like(dg_ref)

        xv = h_ref[...]
        gv = g_ref[...]
        r = lax.rsqrt(jnp.mean(xv * xv, axis=-1, keepdims=True) + EPS)
        xh = xv * r
        e = xh * gv - t_ref[...]
        loss_ref[...] += 0.5 * jnp.sum(jnp.mean(e * e, axis=-1, keepdims=True))
        dy = e * (1.0 / D)
        dg_ref[...] += jnp.sum(dy * xh, axis=0, keepdims=True)
        dyg = dy * gv
        dx = r * (dyg - xh * jnp.mean(dyg * xh, axis=-1, keepdims=True))
        dh_ref[...] = dx
        dhb_ref[...] = dx.astype(BF16)

    return pl.pallas_call(
        body, name=name, grid=(S // tm,),
        in_specs=[pl.BlockSpec((tm, D), lambda i: (i, 0)), pl.BlockSpec((1, D), lambda i: (0, 0)),
                  pl.BlockSpec((tm, D), lambda i: (i, 0))],
        out_specs=[pl.BlockSpec((8, LANES), lambda i: (0, 0)), pl.BlockSpec((tm, D), lambda i: (i, 0)),
                   pl.BlockSpec((tm, D), lambda i: (i, 0)), pl.BlockSpec((1, D), lambda i: (0, 0))],
        out_shape=[jax.ShapeDtypeStruct((8, LANES), F32), jax.ShapeDtypeStruct((S, D), F32),
                   jax.ShapeDtypeStruct((S, D), BF16), jax.ShapeDtypeStruct((1, D), F32)],
        compiler_params=_cparams(("arbitrary",), 40),
    )(h, g, target)


POOL_HALO = 16
POOL_ROWS = 256


def _pool_window_mean_minus_token(ext, tok0, w):
    s = ext
    k = 1
    while k < w:
        s = s + pltpu.roll(s, k, 0)
        k *= 2
    win = s[POOL_HALO:, :]
    tok = tok0 + lax.broadcasted_iota(jnp.int32, (POOL_ROWS, 1), 0)
    cnt = jnp.minimum(tok + 1, w).astype(F32)
    return win / cnt - ext[POOL_HALO:, :], cnt


def _pool_fwd(u, w_pool, scale, *, name):
    S = u.shape[0]
    C = POOL_GROUP_DIM
    nsteps = S // POOL_ROWS

    def body(p_ref, w_ref, sc_ref, y_ref, xp_ref):
        xp_ref[0:POOL_HALO, :] = jnp.zeros((POOL_HALO, D_POOL), F32)
        xp_ref[POOL_HALO:, :] = p_ref[...]
        for gi, win in enumerate(POOL_WINDOWS):
            cols = slice(gi * C, (gi + 1) * C)

            def step(c, carry, cols=cols, win=win, gi=gi):
                r0 = pl.multiple_of(c * POOL_ROWS, POOL_ROWS)
                ext = xp_ref[pl.ds(r0, POOL_ROWS + POOL_HALO), cols]
                pooled, _ = _pool_window_mean_minus_token(ext, r0, win)
                y = _dot_nn(pooled.astype(BF16), w_ref[gi]) * sc_ref[:, cols]
                y_ref[pl.ds(r0, POOL_ROWS), cols] = y.astype(BF16)
                return carry

            lax.fori_loop(0, nsteps, step, 0)

    return pl.pallas_call(
        body, name=name, grid=(1,),
        in_specs=[pl.BlockSpec((S, D_POOL), lambda i: (0, 0)),
                  pl.BlockSpec((4, C, C), lambda i: (0, 0, 0)),
                  pl.BlockSpec((1, D_POOL), lambda i: (0, 0))],
        out_specs=pl.BlockSpec((S, D_POOL), lambda i: (0, 0)),
        out_shape=jax.ShapeDtypeStruct((S, D_POOL), BF16),
        scratch_shapes=[pltpu.VMEM((S + POOL_HALO, D_POOL), F32)],
        compiler_params=_cparams(("arbitrary",), 48),
    )(u, w_pool, scale)


def _pool_bwd(u, dcat, w_pool, scale, *, name):
    S = u.shape[0]
    C = POOL_GROUP_DIM
    nsteps = S // POOL_ROWS

    def body(p_ref, dy_ref, w_ref, sc_ref, dp_ref, dw_ref, dsc_ref, xp_ref, e_ref, neg_ref):
        xp_ref[0:POOL_HALO, :] = jnp.zeros((POOL_HALO, D_POOL), F32)
        xp_ref[POOL_HALO:, :] = p_ref[...]
        e_ref[S:, :] = jnp.zeros((POOL_HALO, C), F32)
        for gi, win in enumerate(POOL_WINDOWS):
            cols = slice(gi * C, (gi + 1) * C)

            def step_a(c, carry, cols=cols, win=win, gi=gi):
                dw, dsc = carry
                r0 = pl.multiple_of(c * POOL_ROWS, POOL_ROWS)
                ext = xp_ref[pl.ds(r0, POOL_ROWS + POOL_HALO), cols]
                pooled, cnt = _pool_window_mean_minus_token(ext, r0, win)
                pb = pooled.astype(BF16)
                wv = w_ref[gi]
                dy = dy_ref[pl.ds(r0, POOL_ROWS), cols]
                dsc = dsc + jnp.sum(dy * _dot_nn(pb, wv), axis=0, keepdims=True)
                dyp = (dy * sc_ref[:, cols]).astype(BF16)
                dw = dw + _dot_tn(pb, dyp)
                dpooled = _dot_nt(dyp, wv)
                e_ref[pl.ds(r0, POOL_ROWS), :] = dpooled / cnt
                neg_ref[pl.ds(r0, POOL_ROWS), :] = -dpooled
                return dw, dsc

            dw, dsc = lax.fori_loop(0, nsteps, step_a, (jnp.zeros((C, C), F32), jnp.zeros((1, C), F32)))
            dw_ref[gi] = dw
            dsc_ref[:, cols] = dsc

            def step_b(c, carry, cols=cols, win=win):
                r0 = pl.multiple_of(c * POOL_ROWS, POOL_ROWS)
                s = e_ref[pl.ds(r0, POOL_ROWS + POOL_HALO), :]
                n = POOL_ROWS + POOL_HALO
                k = 1
                while k < win:
                    s = s + pltpu.roll(s, n - k, 0)
                    k *= 2
                du = s[:POOL_ROWS, :] + neg_ref[pl.ds(r0, POOL_ROWS), :]
                dp_ref[pl.ds(r0, POOL_ROWS), cols] = du.astype(BF16)
                return carry

            lax.fori_loop(0, nsteps, step_b, 0)

    return pl.pallas_call(
        body, name=name, grid=(1,),
        in_specs=[pl.BlockSpec((S, D_POOL), lambda i: (0, 0)),
                  pl.BlockSpec((S, D_POOL), lambda i: (0, 0)),
                  pl.BlockSpec((4, C, C), lambda i: (0, 0, 0)),
                  pl.BlockSpec((1, D_POOL), lambda i: (0, 0))],
        out_specs=[pl.BlockSpec((S, D_POOL), lambda i: (0, 0)),
                   pl.BlockSpec((4, C, C), lambda i: (0, 0, 0)),
                   pl.BlockSpec((1, D_POOL), lambda i: (0, 0))],
        out_shape=[jax.ShapeDtypeStruct((S, D_POOL), BF16), jax.ShapeDtypeStruct((4, C, C), F32),
                   jax.ShapeDtypeStruct((1, D_POOL), F32)],
        scratch_shapes=[pltpu.VMEM((S + POOL_HALO, D_POOL), F32), pltpu.VMEM((S + POOL_HALO, C), F32),
                        pltpu.VMEM((S, C), F32)],
        compiler_params=_cparams(("arbitrary",), 56),
    )(u, dcat, w_pool, scale)


GLA_ROWS = 128
U_Q_BLK, U_K_BLK = 2, 3
U_V_BLK, U_G_BLK = 2, 3
U_R_BLK = 32


def _prefix_sum_rows(x):
    n = x.shape[0]
    row = lax.broadcasted_iota(jnp.int32, x.shape, 0)
    k = 1
    while k < n:
        x = x + jnp.where(row >= k, pltpu.roll(x, k, 0), 0.0)
        k *= 2
    return x


def _suffix_sum_rows(x):
    n = x.shape[0]
    row = lax.broadcasted_iota(jnp.int32, x.shape, 0)
    k = 1
    while k < n:
        x = x + jnp.where(row < n - k, pltpu.roll(x, n - k, 0), 0.0)
        k *= 2
    return x


def _log_sigmoid(z):
    return jnp.minimum(z, 0.0) - jnp.log(1.0 + jnp.exp(-jnp.abs(z)))


def _gla_chunk_terms(la_c, q_c, k_c):
    bc = _prefix_sum_rows(la_c)
    bl = jnp.sum(la_c, axis=0, keepdims=True)
    eb = jnp.exp(bc)
    enb = jnp.exp(-bc)
    etail = jnp.exp(bl - bc)
    qd = q_c * (GLA_DK ** -0.5) * eb
    ki = k_c * enb
    kt = k_c * etail
    d = jnp.exp(bl)
    return eb, enb, etail, qd, ki, kt, d


def _gla_fwd(u, y_pool, w_alpha, b_alpha, gnorm, *, name):
    S = u.shape[0]
    RB = GLA_ROWS
    ncc = RB // CHUNK
    H, DK, DV = GLA_HEADS, GLA_DK, GLA_DV

    def body(q_ref, k_ref, v_ref, go_ref, r_ref, yp_ref, wa_ref, ba_ref, gn_ref, cat_ref, o_ref, st_ref, state):
        i = pl.program_id(0)

        @pl.when(i == 0)
        def _():
            state[...] = jnp.zeros_like(state)

        cat_ref[:, :D_POOL] = yp_ref[...]
        y_ref = cat_ref.at[:, D_POOL:]

        z = _dot_nn(r_ref[...].astype(BF16), wa_ref[...]) + ba_ref[...]
        la = _log_sigmoid(z) / GATE_LOGIT_NORMALIZER
        ri = lax.broadcasted_iota(jnp.int32, (CHUNK, CHUNK), 0)
        ci = lax.broadcasted_iota(jnp.int32, (CHUNK, CHUNK), 1)
        tri = ri >= ci
        gn = gn_ref[...]
        for cc in range(ncc):
            rs = slice(cc * CHUNK, (cc + 1) * CHUNK)
            for h in range(H):
                ks = slice(h * DK, (h + 1) * DK)
                vs = slice(h * DV, (h + 1) * DV)
                _, _, _, qd, ki, kt, d = _gla_chunk_terms(la[rs, ks], q_ref[rs, ks], k_ref[rs, ks])
                qdb = qd.astype(BF16)
                vb = v_ref[rs, vs].astype(BF16)
                p = jnp.where(tri, _dot_nt(qdb, ki.astype(BF16)), 0.0)
                st = state[h]
                st_ref[cc, h] = st
                o = _dot_nn(p.astype(BF16), vb) + _dot_nt(qdb, st.astype(BF16))
                state[h] = st * d + _dot_tn(vb, kt.astype(BF16))
                o_ref[rs, vs] = o
                rinv = lax.rsqrt(jnp.mean(o * o, axis=-1, keepdims=True) + EPS)
                go = go_ref[rs, vs]
                y_ref[rs, vs] = (o * rinv * gn * (go * _sigmoid(go))).astype(BF16)

    nblk = S // RB
    return pl.pallas_call(
        body, name=name, grid=(nblk,),
        in_specs=[
            pl.BlockSpec((RB, GLA_DK_TOTAL), lambda i: (i, U_Q_BLK)),
            pl.BlockSpec((RB, GLA_DK_TOTAL), lambda i: (i, U_K_BLK)),
            pl.BlockSpec((RB, D_GLA), lambda i: (i, U_V_BLK)),
            pl.BlockSpec((RB, D_GLA), lambda i: (i, U_G_BLK)),
            pl.BlockSpec((RB, LANES), lambda i: (i, U_R_BLK)),
            pl.BlockSpec((RB, D_POOL), lambda i: (i, 0)),
            pl.BlockSpec((LANES, GLA_DK_TOTAL), lambda i: (0, 0)),
            pl.BlockSpec((1, GLA_DK_TOTAL), lambda i: (0, 0)),
            pl.BlockSpec((1, DV), lambda i: (0, 0)),
        ],
        out_specs=[
            pl.BlockSpec((RB, D_POOL + D_GLA), lambda i: (i, 0)),
            pl.BlockSpec((RB, D_GLA), lambda i: (i, 0)),
            pl.BlockSpec((ncc, H, DV, DK), lambda i: (i, 0, 0, 0)),
        ],
        out_shape=[jax.ShapeDtypeStruct((S, D_POOL + D_GLA), BF16), jax.ShapeDtypeStruct((S, D_GLA), F32),
                   jax.ShapeDtypeStruct((S // CHUNK, H, DV, DK), F32)],
        scratch_shapes=[pltpu.VMEM((H, DV, DK), F32)],
        compiler_params=_cparams(("arbitrary",), 32),
    )(u, u, u, u, u, y_pool, w_alpha, b_alpha, gnorm)


def _gla_bwd(u, o, states, dcat, dpool, w_alpha, b_alpha, gnorm, *, name):
    S = u.shape[0]
    RB = GLA_ROWS
    ncc = RB // CHUNK
    H, DK, DV = GLA_HEADS, GLA_DK, GLA_DV
    nblk = S // RB
    o_q, o_k = D_POOL, D_POOL + GLA_DK_TOTAL
    o_v, o_g, o_r = o_k + GLA_DK_TOTAL, o_k + GLA_DK_TOTAL + D_GLA, o_k + GLA_DK_TOTAL + 2 * D_GLA

    def body(q_ref, k_ref, v_ref, go_ref, r_ref, o_ref, st_ref, dy_ref, dpool_ref, wa_ref, ba_ref, gn_ref,
             du_ref, dwa_ref, dba_ref, dgn_ref, dstate, dz_ref):
        i = pl.program_id(0)

        @pl.when(i == 0)
        def _():
            dstate[...] = jnp.zeros_like(dstate)
            dwa_ref[...] = jnp.zeros_like(dwa_ref)
            dba_ref[...] = jnp.zeros_like(dba_ref)
            dgn_ref[...] = jnp.zeros_like(dgn_ref)

        du_ref[:, :o_q] = dpool_ref[...]
        dq_ref, dk_ref = du_ref.at[:, o_q:o_k], du_ref.at[:, o_k:o_v]
        dv_ref, dgo_ref, dr_ref = du_ref.at[:, o_v:o_g], du_ref.at[:, o_g:o_r], du_ref.at[:, o_r:]

        rb = r_ref[...].astype(BF16)
        wa = wa_ref[...]
        z = _dot_nn(rb, wa) + ba_ref[...]
        la = _log_sigmoid(z) / GATE_LOGIT_NORMALIZER
        ri = lax.broadcasted_iota(jnp.int32, (CHUNK, CHUNK), 0)
        ci = lax.broadcasted_iota(jnp.int32, (CHUNK, CHUNK), 1)
        tri = ri >= ci
        last_row = lax.broadcasted_iota(jnp.int32, (CHUNK, DK), 0) == CHUNK - 1
        gn = gn_ref[...]
        dgn = jnp.zeros((1, DV), F32)
        for cc in reversed(range(ncc)):
            rs = slice(cc * CHUNK, (cc + 1) * CHUNK)
            for h in range(H):
                ks = slice(h * DK, (h + 1) * DK)
                vs = slice(h * DV, (h + 1) * DV)
                eb, enb, etail, qd, ki, kt, d = _gla_chunk_terms(la[rs, ks], q_ref[rs, ks], k_ref[rs, ks])
                qdb, kib, ktb = qd.astype(BF16), ki.astype(BF16), kt.astype(BF16)
                vb = v_ref[rs, vs].astype(BF16)
                p = jnp.where(tri, _dot_nt(qdb, kib), 0.0)
                ov = o_ref[rs, vs]
                go = go_ref[rs, vs]
                dy = dy_ref[rs, vs]
                rinv = lax.rsqrt(jnp.mean(ov * ov, axis=-1, keepdims=True) + EPS)
                oh = ov * rinv
                sg = _sigmoid(go)
                dgo_ref[rs, vs] = (dy * (oh * gn) * (sg * (1.0 + go * (1.0 - sg)))).astype(BF16)
                don = dy * (go * sg)
                dgn = dgn + jnp.sum(don * oh, axis=0, keepdims=True)
                doh = don * gn
                do = rinv * (doh - oh * jnp.mean(doh * oh, axis=-1, keepdims=True))
                dob = do.astype(BF16)
                st = st_ref[cc, h]
                dst = dstate[h]
                stb, dstb = st.astype(BF16), dst.astype(BF16)
                dp = jnp.where(tri, _dot_nt(dob, vb), 0.0).astype(BF16)
                dv_ref[rs, vs] = (_dot_tn(p.astype(BF16), dob) + _dot_nt(ktb, dstb)).astype(BF16)
                dqd = _dot_nn(dp, kib) + _dot_nn(dob, stb)
                dki = _dot_tn(dp, qdb)
                dkt = _dot_nn(vb, dstb)
                dd = jnp.sum(dst * st, axis=0, keepdims=True)
                dstate[h] = dst * d + _dot_tn(dob, qdb)
                dq_ref[rs, ks] = (dqd * eb * (DK ** -0.5)).astype(BF16)
                dk_ref[rs, ks] = (dki * enb + dkt * etail).astype(BF16)
                dbl = jnp.sum(dkt * kt, axis=0, keepdims=True) + dd * d
                dbc = dqd * qd - dki * ki - dkt * kt
                dbc = dbc + jnp.where(last_row, dbl, 0.0)
                dla = _suffix_sum_rows(dbc)
                dz_ref[rs, ks] = dla * (1.0 / GATE_LOGIT_NORMALIZER) * (1.0 - _sigmoid(z[rs, ks]))
        dz = dz_ref[...]
        dzb = dz.astype(BF16)
        dr_ref[...] = _dot_nt(dzb, wa).astype(BF16)
        dwa_ref[...] += _dot_tn(rb, dzb)
        dba_ref[...] += jnp.sum(dz, axis=0, keepdims=True)
        dgn_ref[...] += dgn

    def rev(blk):
        return lambda i: (nblk - 1 - i, blk)

    return pl.pallas_call(
        body, name=name, grid=(nblk,),
        in_specs=[
            pl.BlockSpec((RB, GLA_DK_TOTAL), rev(U_Q_BLK)),
            pl.BlockSpec((RB, GLA_DK_TOTAL), rev(U_K_BLK)),
            pl.BlockSpec((RB, D_GLA), rev(U_V_BLK)),
            pl.BlockSpec((RB, D_GLA), rev(U_G_BLK)),
            pl.BlockSpec((RB, LANES), rev(U_R_BLK)),
            pl.BlockSpec((RB, D_GLA), rev(0)),
            pl.BlockSpec((ncc, H, DV, DK), lambda i: (nblk - 1 - i, 0, 0, 0)),
            pl.BlockSpec((RB, D_GLA), rev(1)),
            pl.BlockSpec((RB, D_POOL), rev(0)),
            pl.BlockSpec((LANES, GLA_DK_TOTAL), lambda i: (0, 0)),
            pl.BlockSpec((1, GLA_DK_TOTAL), lambda i: (0, 0)),
            pl.BlockSpec((1, DV), lambda i: (0, 0)),
        ],
        out_specs=[
            pl.BlockSpec((RB, D_IN_PAD), rev(0)),
            pl.BlockSpec((LANES, GLA_DK_TOTAL), lambda i: (0, 0)),
            pl.BlockSpec((1, GLA_DK_TOTAL), lambda i: (0, 0)),
            pl.BlockSpec((1, DV), lambda i: (0, 0)),
        ],
        out_shape=[
            jax.ShapeDtypeStruct((S, D_IN_PAD), BF16),
            jax.ShapeDtypeStruct((LANES, GLA_DK_TOTAL), F32), jax.ShapeDtypeStruct((1, GLA_DK_TOTAL), F32),
            jax.ShapeDtypeStruct((1, DV), F32),
        ],
        scratch_shapes=[pltpu.VMEM((H, DV, DK), F32), pltpu.VMEM((RB, GLA_DK_TOTAL), F32)],
        compiler_params=_cparams(("arbitrary",), 32),
    )(u, u, u, u, u, o, states, dcat, dpool, w_alpha, b_alpha, gnorm)


def _concat_shards(w4, width, *, name):
    ns, R, cs = w4.shape
    tr = 256

    def body(w_ref, o_ref):
        for t in range(ns):
            o_ref[:, t * cs:(t + 1) * cs] = w_ref[t]
        o_ref[:, ns * cs:] = jnp.zeros((tr, width - ns * cs), o_ref.dtype)

    return pl.pallas_call(
        body, name=name, grid=(R // tr,),
        in_specs=[pl.BlockSpec((ns, tr, cs), lambda i: (0, i, 0))],
        out_specs=pl.BlockSpec((tr, width), lambda i: (i, 0)),
        out_shape=jax.ShapeDtypeStruct((R, width), w4.dtype),
        compiler_params=_cparams(("parallel",), 32),
    )(w4)


def _split_shards(a, cs, *, name):
    R, width = a.shape
    tr = 256

    def body(a_ref, o_ref):
        for t in range(N_CHIPS):
            o_ref[t] = a_ref[:, t * cs:(t + 1) * cs]

    return pl.pallas_call(
        body, name=name, grid=(R // tr,),
        in_specs=[pl.BlockSpec((tr, width), lambda i: (i, 0))],
        out_specs=pl.BlockSpec((N_CHIPS, tr, cs), lambda i: (0, i, 0)),
        out_shape=jax.ShapeDtypeStruct((N_CHIPS, R, cs), a.dtype),
        compiler_params=_cparams(("parallel",), 32),
    )(a)


def _row_tile(rows, cols, itemsize, budget=2 * 1024 * 1024):
    if rows * cols * itemsize <= budget or rows % 16:
        return rows
    best = 16
    for t in range(16, rows + 1, 16):
        if rows % t == 0 and t * cols * itemsize <= budget:
            best = t
    return best


def _adamw(w, g, m, v, after, *, name):
    R, C = w.shape
    tr = _row_tile(R, C, 4)

    def body(w_ref, g_ref, m_ref, v_ref, after_ref, go_ref, d_ref, nm_ref, nv_ref):
        gv = g_ref[...]
        go_ref[...] = gv
        mn = ADAM_B1 * m_ref[...] + (1.0 - ADAM_B1) * gv
        vn = ADAM_B2 * v_ref[...] + (1.0 - ADAM_B2) * jnp.square(gv)
        m_hat = mn / (1.0 - ADAM_B1 ** ADAM_STEP)
        v_hat = vn / (1.0 - ADAM_B2 ** ADAM_STEP)
        d_ref[...] = -ADAM_LR * (m_hat / (jnp.sqrt(v_hat) + ADAM_EPS) + ADAM_WD * w_ref[...])
        nm_ref[...] = mn
        nv_ref[...] = vn

    spec = pl.BlockSpec((tr, C), lambda i: (i, 0))
    shp = jax.ShapeDtypeStruct((R, C), F32)
    return pl.pallas_call(
        body, name=name, grid=(R // tr,), in_specs=[spec] * 4 + [pl.BlockSpec(memory_space=pl.ANY)],
        out_specs=[spec] * 4, out_shape=[shp] * 4,
        compiler_params=_cparams(("parallel",), 48),
    )(w, g, m, v, after)


def _pair_sum(g4, recv, c_idx, *, name):
    ns, _, R2, C = g4.shape
    tr = _row_tile(R2, C, 2)

    def body(c_ref, g_ref, r_ref, o_ref):
        o_ref[...] = (g_ref[...].astype(F32) + r_ref[...].astype(F32)).astype(BF16)

    return pl.pallas_call(
        body, name=name,
        grid_spec=pltpu.PrefetchScalarGridSpec(
            num_scalar_prefetch=1, grid=(ns, R2 // tr),
            in_specs=[pl.BlockSpec((None, None, tr, C), lambda s, i, c: (s, c[0], i, 0)),
                      pl.BlockSpec((None, tr, C), lambda s, i, c: (s, i, 0))],
            out_specs=pl.BlockSpec((None, tr, C), lambda s, i, c: (s, i, 0)),
        ),
        out_shape=jax.ShapeDtypeStruct((ns, R2, C), BF16),
        compiler_params=_cparams(("parallel", "parallel"), 32),
    )(c_idx, g4, recv)


def _chip_sum(part, recv, sc_idx, *, name):
    _, R2, C = part.shape
    tr = _row_tile(R2, C, 4)
    nblk = R2 // tr

    def body(s_ref, p_ref, r_ref, o_ref):
        acc = p_ref[...].astype(F32)
        for j in range(N_CHIPS - 1):
            acc = acc + r_ref[j].astype(F32)
        o_ref[...] = acc

    return pl.pallas_call(
        body, name=name,
        grid_spec=pltpu.PrefetchScalarGridSpec(
            num_scalar_prefetch=1, grid=(nblk,),
            in_specs=[pl.BlockSpec((None, tr, C), lambda i, s: (s[0], i, 0)),
                      pl.BlockSpec((N_CHIPS - 1, tr, C), lambda i, s: (0, i, 0))],
            out_specs=pl.BlockSpec((tr, C), lambda i, s: (s[1] * nblk + i, 0)),
        ),
        out_shape=jax.ShapeDtypeStruct((2 * R2, C), F32),
        compiler_params=_cparams(("parallel",), 32),
    )(sc_idx, part, recv)


def _cast_into_slot(w, sc_idx, dtype, after, *, name):
    R, C = w.shape
    tr = _row_tile(R, C, 4)

    def body(s_ref, w_ref, after_ref, o_ref):
        o_ref[...] = w_ref[...].astype(dtype)

    return pl.pallas_call(
        body, name=name,
        grid_spec=pltpu.PrefetchScalarGridSpec(
            num_scalar_prefetch=1, grid=(R // tr,),
            in_specs=[pl.BlockSpec((tr, C), lambda i, s: (i, 0)), pl.BlockSpec(memory_space=pl.ANY)],
            out_specs=pl.BlockSpec((None, tr, C), lambda i, s: (s[0], i, 0)),
        ),
        out_shape=jax.ShapeDtypeStruct((N_CHIPS, R, C), dtype),
        compiler_params=_cparams(("parallel",), 32),
    )(sc_idx, w, after)


def _slab_sum(slabs, *, name):
    n, M, C = slabs.shape

    def body(x_ref, o_ref):
        acc = x_ref[0]
        for d in range(1, n):
            acc = acc + x_ref[d]
        o_ref[...] = acc

    return pl.pallas_call(
        body, name=name, out_shape=jax.ShapeDtypeStruct((M, C), F32),
    )(slabs)


def _mesh_position():
    x, y, c = lax.axis_index("x"), lax.axis_index("y"), lax.axis_index("c")
    other_chips = [(1 - x, y), (x, 1 - y), (1 - x, 1 - y)]
    return x, y, c, other_chips


ANY = pl.BlockSpec(memory_space=pl.ANY)


HBM = pl.BlockSpec(memory_space=pltpu.HBM)
SEM = pl.BlockSpec(memory_space=pltpu.SEMAPHORE)
SPLIT_COPY = pltpu.CompilerParams(has_side_effects=pltpu.SideEffectType.DATAFLOW_SIDE_EFFECTING)
TOKEN = jax.ShapeDtypeStruct((8, LANES), F32)


def _in_hbm(a):
    return pltpu.with_memory_space_constraint(a, pltpu.HBM)


def _half_rows(ref, slot, half):
    hr = ref.shape[1] // 2
    return ref.at[slot, pl.ds(half * hr, hr), :]


def _gather_ici_start(groups, *, name):
    flat = [b for g in groups for b in g]
    K, G = len(flat), len(groups)

    def body(*refs):
        ins, sems, token = refs[:K], refs[K:K + 2 * G], refs[-1]
        x, y, c, chips = _mesh_position()
        s = 2 * x + y
        k = 0
        for gi, g in enumerate(groups):
            for n in range(len(g)):
                own = _half_rows(ins[k], s, c)
                for j, chip in enumerate(chips):
                    pltpu.make_async_remote_copy(
                        src_ref=own, dst_ref=own, send_sem=sems[2 * gi].at[n * (N_CHIPS - 1) + j],
                        recv_sem=sems[2 * gi + 1].at[n * (N_CHIPS - 1) + j],
                        device_id=(*chip, c), device_id_type=MESH).start()
                k += 1
        token[...] = jnp.zeros_like(token)

    sem_shapes = []
    for g in groups:
        sem_shapes += [pltpu.SemaphoreType.DMA((len(g) * (N_CHIPS - 1),))] * 2
    out = pl.pallas_call(
        body, name=name,
        in_specs=[HBM] * K,
        out_specs=[SEM] * (2 * G) + [HBM] * K + [pl.BlockSpec(memory_space=pltpu.VMEM)],
        out_shape=sem_shapes + [pltpu.HBM(b.shape, b.dtype) for b in flat] + [TOKEN],
        input_output_aliases={k: 2 * G + k for k in range(K)},
        compiler_params=SPLIT_COPY,
    )(*[_in_hbm(b) for b in flat])
    handles, k = [], 2 * G
    for gi, g in enumerate(groups):
        handles.append((out[2 * gi], out[2 * gi + 1], list(out[k:k + len(g)])))
        k += len(g)
    return handles, out[-1]


def _gather_ici_wait(handle, after, *, name):
    send, recv, bufs = handle
    n = len(bufs)

    def body(*refs):
        ins, send_ref, recv_ref = refs[:n], refs[n], refs[n + 1]
        x, y, c, chips = _mesh_position()
        s = 2 * x + y
        for k in range(n):
            own = _half_rows(ins[k], s, c)
            for j, chip in enumerate(chips):
                cp = pltpu.make_async_remote_copy(
                    src_ref=own, dst_ref=_half_rows(ins[k], 2 * chip[0] + chip[1], c),
                    send_sem=send_ref.at[k * (N_CHIPS - 1) + j], recv_sem=recv_ref.at[k * (N_CHIPS - 1) + j],
                    device_id=(*chip, c), device_id_type=MESH)
                cp.wait_send()
                cp.wait_recv()

    return pl.pallas_call(
        body, name=name,
        in_specs=[HBM] * n + [SEM, SEM, ANY], out_specs=[HBM] * n,
        out_shape=[pltpu.HBM(b.shape, b.dtype) for b in bufs],
        input_output_aliases={k: k for k in range(n)},
        compiler_params=SPLIT_COPY,
    )(*bufs, send, recv, after)


def _forward_halves(bufs, *, name):
    K = len(bufs)
    per = N_CHIPS - 1

    def body(*refs):
        outs = refs[K:2 * K]
        send_sems, recv_sems = refs[2 * K:]
        x, y, c, chips = _mesh_position()
        copies = []
        for k in range(K):
            for j, chip in enumerate(chips):
                got = _half_rows(outs[k], 2 * chip[0] + chip[1], c)
                cp = pltpu.make_async_remote_copy(
                    src_ref=got, dst_ref=got, send_sem=send_sems.at[k * per + j], recv_sem=recv_sems.at[k * per + j],
                    device_id=(x, y, 1 - c), device_id_type=MESH)
                cp.start()
                copies.append(cp)
        for cp in copies:
            cp.wait()

    return pl.pallas_call(
        body, name=name,
        in_specs=[ANY] * K, out_specs=[ANY] * K,
        out_shape=[jax.ShapeDtypeStruct(a.shape, a.dtype) for a in bufs],
        input_output_aliases={k: k for k in range(K)},
        scratch_shapes=[pltpu.SemaphoreType.DMA((K * per,)), pltpu.SemaphoreType.DMA((K * per,))],
    )(*bufs)


def _chip_exchange_copies(srcs, lands, send_sems, recv_sems):
    x, y, c, chips = _mesh_position()
    per = N_CHIPS - 1
    return [pltpu.make_async_remote_copy(
        src_ref=srcs[k].at[2 * chip[0] + chip[1]], dst_ref=lands[k].at[j],
        send_sem=send_sems.at[k * per + j], recv_sem=recv_sems.at[k * per + j],
        device_id=(*chip, c), device_id_type=MESH) for k in range(len(srcs)) for j, chip in enumerate(chips)]


def _sibling_swap_copies(srcs, lands, send_sems, recv_sems):
    x, y, c, _ = _mesh_position()
    return [pltpu.make_async_remote_copy(
        src_ref=srcs[k].at[pl.ds(0, srcs[k].shape[0]), 1 - c], dst_ref=lands[k],
        send_sem=send_sems.at[k], recv_sem=recv_sems.at[k],
        device_id=(x, y, 1 - c), device_id_type=MESH) for k in range(len(srcs))]


def _split_copy_start(srcs, land_shapes, n_sems, copies, *, name):
    K = len(srcs)

    def body(*refs):
        for cp in copies(refs[:K], refs[K:2 * K], refs[2 * K], refs[2 * K + 1]):
            cp.start()
        refs[-1][...] = jnp.zeros_like(refs[-1])

    out = pl.pallas_call(
        body, name=name,
        in_specs=[HBM] * (2 * K),
        out_specs=[SEM, SEM] + [HBM] * (2 * K) + [pl.BlockSpec(memory_space=pltpu.VMEM)],
        out_shape=[pltpu.SemaphoreType.DMA((n_sems,))] * 2
        + [pltpu.HBM(a.shape, a.dtype) for a in srcs]
        + [pltpu.HBM(s, a.dtype) for s, a in zip(land_shapes, srcs)] + [TOKEN],
        input_output_aliases={k: 2 + k for k in range(2 * K)},
        compiler_params=SPLIT_COPY,
    )(*[_in_hbm(a) for a in srcs], *[_in_hbm(lax.empty(s, a.dtype)) for s, a in zip(land_shapes, srcs)])
    return (out[0], out[1], list(out[2:2 + K]), list(out[2 + K:2 + 2 * K])), out[-1]


def _split_copy_wait(handle, copies, after, *, name):
    send, recv, srcs, lands = handle
    K = len(srcs)

    def body(*refs):
        for cp in copies(refs[:K], refs[K:2 * K], refs[2 * K], refs[2 * K + 1]):
            cp.wait_send()
            cp.wait_recv()

    out = pl.pallas_call(
        body, name=name,
        in_specs=[HBM] * (2 * K) + [SEM, SEM, ANY], out_specs=[HBM] * (2 * K),
        out_shape=[pltpu.HBM(a.shape, a.dtype) for a in srcs] + [pltpu.HBM(a.shape, a.dtype) for a in lands],
        input_output_aliases={k: k for k in range(2 * K)},
        compiler_params=SPLIT_COPY,
    )(*srcs, *lands, send, recv, after)
    return list(out[:K]), list(out[K:])


def _join_copies(bufs, send_sems, recv_sems):
    x, y, c, _ = _mesh_position()
    copies = []
    for k, buf in enumerate(bufs):
        r2 = buf.shape[0] // 2
        mine = buf.at[pl.ds(c * r2, r2), :]
        copies.append(pltpu.make_async_remote_copy(
            src_ref=mine, dst_ref=mine, send_sem=send_sems.at[k], recv_sem=recv_sems.at[k],
            device_id=(x, y, 1 - c), device_id_type=MESH))
    return copies


def _join_start(bufs, *, name):
    K = len(bufs)

    def body(*refs):
        for cp in _join_copies(refs[:K], refs[K], refs[K + 1]):
            cp.start()
        refs[-1][...] = jnp.zeros_like(refs[-1])

    out = pl.pallas_call(
        body, name=name,
        in_specs=[HBM] * K,
        out_specs=[SEM, SEM] + [HBM] * K + [pl.BlockSpec(memory_space=pltpu.VMEM)],
        out_shape=[pltpu.SemaphoreType.DMA((K,))] * 2 + [pltpu.HBM(a.shape, a.dtype) for a in bufs] + [TOKEN],
        input_output_aliases={k: 2 + k for k in range(K)},
        compiler_params=SPLIT_COPY,
    )(*[_in_hbm(a) for a in bufs])
    return (out[0], out[1], list(out[2:2 + K])), out[-1]


def _join_wait(handle, after, *, name):
    send, recv, bufs = handle
    K = len(bufs)

    def body(*refs):
        for cp in _join_copies(refs[:K], refs[K], refs[K + 1]):
            cp.wait_send()
            cp.wait_recv()

    return pl.pallas_call(
        body, name=name,
        in_specs=[HBM] * K + [SEM, SEM, ANY], out_specs=[HBM] * K,
        out_shape=[pltpu.HBM(a.shape, a.dtype) for a in bufs],
        input_output_aliases={k: k for k in range(K)},
        compiler_params=SPLIT_COPY,
    )(*bufs, send, recv, after)


def _all_gather_slab(slab):
    m_per, n = slab.shape

    def body(x_ref, out_ref, send_sems, recv_sems, local_sem):
        x, y, c, chips = _mesh_position()
        me, sibling = (x, y, c), (x, y, 1 - c)

        def rows(px, py, pc):
            return out_ref.at[pl.ds((4 * px + 2 * py + pc) * m_per, m_per), :]

        def copy(k, block, to, src=None):
            return pltpu.make_async_remote_copy(
                src_ref=rows(*block) if src is None else src, dst_ref=rows(*block),
                send_sem=send_sems.at[k], recv_sem=recv_sems.at[k], device_id=to, device_id_type=MESH)

        mine = pltpu.make_async_copy(x_ref, rows(*me), local_sem)
        mine.start()
        first = [copy(0, me, sibling, src=x_ref)]
        first += [copy(1 + j, me, (*chip, c), src=x_ref) for j, chip in enumerate(chips)]
        for cp in first:
            cp.start()
        passed = [copy(4 + j, (*chip, c), sibling) for j, chip in enumerate(chips)]
        for j, chip in enumerate(chips):
            copy(1 + j, (*chip, c), me).wait_recv()
            passed[j].start()
        copy(0, sibling, me).wait_recv()
        for j, chip in enumerate(chips):
            copy(4 + j, (*chip, 1 - c), me).wait_recv()
        for cp in first + passed:
            cp.wait_send()
        mine.wait()

    return pl.pallas_call(
        body, name="gather_small_grads",
        out_shape=jax.ShapeDtypeStruct((N_DEV * m_per, n), slab.dtype),
        in_specs=[pl.BlockSpec(memory_space=pltpu.VMEM)],
        out_specs=pl.BlockSpec(memory_space=pltpu.VMEM),
        scratch_shapes=[pltpu.SemaphoreType.DMA((7,)), pltpu.SemaphoreType.DMA((7,)), pltpu.SemaphoreType.DMA],
    )(slab)


def _ffn_backward_weights(dhb, saved, w_in, w_out, after, tag, on_dw_out=None):
    n, fa, a = saved
    S = n.shape[0]
    ns, D, cs = w_in.shape
    F = w_out.shape[0]
    to = 512
    dw_out = _mm_tn(
        a, dhb, grid=(F // to,),
        a_spec=pl.BlockSpec((S, to), lambda j: (0, j)), b_spec=pl.BlockSpec((S, D), lambda j: (0, 0)),
        out_spec=pl.BlockSpec((to, D), lambda j: (j, 0)), out_shape=jax.ShapeDtypeStruct((F, D), BF16),
        scale=0.5, name=f"{tag}_dw_out").reshape(N_CHIPS, F // N_CHIPS, D)
    if on_dw_out is not None:
        after = on_dw_out(dw_out)
    dgu = _ffn_bwd_act(dhb, w_out, fa, after, name=f"{tag}_bwd_act")
    ti, tr = SHARD_TILE, D // 2
    per_g, per_s = F // ti, cs // ti
    dw_in = _mm_tn(
        n.T, dgu, grid=(D // tr, 2 * F // ti), a_is_transposed=True,
        a_spec=pl.BlockSpec((tr, S), lambda i, j: (i, 0)),
        b_spec=pl.BlockSpec((None, S, ti), lambda i, j: (lax.div(j, per_g), 0, lax.rem(j, per_g))),
        out_spec=pl.BlockSpec((None, tr, ti), lambda i, j: (lax.div(j, per_s), i, lax.rem(j, per_s))),
        out_shape=jax.ShapeDtypeStruct((ns, D, cs), BF16), scale=1.0, name=f"{tag}_dw_in")
    return dgu, dw_in, dw_out


def kernel(x, ffn1_norm, ffn1_w_in, ffn1_w_out, mix_norm, w_in_mix, w_pool, pool_scale, w_alpha, b_alpha, gla_norm, w_out_mix, ffn2_norm, ffn2_w_in, ffn2_w_out, final_norm, loss_target, m_ffn1_norm, m_ffn1_w_in, m_ffn1_w_out, m_mix_norm, m_w_in_mix, m_w_pool, m_pool_scale, m_w_alpha, m_b_alpha, m_gla_norm, m_w_out_mix, m_ffn2_norm, m_ffn2_w_in, m_ffn2_w_out, m_final_norm, v_ffn1_norm, v_ffn1_w_in, v_ffn1_w_out, v_mix_norm, v_w_in_mix, v_w_pool, v_pool_scale, v_w_alpha, v_b_alpha, v_gla_norm, v_w_out_mix, v_ffn2_norm, v_ffn2_w_in, v_ffn2_w_out, v_final_norm):
    names = ["ffn1_norm", "ffn1_w_in", "ffn1_w_out", "mix_norm", "w_in_mix", "w_pool", "pool_scale", "w_alpha",
             "b_alpha", "gla_norm", "w_out_mix", "ffn2_norm", "ffn2_w_in", "ffn2_w_out", "final_norm"]
    weights = dict(zip(names, [ffn1_norm, ffn1_w_in, ffn1_w_out, mix_norm, w_in_mix, w_pool, pool_scale, w_alpha,
                               b_alpha, gla_norm, w_out_mix, ffn2_norm, ffn2_w_in, ffn2_w_out, final_norm]))
    moms = dict(zip(names, [m_ffn1_norm, m_ffn1_w_in, m_ffn1_w_out, m_mix_norm, m_w_in_mix, m_w_pool, m_pool_scale,
                            m_w_alpha, m_b_alpha, m_gla_norm, m_w_out_mix, m_ffn2_norm, m_ffn2_w_in, m_ffn2_w_out,
                            m_final_norm]))
    vels = dict(zip(names, [v_ffn1_norm, v_ffn1_w_in, v_ffn1_w_out, v_mix_norm, v_w_in_mix, v_w_pool, v_pool_scale,
                            v_w_alpha, v_b_alpha, v_gla_norm, v_w_out_mix, v_ffn2_norm, v_ffn2_w_in, v_ffn2_w_out,
                            v_final_norm]))
    xi, yi, ci = lax.axis_index("x"), lax.axis_index("y"), lax.axis_index("c")
    chip = 2 * xi + yi
    c_idx = jnp.reshape(ci, (1,)).astype(jnp.int32)
    sc_idx = jnp.stack([chip, ci]).astype(jnp.int32)

    def flat2d(a):
        return a.reshape(-1, a.shape[-1])

    ex = _Exchanges(sc_idx, c_idx)

    def cast(n, after):
        return _cast_into_slot(flat2d(weights[n]), sc_idx, F32 if n == "w_alpha" else BF16, after, name=f"cast_{n}")

    first, rest = _Exchanges.GATHER_GROUPS[:2], _Exchanges.GATHER_GROUPS[2:]
    tok = ex.start_gather({n: cast(n, sc_idx) for g in first for n in g}, first, name="gather_ici_start_ffn1")
    tok2 = ex.start_gather({n: cast(n, tok) for g in rest for n in g}, rest, name="gather_ici_start_rest")
    early = [flat2d(moms["w_in_mix"]), flat2d(vels["w_in_mix"])]
    small_params = dict(g1=ffn1_norm, gm=mix_norm, g2=ffn2_norm, gf=final_norm.reshape(1, D_MODEL),
                        pool_scale=pool_scale, b_alpha=b_alpha, gla_norm=gla_norm, early=early)
    loss_blk, dx, small = _forward_backward(x[0], loss_target[0], ex, tok[0, 0] + tok2[0, 0], small_params)

    outs = {}

    def update(n, g, after):
        w = weights[n]
        w2 = flat2d(w) if w.ndim > 1 else w.reshape(1, -1)
        go, d, nm, nv = _adamw(w2, g.reshape(w2.shape), moms[n].reshape(w2.shape), vels[n].reshape(w2.shape), after,
                               name=f"adamw_{n}")
        outs[n] = (go.reshape(w.shape), d.reshape(w.shape), nm.reshape(w.shape), nv.reshape(w.shape))
        return nv

    tags = _Exchanges.REDUCE_ORDER
    last = ex.finish_exchange(tags[0], after=dx)
    for prev, tag in zip(tags, tags[1:]):
        last = ex.finish_exchange(tag, after=last)
        for n, g in ex.reduced(prev, after=last).items():
            last = update(n, g, last)

    grads = {}
    small_names = ["ffn1_norm", "mix_norm", "ffn2_norm", "final_norm", "pool_scale", "b_alpha", "gla_norm", "w_alpha",
                   "loss"]
    small = small + [loss_blk[0:1]]
    rows = [a.size // LANES for a in small]
    slab = jnp.concatenate([a.reshape(-1, LANES) for a in small], axis=0)
    pad = -slab.shape[0] % 8
    slab = jnp.pad(slab, ((0, pad), (0, 0)))
    gathered = _all_gather_slab(slab).reshape(N_DEV, slab.shape[0], LANES)
    total = _slab_sum(gathered, name="sum_small_grads")
    off = 0
    for n, a, r in zip(small_names, small, rows):
        grads[n] = total[off:off + r].reshape(a.shape)
        off += r
    grads["w_alpha"] = lax.dynamic_slice_in_dim(grads["w_alpha"], chip * (GLA_DK_TOTAL // N_CHIPS),
                                                GLA_DK_TOTAL // N_CHIPS, axis=1)

    loss = grads.pop("loss")[0, 0]
    for n in small_names[:-1]:
        last = update(n, grads[n], last)
    for n, g in ex.reduced(tags[-1], after=last).items():
        last = update(n, g, last)
    return (loss, dx[None], *[outs[n][0] for n in names], *[outs[n][1] for n in names],
            *[outs[n][2] for n in names], *[outs[n][3] for n in names])


class _Exchanges:
    GATHER_GROUPS = (("ffn1_w_in",), ("ffn1_w_out",), ("w_in_mix", "w_pool", "w_alpha"), ("w_out_mix",),
                     ("ffn2_w_in",), ("ffn2_w_out",))
    REDUCE_ORDER = ("ffn2", "mix", "ffn1_out", "ffn1_in")

    def __init__(self, sc_idx, c_idx):
        self.sc_idx, self.c_idx = sc_idx, c_idx
        self._gathers, self._swaps, self._reduces, self._joins = {}, {}, {}, {}

    def start_gather(self, bufs, groups, *, name):
        handles, token = _gather_ici_start([[bufs[n] for n in g] for g in groups], name=name)
        for g, h in zip(groups, handles):
            self._gathers[g[0]] = (g, h)
        return token

    def gathered(self, first, after):
        names, handle = self._gathers.pop(first)
        got = _gather_ici_wait(handle, after, name=f"gather_ici_wait_{first}")
        return dict(zip(names, _forward_halves(got, name=f"gather_forward_{first}")))

    def begin_reduce(self, tag, full):
        g4 = [a.reshape(N_CHIPS, 2, a.shape[1] // 2, a.shape[2]) for a in full.values()]
        lands = [(a.shape[0],) + a.shape[2:] for a in g4]
        handle, token = _split_copy_start(g4, lands, len(g4), _sibling_swap_copies, name=f"swap_start_{tag}")
        self._swaps[tag] = (list(full), handle)
        return token

    def start_reduce(self, tag, after):
        names, handle = self._swaps.pop(tag)
        g4, from_sibling = _split_copy_wait(handle, _sibling_swap_copies, after, name=f"swap_wait_{tag}")
        pair = [_pair_sum(a, b, self.c_idx, name=f"pair_sum_{n}") for n, a, b in zip(names, g4, from_sibling)]
        lands = [(N_CHIPS - 1,) + a.shape[1:] for a in pair]
        handle, token = _split_copy_start(pair, lands, len(pair) * (N_CHIPS - 1), _chip_exchange_copies,
                                          name=f"exchange_start_{tag}")
        self._reduces[tag] = (names, handle)
        return token

    def finish_exchange(self, tag, after):
        names, handle = self._reduces.pop(tag)
        pair, lands = _split_copy_wait(handle, _chip_exchange_copies, after, name=f"exchange_wait_{tag}")
        halves = [_chip_sum(a, b, self.sc_idx, name=f"chip_sum_{n}") for n, a, b in zip(names, pair, lands)]
        handle, token = _join_start(halves, name=f"join_start_{tag}")
        self._joins[tag] = (names, handle)
        return token

    def reduced(self, tag, after):
        names, handle = self._joins.pop(tag)
        return dict(zip(names, _join_wait(handle, after, name=f"join_wait_{tag}")))


def _forward_backward(h0, target, ex, started, sp):
    g1, gm, g2, gf = sp["g1"], sp["gm"], sp["g2"], sp["gf"]
    pool_scale, b_alpha, gla_norm = sp["pool_scale"], sp["b_alpha"], sp["gla_norm"]
    cs_mix = D_IN // N_CHIPS

    n1 = _rms_fwd(h0, g1 + started, sp["early"], name="ffn1_norm")
    w1_in = ex.gathered("ffn1_w_in", after=n1)["ffn1_w_in"]
    fa1, a1 = _ffn_up(n1, w1_in, [n1.T], name="ffn1_up")
    w1_out = ex.gathered("ffn1_w_out", after=a1)["ffn1_w_out"].reshape(D_FF, D_MODEL)
    h1 = _mm_nn(a1, w1_out, h0, 0.5, tm=512, tn=D_MODEL, tk=SHARD_TILE, name="ffn1_down")
    saved1 = (n1, fa1, a1)
    n_mix = _rms_fwd(h1, gm, [], name="mix_norm")
    gw = ex.gathered("w_in_mix", after=n_mix)
    w_mix = _concat_shards(gw["w_in_mix"], D_IN_PAD, name="w_mix_concat")[None]
    wp = gw["w_pool"].reshape(N_CHIPS, 4, POOL_GROUP_DIM // N_CHIPS, POOL_GROUP_DIM)
    wp = wp.transpose(1, 0, 2, 3).reshape(4, POOL_GROUP_DIM, POOL_GROUP_DIM)
    wa = gw["w_alpha"].transpose(1, 0, 2).reshape(GLA_GATE_RANK, GLA_DK_TOTAL)
    wa = jnp.pad(wa, ((0, LANES - GLA_GATE_RANK), (0, 0))).astype(BF16)
    u = _mm_nn(n_mix, w_mix[0], None, 1.0, tm=512, tn=D_IN_PAD, tk=D_MODEL, name="mix_in")
    y_pool = _pool_fwd(u, wp, pool_scale, name="pool_fwd")
    cat, o_gla, states = _gla_fwd(u, y_pool, wa, b_alpha, gla_norm, name="gla_fwd")
    w_omix = ex.gathered("w_out_mix", after=cat)["w_out_mix"].reshape(D_MODEL, D_MODEL)
    h2 = _mm_nn(cat, w_omix, h1, 1.0, tm=512, tn=D_MODEL, tk=1024, name="mix_out")
    n3 = _rms_fwd(h2, g2, [], name="ffn2_norm")
    w2_in = ex.gathered("ffn2_w_in", after=n3)["ffn2_w_in"]
    fa2, a2 = _ffn_up(n3, w2_in, [], name="ffn2_up")
    w2_out = ex.gathered("ffn2_w_out", after=a2)["ffn2_w_out"].reshape(D_FF, D_MODEL)
    h3 = _mm_nn(a2, w2_out, h2, 0.5, tm=512, tn=D_MODEL, tk=SHARD_TILE, name="ffn2_down")
    saved2 = (n3, fa2, a2)
    loss_blk, dh3, dh3b, d_gf = _final_loss(h3, gf, target, name="final_loss")

    dgu2, dw2_in, dw2_out = _ffn_backward_weights(dh3b, saved2, w2_in, w2_out, dh3b, "ffn2")
    tok = ex.begin_reduce("ffn2", {"ffn2_w_in": dw2_in, "ffn2_w_out": dw2_out})
    dh2, dh2b, d_g2 = _mm_nt_rmsbwd(dgu2, w2_in, h2, dh3, g2 + tok[0, 0], tk=SHARD_TILE, name="ffn2_dx")
    tok = ex.start_reduce("ffn2", after=dh2b)
    S = h0.shape[0]
    dcat = _mm_nt(dh2b, w_omix, tok, tm=512, tn=1024, name="mix_out_dx")
    dw_omix = _mm_tn(
        cat, dh2b, grid=(4,),
        a_spec=pl.BlockSpec((S, 512), lambda j: (0, j)), b_spec=pl.BlockSpec((S, D_MODEL), lambda j: (0, 0)),
        out_spec=pl.BlockSpec((512, D_MODEL), lambda j: (j, 0)),
        out_shape=jax.ShapeDtypeStruct((D_MODEL, D_MODEL), BF16), scale=1.0, name="mix_out_dw")
    dp, dw_pool, d_pscale = _pool_bwd(u, dcat, wp, pool_scale, name="pool_bwd")
    du, d_wa, d_ba, d_gn = _gla_bwd(u, o_gla, states, dcat, dp, wa, b_alpha, gla_norm, name="gla_bwd")
    du = du[None]
    tn_mix = COL_TILE
    dw_mix = _mm_tn(
        n_mix.T, du, grid=(2, D_IN_PAD // tn_mix), a_is_transposed=True,
        a_spec=pl.BlockSpec((D_MODEL // 2, S), lambda i, j: (i, 0)),
        b_spec=pl.BlockSpec((None, S, tn_mix), lambda i, j: (0, 0, j)),
        out_spec=pl.BlockSpec((D_MODEL // 2, tn_mix), lambda i, j: (i, j)),
        out_shape=jax.ShapeDtypeStruct((D_MODEL, D_IN_PAD), BF16), scale=1.0, name="mix_in_dw")
    dw_mix_s = _split_shards(dw_mix, cs_mix, name="dw_mix_split")
    dw_pool_s = dw_pool.reshape(4, N_CHIPS, POOL_GROUP_DIM // N_CHIPS, POOL_GROUP_DIM).transpose(1, 0, 2, 3)
    dw_pool_s = dw_pool_s.reshape(N_CHIPS, POOL_GROUP_DIM, POOL_GROUP_DIM).astype(BF16)
    tok = ex.begin_reduce("mix", {"w_in_mix": dw_mix_s,
                                  "w_out_mix": dw_omix.reshape(N_CHIPS, D_MODEL // N_CHIPS, D_MODEL),
                                  "w_pool": dw_pool_s})
    dh1, dh1b, d_gm = _mm_nt_rmsbwd(du, w_mix, h1, dh2, gm + tok[0, 0], tk=tn_mix, name="mix_in_dx")
    tok = ex.start_reduce("mix", after=dh1b)
    mix_started = tok
    dgu1, dw1_in, _ = _ffn_backward_weights(
        dh1b, saved1, w1_in, w1_out, None, "ffn1",
        on_dw_out=lambda dw: mix_started + ex.begin_reduce("ffn1_out", {"ffn1_w_out": dw}))
    out_started = ex.start_reduce("ffn1_out", after=dgu1)
    tok = ex.start_reduce("ffn1_in", after=out_started + ex.begin_reduce("ffn1_in", {"ffn1_w_in": dw1_in}))
    dx, _, d_g1 = _mm_nt_rmsbwd(dgu1, w1_in, h0, dh1, g1 + tok[0, 0], tk=SHARD_TILE, name="ffn1_dx")
    small = [d_g1, d_gm, d_g2, d_gf, d_pscale, d_ba, d_gn, d_wa[:GLA_GATE_RANK]]
    return loss_blk, dx, small
```

```python
import functools

import jax
import jax.numpy as jnp
from jax import lax
from jax.experimental import pallas as pl
from jax.experimental.pallas import tpu as pltpu

F32 = jnp.float32
BF16 = jnp.bfloat16
MESH = pl.DeviceIdType.MESH

D_MODEL = 2048
D_FF = 5632
D_POOL = 1024
POOL_WINDOWS = (2, 4, 8, 16)
POOL_GROUP_DIM = 256
D_GLA = 1024
GLA_HEADS = 4
GLA_DV = 256
GLA_DK = 128
GLA_DK_TOTAL = 512
GLA_GATE_RANK = 16
GATE_LOGIT_NORMALIZER = 16.0
CHUNK = 64
D_IN = 4112
D_IN_PAD = 4224
EPS = 1e-6

ADAM_LR = 0.001
ADAM_B1 = 0.9
ADAM_B2 = 0.999
ADAM_EPS = 1e-08
ADAM_WD = 0.01
ADAM_STEP = 10

N_CHIPS = 4
N_DEV = 8
V7X_VMEM_BYTES = 64 * 1024 * 1024
LANES = 128
MXU_TILE = 256
COL_TILE = 1408
SHARD_TILE = 2816


def _cparams(semantics, vmem_mb):
    assert vmem_mb * 1024 * 1024 < V7X_VMEM_BYTES
    return pltpu.CompilerParams(dimension_semantics=semantics, vmem_limit_bytes=vmem_mb * 1024 * 1024)


def _dot_nn(a, b):
    return jnp.dot(a, b, preferred_element_type=F32)


def _dot_nt(a, b):
    return lax.dot_general(a, b, (((1,), (1,)), ((), ())), preferred_element_type=F32)


def _dot_tn(a, b):
    return lax.dot_general(a, b, (((0,), (0,)), ((), ())), preferred_element_type=F32)


def _sigmoid(x):
    return 1.0 / (1.0 + jnp.exp(-x))


def _rms_fwd(x, g, early, *, name):
    S, D = x.shape
    tm = 256

    def body(x_ref, g_ref, *rest):
        o_ref, ot_ref = rest[len(early):]
        xv = x_ref[...]
        r = lax.rsqrt(jnp.mean(xv * xv, axis=-1, keepdims=True) + EPS)
        n = xv * r * g_ref[...]
        o_ref[...] = n.astype(BF16)
        ot_ref[...] = n.T.astype(BF16)

    return pl.pallas_call(
        body, name=name, grid=(S // tm,),
        in_specs=[pl.BlockSpec((tm, D), lambda i: (i, 0)), pl.BlockSpec((1, D), lambda i: (0, 0))]
        + [pl.BlockSpec(memory_space=pl.ANY)] * len(early),
        out_specs=[pl.BlockSpec((tm, D), lambda i: (i, 0)), pl.BlockSpec((D, tm), lambda i: (0, i))],
        out_shape=[jax.ShapeDtypeStruct((S, D), BF16), jax.ShapeDtypeStruct((D, S), BF16)],
        compiler_params=_cparams(("parallel",), 32),
    )(x, g, *early)


def _ffn_up(n, w_in, early, *, name):
    S, D = n.shape
    ns, _, cs = w_in.shape
    half = ns // 2
    F = cs * half
    tm, tn = 128, SHARD_TILE
    nb = cs // tn

    def body(n_ref, wg_ref, wu_ref, *rest):
        fa_ref, a_ref = rest[len(early):]
        nv = n_ref[...]
        g = _dot_nn(nv, wg_ref[...])
        u = _dot_nn(nv, wu_ref[...])
        s = _sigmoid(g)
        silu = g * s
        fa_ref[0] = (u * (s * (1.0 + g * (1.0 - s)))).astype(BF16)
        fa_ref[1] = silu.astype(BF16)
        a_ref[...] = (silu * u).astype(BF16)

    return pl.pallas_call(
        body, name=name, grid=(F // tn, S // tm),
        in_specs=[
            pl.BlockSpec((tm, D), lambda j, i: (i, 0)),
            pl.BlockSpec((None, D, tn), lambda j, i: (lax.div(j, nb), 0, lax.rem(j, nb))),
            pl.BlockSpec((None, D, tn), lambda j, i: (half + lax.div(j, nb), 0, lax.rem(j, nb))),
        ] + [pl.BlockSpec(memory_space=pl.ANY)] * len(early),
        out_specs=[
            pl.BlockSpec((2, tm, tn), lambda j, i: (0, i, j)),
            pl.BlockSpec((tm, tn), lambda j, i: (i, j)),
        ],
        out_shape=[jax.ShapeDtypeStruct((2, S, F), BF16), jax.ShapeDtypeStruct((S, F), BF16)],
        compiler_params=_cparams(("parallel", "parallel"), 56),
    )(n, w_in, w_in, *early)


def _mm_nn(a, b, resid, scale, *, tm, tn, tk, name):
    S, K = a.shape
    N = b.shape[1]
    nk = K // tk

    def body(*refs):
        if resid is None:
            a_ref, b_ref, o_ref, acc_ref = refs
            r_ref = None
        else:
            a_ref, b_ref, r_ref, o_ref, acc_ref = refs
        k = pl.program_id(2)

        @pl.when(k == 0)
        def _():
            acc_ref[...] = jnp.zeros_like(acc_ref)

        acc_ref[...] += _dot_nn(a_ref[...], b_ref[...])

        @pl.when(k == nk - 1)
        def _():
            out = acc_ref[...] * scale
            if r_ref is not None:
                out = r_ref[...] + out
            o_ref[...] = out

    in_specs = [pl.BlockSpec((tm, tk), lambda i, j, k: (i, k)), pl.BlockSpec((tk, tn), lambda i, j, k: (k, j))]
    args = [a, b]
    if resid is not None:
        in_specs.append(pl.BlockSpec((tm, tn), lambda i, j, k: (i, j)))
        args.append(resid)
    return pl.pallas_call(
        body, name=name, grid=(S // tm, N // tn, nk),
        in_specs=in_specs,
        out_specs=pl.BlockSpec((tm, tn), lambda i, j, k: (i, j)),
        out_shape=jax.ShapeDtypeStruct((S, N), F32),
        scratch_shapes=[pltpu.VMEM((tm, tn), F32)],
        compiler_params=_cparams(("parallel", "parallel", "arbitrary"), 48),
    )(*args)


def _mm_nt(a, b, after, *, tm, tn, name):
    S, K = a.shape
    N = b.shape[0]

    def body(a_ref, b_ref, after_ref, o_ref):
        o_ref[...] = _dot_nt(a_ref[...], b_ref[...])

    return pl.pallas_call(
        body, name=name, grid=(N // tn, S // tm),
        in_specs=[pl.BlockSpec((tm, K), lambda j, i: (i, 0)), pl.BlockSpec((tn, K), lambda j, i: (j, 0)),
                  pl.BlockSpec(memory_space=pl.ANY)],
        out_specs=pl.BlockSpec((tm, tn), lambda j, i: (i, j)),
        out_shape=jax.ShapeDtypeStruct((S, N), F32),
        compiler_params=_cparams(("parallel", "parallel"), 48),
    )(a, b, after)


def _mm_tn(a, b, *, grid, a_spec, b_spec, out_spec, out_shape, scale, name, a_is_transposed=False):
    dot = _dot_nn if a_is_transposed else _dot_tn

    def body(a_ref, b_ref, o_ref):
        o_ref[...] = (scale * dot(a_ref[...], b_ref[...])).astype(o_ref.dtype)

    return pl.pallas_call(
        body, name=name, grid=grid, in_specs=[a_spec, b_spec], out_specs=out_spec, out_shape=out_shape,
        compiler_params=_cparams(("parallel",) * len(grid), 56),
    )(a, b)


def _ffn_bwd_act(dhb, w_out, fa, after, *, name):
    S, D = dhb.shape
    F = w_out.shape[0]
    tm, tn = 256, SHARD_TILE

    def body(dh_ref, w_ref, fa_ref, after_ref, dgu_ref):
        da = 0.5 * _dot_nt(dh_ref[...], w_ref[...])
        dgu_ref[0] = (da * fa_ref[0].astype(F32)).astype(BF16)
        dgu_ref[1] = (da * fa_ref[1].astype(F32)).astype(BF16)

    return pl.pallas_call(
        body, name=name, grid=(F // tn, S // tm),
        in_specs=[
            pl.BlockSpec((tm, D), lambda j, i: (i, 0)),
            pl.BlockSpec((tn, D), lambda j, i: (j, 0)),
            pl.BlockSpec((2, tm, tn), lambda j, i: (0, i, j)),
            pl.BlockSpec(memory_space=pl.ANY),
        ],
        out_specs=pl.BlockSpec((2, tm, tn), lambda j, i: (0, i, j)),
        out_shape=jax.ShapeDtypeStruct((2, S, F), BF16),
        compiler_params=_cparams(("parallel", "parallel"), 56),
    )(dhb, w_out, fa, after)


def _mm_nt_rmsbwd(dact, w, h_in, dh_out, g, *, tk, name):
    ng, S, fg = dact.shape
    ns, D, cs = w.shape
    assert ng * fg == ns * cs
    tm, rc = 512, 64
    kpg, kps = fg // tk, cs // tk
    nk = ng * kpg

    def body(a_ref, w_ref, h_ref, dho_ref, g_ref, dh_ref, dhb_ref, dg_ref, acc_ref):
        i = pl.program_id(0)
        k = pl.program_id(1)

        @pl.when(k == 0)
        def _():
            acc_ref[...] = jnp.zeros_like(acc_ref)

        acc_ref[...] += _dot_nt(a_ref[...], w_ref[...])

        @pl.when(jnp.logical_and(i == 0, k == 0))
        def _():
            dg_ref[...] = jnp.zeros_like(dg_ref)

        @pl.when(k == nk - 1)
        def _():
            gv = g_ref[...]

            def rows_step(c, dg):
                rows = pl.ds(pl.multiple_of(c * rc, rc), rc)
                dn = acc_ref[rows, :]
                xv = h_ref[rows, :]
                r = lax.rsqrt(jnp.mean(xv * xv, axis=-1, keepdims=True) + EPS)
                xh = xv * r
                dng = dn * gv
                dx = r * (dng - xh * jnp.mean(dng * xh, axis=-1, keepdims=True))
                out = dho_ref[rows, :] + dx
                dh_ref[rows, :] = out
                dhb_ref[rows, :] = out.astype(BF16)
                return dg + jnp.sum(dn * xh, axis=0, keepdims=True)

            dg_ref[...] += lax.fori_loop(0, tm // rc, rows_step, jnp.zeros((1, D), F32))

    return pl.pallas_call(
        body, name=name, grid=(S // tm, nk),
        in_specs=[
            pl.BlockSpec((None, tm, tk), lambda i, k: (lax.div(k, kpg), i, lax.rem(k, kpg))),
            pl.BlockSpec((None, D, tk), lambda i, k: (lax.div(k, kps), 0, lax.rem(k, kps))),
            pl.BlockSpec((tm, D), lambda i, k: (i, 0)),
            pl.BlockSpec((tm, D), lambda i, k: (i, 0)),
            pl.BlockSpec((1, D), lambda i, k: (0, 0)),
        ],
        out_specs=[
            pl.BlockSpec((tm, D), lambda i, k: (i, 0), pipeline_mode=pl.Buffered(1)),
            pl.BlockSpec((tm, D), lambda i, k: (i, 0), pipeline_mode=pl.Buffered(1)),
            pl.BlockSpec((1, D), lambda i, k: (0, 0)),
        ],
        out_shape=[jax.ShapeDtypeStruct((S, D), F32), jax.ShapeDtypeStruct((S, D), BF16),
                   jax.ShapeDtypeStruct((1, D), F32)],
        scratch_shapes=[pltpu.VMEM((tm, D), F32)],
        compiler_params=_cparams(("arbitrary", "arbitrary"), 56),
    )(dact, w, h_in, dh_out, g)


def _final_loss(h, g, target, *, name):
    S, D = h.shape
    tm = 256

    def body(h_ref, g_ref, t_ref, loss_ref, dh_ref, dhb_ref, dg_ref):
        i = pl.program_id(0)

        @pl.when(i == 0)
        def _():
            loss_ref[...] = jnp.zeros_like(loss_ref)
            dg_ref[...] = jnp.zeros_like(dg_ref)

        xv = h_ref[...]
        gv = g_ref[...]
        r = lax.rsqrt(jnp.mean(xv * xv, axis=-1, keepdims=True) + EPS)
        xh = xv * r
        e = xh * gv - t_ref[...]
        loss_ref[...] += 0.5 * jnp.sum(jnp.mean(e * e, axis=-1, keepdims=True))
        dy = e * (1.0 / D)
        dg_ref[...] += jnp.sum(dy * xh, axis=0, keepdims=True)
        dyg = dy * gv
        dx = r * (dyg - xh * jnp.mean(dyg * xh, axis=-1, keepdims=True))
        dh_ref[...] = dx
        dhb_ref[...] = dx.astype(BF16)

    return pl.pallas_call(
        body, name=name, grid=(S // tm,),
        in_specs=[pl.BlockSpec((tm, D), lambda i: (i, 0)), pl.BlockSpec((1, D), lambda i: (0, 0)),
                  pl.BlockSpec((tm, D), lambda i: (i, 0))],
        out_specs=[pl.BlockSpec((8, LANES), lambda i: (0, 0)), pl.BlockSpec((tm, D), lambda i: (i, 0)),
                   pl.BlockSpec((tm, D), lambda i: (i, 0)), pl.BlockSpec((1, D), lambda i: (0, 0))],
        out_shape=[jax.ShapeDtypeStruct((8, LANES), F32), jax.ShapeDtypeStruct((S, D), F32),
                   jax.ShapeDtypeStruct((S, D), BF16), jax.ShapeDtypeStruct((1, D), F32)],
        compiler_params=_cparams(("arbitrary",), 40),
    )(h, g, target)


POOL_HALO = 16
POOL_ROWS = 256


def _pool_window_mean_minus_token(ext, tok0, w):
    s = ext
    k = 1
    while k < w:
        s = s + pltpu.roll(s, k, 0)
        k *= 2
    win = s[POOL_HALO:, :]
    tok = tok0 + lax.broadcasted_iota(jnp.int32, (POOL_ROWS, 1), 0)
    cnt = jnp.minimum(tok + 1, w).astype(F32)
    return win / cnt - ext[POOL_HALO:, :], cnt


def _pool_fwd(u, w_pool, scale, *, name):
    S = u.shape[0]
    C = POOL_GROUP_DIM
    nsteps = S // POOL_ROWS

    def body(p_ref, w_ref, sc_ref, y_ref, xp_ref):
        xp_ref[0:POOL_HALO, :] = jnp.zeros((POOL_HALO, D_POOL), F32)
        xp_ref[POOL_HALO:, :] = p_ref[...]
        for gi, win in enumerate(POOL_WINDOWS):
            cols = slice(gi * C, (gi + 1) * C)

            def step(c, carry, cols=cols, win=win, gi=gi):
                r0 = pl.multiple_of(c * POOL_ROWS, POOL_ROWS)
                ext = xp_ref[pl.ds(r0, POOL_ROWS + POOL_HALO), cols]
                pooled, _ = _pool_window_mean_minus_token(ext, r0, win)
                y = _dot_nn(pooled.astype(BF16), w_ref[gi]) * sc_ref[:, cols]
                y_ref[pl.ds(r0, POOL_ROWS), cols] = y.astype(BF16)
                return carry

            lax.fori_loop(0, nsteps, step, 0)

    return pl.pallas_call(
        body, name=name, grid=(1,),
        in_specs=[pl.BlockSpec((S, D_POOL), lambda i: (0, 0)),
                  pl.BlockSpec((4, C, C), lambda i: (0, 0, 0)),
                  pl.BlockSpec((1, D_POOL), lambda i: (0, 0))],
        out_specs=pl.BlockSpec((S, D_POOL), lambda i: (0, 0)),
        out_shape=jax.ShapeDtypeStruct((S, D_POOL), BF16),
        scratch_shapes=[pltpu.VMEM((S + POOL_HALO, D_POOL), F32)],
        compiler_params=_cparams(("arbitrary",), 48),
    )(u, w_pool, scale)


def _pool_bwd(u, dcat, w_pool, scale, *, name):
    S = u.shape[0]
    C = POOL_GROUP_DIM
    nsteps = S // POOL_ROWS

    def body(p_ref, dy_ref, w_ref, sc_ref, dp_ref, dw_ref, dsc_ref, xp_ref, e_ref, neg_ref):
        xp_ref[0:POOL_HALO, :] = jnp.zeros((POOL_HALO, D_POOL), F32)
        xp_ref[POOL_HALO:, :] = p_ref[...]
        e_ref[S:, :] = jnp.zeros((POOL_HALO, C), F32)
        for gi, win in enumerate(POOL_WINDOWS):
            cols = slice(gi * C, (gi + 1) * C)

            def step_a(c, carry, cols=cols, win=win, gi=gi):
                dw, dsc = carry
                r0 = pl.multiple_of(c * POOL_ROWS, POOL_ROWS)
                ext = xp_ref[pl.ds(r0, POOL_ROWS + POOL_HALO), cols]
                pooled, cnt = _pool_window_mean_minus_token(ext, r0, win)
                pb = pooled.astype(BF16)
                wv = w_ref[gi]
                dy = dy_ref[pl.ds(r0, POOL_ROWS), cols]
                dsc = dsc + jnp.sum(dy * _dot_nn(pb, wv), axis=0, keepdims=True)
                dyp = (dy * sc_ref[:, cols]).astype(BF16)
                dw = dw + _dot_tn(pb, dyp)
                dpooled = _dot_nt(dyp, wv)
                e_ref[pl.ds(r0, POOL_ROWS), :] = dpooled / cnt
                neg_ref[pl.ds(r0, POOL_ROWS), :] = -dpooled
                return dw, dsc

            dw, dsc = lax.fori_loop(0, nsteps, step_a, (jnp.zeros((C, C), F32), jnp.zeros((1, C), F32)))
            dw_ref[gi] = dw
            dsc_ref[:, cols] = dsc

            def step_b(c, carry, cols=cols, win=win):
                r0 = pl.multiple_of(c * POOL_ROWS, POOL_ROWS)
                s = e_ref[pl.ds(r0, POOL_ROWS + POOL_HALO), :]
                n = POOL_ROWS + POOL_HALO
                k = 1
                while k < win:
                    s = s + pltpu.roll(s, n - k, 0)
                    k *= 2
                du = s[:POOL_ROWS, :] + neg_ref[pl.ds(r0, POOL_ROWS), :]
                dp_ref[pl.ds(r0, POOL_ROWS), cols] = du.astype(BF16)
                return carry

            lax.fori_loop(0, nsteps, step_b, 0)

    return pl.pallas_call(
        body, name=name, grid=(1,),
        in_specs=[pl.BlockSpec((S, D_POOL), lambda i: (0, 0)),
                  pl.BlockSpec((S, D_POOL), lambda i: (0, 0)),
                  pl.BlockSpec((4, C, C), lambda i: (0, 0, 0)),
                  pl.BlockSpec((1, D_POOL), lambda i: (0, 0))],
        out_specs=[pl.BlockSpec((S, D_POOL), lambda i: (0, 0)),
                   pl.BlockSpec((4, C, C), lambda i: (0, 0, 0)),
                   pl.BlockSpec((1, D_POOL), lambda i: (0, 0))],
        out_shape=[jax.ShapeDtypeStruct((S, D_POOL), BF16), jax.ShapeDtypeStruct((4, C, C), F32),
                   jax.ShapeDtypeStruct((1, D_POOL), F32)],
        scratch_shapes=[pltpu.VMEM((S + POOL_HALO, D_POOL), F32), pltpu.VMEM((S + POOL_HALO, C), F32),
                        pltpu.VMEM((S, C), F32)],
        compiler_params=_cparams(("arbitrary",), 56),
    )(u, dcat, w_pool, scale)


GLA_ROWS = 128
U_Q_BLK, U_K_BLK = 2, 3
U_V_BLK, U_G_BLK = 2, 3
U_R_BLK = 32


def _prefix_sum_rows(x):
    n = x.shape[0]
    row = lax.broadcasted_iota(jnp.int32, x.shape, 0)
    k = 1
    while k < n:
        x = x + jnp.where(row >= k, pltpu.roll(x, k, 0), 0.0)
        k *= 2
    return x


def _suffix_sum_rows(x):
    n = x.shape[0]
    row = lax.broadcasted_iota(jnp.int32, x.shape, 0)
    k = 1
    while k < n:
        x = x + jnp.where(row < n - k, pltpu.roll(x, n - k, 0), 0.0)
        k *= 2
    return x


def _log_sigmoid(z):
    return jnp.minimum(z, 0.0) - jnp.log(1.0 + jnp.exp(-jnp.abs(z)))


def _gla_chunk_terms(la_c, q_c, k_c):
    bc = _prefix_sum_rows(la_c)
    bl = jnp.sum(la_c, axis=0, keepdims=True)
    eb = jnp.exp(bc)
    enb = jnp.exp(-bc)
    etail = jnp.exp(bl - bc)
    qd = q_c * (GLA_DK ** -0.5) * eb
    ki = k_c * enb
    kt = k_c * etail
    d = jnp.exp(bl)
    return eb, enb, etail, qd, ki, kt, d


def _gla_fwd(u, y_pool, w_alpha, b_alpha, gnorm, *, name):
    S = u.shape[0]
    RB = GLA_ROWS
    ncc = RB // CHUNK
    H, DK, DV = GLA_HEADS, GLA_DK, GLA_DV

    def body(q_ref, k_ref, v_ref, go_ref, r_ref, yp_ref, wa_ref, ba_ref, gn_ref, cat_ref, o_ref, st_ref, state):
        i = pl.program_id(0)

        @pl.when(i == 0)
        def _():
            state[...] = jnp.zeros_like(state)

        cat_ref[:, :D_POOL] = yp_ref[...]
        y_ref = cat_ref.at[:, D_POOL:]

        z = _dot_nn(r_ref[...].astype(BF16), wa_ref[...]) + ba_ref[...]
        la = _log_sigmoid(z) / GATE_LOGIT_NORMALIZER
        ri = lax.broadcasted_iota(jnp.int32, (CHUNK, CHUNK), 0)
        ci = lax.broadcasted_iota(jnp.int32, (CHUNK, CHUNK), 1)
        tri = ri >= ci
        gn = gn_ref[...]
        for cc in range(ncc):
            rs = slice(cc * CHUNK, (cc + 1) * CHUNK)
            for h in range(H):
                ks = slice(h * DK, (h + 1) * DK)
                vs = slice(h * DV, (h + 1) * DV)
                _, _, _, qd, ki, kt, d = _gla_chunk_terms(la[rs, ks], q_ref[rs, ks], k_ref[rs, ks])
                qdb = qd.astype(BF16)
                vb = v_ref[rs, vs].astype(BF16)
                p = jnp.where(tri, _dot_nt(qdb, ki.astype(BF16)), 0.0)
                st = state[h]
                st_ref[cc, h] = st
                o = _dot_nn(p.astype(BF16), vb) + _dot_nt(qdb, st.astype(BF16))
                state[h] = st * d + _dot_tn(vb, kt.astype(BF16))
                o_ref[rs, vs] = o
                rinv = lax.rsqrt(jnp.mean(o * o, axis=-1, keepdims=True) + EPS)
                go = go_ref[rs, vs]
                y_ref[rs, vs] = (o * rinv * gn * (go * _sigmoid(go))).astype(BF16)

    nblk = S // RB
    return pl.pallas_call(
        body, name=name, grid=(nblk,),
        in_specs=[
            pl.BlockSpec((RB, GLA_DK_TOTAL), lambda i: (i, U_Q_BLK)),
            pl.BlockSpec((RB, GLA_DK_TOTAL), lambda i: (i, U_K_BLK)),
            pl.BlockSpec((RB, D_GLA), lambda i: (i, U_V_BLK)),
            pl.BlockSpec((RB, D_GLA), lambda i: (i, U_G_BLK)),
            pl.BlockSpec((RB, LANES), lambda i: (i, U_R_BLK)),
            pl.BlockSpec((RB, D_POOL), lambda i: (i, 0)),
            pl.BlockSpec((LANES, GLA_DK_TOTAL), lambda i: (0, 0)),
            pl.BlockSpec((1, GLA_DK_TOTAL), lambda i: (0, 0)),
            pl.BlockSpec((1, DV), lambda i: (0, 0)),
        ],
        out_specs=[
            pl.BlockSpec((RB, D_POOL + D_GLA), lambda i: (i, 0)),
            pl.BlockSpec((RB, D_GLA), lambda i: (i, 0)),
            pl.BlockSpec((ncc, H, DV, DK), lambda i: (i, 0, 0, 0)),
        ],
        out_shape=[jax.ShapeDtypeStruct((S, D_POOL + D_GLA), BF16), jax.ShapeDtypeStruct((S, D_GLA), F32),
                   jax.ShapeDtypeStruct((S // CHUNK, H, DV, DK), F32)],
        scratch_shapes=[pltpu.VMEM((H, DV, DK), F32)],
        compiler_params=_cparams(("arbitrary",), 32),
    )(u, u, u, u, u, y_pool, w_alpha, b_alpha, gnorm)


def _gla_bwd(u, o, states, dcat, dpool, w_alpha, b_alpha, gnorm, *, name):
    S = u.shape[0]
    RB = GLA_ROWS
    ncc = RB // CHUNK
    H, DK, DV = GLA_HEADS, GLA_DK, GLA_DV
    nblk = S // RB
    o_q, o_k = D_POOL, D_POOL + GLA_DK_TOTAL
    o_v, o_g, o_r = o_k + GLA_DK_TOTAL, o_k + GLA_DK_TOTAL + D_GLA, o_k + GLA_DK_TOTAL + 2 * D_GLA

    def body(q_ref, k_ref, v_ref, go_ref, r_ref, o_ref, st_ref, dy_ref, dpool_ref, wa_ref, ba_ref, gn_ref,
             du_ref, dwa_ref, dba_ref, dgn_ref, dstate, dz_ref):
        i = pl.program_id(0)

        @pl.when(i == 0)
        def _():
            dstate[...] = jnp.zeros_like(dstate)
            dwa_ref[...] = jnp.zeros_like(dwa_ref)
            dba_ref[...] = jnp.zeros_like(dba_ref)
            dgn_ref[...] = jnp.zeros_like(dgn_ref)

        du_ref[:, :o_q] = dpool_ref[...]
        dq_ref, dk_ref = du_ref.at[:, o_q:o_k], du_ref.at[:, o_k:o_v]
        dv_ref, dgo_ref, dr_ref = du_ref.at[:, o_v:o_g], du_ref.at[:, o_g:o_r], du_ref.at[:, o_r:]

        rb = r_ref[...].astype(BF16)
        wa = wa_ref[...]
        z = _dot_nn(rb, wa) + ba_ref[...]
        la = _log_sigmoid(z) / GATE_LOGIT_NORMALIZER
        ri = lax.broadcasted_iota(jnp.int32, (CHUNK, CHUNK), 0)
        ci = lax.broadcasted_iota(jnp.int32, (CHUNK, CHUNK), 1)
        tri = ri >= ci
        last_row = lax.broadcasted_iota(jnp.int32, (CHUNK, DK), 0) == CHUNK - 1
        gn = gn_ref[...]
        dgn = jnp.zeros((1, DV), F32)
        for cc in reversed(range(ncc)):
            rs = slice(cc * CHUNK, (cc + 1) * CHUNK)
            for h in range(H):
                ks = slice(h * DK, (h + 1) * DK)
                vs = slice(h * DV, (h + 1) * DV)
                eb, enb, etail, qd, ki, kt, d = _gla_chunk_terms(la[rs, ks], q_ref[rs, ks], k_ref[rs, ks])
                qdb, kib, ktb = qd.astype(BF16), ki.astype(BF16), kt.astype(BF16)
                vb = v_ref[rs, vs].astype(BF16)
                p = jnp.where(tri, _dot_nt(qdb, kib), 0.0)
                ov = o_ref[rs, vs]
                go = go_ref[rs, vs]
                dy = dy_ref[rs, vs]
                rinv = lax.rsqrt(jnp.mean(ov * ov, axis=-1, keepdims=True) + EPS)
                oh = ov * rinv
                sg = _sigmoid(go)
                dgo_ref[rs, vs] = (dy * (oh * gn) * (sg * (1.0 + go * (1.0 - sg)))).astype(BF16)
                don = dy * (go * sg)
                dgn = dgn + jnp.sum(don * oh, axis=0, keepdims=True)
                doh = don * gn
                do = rinv * (doh - oh * jnp.mean(doh * oh, axis=-1, keepdims=True))
                dob = do.astype(BF16)
                st = st_ref[cc, h]
                dst = dstate[h]
                stb, dstb = st.astype(BF16), dst.astype(BF16)
                dp = jnp.where(tri, _dot_nt(dob, vb), 0.0).astype(BF16)
                dv_ref[rs, vs] = (_dot_tn(p.astype(BF16), dob) + _dot_nt(ktb, dstb)).astype(BF16)
                dqd = _dot_nn(dp, kib) + _dot_nn(dob, stb)
                dki = _dot_tn(dp, qdb)
                dkt = _dot_nn(vb, dstb)
                dd = jnp.sum(dst * st, axis=0, keepdims=True)
                dstate[h] = dst * d + _dot_tn(dob, qdb)
                dq_ref[rs, ks] = (dqd * eb * (DK ** -0.5)).astype(BF16)
                dk_ref[rs, ks] = (dki * enb + dkt * etail).astype(BF16)
                dbl = jnp.sum(dkt * kt, axis=0, keepdims=True) + dd * d
                dbc = dqd * qd - dki * ki - dkt * kt
                dbc = dbc + jnp.where(last_row, dbl, 0.0)
                dla = _suffix_sum_rows(dbc)
                dz_ref[rs, ks] = dla * (1.0 / GATE_LOGIT_NORMALIZER) * (1.0 - _sigmoid(z[rs, ks]))
        dz = dz_ref[...]
        dzb = dz.astype(BF16)
        dr_ref[...] = _dot_nt(dzb, wa).astype(BF16)
        dwa_ref[...] += _dot_tn(rb, dzb)
        dba_ref[...] += jnp.sum(dz, axis=0, keepdims=True)
        dgn_ref[...] += dgn

    def rev(blk):
        return lambda i: (nblk - 1 - i, blk)

    return pl.pallas_call(
        body, name=name, grid=(nblk,),
        in_specs=[
            pl.BlockSpec((RB, GLA_DK_TOTAL), rev(U_Q_BLK)),
            pl.BlockSpec((RB, GLA_DK_TOTAL), rev(U_K_BLK)),
            pl.BlockSpec((RB, D_GLA), rev(U_V_BLK)),
            pl.BlockSpec((RB, D_GLA), rev(U_G_BLK)),
            pl.BlockSpec((RB, LANES), rev(U_R_BLK)),
            pl.BlockSpec((RB, D_GLA), rev(0)),
            pl.BlockSpec((ncc, H, DV, DK), lambda i: (nblk - 1 - i, 0, 0, 0)),
            pl.BlockSpec((RB, D_GLA), rev(1)),
            pl.BlockSpec((RB, D_POOL), rev(0)),
            pl.BlockSpec((LANES, GLA_DK_TOTAL), lambda i: (0, 0)),
            pl.BlockSpec((1, GLA_DK_TOTAL), lambda i: (0, 0)),
            pl.BlockSpec((1, DV), lambda i: (0, 0)),
        ],
        out_specs=[
            pl.BlockSpec((RB, D_IN_PAD), rev(0)),
            pl.BlockSpec((LANES, GLA_DK_TOTAL), lambda i: (0, 0)),
            pl.BlockSpec((1, GLA_DK_TOTAL), lambda i: (0, 0)),
            pl.BlockSpec((1, DV), lambda i: (0, 0)),
        ],
        out_shape=[
            jax.ShapeDtypeStruct((S, D_IN_PAD), BF16),
            jax.ShapeDtypeStruct((LANES, GLA_DK_TOTAL), F32), jax.ShapeDtypeStruct((1, GLA_DK_TOTAL), F32),
            jax.ShapeDtypeStruct((1, DV), F32),
        ],
        scratch_shapes=[pltpu.VMEM((H, DV, DK), F32), pltpu.VMEM((RB, GLA_DK_TOTAL), F32)],
        compiler_params=_cparams(("arbitrary",), 32),
    )(u, u, u, u, u, o, states, dcat, dpool, w_alpha, b_alpha, gnorm)


def _concat_shards(w4, width, *, name):
    ns, R, cs = w4.shape
    tr = 256

    def body(w_ref, o_ref):
        for t in range(ns):
            o_ref[:, t * cs:(t + 1) * cs] = w_ref[t]
        o_ref[:, ns * cs:] = jnp.zeros((tr, width - ns * cs), o_ref.dtype)

    return pl.pallas_call(
        body, name=name, grid=(R // tr,),
        in_specs=[pl.BlockSpec((ns, tr, cs), lambda i: (0, i, 0))],
        out_specs=pl.BlockSpec((tr, width), lambda i: (i, 0)),
        out_shape=jax.ShapeDtypeStruct((R, width), w4.dtype),
        compiler_params=_cparams(("parallel",), 32),
    )(w4)


def _split_shards(a, cs, *, name):
    R, width = a.shape
    tr = 256

    def body(a_ref, o_ref):
        for t in range(N_CHIPS):
            o_ref[t] = a_ref[:, t * cs:(t + 1) * cs]

    return pl.pallas_call(
        body, name=name, grid=(R // tr,),
        in_specs=[pl.BlockSpec((tr, width), lambda i: (i, 0))],
        out_specs=pl.BlockSpec((N_CHIPS, tr, cs), lambda i: (0, i, 0)),
        out_shape=jax.ShapeDtypeStruct((N_CHIPS, R, cs), a.dtype),
        compiler_params=_cparams(("parallel",), 32),
    )(a)


def _row_tile(rows, cols, itemsize, budget=2 * 1024 * 1024):
    if rows * cols * itemsize <= budget or rows % 16:
        return rows
    best = 16
    for t in range(16, rows + 1, 16):
        if rows % t == 0 and t * cols * itemsize <= budget:
            best = t
    return best


def _adamw(w, g, m, v, after, *, name):
    R, C = w.shape
    tr = _row_tile(R, C, 4)

    def body(w_ref, g_ref, m_ref, v_ref, after_ref, go_ref, d_ref, nm_ref, nv_ref):
        gv = g_ref[...]
        go_ref[...] = gv
        mn = ADAM_B1 * m_ref[...] + (1.0 - ADAM_B1) * gv
        vn = ADAM_B2 * v_ref[...] + (1.0 - ADAM_B2) * jnp.square(gv)
        m_hat = mn / (1.0 - ADAM_B1 ** ADAM_STEP)
        v_hat = vn / (1.0 - ADAM_B2 ** ADAM_STEP)
        d_ref[...] = -ADAM_LR * (m_hat / (jnp.sqrt(v_hat) + ADAM_EPS) + ADAM_WD * w_ref[...])
        nm_ref[...] = mn
        nv_ref[...] = vn

    spec = pl.BlockSpec((tr, C), lambda i: (i, 0))
    shp = jax.ShapeDtypeStruct((R, C), F32)
    return pl.pallas_call(
        body, name=name, grid=(R // tr,), in_specs=[spec] * 4 + [pl.BlockSpec(memory_space=pl.ANY)],
        out_specs=[spec] * 4, out_shape=[shp] * 4,
        compiler_params=_cparams(("parallel",), 48),
    )(w, g, m, v, after)


def _pair_sum(g4, recv, c_idx, *, name):
    ns, _, R2, C = g4.shape
    tr = _row_tile(R2, C, 2)

    def body(c_ref, g_ref, r_ref, o_ref):
        o_ref[...] = (g_ref[...].astype(F32) + r_ref[...].astype(F32)).astype(BF16)

    return pl.pallas_call(
        body, name=name,
        grid_spec=pltpu.PrefetchScalarGridSpec(
            num_scalar_prefetch=1, grid=(ns, R2 // tr),
            in_specs=[pl.BlockSpec((None, None, tr, C), lambda s, i, c: (s, c[0], i, 0)),
                      pl.BlockSpec((None, tr, C), lambda s, i, c: (s, i, 0))],
            out_specs=pl.BlockSpec((None, tr, C), lambda s, i, c: (s, i, 0)),
        ),
        out_shape=jax.ShapeDtypeStruct((ns, R2, C), BF16),
        compiler_params=_cparams(("parallel", "parallel"), 32),
    )(c_idx, g4, recv)


def _chip_sum(part, recv, sc_idx, *, name):
    _, R2, C = part.shape
    tr = _row_tile(R2, C, 4)
    nblk = R2 // tr

    def body(s_ref, p_ref, r_ref, o_ref):
        acc = p_ref[...].astype(F32)
        for j in range(N_CHIPS - 1):
            acc = acc + r_ref[j].astype(F32)
        o_ref[...] = acc

    return pl.pallas_call(
        body, name=name,
        grid_spec=pltpu.PrefetchScalarGridSpec(
            num_scalar_prefetch=1, grid=(nblk,),
            in_specs=[pl.BlockSpec((None, tr, C), lambda i, s: (s[0], i, 0)),
                      pl.BlockSpec((N_CHIPS - 1, tr, C), lambda i, s: (0, i, 0))],
            out_specs=pl.BlockSpec((tr, C), lambda i, s: (s[1] * nblk + i, 0)),
        ),
        out_shape=jax.ShapeDtypeStruct((2 * R2, C), F32),
        compiler_params=_cparams(("parallel",), 32),
    )(sc_idx, part, recv)


def _cast_into_slot(w, sc_idx, dtype, after, *, name):
    R, C = w.shape
    tr = _row_tile(R, C, 4)

    def body(s_ref, w_ref, after_ref, o_ref):
        o_ref[...] = w_ref[...].astype(dtype)

    return pl.pallas_call(
        body, name=name,
        grid_spec=pltpu.PrefetchScalarGridSpec(
            num_scalar_prefetch=1, grid=(R // tr,),
            in_specs=[pl.BlockSpec((tr, C), lambda i, s: (i, 0)), pl.BlockSpec(memory_space=pl.ANY)],
            out_specs=pl.BlockSpec((None, tr, C), lambda i, s: (s[0], i, 0)),
        ),
        out_shape=jax.ShapeDtypeStruct((N_CHIPS, R, C), dtype),
        compiler_params=_cparams(("parallel",), 32),
    )(sc_idx, w, after)


def _slab_sum(slabs, *, name):
    n, M, C = slabs.shape

    def body(x_ref, o_ref):
        acc = x_ref[0]
        for d in range(1, n):
            acc = acc + x_ref[d]
        o_ref[...] = acc

    return pl.pallas_call(
        body, name=name, out_shape=jax.ShapeDtypeStruct((M, C), F32),
    )(slabs)


def _mesh_position():
    x, y, c = lax.axis_index("x"), lax.axis_index("y"), lax.axis_index("c")
    other_chips = [(1 - x, y), (x, 1 - y), (1 - x, 1 - y)]
    return x, y, c, other_chips


ANY = pl.BlockSpec(memory_space=pl.ANY)


HBM = pl.BlockSpec(memory_space=pltpu.HBM)
SEM = pl.BlockSpec(memory_space=pltpu.SEMAPHORE)
SPLIT_COPY = pltpu.CompilerParams(has_side_effects=pltpu.SideEffectType.DATAFLOW_SIDE_EFFECTING)
TOKEN = jax.ShapeDtypeStruct((8, LANES), F32)


def _in_hbm(a):
    return pltpu.with_memory_space_constraint(a, pltpu.HBM)


def _half_rows(ref, slot, half):
    hr = ref.shape[1] // 2
    return ref.at[slot, pl.ds(half * hr, hr), :]


GATHER_COPIES = {"direct": 3, "first_hop": 2, "second_hop": 1}


def _gather_routes(kind):
    x, y, c, chips = _mesh_position()
    me = 2 * x + y
    first = (x + (1 - c) * (1 - 2 * x), y + c * (1 - 2 * y))
    second = (x + c * (1 - 2 * x), y + (1 - c) * (1 - 2 * y))
    if kind == "direct":
        return [(me, (*p, c)) for p in chips], [2 * p[0] + p[1] for p in chips]
    if kind == "first_hop":
        return [(me, (*first, c)), (me, (*second, c))], [2 * first[0] + first[1], 2 * second[0] + second[1]]
    assert kind == "second_hop"
    return [(2 * first[0] + first[1], (*second, c))], [2 * (1 - x) + (1 - y)]


def _gather_ici_start(groups, kinds, *, name):
    flat = [b for g in groups for b in g]
    K, G = len(flat), len(groups)

    def body(*refs):
        ins, sems, token = refs[:K], refs[K:K + 2 * G], refs[-1]
        c = lax.axis_index("c")
        k = 0
        for gi, (g, kind) in enumerate(zip(groups, kinds)):
            sends, _ = _gather_routes(kind)
            for n in range(len(g)):
                for j, (slot, to) in enumerate(sends):
                    part = _half_rows(ins[k], slot, c)
                    pltpu.make_async_remote_copy(
                        src_ref=part, dst_ref=part, send_sem=sems[2 * gi].at[n * len(sends) + j],
                        recv_sem=sems[2 * gi + 1].at[n * len(sends) + j],
                        device_id=to, device_id_type=MESH).start()
                k += 1
        token[...] = jnp.zeros_like(token)

    sem_shapes = []
    for g, kind in zip(groups, kinds):
        sem_shapes += [pltpu.SemaphoreType.DMA((len(g) * GATHER_COPIES[kind],))] * 2
    out = pl.pallas_call(
        body, name=name,
        in_specs=[HBM] * K,
        out_specs=[SEM] * (2 * G) + [HBM] * K + [pl.BlockSpec(memory_space=pltpu.VMEM)],
        out_shape=sem_shapes + [pltpu.HBM(b.shape, b.dtype) for b in flat] + [TOKEN],
        input_output_aliases={k: 2 * G + k for k in range(K)},
        compiler_params=SPLIT_COPY,
    )(*[_in_hbm(b) for b in flat])
    handles, k = [], 2 * G
    for gi, (g, kind) in enumerate(zip(groups, kinds)):
        handles.append((out[2 * gi], out[2 * gi + 1], list(out[k:k + len(g)]), kind))
        k += len(g)
    return handles, out[-1]


def _gather_ici_wait(handle, after, *, name):
    send, recv, bufs, kind = handle
    n = len(bufs)

    def body(*refs):
        ins, send_ref, recv_ref = refs[:n], refs[n], refs[n + 1]
        c = lax.axis_index("c")
        sends, arrivals = _gather_routes(kind)
        for k in range(n):
            for j, ((slot, to), landed) in enumerate(zip(sends, arrivals)):
                cp = pltpu.make_async_remote_copy(
                    src_ref=_half_rows(ins[k], slot, c), dst_ref=_half_rows(ins[k], landed, c),
                    send_sem=send_ref.at[k * len(sends) + j], recv_sem=recv_ref.at[k * len(sends) + j],
                    device_id=to, device_id_type=MESH)
                cp.wait_send()
                cp.wait_recv()

    return pl.pallas_call(
        body, name=name,
        in_specs=[HBM] * n + [SEM, SEM, ANY], out_specs=[HBM] * n,
        out_shape=[pltpu.HBM(b.shape, b.dtype) for b in bufs],
        input_output_aliases={k: k for k in range(n)},
        compiler_params=SPLIT_COPY,
    )(*bufs, send, recv, after)


def _forward_halves(bufs, *, name):
    K = len(bufs)
    per = N_CHIPS - 1

    def body(*refs):
        outs = refs[K:2 * K]
        send_sems, recv_sems = refs[2 * K:]
        x, y, c, chips = _mesh_position()
        copies = []
        for k in range(K):
            for j, chip in enumerate(chips):
                got = _half_rows(outs[k], 2 * chip[0] + chip[1], c)
                cp = pltpu.make_async_remote_copy(
                    src_ref=got, dst_ref=got, send_sem=send_sems.at[k * per + j], recv_sem=recv_sems.at[k * per + j],
                    device_id=(x, y, 1 - c), device_id_type=MESH)
                cp.start()
                copies.append(cp)
        for cp in copies:
            cp.wait()

    return pl.pallas_call(
        body, name=name,
        in_specs=[ANY] * K, out_specs=[ANY] * K,
        out_shape=[jax.ShapeDtypeStruct(a.shape, a.dtype) for a in bufs],
        input_output_aliases={k: k for k in range(K)},
        scratch_shapes=[pltpu.SemaphoreType.DMA((K * per,)), pltpu.SemaphoreType.DMA((K * per,))],
    )(*bufs)


def _chip_exchange_copies(srcs, lands, send_sems, recv_sems):
    x, y, c, chips = _mesh_position()
    per = N_CHIPS - 1
    return [pltpu.make_async_remote_copy(
        src_ref=srcs[k].at[2 * chip[0] + chip[1]], dst_ref=lands[k].at[j],
        send_sem=send_sems.at[k * per + j], recv_sem=recv_sems.at[k * per + j],
        device_id=(*chip, c), device_id_type=MESH) for k in range(len(srcs)) for j, chip in enumerate(chips)]


def _sibling_swap_copies(srcs, lands, send_sems, recv_sems):
    x, y, c, _ = _mesh_position()
    return [pltpu.make_async_remote_copy(
        src_ref=srcs[k].at[pl.ds(0, srcs[k].shape[0]), 1 - c], dst_ref=lands[k],
        send_sem=send_sems.at[k], recv_sem=recv_sems.at[k],
        device_id=(x, y, 1 - c), device_id_type=MESH) for k in range(len(srcs))]


def _split_copy_start(srcs, land_shapes, n_sems, copies, *, name):
    K = len(srcs)

    def body(*refs):
        for cp in copies(refs[:K], refs[K:2 * K], refs[2 * K], refs[2 * K + 1]):
            cp.start()
        refs[-1][...] = jnp.zeros_like(refs[-1])

    out = pl.pallas_call(
        body, name=name,
        in_specs=[HBM] * (2 * K),
        out_specs=[SEM, SEM] + [HBM] * (2 * K) + [pl.BlockSpec(memory_space=pltpu.VMEM)],
        out_shape=[pltpu.SemaphoreType.DMA((n_sems,))] * 2
        + [pltpu.HBM(a.shape, a.dtype) for a in srcs]
        + [pltpu.HBM(s, a.dtype) for s, a in zip(land_shapes, srcs)] + [TOKEN],
        input_output_aliases={k: 2 + k for k in range(2 * K)},
        compiler_params=SPLIT_COPY,
    )(*[_in_hbm(a) for a in srcs], *[_in_hbm(lax.empty(s, a.dtype)) for s, a in zip(land_shapes, srcs)])
    return (out[0], out[1], list(out[2:2 + K]), list(out[2 + K:2 + 2 * K])), out[-1]


def _split_copy_wait(handle, copies, after, *, name):
    send, recv, srcs, lands = handle
    K = len(srcs)

    def body(*refs):
        for cp in copies(refs[:K], refs[K:2 * K], refs[2 * K], refs[2 * K + 1]):
            cp.wait_send()
            cp.wait_recv()

    out = pl.pallas_call(
        body, name=name,
        in_specs=[HBM] * (2 * K) + [SEM, SEM, ANY], out_specs=[HBM] * (2 * K),
        out_shape=[pltpu.HBM(a.shape, a.dtype) for a in srcs] + [pltpu.HBM(a.shape, a.dtype) for a in lands],
        input_output_aliases={k: k for k in range(2 * K)},
        compiler_params=SPLIT_COPY,
    )(*srcs, *lands, send, recv, after)
    return list(out[:K]), list(out[K:])


def _join_copies(bufs, send_sems, recv_sems):
    x, y, c, _ = _mesh_position()
    copies = []
    for k, buf in enumerate(bufs):
        r2 = buf.shape[0] // 2
        mine = buf.at[pl.ds(c * r2, r2), :]
        copies.append(pltpu.make_async_remote_copy(
            src_ref=mine, dst_ref=mine, send_sem=send_sems.at[k], recv_sem=recv_sems.at[k],
            device_id=(x, y, 1 - c), device_id_type=MESH))
    return copies


def _join_start(bufs, *, name):
    K = len(bufs)

    def body(*refs):
        for cp in _join_copies(refs[:K], refs[K], refs[K + 1]):
            cp.start()
        refs[-1][...] = jnp.zeros_like(refs[-1])

    out = pl.pallas_call(
        body, name=name,
        in_specs=[HBM] * K,
        out_specs=[SEM, SEM] + [HBM] * K + [pl.BlockSpec(memory_space=pltpu.VMEM)],
        out_shape=[pltpu.SemaphoreType.DMA((K,))] * 2 + [pltpu.HBM(a.shape, a.dtype) for a in bufs] + [TOKEN],
        input_output_aliases={k: 2 + k for k in range(K)},
        compiler_params=SPLIT_COPY,
    )(*[_in_hbm(a) for a in bufs])
    return (out[0], out[1], list(out[2:2 + K])), out[-1]


def _join_wait(handle, after, *, name):
    send, recv, bufs = handle
    K = len(bufs)

    def body(*refs):
        for cp in _join_copies(refs[:K], refs[K], refs[K + 1]):
            cp.wait_send()
            cp.wait_recv()

    return pl.pallas_call(
        body, name=name,
        in_specs=[HBM] * K + [SEM, SEM, ANY], out_specs=[HBM] * K,
        out_shape=[pltpu.HBM(a.shape, a.dtype) for a in bufs],
        input_output_aliases={k: k for k in range(K)},
        compiler_params=SPLIT_COPY,
    )(*bufs, send, recv, after)


def _all_gather_slab(slab):
    m_per, n = slab.shape

    def body(x_ref, out_ref, send_sems, recv_sems, local_sem):
        x, y, c, chips = _mesh_position()
        me, sibling = (x, y, c), (x, y, 1 - c)

        def rows(px, py, pc):
            return out_ref.at[pl.ds((4 * px + 2 * py + pc) * m_per, m_per), :]

        def copy(k, block, to, src=None):
            return pltpu.make_async_remote_copy(
                src_ref=rows(*block) if src is None else src, dst_ref=rows(*block),
                send_sem=send_sems.at[k], recv_sem=recv_sems.at[k], device_id=to, device_id_type=MESH)

        mine = pltpu.make_async_copy(x_ref, rows(*me), local_sem)
        mine.start()
        first = [copy(0, me, sibling, src=x_ref)]
        first += [copy(1 + j, me, (*chip, c), src=x_ref) for j, chip in enumerate(chips)]
        for cp in first:
            cp.start()
        passed = [copy(4 + j, (*chip, c), sibling) for j, chip in enumerate(chips)]
        for j, chip in enumerate(chips):
            copy(1 + j, (*chip, c), me).wait_recv()
            passed[j].start()
        copy(0, sibling, me).wait_recv()
        for j, chip in enumerate(chips):
            copy(4 + j, (*chip, 1 - c), me).wait_recv()
        for cp in first + passed:
            cp.wait_send()
        mine.wait()

    return pl.pallas_call(
        body, name="gather_small_grads",
        out_shape=jax.ShapeDtypeStruct((N_DEV * m_per, n), slab.dtype),
        in_specs=[pl.BlockSpec(memory_space=pltpu.VMEM)],
        out_specs=pl.BlockSpec(memory_space=pltpu.VMEM),
        scratch_shapes=[pltpu.SemaphoreType.DMA((7,)), pltpu.SemaphoreType.DMA((7,)), pltpu.SemaphoreType.DMA],
    )(slab)


def _ffn_backward_weights(dhb, saved, w_in, w_out, after, tag, on_dw_out=None):
    nt, fa, a = saved
    S = nt.shape[1]
    ns, D, cs = w_in.shape
    F = w_out.shape[0]
    to = 512
    dw_out = _mm_tn(
        a, dhb, grid=(F // to,),
        a_spec=pl.BlockSpec((S, to), lambda j: (0, j)), b_spec=pl.BlockSpec((S, D), lambda j: (0, 0)),
        out_spec=pl.BlockSpec((to, D), lambda j: (j, 0)), out_shape=jax.ShapeDtypeStruct((F, D), BF16),
        scale=0.5, name=f"{tag}_dw_out").reshape(N_CHIPS, F // N_CHIPS, D)
    if on_dw_out is not None:
        after = on_dw_out(dw_out)
    dgu = _ffn_bwd_act(dhb, w_out, fa, after, name=f"{tag}_bwd_act")
    ti, tr = SHARD_TILE, D // 2
    per_g, per_s = F // ti, cs // ti
    dw_in = _mm_tn(
        nt, dgu, grid=(D // tr, 2 * F // ti), a_is_transposed=True,
        a_spec=pl.BlockSpec((tr, S), lambda i, j: (i, 0)),
        b_spec=pl.BlockSpec((None, S, ti), lambda i, j: (lax.div(j, per_g), 0, lax.rem(j, per_g))),
        out_spec=pl.BlockSpec((None, tr, ti), lambda i, j: (lax.div(j, per_s), i, lax.rem(j, per_s))),
        out_shape=jax.ShapeDtypeStruct((ns, D, cs), BF16), scale=1.0, name=f"{tag}_dw_in")
    return dgu, dw_in, dw_out


def kernel(x, ffn1_norm, ffn1_w_in, ffn1_w_out, mix_norm, w_in_mix, w_pool, pool_scale, w_alpha, b_alpha, gla_norm, w_out_mix, ffn2_norm, ffn2_w_in, ffn2_w_out, final_norm, loss_target, m_ffn1_norm, m_ffn1_w_in, m_ffn1_w_out, m_mix_norm, m_w_in_mix, m_w_pool, m_pool_scale, m_w_alpha, m_b_alpha, m_gla_norm, m_w_out_mix, m_ffn2_norm, m_ffn2_w_in, m_ffn2_w_out, m_final_norm, v_ffn1_norm, v_ffn1_w_in, v_ffn1_w_out, v_mix_norm, v_w_in_mix, v_w_pool, v_pool_scale, v_w_alpha, v_b_alpha, v_gla_norm, v_w_out_mix, v_ffn2_norm, v_ffn2_w_in, v_ffn2_w_out, v_final_norm):
    names = ["ffn1_norm", "ffn1_w_in", "ffn1_w_out", "mix_norm", "w_in_mix", "w_pool", "pool_scale", "w_alpha",
             "b_alpha", "gla_norm", "w_out_mix", "ffn2_norm", "ffn2_w_in", "ffn2_w_out", "final_norm"]
    weights = dict(zip(names, [ffn1_norm, ffn1_w_in, ffn1_w_out, mix_norm, w_in_mix, w_pool, pool_scale, w_alpha,
                               b_alpha, gla_norm, w_out_mix, ffn2_norm, ffn2_w_in, ffn2_w_out, final_norm]))
    moms = dict(zip(names, [m_ffn1_norm, m_ffn1_w_in, m_ffn1_w_out, m_mix_norm, m_w_in_mix, m_w_pool, m_pool_scale,
                            m_w_alpha, m_b_alpha, m_gla_norm, m_w_out_mix, m_ffn2_norm, m_ffn2_w_in, m_ffn2_w_out,
                            m_final_norm]))
    vels = dict(zip(names, [v_ffn1_norm, v_ffn1_w_in, v_ffn1_w_out, v_mix_norm, v_w_in_mix, v_w_pool, v_pool_scale,
                            v_w_alpha, v_b_alpha, v_gla_norm, v_w_out_mix, v_ffn2_norm, v_ffn2_w_in, v_ffn2_w_out,
                            v_final_norm]))
    xi, yi, ci = lax.axis_index("x"), lax.axis_index("y"), lax.axis_index("c")
    chip = 2 * xi + yi
    c_idx = jnp.reshape(ci, (1,)).astype(jnp.int32)
    sc_idx = jnp.stack([chip, ci]).astype(jnp.int32)

    def flat2d(a):
        return a.reshape(-1, a.shape[-1])

    ex = _Exchanges(sc_idx, c_idx)

    def cast(n, after):
        return _cast_into_slot(flat2d(weights[n]), sc_idx, F32 if n == "w_alpha" else BF16, after, name=f"cast_{n}")

    groups = _Exchanges.GATHER_GROUPS
    w1_in = groups[0][0]
    tok = ex.start_gather({w1_in: cast(w1_in, sc_idx)}, groups[:1], ("first_hop",), name="gather_ici_start_first")
    bufs, last = {}, tok
    for g in groups[1:]:
        for n in g:
            bufs[n] = last = cast(n, last)
    bufs[w1_in] = ex.arrived(w1_in, after=last)[w1_in]
    tok2 = ex.start_gather(bufs, groups, ("second_hop",) + ("direct",) * (len(groups) - 1),
                           name="gather_ici_start_rest")
    early = [flat2d(moms["w_in_mix"]), flat2d(vels["w_in_mix"])]
    small_params = dict(g1=ffn1_norm, gm=mix_norm, g2=ffn2_norm, gf=final_norm.reshape(1, D_MODEL),
                        pool_scale=pool_scale, b_alpha=b_alpha, gla_norm=gla_norm, early=early)
    loss_blk, dx, small = _forward_backward(x[0], loss_target[0], ex, tok[0, 0] + tok2[0, 0], small_params)

    outs = {}

    def update(n, g, after):
        w = weights[n]
        w2 = flat2d(w) if w.ndim > 1 else w.reshape(1, -1)
        go, d, nm, nv = _adamw(w2, g.reshape(w2.shape), moms[n].reshape(w2.shape), vels[n].reshape(w2.shape), after,
                               name=f"adamw_{n}")
        outs[n] = (go.reshape(w.shape), d.reshape(w.shape), nm.reshape(w.shape), nv.reshape(w.shape))
        return nv

    tags = _Exchanges.REDUCE_ORDER
    last = ex.finish_exchange(tags[0], after=dx)
    for prev, tag in zip(tags, tags[1:]):
        last = ex.finish_exchange(tag, after=last)
        for n, g in ex.reduced(prev, after=last).items():
            last = update(n, g, last)

    grads = {}
    small_names = ["ffn1_norm", "mix_norm", "ffn2_norm", "final_norm", "pool_scale", "b_alpha", "gla_norm", "w_alpha",
                   "loss"]
    small = small + [loss_blk[0:1]]
    rows = [a.size // LANES for a in small]
    slab = jnp.concatenate([a.reshape(-1, LANES) for a in small], axis=0)
    pad = -slab.shape[0] % 8
    slab = jnp.pad(slab, ((0, pad), (0, 0)))
    gathered = _all_gather_slab(slab).reshape(N_DEV, slab.shape[0], LANES)
    total = _slab_sum(gathered, name="sum_small_grads")
    off = 0
    for n, a, r in zip(small_names, small, rows):
        grads[n] = total[off:off + r].reshape(a.shape)
        off += r
    grads["w_alpha"] = lax.dynamic_slice_in_dim(grads["w_alpha"], chip * (GLA_DK_TOTAL // N_CHIPS),
                                                GLA_DK_TOTAL // N_CHIPS, axis=1)

    loss = grads.pop("loss")[0, 0]
    for n in small_names[:-1]:
        last = update(n, grads[n], last)
    for n, g in ex.reduced(tags[-1], after=last).items():
        last = update(n, g, last)
    return (loss, dx[None], *[outs[n][0] for n in names], *[outs[n][1] for n in names],
            *[outs[n][2] for n in names], *[outs[n][3] for n in names])


class _Exchanges:
    GATHER_GROUPS = (("ffn1_w_in",), ("ffn1_w_out",), ("w_in_mix", "w_pool", "w_alpha"), ("w_out_mix",),
                     ("ffn2_w_in",), ("ffn2_w_out",))
    REDUCE_ORDER = ("ffn2", "mix", "ffn1_out", "ffn1_in")

    def __init__(self, sc_idx, c_idx):
        self.sc_idx, self.c_idx = sc_idx, c_idx
        self._gathers, self._swaps, self._reduces, self._joins = {}, {}, {}, {}

    def start_gather(self, bufs, groups, kinds, *, name):
        handles, token = _gather_ici_start([[bufs[n] for n in g] for g in groups], kinds, name=name)
        for g, h in zip(groups, handles):
            self._gathers[g[0]] = (g, h)
        return token

    def arrived(self, first, after):
        names, handle = self._gathers.pop(first)
        return dict(zip(names, _gather_ici_wait(handle, after, name=f"gather_ici_wait_{first}_{handle[3]}")))

    def gathered(self, first, after):
        got = self.arrived(first, after)
        return dict(zip(got, _forward_halves(list(got.values()), name=f"gather_forward_{first}")))

    def begin_reduce(self, tag, full):
        g4 = [a.reshape(N_CHIPS, 2, a.shape[1] // 2, a.shape[2]) for a in full.values()]
        lands = [(a.shape[0],) + a.shape[2:] for a in g4]
        handle, token = _split_copy_start(g4, lands, len(g4), _sibling_swap_copies, name=f"swap_start_{tag}")
        self._swaps[tag] = (list(full), handle)
        return token

    def start_reduce(self, tag, after):
        names, handle = self._swaps.pop(tag)
        g4, from_sibling = _split_copy_wait(handle, _sibling_swap_copies, after, name=f"swap_wait_{tag}")
        pair = [_pair_sum(a, b, self.c_idx, name=f"pair_sum_{n}") for n, a, b in zip(names, g4, from_sibling)]
        lands = [(N_CHIPS - 1,) + a.shape[1:] for a in pair]
        handle, token = _split_copy_start(pair, lands, len(pair) * (N_CHIPS - 1), _chip_exchange_copies,
                                          name=f"exchange_start_{tag}")
        self._reduces[tag] = (names, handle)
        return token

    def finish_exchange(self, tag, after):
        names, handle = self._reduces.pop(tag)
        pair, lands = _split_copy_wait(handle, _chip_exchange_copies, after, name=f"exchange_wait_{tag}")
        halves = [_chip_sum(a, b, self.sc_idx, name=f"chip_sum_{n}") for n, a, b in zip(names, pair, lands)]
        handle, token = _join_start(halves, name=f"join_start_{tag}")
        self._joins[tag] = (names, handle)
        return token

    def reduced(self, tag, after):
        names, handle = self._joins.pop(tag)
        return dict(zip(names, _join_wait(handle, after, name=f"join_wait_{tag}")))


def _forward_backward(h0, target, ex, started, sp):
    g1, gm, g2, gf = sp["g1"], sp["gm"], sp["g2"], sp["gf"]
    pool_scale, b_alpha, gla_norm = sp["pool_scale"], sp["b_alpha"], sp["gla_norm"]
    cs_mix = D_IN // N_CHIPS

    n1, n1t = _rms_fwd(h0, g1 + started, sp["early"], name="ffn1_norm")
    w1_in = ex.gathered("ffn1_w_in", after=n1)["ffn1_w_in"]
    fa1, a1 = _ffn_up(n1, w1_in, [], name="ffn1_up")
    w1_out = ex.gathered("ffn1_w_out", after=a1)["ffn1_w_out"].reshape(D_FF, D_MODEL)
    h1 = _mm_nn(a1, w1_out, h0, 0.5, tm=512, tn=D_MODEL, tk=SHARD_TILE, name="ffn1_down")
    saved1 = (n1t, fa1, a1)
    n_mix, n_mixt = _rms_fwd(h1, gm, [], name="mix_norm")
    gw = ex.gathered("w_in_mix", after=n_mix)
    w_mix = _concat_shards(gw["w_in_mix"], D_IN_PAD, name="w_mix_concat")[None]
    wp = gw["w_pool"].reshape(N_CHIPS, 4, POOL_GROUP_DIM // N_CHIPS, POOL_GROUP_DIM)
    wp = wp.transpose(1, 0, 2, 3).reshape(4, POOL_GROUP_DIM, POOL_GROUP_DIM)
    wa = gw["w_alpha"].transpose(1, 0, 2).reshape(GLA_GATE_RANK, GLA_DK_TOTAL)
    wa = jnp.pad(wa, ((0, LANES - GLA_GATE_RANK), (0, 0))).astype(BF16)
    u = _mm_nn(n_mix, w_mix[0], None, 1.0, tm=512, tn=D_IN_PAD, tk=D_MODEL, name="mix_in")
    y_pool = _pool_fwd(u, wp, pool_scale, name="pool_fwd")
    cat, o_gla, states = _gla_fwd(u, y_pool, wa, b_alpha, gla_norm, name="gla_fwd")
    w_omix = ex.gathered("w_out_mix", after=cat)["w_out_mix"].reshape(D_MODEL, D_MODEL)
    h2 = _mm_nn(cat, w_omix, h1, 1.0, tm=512, tn=D_MODEL, tk=1024, name="mix_out")
    n3, n3t = _rms_fwd(h2, g2, [], name="ffn2_norm")
    w2_in = ex.gathered("ffn2_w_in", after=n3)["ffn2_w_in"]
    fa2, a2 = _ffn_up(n3, w2_in, [], name="ffn2_up")
    w2_out = ex.gathered("ffn2_w_out", after=a2)["ffn2_w_out"].reshape(D_FF, D_MODEL)
    h3 = _mm_nn(a2, w2_out, h2, 0.5, tm=512, tn=D_MODEL, tk=SHARD_TILE, name="ffn2_down")
    saved2 = (n3t, fa2, a2)
    loss_blk, dh3, dh3b, d_gf = _final_loss(h3, gf, target, name="final_loss")

    dgu2, dw2_in, dw2_out = _ffn_backward_weights(dh3b, saved2, w2_in, w2_out, dh3b, "ffn2")
    tok = ex.begin_reduce("ffn2", {"ffn2_w_in": dw2_in, "ffn2_w_out": dw2_out})
    dh2, dh2b, d_g2 = _mm_nt_rmsbwd(dgu2, w2_in, h2, dh3, g2 + tok[0, 0], tk=SHARD_TILE, name="ffn2_dx")
    tok = ex.start_reduce("ffn2", after=dh2b)
    S = h0.shape[0]
    dcat = _mm_nt(dh2b, w_omix, tok, tm=512, tn=1024, name="mix_out_dx")
    dw_omix = _mm_tn(
        cat, dh2b, grid=(4,),
        a_spec=pl.BlockSpec((S, 512), lambda j: (0, j)), b_spec=pl.BlockSpec((S, D_MODEL), lambda j: (0, 0)),
        out_spec=pl.BlockSpec((512, D_MODEL), lambda j: (j, 0)),
        out_shape=jax.ShapeDtypeStruct((D_MODEL, D_MODEL), BF16), scale=1.0, name="mix_out_dw")
    dp, dw_pool, d_pscale = _pool_bwd(u, dcat, wp, pool_scale, name="pool_bwd")
    du, d_wa, d_ba, d_gn = _gla_bwd(u, o_gla, states, dcat, dp, wa, b_alpha, gla_norm, name="gla_bwd")
    du = du[None]
    tn_mix = COL_TILE
    dw_mix = _mm_tn(
        n_mixt, du, grid=(2, D_IN_PAD // tn_mix), a_is_transposed=True,
        a_spec=pl.BlockSpec((D_MODEL // 2, S), lambda i, j: (i, 0)),
        b_spec=pl.BlockSpec((None, S, tn_mix), lambda i, j: (0, 0, j)),
        out_spec=pl.BlockSpec((D_MODEL // 2, tn_mix), lambda i, j: (i, j)),
        out_shape=jax.ShapeDtypeStruct((D_MODEL, D_IN_PAD), BF16), scale=1.0, name="mix_in_dw")
    dw_mix_s = _split_shards(dw_mix, cs_mix, name="dw_mix_split")
    dw_pool_s = dw_pool.reshape(4, N_CHIPS, POOL_GROUP_DIM // N_CHIPS, POOL_GROUP_DIM).transpose(1, 0, 2, 3)
    dw_pool_s = dw_pool_s.reshape(N_CHIPS, POOL_GROUP_DIM, POOL_GROUP_DIM).astype(BF16)
    tok = ex.begin_reduce("mix", {"w_in_mix": dw_mix_s,
                                  "w_out_mix": dw_omix.reshape(N_CHIPS, D_MODEL // N_CHIPS, D_MODEL),
                                  "w_pool": dw_pool_s})
    dh1, dh1b, d_gm = _mm_nt_rmsbwd(du, w_mix, h1, dh2, gm + tok[0, 0], tk=tn_mix, name="mix_in_dx")
    tok = ex.start_reduce("mix", after=dh1b)
    mix_started = tok
    dgu1, dw1_in, _ = _ffn_backward_weights(
        dh1b, saved1, w1_in, w1_out, None, "ffn1",
        on_dw_out=lambda dw: mix_started + ex.begin_reduce("ffn1_out", {"ffn1_w_out": dw}))
    out_started = ex.start_reduce("ffn1_out", after=dgu1)
    tok = ex.start_reduce("ffn1_in", after=out_started + ex.begin_reduce("ffn1_in", {"ffn1_w_in": dw1_in}))
    dx, _, d_g1 = _mm_nt_rmsbwd(dgu1, w1_in, h0, dh1, g1 + tok[0, 0], tk=SHARD_TILE, name="ffn1_dx")
    small = [d_g1, d_gm, d_g2, d_gf, d_pscale, d_ba, d_gn, d_wa[:GLA_GATE_RANK]]
    return loss_blk, dx, small
```

```python
import functools

import jax
import jax.numpy as jnp
from jax import lax
from jax.experimental import pallas as pl
from jax.experimental.pallas import tpu as pltpu

F32 = jnp.float32
BF16 = jnp.bfloat16
MESH = pl.DeviceIdType.MESH

D_MODEL = 2048
D_FF = 5632
D_POOL = 1024
POOL_WINDOWS = (2, 4, 8, 16)
POOL_GROUP_DIM = 256
D_GLA = 1024
GLA_HEADS = 4
GLA_DV = 256
GLA_DK = 128
GLA_DK_TOTAL = 512
GLA_GATE_RANK = 16
GATE_LOGIT_NORMALIZER = 16.0
CHUNK = 64
D_IN = 4112
D_IN_PAD = 4224
EPS = 1e-6

ADAM_LR = 0.001
ADAM_B1 = 0.9
ADAM_B2 = 0.999
ADAM_EPS = 1e-08
ADAM_WD = 0.01
ADAM_STEP = 10

N_CHIPS = 4
N_DEV = 8
V7X_VMEM_BYTES = 64 * 1024 * 1024
LANES = 128
MXU_TILE = 256
COL_TILE = 1408
SHARD_TILE = 2816


def _cparams(semantics, vmem_mb):
    assert vmem_mb * 1024 * 1024 < V7X_VMEM_BYTES
    return pltpu.CompilerParams(dimension_semantics=semantics, vmem_limit_bytes=vmem_mb * 1024 * 1024)


def _dot_nn(a, b):
    return jnp.dot(a, b, preferred_element_type=F32)


def _dot_nt(a, b):
    return lax.dot_general(a, b, (((1,), (1,)), ((), ())), preferred_element_type=F32)


def _dot_tn(a, b):
    return lax.dot_general(a, b, (((0,), (0,)), ((), ())), preferred_element_type=F32)


def _sigmoid(x):
    return 1.0 / (1.0 + jnp.exp(-x))


def _rms_fwd(x, g, early, *, name):
    S, D = x.shape
    tm = 256

    def body(x_ref, g_ref, *rest):
        o_ref, ot_ref = rest[len(early):]
        xv = x_ref[...]
        r = lax.rsqrt(jnp.mean(xv * xv, axis=-1, keepdims=True) + EPS)
        n = xv * r * g_ref[...]
        o_ref[...] = n.astype(BF16)
        ot_ref[...] = n.T.astype(BF16)

    return pl.pallas_call(
        body, name=name, grid=(S // tm,),
        in_specs=[pl.BlockSpec((tm, D), lambda i: (i, 0)), pl.BlockSpec((1, D), lambda i: (0, 0))]
        + [pl.BlockSpec(memory_space=pl.ANY)] * len(early),
        out_specs=[pl.BlockSpec((tm, D), lambda i: (i, 0)), pl.BlockSpec((D, tm), lambda i: (0, i))],
        out_shape=[jax.ShapeDtypeStruct((S, D), BF16), jax.ShapeDtypeStruct((D, S), BF16)],
        compiler_params=_cparams(("parallel",), 32),
    )(x, g, *early)


def _ffn_up(n, w_in, early, *, name):
    S, D = n.shape
    ns, _, cs = w_in.shape
    half = ns // 2
    F = cs * half
    tm, tn = 128, SHARD_TILE
    nb = cs // tn

    def body(n_ref, wg_ref, wu_ref, *rest):
        fa_ref, a_ref = rest[len(early):]
        nv = n_ref[...]
        g = _dot_nn(nv, wg_ref[...])
        u = _dot_nn(nv, wu_ref[...])
        s = _sigmoid(g)
        silu = g * s
        fa_ref[0] = (u * (s * (1.0 + g * (1.0 - s)))).astype(BF16)
        fa_ref[1] = silu.astype(BF16)
        a_ref[...] = (silu * u).astype(BF16)

    return pl.pallas_call(
        body, name=name, grid=(F // tn, S // tm),
        in_specs=[
            pl.BlockSpec((tm, D), lambda j, i: (i, 0)),
            pl.BlockSpec((None, D, tn), lambda j, i: (lax.div(j, nb), 0, lax.rem(j, nb))),
            pl.BlockSpec((None, D, tn), lambda j, i: (half + lax.div(j, nb), 0, lax.rem(j, nb))),
        ] + [pl.BlockSpec(memory_space=pl.ANY)] * len(early),
        out_specs=[
            pl.BlockSpec((2, tm, tn), lambda j, i: (0, i, j)),
            pl.BlockSpec((tm, tn), lambda j, i: (i, j)),
        ],
        out_shape=[jax.ShapeDtypeStruct((2, S, F), BF16), jax.ShapeDtypeStruct((S, F), BF16)],
        compiler_params=_cparams(("parallel", "parallel"), 56),
    )(n, w_in, w_in, *early)


def _mm_nn(a, b, resid, scale, *, tm, tn, tk, name):
    S, K = a.shape
    N = b.shape[1]
    nk = K // tk

    def body(*refs):
        if resid is None:
            a_ref, b_ref, o_ref, acc_ref = refs
            r_ref = None
        else:
            a_ref, b_ref, r_ref, o_ref, acc_ref = refs
        k = pl.program_id(2)

        @pl.when(k == 0)
        def _():
            acc_ref[...] = jnp.zeros_like(acc_ref)

        acc_ref[...] += _dot_nn(a_ref[...], b_ref[...])

        @pl.when(k == nk - 1)
        def _():
            out = acc_ref[...] * scale
            if r_ref is not None:
                out = r_ref[...] + out
            o_ref[...] = out

    in_specs = [pl.BlockSpec((tm, tk), lambda i, j, k: (i, k)), pl.BlockSpec((tk, tn), lambda i, j, k: (k, j))]
    args = [a, b]
    if resid is not None:
        in_specs.append(pl.BlockSpec((tm, tn), lambda i, j, k: (i, j)))
        args.append(resid)
    return pl.pallas_call(
        body, name=name, grid=(S // tm, N // tn, nk),
        in_specs=in_specs,
        out_specs=pl.BlockSpec((tm, tn), lambda i, j, k: (i, j)),
        out_shape=jax.ShapeDtypeStruct((S, N), F32),
        scratch_shapes=[pltpu.VMEM((tm, tn), F32)],
        compiler_params=_cparams(("parallel", "parallel", "arbitrary"), 48),
    )(*args)


def _mm_nt(a, b, after, *, tm, tn, name):
    S, K = a.shape
    N = b.shape[0]

    def body(a_ref, b_ref, after_ref, o_ref):
        o_ref[...] = _dot_nt(a_ref[...], b_ref[...])

    return pl.pallas_call(
        body, name=name, grid=(N // tn, S // tm),
        in_specs=[pl.BlockSpec((tm, K), lambda j, i: (i, 0)), pl.BlockSpec((tn, K), lambda j, i: (j, 0)),
                  pl.BlockSpec(memory_space=pl.ANY)],
        out_specs=pl.BlockSpec((tm, tn), lambda j, i: (i, j)),
        out_shape=jax.ShapeDtypeStruct((S, N), F32),
        compiler_params=_cparams(("parallel", "parallel"), 48),
    )(a, b, after)


def _mm_tn(a, b, *, grid, a_spec, b_spec, out_spec, out_shape, scale, name, a_is_transposed=False, add=None):
    dot = _dot_nn if a_is_transposed else _dot_tn

    def body(a_ref, b_ref, *rest):
        o_ref = rest[-1]
        acc = scale * dot(a_ref[...], b_ref[...])
        if add is not None:
            acc = acc + rest[0][...].astype(F32)
        o_ref[...] = acc.astype(o_ref.dtype)

    extra = [] if add is None else [add]
    return pl.pallas_call(
        body, name=name, grid=grid, in_specs=[a_spec, b_spec] + [out_spec] * len(extra), out_specs=out_spec,
        out_shape=out_shape, compiler_params=_cparams(("parallel",) * len(grid), 56),
    )(a, b, *extra)


def _ffn_bwd_act(dhb, w_out, fa, after, *, name):
    S, D = dhb.shape
    F = w_out.shape[0]
    tm, tn = 256, SHARD_TILE

    def body(dh_ref, w_ref, fa_ref, after_ref, dgu_ref):
        da = 0.5 * _dot_nt(dh_ref[...], w_ref[...])
        dgu_ref[0] = (da * fa_ref[0].astype(F32)).astype(BF16)
        dgu_ref[1] = (da * fa_ref[1].astype(F32)).astype(BF16)

    return pl.pallas_call(
        body, name=name, grid=(F // tn, S // tm),
        in_specs=[
            pl.BlockSpec((tm, D), lambda j, i: (i, 0)),
            pl.BlockSpec((tn, D), lambda j, i: (j, 0)),
            pl.BlockSpec((2, tm, tn), lambda j, i: (0, i, j)),
            pl.BlockSpec(memory_space=pl.ANY),
        ],
        out_specs=pl.BlockSpec((2, tm, tn), lambda j, i: (0, i, j)),
        out_shape=jax.ShapeDtypeStruct((2, S, F), BF16),
        compiler_params=_cparams(("parallel", "parallel"), 56),
    )(dhb, w_out, fa, after)


def _mm_nt_rmsbwd(dact, w, h_in, dh_out, g, *, tk, name):
    ng, S, fg = dact.shape
    ns, D, cs = w.shape
    assert ng * fg == ns * cs
    tm, rc = 512, 64
    kpg, kps = fg // tk, cs // tk
    nk = ng * kpg

    def body(a_ref, w_ref, h_ref, dho_ref, g_ref, dh_ref, dhb_ref, dg_ref, acc_ref):
        i = pl.program_id(0)
        k = pl.program_id(1)

        @pl.when(k == 0)
        def _():
            acc_ref[...] = jnp.zeros_like(acc_ref)

        acc_ref[...] += _dot_nt(a_ref[...], w_ref[...])

        @pl.when(jnp.logical_and(i == 0, k == 0))
        def _():
            dg_ref[...] = jnp.zeros_like(dg_ref)

        @pl.when(k == nk - 1)
        def _():
            gv = g_ref[...]

            def rows_step(c, dg):
                rows = pl.ds(pl.multiple_of(c * rc, rc), rc)
                dn = acc_ref[rows, :]
                xv = h_ref[rows, :]
                r = lax.rsqrt(jnp.mean(xv * xv, axis=-1, keepdims=True) + EPS)
                xh = xv * r
                dng = dn * gv
                dx = r * (dng - xh * jnp.mean(dng * xh, axis=-1, keepdims=True))
                out = dho_ref[rows, :] + dx
                dh_ref[rows, :] = out
                dhb_ref[rows, :] = out.astype(BF16)
                return dg + jnp.sum(dn * xh, axis=0, keepdims=True)

            dg_ref[...] += lax.fori_loop(0, tm // rc, rows_step, jnp.zeros((1, D), F32))

    return pl.pallas_call(
        body, name=name, grid=(S // tm, nk),
        in_specs=[
            pl.BlockSpec((None, tm, tk), lambda i, k: (lax.div(k, kpg), i, lax.rem(k, kpg))),
            pl.BlockSpec((None, D, tk), lambda i, k: (lax.div(k, kps), 0, lax.rem(k, kps))),
            pl.BlockSpec((tm, D), lambda i, k: (i, 0)),
            pl.BlockSpec((tm, D), lambda i, k: (i, 0)),
            pl.BlockSpec((1, D), lambda i, k: (0, 0)),
        ],
        out_specs=[
            pl.BlockSpec((tm, D), lambda i, k: (i, 0), pipeline_mode=pl.Buffered(1)),
            pl.BlockSpec((tm, D), lambda i, k: (i, 0), pipeline_mode=pl.Buffered(1)),
            pl.BlockSpec((1, D), lambda i, k: (0, 0)),
        ],
        out_shape=[jax.ShapeDtypeStruct((S, D), F32), jax.ShapeDtypeStruct((S, D), BF16),
                   jax.ShapeDtypeStruct((1, D), F32)],
        scratch_shapes=[pltpu.VMEM((tm, D), F32)],
        compiler_params=_cparams(("arbitrary", "arbitrary"), 56),
    )(dact, w, h_in, dh_out, g)


def _final_loss(h, g, target, *, name):
    S, D = h.shape
    tm = 256

    def body(h_ref, g_ref, t_ref, loss_ref, dh_ref, dhb_ref, dg_ref):
        i = pl.program_id(0)

        @pl.when(i == 0)
        def _():
            loss_ref[...] = jnp.zeros_like(loss_ref)
            dg_ref[...] = jnp.zeros_like(dg_ref)

        xv = h_ref[...]
        gv = g_ref[...]
        r = lax.rsqrt(jnp.mean(xv * xv, axis=-1, keepdims=True) + EPS)
        xh = xv * r
        e = xh * gv - t_ref[...]
        loss_ref[...] += 0.5 * jnp.sum(jnp.mean(e * e, axis=-1, keepdims=True))
        dy = e * (1.0 / D)
        dg_ref[...] += jnp.sum(dy * xh, axis=0, keepdims=True)
        dyg = dy * gv
        dx = r * (dyg - xh * jnp.mean(dyg * xh, axis=-1, keepdims=True))
        dh_ref[...] = dx
        dhb_ref[...] = dx.astype(BF16)

    return pl.pallas_call(
        body, name=name, grid=(S // tm,),
        in_specs=[pl.BlockSpec((tm, D), lambda i: (i, 0)), pl.BlockSpec((1, D), lambda i: (0, 0)),
                  pl.BlockSpec((tm, D), lambda i: (i, 0))],
        out_specs=[pl.BlockSpec((8, LANES), lambda i: (0, 0)), pl.BlockSpec((tm, D), lambda i: (i, 0)),
                   pl.BlockSpec((tm, D), lambda i: (i, 0)), pl.BlockSpec((1, D), lambda i: (0, 0))],
        out_shape=[jax.ShapeDtypeStruct((8, LANES), F32), jax.ShapeDtypeStruct((S, D), F32),
                   jax.ShapeDtypeStruct((S, D), BF16), jax.ShapeDtypeStruct((1, D), F32)],
        compiler_params=_cparams(("arbitrary",), 40),
    )(h, g, target)


POOL_HALO = 16
POOL_ROWS = 256


def _pool_window_mean_minus_token(ext, tok0, w):
    s = ext
    k = 1
    while k < w:
        s = s + pltpu.roll(s, k, 0)
        k *= 2
    win = s[POOL_HALO:, :]
    tok = tok0 + lax.broadcasted_iota(jnp.int32, (POOL_ROWS, 1), 0)
    cnt = jnp.minimum(tok + 1, w).astype(F32)
    return win / cnt - ext[POOL_HALO:, :], cnt


def _pool_fwd(u, w_pool, scale, *, name):
    S = u.shape[0]
    C = POOL_GROUP_DIM
    nsteps = S // POOL_ROWS

    def body(p_ref, w_ref, sc_ref, y_ref, xp_ref):
        xp_ref[0:POOL_HALO, :] = jnp.zeros((POOL_HALO, D_POOL), F32)
        xp_ref[POOL_HALO:, :] = p_ref[...]
        for gi, win in enumerate(POOL_WINDOWS):
            cols = slice(gi * C, (gi + 1) * C)

            def step(c, carry, cols=cols, win=win, gi=gi):
                r0 = pl.multiple_of(c * POOL_ROWS, POOL_ROWS)
                ext = xp_ref[pl.ds(r0, POOL_ROWS + POOL_HALO), cols]
                pooled, _ = _pool_window_mean_minus_token(ext, r0, win)
                y = _dot_nn(pooled.astype(BF16), w_ref[gi]) * sc_ref[:, cols]
                y_ref[pl.ds(r0, POOL_ROWS), cols] = y.astype(BF16)
                return carry

            lax.fori_loop(0, nsteps, step, 0)

    return pl.pallas_call(
        body, name=name, grid=(1,),
        in_specs=[pl.BlockSpec((S, D_POOL), lambda i: (0, 0)),
                  pl.BlockSpec((4, C, C), lambda i: (0, 0, 0)),
                  pl.BlockSpec((1, D_POOL), lambda i: (0, 0))],
        out_specs=pl.BlockSpec((S, D_POOL), lambda i: (0, 0)),
        out_shape=jax.ShapeDtypeStruct((S, D_POOL), BF16),
        scratch_shapes=[pltpu.VMEM((S + POOL_HALO, D_POOL), F32)],
        compiler_params=_cparams(("arbitrary",), 48),
    )(u, w_pool, scale)


def _pool_bwd(u, dcat, w_pool, scale, *, name):
    S = u.shape[0]
    C = POOL_GROUP_DIM
    nsteps = S // POOL_ROWS

    def body(p_ref, dy_ref, w_ref, sc_ref, dp_ref, dw_ref, dsc_ref, xp_ref, e_ref, neg_ref):
        xp_ref[0:POOL_HALO, :] = jnp.zeros((POOL_HALO, D_POOL), F32)
        xp_ref[POOL_HALO:, :] = p_ref[...]
        e_ref[S:, :] = jnp.zeros((POOL_HALO, C), F32)
        for gi, win in enumerate(POOL_WINDOWS):
            cols = slice(gi * C, (gi + 1) * C)

            def step_a(c, carry, cols=cols, win=win, gi=gi):
                dw, dsc = carry
                r0 = pl.multiple_of(c * POOL_ROWS, POOL_ROWS)
                ext = xp_ref[pl.ds(r0, POOL_ROWS + POOL_HALO), cols]
                pooled, cnt = _pool_window_mean_minus_token(ext, r0, win)
                pb = pooled.astype(BF16)
                wv = w_ref[gi]
                dy = dy_ref[pl.ds(r0, POOL_ROWS), cols]
                dsc = dsc + jnp.sum(dy * _dot_nn(pb, wv), axis=0, keepdims=True)
                dyp = (dy * sc_ref[:, cols]).astype(BF16)
                dw = dw + _dot_tn(pb, dyp)
                dpooled = _dot_nt(dyp, wv)
                e_ref[pl.ds(r0, POOL_ROWS), :] = dpooled / cnt
                neg_ref[pl.ds(r0, POOL_ROWS), :] = -dpooled
                return dw, dsc

            dw, dsc = lax.fori_loop(0, nsteps, step_a, (jnp.zeros((C, C), F32), jnp.zeros((1, C), F32)))
            dw_ref[gi] = dw
            dsc_ref[:, cols] = dsc

            def step_b(c, carry, cols=cols, win=win):
                r0 = pl.multiple_of(c * POOL_ROWS, POOL_ROWS)
                s = e_ref[pl.ds(r0, POOL_ROWS + POOL_HALO), :]
                n = POOL_ROWS + POOL_HALO
                k = 1
                while k < win:
                    s = s + pltpu.roll(s, n - k, 0)
                    k *= 2
                du = s[:POOL_ROWS, :] + neg_ref[pl.ds(r0, POOL_ROWS), :]
                dp_ref[pl.ds(r0, POOL_ROWS), cols] = du.astype(BF16)
                return carry

            lax.fori_loop(0, nsteps, step_b, 0)

    return pl.pallas_call(
        body, name=name, grid=(1,),
        in_specs=[pl.BlockSpec((S, D_POOL), lambda i: (0, 0)),
                  pl.BlockSpec((S, D_POOL), lambda i: (0, 0)),
                  pl.BlockSpec((4, C, C), lambda i: (0, 0, 0)),
                  pl.BlockSpec((1, D_POOL), lambda i: (0, 0))],
        out_specs=[pl.BlockSpec((S, D_POOL), lambda i: (0, 0)),
                   pl.BlockSpec((4, C, C), lambda i: (0, 0, 0)),
                   pl.BlockSpec((1, D_POOL), lambda i: (0, 0))],
        out_shape=[jax.ShapeDtypeStruct((S, D_POOL), BF16), jax.ShapeDtypeStruct((4, C, C), F32),
                   jax.ShapeDtypeStruct((1, D_POOL), F32)],
        scratch_shapes=[pltpu.VMEM((S + POOL_HALO, D_POOL), F32), pltpu.VMEM((S + POOL_HALO, C), F32),
                        pltpu.VMEM((S, C), F32)],
        compiler_params=_cparams(("arbitrary",), 56),
    )(u, dcat, w_pool, scale)


GLA_ROWS = 128
U_Q_BLK, U_K_BLK = 2, 3
U_V_BLK, U_G_BLK = 2, 3
U_R_BLK = 32


def _prefix_sum_rows(x):
    n = x.shape[0]
    row = lax.broadcasted_iota(jnp.int32, x.shape, 0)
    k = 1
    while k < n:
        x = x + jnp.where(row >= k, pltpu.roll(x, k, 0), 0.0)
        k *= 2
    return x


def _suffix_sum_rows(x):
    n = x.shape[0]
    row = lax.broadcasted_iota(jnp.int32, x.shape, 0)
    k = 1
    while k < n:
        x = x + jnp.where(row < n - k, pltpu.roll(x, n - k, 0), 0.0)
        k *= 2
    return x


def _log_sigmoid(z):
    return jnp.minimum(z, 0.0) - jnp.log(1.0 + jnp.exp(-jnp.abs(z)))


def _gla_chunk_terms(la_c, q_c, k_c):
    bc = _prefix_sum_rows(la_c)
    bl = jnp.sum(la_c, axis=0, keepdims=True)
    eb = jnp.exp(bc)
    enb = jnp.exp(-bc)
    etail = jnp.exp(bl - bc)
    qd = q_c * (GLA_DK ** -0.5) * eb
    ki = k_c * enb
    kt = k_c * etail
    d = jnp.exp(bl)
    return eb, enb, etail, qd, ki, kt, d


def _gla_fwd(u, y_pool, w_alpha, b_alpha, gnorm, *, name):
    S = u.shape[0]
    RB = GLA_ROWS
    ncc = RB // CHUNK
    H, DK, DV = GLA_HEADS, GLA_DK, GLA_DV

    def body(q_ref, k_ref, v_ref, go_ref, r_ref, yp_ref, wa_ref, ba_ref, gn_ref, cat_ref, o_ref, st_ref, state):
        i = pl.program_id(0)

        @pl.when(i == 0)
        def _():
            state[...] = jnp.zeros_like(state)

        cat_ref[:, :D_POOL] = yp_ref[...]
        y_ref = cat_ref.at[:, D_POOL:]

        z = _dot_nn(r_ref[...].astype(BF16), wa_ref[...]) + ba_ref[...]
        la = _log_sigmoid(z) / GATE_LOGIT_NORMALIZER
        ri = lax.broadcasted_iota(jnp.int32, (CHUNK, CHUNK), 0)
        ci = lax.broadcasted_iota(jnp.int32, (CHUNK, CHUNK), 1)
        tri = ri >= ci
        gn = gn_ref[...]
        for cc in range(ncc):
            rs = slice(cc * CHUNK, (cc + 1) * CHUNK)
            for h in range(H):
                ks = slice(h * DK, (h + 1) * DK)
                vs = slice(h * DV, (h + 1) * DV)
                _, _, _, qd, ki, kt, d = _gla_chunk_terms(la[rs, ks], q_ref[rs, ks], k_ref[rs, ks])
                qdb = qd.astype(BF16)
                vb = v_ref[rs, vs].astype(BF16)
                p = jnp.where(tri, _dot_nt(qdb, ki.astype(BF16)), 0.0)
                st = state[h]
                st_ref[cc, h] = st
                o = _dot_nn(p.astype(BF16), vb) + _dot_nt(qdb, st.astype(BF16))
                state[h] = st * d + _dot_tn(vb, kt.astype(BF16))
                o_ref[rs, vs] = o
                rinv = lax.rsqrt(jnp.mean(o * o, axis=-1, keepdims=True) + EPS)
                go = go_ref[rs, vs]
                y_ref[rs, vs] = (o * rinv * gn * (go * _sigmoid(go))).astype(BF16)

    nblk = S // RB
    return pl.pallas_call(
        body, name=name, grid=(nblk,),
        in_specs=[
            pl.BlockSpec((RB, GLA_DK_TOTAL), lambda i: (i, U_Q_BLK)),
            pl.BlockSpec((RB, GLA_DK_TOTAL), lambda i: (i, U_K_BLK)),
            pl.BlockSpec((RB, D_GLA), lambda i: (i, U_V_BLK)),
            pl.BlockSpec((RB, D_GLA), lambda i: (i, U_G_BLK)),
            pl.BlockSpec((RB, LANES), lambda i: (i, U_R_BLK)),
            pl.BlockSpec((RB, D_POOL), lambda i: (i, 0)),
            pl.BlockSpec((LANES, GLA_DK_TOTAL), lambda i: (0, 0)),
            pl.BlockSpec((1, GLA_DK_TOTAL), lambda i: (0, 0)),
            pl.BlockSpec((1, DV), lambda i: (0, 0)),
        ],
        out_specs=[
            pl.BlockSpec((RB, D_POOL + D_GLA), lambda i: (i, 0)),
            pl.BlockSpec((RB, D_GLA), lambda i: (i, 0)),
            pl.BlockSpec((ncc, H, DV, DK), lambda i: (i, 0, 0, 0)),
        ],
        out_shape=[jax.ShapeDtypeStruct((S, D_POOL + D_GLA), BF16), jax.ShapeDtypeStruct((S, D_GLA), F32),
                   jax.ShapeDtypeStruct((S // CHUNK, H, DV, DK), F32)],
        scratch_shapes=[pltpu.VMEM((H, DV, DK), F32)],
        compiler_params=_cparams(("arbitrary",), 32),
    )(u, u, u, u, u, y_pool, w_alpha, b_alpha, gnorm)


def _gla_bwd(u, o, states, dcat, dpool, w_alpha, b_alpha, gnorm, *, name):
    S = u.shape[0]
    RB = GLA_ROWS
    ncc = RB // CHUNK
    H, DK, DV = GLA_HEADS, GLA_DK, GLA_DV
    nblk = S // RB
    o_q, o_k = D_POOL, D_POOL + GLA_DK_TOTAL
    o_v, o_g, o_r = o_k + GLA_DK_TOTAL, o_k + GLA_DK_TOTAL + D_GLA, o_k + GLA_DK_TOTAL + 2 * D_GLA

    def body(q_ref, k_ref, v_ref, go_ref, r_ref, o_ref, st_ref, dy_ref, dpool_ref, wa_ref, ba_ref, gn_ref,
             du_ref, dwa_ref, dba_ref, dgn_ref, dstate, dz_ref):
        i = pl.program_id(0)

        @pl.when(i == 0)
        def _():
            dstate[...] = jnp.zeros_like(dstate)
            dwa_ref[...] = jnp.zeros_like(dwa_ref)
            dba_ref[...] = jnp.zeros_like(dba_ref)
            dgn_ref[...] = jnp.zeros_like(dgn_ref)

        du_ref[:, :o_q] = dpool_ref[...]
        dq_ref, dk_ref = du_ref.at[:, o_q:o_k], du_ref.at[:, o_k:o_v]
        dv_ref, dgo_ref, dr_ref = du_ref.at[:, o_v:o_g], du_ref.at[:, o_g:o_r], du_ref.at[:, o_r:]

        rb = r_ref[...].astype(BF16)
        wa = wa_ref[...]
        z = _dot_nn(rb, wa) + ba_ref[...]
        la = _log_sigmoid(z) / GATE_LOGIT_NORMALIZER
        ri = lax.broadcasted_iota(jnp.int32, (CHUNK, CHUNK), 0)
        ci = lax.broadcasted_iota(jnp.int32, (CHUNK, CHUNK), 1)
        tri = ri >= ci
        last_row = lax.broadcasted_iota(jnp.int32, (CHUNK, DK), 0) == CHUNK - 1
        gn = gn_ref[...]
        dgn = jnp.zeros((1, DV), F32)
        for cc in reversed(range(ncc)):
            rs = slice(cc * CHUNK, (cc + 1) * CHUNK)
            for h in range(H):
                ks = slice(h * DK, (h + 1) * DK)
                vs = slice(h * DV, (h + 1) * DV)
                eb, enb, etail, qd, ki, kt, d = _gla_chunk_terms(la[rs, ks], q_ref[rs, ks], k_ref[rs, ks])
                qdb, kib, ktb = qd.astype(BF16), ki.astype(BF16), kt.astype(BF16)
                vb = v_ref[rs, vs].astype(BF16)
                p = jnp.where(tri, _dot_nt(qdb, kib), 0.0)
                ov = o_ref[rs, vs]
                go = go_ref[rs, vs]
                dy = dy_ref[rs, vs]
                rinv = lax.rsqrt(jnp.mean(ov * ov, axis=-1, keepdims=True) + EPS)
                oh = ov * rinv
                sg = _sigmoid(go)
                dgo_ref[rs, vs] = (dy * (oh * gn) * (sg * (1.0 + go * (1.0 - sg)))).astype(BF16)
                don = dy * (go * sg)
                dgn = dgn + jnp.sum(don * oh, axis=0, keepdims=True)
                doh = don * gn
                do = rinv * (doh - oh * jnp.mean(doh * oh, axis=-1, keepdims=True))
                dob = do.astype(BF16)
                st = st_ref[cc, h]
                dst = dstate[h]
                stb, dstb = st.astype(BF16), dst.astype(BF16)
                dp = jnp.where(tri, _dot_nt(dob, vb), 0.0).astype(BF16)
                dv_ref[rs, vs] = (_dot_tn(p.astype(BF16), dob) + _dot_nt(ktb, dstb)).astype(BF16)
                dqd = _dot_nn(dp, kib) + _dot_nn(dob, stb)
                dki = _dot_tn(dp, qdb)
                dkt = _dot_nn(vb, dstb)
                dd = jnp.sum(dst * st, axis=0, keepdims=True)
                dstate[h] = dst * d + _dot_tn(dob, qdb)
                dq_ref[rs, ks] = (dqd * eb * (DK ** -0.5)).astype(BF16)
                dk_ref[rs, ks] = (dki * enb + dkt * etail).astype(BF16)
                dbl = jnp.sum(dkt * kt, axis=0, keepdims=True) + dd * d
                dbc = dqd * qd - dki * ki - dkt * kt
                dbc = dbc + jnp.where(last_row, dbl, 0.0)
                dla = _suffix_sum_rows(dbc)
                dz_ref[rs, ks] = dla * (1.0 / GATE_LOGIT_NORMALIZER) * (1.0 - _sigmoid(z[rs, ks]))
        dz = dz_ref[...]
        dzb = dz.astype(BF16)
        dr_ref[...] = _dot_nt(dzb, wa).astype(BF16)
        dwa_ref[...] += _dot_tn(rb, dzb)
        dba_ref[...] += jnp.sum(dz, axis=0, keepdims=True)
        dgn_ref[...] += dgn

    def rev(blk):
        return lambda i: (nblk - 1 - i, blk)

    return pl.pallas_call(
        body, name=name, grid=(nblk,),
        in_specs=[
            pl.BlockSpec((RB, GLA_DK_TOTAL), rev(U_Q_BLK)),
            pl.BlockSpec((RB, GLA_DK_TOTAL), rev(U_K_BLK)),
            pl.BlockSpec((RB, D_GLA), rev(U_V_BLK)),
            pl.BlockSpec((RB, D_GLA), rev(U_G_BLK)),
            pl.BlockSpec((RB, LANES), rev(U_R_BLK)),
            pl.BlockSpec((RB, D_GLA), rev(0)),
            pl.BlockSpec((ncc, H, DV, DK), lambda i: (nblk - 1 - i, 0, 0, 0)),
            pl.BlockSpec((RB, D_GLA), rev(1)),
            pl.BlockSpec((RB, D_POOL), rev(0)),
            pl.BlockSpec((LANES, GLA_DK_TOTAL), lambda i: (0, 0)),
            pl.BlockSpec((1, GLA_DK_TOTAL), lambda i: (0, 0)),
            pl.BlockSpec((1, DV), lambda i: (0, 0)),
        ],
        out_specs=[
            pl.BlockSpec((RB, D_IN_PAD), rev(0)),
            pl.BlockSpec((LANES, GLA_DK_TOTAL), lambda i: (0, 0)),
            pl.BlockSpec((1, GLA_DK_TOTAL), lambda i: (0, 0)),
            pl.BlockSpec((1, DV), lambda i: (0, 0)),
        ],
        out_shape=[
            jax.ShapeDtypeStruct((S, D_IN_PAD), BF16),
            jax.ShapeDtypeStruct((LANES, GLA_DK_TOTAL), F32), jax.ShapeDtypeStruct((1, GLA_DK_TOTAL), F32),
            jax.ShapeDtypeStruct((1, DV), F32),
        ],
        scratch_shapes=[pltpu.VMEM((H, DV, DK), F32), pltpu.VMEM((RB, GLA_DK_TOTAL), F32)],
        compiler_params=_cparams(("arbitrary",), 32),
    )(u, u, u, u, u, o, states, dcat, dpool, w_alpha, b_alpha, gnorm)


def _concat_shards(w4, width, *, name):
    ns, R, cs = w4.shape
    tr = 256

    def body(w_ref, o_ref):
        for t in range(ns):
            o_ref[:, t * cs:(t + 1) * cs] = w_ref[t]
        o_ref[:, ns * cs:] = jnp.zeros((tr, width - ns * cs), o_ref.dtype)

    return pl.pallas_call(
        body, name=name, grid=(R // tr,),
        in_specs=[pl.BlockSpec((ns, tr, cs), lambda i: (0, i, 0))],
        out_specs=pl.BlockSpec((tr, width), lambda i: (i, 0)),
        out_shape=jax.ShapeDtypeStruct((R, width), w4.dtype),
        compiler_params=_cparams(("parallel",), 32),
    )(w4)


def _split_shards(a, cs, *, name):
    R, width = a.shape
    tr = 256

    def body(a_ref, o_ref):
        for t in range(N_CHIPS):
            o_ref[t] = a_ref[:, t * cs:(t + 1) * cs]

    return pl.pallas_call(
        body, name=name, grid=(R // tr,),
        in_specs=[pl.BlockSpec((tr, width), lambda i: (i, 0))],
        out_specs=pl.BlockSpec((N_CHIPS, tr, cs), lambda i: (0, i, 0)),
        out_shape=jax.ShapeDtypeStruct((N_CHIPS, R, cs), a.dtype),
        compiler_params=_cparams(("parallel",), 32),
    )(a)


def _row_tile(rows, cols, itemsize, budget=2 * 1024 * 1024):
    if rows * cols * itemsize <= budget or rows % 16:
        return rows
    best = 16
    for t in range(16, rows + 1, 16):
        if rows % t == 0 and t * cols * itemsize <= budget:
            best = t
    return best


def _adamw(w, g, m, v, after, *, name):
    R, C = w.shape
    tr = _row_tile(R, C, 4)

    def body(w_ref, g_ref, m_ref, v_ref, after_ref, go_ref, d_ref, nm_ref, nv_ref):
        gv = g_ref[...]
        go_ref[...] = gv
        mn = ADAM_B1 * m_ref[...] + (1.0 - ADAM_B1) * gv
        vn = ADAM_B2 * v_ref[...] + (1.0 - ADAM_B2) * jnp.square(gv)
        m_hat = mn / (1.0 - ADAM_B1 ** ADAM_STEP)
        v_hat = vn / (1.0 - ADAM_B2 ** ADAM_STEP)
        d_ref[...] = -ADAM_LR * (m_hat / (jnp.sqrt(v_hat) + ADAM_EPS) + ADAM_WD * w_ref[...])
        nm_ref[...] = mn
        nv_ref[...] = vn

    spec = pl.BlockSpec((tr, C), lambda i: (i, 0))
    shp = jax.ShapeDtypeStruct((R, C), F32)
    return pl.pallas_call(
        body, name=name, grid=(R // tr,), in_specs=[spec] * 4 + [pl.BlockSpec(memory_space=pl.ANY)],
        out_specs=[spec] * 4, out_shape=[shp] * 4,
        compiler_params=_cparams(("parallel",), 48),
    )(w, g, m, v, after)


def _pair_sum(g4, recv, c_idx, *, name):
    ns, _, R2, C = g4.shape
    tr = _row_tile(R2, C, 2)

    def body(c_ref, g_ref, r_ref, o_ref):
        o_ref[...] = (g_ref[...].astype(F32) + r_ref[...].astype(F32)).astype(BF16)

    return pl.pallas_call(
        body, name=name,
        grid_spec=pltpu.PrefetchScalarGridSpec(
            num_scalar_prefetch=1, grid=(ns, R2 // tr),
            in_specs=[pl.BlockSpec((None, None, tr, C), lambda s, i, c: (s, c[0], i, 0)),
                      pl.BlockSpec((None, tr, C), lambda s, i, c: (s, i, 0))],
            out_specs=pl.BlockSpec((None, tr, C), lambda s, i, c: (s, i, 0)),
        ),
        out_shape=jax.ShapeDtypeStruct((ns, R2, C), BF16),
        compiler_params=_cparams(("parallel", "parallel"), 32),
    )(c_idx, g4, recv)


def _chip_sum(part, recv, sc_idx, *, name):
    _, R2, C = part.shape
    tr = _row_tile(R2, C, 4)
    nblk = R2 // tr

    def body(s_ref, p_ref, r_ref, o_ref):
        acc = p_ref[...].astype(F32)
        for j in range(N_CHIPS - 1):
            acc = acc + r_ref[j].astype(F32)
        o_ref[...] = acc

    return pl.pallas_call(
        body, name=name,
        grid_spec=pltpu.PrefetchScalarGridSpec(
            num_scalar_prefetch=1, grid=(nblk,),
            in_specs=[pl.BlockSpec((None, tr, C), lambda i, s: (s[0], i, 0)),
                      pl.BlockSpec((N_CHIPS - 1, tr, C), lambda i, s: (0, i, 0))],
            out_specs=pl.BlockSpec((tr, C), lambda i, s: (s[1] * nblk + i, 0)),
        ),
        out_shape=jax.ShapeDtypeStruct((2 * R2, C), F32),
        compiler_params=_cparams(("parallel",), 32),
    )(sc_idx, part, recv)


def _cast_into_slot(w, sc_idx, dtype, after, *, name):
    R, C = w.shape
    tr = _row_tile(R, C, 4)

    def body(s_ref, w_ref, after_ref, o_ref):
        o_ref[...] = w_ref[...].astype(dtype)

    return pl.pallas_call(
        body, name=name,
        grid_spec=pltpu.PrefetchScalarGridSpec(
            num_scalar_prefetch=1, grid=(R // tr,),
            in_specs=[pl.BlockSpec((tr, C), lambda i, s: (i, 0)), pl.BlockSpec(memory_space=pl.ANY)],
            out_specs=pl.BlockSpec((None, tr, C), lambda i, s: (s[0], i, 0)),
        ),
        out_shape=jax.ShapeDtypeStruct((N_CHIPS, R, C), dtype),
        compiler_params=_cparams(("parallel",), 32),
    )(sc_idx, w, after)


def _slab_sum(slabs, *, name):
    n, M, C = slabs.shape

    def body(x_ref, o_ref):
        acc = x_ref[0]
        for d in range(1, n):
            acc = acc + x_ref[d]
        o_ref[...] = acc

    return pl.pallas_call(
        body, name=name, out_shape=jax.ShapeDtypeStruct((M, C), F32),
    )(slabs)


def _mesh_position():
    x, y, c = lax.axis_index("x"), lax.axis_index("y"), lax.axis_index("c")
    other_chips = [(1 - x, y), (x, 1 - y), (1 - x, 1 - y)]
    return x, y, c, other_chips


ANY = pl.BlockSpec(memory_space=pl.ANY)


HBM = pl.BlockSpec(memory_space=pltpu.HBM)
SEM = pl.BlockSpec(memory_space=pltpu.SEMAPHORE)
SPLIT_COPY = pltpu.CompilerParams(has_side_effects=pltpu.SideEffectType.DATAFLOW_SIDE_EFFECTING)
TOKEN = jax.ShapeDtypeStruct((8, LANES), F32)


def _in_hbm(a):
    return pltpu.with_memory_space_constraint(a, pltpu.HBM)


def _half_rows(ref, slot, half):
    hr = ref.shape[1] // 2
    return ref.at[slot, pl.ds(half * hr, hr), :]


GATHER_COPIES = {"direct": 3, "first_hop": 2, "second_hop": 1}


def _gather_routes(kind):
    x, y, c, chips = _mesh_position()
    me = 2 * x + y
    first = (x + (1 - c) * (1 - 2 * x), y + c * (1 - 2 * y))
    second = (x + c * (1 - 2 * x), y + (1 - c) * (1 - 2 * y))
    if kind == "direct":
        return [(me, (*p, c)) for p in chips], [2 * p[0] + p[1] for p in chips]
    if kind == "first_hop":
        return [(me, (*first, c)), (me, (*second, c))], [2 * first[0] + first[1], 2 * second[0] + second[1]]
    assert kind == "second_hop"
    return [(2 * first[0] + first[1], (*second, c))], [2 * (1 - x) + (1 - y)]


def _gather_ici_start(groups, kinds, *, name):
    flat = [b for g in groups for b in g]
    K, G = len(flat), len(groups)

    def body(*refs):
        ins, sems, token = refs[:K], refs[K:K + 2 * G], refs[-1]
        c = lax.axis_index("c")
        k = 0
        for gi, (g, kind) in enumerate(zip(groups, kinds)):
            sends, _ = _gather_routes(kind)
            for n in range(len(g)):
                for j, (slot, to) in enumerate(sends):
                    part = _half_rows(ins[k], slot, c)
                    pltpu.make_async_remote_copy(
                        src_ref=part, dst_ref=part, send_sem=sems[2 * gi].at[n * len(sends) + j],
                        recv_sem=sems[2 * gi + 1].at[n * len(sends) + j],
                        device_id=to, device_id_type=MESH).start()
                k += 1
        token[...] = jnp.zeros_like(token)

    sem_shapes = []
    for g, kind in zip(groups, kinds):
        sem_shapes += [pltpu.SemaphoreType.DMA((len(g) * GATHER_COPIES[kind],))] * 2
    out = pl.pallas_call(
        body, name=name,
        in_specs=[HBM] * K,
        out_specs=[SEM] * (2 * G) + [HBM] * K + [pl.BlockSpec(memory_space=pltpu.VMEM)],
        out_shape=sem_shapes + [pltpu.HBM(b.shape, b.dtype) for b in flat] + [TOKEN],
        input_output_aliases={k: 2 * G + k for k in range(K)},
        compiler_params=SPLIT_COPY,
    )(*[_in_hbm(b) for b in flat])
    handles, k = [], 2 * G
    for gi, (g, kind) in enumerate(zip(groups, kinds)):
        handles.append((out[2 * gi], out[2 * gi + 1], list(out[k:k + len(g)]), kind))
        k += len(g)
    return handles, out[-1]


def _gather_ici_wait(handle, after, *, name):
    send, recv, bufs, kind = handle
    n = len(bufs)

    def body(*refs):
        ins, send_ref, recv_ref = refs[:n], refs[n], refs[n + 1]
        c = lax.axis_index("c")
        sends, arrivals = _gather_routes(kind)
        for k in range(n):
            for j, ((slot, to), landed) in enumerate(zip(sends, arrivals)):
                cp = pltpu.make_async_remote_copy(
                    src_ref=_half_rows(ins[k], slot, c), dst_ref=_half_rows(ins[k], landed, c),
                    send_sem=send_ref.at[k * len(sends) + j], recv_sem=recv_ref.at[k * len(sends) + j],
                    device_id=to, device_id_type=MESH)
                cp.wait_send()
                cp.wait_recv()

    return pl.pallas_call(
        body, name=name,
        in_specs=[HBM] * n + [SEM, SEM, ANY], out_specs=[HBM] * n,
        out_shape=[pltpu.HBM(b.shape, b.dtype) for b in bufs],
        input_output_aliases={k: k for k in range(n)},
        compiler_params=SPLIT_COPY,
    )(*bufs, send, recv, after)


def _forward_halves(bufs, *, name):
    K = len(bufs)
    per = N_CHIPS - 1

    def body(*refs):
        outs = refs[K:2 * K]
        send_sems, recv_sems = refs[2 * K:]
        x, y, c, chips = _mesh_position()
        copies = []
        for k in range(K):
            for j, chip in enumerate(chips):
                got = _half_rows(outs[k], 2 * chip[0] + chip[1], c)
                cp = pltpu.make_async_remote_copy(
                    src_ref=got, dst_ref=got, send_sem=send_sems.at[k * per + j], recv_sem=recv_sems.at[k * per + j],
                    device_id=(x, y, 1 - c), device_id_type=MESH)
                cp.start()
                copies.append(cp)
        for cp in copies:
            cp.wait()

    return pl.pallas_call(
        body, name=name,
        in_specs=[ANY] * K, out_specs=[ANY] * K,
        out_shape=[jax.ShapeDtypeStruct(a.shape, a.dtype) for a in bufs],
        input_output_aliases={k: k for k in range(K)},
        scratch_shapes=[pltpu.SemaphoreType.DMA((K * per,)), pltpu.SemaphoreType.DMA((K * per,))],
    )(*bufs)


def _chip_exchange_copies(srcs, lands, send_sems, recv_sems):
    x, y, c, chips = _mesh_position()
    per = N_CHIPS - 1
    return [pltpu.make_async_remote_copy(
        src_ref=srcs[k].at[2 * chip[0] + chip[1]], dst_ref=lands[k].at[j],
        send_sem=send_sems.at[k * per + j], recv_sem=recv_sems.at[k * per + j],
        device_id=(*chip, c), device_id_type=MESH) for k in range(len(srcs)) for j, chip in enumerate(chips)]


def _sibling_swap_copies(srcs, lands, send_sems, recv_sems):
    x, y, c, _ = _mesh_position()
    return [pltpu.make_async_remote_copy(
        src_ref=srcs[k].at[pl.ds(0, srcs[k].shape[0]), 1 - c], dst_ref=lands[k],
        send_sem=send_sems.at[k], recv_sem=recv_sems.at[k],
        device_id=(x, y, 1 - c), device_id_type=MESH) for k in range(len(srcs))]


def _sibling_send_copies(srcs, lands, send_sems, recv_sems):
    x, y, c, _ = _mesh_position()
    return [pltpu.make_async_remote_copy(
        src_ref=srcs[k], dst_ref=lands[k], send_sem=send_sems.at[k], recv_sem=recv_sems.at[k],
        device_id=(x, y, 1 - c), device_id_type=MESH) for k in range(len(srcs))]


def _split_copy_start(srcs, land_shapes, n_sems, copies, *, name):
    K = len(srcs)

    def body(*refs):
        for cp in copies(refs[:K], refs[K:2 * K], refs[2 * K], refs[2 * K + 1]):
            cp.start()
        refs[-1][...] = jnp.zeros_like(refs[-1])

    out = pl.pallas_call(
        body, name=name,
        in_specs=[HBM] * (2 * K),
        out_specs=[SEM, SEM] + [HBM] * (2 * K) + [pl.BlockSpec(memory_space=pltpu.VMEM)],
        out_shape=[pltpu.SemaphoreType.DMA((n_sems,))] * 2
        + [pltpu.HBM(a.shape, a.dtype) for a in srcs]
        + [pltpu.HBM(s, a.dtype) for s, a in zip(land_shapes, srcs)] + [TOKEN],
        input_output_aliases={k: 2 + k for k in range(2 * K)},
        compiler_params=SPLIT_COPY,
    )(*[_in_hbm(a) for a in srcs], *[_in_hbm(lax.empty(s, a.dtype)) for s, a in zip(land_shapes, srcs)])
    return (out[0], out[1], list(out[2:2 + K]), list(out[2 + K:2 + 2 * K])), out[-1]


def _split_copy_wait(handle, copies, after, *, name):
    send, recv, srcs, lands = handle
    K = len(srcs)

    def body(*refs):
        for cp in copies(refs[:K], refs[K:2 * K], refs[2 * K], refs[2 * K + 1]):
            cp.wait_send()
            cp.wait_recv()

    out = pl.pallas_call(
        body, name=name,
        in_specs=[HBM] * (2 * K) + [SEM, SEM, ANY], out_specs=[HBM] * (2 * K),
        out_shape=[pltpu.HBM(a.shape, a.dtype) for a in srcs] + [pltpu.HBM(a.shape, a.dtype) for a in lands],
        input_output_aliases={k: k for k in range(2 * K)},
        compiler_params=SPLIT_COPY,
    )(*srcs, *lands, send, recv, after)
    return list(out[:K]), list(out[K:])


def _join_copies(bufs, send_sems, recv_sems):
    x, y, c, _ = _mesh_position()
    copies = []
    for k, buf in enumerate(bufs):
        r2 = buf.shape[0] // 2
        mine = buf.at[pl.ds(c * r2, r2), :]
        copies.append(pltpu.make_async_remote_copy(
            src_ref=mine, dst_ref=mine, send_sem=send_sems.at[k], recv_sem=recv_sems.at[k],
            device_id=(x, y, 1 - c), device_id_type=MESH))
    return copies


def _join_start(bufs, *, name):
    K = len(bufs)

    def body(*refs):
        for cp in _join_copies(refs[:K], refs[K], refs[K + 1]):
            cp.start()
        refs[-1][...] = jnp.zeros_like(refs[-1])

    out = pl.pallas_call(
        body, name=name,
        in_specs=[HBM] * K,
        out_specs=[SEM, SEM] + [HBM] * K + [pl.BlockSpec(memory_space=pltpu.VMEM)],
        out_shape=[pltpu.SemaphoreType.DMA((K,))] * 2 + [pltpu.HBM(a.shape, a.dtype) for a in bufs] + [TOKEN],
        input_output_aliases={k: 2 + k for k in range(K)},
        compiler_params=SPLIT_COPY,
    )(*[_in_hbm(a) for a in bufs])
    return (out[0], out[1], list(out[2:2 + K])), out[-1]


def _join_wait(handle, after, *, name):
    send, recv, bufs = handle
    K = len(bufs)

    def body(*refs):
        for cp in _join_copies(refs[:K], refs[K], refs[K + 1]):
            cp.wait_send()
            cp.wait_recv()

    return pl.pallas_call(
        body, name=name,
        in_specs=[HBM] * K + [SEM, SEM, ANY], out_specs=[HBM] * K,
        out_shape=[pltpu.HBM(a.shape, a.dtype) for a in bufs],
        input_output_aliases={k: k for k in range(K)},
        compiler_params=SPLIT_COPY,
    )(*bufs, send, recv, after)


def _all_gather_slab(slab):
    m_per, n = slab.shape

    def body(x_ref, out_ref, send_sems, recv_sems, local_sem):
        x, y, c, chips = _mesh_position()
        me, sibling = (x, y, c), (x, y, 1 - c)

        def rows(px, py, pc):
            return out_ref.at[pl.ds((4 * px + 2 * py + pc) * m_per, m_per), :]

        def copy(k, block, to, src=None):
            return pltpu.make_async_remote_copy(
                src_ref=rows(*block) if src is None else src, dst_ref=rows(*block),
                send_sem=send_sems.at[k], recv_sem=recv_sems.at[k], device_id=to, device_id_type=MESH)

        mine = pltpu.make_async_copy(x_ref, rows(*me), local_sem)
        mine.start()
        first = [copy(0, me, sibling, src=x_ref)]
        first += [copy(1 + j, me, (*chip, c), src=x_ref) for j, chip in enumerate(chips)]
        for cp in first:
            cp.start()
        passed = [copy(4 + j, (*chip, c), sibling) for j, chip in enumerate(chips)]
        for j, chip in enumerate(chips):
            copy(1 + j, (*chip, c), me).wait_recv()
            passed[j].start()
        copy(0, sibling, me).wait_recv()
        for j, chip in enumerate(chips):
            copy(4 + j, (*chip, 1 - c), me).wait_recv()
        for cp in first + passed:
            cp.wait_send()
        mine.wait()

    return pl.pallas_call(
        body, name="gather_small_grads",
        out_shape=jax.ShapeDtypeStruct((N_DEV * m_per, n), slab.dtype),
        in_specs=[pl.BlockSpec(memory_space=pltpu.VMEM)],
        out_specs=pl.BlockSpec(memory_space=pltpu.VMEM),
        scratch_shapes=[pltpu.SemaphoreType.DMA((7,)), pltpu.SemaphoreType.DMA((7,)), pltpu.SemaphoreType.DMA],
    )(slab)


def _ffn_dw_out(dhb, a, tag):
    S, F = a.shape
    D = dhb.shape[1]
    to = 512
    return _mm_tn(
        a, dhb, grid=(F // to,),
        a_spec=pl.BlockSpec((S, to), lambda j: (0, j)), b_spec=pl.BlockSpec((S, D), lambda j: (0, 0)),
        out_spec=pl.BlockSpec((to, D), lambda j: (j, 0)), out_shape=jax.ShapeDtypeStruct((F, D), BF16),
        scale=0.5, name=f"{tag}_dw_out").reshape(N_CHIPS, F // N_CHIPS, D)


def _ffn_dw_in_half(nt, dgu, half, add, w_in_shape, *, name):
    ns, D, cs = w_in_shape
    S = nt.shape[1]
    F = dgu.shape[2]
    ti, tr = SHARD_TILE, D // 2
    per_g, per_s = F // ti, cs // ti
    return _mm_tn(
        lax.dynamic_slice_in_dim(nt, half * tr, tr, axis=0), dgu, grid=(2 * F // ti,), a_is_transposed=True,
        a_spec=pl.BlockSpec((tr, S), lambda j: (0, 0)),
        b_spec=pl.BlockSpec((None, S, ti), lambda j: (lax.div(j, per_g), 0, lax.rem(j, per_g))),
        out_spec=pl.BlockSpec((None, tr, ti), lambda j: (lax.div(j, per_s), 0, lax.rem(j, per_s))),
        out_shape=jax.ShapeDtypeStruct((ns, tr, cs), BF16), scale=1.0, name=name, add=add)


def kernel(x, ffn1_norm, ffn1_w_in, ffn1_w_out, mix_norm, w_in_mix, w_pool, pool_scale, w_alpha, b_alpha, gla_norm, w_out_mix, ffn2_norm, ffn2_w_in, ffn2_w_out, final_norm, loss_target, m_ffn1_norm, m_ffn1_w_in, m_ffn1_w_out, m_mix_norm, m_w_in_mix, m_w_pool, m_pool_scale, m_w_alpha, m_b_alpha, m_gla_norm, m_w_out_mix, m_ffn2_norm, m_ffn2_w_in, m_ffn2_w_out, m_final_norm, v_ffn1_norm, v_ffn1_w_in, v_ffn1_w_out, v_mix_norm, v_w_in_mix, v_w_pool, v_pool_scale, v_w_alpha, v_b_alpha, v_gla_norm, v_w_out_mix, v_ffn2_norm, v_ffn2_w_in, v_ffn2_w_out, v_final_norm):
    names = ["ffn1_norm", "ffn1_w_in", "ffn1_w_out", "mix_norm", "w_in_mix", "w_pool", "pool_scale", "w_alpha",
             "b_alpha", "gla_norm", "w_out_mix", "ffn2_norm", "ffn2_w_in", "ffn2_w_out", "final_norm"]
    weights = dict(zip(names, [ffn1_norm, ffn1_w_in, ffn1_w_out, mix_norm, w_in_mix, w_pool, pool_scale, w_alpha,
                               b_alpha, gla_norm, w_out_mix, ffn2_norm, ffn2_w_in, ffn2_w_out, final_norm]))
    moms = dict(zip(names, [m_ffn1_norm, m_ffn1_w_in, m_ffn1_w_out, m_mix_norm, m_w_in_mix, m_w_pool, m_pool_scale,
                            m_w_alpha, m_b_alpha, m_gla_norm, m_w_out_mix, m_ffn2_norm, m_ffn2_w_in, m_ffn2_w_out,
                            m_final_norm]))
    vels = dict(zip(names, [v_ffn1_norm, v_ffn1_w_in, v_ffn1_w_out, v_mix_norm, v_w_in_mix, v_w_pool, v_pool_scale,
                            v_w_alpha, v_b_alpha, v_gla_norm, v_w_out_mix, v_ffn2_norm, v_ffn2_w_in, v_ffn2_w_out,
                            v_final_norm]))
    xi, yi, ci = lax.axis_index("x"), lax.axis_index("y"), lax.axis_index("c")
    chip = 2 * xi + yi
    c_idx = jnp.reshape(ci, (1,)).astype(jnp.int32)
    sc_idx = jnp.stack([chip, ci]).astype(jnp.int32)

    def flat2d(a):
        return a.reshape(-1, a.shape[-1])

    ex = _Exchanges(sc_idx, c_idx)

    def cast(n, after):
        return _cast_into_slot(flat2d(weights[n]), sc_idx, F32 if n == "w_alpha" else BF16, after, name=f"cast_{n}")

    groups = _Exchanges.GATHER_GROUPS
    w1_in = groups[0][0]
    tok = ex.start_gather({w1_in: cast(w1_in, sc_idx)}, groups[:1], ("first_hop",), name="gather_ici_start_first")
    bufs, last = {}, tok
    for g in groups[1:]:
        for n in g:
            bufs[n] = last = cast(n, last)
    bufs[w1_in] = ex.arrived(w1_in, after=last)[w1_in]
    tok2 = ex.start_gather(bufs, groups, ("second_hop",) + ("direct",) * (len(groups) - 1),
                           name="gather_ici_start_rest")
    early = [flat2d(moms["w_in_mix"]), flat2d(vels["w_in_mix"])]
    small_params = dict(g1=ffn1_norm, gm=mix_norm, g2=ffn2_norm, gf=final_norm.reshape(1, D_MODEL),
                        pool_scale=pool_scale, b_alpha=b_alpha, gla_norm=gla_norm, early=early, core=ci)
    loss_blk, dx, small = _forward_backward(x[0], loss_target[0], ex, tok[0, 0] + tok2[0, 0], small_params)

    outs = {}

    def update(n, g, after):
        w = weights[n]
        w2 = flat2d(w) if w.ndim > 1 else w.reshape(1, -1)
        go, d, nm, nv = _adamw(w2, g.reshape(w2.shape), moms[n].reshape(w2.shape), vels[n].reshape(w2.shape), after,
                               name=f"adamw_{n}")
        outs[n] = (go.reshape(w.shape), d.reshape(w.shape), nm.reshape(w.shape), nv.reshape(w.shape))
        return nv

    tags = _Exchanges.REDUCE_ORDER
    last = ex.finish_exchange(tags[0], after=dx)
    for prev, tag in zip(tags, tags[1:]):
        last = ex.finish_exchange(tag, after=last)
        for n, g in ex.reduced(prev, after=last).items():
            last = update(n, g, last)

    grads = {}
    small_names = ["ffn1_norm", "mix_norm", "ffn2_norm", "final_norm", "pool_scale", "b_alpha", "gla_norm", "w_alpha",
                   "loss"]
    small = small + [loss_blk[0:1]]
    rows = [a.size // LANES for a in small]
    slab = jnp.concatenate([a.reshape(-1, LANES) for a in small], axis=0)
    pad = -slab.shape[0] % 8
    slab = jnp.pad(slab, ((0, pad), (0, 0)))
    gathered = _all_gather_slab(slab).reshape(N_DEV, slab.shape[0], LANES)
    total = _slab_sum(gathered, name="sum_small_grads")
    off = 0
    for n, a, r in zip(small_names, small, rows):
        grads[n] = total[off:off + r].reshape(a.shape)
        off += r
    grads["w_alpha"] = lax.dynamic_slice_in_dim(grads["w_alpha"], chip * (GLA_DK_TOTAL // N_CHIPS),
                                                GLA_DK_TOTAL // N_CHIPS, axis=1)

    loss = grads.pop("loss")[0, 0]
    for n in small_names[:-1]:
        last = update(n, grads[n], last)
    for n, g in ex.reduced(tags[-1], after=last).items():
        last = update(n, g, last)
    return (loss, dx[None], *[outs[n][0] for n in names], *[outs[n][1] for n in names],
            *[outs[n][2] for n in names], *[outs[n][3] for n in names])


class _Exchanges:
    GATHER_GROUPS = (("ffn1_w_in",), ("ffn1_w_out",), ("w_in_mix", "w_pool", "w_alpha"), ("w_out_mix",),
                     ("ffn2_w_in",), ("ffn2_w_out",))
    REDUCE_ORDER = ("ffn2_out", "ffn2_in", "mix", "ffn1_out", "ffn1_in")

    def __init__(self, sc_idx, c_idx):
        self.sc_idx, self.c_idx = sc_idx, c_idx
        self._gathers, self._swaps, self._sends, self._reduces, self._joins = {}, {}, {}, {}, {}

    def start_gather(self, bufs, groups, kinds, *, name):
        handles, token = _gather_ici_start([[bufs[n] for n in g] for g in groups], kinds, name=name)
        for g, h in zip(groups, handles):
            self._gathers[g[0]] = (g, h)
        return token

    def arrived(self, first, after):
        names, handle = self._gathers.pop(first)
        return dict(zip(names, _gather_ici_wait(handle, after, name=f"gather_ici_wait_{first}_{handle[3]}")))

    def gathered(self, first, after):
        got = self.arrived(first, after)
        return dict(zip(got, _forward_halves(list(got.values()), name=f"gather_forward_{first}")))

    def begin_reduce(self, tag, full):
        g4 = [a.reshape(N_CHIPS, 2, a.shape[1] // 2, a.shape[2]) for a in full.values()]
        lands = [(a.shape[0],) + a.shape[2:] for a in g4]
        handle, token = _split_copy_start(g4, lands, len(g4), _sibling_swap_copies, name=f"swap_start_{tag}")
        self._swaps[tag] = (list(full), handle)
        return token

    def start_reduce(self, tag, after):
        names, handle = self._swaps.pop(tag)
        g4, from_sibling = _split_copy_wait(handle, _sibling_swap_copies, after, name=f"swap_wait_{tag}")
        pair = [_pair_sum(a, b, self.c_idx, name=f"pair_sum_{n}") for n, a, b in zip(names, g4, from_sibling)]
        return self.start_exchange(tag, names, pair)

    def start_exchange(self, tag, names, pair):
        lands = [(N_CHIPS - 1,) + a.shape[1:] for a in pair]
        handle, token = _split_copy_start(pair, lands, len(pair) * (N_CHIPS - 1), _chip_exchange_copies,
                                          name=f"exchange_start_{tag}")
        self._reduces[tag] = (names, handle)
        return token

    def send_to_sibling(self, tag, arrays):
        handle, token = _split_copy_start(arrays, [a.shape for a in arrays], len(arrays), _sibling_send_copies,
                                          name=f"send_start_{tag}")
        self._sends[tag] = handle
        return token

    def from_sibling(self, tag, after):
        return _split_copy_wait(self._sends.pop(tag), _sibling_send_copies, after, name=f"send_wait_{tag}")[1]

    def finish_exchange(self, tag, after):
        names, handle = self._reduces.pop(tag)
        pair, lands = _split_copy_wait(handle, _chip_exchange_copies, after, name=f"exchange_wait_{tag}")
        halves = [_chip_sum(a, b, self.sc_idx, name=f"chip_sum_{n}") for n, a, b in zip(names, pair, lands)]
        handle, token = _join_start(halves, name=f"join_start_{tag}")
        self._joins[tag] = (names, handle)
        return token

    def reduced(self, tag, after):
        names, handle = self._joins.pop(tag)
        return dict(zip(names, _join_wait(handle, after, name=f"join_wait_{tag}")))


def _forward_backward(h0, target, ex, started, sp):
    g1, gm, g2, gf = sp["g1"], sp["gm"], sp["g2"], sp["gf"]
    pool_scale, b_alpha, gla_norm = sp["pool_scale"], sp["b_alpha"], sp["gla_norm"]
    cs_mix = D_IN // N_CHIPS

    n1, n1t = _rms_fwd(h0, g1 + started, sp["early"], name="ffn1_norm")
    w1_in = ex.gathered("ffn1_w_in", after=n1)["ffn1_w_in"]
    fa1, a1 = _ffn_up(n1, w1_in, [], name="ffn1_up")
    w1_out = ex.gathered("ffn1_w_out", after=a1)["ffn1_w_out"].reshape(D_FF, D_MODEL)
    h1 = _mm_nn(a1, w1_out, h0, 0.5, tm=512, tn=D_MODEL, tk=SHARD_TILE, name="ffn1_down")
    n_mix, n_mixt = _rms_fwd(h1, gm, [], name="mix_norm")
    gw = ex.gathered("w_in_mix", after=n_mix)
    w_mix = _concat_shards(gw["w_in_mix"], D_IN_PAD, name="w_mix_concat")[None]
    wp = gw["w_pool"].reshape(N_CHIPS, 4, POOL_GROUP_DIM // N_CHIPS, POOL_GROUP_DIM)
    wp = wp.transpose(1, 0, 2, 3).reshape(4, POOL_GROUP_DIM, POOL_GROUP_DIM)
    wa = gw["w_alpha"].transpose(1, 0, 2).reshape(GLA_GATE_RANK, GLA_DK_TOTAL)
    wa = jnp.pad(wa, ((0, LANES - GLA_GATE_RANK), (0, 0))).astype(BF16)
    u = _mm_nn(n_mix, w_mix[0], None, 1.0, tm=512, tn=D_IN_PAD, tk=D_MODEL, name="mix_in")
    y_pool = _pool_fwd(u, wp, pool_scale, name="pool_fwd")
    cat, o_gla, states = _gla_fwd(u, y_pool, wa, b_alpha, gla_norm, name="gla_fwd")
    w_omix = ex.gathered("w_out_mix", after=cat)["w_out_mix"].reshape(D_MODEL, D_MODEL)
    h2 = _mm_nn(cat, w_omix, h1, 1.0, tm=512, tn=D_MODEL, tk=1024, name="mix_out")
    n3, n3t = _rms_fwd(h2, g2, [], name="ffn2_norm")
    w2_in = ex.gathered("ffn2_w_in", after=n3)["ffn2_w_in"]
    fa2, a2 = _ffn_up(n3, w2_in, [], name="ffn2_up")
    w2_out = ex.gathered("ffn2_w_out", after=a2)["ffn2_w_out"].reshape(D_FF, D_MODEL)
    h3 = _mm_nn(a2, w2_out, h2, 0.5, tm=512, tn=D_MODEL, tk=SHARD_TILE, name="ffn2_down")
    loss_blk, dh3, dh3b, d_gf = _final_loss(h3, gf, target, name="final_loss")

    core = sp["core"]
    tok = ex.begin_reduce("ffn2_out", {"ffn2_w_out": _ffn_dw_out(dh3b, a2, "ffn2")})
    dgu2 = _ffn_bwd_act(dh3b, w2_out, fa2, tok, name="ffn2_bwd_act")
    theirs = _ffn_dw_in_half(n3t, dgu2, 1 - core, None, w2_in.shape, name="ffn2_dw_in_other")
    tok = ex.send_to_sibling("ffn2_in", [theirs])
    dh2, dh2b, d_g2 = _mm_nt_rmsbwd(dgu2, w2_in, h2, dh3, g2 + tok[0, 0], tk=SHARD_TILE, name="ffn2_dx")
    tok = ex.start_reduce("ffn2_out", after=dh2b)
    pair = _ffn_dw_in_half(n3t, dgu2, core, ex.from_sibling("ffn2_in", after=tok)[0], w2_in.shape,
                           name="ffn2_dw_in_own")
    tok = ex.start_exchange("ffn2_in", ["ffn2_w_in"], [pair])
    S = h0.shape[0]
    dcat = _mm_nt(dh2b, w_omix, tok, tm=512, tn=1024, name="mix_out_dx")
    dw_omix = _mm_tn(
        cat, dh2b, grid=(4,),
        a_spec=pl.BlockSpec((S, 512), lambda j: (0, j)), b_spec=pl.BlockSpec((S, D_MODEL), lambda j: (0, 0)),
        out_spec=pl.BlockSpec((512, D_MODEL), lambda j: (j, 0)),
        out_shape=jax.ShapeDtypeStruct((D_MODEL, D_MODEL), BF16), scale=1.0, name="mix_out_dw")
    dp, dw_pool, d_pscale = _pool_bwd(u, dcat, wp, pool_scale, name="pool_bwd")
    du, d_wa, d_ba, d_gn = _gla_bwd(u, o_gla, states, dcat, dp, wa, b_alpha, gla_norm, name="gla_bwd")
    du = du[None]
    tn_mix = COL_TILE
    dw_mix = _mm_tn(
        n_mixt, du, grid=(2, D_IN_PAD // tn_mix), a_is_transposed=True,
        a_spec=pl.BlockSpec((D_MODEL // 2, S), lambda i, j: (i, 0)),
        b_spec=pl.BlockSpec((None, S, tn_mix), lambda i, j: (0, 0, j)),
        out_spec=pl.BlockSpec((D_MODEL // 2, tn_mix), lambda i, j: (i, j)),
        out_shape=jax.ShapeDtypeStruct((D_MODEL, D_IN_PAD), BF16), scale=1.0, name="mix_in_dw")
    dw_mix_s = _split_shards(dw_mix, cs_mix, name="dw_mix_split")
    dw_pool_s = dw_pool.reshape(4, N_CHIPS, POOL_GROUP_DIM // N_CHIPS, POOL_GROUP_DIM).transpose(1, 0, 2, 3)
    dw_pool_s = dw_pool_s.reshape(N_CHIPS, POOL_GROUP_DIM, POOL_GROUP_DIM).astype(BF16)
    tok = ex.begin_reduce("mix", {"w_in_mix": dw_mix_s,
                                  "w_out_mix": dw_omix.reshape(N_CHIPS, D_MODEL // N_CHIPS, D_MODEL),
                                  "w_pool": dw_pool_s})
    dh1, dh1b, d_gm = _mm_nt_rmsbwd(du, w_mix, h1, dh2, gm + tok[0, 0], tk=tn_mix, name="mix_in_dx")
    tok = ex.start_reduce("mix", after=dh1b)
    tok = tok + ex.begin_reduce("ffn1_out", {"ffn1_w_out": _ffn_dw_out(dh1b, a1, "ffn1")})
    dgu1 = _ffn_bwd_act(dh1b, w1_out, fa1, tok, name="ffn1_bwd_act")
    theirs = _ffn_dw_in_half(n1t, dgu1, 1 - core, None, w1_in.shape, name="ffn1_dw_in_other")
    tok = ex.send_to_sibling("ffn1_in", [theirs])
    tok = ex.start_reduce("ffn1_out", after=tok)
    pair = _ffn_dw_in_half(n1t, dgu1, core, ex.from_sibling("ffn1_in", after=tok)[0], w1_in.shape,
                           name="ffn1_dw_in_own")
    tok = ex.start_exchange("ffn1_in", ["ffn1_w_in"], [pair])
    dx, _, d_g1 = _mm_nt_rmsbwd(dgu1, w1_in, h0, dh1, g1 + tok[0, 0], tk=SHARD_TILE, name="ffn1_dx")
    small = [d_g1, d_gm, d_g2, d_gf, d_pscale, d_ba, d_gn, d_wa[:GLA_GATE_RANK]]
    return loss_blk, dx, small
```

```python
import functools

import jax
import jax.numpy as jnp
from jax import lax
from jax.experimental import pallas as pl
from jax.experimental.pallas import tpu as pltpu

F32 = jnp.float32
BF16 = jnp.bfloat16
MESH = pl.DeviceIdType.MESH

D_MODEL = 2048
D_FF = 5632
D_POOL = 1024
POOL_WINDOWS = (2, 4, 8, 16)
POOL_GROUP_DIM = 256
D_GLA = 1024
GLA_HEADS = 4
GLA_DV = 256
GLA_DK = 128
GLA_DK_TOTAL = 512
GLA_GATE_RANK = 16
GATE_LOGIT_NORMALIZER = 16.0
CHUNK = 64
D_IN = 4112
D_IN_PAD = 4224
EPS = 1e-6

ADAM_LR = 0.001
ADAM_B1 = 0.9
ADAM_B2 = 0.999
ADAM_EPS = 1e-08
ADAM_WD = 0.01
ADAM_STEP = 10

N_CHIPS = 4
N_DEV = 8
V7X_VMEM_BYTES = 64 * 1024 * 1024
LANES = 128
MXU_TILE = 256
COL_TILE = 1408
SHARD_TILE = 2816


def _cparams(semantics, vmem_mb):
    assert vmem_mb * 1024 * 1024 < V7X_VMEM_BYTES
    return pltpu.CompilerParams(dimension_semantics=semantics, vmem_limit_bytes=vmem_mb * 1024 * 1024)


def _dot_nn(a, b):
    return jnp.dot(a, b, preferred_element_type=F32)


def _dot_nt(a, b):
    return lax.dot_general(a, b, (((1,), (1,)), ((), ())), preferred_element_type=F32)


def _dot_tn(a, b):
    return lax.dot_general(a, b, (((0,), (0,)), ((), ())), preferred_element_type=F32)


def _sigmoid(x):
    return 1.0 / (1.0 + jnp.exp(-x))


def _rms_fwd(x, g, early, *, name):
    S, D = x.shape
    tm = 256

    def body(x_ref, g_ref, *rest):
        o_ref, ot_ref = rest[len(early):]
        xv = x_ref[...]
        r = lax.rsqrt(jnp.mean(xv * xv, axis=-1, keepdims=True) + EPS)
        n = xv * r * g_ref[...]
        o_ref[...] = n.astype(BF16)
        ot_ref[...] = n.T.astype(BF16)

    return pl.pallas_call(
        body, name=name, grid=(S // tm,),
        in_specs=[pl.BlockSpec((tm, D), lambda i: (i, 0)), pl.BlockSpec((1, D), lambda i: (0, 0))]
        + [pl.BlockSpec(memory_space=pl.ANY)] * len(early),
        out_specs=[pl.BlockSpec((tm, D), lambda i: (i, 0)), pl.BlockSpec((D, tm), lambda i: (0, i))],
        out_shape=[jax.ShapeDtypeStruct((S, D), BF16), jax.ShapeDtypeStruct((D, S), BF16)],
        compiler_params=_cparams(("parallel",), 32),
    )(x, g, *early)


def _ffn_up(n, w_in, early, *, name):
    S, D = n.shape
    ns, _, cs = w_in.shape
    half = ns // 2
    F = cs * half
    tm, tn = 128, SHARD_TILE
    nb = cs // tn

    def body(n_ref, wg_ref, wu_ref, *rest):
        fa_ref, a_ref = rest[len(early):]
        nv = n_ref[...]
        g = _dot_nn(nv, wg_ref[...])
        u = _dot_nn(nv, wu_ref[...])
        s = _sigmoid(g)
        silu = g * s
        fa_ref[0] = (u * (s * (1.0 + g * (1.0 - s)))).astype(BF16)
        fa_ref[1] = silu.astype(BF16)
        a_ref[...] = (silu * u).astype(BF16)

    return pl.pallas_call(
        body, name=name, grid=(F // tn, S // tm),
        in_specs=[
            pl.BlockSpec((tm, D), lambda j, i: (i, 0)),
            pl.BlockSpec((None, D, tn), lambda j, i: (lax.div(j, nb), 0, lax.rem(j, nb))),
            pl.BlockSpec((None, D, tn), lambda j, i: (half + lax.div(j, nb), 0, lax.rem(j, nb))),
        ] + [pl.BlockSpec(memory_space=pl.ANY)] * len(early),
        out_specs=[
            pl.BlockSpec((2, tm, tn), lambda j, i: (0, i, j)),
            pl.BlockSpec((tm, tn), lambda j, i: (i, j)),
        ],
        out_shape=[jax.ShapeDtypeStruct((2, S, F), BF16), jax.ShapeDtypeStruct((S, F), BF16)],
        compiler_params=_cparams(("parallel", "parallel"), 56),
    )(n, w_in, w_in, *early)


def _mm_nn(a, b, resid, scale, *, tm, tn, tk, name):
    S, K = a.shape
    N = b.shape[1]
    nk = K // tk

    def body(*refs):
        if resid is None:
            a_ref, b_ref, o_ref, acc_ref = refs
            r_ref = None
        else:
            a_ref, b_ref, r_ref, o_ref, acc_ref = refs
        k = pl.program_id(2)

        @pl.when(k == 0)
        def _():
            acc_ref[...] = jnp.zeros_like(acc_ref)

        acc_ref[...] += _dot_nn(a_ref[...], b_ref[...])

        @pl.when(k == nk - 1)
        def _():
            out = acc_ref[...] * scale
            if r_ref is not None:
                out = r_ref[...] + out
            o_ref[...] = out

    in_specs = [pl.BlockSpec((tm, tk), lambda i, j, k: (i, k)), pl.BlockSpec((tk, tn), lambda i, j, k: (k, j))]
    args = [a, b]
    if resid is not None:
        in_specs.append(pl.BlockSpec((tm, tn), lambda i, j, k: (i, j)))
        args.append(resid)
    return pl.pallas_call(
        body, name=name, grid=(S // tm, N // tn, nk),
        in_specs=in_specs,
        out_specs=pl.BlockSpec((tm, tn), lambda i, j, k: (i, j)),
        out_shape=jax.ShapeDtypeStruct((S, N), F32),
        scratch_shapes=[pltpu.VMEM((tm, tn), F32)],
        compiler_params=_cparams(("parallel", "parallel", "arbitrary"), 48),
    )(*args)


def _mm_nt(a, b, after, *, tm, tn, name):
    S, K = a.shape
    N = b.shape[0]

    def body(a_ref, b_ref, after_ref, o_ref):
        o_ref[...] = _dot_nt(a_ref[...], b_ref[...])

    return pl.pallas_call(
        body, name=name, grid=(N // tn, S // tm),
        in_specs=[pl.BlockSpec((tm, K), lambda j, i: (i, 0)), pl.BlockSpec((tn, K), lambda j, i: (j, 0)),
                  pl.BlockSpec(memory_space=pl.ANY)],
        out_specs=pl.BlockSpec((tm, tn), lambda j, i: (i, j)),
        out_shape=jax.ShapeDtypeStruct((S, N), F32),
        compiler_params=_cparams(("parallel", "parallel"), 48),
    )(a, b, after)


def _mm_tn(a, b, *, grid, a_spec, b_spec, out_spec, out_shape, scale, name, a_is_transposed=False):
    dot = _dot_nn if a_is_transposed else _dot_tn

    def body(a_ref, b_ref, o_ref):
        o_ref[...] = (scale * dot(a_ref[...], b_ref[...])).astype(o_ref.dtype)

    return pl.pallas_call(
        body, name=name, grid=grid, in_specs=[a_spec, b_spec], out_specs=out_spec, out_shape=out_shape,
        compiler_params=_cparams(("parallel",) * len(grid), 56),
    )(a, b)


def _ffn_bwd_act(dhb, w_out, fa, after, *, name):
    S, D = dhb.shape
    F = w_out.shape[0]
    tm, tn = 256, SHARD_TILE

    def body(dh_ref, w_ref, fa_ref, after_ref, dgu_ref):
        da = 0.5 * _dot_nt(dh_ref[...], w_ref[...])
        dgu_ref[0] = (da * fa_ref[0].astype(F32)).astype(BF16)
        dgu_ref[1] = (da * fa_ref[1].astype(F32)).astype(BF16)

    return pl.pallas_call(
        body, name=name, grid=(F // tn, S // tm),
        in_specs=[
            pl.BlockSpec((tm, D), lambda j, i: (i, 0)),
            pl.BlockSpec((tn, D), lambda j, i: (j, 0)),
            pl.BlockSpec((2, tm, tn), lambda j, i: (0, i, j)),
            pl.BlockSpec(memory_space=pl.ANY),
        ],
        out_specs=pl.BlockSpec((2, tm, tn), lambda j, i: (0, i, j)),
        out_shape=jax.ShapeDtypeStruct((2, S, F), BF16),
        compiler_params=_cparams(("parallel", "parallel"), 56),
    )(dhb, w_out, fa, after)


def _mm_nt_rmsbwd(dact, w, h_in, dh_out, g, *, tk, name):
    ng, S, fg = dact.shape
    ns, D, cs = w.shape
    assert ng * fg == ns * cs
    tm, rc = 512, 64
    kpg, kps = fg // tk, cs // tk
    nk = ng * kpg

    def body(a_ref, w_ref, h_ref, dho_ref, g_ref, dh_ref, dhb_ref, dg_ref, acc_ref):
        i = pl.program_id(0)
        k = pl.program_id(1)

        @pl.when(k == 0)
        def _():
            acc_ref[...] = jnp.zeros_like(acc_ref)

        acc_ref[...] += _dot_nt(a_ref[...], w_ref[...])

        @pl.when(jnp.logical_and(i == 0, k == 0))
        def _():
            dg_ref[...] = jnp.zeros_like(dg_ref)

        @pl.when(k == nk - 1)
        def _():
            gv = g_ref[...]

            def rows_step(c, dg):
                rows = pl.ds(pl.multiple_of(c * rc, rc), rc)
                dn = acc_ref[rows, :]
                xv = h_ref[rows, :]
                r = lax.rsqrt(jnp.mean(xv * xv, axis=-1, keepdims=True) + EPS)
                xh = xv * r
                dng = dn * gv
                dx = r * (dng - xh * jnp.mean(dng * xh, axis=-1, keepdims=True))
                out = dho_ref[rows, :] + dx
                dh_ref[rows, :] = out
                dhb_ref[rows, :] = out.astype(BF16)
                return dg + jnp.sum(dn * xh, axis=0, keepdims=True)

            dg_ref[...] += lax.fori_loop(0, tm // rc, rows_step, jnp.zeros((1, D), F32))

    return pl.pallas_call(
        body, name=name, grid=(S // tm, nk),
        in_specs=[
            pl.BlockSpec((None, tm, tk), lambda i, k: (lax.div(k, kpg), i, lax.rem(k, kpg))),
            pl.BlockSpec((None, D, tk), lambda i, k: (lax.div(k, kps), 0, lax.rem(k, kps))),
            pl.BlockSpec((tm, D), lambda i, k: (i, 0)),
            pl.BlockSpec((tm, D), lambda i, k: (i, 0)),
            pl.BlockSpec((1, D), lambda i, k: (0, 0)),
        ],
        out_specs=[
            pl.BlockSpec((tm, D), lambda i, k: (i, 0), pipeline_mode=pl.Buffered(1)),
            pl.BlockSpec((tm, D), lambda i, k: (i, 0), pipeline_mode=pl.Buffered(1)),
            pl.BlockSpec((1, D), lambda i, k: (0, 0)),
        ],
        out_shape=[jax.ShapeDtypeStruct((S, D), F32), jax.ShapeDtypeStruct((S, D), BF16),
                   jax.ShapeDtypeStruct((1, D), F32)],
        scratch_shapes=[pltpu.VMEM((tm, D), F32)],
        compiler_params=_cparams(("arbitrary", "arbitrary"), 56),
    )(dact, w, h_in, dh_out, g)


def _final_loss(h, g, target, *, name):
    S, D = h.shape
    tm = 256

    def body(h_ref, g_ref, t_ref, loss_ref, dh_ref, dhb_ref, dg_ref):
        i = pl.program_id(0)

        @pl.when(i == 0)
        def _():
            loss_ref[...] = jnp.zeros_like(loss_ref)
            dg_ref[...] = jnp.zeros_like(dg_ref)

        xv = h_ref[...]
        gv = g_ref[...]
        r = lax.rsqrt(jnp.mean(xv * xv, axis=-1, keepdims=True) + EPS)
        xh = xv * r
        e = xh * gv - t_ref[...]
        loss_ref[...] += 0.5 * jnp.sum(jnp.mean(e * e, axis=-1, keepdims=True))
        dy = e * (1.0 / D)
        dg_ref[...] += jnp.sum(dy * xh, axis=0, keepdims=True)
        dyg = dy * gv
        dx = r * (dyg - xh * jnp.mean(dyg * xh, axis=-1, keepdims=True))
        dh_ref[...] = dx
        dhb_ref[...] = dx.astype(BF16)

    return pl.pallas_call(
        body, name=name, grid=(S // tm,),
        in_specs=[pl.BlockSpec((tm, D), lambda i: (i, 0)), pl.BlockSpec((1, D), lambda i: (0, 0)),
                  pl.BlockSpec((tm, D), lambda i: (i, 0))],
        out_specs=[pl.BlockSpec((8, LANES), lambda i: (0, 0)), pl.BlockSpec((tm, D), lambda i: (i, 0)),
                   pl.BlockSpec((tm, D), lambda i: (i, 0)), pl.BlockSpec((1, D), lambda i: (0, 0))],
        out_shape=[jax.ShapeDtypeStruct((8, LANES), F32), jax.ShapeDtypeStruct((S, D), F32),
                   jax.ShapeDtypeStruct((S, D), BF16), jax.ShapeDtypeStruct((1, D), F32)],
        compiler_params=_cparams(("arbitrary",), 40),
    )(h, g, target)


POOL_HALO = 16
POOL_ROWS = 256


def _pool_window_mean_minus_token(ext, tok0, w):
    s = ext
    k = 1
    while k < w:
        s = s + pltpu.roll(s, k, 0)
        k *= 2
    win = s[POOL_HALO:, :]
    tok = tok0 + lax.broadcasted_iota(jnp.int32, (POOL_ROWS, 1), 0)
    cnt = jnp.minimum(tok + 1, w).astype(F32)
    return win / cnt - ext[POOL_HALO:, :], cnt


def _pool_fwd(u, w_pool, scale, *, name):
    S = u.shape[0]
    C = POOL_GROUP_DIM
    nsteps = S // POOL_ROWS

    def body(p_ref, w_ref, sc_ref, y_ref, xp_ref):
        xp_ref[0:POOL_HALO, :] = jnp.zeros((POOL_HALO, D_POOL), F32)
        xp_ref[POOL_HALO:, :] = p_ref[...]
        for gi, win in enumerate(POOL_WINDOWS):
            cols = slice(gi * C, (gi + 1) * C)

            def step(c, carry, cols=cols, win=win, gi=gi):
                r0 = pl.multiple_of(c * POOL_ROWS, POOL_ROWS)
                ext = xp_ref[pl.ds(r0, POOL_ROWS + POOL_HALO), cols]
                pooled, _ = _pool_window_mean_minus_token(ext, r0, win)
                y = _dot_nn(pooled.astype(BF16), w_ref[gi]) * sc_ref[:, cols]
                y_ref[pl.ds(r0, POOL_ROWS), cols] = y.astype(BF16)
                return carry

            lax.fori_loop(0, nsteps, step, 0)

    return pl.pallas_call(
        body, name=name, grid=(1,),
        in_specs=[pl.BlockSpec((S, D_POOL), lambda i: (0, 0)),
                  pl.BlockSpec((4, C, C), lambda i: (0, 0, 0)),
                  pl.BlockSpec((1, D_POOL), lambda i: (0, 0))],
        out_specs=pl.BlockSpec((S, D_POOL), lambda i: (0, 0)),
        out_shape=jax.ShapeDtypeStruct((S, D_POOL), BF16),
        scratch_shapes=[pltpu.VMEM((S + POOL_HALO, D_POOL), F32)],
        compiler_params=_cparams(("arbitrary",), 48),
    )(u, w_pool, scale)


def _pool_bwd(u, dcat, w_pool, scale, *, name):
    S = u.shape[0]
    C = POOL_GROUP_DIM
    nsteps = S // POOL_ROWS

    def body(p_ref, dy_ref, w_ref, sc_ref, dp_ref, dw_ref, dsc_ref, xp_ref, e_ref, neg_ref):
        xp_ref[0:POOL_HALO, :] = jnp.zeros((POOL_HALO, D_POOL), F32)
        xp_ref[POOL_HALO:, :] = p_ref[...]
        e_ref[S:, :] = jnp.zeros((POOL_HALO, C), F32)
        for gi, win in enumerate(POOL_WINDOWS):
            cols = slice(gi * C, (gi + 1) * C)

            def step_a(c, carry, cols=cols, win=win, gi=gi):
                dw, dsc = carry
                r0 = pl.multiple_of(c * POOL_ROWS, POOL_ROWS)
                ext = xp_ref[pl.ds(r0, POOL_ROWS + POOL_HALO), cols]
                pooled, cnt = _pool_window_mean_minus_token(ext, r0, win)
                pb = pooled.astype(BF16)
                wv = w_ref[gi]
                dy = dy_ref[pl.ds(r0, POOL_ROWS), cols]
                dsc = dsc + jnp.sum(dy * _dot_nn(pb, wv), axis=0, keepdims=True)
                dyp = (dy * sc_ref[:, cols]).astype(BF16)
                dw = dw + _dot_tn(pb, dyp)
                dpooled = _dot_nt(dyp, wv)
                e_ref[pl.ds(r0, POOL_ROWS), :] = dpooled / cnt
                neg_ref[pl.ds(r0, POOL_ROWS), :] = -dpooled
                return dw, dsc

            dw, dsc = lax.fori_loop(0, nsteps, step_a, (jnp.zeros((C, C), F32), jnp.zeros((1, C), F32)))
            dw_ref[gi] = dw
            dsc_ref[:, cols] = dsc

            def step_b(c, carry, cols=cols, win=win):
                r0 = pl.multiple_of(c * POOL_ROWS, POOL_ROWS)
                s = e_ref[pl.ds(r0, POOL_ROWS + POOL_HALO), :]
                n = POOL_ROWS + POOL_HALO
                k = 1
                while k < win:
                    s = s + pltpu.roll(s, n - k, 0)
                    k *= 2
                du = s[:POOL_ROWS, :] + neg_ref[pl.ds(r0, POOL_ROWS), :]
                dp_ref[pl.ds(r0, POOL_ROWS), cols] = du.astype(BF16)
                return carry

            lax.fori_loop(0, nsteps, step_b, 0)

    return pl.pallas_call(
        body, name=name, grid=(1,),
        in_specs=[pl.BlockSpec((S, D_POOL), lambda i: (0, 0)),
                  pl.BlockSpec((S, D_POOL), lambda i: (0, 0)),
                  pl.BlockSpec((4, C, C), lambda i: (0, 0, 0)),
                  pl.BlockSpec((1, D_POOL), lambda i: (0, 0))],
        out_specs=[pl.BlockSpec((S, D_POOL), lambda i: (0, 0)),
                   pl.BlockSpec((4, C, C), lambda i: (0, 0, 0)),
                   pl.BlockSpec((1, D_POOL), lambda i: (0, 0))],
        out_shape=[jax.ShapeDtypeStruct((S, D_POOL), BF16), jax.ShapeDtypeStruct((4, C, C), F32),
                   jax.ShapeDtypeStruct((1, D_POOL), F32)],
        scratch_shapes=[pltpu.VMEM((S + POOL_HALO, D_POOL), F32), pltpu.VMEM((S + POOL_HALO, C), F32),
                        pltpu.VMEM((S, C), F32)],
        compiler_params=_cparams(("arbitrary",), 56),
    )(u, dcat, w_pool, scale)


GLA_ROWS = 128
U_Q_BLK, U_K_BLK = 2, 3
U_V_BLK, U_G_BLK = 2, 3
U_R_BLK = 32


def _prefix_sum_rows(x):
    n = x.shape[0]
    row = lax.broadcasted_iota(jnp.int32, x.shape, 0)
    k = 1
    while k < n:
        x = x + jnp.where(row >= k, pltpu.roll(x, k, 0), 0.0)
        k *= 2
    return x


def _suffix_sum_rows(x):
    n = x.shape[0]
    row = lax.broadcasted_iota(jnp.int32, x.shape, 0)
    k = 1
    while k < n:
        x = x + jnp.where(row < n - k, pltpu.roll(x, n - k, 0), 0.0)
        k *= 2
    return x


def _log_sigmoid(z):
    return jnp.minimum(z, 0.0) - jnp.log(1.0 + jnp.exp(-jnp.abs(z)))


def _gla_chunk_terms(la_c, q_c, k_c):
    bc = _prefix_sum_rows(la_c)
    bl = jnp.sum(la_c, axis=0, keepdims=True)
    eb = jnp.exp(bc)
    enb = jnp.exp(-bc)
    etail = jnp.exp(bl - bc)
    qd = q_c * (GLA_DK ** -0.5) * eb
    ki = k_c * enb
    kt = k_c * etail
    d = jnp.exp(bl)
    return eb, enb, etail, qd, ki, kt, d


def _gla_fwd(u, y_pool, w_alpha, b_alpha, gnorm, *, name):
    S = u.shape[0]
    RB = GLA_ROWS
    ncc = RB // CHUNK
    H, DK, DV = GLA_HEADS, GLA_DK, GLA_DV

    def body(q_ref, k_ref, v_ref, go_ref, r_ref, yp_ref, wa_ref, ba_ref, gn_ref, cat_ref, o_ref, st_ref, state):
        i = pl.program_id(0)

        @pl.when(i == 0)
        def _():
            state[...] = jnp.zeros_like(state)

        cat_ref[:, :D_POOL] = yp_ref[...]
        y_ref = cat_ref.at[:, D_POOL:]

        z = _dot_nn(r_ref[...].astype(BF16), wa_ref[...]) + ba_ref[...]
        la = _log_sigmoid(z) / GATE_LOGIT_NORMALIZER
        ri = lax.broadcasted_iota(jnp.int32, (CHUNK, CHUNK), 0)
        ci = lax.broadcasted_iota(jnp.int32, (CHUNK, CHUNK), 1)
        tri = ri >= ci
        gn = gn_ref[...]
        for cc in range(ncc):
            rs = slice(cc * CHUNK, (cc + 1) * CHUNK)
            for h in range(H):
                ks = slice(h * DK, (h + 1) * DK)
                vs = slice(h * DV, (h + 1) * DV)
                _, _, _, qd, ki, kt, d = _gla_chunk_terms(la[rs, ks], q_ref[rs, ks], k_ref[rs, ks])
                qdb = qd.astype(BF16)
                vb = v_ref[rs, vs].astype(BF16)
                p = jnp.where(tri, _dot_nt(qdb, ki.astype(BF16)), 0.0)
                st = state[h]
                st_ref[cc, h] = st
                o = _dot_nn(p.astype(BF16), vb) + _dot_nt(qdb, st.astype(BF16))
                state[h] = st * d + _dot_tn(vb, kt.astype(BF16))
                o_ref[rs, vs] = o
                rinv = lax.rsqrt(jnp.mean(o * o, axis=-1, keepdims=True) + EPS)
                go = go_ref[rs, vs]
                y_ref[rs, vs] = (o * rinv * gn * (go * _sigmoid(go))).astype(BF16)

    nblk = S // RB
    return pl.pallas_call(
        body, name=name, grid=(nblk,),
        in_specs=[
            pl.BlockSpec((RB, GLA_DK_TOTAL), lambda i: (i, U_Q_BLK)),
            pl.BlockSpec((RB, GLA_DK_TOTAL), lambda i: (i, U_K_BLK)),
            pl.BlockSpec((RB, D_GLA), lambda i: (i, U_V_BLK)),
            pl.BlockSpec((RB, D_GLA), lambda i: (i, U_G_BLK)),
            pl.BlockSpec((RB, LANES), lambda i: (i, U_R_BLK)),
            pl.BlockSpec((RB, D_POOL), lambda i: (i, 0)),
            pl.BlockSpec((LANES, GLA_DK_TOTAL), lambda i: (0, 0)),
            pl.BlockSpec((1, GLA_DK_TOTAL), lambda i: (0, 0)),
            pl.BlockSpec((1, DV), lambda i: (0, 0)),
        ],
        out_specs=[
            pl.BlockSpec((RB, D_POOL + D_GLA), lambda i: (i, 0)),
            pl.BlockSpec((RB, D_GLA), lambda i: (i, 0)),
            pl.BlockSpec((ncc, H, DV, DK), lambda i: (i, 0, 0, 0)),
        ],
        out_shape=[jax.ShapeDtypeStruct((S, D_POOL + D_GLA), BF16), jax.ShapeDtypeStruct((S, D_GLA), F32),
                   jax.ShapeDtypeStruct((S // CHUNK, H, DV, DK), F32)],
        scratch_shapes=[pltpu.VMEM((H, DV, DK), F32)],
        compiler_params=_cparams(("arbitrary",), 32),
    )(u, u, u, u, u, y_pool, w_alpha, b_alpha, gnorm)


def _gla_bwd(u, o, states, dcat, dpool, w_alpha, b_alpha, gnorm, *, name):
    S = u.shape[0]
    RB = GLA_ROWS
    ncc = RB // CHUNK
    H, DK, DV = GLA_HEADS, GLA_DK, GLA_DV
    nblk = S // RB
    o_q, o_k = D_POOL, D_POOL + GLA_DK_TOTAL
    o_v, o_g, o_r = o_k + GLA_DK_TOTAL, o_k + GLA_DK_TOTAL + D_GLA, o_k + GLA_DK_TOTAL + 2 * D_GLA

    def body(q_ref, k_ref, v_ref, go_ref, r_ref, o_ref, st_ref, dy_ref, dpool_ref, wa_ref, ba_ref, gn_ref,
             du_ref, dwa_ref, dba_ref, dgn_ref, dstate, dz_ref):
        i = pl.program_id(0)

        @pl.when(i == 0)
        def _():
            dstate[...] = jnp.zeros_like(dstate)
            dwa_ref[...] = jnp.zeros_like(dwa_ref)
            dba_ref[...] = jnp.zeros_like(dba_ref)
            dgn_ref[...] = jnp.zeros_like(dgn_ref)

        du_ref[:, :o_q] = dpool_ref[...]
        dq_ref, dk_ref = du_ref.at[:, o_q:o_k], du_ref.at[:, o_k:o_v]
        dv_ref, dgo_ref, dr_ref = du_ref.at[:, o_v:o_g], du_ref.at[:, o_g:o_r], du_ref.at[:, o_r:]

        rb = r_ref[...].astype(BF16)
        wa = wa_ref[...]
        z = _dot_nn(rb, wa) + ba_ref[...]
        la = _log_sigmoid(z) / GATE_LOGIT_NORMALIZER
        ri = lax.broadcasted_iota(jnp.int32, (CHUNK, CHUNK), 0)
        ci = lax.broadcasted_iota(jnp.int32, (CHUNK, CHUNK), 1)
        tri = ri >= ci
        last_row = lax.broadcasted_iota(jnp.int32, (CHUNK, DK), 0) == CHUNK - 1
        gn = gn_ref[...]
        dgn = jnp.zeros((1, DV), F32)
        for cc in reversed(range(ncc)):
            rs = slice(cc * CHUNK, (cc + 1) * CHUNK)
            for h in range(H):
                ks = slice(h * DK, (h + 1) * DK)
                vs = slice(h * DV, (h + 1) * DV)
                eb, enb, etail, qd, ki, kt, d = _gla_chunk_terms(la[rs, ks], q_ref[rs, ks], k_ref[rs, ks])
                qdb, kib, ktb = qd.astype(BF16), ki.astype(BF16), kt.astype(BF16)
                vb = v_ref[rs, vs].astype(BF16)
                p = jnp.where(tri, _dot_nt(qdb, kib), 0.0)
                ov = o_ref[rs, vs]
                go = go_ref[rs, vs]
                dy = dy_ref[rs, vs]
                rinv = lax.rsqrt(jnp.mean(ov * ov, axis=-1, keepdims=True) + EPS)
                oh = ov * rinv
                sg = _sigmoid(go)
                dgo_ref[rs, vs] = (dy * (oh * gn) * (sg * (1.0 + go * (1.0 - sg)))).astype(BF16)
                don = dy * (go * sg)
                dgn = dgn + jnp.sum(don * oh, axis=0, keepdims=True)
                doh = don * gn
                do = rinv * (doh - oh * jnp.mean(doh * oh, axis=-1, keepdims=True))
                dob = do.astype(BF16)
                st = st_ref[cc, h]
                dst = dstate[h]
                stb, dstb = st.astype(BF16), dst.astype(BF16)
                dp = jnp.where(tri, _dot_nt(dob, vb), 0.0).astype(BF16)
                dv_ref[rs, vs] = (_dot_tn(p.astype(BF16), dob) + _dot_nt(ktb, dstb)).astype(BF16)
                dqd = _dot_nn(dp, kib) + _dot_nn(dob, stb)
                dki = _dot_tn(dp, qdb)
                dkt = _dot_nn(vb, dstb)
                dd = jnp.sum(dst * st, axis=0, keepdims=True)
                dstate[h] = dst * d + _dot_tn(dob, qdb)
                dq_ref[rs, ks] = (dqd * eb * (DK ** -0.5)).astype(BF16)
                dk_ref[rs, ks] = (dki * enb + dkt * etail).astype(BF16)
                dbl = jnp.sum(dkt * kt, axis=0, keepdims=True) + dd * d
                dbc = dqd * qd - dki * ki - dkt * kt
                dbc = dbc + jnp.where(last_row, dbl, 0.0)
                dla = _suffix_sum_rows(dbc)
                dz_ref[rs, ks] = dla * (1.0 / GATE_LOGIT_NORMALIZER) * (1.0 - _sigmoid(z[rs, ks]))
        dz = dz_ref[...]
        dzb = dz.astype(BF16)
        dr_ref[...] = _dot_nt(dzb, wa).astype(BF16)
        dwa_ref[...] += _dot_tn(rb, dzb)
        dba_ref[...] += jnp.sum(dz, axis=0, keepdims=True)
        dgn_ref[...] += dgn

    def rev(blk):
        return lambda i: (nblk - 1 - i, blk)

    return pl.pallas_call(
        body, name=name, grid=(nblk,),
        in_specs=[
            pl.BlockSpec((RB, GLA_DK_TOTAL), rev(U_Q_BLK)),
            pl.BlockSpec((RB, GLA_DK_TOTAL), rev(U_K_BLK)),
            pl.BlockSpec((RB, D_GLA), rev(U_V_BLK)),
            pl.BlockSpec((RB, D_GLA), rev(U_G_BLK)),
            pl.BlockSpec((RB, LANES), rev(U_R_BLK)),
            pl.BlockSpec((RB, D_GLA), rev(0)),
            pl.BlockSpec((ncc, H, DV, DK), lambda i: (nblk - 1 - i, 0, 0, 0)),
            pl.BlockSpec((RB, D_GLA), rev(1)),
            pl.BlockSpec((RB, D_POOL), rev(0)),
            pl.BlockSpec((LANES, GLA_DK_TOTAL), lambda i: (0, 0)),
            pl.BlockSpec((1, GLA_DK_TOTAL), lambda i: (0, 0)),
            pl.BlockSpec((1, DV), lambda i: (0, 0)),
        ],
        out_specs=[
            pl.BlockSpec((RB, D_IN_PAD), rev(0)),
            pl.BlockSpec((LANES, GLA_DK_TOTAL), lambda i: (0, 0)),
            pl.BlockSpec((1, GLA_DK_TOTAL), lambda i: (0, 0)),
            pl.BlockSpec((1, DV), lambda i: (0, 0)),
        ],
        out_shape=[
            jax.ShapeDtypeStruct((S, D_IN_PAD), BF16),
            jax.ShapeDtypeStruct((LANES, GLA_DK_TOTAL), F32), jax.ShapeDtypeStruct((1, GLA_DK_TOTAL), F32),
            jax.ShapeDtypeStruct((1, DV), F32),
        ],
        scratch_shapes=[pltpu.VMEM((H, DV, DK), F32), pltpu.VMEM((RB, GLA_DK_TOTAL), F32)],
        compiler_params=_cparams(("arbitrary",), 32),
    )(u, u, u, u, u, o, states, dcat, dpool, w_alpha, b_alpha, gnorm)


def _concat_shards(w4, width, *, name):
    ns, R, cs = w4.shape
    tr = 256

    def body(w_ref, o_ref):
        for t in range(ns):
            o_ref[:, t * cs:(t + 1) * cs] = w_ref[t]
        o_ref[:, ns * cs:] = jnp.zeros((tr, width - ns * cs), o_ref.dtype)

    return pl.pallas_call(
        body, name=name, grid=(R // tr,),
        in_specs=[pl.BlockSpec((ns, tr, cs), lambda i: (0, i, 0))],
        out_specs=pl.BlockSpec((tr, width), lambda i: (i, 0)),
        out_shape=jax.ShapeDtypeStruct((R, width), w4.dtype),
        compiler_params=_cparams(("parallel",), 32),
    )(w4)


def _split_shards(a, cs, *, name):
    R, width = a.shape
    tr = 256

    def body(a_ref, o_ref):
        for t in range(N_CHIPS):
            o_ref[t] = a_ref[:, t * cs:(t + 1) * cs]

    return pl.pallas_call(
        body, name=name, grid=(R // tr,),
        in_specs=[pl.BlockSpec((tr, width), lambda i: (i, 0))],
        out_specs=pl.BlockSpec((N_CHIPS, tr, cs), lambda i: (0, i, 0)),
        out_shape=jax.ShapeDtypeStruct((N_CHIPS, R, cs), a.dtype),
        compiler_params=_cparams(("parallel",), 32),
    )(a)


def _row_tile(rows, cols, itemsize, budget=2 * 1024 * 1024):
    if rows * cols * itemsize <= budget or rows % 16:
        return rows
    best = 16
    for t in range(16, rows + 1, 16):
        if rows % t == 0 and t * cols * itemsize <= budget:
            best = t
    return best


def _adamw(w, g, m, v, after, *, name):
    R, C = w.shape
    tr = _row_tile(R, C, 4)

    def body(w_ref, g_ref, m_ref, v_ref, after_ref, go_ref, d_ref, nm_ref, nv_ref):
        gv = g_ref[...]
        go_ref[...] = gv
        mn = ADAM_B1 * m_ref[...] + (1.0 - ADAM_B1) * gv
        vn = ADAM_B2 * v_ref[...] + (1.0 - ADAM_B2) * jnp.square(gv)
        m_hat = mn / (1.0 - ADAM_B1 ** ADAM_STEP)
        v_hat = vn / (1.0 - ADAM_B2 ** ADAM_STEP)
        d_ref[...] = -ADAM_LR * (m_hat / (jnp.sqrt(v_hat) + ADAM_EPS) + ADAM_WD * w_ref[...])
        nm_ref[...] = mn
        nv_ref[...] = vn

    spec = pl.BlockSpec((tr, C), lambda i: (i, 0))
    shp = jax.ShapeDtypeStruct((R, C), F32)
    return pl.pallas_call(
        body, name=name, grid=(R // tr,), in_specs=[spec] * 4 + [pl.BlockSpec(memory_space=pl.ANY)],
        out_specs=[spec] * 4, out_shape=[shp] * 4,
        compiler_params=_cparams(("parallel",), 48),
    )(w, g, m, v, after)


def _pair_sum(g4, recv, c_idx, *, name):
    ns, _, R2, C = g4.shape
    tr = _row_tile(R2, C, 2)

    def body(c_ref, g_ref, r_ref, o_ref):
        o_ref[...] = (g_ref[...].astype(F32) + r_ref[...].astype(F32)).astype(BF16)

    return pl.pallas_call(
        body, name=name,
        grid_spec=pltpu.PrefetchScalarGridSpec(
            num_scalar_prefetch=1, grid=(ns, R2 // tr),
            in_specs=[pl.BlockSpec((None, None, tr, C), lambda s, i, c: (s, c[0], i, 0)),
                      pl.BlockSpec((None, tr, C), lambda s, i, c: (s, i, 0))],
            out_specs=pl.BlockSpec((None, tr, C), lambda s, i, c: (s, i, 0)),
        ),
        out_shape=jax.ShapeDtypeStruct((ns, R2, C), BF16),
        compiler_params=_cparams(("parallel", "parallel"), 32),
    )(c_idx, g4, recv)


def _chip_sum(part, recv, sc_idx, *, name):
    _, R2, C = part.shape
    tr = _row_tile(R2, C, 4)
    nblk = R2 // tr

    def body(s_ref, p_ref, r_ref, o_ref):
        acc = p_ref[...].astype(F32)
        for j in range(N_CHIPS - 1):
            acc = acc + r_ref[j].astype(F32)
        o_ref[...] = acc

    return pl.pallas_call(
        body, name=name,
        grid_spec=pltpu.PrefetchScalarGridSpec(
            num_scalar_prefetch=1, grid=(nblk,),
            in_specs=[pl.BlockSpec((None, tr, C), lambda i, s: (s[0], i, 0)),
                      pl.BlockSpec((N_CHIPS - 1, tr, C), lambda i, s: (0, i, 0))],
            out_specs=pl.BlockSpec((tr, C), lambda i, s: (s[1] * nblk + i, 0)),
        ),
        out_shape=jax.ShapeDtypeStruct((2 * R2, C), F32),
        compiler_params=_cparams(("parallel",), 32),
    )(sc_idx, part, recv)


def _cast_into_slot(w, sc_idx, dtype, after, *, name):
    R, C = w.shape
    tr = _row_tile(R, C, 4)

    def body(s_ref, w_ref, after_ref, o_ref):
        o_ref[...] = w_ref[...].astype(dtype)

    return pl.pallas_call(
        body, name=name,
        grid_spec=pltpu.PrefetchScalarGridSpec(
            num_scalar_prefetch=1, grid=(R // tr,),
            in_specs=[pl.BlockSpec((tr, C), lambda i, s: (i, 0)), pl.BlockSpec(memory_space=pl.ANY)],
            out_specs=pl.BlockSpec((None, tr, C), lambda i, s: (s[0], i, 0)),
        ),
        out_shape=jax.ShapeDtypeStruct((N_CHIPS, R, C), dtype),
        compiler_params=_cparams(("parallel",), 32),
    )(sc_idx, w, after)


def _slab_sum(slabs, *, name):
    n, M, C = slabs.shape

    def body(x_ref, o_ref):
        acc = x_ref[0]
        for d in range(1, n):
            acc = acc + x_ref[d]
        o_ref[...] = acc

    return pl.pallas_call(
        body, name=name, out_shape=jax.ShapeDtypeStruct((M, C), F32),
    )(slabs)


def _mesh_position():
    x, y, c = lax.axis_index("x"), lax.axis_index("y"), lax.axis_index("c")
    other_chips = [(1 - x, y), (x, 1 - y), (1 - x, 1 - y)]
    return x, y, c, other_chips


ANY = pl.BlockSpec(memory_space=pl.ANY)


HBM = pl.BlockSpec(memory_space=pltpu.HBM)
SEM = pl.BlockSpec(memory_space=pltpu.SEMAPHORE)
SPLIT_COPY = pltpu.CompilerParams(has_side_effects=pltpu.SideEffectType.DATAFLOW_SIDE_EFFECTING)
TOKEN = jax.ShapeDtypeStruct((8, LANES), F32)


def _in_hbm(a):
    return pltpu.with_memory_space_constraint(a, pltpu.HBM)


def _half_rows(ref, slot, half):
    hr = ref.shape[1] // 2
    return ref.at[slot, pl.ds(half * hr, hr), :]


GATHER_COPIES = {"direct": 3, "first_hop": 2, "second_hop": 1}


def _gather_routes(kind):
    x, y, c, chips = _mesh_position()
    me = 2 * x + y
    first = (x + (1 - c) * (1 - 2 * x), y + c * (1 - 2 * y))
    second = (x + c * (1 - 2 * x), y + (1 - c) * (1 - 2 * y))
    if kind == "direct":
        return [(me, (*p, c)) for p in chips], [2 * p[0] + p[1] for p in chips]
    if kind == "first_hop":
        return [(me, (*first, c)), (me, (*second, c))], [2 * first[0] + first[1], 2 * second[0] + second[1]]
    assert kind == "second_hop"
    return [(2 * first[0] + first[1], (*second, c))], [2 * (1 - x) + (1 - y)]


def _gather_ici_start(groups, kinds, *, name):
    flat = [b for g in groups for b in g]
    K, G = len(flat), len(groups)

    def body(*refs):
        ins, sems, token = refs[:K], refs[K:K + 2 * G], refs[-1]
        c = lax.axis_index("c")
        k = 0
        for gi, (g, kind) in enumerate(zip(groups, kinds)):
            sends, _ = _gather_routes(kind)
            for n in range(len(g)):
                for j, (slot, to) in enumerate(sends):
                    part = _half_rows(ins[k], slot, c)
                    pltpu.make_async_remote_copy(
                        src_ref=part, dst_ref=part, send_sem=sems[2 * gi].at[n * len(sends) + j],
                        recv_sem=sems[2 * gi + 1].at[n * len(sends) + j],
                        device_id=to, device_id_type=MESH).start()
                k += 1
        token[...] = jnp.zeros_like(token)

    sem_shapes = []
    for g, kind in zip(groups, kinds):
        sem_shapes += [pltpu.SemaphoreType.DMA((len(g) * GATHER_COPIES[kind],))] * 2
    out = pl.pallas_call(
        body, name=name,
        in_specs=[HBM] * K,
        out_specs=[SEM] * (2 * G) + [HBM] * K + [pl.BlockSpec(memory_space=pltpu.VMEM)],
        out_shape=sem_shapes + [pltpu.HBM(b.shape, b.dtype) for b in flat] + [TOKEN],
        input_output_aliases={k: 2 * G + k for k in range(K)},
        compiler_params=SPLIT_COPY,
    )(*[_in_hbm(b) for b in flat])
    handles, k = [], 2 * G
    for gi, (g, kind) in enumerate(zip(groups, kinds)):
        handles.append((out[2 * gi], out[2 * gi + 1], list(out[k:k + len(g)]), kind))
        k += len(g)
    return handles, out[-1]


def _gather_ici_wait(handle, after, *, name):
    send, recv, bufs, kind = handle
    n = len(bufs)

    def body(*refs):
        ins, send_ref, recv_ref = refs[:n], refs[n], refs[n + 1]
        c = lax.axis_index("c")
        sends, arrivals = _gather_routes(kind)
        for k in range(n):
            for j, ((slot, to), landed) in enumerate(zip(sends, arrivals)):
                cp = pltpu.make_async_remote_copy(
                    src_ref=_half_rows(ins[k], slot, c), dst_ref=_half_rows(ins[k], landed, c),
                    send_sem=send_ref.at[k * len(sends) + j], recv_sem=recv_ref.at[k * len(sends) + j],
                    device_id=to, device_id_type=MESH)
                cp.wait_send()
                cp.wait_recv()

    return pl.pallas_call(
        body, name=name,
        in_specs=[HBM] * n + [SEM, SEM, ANY], out_specs=[HBM] * n,
        out_shape=[pltpu.HBM(b.shape, b.dtype) for b in bufs],
        input_output_aliases={k: k for k in range(n)},
        compiler_params=SPLIT_COPY,
    )(*bufs, send, recv, after)


def _forward_halves(bufs, *, name):
    K = len(bufs)
    per = N_CHIPS - 1

    def body(*refs):
        outs = refs[K:2 * K]
        send_sems, recv_sems = refs[2 * K:]
        x, y, c, chips = _mesh_position()
        copies = []
        for k in range(K):
            for j, chip in enumerate(chips):
                got = _half_rows(outs[k], 2 * chip[0] + chip[1], c)
                cp = pltpu.make_async_remote_copy(
                    src_ref=got, dst_ref=got, send_sem=send_sems.at[k * per + j], recv_sem=recv_sems.at[k * per + j],
                    device_id=(x, y, 1 - c), device_id_type=MESH)
                cp.start()
                copies.append(cp)
        for cp in copies:
            cp.wait()

    return pl.pallas_call(
        body, name=name,
        in_specs=[ANY] * K, out_specs=[ANY] * K,
        out_shape=[jax.ShapeDtypeStruct(a.shape, a.dtype) for a in bufs],
        input_output_aliases={k: k for k in range(K)},
        scratch_shapes=[pltpu.SemaphoreType.DMA((K * per,)), pltpu.SemaphoreType.DMA((K * per,))],
    )(*bufs)


def _chip_exchange_copies(srcs, lands, send_sems, recv_sems):
    x, y, c, chips = _mesh_position()
    per = N_CHIPS - 1
    return [pltpu.make_async_remote_copy(
        src_ref=srcs[k].at[2 * chip[0] + chip[1]], dst_ref=lands[k].at[j],
        send_sem=send_sems.at[k * per + j], recv_sem=recv_sems.at[k * per + j],
        device_id=(*chip, c), device_id_type=MESH) for k in range(len(srcs)) for j, chip in enumerate(chips)]


def _sibling_swap_copies(srcs, lands, send_sems, recv_sems):
    x, y, c, _ = _mesh_position()
    return [pltpu.make_async_remote_copy(
        src_ref=srcs[k].at[pl.ds(0, srcs[k].shape[0]), 1 - c], dst_ref=lands[k],
        send_sem=send_sems.at[k], recv_sem=recv_sems.at[k],
        device_id=(x, y, 1 - c), device_id_type=MESH) for k in range(len(srcs))]


def _sibling_send_copies(srcs, lands, send_sems, recv_sems):
    x, y, c, _ = _mesh_position()
    return [pltpu.make_async_remote_copy(
        src_ref=srcs[k], dst_ref=lands[k], send_sem=send_sems.at[k], recv_sem=recv_sems.at[k],
        device_id=(x, y, 1 - c), device_id_type=MESH) for k in range(len(srcs))]


def _split_copy_start(srcs, land_shapes, n_sems, copies, *, name):
    K = len(srcs)

    def body(*refs):
        for cp in copies(refs[:K], refs[K:2 * K], refs[2 * K], refs[2 * K + 1]):
            cp.start()
        refs[-1][...] = jnp.zeros_like(refs[-1])

    out = pl.pallas_call(
        body, name=name,
        in_specs=[HBM] * (2 * K),
        out_specs=[SEM, SEM] + [HBM] * (2 * K) + [pl.BlockSpec(memory_space=pltpu.VMEM)],
        out_shape=[pltpu.SemaphoreType.DMA((n_sems,))] * 2
        + [pltpu.HBM(a.shape, a.dtype) for a in srcs]
        + [pltpu.HBM(s, a.dtype) for s, a in zip(land_shapes, srcs)] + [TOKEN],
        input_output_aliases={k: 2 + k for k in range(2 * K)},
        compiler_params=SPLIT_COPY,
    )(*[_in_hbm(a) for a in srcs], *[_in_hbm(lax.empty(s, a.dtype)) for s, a in zip(land_shapes, srcs)])
    return (out[0], out[1], list(out[2:2 + K]), list(out[2 + K:2 + 2 * K])), out[-1]


def _split_copy_wait(handle, copies, after, *, name):
    send, recv, srcs, lands = handle
    K = len(srcs)

    def body(*refs):
        for cp in copies(refs[:K], refs[K:2 * K], refs[2 * K], refs[2 * K + 1]):
            cp.wait_send()
            cp.wait_recv()

    out = pl.pallas_call(
        body, name=name,
        in_specs=[HBM] * (2 * K) + [SEM, SEM, ANY], out_specs=[HBM] * (2 * K),
        out_shape=[pltpu.HBM(a.shape, a.dtype) for a in srcs] + [pltpu.HBM(a.shape, a.dtype) for a in lands],
        input_output_aliases={k: k for k in range(2 * K)},
        compiler_params=SPLIT_COPY,
    )(*srcs, *lands, send, recv, after)
    return list(out[:K]), list(out[K:])


def _join_copies(bufs, send_sems, recv_sems):
    x, y, c, _ = _mesh_position()
    copies = []
    for k, buf in enumerate(bufs):
        r2 = buf.shape[0] // 2
        mine = buf.at[pl.ds(c * r2, r2), :]
        copies.append(pltpu.make_async_remote_copy(
            src_ref=mine, dst_ref=mine, send_sem=send_sems.at[k], recv_sem=recv_sems.at[k],
            device_id=(x, y, 1 - c), device_id_type=MESH))
    return copies


def _join_start(bufs, *, name):
    K = len(bufs)

    def body(*refs):
        for cp in _join_copies(refs[:K], refs[K], refs[K + 1]):
            cp.start()
        refs[-1][...] = jnp.zeros_like(refs[-1])

    out = pl.pallas_call(
        body, name=name,
        in_specs=[HBM] * K,
        out_specs=[SEM, SEM] + [HBM] * K + [pl.BlockSpec(memory_space=pltpu.VMEM)],
        out_shape=[pltpu.SemaphoreType.DMA((K,))] * 2 + [pltpu.HBM(a.shape, a.dtype) for a in bufs] + [TOKEN],
        input_output_aliases={k: 2 + k for k in range(K)},
        compiler_params=SPLIT_COPY,
    )(*[_in_hbm(a) for a in bufs])
    return (out[0], out[1], list(out[2:2 + K])), out[-1]


def _join_wait(handle, after, *, name):
    send, recv, bufs = handle
    K = len(bufs)

    def body(*refs):
        for cp in _join_copies(refs[:K], refs[K], refs[K + 1]):
            cp.wait_send()
            cp.wait_recv()

    return pl.pallas_call(
        body, name=name,
        in_specs=[HBM] * K + [SEM, SEM, ANY], out_specs=[HBM] * K,
        out_shape=[pltpu.HBM(a.shape, a.dtype) for a in bufs],
        input_output_aliases={k: k for k in range(K)},
        compiler_params=SPLIT_COPY,
    )(*bufs, send, recv, after)


def _all_gather_slab(slab):
    m_per, n = slab.shape

    def body(x_ref, out_ref, send_sems, recv_sems, local_sem):
        x, y, c, chips = _mesh_position()
        me, sibling = (x, y, c), (x, y, 1 - c)

        def rows(px, py, pc):
            return out_ref.at[pl.ds((4 * px + 2 * py + pc) * m_per, m_per), :]

        def copy(k, block, to, src=None):
            return pltpu.make_async_remote_copy(
                src_ref=rows(*block) if src is None else src, dst_ref=rows(*block),
                send_sem=send_sems.at[k], recv_sem=recv_sems.at[k], device_id=to, device_id_type=MESH)

        mine = pltpu.make_async_copy(x_ref, rows(*me), local_sem)
        mine.start()
        first = [copy(0, me, sibling, src=x_ref)]
        first += [copy(1 + j, me, (*chip, c), src=x_ref) for j, chip in enumerate(chips)]
        for cp in first:
            cp.start()
        passed = [copy(4 + j, (*chip, c), sibling) for j, chip in enumerate(chips)]
        for j, chip in enumerate(chips):
            copy(1 + j, (*chip, c), me).wait_recv()
            passed[j].start()
        copy(0, sibling, me).wait_recv()
        for j, chip in enumerate(chips):
            copy(4 + j, (*chip, 1 - c), me).wait_recv()
        for cp in first + passed:
            cp.wait_send()
        mine.wait()

    return pl.pallas_call(
        body, name="gather_small_grads",
        out_shape=jax.ShapeDtypeStruct((N_DEV * m_per, n), slab.dtype),
        in_specs=[pl.BlockSpec(memory_space=pltpu.VMEM)],
        out_specs=pl.BlockSpec(memory_space=pltpu.VMEM),
        scratch_shapes=[pltpu.SemaphoreType.DMA((7,)), pltpu.SemaphoreType.DMA((7,)), pltpu.SemaphoreType.DMA],
    )(slab)


def _ffn_dw_out(dhb, a, tag):
    S, F = a.shape
    D = dhb.shape[1]
    to = 512
    return _mm_tn(
        a, dhb, grid=(F // to,),
        a_spec=pl.BlockSpec((S, to), lambda j: (0, j)), b_spec=pl.BlockSpec((S, D), lambda j: (0, 0)),
        out_spec=pl.BlockSpec((to, D), lambda j: (j, 0)), out_shape=jax.ShapeDtypeStruct((F, D), BF16),
        scale=0.5, name=f"{tag}_dw_out").reshape(N_CHIPS, F // N_CHIPS, D)


def _ffn_dw_in_half(nt, dgu, half, add, w_in_shape, *, name):
    ns, D, cs = w_in_shape
    S = nt.shape[1]
    F = dgu.shape[2]
    ti, tr = SHARD_TILE, D // 2
    per_g, per_s = F // ti, cs // ti

    def body(h_ref, a_ref, b_ref, *rest):
        acc = _dot_nn(a_ref[...], b_ref[...])
        if add is not None:
            acc = acc + rest[0][...].astype(F32)
        rest[-1][...] = acc.astype(BF16)

    out_spec = pl.BlockSpec((None, tr, ti), lambda j, h: (lax.div(j, per_s), 0, lax.rem(j, per_s)))
    extra = [] if add is None else [add]
    return pl.pallas_call(
        body, name=name,
        grid_spec=pltpu.PrefetchScalarGridSpec(
            num_scalar_prefetch=1, grid=(2 * F // ti,),
            in_specs=[pl.BlockSpec((tr, S), lambda j, h: (h[0], 0)),
                      pl.BlockSpec((None, S, ti), lambda j, h: (lax.div(j, per_g), 0, lax.rem(j, per_g)))]
            + [out_spec] * len(extra),
            out_specs=out_spec,
        ),
        out_shape=jax.ShapeDtypeStruct((ns, tr, cs), BF16),
        compiler_params=_cparams(("parallel",), 56),
    )(half, nt, dgu, *extra)


def kernel(x, ffn1_norm, ffn1_w_in, ffn1_w_out, mix_norm, w_in_mix, w_pool, pool_scale, w_alpha, b_alpha, gla_norm, w_out_mix, ffn2_norm, ffn2_w_in, ffn2_w_out, final_norm, loss_target, m_ffn1_norm, m_ffn1_w_in, m_ffn1_w_out, m_mix_norm, m_w_in_mix, m_w_pool, m_pool_scale, m_w_alpha, m_b_alpha, m_gla_norm, m_w_out_mix, m_ffn2_norm, m_ffn2_w_in, m_ffn2_w_out, m_final_norm, v_ffn1_norm, v_ffn1_w_in, v_ffn1_w_out, v_mix_norm, v_w_in_mix, v_w_pool, v_pool_scale, v_w_alpha, v_b_alpha, v_gla_norm, v_w_out_mix, v_ffn2_norm, v_ffn2_w_in, v_ffn2_w_out, v_final_norm):
    names = ["ffn1_norm", "ffn1_w_in", "ffn1_w_out", "mix_norm", "w_in_mix", "w_pool", "pool_scale", "w_alpha",
             "b_alpha", "gla_norm", "w_out_mix", "ffn2_norm", "ffn2_w_in", "ffn2_w_out", "final_norm"]
    weights = dict(zip(names, [ffn1_norm, ffn1_w_in, ffn1_w_out, mix_norm, w_in_mix, w_pool, pool_scale, w_alpha,
                               b_alpha, gla_norm, w_out_mix, ffn2_norm, ffn2_w_in, ffn2_w_out, final_norm]))
    moms = dict(zip(names, [m_ffn1_norm, m_ffn1_w_in, m_ffn1_w_out, m_mix_norm, m_w_in_mix, m_w_pool, m_pool_scale,
                            m_w_alpha, m_b_alpha, m_gla_norm, m_w_out_mix, m_ffn2_norm, m_ffn2_w_in, m_ffn2_w_out,
                            m_final_norm]))
    vels = dict(zip(names, [v_ffn1_norm, v_ffn1_w_in, v_ffn1_w_out, v_mix_norm, v_w_in_mix, v_w_pool, v_pool_scale,
                            v_w_alpha, v_b_alpha, v_gla_norm, v_w_out_mix, v_ffn2_norm, v_ffn2_w_in, v_ffn2_w_out,
                            v_final_norm]))
    xi, yi, ci = lax.axis_index("x"), lax.axis_index("y"), lax.axis_index("c")
    chip = 2 * xi + yi
    c_idx = jnp.reshape(ci, (1,)).astype(jnp.int32)
    sc_idx = jnp.stack([chip, ci]).astype(jnp.int32)

    def flat2d(a):
        return a.reshape(-1, a.shape[-1])

    ex = _Exchanges(sc_idx, c_idx)

    def cast(n, after):
        return _cast_into_slot(flat2d(weights[n]), sc_idx, F32 if n == "w_alpha" else BF16, after, name=f"cast_{n}")

    groups = _Exchanges.GATHER_GROUPS
    w1_in = groups[0][0]
    tok = ex.start_gather({w1_in: cast(w1_in, sc_idx)}, groups[:1], ("first_hop",), name="gather_ici_start_first")
    bufs, last = {}, tok
    for g in groups[1:]:
        for n in g:
            bufs[n] = last = cast(n, last)
    bufs[w1_in] = ex.arrived(w1_in, after=last)[w1_in]
    tok2 = ex.start_gather(bufs, groups, ("second_hop",) + ("direct",) * (len(groups) - 1),
                           name="gather_ici_start_rest")
    early = [flat2d(moms["w_in_mix"]), flat2d(vels["w_in_mix"])]
    small_params = dict(g1=ffn1_norm, gm=mix_norm, g2=ffn2_norm, gf=final_norm.reshape(1, D_MODEL),
                        pool_scale=pool_scale, b_alpha=b_alpha, gla_norm=gla_norm, early=early, core=c_idx)
    loss_blk, dx, small = _forward_backward(x[0], loss_target[0], ex, tok[0, 0] + tok2[0, 0], small_params)

    outs = {}

    def update(n, g, after):
        w = weights[n]
        w2 = flat2d(w) if w.ndim > 1 else w.reshape(1, -1)
        go, d, nm, nv = _adamw(w2, g.reshape(w2.shape), moms[n].reshape(w2.shape), vels[n].reshape(w2.shape), after,
                               name=f"adamw_{n}")
        outs[n] = (go.reshape(w.shape), d.reshape(w.shape), nm.reshape(w.shape), nv.reshape(w.shape))
        return nv

    tags = _Exchanges.REDUCE_ORDER
    last = ex.finish_exchange(tags[0], after=dx)
    for prev, tag in zip(tags, tags[1:]):
        last = ex.finish_exchange(tag, after=last)
        for n, g in ex.reduced(prev, after=last).items():
            last = update(n, g, last)

    grads = {}
    small_names = ["ffn1_norm", "mix_norm", "ffn2_norm", "final_norm", "pool_scale", "b_alpha", "gla_norm", "w_alpha",
                   "loss"]
    small = small + [loss_blk[0:1]]
    rows = [a.size // LANES for a in small]
    slab = jnp.concatenate([a.reshape(-1, LANES) for a in small], axis=0)
    pad = -slab.shape[0] % 8
    slab = jnp.pad(slab, ((0, pad), (0, 0)))
    gathered = _all_gather_slab(slab).reshape(N_DEV, slab.shape[0], LANES)
    total = _slab_sum(gathered, name="sum_small_grads")
    off = 0
    for n, a, r in zip(small_names, small, rows):
        grads[n] = total[off:off + r].reshape(a.shape)
        off += r
    grads["w_alpha"] = lax.dynamic_slice_in_dim(grads["w_alpha"], chip * (GLA_DK_TOTAL // N_CHIPS),
                                                GLA_DK_TOTAL // N_CHIPS, axis=1)

    loss = grads.pop("loss")[0, 0]
    for n in small_names[:-1]:
        last = update(n, grads[n], last)
    for n, g in ex.reduced(tags[-1], after=last).items():
        last = update(n, g, last)
    return (loss, dx[None], *[outs[n][0] for n in names], *[outs[n][1] for n in names],
            *[outs[n][2] for n in names], *[outs[n][3] for n in names])


class _Exchanges:
    GATHER_GROUPS = (("ffn1_w_in",), ("ffn1_w_out",), ("w_in_mix", "w_pool", "w_alpha"), ("w_out_mix",),
                     ("ffn2_w_in",), ("ffn2_w_out",))
    REDUCE_ORDER = ("ffn2_out", "ffn2_in", "mix", "ffn1_out", "ffn1_in")

    def __init__(self, sc_idx, c_idx):
        self.sc_idx, self.c_idx = sc_idx, c_idx
        self._gathers, self._swaps, self._sends, self._reduces, self._joins = {}, {}, {}, {}, {}

    def start_gather(self, bufs, groups, kinds, *, name):
        handles, token = _gather_ici_start([[bufs[n] for n in g] for g in groups], kinds, name=name)
        for g, h in zip(groups, handles):
            self._gathers[g[0]] = (g, h)
        return token

    def arrived(self, first, after):
        names, handle = self._gathers.pop(first)
        return dict(zip(names, _gather_ici_wait(handle, after, name=f"gather_ici_wait_{first}_{handle[3]}")))

    def gathered(self, first, after):
        got = self.arrived(first, after)
        return dict(zip(got, _forward_halves(list(got.values()), name=f"gather_forward_{first}")))

    def begin_reduce(self, tag, full):
        g4 = [a.reshape(N_CHIPS, 2, a.shape[1] // 2, a.shape[2]) for a in full.values()]
        lands = [(a.shape[0],) + a.shape[2:] for a in g4]
        handle, token = _split_copy_start(g4, lands, len(g4), _sibling_swap_copies, name=f"swap_start_{tag}")
        self._swaps[tag] = (list(full), handle)
        return token

    def start_reduce(self, tag, after):
        names, handle = self._swaps.pop(tag)
        g4, from_sibling = _split_copy_wait(handle, _sibling_swap_copies, after, name=f"swap_wait_{tag}")
        pair = [_pair_sum(a, b, self.c_idx, name=f"pair_sum_{n}") for n, a, b in zip(names, g4, from_sibling)]
        return self.start_exchange(tag, names, pair)

    def start_exchange(self, tag, names, pair):
        lands = [(N_CHIPS - 1,) + a.shape[1:] for a in pair]
        handle, token = _split_copy_start(pair, lands, len(pair) * (N_CHIPS - 1), _chip_exchange_copies,
                                          name=f"exchange_start_{tag}")
        self._reduces[tag] = (names, handle)
        return token

    def send_to_sibling(self, tag, arrays):
        handle, token = _split_copy_start(arrays, [a.shape for a in arrays], len(arrays), _sibling_send_copies,
                                          name=f"send_start_{tag}")
        self._sends[tag] = handle
        return token

    def from_sibling(self, tag, after):
        return _split_copy_wait(self._sends.pop(tag), _sibling_send_copies, after, name=f"send_wait_{tag}")[1]

    def finish_exchange(self, tag, after):
        names, handle = self._reduces.pop(tag)
        pair, lands = _split_copy_wait(handle, _chip_exchange_copies, after, name=f"exchange_wait_{tag}")
        halves = [_chip_sum(a, b, self.sc_idx, name=f"chip_sum_{n}") for n, a, b in zip(names, pair, lands)]
        handle, token = _join_start(halves, name=f"join_start_{tag}")
        self._joins[tag] = (names, handle)
        return token

    def reduced(self, tag, after):
        names, handle = self._joins.pop(tag)
        return dict(zip(names, _join_wait(handle, after, name=f"join_wait_{tag}")))


def _forward_backward(h0, target, ex, started, sp):
    g1, gm, g2, gf = sp["g1"], sp["gm"], sp["g2"], sp["gf"]
    pool_scale, b_alpha, gla_norm = sp["pool_scale"], sp["b_alpha"], sp["gla_norm"]
    cs_mix = D_IN // N_CHIPS

    n1, n1t = _rms_fwd(h0, g1 + started, sp["early"], name="ffn1_norm")
    w1_in = ex.gathered("ffn1_w_in", after=n1)["ffn1_w_in"]
    fa1, a1 = _ffn_up(n1, w1_in, [], name="ffn1_up")
    w1_out = ex.gathered("ffn1_w_out", after=a1)["ffn1_w_out"].reshape(D_FF, D_MODEL)
    h1 = _mm_nn(a1, w1_out, h0, 0.5, tm=512, tn=D_MODEL, tk=SHARD_TILE, name="ffn1_down")
    n_mix, n_mixt = _rms_fwd(h1, gm, [], name="mix_norm")
    gw = ex.gathered("w_in_mix", after=n_mix)
    w_mix = _concat_shards(gw["w_in_mix"], D_IN_PAD, name="w_mix_concat")[None]
    wp = gw["w_pool"].reshape(N_CHIPS, 4, POOL_GROUP_DIM // N_CHIPS, POOL_GROUP_DIM)
    wp = wp.transpose(1, 0, 2, 3).reshape(4, POOL_GROUP_DIM, POOL_GROUP_DIM)
    wa = gw["w_alpha"].transpose(1, 0, 2).reshape(GLA_GATE_RANK, GLA_DK_TOTAL)
    wa = jnp.pad(wa, ((0, LANES - GLA_GATE_RANK), (0, 0))).astype(BF16)
    u = _mm_nn(n_mix, w_mix[0], None, 1.0, tm=512, tn=D_IN_PAD, tk=D_MODEL, name="mix_in")
    y_pool = _pool_fwd(u, wp, pool_scale, name="pool_fwd")
    cat, o_gla, states = _gla_fwd(u, y_pool, wa, b_alpha, gla_norm, name="gla_fwd")
    w_omix = ex.gathered("w_out_mix", after=cat)["w_out_mix"].reshape(D_MODEL, D_MODEL)
    h2 = _mm_nn(cat, w_omix, h1, 1.0, tm=512, tn=D_MODEL, tk=1024, name="mix_out")
    n3, n3t = _rms_fwd(h2, g2, [], name="ffn2_norm")
    w2_in = ex.gathered("ffn2_w_in", after=n3)["ffn2_w_in"]
    fa2, a2 = _ffn_up(n3, w2_in, [], name="ffn2_up")
    w2_out = ex.gathered("ffn2_w_out", after=a2)["ffn2_w_out"].reshape(D_FF, D_MODEL)
    h3 = _mm_nn(a2, w2_out, h2, 0.5, tm=512, tn=D_MODEL, tk=SHARD_TILE, name="ffn2_down")
    loss_blk, dh3, dh3b, d_gf = _final_loss(h3, gf, target, name="final_loss")

    core = sp["core"]
    tok = ex.begin_reduce("ffn2_out", {"ffn2_w_out": _ffn_dw_out(dh3b, a2, "ffn2")})
    dgu2 = _ffn_bwd_act(dh3b, w2_out, fa2, tok, name="ffn2_bwd_act")
    theirs = _ffn_dw_in_half(n3t, dgu2, 1 - core, None, w2_in.shape, name="ffn2_dw_in_other")
    tok = ex.send_to_sibling("ffn2_in", [theirs])
    dh2, dh2b, d_g2 = _mm_nt_rmsbwd(dgu2, w2_in, h2, dh3, g2 + tok[0, 0], tk=SHARD_TILE, name="ffn2_dx")
    tok = ex.start_reduce("ffn2_out", after=dh2b)
    pair = _ffn_dw_in_half(n3t, dgu2, core, ex.from_sibling("ffn2_in", after=tok)[0], w2_in.shape,
                           name="ffn2_dw_in_own")
    tok = ex.start_exchange("ffn2_in", ["ffn2_w_in"], [pair])
    S = h0.shape[0]
    dcat = _mm_nt(dh2b, w_omix, tok, tm=512, tn=1024, name="mix_out_dx")
    dw_omix = _mm_tn(
        cat, dh2b, grid=(4,),
        a_spec=pl.BlockSpec((S, 512), lambda j: (0, j)), b_spec=pl.BlockSpec((S, D_MODEL), lambda j: (0, 0)),
        out_spec=pl.BlockSpec((512, D_MODEL), lambda j: (j, 0)),
        out_shape=jax.ShapeDtypeStruct((D_MODEL, D_MODEL), BF16), scale=1.0, name="mix_out_dw")
    dp, dw_pool, d_pscale = _pool_bwd(u, dcat, wp, pool_scale, name="pool_bwd")
    du, d_wa, d_ba, d_gn = _gla_bwd(u, o_gla, states, dcat, dp, wa, b_alpha, gla_norm, name="gla_bwd")
    du = du[None]
    tn_mix = COL_TILE
    dw_mix = _mm_tn(
        n_mixt, du, grid=(2, D_IN_PAD // tn_mix), a_is_transposed=True,
        a_spec=pl.BlockSpec((D_MODEL // 2, S), lambda i, j: (i, 0)),
        b_spec=pl.BlockSpec((None, S, tn_mix), lambda i, j: (0, 0, j)),
        out_spec=pl.BlockSpec((D_MODEL // 2, tn_mix), lambda i, j: (i, j)),
        out_shape=jax.ShapeDtypeStruct((D_MODEL, D_IN_PAD), BF16), scale=1.0, name="mix_in_dw")
    dw_mix_s = _split_shards(dw_mix, cs_mix, name="dw_mix_split")
    dw_pool_s = dw_pool.reshape(4, N_CHIPS, POOL_GROUP_DIM // N_CHIPS, POOL_GROUP_DIM).transpose(1, 0, 2, 3)
    dw_pool_s = dw_pool_s.reshape(N_CHIPS, POOL_GROUP_DIM, POOL_GROUP_DIM).astype(BF16)
    tok = ex.begin_reduce("mix", {"w_in_mix": dw_mix_s,
                                  "w_out_mix": dw_omix.reshape(N_CHIPS, D_MODEL // N_CHIPS, D_MODEL),
                                  "w_pool": dw_pool_s})
    dh1, dh1b, d_gm = _mm_nt_rmsbwd(du, w_mix, h1, dh2, gm + tok[0, 0], tk=tn_mix, name="mix_in_dx")
    tok = ex.start_reduce("mix", after=dh1b)
    tok = tok + ex.begin_reduce("ffn1_out", {"ffn1_w_out": _ffn_dw_out(dh1b, a1, "ffn1")})
    dgu1 = _ffn_bwd_act(dh1b, w1_out, fa1, tok, name="ffn1_bwd_act")
    theirs = _ffn_dw_in_half(n1t, dgu1, 1 - core, None, w1_in.shape, name="ffn1_dw_in_other")
    tok = ex.send_to_sibling("ffn1_in", [theirs])
    tok = ex.start_reduce("ffn1_out", after=tok)
    pair = _ffn_dw_in_half(n1t, dgu1, core, ex.from_sibling("ffn1_in", after=tok)[0], w1_in.shape,
                           name="ffn1_dw_in_own")
    tok = ex.start_exchange("ffn1_in", ["ffn1_w_in"], [pair])
    dx, _, d_g1 = _mm_nt_rmsbwd(dgu1, w1_in, h0, dh1, g1 + tok[0, 0], tk=SHARD_TILE, name="ffn1_dx")
    small = [d_g1, d_gm, d_g2, d_gf, d_pscale, d_ba, d_gn, d_wa[:GLA_GATE_RANK]]
    return loss_blk, dx, small
```

```python
import functools

import jax
import jax.numpy as jnp
from jax import lax
from jax.experimental import pallas as pl
from jax.experimental.pallas import tpu as pltpu

F32 = jnp.float32
BF16 = jnp.bfloat16
MESH = pl.DeviceIdType.MESH

D_MODEL = 2048
D_FF = 5632
D_POOL = 1024
POOL_WINDOWS = (2, 4, 8, 16)
POOL_GROUP_DIM = 256
D_GLA = 1024
GLA_HEADS = 4
GLA_DV = 256
GLA_DK = 128
GLA_DK_TOTAL = 512
GLA_GATE_RANK = 16
GATE_LOGIT_NORMALIZER = 16.0
CHUNK = 64
D_IN = 4112
D_IN_PAD = 4224
EPS = 1e-6

ADAM_LR = 0.001
ADAM_B1 = 0.9
ADAM_B2 = 0.999
ADAM_EPS = 1e-08
ADAM_WD = 0.01
ADAM_STEP = 10

N_CHIPS = 4
N_DEV = 8
V7X_VMEM_BYTES = 64 * 1024 * 1024
LANES = 128
MXU_TILE = 256
COL_TILE = 1408
SHARD_TILE = 2816


def _cparams(semantics, vmem_mb):
    assert vmem_mb * 1024 * 1024 < V7X_VMEM_BYTES
    return pltpu.CompilerParams(dimension_semantics=semantics, vmem_limit_bytes=vmem_mb * 1024 * 1024)


def _dot_nn(a, b):
    return jnp.dot(a, b, preferred_element_type=F32)


def _dot_nt(a, b):
    return lax.dot_general(a, b, (((1,), (1,)), ((), ())), preferred_element_type=F32)


def _dot_tn(a, b):
    return lax.dot_general(a, b, (((0,), (0,)), ((), ())), preferred_element_type=F32)


def _sigmoid(x):
    return 1.0 / (1.0 + jnp.exp(-x))


def _rms_fwd(x, g, early, *, name):
    S, D = x.shape
    tm = 256

    def body(x_ref, g_ref, *rest):
        o_ref, ot_ref = rest[len(early):]
        xv = x_ref[...]
        r = lax.rsqrt(jnp.mean(xv * xv, axis=-1, keepdims=True) + EPS)
        n = xv * r * g_ref[...]
        o_ref[...] = n.astype(BF16)
        ot_ref[...] = n.T.astype(BF16)

    return pl.pallas_call(
        body, name=name, grid=(S // tm,),
        in_specs=[pl.BlockSpec((tm, D), lambda i: (i, 0)), pl.BlockSpec((1, D), lambda i: (0, 0))]
        + [pl.BlockSpec(memory_space=pl.ANY)] * len(early),
        out_specs=[pl.BlockSpec((tm, D), lambda i: (i, 0)), pl.BlockSpec((D, tm), lambda i: (0, i))],
        out_shape=[jax.ShapeDtypeStruct((S, D), BF16), jax.ShapeDtypeStruct((D, S), BF16)],
        compiler_params=_cparams(("parallel",), 32),
    )(x, g, *early)


def _ffn_up(n, w_in, early, *, name):
    S, D = n.shape
    ns, _, cs = w_in.shape
    half = ns // 2
    F = cs * half
    tm, tn = 128, SHARD_TILE
    nb = cs // tn

    def body(n_ref, wg_ref, wu_ref, *rest):
        fa_ref, a_ref = rest[len(early):]
        nv = n_ref[...]
        g = _dot_nn(nv, wg_ref[...])
        u = _dot_nn(nv, wu_ref[...])
        s = _sigmoid(g)
        silu = g * s
        fa_ref[0] = (u * (s * (1.0 + g * (1.0 - s)))).astype(BF16)
        fa_ref[1] = silu.astype(BF16)
        a_ref[...] = (silu * u).astype(BF16)

    return pl.pallas_call(
        body, name=name, grid=(F // tn, S // tm),
        in_specs=[
            pl.BlockSpec((tm, D), lambda j, i: (i, 0)),
            pl.BlockSpec((None, D, tn), lambda j, i: (lax.div(j, nb), 0, lax.rem(j, nb))),
            pl.BlockSpec((None, D, tn), lambda j, i: (half + lax.div(j, nb), 0, lax.rem(j, nb))),
        ] + [pl.BlockSpec(memory_space=pl.ANY)] * len(early),
        out_specs=[
            pl.BlockSpec((2, tm, tn), lambda j, i: (0, i, j)),
            pl.BlockSpec((tm, tn), lambda j, i: (i, j)),
        ],
        out_shape=[jax.ShapeDtypeStruct((2, S, F), BF16), jax.ShapeDtypeStruct((S, F), BF16)],
        compiler_params=_cparams(("parallel", "parallel"), 56),
    )(n, w_in, w_in, *early)


def _mm_nn(a, b, resid, scale, *, tm, tn, tk, name):
    S, K = a.shape
    N = b.shape[1]
    nk = K // tk

    def body(*refs):
        if resid is None:
            a_ref, b_ref, o_ref, acc_ref = refs
            r_ref = None
        else:
            a_ref, b_ref, r_ref, o_ref, acc_ref = refs
        k = pl.program_id(2)

        @pl.when(k == 0)
        def _():
            acc_ref[...] = jnp.zeros_like(acc_ref)

        acc_ref[...] += _dot_nn(a_ref[...], b_ref[...])

        @pl.when(k == nk - 1)
        def _():
            out = acc_ref[...] * scale
            if r_ref is not None:
                out = r_ref[...] + out
            o_ref[...] = out

    in_specs = [pl.BlockSpec((tm, tk), lambda i, j, k: (i, k)), pl.BlockSpec((tk, tn), lambda i, j, k: (k, j))]
    args = [a, b]
    if resid is not None:
        in_specs.append(pl.BlockSpec((tm, tn), lambda i, j, k: (i, j)))
        args.append(resid)
    return pl.pallas_call(
        body, name=name, grid=(S // tm, N // tn, nk),
        in_specs=in_specs,
        out_specs=pl.BlockSpec((tm, tn), lambda i, j, k: (i, j)),
        out_shape=jax.ShapeDtypeStruct((S, N), F32),
        scratch_shapes=[pltpu.VMEM((tm, tn), F32)],
        compiler_params=_cparams(("parallel", "parallel", "arbitrary"), 48),
    )(*args)


def _mm_nt(a, b, after, *, tm, tn, name):
    S, K = a.shape
    N = b.shape[0]

    def body(a_ref, b_ref, after_ref, o_ref):
        o_ref[...] = _dot_nt(a_ref[...], b_ref[...])

    return pl.pallas_call(
        body, name=name, grid=(N // tn, S // tm),
        in_specs=[pl.BlockSpec((tm, K), lambda j, i: (i, 0)), pl.BlockSpec((tn, K), lambda j, i: (j, 0)),
                  pl.BlockSpec(memory_space=pl.ANY)],
        out_specs=pl.BlockSpec((tm, tn), lambda j, i: (i, j)),
        out_shape=jax.ShapeDtypeStruct((S, N), F32),
        compiler_params=_cparams(("parallel", "parallel"), 48),
    )(a, b, after)


def _mm_tn(a, b, *, grid, a_spec, b_spec, out_spec, out_shape, scale, name, a_is_transposed=False):
    dot = _dot_nn if a_is_transposed else _dot_tn

    def body(a_ref, b_ref, o_ref):
        o_ref[...] = (scale * dot(a_ref[...], b_ref[...])).astype(o_ref.dtype)

    return pl.pallas_call(
        body, name=name, grid=grid, in_specs=[a_spec, b_spec], out_specs=out_spec, out_shape=out_shape,
        compiler_params=_cparams(("parallel",) * len(grid), 56),
    )(a, b)


def _ffn_bwd_act(dhb, w_out, fa, after, *, name):
    S, D = dhb.shape
    F = w_out.shape[0]
    tm, tn = 256, SHARD_TILE

    def body(dh_ref, w_ref, fa_ref, after_ref, dgu_ref):
        da = 0.5 * _dot_nt(dh_ref[...], w_ref[...])
        dgu_ref[0] = (da * fa_ref[0].astype(F32)).astype(BF16)
        dgu_ref[1] = (da * fa_ref[1].astype(F32)).astype(BF16)

    return pl.pallas_call(
        body, name=name, grid=(F // tn, S // tm),
        in_specs=[
            pl.BlockSpec((tm, D), lambda j, i: (i, 0)),
            pl.BlockSpec((tn, D), lambda j, i: (j, 0)),
            pl.BlockSpec((2, tm, tn), lambda j, i: (0, i, j)),
            pl.BlockSpec(memory_space=pl.ANY),
        ],
        out_specs=pl.BlockSpec((2, tm, tn), lambda j, i: (0, i, j)),
        out_shape=jax.ShapeDtypeStruct((2, S, F), BF16),
        compiler_params=_cparams(("parallel", "parallel"), 56),
    )(dhb, w_out, fa, after)


def _mm_nt_rmsbwd(dact, w, h_in, dh_out, g, *, tk, name):
    ng, S, fg = dact.shape
    ns, D, cs = w.shape
    assert ng * fg == ns * cs
    tm, rc = 512, 64
    kpg, kps = fg // tk, cs // tk
    nk = ng * kpg

    def body(a_ref, w_ref, h_ref, dho_ref, g_ref, dh_ref, dhb_ref, dg_ref, acc_ref):
        i = pl.program_id(0)
        k = pl.program_id(1)

        @pl.when(k == 0)
        def _():
            acc_ref[...] = jnp.zeros_like(acc_ref)

        acc_ref[...] += _dot_nt(a_ref[...], w_ref[...])

        @pl.when(jnp.logical_and(i == 0, k == 0))
        def _():
            dg_ref[...] = jnp.zeros_like(dg_ref)

        @pl.when(k == nk - 1)
        def _():
            gv = g_ref[...]

            def rows_step(c, dg):
                rows = pl.ds(pl.multiple_of(c * rc, rc), rc)
                dn = acc_ref[rows, :]
                xv = h_ref[rows, :]
                r = lax.rsqrt(jnp.mean(xv * xv, axis=-1, keepdims=True) + EPS)
                xh = xv * r
                dng = dn * gv
                dx = r * (dng - xh * jnp.mean(dng * xh, axis=-1, keepdims=True))
                out = dho_ref[rows, :] + dx
                dh_ref[rows, :] = out
                dhb_ref[rows, :] = out.astype(BF16)
                return dg + jnp.sum(dn * xh, axis=0, keepdims=True)

            dg_ref[...] += lax.fori_loop(0, tm // rc, rows_step, jnp.zeros((1, D), F32))

    return pl.pallas_call(
        body, name=name, grid=(S // tm, nk),
        in_specs=[
            pl.BlockSpec((None, tm, tk), lambda i, k: (lax.div(k, kpg), i, lax.rem(k, kpg))),
            pl.BlockSpec((None, D, tk), lambda i, k: (lax.div(k, kps), 0, lax.rem(k, kps)),
                         **({"pipeline_mode": pl.Buffered(1)} if nk == 1 else {})),
            pl.BlockSpec((tm, D), lambda i, k: (i, 0)),
            pl.BlockSpec((tm, D), lambda i, k: (i, 0)),
            pl.BlockSpec((1, D), lambda i, k: (0, 0)),
        ],
        out_specs=[
            pl.BlockSpec((tm, D), lambda i, k: (i, 0), pipeline_mode=pl.Buffered(1)),
            pl.BlockSpec((tm, D), lambda i, k: (i, 0), pipeline_mode=pl.Buffered(1)),
            pl.BlockSpec((1, D), lambda i, k: (0, 0)),
        ],
        out_shape=[jax.ShapeDtypeStruct((S, D), F32), jax.ShapeDtypeStruct((S, D), BF16),
                   jax.ShapeDtypeStruct((1, D), F32)],
        scratch_shapes=[pltpu.VMEM((tm, D), F32)],
        compiler_params=_cparams(("arbitrary", "arbitrary"), 56),
    )(dact, w, h_in, dh_out, g)


def _final_loss(h, g, target, *, name):
    S, D = h.shape
    tm = 256

    def body(h_ref, g_ref, t_ref, loss_ref, dh_ref, dhb_ref, dg_ref):
        i = pl.program_id(0)

        @pl.when(i == 0)
        def _():
            loss_ref[...] = jnp.zeros_like(loss_ref)
            dg_ref[...] = jnp.zeros_like(dg_ref)

        xv = h_ref[...]
        gv = g_ref[...]
        r = lax.rsqrt(jnp.mean(xv * xv, axis=-1, keepdims=True) + EPS)
        xh = xv * r
        e = xh * gv - t_ref[...]
        loss_ref[...] += 0.5 * jnp.sum(jnp.mean(e * e, axis=-1, keepdims=True))
        dy = e * (1.0 / D)
        dg_ref[...] += jnp.sum(dy * xh, axis=0, keepdims=True)
        dyg = dy * gv
        dx = r * (dyg - xh * jnp.mean(dyg * xh, axis=-1, keepdims=True))
        dh_ref[...] = dx
        dhb_ref[...] = dx.astype(BF16)

    return pl.pallas_call(
        body, name=name, grid=(S // tm,),
        in_specs=[pl.BlockSpec((tm, D), lambda i: (i, 0)), pl.BlockSpec((1, D), lambda i: (0, 0)),
                  pl.BlockSpec((tm, D), lambda i: (i, 0))],
        out_specs=[pl.BlockSpec((8, LANES), lambda i: (0, 0)), pl.BlockSpec((tm, D), lambda i: (i, 0)),
                   pl.BlockSpec((tm, D), lambda i: (i, 0)), pl.BlockSpec((1, D), lambda i: (0, 0))],
        out_shape=[jax.ShapeDtypeStruct((8, LANES), F32), jax.ShapeDtypeStruct((S, D), F32),
                   jax.ShapeDtypeStruct((S, D), BF16), jax.ShapeDtypeStruct((1, D), F32)],
        compiler_params=_cparams(("arbitrary",), 40),
    )(h, g, target)


POOL_HALO = 16
POOL_ROWS = 256


def _pool_window_mean_minus_token(ext, tok0, w):
    s = ext
    k = 1
    while k < w:
        s = s + pltpu.roll(s, k, 0)
        k *= 2
    win = s[POOL_HALO:, :]
    tok = tok0 + lax.broadcasted_iota(jnp.int32, (POOL_ROWS, 1), 0)
    cnt = jnp.minimum(tok + 1, w).astype(F32)
    return win / cnt - ext[POOL_HALO:, :], cnt


def _pool_fwd(u, w_pool, scale, *, name):
    S = u.shape[0]
    C = POOL_GROUP_DIM
    nsteps = S // POOL_ROWS

    def body(p_ref, w_ref, sc_ref, y_ref, xp_ref):
        xp_ref[0:POOL_HALO, :] = jnp.zeros((POOL_HALO, D_POOL), F32)
        xp_ref[POOL_HALO:, :] = p_ref[...]
        for gi, win in enumerate(POOL_WINDOWS):
            cols = slice(gi * C, (gi + 1) * C)

            def step(c, carry, cols=cols, win=win, gi=gi):
                r0 = pl.multiple_of(c * POOL_ROWS, POOL_ROWS)
                ext = xp_ref[pl.ds(r0, POOL_ROWS + POOL_HALO), cols]
                pooled, _ = _pool_window_mean_minus_token(ext, r0, win)
                y = _dot_nn(pooled.astype(BF16), w_ref[gi]) * sc_ref[:, cols]
                y_ref[pl.ds(r0, POOL_ROWS), cols] = y.astype(BF16)
                return carry

            lax.fori_loop(0, nsteps, step, 0)

    return pl.pallas_call(
        body, name=name, grid=(1,),
        in_specs=[pl.BlockSpec((S, D_POOL), lambda i: (0, 0)),
                  pl.BlockSpec((4, C, C), lambda i: (0, 0, 0)),
                  pl.BlockSpec((1, D_POOL), lambda i: (0, 0))],
        out_specs=pl.BlockSpec((S, D_POOL), lambda i: (0, 0)),
        out_shape=jax.ShapeDtypeStruct((S, D_POOL), BF16),
        scratch_shapes=[pltpu.VMEM((S + POOL_HALO, D_POOL), F32)],
        compiler_params=_cparams(("arbitrary",), 48),
    )(u, w_pool, scale)


def _pool_bwd(u, dcat, w_pool, scale, *, name):
    S = u.shape[0]
    C = POOL_GROUP_DIM
    nsteps = S // POOL_ROWS

    def body(p_ref, dy_ref, w_ref, sc_ref, dp_ref, dw_ref, dsc_ref, xp_ref, e_ref, neg_ref):
        xp_ref[0:POOL_HALO, :] = jnp.zeros((POOL_HALO, D_POOL), F32)
        xp_ref[POOL_HALO:, :] = p_ref[...]
        e_ref[S:, :] = jnp.zeros((POOL_HALO, C), F32)
        for gi, win in enumerate(POOL_WINDOWS):
            cols = slice(gi * C, (gi + 1) * C)

            def step_a(c, carry, cols=cols, win=win, gi=gi):
                dw, dsc = carry
                r0 = pl.multiple_of(c * POOL_ROWS, POOL_ROWS)
                ext = xp_ref[pl.ds(r0, POOL_ROWS + POOL_HALO), cols]
                pooled, cnt = _pool_window_mean_minus_token(ext, r0, win)
                pb = pooled.astype(BF16)
                wv = w_ref[gi]
                dy = dy_ref[pl.ds(r0, POOL_ROWS), cols]
                dsc = dsc + jnp.sum(dy * _dot_nn(pb, wv), axis=0, keepdims=True)
                dyp = (dy * sc_ref[:, cols]).astype(BF16)
                dw = dw + _dot_tn(pb, dyp)
                dpooled = _dot_nt(dyp, wv)
                e_ref[pl.ds(r0, POOL_ROWS), :] = dpooled / cnt
                neg_ref[pl.ds(r0, POOL_ROWS), :] = -dpooled
                return dw, dsc

            dw, dsc = lax.fori_loop(0, nsteps, step_a, (jnp.zeros((C, C), F32), jnp.zeros((1, C), F32)))
            dw_ref[gi] = dw
            dsc_ref[:, cols] = dsc

            def step_b(c, carry, cols=cols, win=win):
                r0 = pl.multiple_of(c * POOL_ROWS, POOL_ROWS)
                s = e_ref[pl.ds(r0, POOL_ROWS + POOL_HALO), :]
                n = POOL_ROWS + POOL_HALO
                k = 1
                while k < win:
                    s = s + pltpu.roll(s, n - k, 0)
                    k *= 2
                du = s[:POOL_ROWS, :] + neg_ref[pl.ds(r0, POOL_ROWS), :]
                dp_ref[pl.ds(r0, POOL_ROWS), cols] = du.astype(BF16)
                return carry

            lax.fori_loop(0, nsteps, step_b, 0)

    return pl.pallas_call(
        body, name=name, grid=(1,),
        in_specs=[pl.BlockSpec((S, D_POOL), lambda i: (0, 0)),
                  pl.BlockSpec((S, D_POOL), lambda i: (0, 0)),
                  pl.BlockSpec((4, C, C), lambda i: (0, 0, 0)),
                  pl.BlockSpec((1, D_POOL), lambda i: (0, 0))],
        out_specs=[pl.BlockSpec((S, D_POOL), lambda i: (0, 0)),
                   pl.BlockSpec((4, C, C), lambda i: (0, 0, 0)),
                   pl.BlockSpec((1, D_POOL), lambda i: (0, 0))],
        out_shape=[jax.ShapeDtypeStruct((S, D_POOL), BF16), jax.ShapeDtypeStruct((4, C, C), F32),
                   jax.ShapeDtypeStruct((1, D_POOL), F32)],
        scratch_shapes=[pltpu.VMEM((S + POOL_HALO, D_POOL), F32), pltpu.VMEM((S + POOL_HALO, C), F32),
                        pltpu.VMEM((S, C), F32)],
        compiler_params=_cparams(("arbitrary",), 56),
    )(u, dcat, w_pool, scale)


GLA_ROWS = 128
U_Q_BLK, U_K_BLK = 2, 3
U_V_BLK, U_G_BLK = 2, 3
U_R_BLK = 32


def _prefix_sum_rows(x):
    n = x.shape[0]
    row = lax.broadcasted_iota(jnp.int32, x.shape, 0)
    k = 1
    while k < n:
        x = x + jnp.where(row >= k, pltpu.roll(x, k, 0), 0.0)
        k *= 2
    return x


def _suffix_sum_rows(x):
    n = x.shape[0]
    row = lax.broadcasted_iota(jnp.int32, x.shape, 0)
    k = 1
    while k < n:
        x = x + jnp.where(row < n - k, pltpu.roll(x, n - k, 0), 0.0)
        k *= 2
    return x


def _log_sigmoid(z):
    return jnp.minimum(z, 0.0) - jnp.log(1.0 + jnp.exp(-jnp.abs(z)))


def _gla_chunk_terms(la_c, q_c, k_c):
    bc = _prefix_sum_rows(la_c)
    bl = jnp.sum(la_c, axis=0, keepdims=True)
    eb = jnp.exp(bc)
    enb = jnp.exp(-bc)
    etail = jnp.exp(bl - bc)
    qd = q_c * (GLA_DK ** -0.5) * eb
    ki = k_c * enb
    kt = k_c * etail
    d = jnp.exp(bl)
    return eb, enb, etail, qd, ki, kt, d


def _gla_fwd(u, y_pool, w_alpha, b_alpha, gnorm, *, name):
    S = u.shape[0]
    RB = GLA_ROWS
    ncc = RB // CHUNK
    H, DK, DV = GLA_HEADS, GLA_DK, GLA_DV

    def body(q_ref, k_ref, v_ref, go_ref, r_ref, yp_ref, wa_ref, ba_ref, gn_ref, cat_ref, o_ref, st_ref, state):
        i = pl.program_id(0)

        @pl.when(i == 0)
        def _():
            state[...] = jnp.zeros_like(state)

        cat_ref[:, :D_POOL] = yp_ref[...]
        y_ref = cat_ref.at[:, D_POOL:]

        z = _dot_nn(r_ref[...].astype(BF16), wa_ref[...]) + ba_ref[...]
        la = _log_sigmoid(z) / GATE_LOGIT_NORMALIZER
        ri = lax.broadcasted_iota(jnp.int32, (CHUNK, CHUNK), 0)
        ci = lax.broadcasted_iota(jnp.int32, (CHUNK, CHUNK), 1)
        tri = ri >= ci
        gn = gn_ref[...]
        for cc in range(ncc):
            rs = slice(cc * CHUNK, (cc + 1) * CHUNK)
            for h in range(H):
                ks = slice(h * DK, (h + 1) * DK)
                vs = slice(h * DV, (h + 1) * DV)
                _, _, _, qd, ki, kt, d = _gla_chunk_terms(la[rs, ks], q_ref[rs, ks], k_ref[rs, ks])
                qdb = qd.astype(BF16)
                vb = v_ref[rs, vs].astype(BF16)
                p = jnp.where(tri, _dot_nt(qdb, ki.astype(BF16)), 0.0)
                st = state[h]
                st_ref[cc, h] = st
                o = _dot_nn(p.astype(BF16), vb) + _dot_nt(qdb, st.astype(BF16))
                state[h] = st * d + _dot_tn(vb, kt.astype(BF16))
                o_ref[rs, vs] = o
                rinv = lax.rsqrt(jnp.mean(o * o, axis=-1, keepdims=True) + EPS)
                go = go_ref[rs, vs]
                y_ref[rs, vs] = (o * rinv * gn * (go * _sigmoid(go))).astype(BF16)

    nblk = S // RB
    return pl.pallas_call(
        body, name=name, grid=(nblk,),
        in_specs=[
            pl.BlockSpec((RB, GLA_DK_TOTAL), lambda i: (i, U_Q_BLK)),
            pl.BlockSpec((RB, GLA_DK_TOTAL), lambda i: (i, U_K_BLK)),
            pl.BlockSpec((RB, D_GLA), lambda i: (i, U_V_BLK)),
            pl.BlockSpec((RB, D_GLA), lambda i: (i, U_G_BLK)),
            pl.BlockSpec((RB, LANES), lambda i: (i, U_R_BLK)),
            pl.BlockSpec((RB, D_POOL), lambda i: (i, 0)),
            pl.BlockSpec((LANES, GLA_DK_TOTAL), lambda i: (0, 0)),
            pl.BlockSpec((1, GLA_DK_TOTAL), lambda i: (0, 0)),
            pl.BlockSpec((1, DV), lambda i: (0, 0)),
        ],
        out_specs=[
            pl.BlockSpec((RB, D_POOL + D_GLA), lambda i: (i, 0)),
            pl.BlockSpec((RB, D_GLA), lambda i: (i, 0)),
            pl.BlockSpec((ncc, H, DV, DK), lambda i: (i, 0, 0, 0)),
        ],
        out_shape=[jax.ShapeDtypeStruct((S, D_POOL + D_GLA), BF16), jax.ShapeDtypeStruct((S, D_GLA), F32),
                   jax.ShapeDtypeStruct((S // CHUNK, H, DV, DK), F32)],
        scratch_shapes=[pltpu.VMEM((H, DV, DK), F32)],
        compiler_params=_cparams(("arbitrary",), 32),
    )(u, u, u, u, u, y_pool, w_alpha, b_alpha, gnorm)


def _gla_bwd(u, o, states, dcat, dpool, w_alpha, b_alpha, gnorm, *, name):
    S = u.shape[0]
    RB = GLA_ROWS
    ncc = RB // CHUNK
    H, DK, DV = GLA_HEADS, GLA_DK, GLA_DV
    nblk = S // RB
    o_q, o_k = D_POOL, D_POOL + GLA_DK_TOTAL
    o_v, o_g, o_r = o_k + GLA_DK_TOTAL, o_k + GLA_DK_TOTAL + D_GLA, o_k + GLA_DK_TOTAL + 2 * D_GLA

    def body(q_ref, k_ref, v_ref, go_ref, r_ref, o_ref, st_ref, dy_ref, dpool_ref, wa_ref, ba_ref, gn_ref,
             du_ref, dwa_ref, dba_ref, dgn_ref, dstate, dz_ref):
        i = pl.program_id(0)

        @pl.when(i == 0)
        def _():
            dstate[...] = jnp.zeros_like(dstate)
            dwa_ref[...] = jnp.zeros_like(dwa_ref)
            dba_ref[...] = jnp.zeros_like(dba_ref)
            dgn_ref[...] = jnp.zeros_like(dgn_ref)

        du_ref[:, :o_q] = dpool_ref[...]
        dq_ref, dk_ref = du_ref.at[:, o_q:o_k], du_ref.at[:, o_k:o_v]
        dv_ref, dgo_ref, dr_ref = du_ref.at[:, o_v:o_g], du_ref.at[:, o_g:o_r], du_ref.at[:, o_r:]

        rb = r_ref[...].astype(BF16)
        wa = wa_ref[...]
        z = _dot_nn(rb, wa) + ba_ref[...]
        la = _log_sigmoid(z) / GATE_LOGIT_NORMALIZER
        ri = lax.broadcasted_iota(jnp.int32, (CHUNK, CHUNK), 0)
        ci = lax.broadcasted_iota(jnp.int32, (CHUNK, CHUNK), 1)
        tri = ri >= ci
        last_row = lax.broadcasted_iota(jnp.int32, (CHUNK, DK), 0) == CHUNK - 1
        gn = gn_ref[...]
        dgn = jnp.zeros((1, DV), F32)
        for cc in reversed(range(ncc)):
            rs = slice(cc * CHUNK, (cc + 1) * CHUNK)
            for h in range(H):
                ks = slice(h * DK, (h + 1) * DK)
                vs = slice(h * DV, (h + 1) * DV)
                eb, enb, etail, qd, ki, kt, d = _gla_chunk_terms(la[rs, ks], q_ref[rs, ks], k_ref[rs, ks])
                qdb, kib, ktb = qd.astype(BF16), ki.astype(BF16), kt.astype(BF16)
                vb = v_ref[rs, vs].astype(BF16)
                p = jnp.where(tri, _dot_nt(qdb, kib), 0.0)
                ov = o_ref[rs, vs]
                go = go_ref[rs, vs]
                dy = dy_ref[rs, vs]
                rinv = lax.rsqrt(jnp.mean(ov * ov, axis=-1, keepdims=True) + EPS)
                oh = ov * rinv
                sg = _sigmoid(go)
                dgo_ref[rs, vs] = (dy * (oh * gn) * (sg * (1.0 + go * (1.0 - sg)))).astype(BF16)
                don = dy * (go * sg)
                dgn = dgn + jnp.sum(don * oh, axis=0, keepdims=True)
                doh = don * gn
                do = rinv * (doh - oh * jnp.mean(doh * oh, axis=-1, keepdims=True))
                dob = do.astype(BF16)
                st = st_ref[cc, h]
                dst = dstate[h]
                stb, dstb = st.astype(BF16), dst.astype(BF16)
                dp = jnp.where(tri, _dot_nt(dob, vb), 0.0).astype(BF16)
                dv_ref[rs, vs] = (_dot_tn(p.astype(BF16), dob) + _dot_nt(ktb, dstb)).astype(BF16)
                dqd = _dot_nn(dp, kib) + _dot_nn(dob, stb)
                dki = _dot_tn(dp, qdb)
                dkt = _dot_nn(vb, dstb)
                dd = jnp.sum(dst * st, axis=0, keepdims=True)
                dstate[h] = dst * d + _dot_tn(dob, qdb)
                dq_ref[rs, ks] = (dqd * eb * (DK ** -0.5)).astype(BF16)
                dk_ref[rs, ks] = (dki * enb + dkt * etail).astype(BF16)
                dbl = jnp.sum(dkt * kt, axis=0, keepdims=True) + dd * d
                dbc = dqd * qd - dki * ki - dkt * kt
                dbc = dbc + jnp.where(last_row, dbl, 0.0)
                dla = _suffix_sum_rows(dbc)
                dz_ref[rs, ks] = dla * (1.0 / GATE_LOGIT_NORMALIZER) * (1.0 - _sigmoid(z[rs, ks]))
        dz = dz_ref[...]
        dzb = dz.astype(BF16)
        dr_ref[...] = _dot_nt(dzb, wa).astype(BF16)
        dwa_ref[...] += _dot_tn(rb, dzb)
        dba_ref[...] += jnp.sum(dz, axis=0, keepdims=True)
        dgn_ref[...] += dgn

    def rev(blk):
        return lambda i: (nblk - 1 - i, blk)

    return pl.pallas_call(
        body, name=name, grid=(nblk,),
        in_specs=[
            pl.BlockSpec((RB, GLA_DK_TOTAL), rev(U_Q_BLK)),
            pl.BlockSpec((RB, GLA_DK_TOTAL), rev(U_K_BLK)),
            pl.BlockSpec((RB, D_GLA), rev(U_V_BLK)),
            pl.BlockSpec((RB, D_GLA), rev(U_G_BLK)),
            pl.BlockSpec((RB, LANES), rev(U_R_BLK)),
            pl.BlockSpec((RB, D_GLA), rev(0)),
            pl.BlockSpec((ncc, H, DV, DK), lambda i: (nblk - 1 - i, 0, 0, 0)),
            pl.BlockSpec((RB, D_GLA), rev(1)),
            pl.BlockSpec((RB, D_POOL), rev(0)),
            pl.BlockSpec((LANES, GLA_DK_TOTAL), lambda i: (0, 0)),
            pl.BlockSpec((1, GLA_DK_TOTAL), lambda i: (0, 0)),
            pl.BlockSpec((1, DV), lambda i: (0, 0)),
        ],
        out_specs=[
            pl.BlockSpec((RB, D_IN_PAD), rev(0)),
            pl.BlockSpec((LANES, GLA_DK_TOTAL), lambda i: (0, 0)),
            pl.BlockSpec((1, GLA_DK_TOTAL), lambda i: (0, 0)),
            pl.BlockSpec((1, DV), lambda i: (0, 0)),
        ],
        out_shape=[
            jax.ShapeDtypeStruct((S, D_IN_PAD), BF16),
            jax.ShapeDtypeStruct((LANES, GLA_DK_TOTAL), F32), jax.ShapeDtypeStruct((1, GLA_DK_TOTAL), F32),
            jax.ShapeDtypeStruct((1, DV), F32),
        ],
        scratch_shapes=[pltpu.VMEM((H, DV, DK), F32), pltpu.VMEM((RB, GLA_DK_TOTAL), F32)],
        compiler_params=_cparams(("arbitrary",), 32),
    )(u, u, u, u, u, o, states, dcat, dpool, w_alpha, b_alpha, gnorm)


def _concat_shards(w4, width, *, name):
    ns, R, cs = w4.shape
    tr = 256

    def body(w_ref, o_ref):
        for t in range(ns):
            o_ref[:, t * cs:(t + 1) * cs] = w_ref[t]
        o_ref[:, ns * cs:] = jnp.zeros((tr, width - ns * cs), o_ref.dtype)

    return pl.pallas_call(
        body, name=name, grid=(R // tr,),
        in_specs=[pl.BlockSpec((ns, tr, cs), lambda i: (0, i, 0))],
        out_specs=pl.BlockSpec((tr, width), lambda i: (i, 0)),
        out_shape=jax.ShapeDtypeStruct((R, width), w4.dtype),
        compiler_params=_cparams(("parallel",), 32),
    )(w4)


def _split_shards(a, cs, *, name):
    R, width = a.shape
    tr = 256

    def body(a_ref, o_ref):
        for t in range(N_CHIPS):
            o_ref[t] = a_ref[:, t * cs:(t + 1) * cs]

    return pl.pallas_call(
        body, name=name, grid=(R // tr,),
        in_specs=[pl.BlockSpec((tr, width), lambda i: (i, 0))],
        out_specs=pl.BlockSpec((N_CHIPS, tr, cs), lambda i: (0, i, 0)),
        out_shape=jax.ShapeDtypeStruct((N_CHIPS, R, cs), a.dtype),
        compiler_params=_cparams(("parallel",), 32),
    )(a)


def _row_tile(rows, cols, itemsize, budget=2 * 1024 * 1024):
    if rows * cols * itemsize <= budget or rows % 16:
        return rows
    best = 16
    for t in range(16, rows + 1, 16):
        if rows % t == 0 and t * cols * itemsize <= budget:
            best = t
    return best


def _adamw(w, g, m, v, after, *, name):
    R, C = w.shape
    tr = _row_tile(R, C, 4, budget=1024 * 1024)

    def body(w_ref, g_ref, m_ref, v_ref, after_ref, go_ref, d_ref, nm_ref, nv_ref):
        gv = g_ref[...]
        go_ref[...] = gv
        mn = ADAM_B1 * m_ref[...] + (1.0 - ADAM_B1) * gv
        vn = ADAM_B2 * v_ref[...] + (1.0 - ADAM_B2) * jnp.square(gv)
        m_hat = mn / (1.0 - ADAM_B1 ** ADAM_STEP)
        v_hat = vn / (1.0 - ADAM_B2 ** ADAM_STEP)
        d_ref[...] = -ADAM_LR * (m_hat / (jnp.sqrt(v_hat) + ADAM_EPS) + ADAM_WD * w_ref[...])
        nm_ref[...] = mn
        nv_ref[...] = vn

    spec = pl.BlockSpec((tr, C), lambda i: (i, 0))
    shp = jax.ShapeDtypeStruct((R, C), F32)
    return pl.pallas_call(
        body, name=name, grid=(R // tr,), in_specs=[spec] * 4 + [pl.BlockSpec(memory_space=pl.ANY)],
        out_specs=[spec] * 4, out_shape=[shp] * 4,
        compiler_params=_cparams(("parallel",), 48),
    )(w, g, m, v, after)


def _pair_sum(g4, recv, c_idx, *, name):
    ns, _, R2, C = g4.shape
    tr = _row_tile(R2, C, 2)

    def body(c_ref, g_ref, r_ref, o_ref):
        o_ref[...] = (g_ref[...].astype(F32) + r_ref[...].astype(F32)).astype(BF16)

    return pl.pallas_call(
        body, name=name,
        grid_spec=pltpu.PrefetchScalarGridSpec(
            num_scalar_prefetch=1, grid=(ns, R2 // tr),
            in_specs=[pl.BlockSpec((None, None, tr, C), lambda s, i, c: (s, c[0], i, 0)),
                      pl.BlockSpec((None, tr, C), lambda s, i, c: (s, i, 0))],
            out_specs=pl.BlockSpec((None, tr, C), lambda s, i, c: (s, i, 0)),
        ),
        out_shape=jax.ShapeDtypeStruct((ns, R2, C), BF16),
        compiler_params=_cparams(("parallel", "parallel"), 32),
    )(c_idx, g4, recv)


def _chip_sum(part, recv, sc_idx, *, name):
    _, R2, C = part.shape
    tr = _row_tile(R2, C, 4)
    nblk = R2 // tr

    def body(s_ref, p_ref, r_ref, o_ref):
        acc = p_ref[...].astype(F32)
        for j in range(N_CHIPS - 1):
            acc = acc + r_ref[j].astype(F32)
        o_ref[...] = acc

    return pl.pallas_call(
        body, name=name,
        grid_spec=pltpu.PrefetchScalarGridSpec(
            num_scalar_prefetch=1, grid=(nblk,),
            in_specs=[pl.BlockSpec((None, tr, C), lambda i, s: (s[0], i, 0)),
                      pl.BlockSpec((N_CHIPS - 1, tr, C), lambda i, s: (0, i, 0))],
            out_specs=pl.BlockSpec((tr, C), lambda i, s: (s[1] * nblk + i, 0)),
        ),
        out_shape=jax.ShapeDtypeStruct((2 * R2, C), F32),
        compiler_params=_cparams(("parallel",), 32),
    )(sc_idx, part, recv)


def _cast_into_slot(w, sc_idx, dtype, after, *, name):
    R, C = w.shape
    tr = _row_tile(R, C, 4)

    def body(s_ref, w_ref, after_ref, o_ref):
        o_ref[...] = w_ref[...].astype(dtype)

    return pl.pallas_call(
        body, name=name,
        grid_spec=pltpu.PrefetchScalarGridSpec(
            num_scalar_prefetch=1, grid=(R // tr,),
            in_specs=[pl.BlockSpec((tr, C), lambda i, s: (i, 0)), pl.BlockSpec(memory_space=pl.ANY)],
            out_specs=pl.BlockSpec((None, tr, C), lambda i, s: (s[0], i, 0)),
        ),
        out_shape=jax.ShapeDtypeStruct((N_CHIPS, R, C), dtype),
        compiler_params=_cparams(("parallel",), 32),
    )(sc_idx, w, after)


def _slab_sum(slabs, *, name):
    n, M, C = slabs.shape

    def body(x_ref, o_ref):
        acc = x_ref[0]
        for d in range(1, n):
            acc = acc + x_ref[d]
        o_ref[...] = acc

    return pl.pallas_call(
        body, name=name, out_shape=jax.ShapeDtypeStruct((M, C), F32),
    )(slabs)


def _mesh_position():
    x, y, c = lax.axis_index("x"), lax.axis_index("y"), lax.axis_index("c")
    other_chips = [(1 - x, y), (x, 1 - y), (1 - x, 1 - y)]
    return x, y, c, other_chips


ANY = pl.BlockSpec(memory_space=pl.ANY)


HBM = pl.BlockSpec(memory_space=pltpu.HBM)
SEM = pl.BlockSpec(memory_space=pltpu.SEMAPHORE)
SPLIT_COPY = pltpu.CompilerParams(has_side_effects=pltpu.SideEffectType.DATAFLOW_SIDE_EFFECTING)
TOKEN = jax.ShapeDtypeStruct((8, LANES), F32)


def _in_hbm(a):
    return pltpu.with_memory_space_constraint(a, pltpu.HBM)


def _half_rows(ref, slot, half):
    hr = ref.shape[1] // 2
    return ref.at[slot, pl.ds(half * hr, hr), :]


GATHER_COPIES = {"direct": 3, "first_hop": 2, "second_hop": 1}


def _gather_routes(kind):
    x, y, c, chips = _mesh_position()
    me = 2 * x + y
    first = (x + (1 - c) * (1 - 2 * x), y + c * (1 - 2 * y))
    second = (x + c * (1 - 2 * x), y + (1 - c) * (1 - 2 * y))
    if kind == "direct":
        return [(me, (*p, c)) for p in chips], [2 * p[0] + p[1] for p in chips]
    if kind == "first_hop":
        return [(me, (*first, c)), (me, (*second, c))], [2 * first[0] + first[1], 2 * second[0] + second[1]]
    assert kind == "second_hop"
    return [(2 * first[0] + first[1], (*second, c))], [2 * (1 - x) + (1 - y)]


def _gather_ici_start(groups, kinds, *, name):
    flat = [b for g in groups for b in g]
    K, G = len(flat), len(groups)

    def body(*refs):
        ins, sems, token = refs[:K], refs[K:K + 2 * G], refs[-1]
        c = lax.axis_index("c")
        k = 0
        for gi, (g, kind) in enumerate(zip(groups, kinds)):
            sends, _ = _gather_routes(kind)
            for n in range(len(g)):
                for j, (slot, to) in enumerate(sends):
                    part = _half_rows(ins[k], slot, c)
                    pltpu.make_async_remote_copy(
                        src_ref=part, dst_ref=part, send_sem=sems[2 * gi].at[n * len(sends) + j],
                        recv_sem=sems[2 * gi + 1].at[n * len(sends) + j],
                        device_id=to, device_id_type=MESH).start()
                k += 1
        token[...] = jnp.zeros_like(token)

    sem_shapes = []
    for g, kind in zip(groups, kinds):
        sem_shapes += [pltpu.SemaphoreType.DMA((len(g) * GATHER_COPIES[kind],))] * 2
    out = pl.pallas_call(
        body, name=name,
        in_specs=[HBM] * K,
        out_specs=[SEM] * (2 * G) + [HBM] * K + [pl.BlockSpec(memory_space=pltpu.VMEM)],
        out_shape=sem_shapes + [pltpu.HBM(b.shape, b.dtype) for b in flat] + [TOKEN],
        input_output_aliases={k: 2 * G + k for k in range(K)},
        compiler_params=SPLIT_COPY,
    )(*[_in_hbm(b) for b in flat])
    handles, k = [], 2 * G
    for gi, (g, kind) in enumerate(zip(groups, kinds)):
        handles.append((out[2 * gi], out[2 * gi + 1], list(out[k:k + len(g)]), kind))
        k += len(g)
    return handles, out[-1]


def _gather_ici_wait(handle, after, *, name):
    send, recv, bufs, kind = handle
    n = len(bufs)

    def body(*refs):
        ins, send_ref, recv_ref = refs[:n], refs[n], refs[n + 1]
        c = lax.axis_index("c")
        sends, arrivals = _gather_routes(kind)
        for k in range(n):
            for j, ((slot, to), landed) in enumerate(zip(sends, arrivals)):
                cp = pltpu.make_async_remote_copy(
                    src_ref=_half_rows(ins[k], slot, c), dst_ref=_half_rows(ins[k], landed, c),
                    send_sem=send_ref.at[k * len(sends) + j], recv_sem=recv_ref.at[k * len(sends) + j],
                    device_id=to, device_id_type=MESH)
                cp.wait_send()
                cp.wait_recv()

    return pl.pallas_call(
        body, name=name,
        in_specs=[HBM] * n + [SEM, SEM, ANY], out_specs=[HBM] * n,
        out_shape=[pltpu.HBM(b.shape, b.dtype) for b in bufs],
        input_output_aliases={k: k for k in range(n)},
        compiler_params=SPLIT_COPY,
    )(*bufs, send, recv, after)


def _forward_halves(bufs, *, name):
    K = len(bufs)
    per = N_CHIPS - 1

    def body(*refs):
        outs = refs[K:2 * K]
        send_sems, recv_sems = refs[2 * K:]
        x, y, c, chips = _mesh_position()
        copies = []
        for k in range(K):
            for j, chip in enumerate(chips):
                got = _half_rows(outs[k], 2 * chip[0] + chip[1], c)
                cp = pltpu.make_async_remote_copy(
                    src_ref=got, dst_ref=got, send_sem=send_sems.at[k * per + j], recv_sem=recv_sems.at[k * per + j],
                    device_id=(x, y, 1 - c), device_id_type=MESH)
                cp.start()
                copies.append(cp)
        for cp in copies:
            cp.wait()

    return pl.pallas_call(
        body, name=name,
        in_specs=[ANY] * K, out_specs=[ANY] * K,
        out_shape=[jax.ShapeDtypeStruct(a.shape, a.dtype) for a in bufs],
        input_output_aliases={k: k for k in range(K)},
        scratch_shapes=[pltpu.SemaphoreType.DMA((K * per,)), pltpu.SemaphoreType.DMA((K * per,))],
    )(*bufs)


def _chip_exchange_copies(srcs, lands, send_sems, recv_sems):
    x, y, c, chips = _mesh_position()
    per = N_CHIPS - 1
    return [pltpu.make_async_remote_copy(
        src_ref=srcs[k].at[2 * chip[0] + chip[1]], dst_ref=lands[k].at[j],
        send_sem=send_sems.at[k * per + j], recv_sem=recv_sems.at[k * per + j],
        device_id=(*chip, c), device_id_type=MESH) for k in range(len(srcs)) for j, chip in enumerate(chips)]


def _sibling_swap_copies(srcs, lands, send_sems, recv_sems):
    x, y, c, _ = _mesh_position()
    return [pltpu.make_async_remote_copy(
        src_ref=srcs[k].at[pl.ds(0, srcs[k].shape[0]), 1 - c], dst_ref=lands[k],
        send_sem=send_sems.at[k], recv_sem=recv_sems.at[k],
        device_id=(x, y, 1 - c), device_id_type=MESH) for k in range(len(srcs))]


def _sibling_send_copies(srcs, lands, send_sems, recv_sems):
    x, y, c, _ = _mesh_position()
    return [pltpu.make_async_remote_copy(
        src_ref=srcs[k], dst_ref=lands[k], send_sem=send_sems.at[k], recv_sem=recv_sems.at[k],
        device_id=(x, y, 1 - c), device_id_type=MESH) for k in range(len(srcs))]


def _split_copy_start(srcs, land_shapes, n_sems, copies, *, name):
    K = len(srcs)

    def body(*refs):
        for cp in copies(refs[:K], refs[K:2 * K], refs[2 * K], refs[2 * K + 1]):
            cp.start()
        refs[-1][...] = jnp.zeros_like(refs[-1])

    out = pl.pallas_call(
        body, name=name,
        in_specs=[HBM] * (2 * K),
        out_specs=[SEM, SEM] + [HBM] * (2 * K) + [pl.BlockSpec(memory_space=pltpu.VMEM)],
        out_shape=[pltpu.SemaphoreType.DMA((n_sems,))] * 2
        + [pltpu.HBM(a.shape, a.dtype) for a in srcs]
        + [pltpu.HBM(s, a.dtype) for s, a in zip(land_shapes, srcs)] + [TOKEN],
        input_output_aliases={k: 2 + k for k in range(2 * K)},
        compiler_params=SPLIT_COPY,
    )(*[_in_hbm(a) for a in srcs], *[_in_hbm(lax.empty(s, a.dtype)) for s, a in zip(land_shapes, srcs)])
    return (out[0], out[1], list(out[2:2 + K]), list(out[2 + K:2 + 2 * K])), out[-1]


def _split_copy_wait(handle, copies, after, *, name):
    send, recv, srcs, lands = handle
    K = len(srcs)

    def body(*refs):
        for cp in copies(refs[:K], refs[K:2 * K], refs[2 * K], refs[2 * K + 1]):
            cp.wait_send()
            cp.wait_recv()

    out = pl.pallas_call(
        body, name=name,
        in_specs=[HBM] * (2 * K) + [SEM, SEM, ANY], out_specs=[HBM] * (2 * K),
        out_shape=[pltpu.HBM(a.shape, a.dtype) for a in srcs] + [pltpu.HBM(a.shape, a.dtype) for a in lands],
        input_output_aliases={k: k for k in range(2 * K)},
        compiler_params=SPLIT_COPY,
    )(*srcs, *lands, send, recv, after)
    return list(out[:K]), list(out[K:])


def _join_copies(bufs, send_sems, recv_sems):
    x, y, c, _ = _mesh_position()
    copies = []
    for k, buf in enumerate(bufs):
        r2 = buf.shape[0] // 2
        mine = buf.at[pl.ds(c * r2, r2), :]
        copies.append(pltpu.make_async_remote_copy(
            src_ref=mine, dst_ref=mine, send_sem=send_sems.at[k], recv_sem=recv_sems.at[k],
            device_id=(x, y, 1 - c), device_id_type=MESH))
    return copies


def _join_start(bufs, *, name):
    K = len(bufs)

    def body(*refs):
        for cp in _join_copies(refs[:K], refs[K], refs[K + 1]):
            cp.start()
        refs[-1][...] = jnp.zeros_like(refs[-1])

    out = pl.pallas_call(
        body, name=name,
        in_specs=[HBM] * K,
        out_specs=[SEM, SEM] + [HBM] * K + [pl.BlockSpec(memory_space=pltpu.VMEM)],
        out_shape=[pltpu.SemaphoreType.DMA((K,))] * 2 + [pltpu.HBM(a.shape, a.dtype) for a in bufs] + [TOKEN],
        input_output_aliases={k: 2 + k for k in range(K)},
        compiler_params=SPLIT_COPY,
    )(*[_in_hbm(a) for a in bufs])
    return (out[0], out[1], list(out[2:2 + K])), out[-1]


def _join_wait(handle, after, *, name):
    send, recv, bufs = handle
    K = len(bufs)

    def body(*refs):
        for cp in _join_copies(refs[:K], refs[K], refs[K + 1]):
            cp.wait_send()
            cp.wait_recv()

    return pl.pallas_call(
        body, name=name,
        in_specs=[HBM] * K + [SEM, SEM, ANY], out_specs=[HBM] * K,
        out_shape=[pltpu.HBM(a.shape, a.dtype) for a in bufs],
        input_output_aliases={k: k for k in range(K)},
        compiler_params=SPLIT_COPY,
    )(*bufs, send, recv, after)


def _all_gather_slab(slab):
    m_per, n = slab.shape

    def body(x_ref, out_ref, send_sems, recv_sems, local_sem):
        x, y, c, chips = _mesh_position()
        me, sibling = (x, y, c), (x, y, 1 - c)

        def rows(px, py, pc):
            return out_ref.at[pl.ds((4 * px + 2 * py + pc) * m_per, m_per), :]

        def copy(k, block, to, src=None):
            return pltpu.make_async_remote_copy(
                src_ref=rows(*block) if src is None else src, dst_ref=rows(*block),
                send_sem=send_sems.at[k], recv_sem=recv_sems.at[k], device_id=to, device_id_type=MESH)

        mine = pltpu.make_async_copy(x_ref, rows(*me), local_sem)
        mine.start()
        first = [copy(0, me, sibling, src=x_ref)]
        first += [copy(1 + j, me, (*chip, c), src=x_ref) for j, chip in enumerate(chips)]
        for cp in first:
            cp.start()
        passed = [copy(4 + j, (*chip, c), sibling) for j, chip in enumerate(chips)]
        for j, chip in enumerate(chips):
            copy(1 + j, (*chip, c), me).wait_recv()
            passed[j].start()
        copy(0, sibling, me).wait_recv()
        for j, chip in enumerate(chips):
            copy(4 + j, (*chip, 1 - c), me).wait_recv()
        for cp in first + passed:
            cp.wait_send()
        mine.wait()

    return pl.pallas_call(
        body, name="gather_small_grads",
        out_shape=jax.ShapeDtypeStruct((N_DEV * m_per, n), slab.dtype),
        in_specs=[pl.BlockSpec(memory_space=pltpu.VMEM)],
        out_specs=pl.BlockSpec(memory_space=pltpu.VMEM),
        scratch_shapes=[pltpu.SemaphoreType.DMA((7,)), pltpu.SemaphoreType.DMA((7,)), pltpu.SemaphoreType.DMA],
    )(slab)


def _ffn_dw_out(dhb, a, tag):
    S, F = a.shape
    D = dhb.shape[1]
    to = 512
    return _mm_tn(
        a, dhb, grid=(F // to,),
        a_spec=pl.BlockSpec((S, to), lambda j: (0, j)), b_spec=pl.BlockSpec((S, D), lambda j: (0, 0)),
        out_spec=pl.BlockSpec((to, D), lambda j: (j, 0)), out_shape=jax.ShapeDtypeStruct((F, D), BF16),
        scale=0.5, name=f"{tag}_dw_out").reshape(N_CHIPS, F // N_CHIPS, D)


def _ffn_dw_in_half(nt, dgu, half, add, w_in_shape, *, name):
    ns, D, cs = w_in_shape
    S = nt.shape[1]
    F = dgu.shape[2]
    ti, tr = SHARD_TILE, D // 2
    per_g, per_s = F // ti, cs // ti

    def body(h_ref, a_ref, b_ref, *rest):
        acc = _dot_nn(a_ref[...], b_ref[...])
        if add is not None:
            acc = acc + rest[0][...].astype(F32)
        rest[-1][...] = acc.astype(BF16)

    out_spec = pl.BlockSpec((None, tr, ti), lambda j, h: (lax.div(j, per_s), 0, lax.rem(j, per_s)))
    extra = [] if add is None else [add]
    return pl.pallas_call(
        body, name=name,
        grid_spec=pltpu.PrefetchScalarGridSpec(
            num_scalar_prefetch=1, grid=(2 * F // ti,),
            in_specs=[pl.BlockSpec((tr, S), lambda j, h: (h[0], 0)),
                      pl.BlockSpec((None, S, ti), lambda j, h: (lax.div(j, per_g), 0, lax.rem(j, per_g)))]
            + [out_spec] * len(extra),
            out_specs=out_spec,
        ),
        out_shape=jax.ShapeDtypeStruct((ns, tr, cs), BF16),
        compiler_params=_cparams(("parallel",), 56),
    )(half, nt, dgu, *extra)


def kernel(x, ffn1_norm, ffn1_w_in, ffn1_w_out, mix_norm, w_in_mix, w_pool, pool_scale, w_alpha, b_alpha, gla_norm, w_out_mix, ffn2_norm, ffn2_w_in, ffn2_w_out, final_norm, loss_target, m_ffn1_norm, m_ffn1_w_in, m_ffn1_w_out, m_mix_norm, m_w_in_mix, m_w_pool, m_pool_scale, m_w_alpha, m_b_alpha, m_gla_norm, m_w_out_mix, m_ffn2_norm, m_ffn2_w_in, m_ffn2_w_out, m_final_norm, v_ffn1_norm, v_ffn1_w_in, v_ffn1_w_out, v_mix_norm, v_w_in_mix, v_w_pool, v_pool_scale, v_w_alpha, v_b_alpha, v_gla_norm, v_w_out_mix, v_ffn2_norm, v_ffn2_w_in, v_ffn2_w_out, v_final_norm):
    names = ["ffn1_norm", "ffn1_w_in", "ffn1_w_out", "mix_norm", "w_in_mix", "w_pool", "pool_scale", "w_alpha",
             "b_alpha", "gla_norm", "w_out_mix", "ffn2_norm", "ffn2_w_in", "ffn2_w_out", "final_norm"]
    weights = dict(zip(names, [ffn1_norm, ffn1_w_in, ffn1_w_out, mix_norm, w_in_mix, w_pool, pool_scale, w_alpha,
                               b_alpha, gla_norm, w_out_mix, ffn2_norm, ffn2_w_in, ffn2_w_out, final_norm]))
    moms = dict(zip(names, [m_ffn1_norm, m_ffn1_w_in, m_ffn1_w_out, m_mix_norm, m_w_in_mix, m_w_pool, m_pool_scale,
                            m_w_alpha, m_b_alpha, m_gla_norm, m_w_out_mix, m_ffn2_norm, m_ffn2_w_in, m_ffn2_w_out,
                            m_final_norm]))
    vels = dict(zip(names, [v_ffn1_norm, v_ffn1_w_in, v_ffn1_w_out, v_mix_norm, v_w_in_mix, v_w_pool, v_pool_scale,
                            v_w_alpha, v_b_alpha, v_gla_norm, v_w_out_mix, v_ffn2_norm, v_ffn2_w_in, v_ffn2_w_out,
                            v_final_norm]))
    xi, yi, ci = lax.axis_index("x"), lax.axis_index("y"), lax.axis_index("c")
    chip = 2 * xi + yi
    c_idx = jnp.reshape(ci, (1,)).astype(jnp.int32)
    sc_idx = jnp.stack([chip, ci]).astype(jnp.int32)

    def flat2d(a):
        return a.reshape(-1, a.shape[-1])

    ex = _Exchanges(sc_idx, c_idx)

    def cast(n, after):
        return _cast_into_slot(flat2d(weights[n]), sc_idx, F32 if n == "w_alpha" else BF16, after, name=f"cast_{n}")

    groups = _Exchanges.GATHER_GROUPS
    w1_in = groups[0][0]
    tok = ex.start_gather({w1_in: cast(w1_in, sc_idx)}, groups[:1], ("first_hop",), name="gather_ici_start_first")
    bufs, last = {}, tok
    for g in groups[1:]:
        for n in g:
            bufs[n] = last = cast(n, last)
    bufs[w1_in] = ex.arrived(w1_in, after=last)[w1_in]
    tok2 = ex.start_gather(bufs, groups, ("second_hop",) + ("direct",) * (len(groups) - 1),
                           name="gather_ici_start_rest")
    early = [flat2d(moms["w_in_mix"]), flat2d(vels["w_in_mix"])]
    small_params = dict(g1=ffn1_norm, gm=mix_norm, g2=ffn2_norm, gf=final_norm.reshape(1, D_MODEL),
                        pool_scale=pool_scale, b_alpha=b_alpha, gla_norm=gla_norm, early=early, core=c_idx)
    loss_blk, dx, small = _forward_backward(x[0], loss_target[0], ex, tok[0, 0] + tok2[0, 0], small_params)

    outs = {}

    def update(n, g, after):
        w = weights[n]
        w2 = flat2d(w) if w.ndim > 1 else w.reshape(1, -1)
        go, d, nm, nv = _adamw(w2, g.reshape(w2.shape), moms[n].reshape(w2.shape), vels[n].reshape(w2.shape), after,
                               name=f"adamw_{n}")
        outs[n] = (go.reshape(w.shape), d.reshape(w.shape), nm.reshape(w.shape), nv.reshape(w.shape))
        return nv

    tags = _Exchanges.REDUCE_ORDER
    last = ex.finish_exchange(tags[0], after=dx)
    for prev, tag in zip(tags, tags[1:]):
        last = ex.finish_exchange(tag, after=last)
        for n, g in ex.reduced(prev, after=last).items():
            last = update(n, g, last)

    grads = {}
    small_names = ["ffn1_norm", "mix_norm", "ffn2_norm", "final_norm", "pool_scale", "b_alpha", "gla_norm", "w_alpha",
                   "loss"]
    small = small + [loss_blk[0:1]]
    rows = [a.size // LANES for a in small]
    slab = jnp.concatenate([a.reshape(-1, LANES) for a in small], axis=0)
    pad = -slab.shape[0] % 8
    slab = jnp.pad(slab, ((0, pad), (0, 0)))
    gathered = _all_gather_slab(slab).reshape(N_DEV, slab.shape[0], LANES)
    total = _slab_sum(gathered, name="sum_small_grads")
    off = 0
    for n, a, r in zip(small_names, small, rows):
        grads[n] = total[off:off + r].reshape(a.shape)
        off += r
    grads["w_alpha"] = lax.dynamic_slice_in_dim(grads["w_alpha"], chip * (GLA_DK_TOTAL // N_CHIPS),
                                                GLA_DK_TOTAL // N_CHIPS, axis=1)

    loss = grads.pop("loss")[0, 0]
    for n in small_names[:-1]:
        last = update(n, grads[n], last)
    for n, g in ex.reduced(tags[-1], after=last).items():
        last = update(n, g, last)
    return (loss, dx[None], *[outs[n][0] for n in names], *[outs[n][1] for n in names],
            *[outs[n][2] for n in names], *[outs[n][3] for n in names])


class _Exchanges:
    GATHER_GROUPS = (("ffn1_w_in",), ("ffn1_w_out",), ("w_in_mix", "w_pool", "w_alpha"), ("w_out_mix",),
                     ("ffn2_w_in",), ("ffn2_w_out",))
    REDUCE_ORDER = ("ffn2_out", "ffn2_in", "mix", "ffn1_out", "ffn1_in")

    def __init__(self, sc_idx, c_idx):
        self.sc_idx, self.c_idx = sc_idx, c_idx
        self._gathers, self._swaps, self._sends, self._reduces, self._joins = {}, {}, {}, {}, {}

    def start_gather(self, bufs, groups, kinds, *, name):
        handles, token = _gather_ici_start([[bufs[n] for n in g] for g in groups], kinds, name=name)
        for g, h in zip(groups, handles):
            self._gathers[g[0]] = (g, h)
        return token

    def arrived(self, first, after):
        names, handle = self._gathers.pop(first)
        return dict(zip(names, _gather_ici_wait(handle, after, name=f"gather_ici_wait_{first}_{handle[3]}")))

    def gathered(self, first, after):
        got = self.arrived(first, after)
        return dict(zip(got, _forward_halves(list(got.values()), name=f"gather_forward_{first}")))

    def begin_reduce(self, tag, full):
        g4 = [a.reshape(N_CHIPS, 2, a.shape[1] // 2, a.shape[2]) for a in full.values()]
        lands = [(a.shape[0],) + a.shape[2:] for a in g4]
        handle, token = _split_copy_start(g4, lands, len(g4), _sibling_swap_copies, name=f"swap_start_{tag}")
        self._swaps[tag] = (list(full), handle)
        return token

    def start_reduce(self, tag, after):
        names, handle = self._swaps.pop(tag)
        g4, from_sibling = _split_copy_wait(handle, _sibling_swap_copies, after, name=f"swap_wait_{tag}")
        pair = [_pair_sum(a, b, self.c_idx, name=f"pair_sum_{n}") for n, a, b in zip(names, g4, from_sibling)]
        return self.start_exchange(tag, names, pair)

    def start_exchange(self, tag, names, pair):
        lands = [(N_CHIPS - 1,) + a.shape[1:] for a in pair]
        handle, token = _split_copy_start(pair, lands, len(pair) * (N_CHIPS - 1), _chip_exchange_copies,
                                          name=f"exchange_start_{tag}")
        self._reduces[tag] = (names, handle)
        return token

    def send_to_sibling(self, tag, arrays):
        handle, token = _split_copy_start(arrays, [a.shape for a in arrays], len(arrays), _sibling_send_copies,
                                          name=f"send_start_{tag}")
        self._sends[tag] = handle
        return token

    def from_sibling(self, tag, after):
        return _split_copy_wait(self._sends.pop(tag), _sibling_send_copies, after, name=f"send_wait_{tag}")[1]

    def finish_exchange(self, tag, after):
        names, handle = self._reduces.pop(tag)
        pair, lands = _split_copy_wait(handle, _chip_exchange_copies, after, name=f"exchange_wait_{tag}")
        halves = [_chip_sum(a, b, self.sc_idx, name=f"chip_sum_{n}") for n, a, b in zip(names, pair, lands)]
        handle, token = _join_start(halves, name=f"join_start_{tag}")
        self._joins[tag] = (names, handle)
        return token

    def reduced(self, tag, after):
        names, handle = self._joins.pop(tag)
        return dict(zip(names, _join_wait(handle, after, name=f"join_wait_{tag}")))


def _forward_backward(h0, target, ex, started, sp):
    g1, gm, g2, gf = sp["g1"], sp["gm"], sp["g2"], sp["gf"]
    pool_scale, b_alpha, gla_norm = sp["pool_scale"], sp["b_alpha"], sp["gla_norm"]
    cs_mix = D_IN // N_CHIPS

    n1, n1t = _rms_fwd(h0, g1 + started, sp["early"], name="ffn1_norm")
    w1_in = ex.gathered("ffn1_w_in", after=n1)["ffn1_w_in"]
    fa1, a1 = _ffn_up(n1, w1_in, [], name="ffn1_up")
    w1_out = ex.gathered("ffn1_w_out", after=a1)["ffn1_w_out"].reshape(D_FF, D_MODEL)
    h1 = _mm_nn(a1, w1_out, h0, 0.5, tm=512, tn=D_MODEL, tk=SHARD_TILE, name="ffn1_down")
    n_mix, n_mixt = _rms_fwd(h1, gm, [], name="mix_norm")
    gw = ex.gathered("w_in_mix", after=n_mix)
    w_mix = _concat_shards(gw["w_in_mix"], D_IN_PAD, name="w_mix_concat")[None]
    wp = gw["w_pool"].reshape(N_CHIPS, 4, POOL_GROUP_DIM // N_CHIPS, POOL_GROUP_DIM)
    wp = wp.transpose(1, 0, 2, 3).reshape(4, POOL_GROUP_DIM, POOL_GROUP_DIM)
    wa = gw["w_alpha"].transpose(1, 0, 2).reshape(GLA_GATE_RANK, GLA_DK_TOTAL)
    wa = jnp.pad(wa, ((0, LANES - GLA_GATE_RANK), (0, 0))).astype(BF16)
    u = _mm_nn(n_mix, w_mix[0], None, 1.0, tm=512, tn=D_IN_PAD, tk=D_MODEL, name="mix_in")
    y_pool = _pool_fwd(u, wp, pool_scale, name="pool_fwd")
    cat, o_gla, states = _gla_fwd(u, y_pool, wa, b_alpha, gla_norm, name="gla_fwd")
    w_omix = ex.gathered("w_out_mix", after=cat)["w_out_mix"].reshape(D_MODEL, D_MODEL)
    h2 = _mm_nn(cat, w_omix, h1, 1.0, tm=512, tn=D_MODEL, tk=1024, name="mix_out")
    n3, n3t = _rms_fwd(h2, g2, [], name="ffn2_norm")
    w2_in = ex.gathered("ffn2_w_in", after=n3)["ffn2_w_in"]
    fa2, a2 = _ffn_up(n3, w2_in, [], name="ffn2_up")
    w2_out = ex.gathered("ffn2_w_out", after=a2)["ffn2_w_out"].reshape(D_FF, D_MODEL)
    h3 = _mm_nn(a2, w2_out, h2, 0.5, tm=512, tn=D_MODEL, tk=SHARD_TILE, name="ffn2_down")
    loss_blk, dh3, dh3b, d_gf = _final_loss(h3, gf, target, name="final_loss")

    core = sp["core"]
    tok = ex.begin_reduce("ffn2_out", {"ffn2_w_out": _ffn_dw_out(dh3b, a2, "ffn2")})
    dgu2 = _ffn_bwd_act(dh3b, w2_out, fa2, tok, name="ffn2_bwd_act")
    theirs = _ffn_dw_in_half(n3t, dgu2, 1 - core, None, w2_in.shape, name="ffn2_dw_in_other")
    tok = ex.send_to_sibling("ffn2_in", [theirs])
    dh2, dh2b, d_g2 = _mm_nt_rmsbwd(dgu2, w2_in, h2, dh3, g2 + tok[0, 0], tk=SHARD_TILE, name="ffn2_dx")
    tok = ex.start_reduce("ffn2_out", after=dh2b)
    pair = _ffn_dw_in_half(n3t, dgu2, core, ex.from_sibling("ffn2_in", after=tok)[0], w2_in.shape,
                           name="ffn2_dw_in_own")
    tok = ex.start_exchange("ffn2_in", ["ffn2_w_in"], [pair])
    S = h0.shape[0]
    dcat = _mm_nt(dh2b, w_omix, tok, tm=512, tn=1024, name="mix_out_dx")
    dw_omix = _mm_tn(
        cat, dh2b, grid=(4,),
        a_spec=pl.BlockSpec((S, 512), lambda j: (0, j)), b_spec=pl.BlockSpec((S, D_MODEL), lambda j: (0, 0)),
        out_spec=pl.BlockSpec((512, D_MODEL), lambda j: (j, 0)),
        out_shape=jax.ShapeDtypeStruct((D_MODEL, D_MODEL), BF16), scale=1.0, name="mix_out_dw")
    dp, dw_pool, d_pscale = _pool_bwd(u, dcat, wp, pool_scale, name="pool_bwd")
    du, d_wa, d_ba, d_gn = _gla_bwd(u, o_gla, states, dcat, dp, wa, b_alpha, gla_norm, name="gla_bwd")
    du = du[None]
    tn_mix = COL_TILE
    dw_mix = _mm_tn(
        n_mixt, du, grid=(2, D_IN_PAD // tn_mix), a_is_transposed=True,
        a_spec=pl.BlockSpec((D_MODEL // 2, S), lambda i, j: (i, 0)),
        b_spec=pl.BlockSpec((None, S, tn_mix), lambda i, j: (0, 0, j)),
        out_spec=pl.BlockSpec((D_MODEL // 2, tn_mix), lambda i, j: (i, j)),
        out_shape=jax.ShapeDtypeStruct((D_MODEL, D_IN_PAD), BF16), scale=1.0, name="mix_in_dw")
    dw_mix_s = _split_shards(dw_mix, cs_mix, name="dw_mix_split")
    dw_pool_s = dw_pool.reshape(4, N_CHIPS, POOL_GROUP_DIM // N_CHIPS, POOL_GROUP_DIM).transpose(1, 0, 2, 3)
    dw_pool_s = dw_pool_s.reshape(N_CHIPS, POOL_GROUP_DIM, POOL_GROUP_DIM).astype(BF16)
    tok = ex.begin_reduce("mix", {"w_in_mix": dw_mix_s,
                                  "w_out_mix": dw_omix.reshape(N_CHIPS, D_MODEL // N_CHIPS, D_MODEL),
                                  "w_pool": dw_pool_s})
    dh1, dh1b, d_gm = _mm_nt_rmsbwd(du, w_mix, h1, dh2, gm + tok[0, 0], tk=D_IN_PAD, name="mix_in_dx")
    tok = ex.start_reduce("mix", after=dh1b)
    tok = tok + ex.begin_reduce("ffn1_out", {"ffn1_w_out": _ffn_dw_out(dh1b, a1, "ffn1")})
    dgu1 = _ffn_bwd_act(dh1b, w1_out, fa1, tok, name="ffn1_bwd_act")
    theirs = _ffn_dw_in_half(n1t, dgu1, 1 - core, None, w1_in.shape, name="ffn1_dw_in_other")
    tok = ex.send_to_sibling("ffn1_in", [theirs])
    tok = ex.start_reduce("ffn1_out", after=tok)
    pair = _ffn_dw_in_half(n1t, dgu1, core, ex.from_sibling("ffn1_in", after=tok)[0], w1_in.shape,
                           name="ffn1_dw_in_own")
    tok = ex.start_exchange("ffn1_in", ["ffn1_w_in"], [pair])
    dx, _, d_g1 = _mm_nt_rmsbwd(dgu1, w1_in, h0, dh1, g1 + tok[0, 0], tk=SHARD_TILE, name="ffn1_dx")
    small = [d_g1, d_gm, d_g2, d_gf, d_pscale, d_ba, d_gn, d_wa[:GLA_GATE_RANK]]
    return loss_blk, dx, small
```

```python
import functools

import jax
import jax.numpy as jnp
from jax import lax
from jax.experimental import pallas as pl
from jax.experimental.pallas import tpu as pltpu

F32 = jnp.float32
BF16 = jnp.bfloat16
MESH = pl.DeviceIdType.MESH

D_MODEL = 2048
D_FF = 5632
D_POOL = 1024
POOL_WINDOWS = (2, 4, 8, 16)
POOL_GROUP_DIM = 256
D_GLA = 1024
GLA_HEADS = 4
GLA_DV = 256
GLA_DK = 128
GLA_DK_TOTAL = 512
GLA_GATE_RANK = 16
GATE_LOGIT_NORMALIZER = 16.0
CHUNK = 64
D_IN = 4112
D_IN_PAD = 4224
EPS = 1e-6

ADAM_LR = 0.001
ADAM_B1 = 0.9
ADAM_B2 = 0.999
ADAM_EPS = 1e-08
ADAM_WD = 0.01
ADAM_STEP = 10

N_CHIPS = 4
N_DEV = 8
V7X_VMEM_BYTES = 64 * 1024 * 1024
LANES = 128
MXU_TILE = 256
COL_TILE = 1408
SHARD_TILE = 2816


def _cparams(semantics, vmem_mb):
    assert vmem_mb * 1024 * 1024 < V7X_VMEM_BYTES
    return pltpu.CompilerParams(dimension_semantics=semantics, vmem_limit_bytes=vmem_mb * 1024 * 1024)


def _dot_nn(a, b):
    return jnp.dot(a, b, preferred_element_type=F32)


def _dot_nt(a, b):
    return lax.dot_general(a, b, (((1,), (1,)), ((), ())), preferred_element_type=F32)


def _dot_tn(a, b):
    return lax.dot_general(a, b, (((0,), (0,)), ((), ())), preferred_element_type=F32)


def _sigmoid(x):
    return 1.0 / (1.0 + jnp.exp(-x))


def _rms_fwd(x, g, early, *, name):
    S, D = x.shape
    tm = 256

    def body(x_ref, g_ref, *rest):
        o_ref, ot_ref = rest[len(early):]
        xv = x_ref[...]
        r = lax.rsqrt(jnp.mean(xv * xv, axis=-1, keepdims=True) + EPS)
        n = xv * r * g_ref[...]
        o_ref[...] = n.astype(BF16)
        ot_ref[...] = n.T.astype(BF16)

    return pl.pallas_call(
        body, name=name, grid=(S // tm,),
        in_specs=[pl.BlockSpec((tm, D), lambda i: (i, 0)), pl.BlockSpec((1, D), lambda i: (0, 0))]
        + [pl.BlockSpec(memory_space=pl.ANY)] * len(early),
        out_specs=[pl.BlockSpec((tm, D), lambda i: (i, 0)), pl.BlockSpec((D, tm), lambda i: (0, i))],
        out_shape=[jax.ShapeDtypeStruct((S, D), BF16), jax.ShapeDtypeStruct((D, S), BF16)],
        compiler_params=_cparams(("parallel",), 32),
    )(x, g, *early)


def _ffn_up(n, w_in, early, *, name):
    S, D = n.shape
    ns, _, cs = w_in.shape
    half = ns // 2
    F = cs * half
    tm, tn = 128, SHARD_TILE
    nb = cs // tn

    def body(n_ref, wg_ref, wu_ref, *rest):
        fa_ref, a_ref = rest[len(early):]
        nv = n_ref[...]
        g = _dot_nn(nv, wg_ref[...])
        u = _dot_nn(nv, wu_ref[...])
        s = _sigmoid(g)
        silu = g * s
        fa_ref[0] = (u * (s * (1.0 + g * (1.0 - s)))).astype(BF16)
        fa_ref[1] = silu.astype(BF16)
        a_ref[...] = (silu * u).astype(BF16)

    return pl.pallas_call(
        body, name=name, grid=(F // tn, S // tm),
        in_specs=[
            pl.BlockSpec((tm, D), lambda j, i: (i, 0)),
            pl.BlockSpec((None, D, tn), lambda j, i: (lax.div(j, nb), 0, lax.rem(j, nb))),
            pl.BlockSpec((None, D, tn), lambda j, i: (half + lax.div(j, nb), 0, lax.rem(j, nb))),
        ] + [pl.BlockSpec(memory_space=pl.ANY)] * len(early),
        out_specs=[
            pl.BlockSpec((2, tm, tn), lambda j, i: (0, i, j)),
            pl.BlockSpec((tm, tn), lambda j, i: (i, j)),
        ],
        out_shape=[jax.ShapeDtypeStruct((2, S, F), BF16), jax.ShapeDtypeStruct((S, F), BF16)],
        compiler_params=_cparams(("parallel", "parallel"), 56),
    )(n, w_in, w_in, *early)


def _mm_nn(a, b, resid, scale, *, tm, tn, tk, name):
    S, K = a.shape
    N = b.shape[1]
    nk = K // tk

    def body(*refs):
        if resid is None:
            a_ref, b_ref, o_ref, acc_ref = refs
            r_ref = None
        else:
            a_ref, b_ref, r_ref, o_ref, acc_ref = refs
        k = pl.program_id(2)

        @pl.when(k == 0)
        def _():
            acc_ref[...] = jnp.zeros_like(acc_ref)

        acc_ref[...] += _dot_nn(a_ref[...], b_ref[...])

        @pl.when(k == nk - 1)
        def _():
            out = acc_ref[...] * scale
            if r_ref is not None:
                out = r_ref[...] + out
            o_ref[...] = out

    in_specs = [pl.BlockSpec((tm, tk), lambda i, j, k: (i, k)), pl.BlockSpec((tk, tn), lambda i, j, k: (k, j))]
    args = [a, b]
    if resid is not None:
        in_specs.append(pl.BlockSpec((tm, tn), lambda i, j, k: (i, j)))
        args.append(resid)
    return pl.pallas_call(
        body, name=name, grid=(S // tm, N // tn, nk),
        in_specs=in_specs,
        out_specs=pl.BlockSpec((tm, tn), lambda i, j, k: (i, j)),
        out_shape=jax.ShapeDtypeStruct((S, N), F32),
        scratch_shapes=[pltpu.VMEM((tm, tn), F32)],
        compiler_params=_cparams(("parallel", "parallel", "arbitrary"), 48),
    )(*args)


def _mm_nt(a, b, after, *, tm, tn, name):
    S, K = a.shape
    N = b.shape[0]

    def body(a_ref, b_ref, after_ref, o_ref):
        o_ref[...] = _dot_nt(a_ref[...], b_ref[...])

    return pl.pallas_call(
        body, name=name, grid=(N // tn, S // tm),
        in_specs=[pl.BlockSpec((tm, K), lambda j, i: (i, 0)), pl.BlockSpec((tn, K), lambda j, i: (j, 0)),
                  pl.BlockSpec(memory_space=pl.ANY)],
        out_specs=pl.BlockSpec((tm, tn), lambda j, i: (i, j)),
        out_shape=jax.ShapeDtypeStruct((S, N), F32),
        compiler_params=_cparams(("parallel", "parallel"), 48),
    )(a, b, after)


def _mm_tn(a, b, *, grid, a_spec, b_spec, out_spec, out_shape, scale, name, a_is_transposed=False):
    dot = _dot_nn if a_is_transposed else _dot_tn

    def body(a_ref, b_ref, o_ref):
        o_ref[...] = (scale * dot(a_ref[...], b_ref[...])).astype(o_ref.dtype)

    return pl.pallas_call(
        body, name=name, grid=grid, in_specs=[a_spec, b_spec], out_specs=out_spec, out_shape=out_shape,
        compiler_params=_cparams(("parallel",) * len(grid), 56),
    )(a, b)


def _ffn_bwd_act(dhb, w_out, fa, after, *, name):
    S, D = dhb.shape
    F = w_out.shape[0]
    tm, tn = 256, SHARD_TILE

    def body(dh_ref, w_ref, fa_ref, after_ref, dgu_ref):
        da = 0.5 * _dot_nt(dh_ref[...], w_ref[...])
        dgu_ref[0] = (da * fa_ref[0].astype(F32)).astype(BF16)
        dgu_ref[1] = (da * fa_ref[1].astype(F32)).astype(BF16)

    return pl.pallas_call(
        body, name=name, grid=(F // tn, S // tm),
        in_specs=[
            pl.BlockSpec((tm, D), lambda j, i: (i, 0)),
            pl.BlockSpec((tn, D), lambda j, i: (j, 0)),
            pl.BlockSpec((2, tm, tn), lambda j, i: (0, i, j)),
            pl.BlockSpec(memory_space=pl.ANY),
        ],
        out_specs=pl.BlockSpec((2, tm, tn), lambda j, i: (0, i, j)),
        out_shape=jax.ShapeDtypeStruct((2, S, F), BF16),
        compiler_params=_cparams(("parallel", "parallel"), 56),
    )(dhb, w_out, fa, after)


def _mm_nt_rmsbwd(dact, w, h_in, dh_out, g, *, tk, name):
    ng, S, fg = dact.shape
    ns, D, cs = w.shape
    assert ng * fg == ns * cs
    tm, rc = 512, 64
    kpg, kps = fg // tk, cs // tk
    nk = ng * kpg

    def body(a_ref, w_ref, h_ref, dho_ref, g_ref, dh_ref, dhb_ref, dg_ref, acc_ref):
        i = pl.program_id(0)
        k = pl.program_id(1)

        @pl.when(k == 0)
        def _():
            acc_ref[...] = jnp.zeros_like(acc_ref)

        acc_ref[...] += _dot_nt(a_ref[...], w_ref[...])

        @pl.when(jnp.logical_and(i == 0, k == 0))
        def _():
            dg_ref[...] = jnp.zeros_like(dg_ref)

        @pl.when(k == nk - 1)
        def _():
            gv = g_ref[...]

            def rows_step(c, dg):
                rows = pl.ds(pl.multiple_of(c * rc, rc), rc)
                dn = acc_ref[rows, :]
                xv = h_ref[rows, :]
                r = lax.rsqrt(jnp.mean(xv * xv, axis=-1, keepdims=True) + EPS)
                xh = xv * r
                dng = dn * gv
                dx = r * (dng - xh * jnp.mean(dng * xh, axis=-1, keepdims=True))
                out = dho_ref[rows, :] + dx
                dh_ref[rows, :] = out
                dhb_ref[rows, :] = out.astype(BF16)
                return dg + jnp.sum(dn * xh, axis=0, keepdims=True)

            dg_ref[...] += lax.fori_loop(0, tm // rc, rows_step, jnp.zeros((1, D), F32))

    return pl.pallas_call(
        body, name=name, grid=(S // tm, nk),
        in_specs=[
            pl.BlockSpec((None, tm, tk), lambda i, k: (lax.div(k, kpg), i, lax.rem(k, kpg))),
            pl.BlockSpec((None, D, tk), lambda i, k: (lax.div(k, kps), 0, lax.rem(k, kps))),
            pl.BlockSpec((tm, D), lambda i, k: (i, 0)),
            pl.BlockSpec((tm, D), lambda i, k: (i, 0)),
            pl.BlockSpec((1, D), lambda i, k: (0, 0)),
        ],
        out_specs=[
            pl.BlockSpec((tm, D), lambda i, k: (i, 0), pipeline_mode=pl.Buffered(1)),
            pl.BlockSpec((tm, D), lambda i, k: (i, 0), pipeline_mode=pl.Buffered(1)),
            pl.BlockSpec((1, D), lambda i, k: (0, 0)),
        ],
        out_shape=[jax.ShapeDtypeStruct((S, D), F32), jax.ShapeDtypeStruct((S, D), BF16),
                   jax.ShapeDtypeStruct((1, D), F32)],
        scratch_shapes=[pltpu.VMEM((tm, D), F32)],
        compiler_params=_cparams(("arbitrary", "arbitrary"), 56),
    )(dact, w, h_in, dh_out, g)


def _final_loss(h, g, target, *, name):
    S, D = h.shape
    tm = 256

    def body(h_ref, g_ref, t_ref, loss_ref, dh_ref, dhb_ref, dg_ref):
        i = pl.program_id(0)

        @pl.when(i == 0)
        def _():
            loss_ref[...] = jnp.zeros_like(loss_ref)
            dg_ref[...] = jnp.zeros_like(dg_ref)

        xv = h_ref[...]
        gv = g_ref[...]
        r = lax.rsqrt(jnp.mean(xv * xv, axis=-1, keepdims=True) + EPS)
        xh = xv * r
        e = xh * gv - t_ref[...]
        loss_ref[...] += 0.5 * jnp.sum(jnp.mean(e * e, axis=-1, keepdims=True))
        dy = e * (1.0 / D)
        dg_ref[...] += jnp.sum(dy * xh, axis=0, keepdims=True)
        dyg = dy * gv
        dx = r * (dyg - xh * jnp.mean(dyg * xh, axis=-1, keepdims=True))
        dh_ref[...] = dx
        dhb_ref[...] = dx.astype(BF16)

    return pl.pallas_call(
        body, name=name, grid=(S // tm,),
        in_specs=[pl.BlockSpec((tm, D), lambda i: (i, 0)), pl.BlockSpec((1, D), lambda i: (0, 0)),
                  pl.BlockSpec((tm, D), lambda i: (i, 0))],
        out_specs=[pl.BlockSpec((8, LANES), lambda i: (0, 0)), pl.BlockSpec((tm, D), lambda i: (i, 0)),
                   pl.BlockSpec((tm, D), lambda i: (i, 0)), pl.BlockSpec((1, D), lambda i: (0, 0))],
        out_shape=[jax.ShapeDtypeStruct((8, LANES), F32), jax.ShapeDtypeStruct((S, D), F32),
                   jax.ShapeDtypeStruct((S, D), BF16), jax.ShapeDtypeStruct((1, D), F32)],
        compiler_params=_cparams(("arbitrary",), 40),
    )(h, g, target)


POOL_HALO = 16
POOL_ROWS = 256


def _pool_window_mean_minus_token(ext, tok0, w):
    s = ext
    k = 1
    while k < w:
        s = s + pltpu.roll(s, k, 0)
        k *= 2
    win = s[POOL_HALO:, :]
    tok = tok0 + lax.broadcasted_iota(jnp.int32, (POOL_ROWS, 1), 0)
    cnt = jnp.minimum(tok + 1, w).astype(F32)
    return win / cnt - ext[POOL_HALO:, :], cnt


def _pool_fwd(u, w_pool, scale, *, name):
    S = u.shape[0]
    C = POOL_GROUP_DIM
    nsteps = S // POOL_ROWS

    def body(p_ref, w_ref, sc_ref, y_ref, xp_ref):
        xp_ref[0:POOL_HALO, :] = jnp.zeros((POOL_HALO, D_POOL), F32)
        xp_ref[POOL_HALO:, :] = p_ref[...]
        for gi, win in enumerate(POOL_WINDOWS):
            cols = slice(gi * C, (gi + 1) * C)

            def step(c, carry, cols=cols, win=win, gi=gi):
                r0 = pl.multiple_of(c * POOL_ROWS, POOL_ROWS)
                ext = xp_ref[pl.ds(r0, POOL_ROWS + POOL_HALO), cols]
                pooled, _ = _pool_window_mean_minus_token(ext, r0, win)
                y = _dot_nn(pooled.astype(BF16), w_ref[gi]) * sc_ref[:, cols]
                y_ref[pl.ds(r0, POOL_ROWS), cols] = y.astype(BF16)
                return carry

            lax.fori_loop(0, nsteps, step, 0)

    return pl.pallas_call(
        body, name=name, grid=(1,),
        in_specs=[pl.BlockSpec((S, D_POOL), lambda i: (0, 0)),
                  pl.BlockSpec((4, C, C), lambda i: (0, 0, 0)),
                  pl.BlockSpec((1, D_POOL), lambda i: (0, 0))],
        out_specs=pl.BlockSpec((S, D_POOL), lambda i: (0, 0)),
        out_shape=jax.ShapeDtypeStruct((S, D_POOL), BF16),
        scratch_shapes=[pltpu.VMEM((S + POOL_HALO, D_POOL), F32)],
        compiler_params=_cparams(("arbitrary",), 48),
    )(u, w_pool, scale)


def _pool_bwd(u, dcat, w_pool, scale, *, name):
    S = u.shape[0]
    C = POOL_GROUP_DIM
    nsteps = S // POOL_ROWS

    def body(p_ref, dy_ref, w_ref, sc_ref, dp_ref, dw_ref, dsc_ref, xp_ref, e_ref, neg_ref):
        xp_ref[0:POOL_HALO, :] = jnp.zeros((POOL_HALO, D_POOL), F32)
        xp_ref[POOL_HALO:, :] = p_ref[...]
        e_ref[S:, :] = jnp.zeros((POOL_HALO, C), F32)
        for gi, win in enumerate(POOL_WINDOWS):
            cols = slice(gi * C, (gi + 1) * C)

            def step_a(c, carry, cols=cols, win=win, gi=gi):
                dw, dsc = carry
                r0 = pl.multiple_of(c * POOL_ROWS, POOL_ROWS)
                ext = xp_ref[pl.ds(r0, POOL_ROWS + POOL_HALO), cols]
                pooled, cnt = _pool_window_mean_minus_token(ext, r0, win)
                pb = pooled.astype(BF16)
                wv = w_ref[gi]
                dy = dy_ref[pl.ds(r0, POOL_ROWS), cols]
                dsc = dsc + jnp.sum(dy * _dot_nn(pb, wv), axis=0, keepdims=True)
                dyp = (dy * sc_ref[:, cols]).astype(BF16)
                dw = dw + _dot_tn(pb, dyp)
                dpooled = _dot_nt(dyp, wv)
                e_ref[pl.ds(r0, POOL_ROWS), :] = dpooled / cnt
                neg_ref[pl.ds(r0, POOL_ROWS), :] = -dpooled
                return dw, dsc

            dw, dsc = lax.fori_loop(0, nsteps, step_a, (jnp.zeros((C, C), F32), jnp.zeros((1, C), F32)))
            dw_ref[gi] = dw
            dsc_ref[:, cols] = dsc

            def step_b(c, carry, cols=cols, win=win):
                r0 = pl.multiple_of(c * POOL_ROWS, POOL_ROWS)
                s = e_ref[pl.ds(r0, POOL_ROWS + POOL_HALO), :]
                n = POOL_ROWS + POOL_HALO
                k = 1
                while k < win:
                    s = s + pltpu.roll(s, n - k, 0)
                    k *= 2
                du = s[:POOL_ROWS, :] + neg_ref[pl.ds(r0, POOL_ROWS), :]
                dp_ref[pl.ds(r0, POOL_ROWS), cols] = du.astype(BF16)
                return carry

            lax.fori_loop(0, nsteps, step_b, 0)

    return pl.pallas_call(
        body, name=name, grid=(1,),
        in_specs=[pl.BlockSpec((S, D_POOL), lambda i: (0, 0)),
                  pl.BlockSpec((S, D_POOL), lambda i: (0, 0)),
                  pl.BlockSpec((4, C, C), lambda i: (0, 0, 0)),
                  pl.BlockSpec((1, D_POOL), lambda i: (0, 0))],
        out_specs=[pl.BlockSpec((S, D_POOL), lambda i: (0, 0)),
                   pl.BlockSpec((4, C, C), lambda i: (0, 0, 0)),
                   pl.BlockSpec((1, D_POOL), lambda i: (0, 0))],
        out_shape=[jax.ShapeDtypeStruct((S, D_POOL), BF16), jax.ShapeDtypeStruct((4, C, C), F32),
                   jax.ShapeDtypeStruct((1, D_POOL), F32)],
        scratch_shapes=[pltpu.VMEM((S + POOL_HALO, D_POOL), F32), pltpu.VMEM((S + POOL_HALO, C), F32),
                        pltpu.VMEM((S, C), F32)],
        compiler_params=_cparams(("arbitrary",), 56),
    )(u, dcat, w_pool, scale)


GLA_ROWS = 256
U_Q_BLK, U_K_BLK = 2, 3
U_V_BLK, U_G_BLK = 2, 3
U_R_BLK = 32


def _prefix_sum_rows(x):
    n = x.shape[0]
    row = lax.broadcasted_iota(jnp.int32, x.shape, 0)
    k = 1
    while k < n:
        x = x + jnp.where(row >= k, pltpu.roll(x, k, 0), 0.0)
        k *= 2
    return x


def _suffix_sum_rows(x):
    n = x.shape[0]
    row = lax.broadcasted_iota(jnp.int32, x.shape, 0)
    k = 1
    while k < n:
        x = x + jnp.where(row < n - k, pltpu.roll(x, n - k, 0), 0.0)
        k *= 2
    return x


def _log_sigmoid(z):
    return jnp.minimum(z, 0.0) - jnp.log(1.0 + jnp.exp(-jnp.abs(z)))


def _gla_chunk_terms(la_c, q_c, k_c):
    bc = _prefix_sum_rows(la_c)
    bl = jnp.sum(la_c, axis=0, keepdims=True)
    eb = jnp.exp(bc)
    enb = jnp.exp(-bc)
    etail = jnp.exp(bl - bc)
    qd = q_c * (GLA_DK ** -0.5) * eb
    ki = k_c * enb
    kt = k_c * etail
    d = jnp.exp(bl)
    return eb, enb, etail, qd, ki, kt, d


def _gla_fwd(u, y_pool, w_alpha, b_alpha, gnorm, *, name):
    S = u.shape[0]
    RB = GLA_ROWS
    ncc = RB // CHUNK
    H, DK, DV = GLA_HEADS, GLA_DK, GLA_DV

    def body(q_ref, k_ref, v_ref, go_ref, r_ref, yp_ref, wa_ref, ba_ref, gn_ref, cat_ref, o_ref, st_ref, state):
        i = pl.program_id(0)

        @pl.when(i == 0)
        def _():
            state[...] = jnp.zeros_like(state)

        cat_ref[:, :D_POOL] = yp_ref[...]
        y_ref = cat_ref.at[:, D_POOL:]

        z = _dot_nn(r_ref[...].astype(BF16), wa_ref[...]) + ba_ref[...]
        la = _log_sigmoid(z) / GATE_LOGIT_NORMALIZER
        ri = lax.broadcasted_iota(jnp.int32, (CHUNK, CHUNK), 0)
        ci = lax.broadcasted_iota(jnp.int32, (CHUNK, CHUNK), 1)
        tri = ri >= ci
        gn = gn_ref[...]
        for cc in range(ncc):
            rs = slice(cc * CHUNK, (cc + 1) * CHUNK)
            for h in range(H):
                ks = slice(h * DK, (h + 1) * DK)
                vs = slice(h * DV, (h + 1) * DV)
                _, _, _, qd, ki, kt, d = _gla_chunk_terms(la[rs, ks], q_ref[rs, ks], k_ref[rs, ks])
                qdb = qd.astype(BF16)
                vb = v_ref[rs, vs].astype(BF16)
                p = jnp.where(tri, _dot_nt(qdb, ki.astype(BF16)), 0.0)
                st = state[h]
                st_ref[cc, h] = st
                o = _dot_nn(p.astype(BF16), vb) + _dot_nt(qdb, st.astype(BF16))
                state[h] = st * d + _dot_tn(vb, kt.astype(BF16))
                o_ref[rs, vs] = o
                rinv = lax.rsqrt(jnp.mean(o * o, axis=-1, keepdims=True) + EPS)
                go = go_ref[rs, vs]
                y_ref[rs, vs] = (o * rinv * gn * (go * _sigmoid(go))).astype(BF16)

    nblk = S // RB
    return pl.pallas_call(
        body, name=name, grid=(nblk,),
        in_specs=[
            pl.BlockSpec((RB, GLA_DK_TOTAL), lambda i: (i, U_Q_BLK)),
            pl.BlockSpec((RB, GLA_DK_TOTAL), lambda i: (i, U_K_BLK)),
            pl.BlockSpec((RB, D_GLA), lambda i: (i, U_V_BLK)),
            pl.BlockSpec((RB, D_GLA), lambda i: (i, U_G_BLK)),
            pl.BlockSpec((RB, LANES), lambda i: (i, U_R_BLK)),
            pl.BlockSpec((RB, D_POOL), lambda i: (i, 0)),
            pl.BlockSpec((LANES, GLA_DK_TOTAL), lambda i: (0, 0)),
            pl.BlockSpec((1, GLA_DK_TOTAL), lambda i: (0, 0)),
            pl.BlockSpec((1, DV), lambda i: (0, 0)),
        ],
        out_specs=[
            pl.BlockSpec((RB, D_POOL + D_GLA), lambda i: (i, 0)),
            pl.BlockSpec((RB, D_GLA), lambda i: (i, 0)),
            pl.BlockSpec((ncc, H, DV, DK), lambda i: (i, 0, 0, 0)),
        ],
        out_shape=[jax.ShapeDtypeStruct((S, D_POOL + D_GLA), BF16), jax.ShapeDtypeStruct((S, D_GLA), F32),
                   jax.ShapeDtypeStruct((S // CHUNK, H, DV, DK), F32)],
        scratch_shapes=[pltpu.VMEM((H, DV, DK), F32)],
        compiler_params=_cparams(("arbitrary",), 32),
    )(u, u, u, u, u, y_pool, w_alpha, b_alpha, gnorm)


def _gla_bwd(u, o, states, dcat, dpool, w_alpha, b_alpha, gnorm, *, name):
    S = u.shape[0]
    RB = GLA_ROWS
    ncc = RB // CHUNK
    H, DK, DV = GLA_HEADS, GLA_DK, GLA_DV
    nblk = S // RB
    o_q, o_k = D_POOL, D_POOL + GLA_DK_TOTAL
    o_v, o_g, o_r = o_k + GLA_DK_TOTAL, o_k + GLA_DK_TOTAL + D_GLA, o_k + GLA_DK_TOTAL + 2 * D_GLA

    def body(q_ref, k_ref, v_ref, go_ref, r_ref, o_ref, st_ref, dy_ref, dpool_ref, wa_ref, ba_ref, gn_ref,
             du_ref, dwa_ref, dba_ref, dgn_ref, dstate, dz_ref):
        i = pl.program_id(0)

        @pl.when(i == 0)
        def _():
            dstate[...] = jnp.zeros_like(dstate)
            dwa_ref[...] = jnp.zeros_like(dwa_ref)
            dba_ref[...] = jnp.zeros_like(dba_ref)
            dgn_ref[...] = jnp.zeros_like(dgn_ref)

        du_ref[:, :o_q] = dpool_ref[...]
        dq_ref, dk_ref = du_ref.at[:, o_q:o_k], du_ref.at[:, o_k:o_v]
        dv_ref, dgo_ref, dr_ref = du_ref.at[:, o_v:o_g], du_ref.at[:, o_g:o_r], du_ref.at[:, o_r:]

        rb = r_ref[...].astype(BF16)
        wa = wa_ref[...]
        z = _dot_nn(rb, wa) + ba_ref[...]
        la = _log_sigmoid(z) / GATE_LOGIT_NORMALIZER
        ri = lax.broadcasted_iota(jnp.int32, (CHUNK, CHUNK), 0)
        ci = lax.broadcasted_iota(jnp.int32, (CHUNK, CHUNK), 1)
        tri = ri >= ci
        last_row = lax.broadcasted_iota(jnp.int32, (CHUNK, DK), 0) == CHUNK - 1
        gn = gn_ref[...]
        dgn = jnp.zeros((1, DV), F32)
        for cc in reversed(range(ncc)):
            rs = slice(cc * CHUNK, (cc + 1) * CHUNK)
            for h in range(H):
                ks = slice(h * DK, (h + 1) * DK)
                vs = slice(h * DV, (h + 1) * DV)
                eb, enb, etail, qd, ki, kt, d = _gla_chunk_terms(la[rs, ks], q_ref[rs, ks], k_ref[rs, ks])
                qdb, kib, ktb = qd.astype(BF16), ki.astype(BF16), kt.astype(BF16)
                vb = v_ref[rs, vs].astype(BF16)
                p = jnp.where(tri, _dot_nt(qdb, kib), 0.0)
                ov = o_ref[rs, vs]
                go = go_ref[rs, vs]
                dy = dy_ref[rs, vs]
                rinv = lax.rsqrt(jnp.mean(ov * ov, axis=-1, keepdims=True) + EPS)
                oh = ov * rinv
                sg = _sigmoid(go)
                dgo_ref[rs, vs] = (dy * (oh * gn) * (sg * (1.0 + go * (1.0 - sg)))).astype(BF16)
                don = dy * (go * sg)
                dgn = dgn + jnp.sum(don * oh, axis=0, keepdims=True)
                doh = don * gn
                do = rinv * (doh - oh * jnp.mean(doh * oh, axis=-1, keepdims=True))
                dob = do.astype(BF16)
                st = st_ref[cc, h]
                dst = dstate[h]
                stb, dstb = st.astype(BF16), dst.astype(BF16)
                dp = jnp.where(tri, _dot_nt(dob, vb), 0.0).astype(BF16)
                dv_ref[rs, vs] = (_dot_tn(p.astype(BF16), dob) + _dot_nt(ktb, dstb)).astype(BF16)
                dqd = _dot_nn(dp, kib) + _dot_nn(dob, stb)
                dki = _dot_tn(dp, qdb)
                dkt = _dot_nn(vb, dstb)
                dd = jnp.sum(dst * st, axis=0, keepdims=True)
                dstate[h] = dst * d + _dot_tn(dob, qdb)
                dq_ref[rs, ks] = (dqd * eb * (DK ** -0.5)).astype(BF16)
                dk_ref[rs, ks] = (dki * enb + dkt * etail).astype(BF16)
                dbl = jnp.sum(dkt * kt, axis=0, keepdims=True) + dd * d
                dbc = dqd * qd - dki * ki - dkt * kt
                dbc = dbc + jnp.where(last_row, dbl, 0.0)
                dla = _suffix_sum_rows(dbc)
                dz_ref[rs, ks] = dla * (1.0 / GATE_LOGIT_NORMALIZER) * (1.0 - _sigmoid(z[rs, ks]))
        dz = dz_ref[...]
        dzb = dz.astype(BF16)
        dr_ref[...] = _dot_nt(dzb, wa).astype(BF16)
        dwa_ref[...] += _dot_tn(rb, dzb)
        dba_ref[...] += jnp.sum(dz, axis=0, keepdims=True)
        dgn_ref[...] += dgn

    def rev(blk):
        return lambda i: (nblk - 1 - i, blk)

    return pl.pallas_call(
        body, name=name, grid=(nblk,),
        in_specs=[
            pl.BlockSpec((RB, GLA_DK_TOTAL), rev(U_Q_BLK)),
            pl.BlockSpec((RB, GLA_DK_TOTAL), rev(U_K_BLK)),
            pl.BlockSpec((RB, D_GLA), rev(U_V_BLK)),
            pl.BlockSpec((RB, D_GLA), rev(U_G_BLK)),
            pl.BlockSpec((RB, LANES), rev(U_R_BLK)),
            pl.BlockSpec((RB, D_GLA), rev(0)),
            pl.BlockSpec((ncc, H, DV, DK), lambda i: (nblk - 1 - i, 0, 0, 0)),
            pl.BlockSpec((RB, D_GLA), rev(1)),
            pl.BlockSpec((RB, D_POOL), rev(0)),
            pl.BlockSpec((LANES, GLA_DK_TOTAL), lambda i: (0, 0)),
            pl.BlockSpec((1, GLA_DK_TOTAL), lambda i: (0, 0)),
            pl.BlockSpec((1, DV), lambda i: (0, 0)),
        ],
        out_specs=[
            pl.BlockSpec((RB, D_IN_PAD), rev(0)),
            pl.BlockSpec((LANES, GLA_DK_TOTAL), lambda i: (0, 0)),
            pl.BlockSpec((1, GLA_DK_TOTAL), lambda i: (0, 0)),
            pl.BlockSpec((1, DV), lambda i: (0, 0)),
        ],
        out_shape=[
            jax.ShapeDtypeStruct((S, D_IN_PAD), BF16),
            jax.ShapeDtypeStruct((LANES, GLA_DK_TOTAL), F32), jax.ShapeDtypeStruct((1, GLA_DK_TOTAL), F32),
            jax.ShapeDtypeStruct((1, DV), F32),
        ],
        scratch_shapes=[pltpu.VMEM((H, DV, DK), F32), pltpu.VMEM((RB, GLA_DK_TOTAL), F32)],
        compiler_params=_cparams(("arbitrary",), 32),
    )(u, u, u, u, u, o, states, dcat, dpool, w_alpha, b_alpha, gnorm)


def _concat_shards(w4, width, *, name):
    ns, R, cs = w4.shape
    tr = 256

    def body(w_ref, o_ref):
        for t in range(ns):
            o_ref[:, t * cs:(t + 1) * cs] = w_ref[t]
        o_ref[:, ns * cs:] = jnp.zeros((tr, width - ns * cs), o_ref.dtype)

    return pl.pallas_call(
        body, name=name, grid=(R // tr,),
        in_specs=[pl.BlockSpec((ns, tr, cs), lambda i: (0, i, 0))],
        out_specs=pl.BlockSpec((tr, width), lambda i: (i, 0)),
        out_shape=jax.ShapeDtypeStruct((R, width), w4.dtype),
        compiler_params=_cparams(("parallel",), 32),
    )(w4)


def _split_shards(a, cs, *, name):
    R, width = a.shape
    tr = 256

    def body(a_ref, o_ref):
        for t in range(N_CHIPS):
            o_ref[t] = a_ref[:, t * cs:(t + 1) * cs]

    return pl.pallas_call(
        body, name=name, grid=(R // tr,),
        in_specs=[pl.BlockSpec((tr, width), lambda i: (i, 0))],
        out_specs=pl.BlockSpec((N_CHIPS, tr, cs), lambda i: (0, i, 0)),
        out_shape=jax.ShapeDtypeStruct((N_CHIPS, R, cs), a.dtype),
        compiler_params=_cparams(("parallel",), 32),
    )(a)


def _row_tile(rows, cols, itemsize, budget=2 * 1024 * 1024):
    if rows * cols * itemsize <= budget or rows % 16:
        return rows
    best = 16
    for t in range(16, rows + 1, 16):
        if rows % t == 0 and t * cols * itemsize <= budget:
            best = t
    return best


def _adamw(w, g, m, v, after, *, name):
    R, C = w.shape
    tr = _row_tile(R, C, 4)

    def body(w_ref, g_ref, m_ref, v_ref, after_ref, go_ref, d_ref, nm_ref, nv_ref):
        gv = g_ref[...]
        go_ref[...] = gv
        mn = ADAM_B1 * m_ref[...] + (1.0 - ADAM_B1) * gv
        vn = ADAM_B2 * v_ref[...] + (1.0 - ADAM_B2) * jnp.square(gv)
        m_hat = mn / (1.0 - ADAM_B1 ** ADAM_STEP)
        v_hat = vn / (1.0 - ADAM_B2 ** ADAM_STEP)
        d_ref[...] = -ADAM_LR * (m_hat / (jnp.sqrt(v_hat) + ADAM_EPS) + ADAM_WD * w_ref[...])
        nm_ref[...] = mn
        nv_ref[...] = vn

    spec = pl.BlockSpec((tr, C), lambda i: (i, 0))
    shp = jax.ShapeDtypeStruct((R, C), F32)
    return pl.pallas_call(
        body, name=name, grid=(R // tr,), in_specs=[spec] * 4 + [pl.BlockSpec(memory_space=pl.ANY)],
        out_specs=[spec] * 4, out_shape=[shp] * 4,
        compiler_params=_cparams(("parallel",), 48),
    )(w, g, m, v, after)


def _pair_sum(g4, recv, c_idx, *, name):
    ns, _, R2, C = g4.shape
    tr = _row_tile(R2, C, 2)

    def body(c_ref, g_ref, r_ref, o_ref):
        o_ref[...] = (g_ref[...].astype(F32) + r_ref[...].astype(F32)).astype(BF16)

    return pl.pallas_call(
        body, name=name,
        grid_spec=pltpu.PrefetchScalarGridSpec(
            num_scalar_prefetch=1, grid=(ns, R2 // tr),
            in_specs=[pl.BlockSpec((None, None, tr, C), lambda s, i, c: (s, c[0], i, 0)),
                      pl.BlockSpec((None, tr, C), lambda s, i, c: (s, i, 0))],
            out_specs=pl.BlockSpec((None, tr, C), lambda s, i, c: (s, i, 0)),
        ),
        out_shape=jax.ShapeDtypeStruct((ns, R2, C), BF16),
        compiler_params=_cparams(("parallel", "parallel"), 32),
    )(c_idx, g4, recv)


def _chip_sum(part, recv, sc_idx, *, name):
    _, R2, C = part.shape
    tr = _row_tile(R2, C, 4)
    nblk = R2 // tr

    def body(s_ref, p_ref, r_ref, o_ref):
        acc = p_ref[...].astype(F32)
        for j in range(N_CHIPS - 1):
            acc = acc + r_ref[j].astype(F32)
        o_ref[...] = acc

    return pl.pallas_call(
        body, name=name,
        grid_spec=pltpu.PrefetchScalarGridSpec(
            num_scalar_prefetch=1, grid=(nblk,),
            in_specs=[pl.BlockSpec((None, tr, C), lambda i, s: (s[0], i, 0)),
                      pl.BlockSpec((N_CHIPS - 1, tr, C), lambda i, s: (0, i, 0))],
            out_specs=pl.BlockSpec((tr, C), lambda i, s: (s[1] * nblk + i, 0)),
        ),
        out_shape=jax.ShapeDtypeStruct((2 * R2, C), F32),
        compiler_params=_cparams(("parallel",), 32),
    )(sc_idx, part, recv)


def _cast_into_slot(w, sc_idx, dtype, after, *, name):
    R, C = w.shape
    tr = _row_tile(R, C, 4)

    def body(s_ref, w_ref, after_ref, o_ref):
        o_ref[...] = w_ref[...].astype(dtype)

    return pl.pallas_call(
        body, name=name,
        grid_spec=pltpu.PrefetchScalarGridSpec(
            num_scalar_prefetch=1, grid=(R // tr,),
            in_specs=[pl.BlockSpec((tr, C), lambda i, s: (i, 0)), pl.BlockSpec(memory_space=pl.ANY)],
            out_specs=pl.BlockSpec((None, tr, C), lambda i, s: (s[0], i, 0)),
        ),
        out_shape=jax.ShapeDtypeStruct((N_CHIPS, R, C), dtype),
        compiler_params=_cparams(("parallel",), 32),
    )(sc_idx, w, after)


def _slab_sum(slabs, *, name):
    n, M, C = slabs.shape

    def body(x_ref, o_ref):
        acc = x_ref[0]
        for d in range(1, n):
            acc = acc + x_ref[d]
        o_ref[...] = acc

    return pl.pallas_call(
        body, name=name, out_shape=jax.ShapeDtypeStruct((M, C), F32),
    )(slabs)


def _mesh_position():
    x, y, c = lax.axis_index("x"), lax.axis_index("y"), lax.axis_index("c")
    other_chips = [(1 - x, y), (x, 1 - y), (1 - x, 1 - y)]
    return x, y, c, other_chips


ANY = pl.BlockSpec(memory_space=pl.ANY)


HBM = pl.BlockSpec(memory_space=pltpu.HBM)
SEM = pl.BlockSpec(memory_space=pltpu.SEMAPHORE)
SPLIT_COPY = pltpu.CompilerParams(has_side_effects=pltpu.SideEffectType.DATAFLOW_SIDE_EFFECTING)
TOKEN = jax.ShapeDtypeStruct((8, LANES), F32)


def _in_hbm(a):
    return pltpu.with_memory_space_constraint(a, pltpu.HBM)


def _half_rows(ref, slot, half):
    hr = ref.shape[1] // 2
    return ref.at[slot, pl.ds(half * hr, hr), :]


GATHER_COPIES = {"direct": 3, "first_hop": 2, "second_hop": 1}


def _gather_routes(kind):
    x, y, c, chips = _mesh_position()
    me = 2 * x + y
    first = (x + (1 - c) * (1 - 2 * x), y + c * (1 - 2 * y))
    second = (x + c * (1 - 2 * x), y + (1 - c) * (1 - 2 * y))
    if kind == "direct":
        return [(me, (*p, c)) for p in chips], [2 * p[0] + p[1] for p in chips]
    if kind == "first_hop":
        return [(me, (*first, c)), (me, (*second, c))], [2 * first[0] + first[1], 2 * second[0] + second[1]]
    assert kind == "second_hop"
    return [(2 * first[0] + first[1], (*second, c))], [2 * (1 - x) + (1 - y)]


def _gather_ici_start(groups, kinds, *, name):
    flat = [b for g in groups for b in g]
    K, G = len(flat), len(groups)

    def body(*refs):
        ins, sems, token = refs[:K], refs[K:K + 2 * G], refs[-1]
        c = lax.axis_index("c")
        k = 0
        for gi, (g, kind) in enumerate(zip(groups, kinds)):
            sends, _ = _gather_routes(kind)
            for n in range(len(g)):
                for j, (slot, to) in enumerate(sends):
                    part = _half_rows(ins[k], slot, c)
                    pltpu.make_async_remote_copy(
                        src_ref=part, dst_ref=part, send_sem=sems[2 * gi].at[n * len(sends) + j],
                        recv_sem=sems[2 * gi + 1].at[n * len(sends) + j],
                        device_id=to, device_id_type=MESH).start()
                k += 1
        token[...] = jnp.zeros_like(token)

    sem_shapes = []
    for g, kind in zip(groups, kinds):
        sem_shapes += [pltpu.SemaphoreType.DMA((len(g) * GATHER_COPIES[kind],))] * 2
    out = pl.pallas_call(
        body, name=name,
        in_specs=[HBM] * K,
        out_specs=[SEM] * (2 * G) + [HBM] * K + [pl.BlockSpec(memory_space=pltpu.VMEM)],
        out_shape=sem_shapes + [pltpu.HBM(b.shape, b.dtype) for b in flat] + [TOKEN],
        input_output_aliases={k: 2 * G + k for k in range(K)},
        compiler_params=SPLIT_COPY,
    )(*[_in_hbm(b) for b in flat])
    handles, k = [], 2 * G
    for gi, (g, kind) in enumerate(zip(groups, kinds)):
        handles.append((out[2 * gi], out[2 * gi + 1], list(out[k:k + len(g)]), kind))
        k += len(g)
    return handles, out[-1]


def _gather_ici_wait(handle, after, *, name):
    send, recv, bufs, kind = handle
    n = len(bufs)

    def body(*refs):
        ins, send_ref, recv_ref = refs[:n], refs[n], refs[n + 1]
        c = lax.axis_index("c")
        sends, arrivals = _gather_routes(kind)
        for k in range(n):
            for j, ((slot, to), landed) in enumerate(zip(sends, arrivals)):
                cp = pltpu.make_async_remote_copy(
                    src_ref=_half_rows(ins[k], slot, c), dst_ref=_half_rows(ins[k], landed, c),
                    send_sem=send_ref.at[k * len(sends) + j], recv_sem=recv_ref.at[k * len(sends) + j],
                    device_id=to, device_id_type=MESH)
                cp.wait_send()
                cp.wait_recv()

    return pl.pallas_call(
        body, name=name,
        in_specs=[HBM] * n + [SEM, SEM, ANY], out_specs=[HBM] * n,
        out_shape=[pltpu.HBM(b.shape, b.dtype) for b in bufs],
        input_output_aliases={k: k for k in range(n)},
        compiler_params=SPLIT_COPY,
    )(*bufs, send, recv, after)


def _forward_halves(bufs, *, name):
    K = len(bufs)
    per = N_CHIPS - 1

    def body(*refs):
        outs = refs[K:2 * K]
        send_sems, recv_sems = refs[2 * K:]
        x, y, c, chips = _mesh_position()
        copies = []
        for k in range(K):
            for j, chip in enumerate(chips):
                got = _half_rows(outs[k], 2 * chip[0] + chip[1], c)
                cp = pltpu.make_async_remote_copy(
                    src_ref=got, dst_ref=got, send_sem=send_sems.at[k * per + j], recv_sem=recv_sems.at[k * per + j],
                    device_id=(x, y, 1 - c), device_id_type=MESH)
                cp.start()
                copies.append(cp)
        for cp in copies:
            cp.wait()

    return pl.pallas_call(
        body, name=name,
        in_specs=[ANY] * K, out_specs=[ANY] * K,
        out_shape=[jax.ShapeDtypeStruct(a.shape, a.dtype) for a in bufs],
        input_output_aliases={k: k for k in range(K)},
        scratch_shapes=[pltpu.SemaphoreType.DMA((K * per,)), pltpu.SemaphoreType.DMA((K * per,))],
    )(*bufs)


def _chip_exchange_copies(srcs, lands, send_sems, recv_sems):
    x, y, c, chips = _mesh_position()
    per = N_CHIPS - 1
    return [pltpu.make_async_remote_copy(
        src_ref=srcs[k].at[2 * chip[0] + chip[1]], dst_ref=lands[k].at[j],
        send_sem=send_sems.at[k * per + j], recv_sem=recv_sems.at[k * per + j],
        device_id=(*chip, c), device_id_type=MESH) for k in range(len(srcs)) for j, chip in enumerate(chips)]


def _sibling_swap_copies(srcs, lands, send_sems, recv_sems):
    x, y, c, _ = _mesh_position()
    return [pltpu.make_async_remote_copy(
        src_ref=srcs[k].at[pl.ds(0, srcs[k].shape[0]), 1 - c], dst_ref=lands[k],
        send_sem=send_sems.at[k], recv_sem=recv_sems.at[k],
        device_id=(x, y, 1 - c), device_id_type=MESH) for k in range(len(srcs))]


def _sibling_send_copies(srcs, lands, send_sems, recv_sems):
    x, y, c, _ = _mesh_position()
    return [pltpu.make_async_remote_copy(
        src_ref=srcs[k], dst_ref=lands[k], send_sem=send_sems.at[k], recv_sem=recv_sems.at[k],
        device_id=(x, y, 1 - c), device_id_type=MESH) for k in range(len(srcs))]


def _split_copy_start(srcs, land_shapes, n_sems, copies, *, name):
    K = len(srcs)

    def body(*refs):
        for cp in copies(refs[:K], refs[K:2 * K], refs[2 * K], refs[2 * K + 1]):
            cp.start()
        refs[-1][...] = jnp.zeros_like(refs[-1])

    out = pl.pallas_call(
        body, name=name,
        in_specs=[HBM] * (2 * K),
        out_specs=[SEM, SEM] + [HBM] * (2 * K) + [pl.BlockSpec(memory_space=pltpu.VMEM)],
        out_shape=[pltpu.SemaphoreType.DMA((n_sems,))] * 2
        + [pltpu.HBM(a.shape, a.dtype) for a in srcs]
        + [pltpu.HBM(s, a.dtype) for s, a in zip(land_shapes, srcs)] + [TOKEN],
        input_output_aliases={k: 2 + k for k in range(2 * K)},
        compiler_params=SPLIT_COPY,
    )(*[_in_hbm(a) for a in srcs], *[_in_hbm(lax.empty(s, a.dtype)) for s, a in zip(land_shapes, srcs)])
    return (out[0], out[1], list(out[2:2 + K]), list(out[2 + K:2 + 2 * K])), out[-1]


def _split_copy_wait(handle, copies, after, *, name):
    send, recv, srcs, lands = handle
    K = len(srcs)

    def body(*refs):
        for cp in copies(refs[:K], refs[K:2 * K], refs[2 * K], refs[2 * K + 1]):
            cp.wait_send()
            cp.wait_recv()

    out = pl.pallas_call(
        body, name=name,
        in_specs=[HBM] * (2 * K) + [SEM, SEM, ANY], out_specs=[HBM] * (2 * K),
        out_shape=[pltpu.HBM(a.shape, a.dtype) for a in srcs] + [pltpu.HBM(a.shape, a.dtype) for a in lands],
        input_output_aliases={k: k for k in range(2 * K)},
        compiler_params=SPLIT_COPY,
    )(*srcs, *lands, send, recv, after)
    return list(out[:K]), list(out[K:])


def _join_copies(bufs, send_sems, recv_sems):
    x, y, c, _ = _mesh_position()
    copies = []
    for k, buf in enumerate(bufs):
        r2 = buf.shape[0] // 2
        mine = buf.at[pl.ds(c * r2, r2), :]
        copies.append(pltpu.make_async_remote_copy(
            src_ref=mine, dst_ref=mine, send_sem=send_sems.at[k], recv_sem=recv_sems.at[k],
            device_id=(x, y, 1 - c), device_id_type=MESH))
    return copies


def _join_start(bufs, *, name):
    K = len(bufs)

    def body(*refs):
        for cp in _join_copies(refs[:K], refs[K], refs[K + 1]):
            cp.start()
        refs[-1][...] = jnp.zeros_like(refs[-1])

    out = pl.pallas_call(
        body, name=name,
        in_specs=[HBM] * K,
        out_specs=[SEM, SEM] + [HBM] * K + [pl.BlockSpec(memory_space=pltpu.VMEM)],
        out_shape=[pltpu.SemaphoreType.DMA((K,))] * 2 + [pltpu.HBM(a.shape, a.dtype) for a in bufs] + [TOKEN],
        input_output_aliases={k: 2 + k for k in range(K)},
        compiler_params=SPLIT_COPY,
    )(*[_in_hbm(a) for a in bufs])
    return (out[0], out[1], list(out[2:2 + K])), out[-1]


def _join_wait(handle, after, *, name):
    send, recv, bufs = handle
    K = len(bufs)

    def body(*refs):
        for cp in _join_copies(refs[:K], refs[K], refs[K + 1]):
            cp.wait_send()
            cp.wait_recv()

    return pl.pallas_call(
        body, name=name,
        in_specs=[HBM] * K + [SEM, SEM, ANY], out_specs=[HBM] * K,
        out_shape=[pltpu.HBM(a.shape, a.dtype) for a in bufs],
        input_output_aliases={k: k for k in range(K)},
        compiler_params=SPLIT_COPY,
    )(*bufs, send, recv, after)


def _all_gather_slab(slab):
    m_per, n = slab.shape

    def body(x_ref, out_ref, send_sems, recv_sems, local_sem):
        x, y, c, chips = _mesh_position()
        me, sibling = (x, y, c), (x, y, 1 - c)

        def rows(px, py, pc):
            return out_ref.at[pl.ds((4 * px + 2 * py + pc) * m_per, m_per), :]

        def copy(k, block, to, src=None):
            return pltpu.make_async_remote_copy(
                src_ref=rows(*block) if src is None else src, dst_ref=rows(*block),
                send_sem=send_sems.at[k], recv_sem=recv_sems.at[k], device_id=to, device_id_type=MESH)

        mine = pltpu.make_async_copy(x_ref, rows(*me), local_sem)
        mine.start()
        first = [copy(0, me, sibling, src=x_ref)]
        first += [copy(1 + j, me, (*chip, c), src=x_ref) for j, chip in enumerate(chips)]
        for cp in first:
            cp.start()
        passed = [copy(4 + j, (*chip, c), sibling) for j, chip in enumerate(chips)]
        for j, chip in enumerate(chips):
            copy(1 + j, (*chip, c), me).wait_recv()
            passed[j].start()
        copy(0, sibling, me).wait_recv()
        for j, chip in enumerate(chips):
            copy(4 + j, (*chip, 1 - c), me).wait_recv()
        for cp in first + passed:
            cp.wait_send()
        mine.wait()

    return pl.pallas_call(
        body, name="gather_small_grads",
        out_shape=jax.ShapeDtypeStruct((N_DEV * m_per, n), slab.dtype),
        in_specs=[pl.BlockSpec(memory_space=pltpu.VMEM)],
        out_specs=pl.BlockSpec(memory_space=pltpu.VMEM),
        scratch_shapes=[pltpu.SemaphoreType.DMA((7,)), pltpu.SemaphoreType.DMA((7,)), pltpu.SemaphoreType.DMA],
    )(slab)


def _ffn_dw_out(dhb, a, tag):
    S, F = a.shape
    D = dhb.shape[1]
    to = 512
    return _mm_tn(
        a, dhb, grid=(F // to,),
        a_spec=pl.BlockSpec((S, to), lambda j: (0, j)), b_spec=pl.BlockSpec((S, D), lambda j: (0, 0)),
        out_spec=pl.BlockSpec((to, D), lambda j: (j, 0)), out_shape=jax.ShapeDtypeStruct((F, D), BF16),
        scale=0.5, name=f"{tag}_dw_out").reshape(N_CHIPS, F // N_CHIPS, D)


def _ffn_dw_in_half(nt, dgu, half, add, w_in_shape, *, name):
    ns, D, cs = w_in_shape
    S = nt.shape[1]
    F = dgu.shape[2]
    ti, tr = SHARD_TILE, D // 2
    per_g, per_s = F // ti, cs // ti

    def body(h_ref, a_ref, b_ref, *rest):
        acc = _dot_nn(a_ref[...], b_ref[...])
        if add is not None:
            acc = acc + rest[0][...].astype(F32)
        rest[-1][...] = acc.astype(BF16)

    out_spec = pl.BlockSpec((None, tr, ti), lambda j, h: (lax.div(j, per_s), 0, lax.rem(j, per_s)))
    extra = [] if add is None else [add]
    return pl.pallas_call(
        body, name=name,
        grid_spec=pltpu.PrefetchScalarGridSpec(
            num_scalar_prefetch=1, grid=(2 * F // ti,),
            in_specs=[pl.BlockSpec((tr, S), lambda j, h: (h[0], 0)),
                      pl.BlockSpec((None, S, ti), lambda j, h: (lax.div(j, per_g), 0, lax.rem(j, per_g)))]
            + [out_spec] * len(extra),
            out_specs=out_spec,
        ),
        out_shape=jax.ShapeDtypeStruct((ns, tr, cs), BF16),
        compiler_params=_cparams(("parallel",), 56),
    )(half, nt, dgu, *extra)


def kernel(x, ffn1_norm, ffn1_w_in, ffn1_w_out, mix_norm, w_in_mix, w_pool, pool_scale, w_alpha, b_alpha, gla_norm, w_out_mix, ffn2_norm, ffn2_w_in, ffn2_w_out, final_norm, loss_target, m_ffn1_norm, m_ffn1_w_in, m_ffn1_w_out, m_mix_norm, m_w_in_mix, m_w_pool, m_pool_scale, m_w_alpha, m_b_alpha, m_gla_norm, m_w_out_mix, m_ffn2_norm, m_ffn2_w_in, m_ffn2_w_out, m_final_norm, v_ffn1_norm, v_ffn1_w_in, v_ffn1_w_out, v_mix_norm, v_w_in_mix, v_w_pool, v_pool_scale, v_w_alpha, v_b_alpha, v_gla_norm, v_w_out_mix, v_ffn2_norm, v_ffn2_w_in, v_ffn2_w_out, v_final_norm):
    names = ["ffn1_norm", "ffn1_w_in", "ffn1_w_out", "mix_norm", "w_in_mix", "w_pool", "pool_scale", "w_alpha",
             "b_alpha", "gla_norm", "w_out_mix", "ffn2_norm", "ffn2_w_in", "ffn2_w_out", "final_norm"]
    weights = dict(zip(names, [ffn1_norm, ffn1_w_in, ffn1_w_out, mix_norm, w_in_mix, w_pool, pool_scale, w_alpha,
                               b_alpha, gla_norm, w_out_mix, ffn2_norm, ffn2_w_in, ffn2_w_out, final_norm]))
    moms = dict(zip(names, [m_ffn1_norm, m_ffn1_w_in, m_ffn1_w_out, m_mix_norm, m_w_in_mix, m_w_pool, m_pool_scale,
                            m_w_alpha, m_b_alpha, m_gla_norm, m_w_out_mix, m_ffn2_norm, m_ffn2_w_in, m_ffn2_w_out,
                            m_final_norm]))
    vels = dict(zip(names, [v_ffn1_norm, v_ffn1_w_in, v_ffn1_w_out, v_mix_norm, v_w_in_mix, v_w_pool, v_pool_scale,
                            v_w_alpha, v_b_alpha, v_gla_norm, v_w_out_mix, v_ffn2_norm, v_ffn2_w_in, v_ffn2_w_out,
                            v_final_norm]))
    xi, yi, ci = lax.axis_index("x"), lax.axis_index("y"), lax.axis_index("c")
    chip = 2 * xi + yi
    c_idx = jnp.reshape(ci, (1,)).astype(jnp.int32)
    sc_idx = jnp.stack([chip, ci]).astype(jnp.int32)

    def flat2d(a):
        return a.reshape(-1, a.shape[-1])

    ex = _Exchanges(sc_idx, c_idx)

    def cast(n, after):
        return _cast_into_slot(flat2d(weights[n]), sc_idx, F32 if n == "w_alpha" else BF16, after, name=f"cast_{n}")

    groups = _Exchanges.GATHER_GROUPS
    w1_in = groups[0][0]
    tok = ex.start_gather({w1_in: cast(w1_in, sc_idx)}, groups[:1], ("first_hop",), name="gather_ici_start_first")
    bufs, last = {}, tok
    for g in groups[1:]:
        for n in g:
            bufs[n] = last = cast(n, last)
    bufs[w1_in] = ex.arrived(w1_in, after=last)[w1_in]
    tok2 = ex.start_gather(bufs, groups, ("second_hop",) + ("direct",) * (len(groups) - 1),
                           name="gather_ici_start_rest")
    early = [flat2d(moms["w_in_mix"]), flat2d(vels["w_in_mix"])]
    small_params = dict(g1=ffn1_norm, gm=mix_norm, g2=ffn2_norm, gf=final_norm.reshape(1, D_MODEL),
                        pool_scale=pool_scale, b_alpha=b_alpha, gla_norm=gla_norm, early=early, core=c_idx)
    loss_blk, dx, small = _forward_backward(x[0], loss_target[0], ex, tok[0, 0] + tok2[0, 0], small_params)

    outs = {}

    def update(n, g, after):
        w = weights[n]
        w2 = flat2d(w) if w.ndim > 1 else w.reshape(1, -1)
        go, d, nm, nv = _adamw(w2, g.reshape(w2.shape), moms[n].reshape(w2.shape), vels[n].reshape(w2.shape), after,
                               name=f"adamw_{n}")
        outs[n] = (go.reshape(w.shape), d.reshape(w.shape), nm.reshape(w.shape), nv.reshape(w.shape))
        return nv

    tags = _Exchanges.REDUCE_ORDER
    last = ex.finish_exchange(tags[0], after=dx)
    for prev, tag in zip(tags, tags[1:]):
        last = ex.finish_exchange(tag, after=last)
        for n, g in ex.reduced(prev, after=last).items():
            last = update(n, g, last)

    grads = {}
    small_names = ["ffn1_norm", "mix_norm", "ffn2_norm", "final_norm", "pool_scale", "b_alpha", "gla_norm", "w_alpha",
                   "loss"]
    small = small + [loss_blk[0:1]]
    rows = [a.size // LANES for a in small]
    slab = jnp.concatenate([a.reshape(-1, LANES) for a in small], axis=0)
    pad = -slab.shape[0] % 8
    slab = jnp.pad(slab, ((0, pad), (0, 0)))
    gathered = _all_gather_slab(slab).reshape(N_DEV, slab.shape[0], LANES)
    total = _slab_sum(gathered, name="sum_small_grads")
    off = 0
    for n, a, r in zip(small_names, small, rows):
        grads[n] = total[off:off + r].reshape(a.shape)
        off += r
    grads["w_alpha"] = lax.dynamic_slice_in_dim(grads["w_alpha"], chip * (GLA_DK_TOTAL // N_CHIPS),
                                                GLA_DK_TOTAL // N_CHIPS, axis=1)

    loss = grads.pop("loss")[0, 0]
    for n in small_names[:-1]:
        last = update(n, grads[n], last)
    for n, g in ex.reduced(tags[-1], after=last).items():
        last = update(n, g, last)
    return (loss, dx[None], *[outs[n][0] for n in names], *[outs[n][1] for n in names],
            *[outs[n][2] for n in names], *[outs[n][3] for n in names])


class _Exchanges:
    GATHER_GROUPS = (("ffn1_w_in",), ("ffn1_w_out",), ("w_in_mix", "w_pool", "w_alpha"), ("w_out_mix",),
                     ("ffn2_w_in",), ("ffn2_w_out",))
    REDUCE_ORDER = ("ffn2_out", "ffn2_in", "mix", "ffn1_out", "ffn1_in")

    def __init__(self, sc_idx, c_idx):
        self.sc_idx, self.c_idx = sc_idx, c_idx
        self._gathers, self._swaps, self._sends, self._reduces, self._joins = {}, {}, {}, {}, {}

    def start_gather(self, bufs, groups, kinds, *, name):
        handles, token = _gather_ici_start([[bufs[n] for n in g] for g in groups], kinds, name=name)
        for g, h in zip(groups, handles):
            self._gathers[g[0]] = (g, h)
        return token

    def arrived(self, first, after):
        names, handle = self._gathers.pop(first)
        return dict(zip(names, _gather_ici_wait(handle, after, name=f"gather_ici_wait_{first}_{handle[3]}")))

    def gathered(self, first, after):
        got = self.arrived(first, after)
        return dict(zip(got, _forward_halves(list(got.values()), name=f"gather_forward_{first}")))

    def begin_reduce(self, tag, full):
        g4 = [a.reshape(N_CHIPS, 2, a.shape[1] // 2, a.shape[2]) for a in full.values()]
        lands = [(a.shape[0],) + a.shape[2:] for a in g4]
        handle, token = _split_copy_start(g4, lands, len(g4), _sibling_swap_copies, name=f"swap_start_{tag}")
        self._swaps[tag] = (list(full), handle)
        return token

    def start_reduce(self, tag, after):
        names, handle = self._swaps.pop(tag)
        g4, from_sibling = _split_copy_wait(handle, _sibling_swap_copies, after, name=f"swap_wait_{tag}")
        pair = [_pair_sum(a, b, self.c_idx, name=f"pair_sum_{n}") for n, a, b in zip(names, g4, from_sibling)]
        return self.start_exchange(tag, names, pair)

    def start_exchange(self, tag, names, pair):
        lands = [(N_CHIPS - 1,) + a.shape[1:] for a in pair]
        handle, token = _split_copy_start(pair, lands, len(pair) * (N_CHIPS - 1), _chip_exchange_copies,
                                          name=f"exchange_start_{tag}")
        self._reduces[tag] = (names, handle)
        return token

    def send_to_sibling(self, tag, arrays):
        handle, token = _split_copy_start(arrays, [a.shape for a in arrays], len(arrays), _sibling_send_copies,
                                          name=f"send_start_{tag}")
        self._sends[tag] = handle
        return token

    def from_sibling(self, tag, after):
        return _split_copy_wait(self._sends.pop(tag), _sibling_send_copies, after, name=f"send_wait_{tag}")[1]

    def finish_exchange(self, tag, after):
        names, handle = self._reduces.pop(tag)
        pair, lands = _split_copy_wait(handle, _chip_exchange_copies, after, name=f"exchange_wait_{tag}")
        halves = [_chip_sum(a, b, self.sc_idx, name=f"chip_sum_{n}") for n, a, b in zip(names, pair, lands)]
        handle, token = _join_start(halves, name=f"join_start_{tag}")
        self._joins[tag] = (names, handle)
        return token

    def reduced(self, tag, after):
        names, handle = self._joins.pop(tag)
        return dict(zip(names, _join_wait(handle, after, name=f"join_wait_{tag}")))


def _forward_backward(h0, target, ex, started, sp):
    g1, gm, g2, gf = sp["g1"], sp["gm"], sp["g2"], sp["gf"]
    pool_scale, b_alpha, gla_norm = sp["pool_scale"], sp["b_alpha"], sp["gla_norm"]
    cs_mix = D_IN // N_CHIPS

    n1, n1t = _rms_fwd(h0, g1 + started, sp["early"], name="ffn1_norm")
    w1_in = ex.gathered("ffn1_w_in", after=n1)["ffn1_w_in"]
    fa1, a1 = _ffn_up(n1, w1_in, [], name="ffn1_up")
    w1_out = ex.gathered("ffn1_w_out", after=a1)["ffn1_w_out"].reshape(D_FF, D_MODEL)
    h1 = _mm_nn(a1, w1_out, h0, 0.5, tm=512, tn=D_MODEL, tk=SHARD_TILE, name="ffn1_down")
    n_mix, n_mixt = _rms_fwd(h1, gm, [], name="mix_norm")
    gw = ex.gathered("w_in_mix", after=n_mix)
    w_mix = _concat_shards(gw["w_in_mix"], D_IN_PAD, name="w_mix_concat")[None]
    wp = gw["w_pool"].reshape(N_CHIPS, 4, POOL_GROUP_DIM // N_CHIPS, POOL_GROUP_DIM)
    wp = wp.transpose(1, 0, 2, 3).reshape(4, POOL_GROUP_DIM, POOL_GROUP_DIM)
    wa = gw["w_alpha"].transpose(1, 0, 2).reshape(GLA_GATE_RANK, GLA_DK_TOTAL)
    wa = jnp.pad(wa, ((0, LANES - GLA_GATE_RANK), (0, 0))).astype(BF16)
    u = _mm_nn(n_mix, w_mix[0], None, 1.0, tm=512, tn=D_IN_PAD, tk=D_MODEL, name="mix_in")
    y_pool = _pool_fwd(u, wp, pool_scale, name="pool_fwd")
    cat, o_gla, states = _gla_fwd(u, y_pool, wa, b_alpha, gla_norm, name="gla_fwd")
    w_omix = ex.gathered("w_out_mix", after=cat)["w_out_mix"].reshape(D_MODEL, D_MODEL)
    h2 = _mm_nn(cat, w_omix, h1, 1.0, tm=512, tn=D_MODEL, tk=1024, name="mix_out")
    n3, n3t = _rms_fwd(h2, g2, [], name="ffn2_norm")
    w2_in = ex.gathered("ffn2_w_in", after=n3)["ffn2_w_in"]
    fa2, a2 = _ffn_up(n3, w2_in, [], name="ffn2_up")
    w2_out = ex.gathered("ffn2_w_out", after=a2)["ffn2_w_out"].reshape(D_FF, D_MODEL)
    h3 = _mm_nn(a2, w2_out, h2, 0.5, tm=512, tn=D_MODEL, tk=SHARD_TILE, name="ffn2_down")
    loss_blk, dh3, dh3b, d_gf = _final_loss(h3, gf, target, name="final_loss")

    core = sp["core"]
    tok = ex.begin_reduce("ffn2_out", {"ffn2_w_out": _ffn_dw_out(dh3b, a2, "ffn2")})
    dgu2 = _ffn_bwd_act(dh3b, w2_out, fa2, tok, name="ffn2_bwd_act")
    theirs = _ffn_dw_in_half(n3t, dgu2, 1 - core, None, w2_in.shape, name="ffn2_dw_in_other")
    tok = ex.send_to_sibling("ffn2_in", [theirs])
    dh2, dh2b, d_g2 = _mm_nt_rmsbwd(dgu2, w2_in, h2, dh3, g2 + tok[0, 0], tk=SHARD_TILE, name="ffn2_dx")
    tok = ex.start_reduce("ffn2_out", after=dh2b)
    pair = _ffn_dw_in_half(n3t, dgu2, core, ex.from_sibling("ffn2_in", after=tok)[0], w2_in.shape,
                           name="ffn2_dw_in_own")
    tok = ex.start_exchange("ffn2_in", ["ffn2_w_in"], [pair])
    S = h0.shape[0]
    dcat = _mm_nt(dh2b, w_omix, tok, tm=512, tn=1024, name="mix_out_dx")
    dw_omix = _mm_tn(
        cat, dh2b, grid=(4,),
        a_spec=pl.BlockSpec((S, 512), lambda j: (0, j)), b_spec=pl.BlockSpec((S, D_MODEL), lambda j: (0, 0)),
        out_spec=pl.BlockSpec((512, D_MODEL), lambda j: (j, 0)),
        out_shape=jax.ShapeDtypeStruct((D_MODEL, D_MODEL), BF16), scale=1.0, name="mix_out_dw")
    dp, dw_pool, d_pscale = _pool_bwd(u, dcat, wp, pool_scale, name="pool_bwd")
    du, d_wa, d_ba, d_gn = _gla_bwd(u, o_gla, states, dcat, dp, wa, b_alpha, gla_norm, name="gla_bwd")
    du = du[None]
    tn_mix = COL_TILE
    dw_mix = _mm_tn(
        n_mixt, du, grid=(2, D_IN_PAD // tn_mix), a_is_transposed=True,
        a_spec=pl.BlockSpec((D_MODEL // 2, S), lambda i, j: (i, 0)),
        b_spec=pl.BlockSpec((None, S, tn_mix), lambda i, j: (0, 0, j)),
        out_spec=pl.BlockSpec((D_MODEL // 2, tn_mix), lambda i, j: (i, j)),
        out_shape=jax.ShapeDtypeStruct((D_MODEL, D_IN_PAD), BF16), scale=1.0, name="mix_in_dw")
    dw_mix_s = _split_shards(dw_mix, cs_mix, name="dw_mix_split")
    dw_pool_s = dw_pool.reshape(4, N_CHIPS, POOL_GROUP_DIM // N_CHIPS, POOL_GROUP_DIM).transpose(1, 0, 2, 3)
    dw_pool_s = dw_pool_s.reshape(N_CHIPS, POOL_GROUP_DIM, POOL_GROUP_DIM).astype(BF16)
    tok = ex.begin_reduce("mix", {"w_in_mix": dw_mix_s,
                                  "w_out_mix": dw_omix.reshape(N_CHIPS, D_MODEL // N_CHIPS, D_MODEL),
                                  "w_pool": dw_pool_s})
    dh1, dh1b, d_gm = _mm_nt_rmsbwd(du, w_mix, h1, dh2, gm + tok[0, 0], tk=tn_mix, name="mix_in_dx")
    tok = ex.start_reduce("mix", after=dh1b)
    tok = tok + ex.begin_reduce("ffn1_out", {"ffn1_w_out": _ffn_dw_out(dh1b, a1, "ffn1")})
    dgu1 = _ffn_bwd_act(dh1b, w1_out, fa1, tok, name="ffn1_bwd_act")
    theirs = _ffn_dw_in_half(n1t, dgu1, 1 - core, None, w1_in.shape, name="ffn1_dw_in_other")
    tok = ex.send_to_sibling("ffn1_in", [theirs])
    tok = ex.start_reduce("ffn1_out", after=tok)
    pair = _ffn_dw_in_half(n1t, dgu1, core, ex.from_sibling("ffn1_in", after=tok)[0], w1_in.shape,
                           name="ffn1_dw_in_own")
    tok = ex.start_exchange("ffn1_in", ["ffn1_w_in"], [pair])
    dx, _, d_g1 = _mm_nt_rmsbwd(dgu1, w1_in, h0, dh1, g1 + tok[0, 0], tk=SHARD_TILE, name="ffn1_dx")
    small = [d_g1, d_gm, d_g2, d_gf, d_pscale, d_ba, d_gn, d_wa[:GLA_GATE_RANK]]
    return loss_blk, dx, small
```

```python
import functools

import jax
import jax.numpy as jnp
from jax import lax
from jax.experimental import pallas as pl
from jax.experimental.pallas import tpu as pltpu

F32 = jnp.float32
BF16 = jnp.bfloat16
MESH = pl.DeviceIdType.MESH

D_MODEL = 2048
D_FF = 5632
D_POOL = 1024
POOL_WINDOWS = (2, 4, 8, 16)
POOL_GROUP_DIM = 256
D_GLA = 1024
GLA_HEADS = 4
GLA_DV = 256
GLA_DK = 128
GLA_DK_TOTAL = 512
GLA_GATE_RANK = 16
GATE_LOGIT_NORMALIZER = 16.0
CHUNK = 64
D_IN = 4112
D_IN_PAD = 4224
EPS = 1e-6

ADAM_LR = 0.001
ADAM_B1 = 0.9
ADAM_B2 = 0.999
ADAM_EPS = 1e-08
ADAM_WD = 0.01
ADAM_STEP = 10

N_CHIPS = 4
N_DEV = 8
V7X_VMEM_BYTES = 64 * 1024 * 1024
LANES = 128
MXU_TILE = 256
COL_TILE = 1408
SHARD_TILE = 2816


def _cparams(semantics, vmem_mb):
    assert vmem_mb * 1024 * 1024 < V7X_VMEM_BYTES
    return pltpu.CompilerParams(dimension_semantics=semantics, vmem_limit_bytes=vmem_mb * 1024 * 1024)


def _dot_nn(a, b):
    return jnp.dot(a, b, preferred_element_type=F32)


def _dot_nt(a, b):
    return lax.dot_general(a, b, (((1,), (1,)), ((), ())), preferred_element_type=F32)


def _dot_tn(a, b):
    return lax.dot_general(a, b, (((0,), (0,)), ((), ())), preferred_element_type=F32)


def _sigmoid(x):
    return 1.0 / (1.0 + jnp.exp(-x))


def _rms_fwd(x, g, early, *, name):
    S, D = x.shape
    tm = 256

    def body(x_ref, g_ref, *rest):
        o_ref, ot_ref = rest[len(early):]
        xv = x_ref[...]
        r = lax.rsqrt(jnp.mean(xv * xv, axis=-1, keepdims=True) + EPS)
        n = xv * r * g_ref[...]
        o_ref[...] = n.astype(BF16)
        ot_ref[...] = n.T.astype(BF16)

    return pl.pallas_call(
        body, name=name, grid=(S // tm,),
        in_specs=[pl.BlockSpec((tm, D), lambda i: (i, 0)), pl.BlockSpec((1, D), lambda i: (0, 0))]
        + [pl.BlockSpec(memory_space=pl.ANY)] * len(early),
        out_specs=[pl.BlockSpec((tm, D), lambda i: (i, 0)), pl.BlockSpec((D, tm), lambda i: (0, i))],
        out_shape=[jax.ShapeDtypeStruct((S, D), BF16), jax.ShapeDtypeStruct((D, S), BF16)],
        compiler_params=_cparams(("parallel",), 32),
    )(x, g, *early)


def _ffn_up(n, w_in, early, *, name):
    S, D = n.shape
    ns, _, cs = w_in.shape
    half = ns // 2
    F = cs * half
    tm, tn = 128, SHARD_TILE
    nb = cs // tn

    def body(n_ref, wg_ref, wu_ref, *rest):
        fa_ref, a_ref = rest[len(early):]
        nv = n_ref[...]
        g = _dot_nn(nv, wg_ref[...])
        u = _dot_nn(nv, wu_ref[...])
        s = _sigmoid(g)
        silu = g * s
        fa_ref[0] = (u * (s * (1.0 + g * (1.0 - s)))).astype(BF16)
        fa_ref[1] = silu.astype(BF16)
        a_ref[...] = (silu * u).astype(BF16)

    return pl.pallas_call(
        body, name=name, grid=(F // tn, S // tm),
        in_specs=[
            pl.BlockSpec((tm, D), lambda j, i: (i, 0)),
            pl.BlockSpec((None, D, tn), lambda j, i: (lax.div(j, nb), 0, lax.rem(j, nb))),
            pl.BlockSpec((None, D, tn), lambda j, i: (half + lax.div(j, nb), 0, lax.rem(j, nb))),
        ] + [pl.BlockSpec(memory_space=pl.ANY)] * len(early),
        out_specs=[
            pl.BlockSpec((2, tm, tn), lambda j, i: (0, i, j)),
            pl.BlockSpec((tm, tn), lambda j, i: (i, j)),
        ],
        out_shape=[jax.ShapeDtypeStruct((2, S, F), BF16), jax.ShapeDtypeStruct((S, F), BF16)],
        compiler_params=_cparams(("parallel", "parallel"), 56),
    )(n, w_in, w_in, *early)


def _mm_nn(a, b, resid, scale, *, tm, tn, tk, name):
    S, K = a.shape
    N = b.shape[1]
    nk = K // tk

    def body(*refs):
        if resid is None:
            a_ref, b_ref, o_ref, acc_ref = refs
            r_ref = None
        else:
            a_ref, b_ref, r_ref, o_ref, acc_ref = refs
        k = pl.program_id(2)

        @pl.when(k == 0)
        def _():
            acc_ref[...] = jnp.zeros_like(acc_ref)

        acc_ref[...] += _dot_nn(a_ref[...], b_ref[...])

        @pl.when(k == nk - 1)
        def _():
            out = acc_ref[...] * scale
            if r_ref is not None:
                out = r_ref[...] + out
            o_ref[...] = out

    in_specs = [pl.BlockSpec((tm, tk), lambda i, j, k: (i, k)), pl.BlockSpec((tk, tn), lambda i, j, k: (k, j))]
    args = [a, b]
    if resid is not None:
        in_specs.append(pl.BlockSpec((tm, tn), lambda i, j, k: (i, j)))
        args.append(resid)
    return pl.pallas_call(
        body, name=name, grid=(S // tm, N // tn, nk),
        in_specs=in_specs,
        out_specs=pl.BlockSpec((tm, tn), lambda i, j, k: (i, j)),
        out_shape=jax.ShapeDtypeStruct((S, N), F32),
        scratch_shapes=[pltpu.VMEM((tm, tn), F32)],
        compiler_params=_cparams(("parallel", "parallel", "arbitrary"), 48),
    )(*args)


def _mm_nt(a, b, after, *, tm, tn, name):
    S, K = a.shape
    N = b.shape[0]

    def body(a_ref, b_ref, after_ref, o_ref):
        o_ref[...] = _dot_nt(a_ref[...], b_ref[...])

    return pl.pallas_call(
        body, name=name, grid=(N // tn, S // tm),
        in_specs=[pl.BlockSpec((tm, K), lambda j, i: (i, 0)), pl.BlockSpec((tn, K), lambda j, i: (j, 0)),
                  pl.BlockSpec(memory_space=pl.ANY)],
        out_specs=pl.BlockSpec((tm, tn), lambda j, i: (i, j)),
        out_shape=jax.ShapeDtypeStruct((S, N), F32),
        compiler_params=_cparams(("parallel", "parallel"), 48),
    )(a, b, after)


def _mm_tn(a, b, *, grid, a_spec, b_spec, out_spec, out_shape, scale, name, a_is_transposed=False):
    dot = _dot_nn if a_is_transposed else _dot_tn

    def body(a_ref, b_ref, o_ref):
        o_ref[...] = (scale * dot(a_ref[...], b_ref[...])).astype(o_ref.dtype)

    return pl.pallas_call(
        body, name=name, grid=grid, in_specs=[a_spec, b_spec], out_specs=out_spec, out_shape=out_shape,
        compiler_params=_cparams(("parallel",) * len(grid), 56),
    )(a, b)


def _ffn_bwd_act(dhb, w_out, fa, after, *, name):
    S, D = dhb.shape
    F = w_out.shape[0]
    tm, tn = 256, SHARD_TILE

    def body(dh_ref, w_ref, fa_ref, after_ref, dgu_ref):
        da = 0.5 * _dot_nt(dh_ref[...], w_ref[...])
        dgu_ref[0] = (da * fa_ref[0].astype(F32)).astype(BF16)
        dgu_ref[1] = (da * fa_ref[1].astype(F32)).astype(BF16)

    return pl.pallas_call(
        body, name=name, grid=(F // tn, S // tm),
        in_specs=[
            pl.BlockSpec((tm, D), lambda j, i: (i, 0)),
            pl.BlockSpec((tn, D), lambda j, i: (j, 0)),
            pl.BlockSpec((2, tm, tn), lambda j, i: (0, i, j)),
            pl.BlockSpec(memory_space=pl.ANY),
        ],
        out_specs=pl.BlockSpec((2, tm, tn), lambda j, i: (0, i, j)),
        out_shape=jax.ShapeDtypeStruct((2, S, F), BF16),
        compiler_params=_cparams(("parallel", "parallel"), 56),
    )(dhb, w_out, fa, after)


def _mm_nt_rmsbwd(dact, w, h_in, dh_out, g, *, tk, name):
    ng, S, fg = dact.shape
    ns, D, cs = w.shape
    assert ng * fg == ns * cs
    tm, rc = 512, 64
    kpg, kps = fg // tk, cs // tk
    nk = ng * kpg

    def body(a_ref, w_ref, h_ref, dho_ref, g_ref, dh_ref, dhb_ref, dg_ref, acc_ref):
        i = pl.program_id(0)
        k = pl.program_id(1)

        @pl.when(k == 0)
        def _():
            acc_ref[...] = jnp.zeros_like(acc_ref)

        acc_ref[...] += _dot_nt(a_ref[...], w_ref[...])

        @pl.when(jnp.logical_and(i == 0, k == 0))
        def _():
            dg_ref[...] = jnp.zeros_like(dg_ref)

        @pl.when(k == nk - 1)
        def _():
            gv = g_ref[...]

            def rows_step(c, dg):
                rows = pl.ds(pl.multiple_of(c * rc, rc), rc)
                dn = acc_ref[rows, :]
                xv = h_ref[rows, :]
                r = lax.rsqrt(jnp.mean(xv * xv, axis=-1, keepdims=True) + EPS)
                xh = xv * r
                dng = dn * gv
                dx = r * (dng - xh * jnp.mean(dng * xh, axis=-1, keepdims=True))
                out = dho_ref[rows, :] + dx
                dh_ref[rows, :] = out
                dhb_ref[rows, :] = out.astype(BF16)
                return dg + jnp.sum(dn * xh, axis=0, keepdims=True)

            dg_ref[...] += lax.fori_loop(0, tm // rc, rows_step, jnp.zeros((1, D), F32))

    return pl.pallas_call(
        body, name=name, grid=(S // tm, nk),
        in_specs=[
            pl.BlockSpec((None, tm, tk), lambda i, k: (lax.div(k, kpg), i, lax.rem(k, kpg))),
            pl.BlockSpec((None, D, tk), lambda i, k: (lax.div(k, kps), 0, lax.rem(k, kps))),
            pl.BlockSpec((tm, D), lambda i, k: (i, 0)),
            pl.BlockSpec((tm, D), lambda i, k: (i, 0)),
            pl.BlockSpec((1, D), lambda i, k: (0, 0)),
        ],
        out_specs=[
            pl.BlockSpec((tm, D), lambda i, k: (i, 0), pipeline_mode=pl.Buffered(1)),
            pl.BlockSpec((tm, D), lambda i, k: (i, 0), pipeline_mode=pl.Buffered(1)),
            pl.BlockSpec((1, D), lambda i, k: (0, 0)),
        ],
        out_shape=[jax.ShapeDtypeStruct((S, D), F32), jax.ShapeDtypeStruct((S, D), BF16),
                   jax.ShapeDtypeStruct((1, D), F32)],
        scratch_shapes=[pltpu.VMEM((tm, D), F32)],
        compiler_params=_cparams(("arbitrary", "arbitrary"), 56),
    )(dact, w, h_in, dh_out, g)


def _final_loss(h, g, target, *, name):
    S, D = h.shape
    tm = 256

    def body(h_ref, g_ref, t_ref, loss_ref, dh_ref, dhb_ref, dg_ref):
        i = pl.program_id(0)

        @pl.when(i == 0)
        def _():
            loss_ref[...] = jnp.zeros_like(loss_ref)
            dg_ref[...] = jnp.zeros_like(dg_ref)

        xv = h_ref[...]
        gv = g_ref[...]
        r = lax.rsqrt(jnp.mean(xv * xv, axis=-1, keepdims=True) + EPS)
        xh = xv * r
        e = xh * gv - t_ref[...]
        loss_ref[...] += 0.5 * jnp.sum(jnp.mean(e * e, axis=-1, keepdims=True))
        dy = e * (1.0 / D)
        dg_ref[...] += jnp.sum(dy * xh, axis=0, keepdims=True)
        dyg = dy * gv
        dx = r * (dyg - xh * jnp.mean(dyg * xh, axis=-1, keepdims=True))
        dh_ref[...] = dx
        dhb_ref[...] = dx.astype(BF16)

    return pl.pallas_call(
        body, name=name, grid=(S // tm,),
        in_specs=[pl.BlockSpec((tm, D), lambda i: (i, 0)), pl.BlockSpec((1, D), lambda i: (0, 0)),
                  pl.BlockSpec((tm, D), lambda i: (i, 0))],
        out_specs=[pl.BlockSpec((8, LANES), lambda i: (0, 0)), pl.BlockSpec((tm, D), lambda i: (i, 0)),
                   pl.BlockSpec((tm, D), lambda i: (i, 0)), pl.BlockSpec((1, D), lambda i: (0, 0))],
        out_shape=[jax.ShapeDtypeStruct((8, LANES), F32), jax.ShapeDtypeStruct((S, D), F32),
                   jax.ShapeDtypeStruct((S, D), BF16), jax.ShapeDtypeStruct((1, D), F32)],
        compiler_params=_cparams(("arbitrary",), 40),
    )(h, g, target)


POOL_HALO = 16
POOL_ROWS = 512


def _pool_window_mean_minus_token(ext, tok0, w):
    s = ext
    k = 1
    while k < w:
        s = s + pltpu.roll(s, k, 0)
        k *= 2
    win = s[POOL_HALO:, :]
    tok = tok0 + lax.broadcasted_iota(jnp.int32, (POOL_ROWS, 1), 0)
    cnt = jnp.minimum(tok + 1, w).astype(F32)
    return win / cnt - ext[POOL_HALO:, :], cnt


def _pool_fwd(u, w_pool, scale, *, name):
    S = u.shape[0]
    C = POOL_GROUP_DIM
    nsteps = S // POOL_ROWS

    def body(p_ref, w_ref, sc_ref, y_ref, xp_ref):
        xp_ref[0:POOL_HALO, :] = jnp.zeros((POOL_HALO, D_POOL), F32)
        xp_ref[POOL_HALO:, :] = p_ref[...]
        for gi, win in enumerate(POOL_WINDOWS):
            cols = slice(gi * C, (gi + 1) * C)

            def step(c, carry, cols=cols, win=win, gi=gi):
                r0 = pl.multiple_of(c * POOL_ROWS, POOL_ROWS)
                ext = xp_ref[pl.ds(r0, POOL_ROWS + POOL_HALO), cols]
                pooled, _ = _pool_window_mean_minus_token(ext, r0, win)
                y = _dot_nn(pooled.astype(BF16), w_ref[gi]) * sc_ref[:, cols]
                y_ref[pl.ds(r0, POOL_ROWS), cols] = y.astype(BF16)
                return carry

            lax.fori_loop(0, nsteps, step, 0)

    return pl.pallas_call(
        body, name=name, grid=(1,),
        in_specs=[pl.BlockSpec((S, D_POOL), lambda i: (0, 0)),
                  pl.BlockSpec((4, C, C), lambda i: (0, 0, 0)),
                  pl.BlockSpec((1, D_POOL), lambda i: (0, 0))],
        out_specs=pl.BlockSpec((S, D_POOL), lambda i: (0, 0)),
        out_shape=jax.ShapeDtypeStruct((S, D_POOL), BF16),
        scratch_shapes=[pltpu.VMEM((S + POOL_HALO, D_POOL), F32)],
        compiler_params=_cparams(("arbitrary",), 48),
    )(u, w_pool, scale)


def _pool_bwd(u, dcat, w_pool, scale, *, name):
    S = u.shape[0]
    C = POOL_GROUP_DIM
    nsteps = S // POOL_ROWS

    def body(p_ref, dy_ref, w_ref, sc_ref, dp_ref, dw_ref, dsc_ref, xp_ref, e_ref, neg_ref):
        xp_ref[0:POOL_HALO, :] = jnp.zeros((POOL_HALO, D_POOL), F32)
        xp_ref[POOL_HALO:, :] = p_ref[...]
        e_ref[S:, :] = jnp.zeros((POOL_HALO, C), F32)
        for gi, win in enumerate(POOL_WINDOWS):
            cols = slice(gi * C, (gi + 1) * C)

            def step_a(c, carry, cols=cols, win=win, gi=gi):
                dw, dsc = carry
                r0 = pl.multiple_of(c * POOL_ROWS, POOL_ROWS)
                ext = xp_ref[pl.ds(r0, POOL_ROWS + POOL_HALO), cols]
                pooled, cnt = _pool_window_mean_minus_token(ext, r0, win)
                pb = pooled.astype(BF16)
                wv = w_ref[gi]
                dy = dy_ref[pl.ds(r0, POOL_ROWS), cols]
                dsc = dsc + jnp.sum(dy * _dot_nn(pb, wv), axis=0, keepdims=True)
                dyp = (dy * sc_ref[:, cols]).astype(BF16)
                dw = dw + _dot_tn(pb, dyp)
                dpooled = _dot_nt(dyp, wv)
                e_ref[pl.ds(r0, POOL_ROWS), :] = dpooled / cnt
                neg_ref[pl.ds(r0, POOL_ROWS), :] = -dpooled
                return dw, dsc

            dw, dsc = lax.fori_loop(0, nsteps, step_a, (jnp.zeros((C, C), F32), jnp.zeros((1, C), F32)))
            dw_ref[gi] = dw
            dsc_ref[:, cols] = dsc

            def step_b(c, carry, cols=cols, win=win):
                r0 = pl.multiple_of(c * POOL_ROWS, POOL_ROWS)
                s = e_ref[pl.ds(r0, POOL_ROWS + POOL_HALO), :]
                n = POOL_ROWS + POOL_HALO
                k = 1
                while k < win:
                    s = s + pltpu.roll(s, n - k, 0)
                    k *= 2
                du = s[:POOL_ROWS, :] + neg_ref[pl.ds(r0, POOL_ROWS), :]
                dp_ref[pl.ds(r0, POOL_ROWS), cols] = du.astype(BF16)
                return carry

            lax.fori_loop(0, nsteps, step_b, 0)

    return pl.pallas_call(
        body, name=name, grid=(1,),
        in_specs=[pl.BlockSpec((S, D_POOL), lambda i: (0, 0)),
                  pl.BlockSpec((S, D_POOL), lambda i: (0, 0)),
                  pl.BlockSpec((4, C, C), lambda i: (0, 0, 0)),
                  pl.BlockSpec((1, D_POOL), lambda i: (0, 0))],
        out_specs=[pl.BlockSpec((S, D_POOL), lambda i: (0, 0)),
                   pl.BlockSpec((4, C, C), lambda i: (0, 0, 0)),
                   pl.BlockSpec((1, D_POOL), lambda i: (0, 0))],
        out_shape=[jax.ShapeDtypeStruct((S, D_POOL), BF16), jax.ShapeDtypeStruct((4, C, C), F32),
                   jax.ShapeDtypeStruct((1, D_POOL), F32)],
        scratch_shapes=[pltpu.VMEM((S + POOL_HALO, D_POOL), F32), pltpu.VMEM((S + POOL_HALO, C), F32),
                        pltpu.VMEM((S, C), F32)],
        compiler_params=_cparams(("arbitrary",), 56),
    )(u, dcat, w_pool, scale)


GLA_ROWS = 256
U_Q_BLK, U_K_BLK = 2, 3
U_V_BLK, U_G_BLK = 2, 3
U_R_BLK = 32


def _prefix_sum_rows(x):
    n = x.shape[0]
    row = lax.broadcasted_iota(jnp.int32, x.shape, 0)
    k = 1
    while k < n:
        x = x + jnp.where(row >= k, pltpu.roll(x, k, 0), 0.0)
        k *= 2
    return x


def _suffix_sum_rows(x):
    n = x.shape[0]
    row = lax.broadcasted_iota(jnp.int32, x.shape, 0)
    k = 1
    while k < n:
        x = x + jnp.where(row < n - k, pltpu.roll(x, n - k, 0), 0.0)
        k *= 2
    return x


def _log_sigmoid(z):
    return jnp.minimum(z, 0.0) - jnp.log(1.0 + jnp.exp(-jnp.abs(z)))


def _gla_chunk_terms(la_c, q_c, k_c):
    bc = _prefix_sum_rows(la_c)
    bl = jnp.sum(la_c, axis=0, keepdims=True)
    eb = jnp.exp(bc)
    enb = jnp.exp(-bc)
    etail = jnp.exp(bl - bc)
    qd = q_c * (GLA_DK ** -0.5) * eb
    ki = k_c * enb
    kt = k_c * etail
    d = jnp.exp(bl)
    return eb, enb, etail, qd, ki, kt, d


def _gla_fwd(u, y_pool, w_alpha, b_alpha, gnorm, *, name):
    S = u.shape[0]
    RB = GLA_ROWS
    ncc = RB // CHUNK
    H, DK, DV = GLA_HEADS, GLA_DK, GLA_DV

    def body(q_ref, k_ref, v_ref, go_ref, r_ref, yp_ref, wa_ref, ba_ref, gn_ref, cat_ref, o_ref, st_ref, state):
        i = pl.program_id(0)

        @pl.when(i == 0)
        def _():
            state[...] = jnp.zeros_like(state)

        cat_ref[:, :D_POOL] = yp_ref[...]
        y_ref = cat_ref.at[:, D_POOL:]

        z = _dot_nn(r_ref[...].astype(BF16), wa_ref[...]) + ba_ref[...]
        la = _log_sigmoid(z) / GATE_LOGIT_NORMALIZER
        ri = lax.broadcasted_iota(jnp.int32, (CHUNK, CHUNK), 0)
        ci = lax.broadcasted_iota(jnp.int32, (CHUNK, CHUNK), 1)
        tri = ri >= ci
        gn = gn_ref[...]
        for cc in range(ncc):
            rs = slice(cc * CHUNK, (cc + 1) * CHUNK)
            for h in range(H):
                ks = slice(h * DK, (h + 1) * DK)
                vs = slice(h * DV, (h + 1) * DV)
                _, _, _, qd, ki, kt, d = _gla_chunk_terms(la[rs, ks], q_ref[rs, ks], k_ref[rs, ks])
                qdb = qd.astype(BF16)
                vb = v_ref[rs, vs].astype(BF16)
                p = jnp.where(tri, _dot_nt(qdb, ki.astype(BF16)), 0.0)
                st = state[h]
                st_ref[cc, h] = st
                o = _dot_nn(p.astype(BF16), vb) + _dot_nt(qdb, st.astype(BF16))
                state[h] = st * d + _dot_tn(vb, kt.astype(BF16))
                o_ref[rs, vs] = o
                rinv = lax.rsqrt(jnp.mean(o * o, axis=-1, keepdims=True) + EPS)
                go = go_ref[rs, vs]
                y_ref[rs, vs] = (o * rinv * gn * (go * _sigmoid(go))).astype(BF16)

    nblk = S // RB
    return pl.pallas_call(
        body, name=name, grid=(nblk,),
        in_specs=[
            pl.BlockSpec((RB, GLA_DK_TOTAL), lambda i: (i, U_Q_BLK)),
            pl.BlockSpec((RB, GLA_DK_TOTAL), lambda i: (i, U_K_BLK)),
            pl.BlockSpec((RB, D_GLA), lambda i: (i, U_V_BLK)),
            pl.BlockSpec((RB, D_GLA), lambda i: (i, U_G_BLK)),
            pl.BlockSpec((RB, LANES), lambda i: (i, U_R_BLK)),
            pl.BlockSpec((RB, D_POOL), lambda i: (i, 0)),
            pl.BlockSpec((LANES, GLA_DK_TOTAL), lambda i: (0, 0)),
            pl.BlockSpec((1, GLA_DK_TOTAL), lambda i: (0, 0)),
            pl.BlockSpec((1, DV), lambda i: (0, 0)),
        ],
        out_specs=[
            pl.BlockSpec((RB, D_POOL + D_GLA), lambda i: (i, 0)),
            pl.BlockSpec((RB, D_GLA), lambda i: (i, 0)),
            pl.BlockSpec((ncc, H, DV, DK), lambda i: (i, 0, 0, 0)),
        ],
        out_shape=[jax.ShapeDtypeStruct((S, D_POOL + D_GLA), BF16), jax.ShapeDtypeStruct((S, D_GLA), F32),
                   jax.ShapeDtypeStruct((S // CHUNK, H, DV, DK), F32)],
        scratch_shapes=[pltpu.VMEM((H, DV, DK), F32)],
        compiler_params=_cparams(("arbitrary",), 32),
    )(u, u, u, u, u, y_pool, w_alpha, b_alpha, gnorm)


def _gla_bwd(u, o, states, dcat, dpool, w_alpha, b_alpha, gnorm, *, name):
    S = u.shape[0]
    RB = GLA_ROWS
    ncc = RB // CHUNK
    H, DK, DV = GLA_HEADS, GLA_DK, GLA_DV
    nblk = S // RB
    o_q, o_k = D_POOL, D_POOL + GLA_DK_TOTAL
    o_v, o_g, o_r = o_k + GLA_DK_TOTAL, o_k + GLA_DK_TOTAL + D_GLA, o_k + GLA_DK_TOTAL + 2 * D_GLA

    def body(q_ref, k_ref, v_ref, go_ref, r_ref, o_ref, st_ref, dy_ref, dpool_ref, wa_ref, ba_ref, gn_ref,
             du_ref, dwa_ref, dba_ref, dgn_ref, dstate, dz_ref):
        i = pl.program_id(0)

        @pl.when(i == 0)
        def _():
            dstate[...] = jnp.zeros_like(dstate)
            dwa_ref[...] = jnp.zeros_like(dwa_ref)
            dba_ref[...] = jnp.zeros_like(dba_ref)
            dgn_ref[...] = jnp.zeros_like(dgn_ref)

        du_ref[:, :o_q] = dpool_ref[...]
        dq_ref, dk_ref = du_ref.at[:, o_q:o_k], du_ref.at[:, o_k:o_v]
        dv_ref, dgo_ref, dr_ref = du_ref.at[:, o_v:o_g], du_ref.at[:, o_g:o_r], du_ref.at[:, o_r:]

        rb = r_ref[...].astype(BF16)
        wa = wa_ref[...]
        z = _dot_nn(rb, wa) + ba_ref[...]
        la = _log_sigmoid(z) / GATE_LOGIT_NORMALIZER
        ri = lax.broadcasted_iota(jnp.int32, (CHUNK, CHUNK), 0)
        ci = lax.broadcasted_iota(jnp.int32, (CHUNK, CHUNK), 1)
        tri = ri >= ci
        last_row = lax.broadcasted_iota(jnp.int32, (CHUNK, DK), 0) == CHUNK - 1
        gn = gn_ref[...]
        dgn = jnp.zeros((1, DV), F32)
        for cc in reversed(range(ncc)):
            rs = slice(cc * CHUNK, (cc + 1) * CHUNK)
            for h in range(H):
                ks = slice(h * DK, (h + 1) * DK)
                vs = slice(h * DV, (h + 1) * DV)
                eb, enb, etail, qd, ki, kt, d = _gla_chunk_terms(la[rs, ks], q_ref[rs, ks], k_ref[rs, ks])
                qdb, kib, ktb = qd.astype(BF16), ki.astype(BF16), kt.astype(BF16)
                vb = v_ref[rs, vs].astype(BF16)
                p = jnp.where(tri, _dot_nt(qdb, kib), 0.0)
                ov = o_ref[rs, vs]
                go = go_ref[rs, vs]
                dy = dy_ref[rs, vs]
                rinv = lax.rsqrt(jnp.mean(ov * ov, axis=-1, keepdims=True) + EPS)
                oh = ov * rinv
                sg = _sigmoid(go)
                dgo_ref[rs, vs] = (dy * (oh * gn) * (sg * (1.0 + go * (1.0 - sg)))).astype(BF16)
                don = dy * (go * sg)
                dgn = dgn + jnp.sum(don * oh, axis=0, keepdims=True)
                doh = don * gn
                do = rinv * (doh - oh * jnp.mean(doh * oh, axis=-1, keepdims=True))
                dob = do.astype(BF16)
                st = st_ref[cc, h]
                dst = dstate[h]
                stb, dstb = st.astype(BF16), dst.astype(BF16)
                dp = jnp.where(tri, _dot_nt(dob, vb), 0.0).astype(BF16)
                dv_ref[rs, vs] = (_dot_tn(p.astype(BF16), dob) + _dot_nt(ktb, dstb)).astype(BF16)
                dqd = _dot_nn(dp, kib) + _dot_nn(dob, stb)
                dki = _dot_tn(dp, qdb)
                dkt = _dot_nn(vb, dstb)
                dd = jnp.sum(dst * st, axis=0, keepdims=True)
                dstate[h] = dst * d + _dot_tn(dob, qdb)
                dq_ref[rs, ks] = (dqd * eb * (DK ** -0.5)).astype(BF16)
                dk_ref[rs, ks] = (dki * enb + dkt * etail).astype(BF16)
                dbl = jnp.sum(dkt * kt, axis=0, keepdims=True) + dd * d
                dbc = dqd * qd - dki * ki - dkt * kt
                dbc = dbc + jnp.where(last_row, dbl, 0.0)
                dla = _suffix_sum_rows(dbc)
                dz_ref[rs, ks] = dla * (1.0 / GATE_LOGIT_NORMALIZER) * (1.0 - _sigmoid(z[rs, ks]))
        dz = dz_ref[...]
        dzb = dz.astype(BF16)
        dr_ref[...] = _dot_nt(dzb, wa).astype(BF16)
        dwa_ref[...] += _dot_tn(rb, dzb)
        dba_ref[...] += jnp.sum(dz, axis=0, keepdims=True)
        dgn_ref[...] += dgn

    def rev(blk):
        return lambda i: (nblk - 1 - i, blk)

    return pl.pallas_call(
        body, name=name, grid=(nblk,),
        in_specs=[
            pl.BlockSpec((RB, GLA_DK_TOTAL), rev(U_Q_BLK)),
            pl.BlockSpec((RB, GLA_DK_TOTAL), rev(U_K_BLK)),
            pl.BlockSpec((RB, D_GLA), rev(U_V_BLK)),
            pl.BlockSpec((RB, D_GLA), rev(U_G_BLK)),
            pl.BlockSpec((RB, LANES), rev(U_R_BLK)),
            pl.BlockSpec((RB, D_GLA), rev(0)),
            pl.BlockSpec((ncc, H, DV, DK), lambda i: (nblk - 1 - i, 0, 0, 0)),
            pl.BlockSpec((RB, D_GLA), rev(1)),
            pl.BlockSpec((RB, D_POOL), rev(0)),
            pl.BlockSpec((LANES, GLA_DK_TOTAL), lambda i: (0, 0)),
            pl.BlockSpec((1, GLA_DK_TOTAL), lambda i: (0, 0)),
            pl.BlockSpec((1, DV), lambda i: (0, 0)),
        ],
        out_specs=[
            pl.BlockSpec((RB, D_IN_PAD), rev(0)),
            pl.BlockSpec((LANES, GLA_DK_TOTAL), lambda i: (0, 0)),
            pl.BlockSpec((1, GLA_DK_TOTAL), lambda i: (0, 0)),
            pl.BlockSpec((1, DV), lambda i: (0, 0)),
        ],
        out_shape=[
            jax.ShapeDtypeStruct((S, D_IN_PAD), BF16),
            jax.ShapeDtypeStruct((LANES, GLA_DK_TOTAL), F32), jax.ShapeDtypeStruct((1, GLA_DK_TOTAL), F32),
            jax.ShapeDtypeStruct((1, DV), F32),
        ],
        scratch_shapes=[pltpu.VMEM((H, DV, DK), F32), pltpu.VMEM((RB, GLA_DK_TOTAL), F32)],
        compiler_params=_cparams(("arbitrary",), 32),
    )(u, u, u, u, u, o, states, dcat, dpool, w_alpha, b_alpha, gnorm)


def _concat_shards(w4, width, *, name):
    ns, R, cs = w4.shape
    tr = 256

    def body(w_ref, o_ref):
        for t in range(ns):
            o_ref[:, t * cs:(t + 1) * cs] = w_ref[t]
        o_ref[:, ns * cs:] = jnp.zeros((tr, width - ns * cs), o_ref.dtype)

    return pl.pallas_call(
        body, name=name, grid=(R // tr,),
        in_specs=[pl.BlockSpec((ns, tr, cs), lambda i: (0, i, 0))],
        out_specs=pl.BlockSpec((tr, width), lambda i: (i, 0)),
        out_shape=jax.ShapeDtypeStruct((R, width), w4.dtype),
        compiler_params=_cparams(("parallel",), 32),
    )(w4)


def _split_shards(a, cs, *, name):
    R, width = a.shape
    tr = 256

    def body(a_ref, o_ref):
        for t in range(N_CHIPS):
            o_ref[t] = a_ref[:, t * cs:(t + 1) * cs]

    return pl.pallas_call(
        body, name=name, grid=(R // tr,),
        in_specs=[pl.BlockSpec((tr, width), lambda i: (i, 0))],
        out_specs=pl.BlockSpec((N_CHIPS, tr, cs), lambda i: (0, i, 0)),
        out_shape=jax.ShapeDtypeStruct((N_CHIPS, R, cs), a.dtype),
        compiler_params=_cparams(("parallel",), 32),
    )(a)


def _row_tile(rows, cols, itemsize, budget=2 * 1024 * 1024):
    if rows * cols * itemsize <= budget or rows % 16:
        return rows
    best = 16
    for t in range(16, rows + 1, 16):
        if rows % t == 0 and t * cols * itemsize <= budget:
            best = t
    return best


def _adamw(w, g, m, v, after, *, name):
    R, C = w.shape
    tr = _row_tile(R, C, 4)

    def body(w_ref, g_ref, m_ref, v_ref, after_ref, go_ref, d_ref, nm_ref, nv_ref):
        gv = g_ref[...]
        go_ref[...] = gv
        mn = ADAM_B1 * m_ref[...] + (1.0 - ADAM_B1) * gv
        vn = ADAM_B2 * v_ref[...] + (1.0 - ADAM_B2) * jnp.square(gv)
        m_hat = mn / (1.0 - ADAM_B1 ** ADAM_STEP)
        v_hat = vn / (1.0 - ADAM_B2 ** ADAM_STEP)
        d_ref[...] = -ADAM_LR * (m_hat / (jnp.sqrt(v_hat) + ADAM_EPS) + ADAM_WD * w_ref[...])
        nm_ref[...] = mn
        nv_ref[...] = vn

    spec = pl.BlockSpec((tr, C), lambda i: (i, 0))
    shp = jax.ShapeDtypeStruct((R, C), F32)
    return pl.pallas_call(
        body, name=name, grid=(R // tr,), in_specs=[spec] * 4 + [pl.BlockSpec(memory_space=pl.ANY)],
        out_specs=[spec] * 4, out_shape=[shp] * 4,
        compiler_params=_cparams(("parallel",), 48),
    )(w, g, m, v, after)


def _pair_sum(g4, recv, c_idx, *, name):
    ns, _, R2, C = g4.shape
    tr = _row_tile(R2, C, 2)

    def body(c_ref, g_ref, r_ref, o_ref):
        o_ref[...] = (g_ref[...].astype(F32) + r_ref[...].astype(F32)).astype(BF16)

    return pl.pallas_call(
        body, name=name,
        grid_spec=pltpu.PrefetchScalarGridSpec(
            num_scalar_prefetch=1, grid=(ns, R2 // tr),
            in_specs=[pl.BlockSpec((None, None, tr, C), lambda s, i, c: (s, c[0], i, 0)),
                      pl.BlockSpec((None, tr, C), lambda s, i, c: (s, i, 0))],
            out_specs=pl.BlockSpec((None, tr, C), lambda s, i, c: (s, i, 0)),
        ),
        out_shape=jax.ShapeDtypeStruct((ns, R2, C), BF16),
        compiler_params=_cparams(("parallel", "parallel"), 32),
    )(c_idx, g4, recv)


def _chip_sum(part, recv, sc_idx, *, name):
    _, R2, C = part.shape
    tr = _row_tile(R2, C, 4)
    nblk = R2 // tr

    def body(s_ref, p_ref, r_ref, o_ref):
        acc = p_ref[...].astype(F32)
        for j in range(N_CHIPS - 1):
            acc = acc + r_ref[j].astype(F32)
        o_ref[...] = acc

    return pl.pallas_call(
        body, name=name,
        grid_spec=pltpu.PrefetchScalarGridSpec(
            num_scalar_prefetch=1, grid=(nblk,),
            in_specs=[pl.BlockSpec((None, tr, C), lambda i, s: (s[0], i, 0)),
                      pl.BlockSpec((N_CHIPS - 1, tr, C), lambda i, s: (0, i, 0))],
            out_specs=pl.BlockSpec((tr, C), lambda i, s: (s[1] * nblk + i, 0)),
        ),
        out_shape=jax.ShapeDtypeStruct((2 * R2, C), F32),
        compiler_params=_cparams(("parallel",), 32),
    )(sc_idx, part, recv)


def _cast_into_slot(w, sc_idx, dtype, after, *, name):
    R, C = w.shape
    tr = _row_tile(R, C, 4)

    def body(s_ref, w_ref, after_ref, o_ref):
        o_ref[...] = w_ref[...].astype(dtype)

    return pl.pallas_call(
        body, name=name,
        grid_spec=pltpu.PrefetchScalarGridSpec(
            num_scalar_prefetch=1, grid=(R // tr,),
            in_specs=[pl.BlockSpec((tr, C), lambda i, s: (i, 0)), pl.BlockSpec(memory_space=pl.ANY)],
            out_specs=pl.BlockSpec((None, tr, C), lambda i, s: (s[0], i, 0)),
        ),
        out_shape=jax.ShapeDtypeStruct((N_CHIPS, R, C), dtype),
        compiler_params=_cparams(("parallel",), 32),
    )(sc_idx, w, after)


def _slab_sum(slabs, *, name):
    n, M, C = slabs.shape

    def body(x_ref, o_ref):
        acc = x_ref[0]
        for d in range(1, n):
            acc = acc + x_ref[d]
        o_ref[...] = acc

    return pl.pallas_call(
        body, name=name, out_shape=jax.ShapeDtypeStruct((M, C), F32),
    )(slabs)


def _mesh_position():
    x, y, c = lax.axis_index("x"), lax.axis_index("y"), lax.axis_index("c")
    other_chips = [(1 - x, y), (x, 1 - y), (1 - x, 1 - y)]
    return x, y, c, other_chips


ANY = pl.BlockSpec(memory_space=pl.ANY)


HBM = pl.BlockSpec(memory_space=pltpu.HBM)
SEM = pl.BlockSpec(memory_space=pltpu.SEMAPHORE)
SPLIT_COPY = pltpu.CompilerParams(has_side_effects=pltpu.SideEffectType.DATAFLOW_SIDE_EFFECTING)
TOKEN = jax.ShapeDtypeStruct((8, LANES), F32)


def _in_hbm(a):
    return pltpu.with_memory_space_constraint(a, pltpu.HBM)


def _half_rows(ref, slot, half):
    hr = ref.shape[1] // 2
    return ref.at[slot, pl.ds(half * hr, hr), :]


GATHER_COPIES = {"direct": 3, "first_hop": 2, "second_hop": 1}


def _gather_routes(kind):
    x, y, c, chips = _mesh_position()
    me = 2 * x + y
    first = (x + (1 - c) * (1 - 2 * x), y + c * (1 - 2 * y))
    second = (x + c * (1 - 2 * x), y + (1 - c) * (1 - 2 * y))
    if kind == "direct":
        return [(me, (*p, c)) for p in chips], [2 * p[0] + p[1] for p in chips]
    if kind == "first_hop":
        return [(me, (*first, c)), (me, (*second, c))], [2 * first[0] + first[1], 2 * second[0] + second[1]]
    assert kind == "second_hop"
    return [(2 * first[0] + first[1], (*second, c))], [2 * (1 - x) + (1 - y)]


def _gather_ici_start(groups, kinds, *, name):
    flat = [b for g in groups for b in g]
    K, G = len(flat), len(groups)

    def body(*refs):
        ins, sems, token = refs[:K], refs[K:K + 2 * G], refs[-1]
        c = lax.axis_index("c")
        k = 0
        for gi, (g, kind) in enumerate(zip(groups, kinds)):
            sends, _ = _gather_routes(kind)
            for n in range(len(g)):
                for j, (slot, to) in enumerate(sends):
                    part = _half_rows(ins[k], slot, c)
                    pltpu.make_async_remote_copy(
                        src_ref=part, dst_ref=part, send_sem=sems[2 * gi].at[n * len(sends) + j],
                        recv_sem=sems[2 * gi + 1].at[n * len(sends) + j],
                        device_id=to, device_id_type=MESH).start()
                k += 1
        token[...] = jnp.zeros_like(token)

    sem_shapes = []
    for g, kind in zip(groups, kinds):
        sem_shapes += [pltpu.SemaphoreType.DMA((len(g) * GATHER_COPIES[kind],))] * 2
    out = pl.pallas_call(
        body, name=name,
        in_specs=[HBM] * K,
        out_specs=[SEM] * (2 * G) + [HBM] * K + [pl.BlockSpec(memory_space=pltpu.VMEM)],
        out_shape=sem_shapes + [pltpu.HBM(b.shape, b.dtype) for b in flat] + [TOKEN],
        input_output_aliases={k: 2 * G + k for k in range(K)},
        compiler_params=SPLIT_COPY,
    )(*[_in_hbm(b) for b in flat])
    handles, k = [], 2 * G
    for gi, (g, kind) in enumerate(zip(groups, kinds)):
        handles.append((out[2 * gi], out[2 * gi + 1], list(out[k:k + len(g)]), kind))
        k += len(g)
    return handles, out[-1]


def _gather_ici_wait(handle, after, *, name):
    send, recv, bufs, kind = handle
    n = len(bufs)

    def body(*refs):
        ins, send_ref, recv_ref = refs[:n], refs[n], refs[n + 1]
        c = lax.axis_index("c")
        sends, arrivals = _gather_routes(kind)
        for k in range(n):
            for j, ((slot, to), landed) in enumerate(zip(sends, arrivals)):
                cp = pltpu.make_async_remote_copy(
                    src_ref=_half_rows(ins[k], slot, c), dst_ref=_half_rows(ins[k], landed, c),
                    send_sem=send_ref.at[k * len(sends) + j], recv_sem=recv_ref.at[k * len(sends) + j],
                    device_id=to, device_id_type=MESH)
                cp.wait_send()
                cp.wait_recv()

    return pl.pallas_call(
        body, name=name,
        in_specs=[HBM] * n + [SEM, SEM, ANY], out_specs=[HBM] * n,
        out_shape=[pltpu.HBM(b.shape, b.dtype) for b in bufs],
        input_output_aliases={k: k for k in range(n)},
        compiler_params=SPLIT_COPY,
    )(*bufs, send, recv, after)


def _forward_halves(bufs, *, name):
    K = len(bufs)
    per = N_CHIPS - 1

    def body(*refs):
        outs = refs[K:2 * K]
        send_sems, recv_sems = refs[2 * K:]
        x, y, c, chips = _mesh_position()
        copies = []
        for k in range(K):
            for j, chip in enumerate(chips):
                got = _half_rows(outs[k], 2 * chip[0] + chip[1], c)
                cp = pltpu.make_async_remote_copy(
                    src_ref=got, dst_ref=got, send_sem=send_sems.at[k * per + j], recv_sem=recv_sems.at[k * per + j],
                    device_id=(x, y, 1 - c), device_id_type=MESH)
                cp.start()
                copies.append(cp)
        for cp in copies:
            cp.wait()

    return pl.pallas_call(
        body, name=name,
        in_specs=[ANY] * K, out_specs=[ANY] * K,
        out_shape=[jax.ShapeDtypeStruct(a.shape, a.dtype) for a in bufs],
        input_output_aliases={k: k for k in range(K)},
        scratch_shapes=[pltpu.SemaphoreType.DMA((K * per,)), pltpu.SemaphoreType.DMA((K * per,))],
    )(*bufs)


def _chip_exchange_copies(srcs, lands, send_sems, recv_sems):
    x, y, c, chips = _mesh_position()
    per = N_CHIPS - 1
    return [pltpu.make_async_remote_copy(
        src_ref=srcs[k].at[2 * chip[0] + chip[1]], dst_ref=lands[k].at[j],
        send_sem=send_sems.at[k * per + j], recv_sem=recv_sems.at[k * per + j],
        device_id=(*chip, c), device_id_type=MESH) for k in range(len(srcs)) for j, chip in enumerate(chips)]


def _sibling_swap_copies(srcs, lands, send_sems, recv_sems):
    x, y, c, _ = _mesh_position()
    return [pltpu.make_async_remote_copy(
        src_ref=srcs[k].at[pl.ds(0, srcs[k].shape[0]), 1 - c], dst_ref=lands[k],
        send_sem=send_sems.at[k], recv_sem=recv_sems.at[k],
        device_id=(x, y, 1 - c), device_id_type=MESH) for k in range(len(srcs))]


def _sibling_send_copies(srcs, lands, send_sems, recv_sems):
    x, y, c, _ = _mesh_position()
    return [pltpu.make_async_remote_copy(
        src_ref=srcs[k], dst_ref=lands[k], send_sem=send_sems.at[k], recv_sem=recv_sems.at[k],
        device_id=(x, y, 1 - c), device_id_type=MESH) for k in range(len(srcs))]


def _split_copy_start(srcs, land_shapes, n_sems, copies, *, name):
    K = len(srcs)

    def body(*refs):
        for cp in copies(refs[:K], refs[K:2 * K], refs[2 * K], refs[2 * K + 1]):
            cp.start()
        refs[-1][...] = jnp.zeros_like(refs[-1])

    out = pl.pallas_call(
        body, name=name,
        in_specs=[HBM] * (2 * K),
        out_specs=[SEM, SEM] + [HBM] * (2 * K) + [pl.BlockSpec(memory_space=pltpu.VMEM)],
        out_shape=[pltpu.SemaphoreType.DMA((n_sems,))] * 2
        + [pltpu.HBM(a.shape, a.dtype) for a in srcs]
        + [pltpu.HBM(s, a.dtype) for s, a in zip(land_shapes, srcs)] + [TOKEN],
        input_output_aliases={k: 2 + k for k in range(2 * K)},
        compiler_params=SPLIT_COPY,
    )(*[_in_hbm(a) for a in srcs], *[_in_hbm(lax.empty(s, a.dtype)) for s, a in zip(land_shapes, srcs)])
    return (out[0], out[1], list(out[2:2 + K]), list(out[2 + K:2 + 2 * K])), out[-1]


def _split_copy_wait(handle, copies, after, *, name):
    send, recv, srcs, lands = handle
    K = len(srcs)

    def body(*refs):
        for cp in copies(refs[:K], refs[K:2 * K], refs[2 * K], refs[2 * K + 1]):
            cp.wait_send()
            cp.wait_recv()

    out = pl.pallas_call(
        body, name=name,
        in_specs=[HBM] * (2 * K) + [SEM, SEM, ANY], out_specs=[HBM] * (2 * K),
        out_shape=[pltpu.HBM(a.shape, a.dtype) for a in srcs] + [pltpu.HBM(a.shape, a.dtype) for a in lands],
        input_output_aliases={k: k for k in range(2 * K)},
        compiler_params=SPLIT_COPY,
    )(*srcs, *lands, send, recv, after)
    return list(out[:K]), list(out[K:])


def _join_copies(bufs, send_sems, recv_sems):
    x, y, c, _ = _mesh_position()
    copies = []
    for k, buf in enumerate(bufs):
        r2 = buf.shape[0] // 2
        mine = buf.at[pl.ds(c * r2, r2), :]
        copies.append(pltpu.make_async_remote_copy(
            src_ref=mine, dst_ref=mine, send_sem=send_sems.at[k], recv_sem=recv_sems.at[k],
            device_id=(x, y, 1 - c), device_id_type=MESH))
    return copies


def _join_start(bufs, *, name):
    K = len(bufs)

    def body(*refs):
        for cp in _join_copies(refs[:K], refs[K], refs[K + 1]):
            cp.start()
        refs[-1][...] = jnp.zeros_like(refs[-1])

    out = pl.pallas_call(
        body, name=name,
        in_specs=[HBM] * K,
        out_specs=[SEM, SEM] + [HBM] * K + [pl.BlockSpec(memory_space=pltpu.VMEM)],
        out_shape=[pltpu.SemaphoreType.DMA((K,))] * 2 + [pltpu.HBM(a.shape, a.dtype) for a in bufs] + [TOKEN],
        input_output_aliases={k: 2 + k for k in range(K)},
        compiler_params=SPLIT_COPY,
    )(*[_in_hbm(a) for a in bufs])
    return (out[0], out[1], list(out[2:2 + K])), out[-1]


def _join_wait(handle, after, *, name):
    send, recv, bufs = handle
    K = len(bufs)

    def body(*refs):
        for cp in _join_copies(refs[:K], refs[K], refs[K + 1]):
            cp.wait_send()
            cp.wait_recv()

    return pl.pallas_call(
        body, name=name,
        in_specs=[HBM] * K + [SEM, SEM, ANY], out_specs=[HBM] * K,
        out_shape=[pltpu.HBM(a.shape, a.dtype) for a in bufs],
        input_output_aliases={k: k for k in range(K)},
        compiler_params=SPLIT_COPY,
    )(*bufs, send, recv, after)


def _all_gather_slab(slab):
    m_per, n = slab.shape

    def body(x_ref, out_ref, send_sems, recv_sems, local_sem):
        x, y, c, chips = _mesh_position()
        me, sibling = (x, y, c), (x, y, 1 - c)

        def rows(px, py, pc):
            return out_ref.at[pl.ds((4 * px + 2 * py + pc) * m_per, m_per), :]

        def copy(k, block, to, src=None):
            return pltpu.make_async_remote_copy(
                src_ref=rows(*block) if src is None else src, dst_ref=rows(*block),
                send_sem=send_sems.at[k], recv_sem=recv_sems.at[k], device_id=to, device_id_type=MESH)

        mine = pltpu.make_async_copy(x_ref, rows(*me), local_sem)
        mine.start()
        first = [copy(0, me, sibling, src=x_ref)]
        first += [copy(1 + j, me, (*chip, c), src=x_ref) for j, chip in enumerate(chips)]
        for cp in first:
            cp.start()
        passed = [copy(4 + j, (*chip, c), sibling) for j, chip in enumerate(chips)]
        for j, chip in enumerate(chips):
            copy(1 + j, (*chip, c), me).wait_recv()
            passed[j].start()
        copy(0, sibling, me).wait_recv()
        for j, chip in enumerate(chips):
            copy(4 + j, (*chip, 1 - c), me).wait_recv()
        for cp in first + passed:
            cp.wait_send()
        mine.wait()

    return pl.pallas_call(
        body, name="gather_small_grads",
        out_shape=jax.ShapeDtypeStruct((N_DEV * m_per, n), slab.dtype),
        in_specs=[pl.BlockSpec(memory_space=pltpu.VMEM)],
        out_specs=pl.BlockSpec(memory_space=pltpu.VMEM),
        scratch_shapes=[pltpu.SemaphoreType.DMA((7,)), pltpu.SemaphoreType.DMA((7,)), pltpu.SemaphoreType.DMA],
    )(slab)


def _ffn_dw_out(dhb, a, tag):
    S, F = a.shape
    D = dhb.shape[1]
    to = 512
    return _mm_tn(
        a, dhb, grid=(F // to,),
        a_spec=pl.BlockSpec((S, to), lambda j: (0, j)), b_spec=pl.BlockSpec((S, D), lambda j: (0, 0)),
        out_spec=pl.BlockSpec((to, D), lambda j: (j, 0)), out_shape=jax.ShapeDtypeStruct((F, D), BF16),
        scale=0.5, name=f"{tag}_dw_out").reshape(N_CHIPS, F // N_CHIPS, D)


def _ffn_dw_in_half(nt, dgu, half, add, w_in_shape, *, name):
    ns, D, cs = w_in_shape
    S = nt.shape[1]
    F = dgu.shape[2]
    ti, tr = SHARD_TILE, D // 2
    per_g, per_s = F // ti, cs // ti

    def body(h_ref, a_ref, b_ref, *rest):
        acc = _dot_nn(a_ref[...], b_ref[...])
        if add is not None:
            acc = acc + rest[0][...].astype(F32)
        rest[-1][...] = acc.astype(BF16)

    out_spec = pl.BlockSpec((None, tr, ti), lambda j, h: (lax.div(j, per_s), 0, lax.rem(j, per_s)))
    extra = [] if add is None else [add]
    return pl.pallas_call(
        body, name=name,
        grid_spec=pltpu.PrefetchScalarGridSpec(
            num_scalar_prefetch=1, grid=(2 * F // ti,),
            in_specs=[pl.BlockSpec((tr, S), lambda j, h: (h[0], 0)),
                      pl.BlockSpec((None, S, ti), lambda j, h: (lax.div(j, per_g), 0, lax.rem(j, per_g)))]
            + [out_spec] * len(extra),
            out_specs=out_spec,
        ),
        out_shape=jax.ShapeDtypeStruct((ns, tr, cs), BF16),
        compiler_params=_cparams(("parallel",), 56),
    )(half, nt, dgu, *extra)


def kernel(x, ffn1_norm, ffn1_w_in, ffn1_w_out, mix_norm, w_in_mix, w_pool, pool_scale, w_alpha, b_alpha, gla_norm, w_out_mix, ffn2_norm, ffn2_w_in, ffn2_w_out, final_norm, loss_target, m_ffn1_norm, m_ffn1_w_in, m_ffn1_w_out, m_mix_norm, m_w_in_mix, m_w_pool, m_pool_scale, m_w_alpha, m_b_alpha, m_gla_norm, m_w_out_mix, m_ffn2_norm, m_ffn2_w_in, m_ffn2_w_out, m_final_norm, v_ffn1_norm, v_ffn1_w_in, v_ffn1_w_out, v_mix_norm, v_w_in_mix, v_w_pool, v_pool_scale, v_w_alpha, v_b_alpha, v_gla_norm, v_w_out_mix, v_ffn2_norm, v_ffn2_w_in, v_ffn2_w_out, v_final_norm):
    names = ["ffn1_norm", "ffn1_w_in", "ffn1_w_out", "mix_norm", "w_in_mix", "w_pool", "pool_scale", "w_alpha",
             "b_alpha", "gla_norm", "w_out_mix", "ffn2_norm", "ffn2_w_in", "ffn2_w_out", "final_norm"]
    weights = dict(zip(names, [ffn1_norm, ffn1_w_in, ffn1_w_out, mix_norm, w_in_mix, w_pool, pool_scale, w_alpha,
                               b_alpha, gla_norm, w_out_mix, ffn2_norm, ffn2_w_in, ffn2_w_out, final_norm]))
    moms = dict(zip(names, [m_ffn1_norm, m_ffn1_w_in, m_ffn1_w_out, m_mix_norm, m_w_in_mix, m_w_pool, m_pool_scale,
                            m_w_alpha, m_b_alpha, m_gla_norm, m_w_out_mix, m_ffn2_norm, m_ffn2_w_in, m_ffn2_w_out,
                            m_final_norm]))
    vels = dict(zip(names, [v_ffn1_norm, v_ffn1_w_in, v_ffn1_w_out, v_mix_norm, v_w_in_mix, v_w_pool, v_pool_scale,
                            v_w_alpha, v_b_alpha, v_gla_norm, v_w_out_mix, v_ffn2_norm, v_ffn2_w_in, v_ffn2_w_out,
                            v_final_norm]))
    xi, yi, ci = lax.axis_index("x"), lax.axis_index("y"), lax.axis_index("c")
    chip = 2 * xi + yi
    c_idx = jnp.reshape(ci, (1,)).astype(jnp.int32)
    sc_idx = jnp.stack([chip, ci]).astype(jnp.int32)

    def flat2d(a):
        return a.reshape(-1, a.shape[-1])

    ex = _Exchanges(sc_idx, c_idx)

    def cast(n, after):
        return _cast_into_slot(flat2d(weights[n]), sc_idx, F32 if n == "w_alpha" else BF16, after, name=f"cast_{n}")

    groups = _Exchanges.GATHER_GROUPS
    w1_in = groups[0][0]
    tok = ex.start_gather({w1_in: cast(w1_in, sc_idx)}, groups[:1], ("first_hop",), name="gather_ici_start_first")
    bufs, last = {}, tok
    for g in groups[1:]:
        for n in g:
            bufs[n] = last = cast(n, last)
    bufs[w1_in] = ex.arrived(w1_in, after=last)[w1_in]
    tok2 = ex.start_gather(bufs, groups, ("second_hop",) + ("direct",) * (len(groups) - 1),
                           name="gather_ici_start_rest")
    early = [flat2d(moms["w_in_mix"]), flat2d(vels["w_in_mix"])]
    small_params = dict(g1=ffn1_norm, gm=mix_norm, g2=ffn2_norm, gf=final_norm.reshape(1, D_MODEL),
                        pool_scale=pool_scale, b_alpha=b_alpha, gla_norm=gla_norm, early=early, core=c_idx)
    loss_blk, dx, small = _forward_backward(x[0], loss_target[0], ex, tok[0, 0] + tok2[0, 0], small_params)

    outs = {}

    def update(n, g, after):
        w = weights[n]
        w2 = flat2d(w) if w.ndim > 1 else w.reshape(1, -1)
        go, d, nm, nv = _adamw(w2, g.reshape(w2.shape), moms[n].reshape(w2.shape), vels[n].reshape(w2.shape), after,
                               name=f"adamw_{n}")
        outs[n] = (go.reshape(w.shape), d.reshape(w.shape), nm.reshape(w.shape), nv.reshape(w.shape))
        return nv

    tags = _Exchanges.REDUCE_ORDER
    last = ex.finish_exchange(tags[0], after=dx)
    for prev, tag in zip(tags, tags[1:]):
        last = ex.finish_exchange(tag, after=last)
        for n, g in ex.reduced(prev, after=last).items():
            last = update(n, g, last)

    grads = {}
    small_names = ["ffn1_norm", "mix_norm", "ffn2_norm", "final_norm", "pool_scale", "b_alpha", "gla_norm", "w_alpha",
                   "loss"]
    small = small + [loss_blk[0:1]]
    rows = [a.size // LANES for a in small]
    slab = jnp.concatenate([a.reshape(-1, LANES) for a in small], axis=0)
    pad = -slab.shape[0] % 8
    slab = jnp.pad(slab, ((0, pad), (0, 0)))
    gathered = _all_gather_slab(slab).reshape(N_DEV, slab.shape[0], LANES)
    total = _slab_sum(gathered, name="sum_small_grads")
    off = 0
    for n, a, r in zip(small_names, small, rows):
        grads[n] = total[off:off + r].reshape(a.shape)
        off += r
    grads["w_alpha"] = lax.dynamic_slice_in_dim(grads["w_alpha"], chip * (GLA_DK_TOTAL // N_CHIPS),
                                                GLA_DK_TOTAL // N_CHIPS, axis=1)

    loss = grads.pop("loss")[0, 0]
    for n in small_names[:-1]:
        last = update(n, grads[n], last)
    for n, g in ex.reduced(tags[-1], after=last).items():
        last = update(n, g, last)
    return (loss, dx[None], *[outs[n][0] for n in names], *[outs[n][1] for n in names],
            *[outs[n][2] for n in names], *[outs[n][3] for n in names])


class _Exchanges:
    GATHER_GROUPS = (("ffn1_w_in",), ("ffn1_w_out",), ("w_in_mix", "w_pool", "w_alpha"), ("w_out_mix",),
                     ("ffn2_w_in",), ("ffn2_w_out",))
    REDUCE_ORDER = ("ffn2_out", "ffn2_in", "mix", "ffn1_out", "ffn1_in")

    def __init__(self, sc_idx, c_idx):
        self.sc_idx, self.c_idx = sc_idx, c_idx
        self._gathers, self._swaps, self._sends, self._reduces, self._joins = {}, {}, {}, {}, {}

    def start_gather(self, bufs, groups, kinds, *, name):
        handles, token = _gather_ici_start([[bufs[n] for n in g] for g in groups], kinds, name=name)
        for g, h in zip(groups, handles):
            self._gathers[g[0]] = (g, h)
        return token

    def arrived(self, first, after):
        names, handle = self._gathers.pop(first)
        return dict(zip(names, _gather_ici_wait(handle, after, name=f"gather_ici_wait_{first}_{handle[3]}")))

    def gathered(self, first, after):
        got = self.arrived(first, after)
        return dict(zip(got, _forward_halves(list(got.values()), name=f"gather_forward_{first}")))

    def begin_reduce(self, tag, full):
        g4 = [a.reshape(N_CHIPS, 2, a.shape[1] // 2, a.shape[2]) for a in full.values()]
        lands = [(a.shape[0],) + a.shape[2:] for a in g4]
        handle, token = _split_copy_start(g4, lands, len(g4), _sibling_swap_copies, name=f"swap_start_{tag}")
        self._swaps[tag] = (list(full), handle)
        return token

    def start_reduce(self, tag, after):
        names, handle = self._swaps.pop(tag)
        g4, from_sibling = _split_copy_wait(handle, _sibling_swap_copies, after, name=f"swap_wait_{tag}")
        pair = [_pair_sum(a, b, self.c_idx, name=f"pair_sum_{n}") for n, a, b in zip(names, g4, from_sibling)]
        return self.start_exchange(tag, names, pair)

    def start_exchange(self, tag, names, pair):
        lands = [(N_CHIPS - 1,) + a.shape[1:] for a in pair]
        handle, token = _split_copy_start(pair, lands, len(pair) * (N_CHIPS - 1), _chip_exchange_copies,
                                          name=f"exchange_start_{tag}")
        self._reduces[tag] = (names, handle)
        return token

    def send_to_sibling(self, tag, arrays):
        handle, token = _split_copy_start(arrays, [a.shape for a in arrays], len(arrays), _sibling_send_copies,
                                          name=f"send_start_{tag}")
        self._sends[tag] = handle
        return token

    def from_sibling(self, tag, after):
        return _split_copy_wait(self._sends.pop(tag), _sibling_send_copies, after, name=f"send_wait_{tag}")[1]

    def finish_exchange(self, tag, after):
        names, handle = self._reduces.pop(tag)
        pair, lands = _split_copy_wait(handle, _chip_exchange_copies, after, name=f"exchange_wait_{tag}")
        halves = [_chip_sum(a, b, self.sc_idx, name=f"chip_sum_{n}") for n, a, b in zip(names, pair, lands)]
        handle, token = _join_start(halves, name=f"join_start_{tag}")
        self._joins[tag] = (names, handle)
        return token

    def reduced(self, tag, after):
        names, handle = self._joins.pop(tag)
        return dict(zip(names, _join_wait(handle, after, name=f"join_wait_{tag}")))


def _forward_backward(h0, target, ex, started, sp):
    g1, gm, g2, gf = sp["g1"], sp["gm"], sp["g2"], sp["gf"]
    pool_scale, b_alpha, gla_norm = sp["pool_scale"], sp["b_alpha"], sp["gla_norm"]
    cs_mix = D_IN // N_CHIPS

    n1, n1t = _rms_fwd(h0, g1 + started, sp["early"], name="ffn1_norm")
    w1_in = ex.gathered("ffn1_w_in", after=n1)["ffn1_w_in"]
    fa1, a1 = _ffn_up(n1, w1_in, [], name="ffn1_up")
    w1_out = ex.gathered("ffn1_w_out", after=a1)["ffn1_w_out"].reshape(D_FF, D_MODEL)
    h1 = _mm_nn(a1, w1_out, h0, 0.5, tm=512, tn=D_MODEL, tk=SHARD_TILE, name="ffn1_down")
    n_mix, n_mixt = _rms_fwd(h1, gm, [], name="mix_norm")
    gw = ex.gathered("w_in_mix", after=n_mix)
    w_mix = _concat_shards(gw["w_in_mix"], D_IN_PAD, name="w_mix_concat")[None]
    wp = gw["w_pool"].reshape(N_CHIPS, 4, POOL_GROUP_DIM // N_CHIPS, POOL_GROUP_DIM)
    wp = wp.transpose(1, 0, 2, 3).reshape(4, POOL_GROUP_DIM, POOL_GROUP_DIM)
    wa = gw["w_alpha"].transpose(1, 0, 2).reshape(GLA_GATE_RANK, GLA_DK_TOTAL)
    wa = jnp.pad(wa, ((0, LANES - GLA_GATE_RANK), (0, 0))).astype(BF16)
    u = _mm_nn(n_mix, w_mix[0], None, 1.0, tm=512, tn=D_IN_PAD, tk=D_MODEL, name="mix_in")
    y_pool = _pool_fwd(u, wp, pool_scale, name="pool_fwd")
    cat, o_gla, states = _gla_fwd(u, y_pool, wa, b_alpha, gla_norm, name="gla_fwd")
    w_omix = ex.gathered("w_out_mix", after=cat)["w_out_mix"].reshape(D_MODEL, D_MODEL)
    h2 = _mm_nn(cat, w_omix, h1, 1.0, tm=512, tn=D_MODEL, tk=1024, name="mix_out")
    n3, n3t = _rms_fwd(h2, g2, [], name="ffn2_norm")
    w2_in = ex.gathered("ffn2_w_in", after=n3)["ffn2_w_in"]
    fa2, a2 = _ffn_up(n3, w2_in, [], name="ffn2_up")
    w2_out = ex.gathered("ffn2_w_out", after=a2)["ffn2_w_out"].reshape(D_FF, D_MODEL)
    h3 = _mm_nn(a2, w2_out, h2, 0.5, tm=512, tn=D_MODEL, tk=SHARD_TILE, name="ffn2_down")
    loss_blk, dh3, dh3b, d_gf = _final_loss(h3, gf, target, name="final_loss")

    core = sp["core"]
    tok = ex.begin_reduce("ffn2_out", {"ffn2_w_out": _ffn_dw_out(dh3b, a2, "ffn2")})
    dgu2 = _ffn_bwd_act(dh3b, w2_out, fa2, tok, name="ffn2_bwd_act")
    theirs = _ffn_dw_in_half(n3t, dgu2, 1 - core, None, w2_in.shape, name="ffn2_dw_in_other")
    tok = ex.send_to_sibling("ffn2_in", [theirs])
    dh2, dh2b, d_g2 = _mm_nt_rmsbwd(dgu2, w2_in, h2, dh3, g2 + tok[0, 0], tk=SHARD_TILE, name="ffn2_dx")
    tok = ex.start_reduce("ffn2_out", after=dh2b)
    pair = _ffn_dw_in_half(n3t, dgu2, core, ex.from_sibling("ffn2_in", after=tok)[0], w2_in.shape,
                           name="ffn2_dw_in_own")
    tok = ex.start_exchange("ffn2_in", ["ffn2_w_in"], [pair])
    S = h0.shape[0]
    dcat = _mm_nt(dh2b, w_omix, tok, tm=512, tn=1024, name="mix_out_dx")
    dw_omix = _mm_tn(
        cat, dh2b, grid=(4,),
        a_spec=pl.BlockSpec((S, 512), lambda j: (0, j)), b_spec=pl.BlockSpec((S, D_MODEL), lambda j: (0, 0)),
        out_spec=pl.BlockSpec((512, D_MODEL), lambda j: (j, 0)),
        out_shape=jax.ShapeDtypeStruct((D_MODEL, D_MODEL), BF16), scale=1.0, name="mix_out_dw")
    dp, dw_pool, d_pscale = _pool_bwd(u, dcat, wp, pool_scale, name="pool_bwd")
    du, d_wa, d_ba, d_gn = _gla_bwd(u, o_gla, states, dcat, dp, wa, b_alpha, gla_norm, name="gla_bwd")
    du = du[None]
    tn_mix = COL_TILE
    dw_mix = _mm_tn(
        n_mixt, du, grid=(2, D_IN_PAD // tn_mix), a_is_transposed=True,
        a_spec=pl.BlockSpec((D_MODEL // 2, S), lambda i, j: (i, 0)),
        b_spec=pl.BlockSpec((None, S, tn_mix), lambda i, j: (0, 0, j)),
        out_spec=pl.BlockSpec((D_MODEL // 2, tn_mix), lambda i, j: (i, j)),
        out_shape=jax.ShapeDtypeStruct((D_MODEL, D_IN_PAD), BF16), scale=1.0, name="mix_in_dw")
    dw_mix_s = _split_shards(dw_mix, cs_mix, name="dw_mix_split")
    dw_pool_s = dw_pool.reshape(4, N_CHIPS, POOL_GROUP_DIM // N_CHIPS, POOL_GROUP_DIM).transpose(1, 0, 2, 3)
    dw_pool_s = dw_pool_s.reshape(N_CHIPS, POOL_GROUP_DIM, POOL_GROUP_DIM).astype(BF16)
    tok = ex.begin_reduce("mix", {"w_in_mix": dw_mix_s,
                                  "w_out_mix": dw_omix.reshape(N_CHIPS, D_MODEL // N_CHIPS, D_MODEL),
                                  "w_pool": dw_pool_s})
    dh1, dh1b, d_gm = _mm_nt_rmsbwd(du, w_mix, h1, dh2, gm + tok[0, 0], tk=tn_mix, name="mix_in_dx")
    tok = ex.start_reduce("mix", after=dh1b)
    tok = tok + ex.begin_reduce("ffn1_out", {"ffn1_w_out": _ffn_dw_out(dh1b, a1, "ffn1")})
    dgu1 = _ffn_bwd_act(dh1b, w1_out, fa1, tok, name="ffn1_bwd_act")
    theirs = _ffn_dw_in_half(n1t, dgu1, 1 - core, None, w1_in.shape, name="ffn1_dw_in_other")
    tok = ex.send_to_sibling("ffn1_in", [theirs])
    tok = ex.start_reduce("ffn1_out", after=tok)
    pair = _ffn_dw_in_half(n1t, dgu1, core, ex.from_sibling("ffn1_in", after=tok)[0], w1_in.shape,
                           name="ffn1_dw_in_own")
    tok = ex.start_exchange("ffn1_in", ["ffn1_w_in"], [pair])
    dx, _, d_g1 = _mm_nt_rmsbwd(dgu1, w1_in, h0, dh1, g1 + tok[0, 0], tk=SHARD_TILE, name="ffn1_dx")
    small = [d_g1, d_gm, d_g2, d_gf, d_pscale, d_ba, d_gn, d_wa[:GLA_GATE_RANK]]
    return loss_blk, dx, small
```
